```python
import math
import jax, jax.numpy as jnp
from jax import lax
import numpy as np

D_MODEL = 2048
BATCH = 4
SEQ = 2048
DEPTH = 1
DEC_BATCH = 128
DEC_SEQ = 4
PAST_LEN = 16384
PAGE_SIZE = 128

RET_HEADS = 8
RET_DK = 128
RET_DV = 256
RET_CHUNK = 128
ROPE_BASE = 10000.0
RET_QK = RET_HEADS * RET_DK
RET_V = RET_HEADS * RET_DV
SG_GROUPS = 4
SG_WIDTH = 2048
SG_CHUNK = 128
MEM_LEN = 256
XA_HEADS = 4
XA_DH = 128
XA_W = XA_HEADS * XA_DH
MOE_GROUPS = 4
MOE_PER_GROUP = 8
MOE_EXPERTS = MOE_GROUPS * MOE_PER_GROUP
MOE_TOPK = 2
MOE_FF = 512
EPS = 1e-6

Q_END = RET_QK
K_END = Q_END + RET_QK
V_END = K_END + RET_V
GSW_END = V_END + RET_V
SG_END = GSW_END + 2 * SG_WIDTH
GA_END = SG_END + D_MODEL
IN_WIDTH = GA_END + D_MODEL
SPLITS = [Q_END, K_END, V_END, GSW_END, SG_END, GA_END]

kernel_name = "retention_sgmlp_hmoe_hybrid_step"


def rms_norm(x, g):
    xf = x.astype(jnp.float32)
    y = xf * lax.rsqrt(jnp.mean(xf * xf, -1, keepdims=True) + EPS)
    return (y * g.astype(jnp.float32)).astype(x.dtype)


def layer_norm(x, g, b):
    xf = x.astype(jnp.float32)
    mu = jnp.mean(xf, -1, keepdims=True)
    var = jnp.mean(jnp.square(xf - mu), -1, keepdims=True)
    y = (xf - mu) * lax.rsqrt(var + EPS)
    return (y * g.astype(jnp.float32) + b.astype(jnp.float32)).astype(x.dtype)


def head_norm(o, g):
    B, T, H, DV = o.shape
    of = o.astype(jnp.float32)
    mu = jnp.mean(of, -1, keepdims=True)
    var = jnp.mean(jnp.square(of - mu), -1, keepdims=True)
    y = ((of - mu) * lax.rsqrt(var + EPS)).reshape(B, T, H * DV)
    return (y * g.astype(jnp.float32)).astype(o.dtype)


def rope(x, pos):
    half = x.shape[-1] // 2
    inv = ROPE_BASE ** (-jnp.arange(half, dtype=jnp.float32) / half)
    ang = pos.astype(jnp.float32)[:, None] * inv[None, :]
    cos = jnp.cos(ang)[None, :, None, :]
    sin = jnp.sin(ang)[None, :, None, :]
    x1 = x[..., :half].astype(jnp.float32)
    x2 = x[..., half:].astype(jnp.float32)
    return jnp.concatenate([x1 * cos - x2 * sin, x1 * sin + x2 * cos], -1).astype(x.dtype)


def retention(q, k, v, S0):
    B, T, H, DK = q.shape
    DV = v.shape[-1]
    dt = q.dtype
    C = RET_CHUNK if T % RET_CHUNK == 0 else T
    n = T // C
    lg = jnp.log1p(-jnp.power(2.0, -5.0 - jnp.arange(H, dtype=jnp.float32)))
    idx = jnp.arange(C, dtype=jnp.float32)
    rel = idx[:, None] - idx[None, :]
    dmask = jnp.where(rel >= 0, jnp.exp(lg[:, None, None] * jnp.maximum(rel, 0.0)), 0.0).astype(dt)
    xi = jnp.exp(lg[:, None] * (idx[None, :] + 1.0)).astype(dt)
    zeta = jnp.exp(lg[:, None] * (C - 1.0 - idx[None, :])).astype(dt)
    gC = jnp.exp(lg * C).astype(dt)

    def to_chunks(a):
        return a.reshape(B, n, C, H, a.shape[-1]).transpose(1, 0, 3, 2, 4)

    def step(S, inp):
        qc, kc, vc = inp
        inner = jnp.einsum('bhid,bhjd->bhij', qc, kc) * dmask[None]
        o = jnp.einsum('bhij,bhjv->bhiv', inner, vc)
        o = o + jnp.einsum('bhid,bhdv->bhiv', qc, S) * xi[None, :, :, None]
        S_new = gC[None, :, None, None] * S + jnp.einsum('bhjd,bhjv->bhdv', kc * zeta[None, :, :, None], vc)
        return S_new, o

    S_fin, oc = lax.scan(step, S0.astype(dt), (to_chunks(q), to_chunks(k), to_chunks(v)))
    o = oc.transpose(1, 0, 3, 2, 4).reshape(B, T, H, DV)
    return o, S_fin


def spatial_gating(u, v, sg_ws, sg_bs):
    B, T, W = v.shape
    C = min(SG_CHUNK, T)
    n = T // C
    mask = jnp.tril(jnp.ones((C, C), dtype=bool))
    w = jnp.where(mask[None], sg_ws[:, :C, :C], 0.0).astype(v.dtype)
    vg = v.reshape(B, n, C, SG_GROUPS, W // SG_GROUPS)
    mixed = jnp.einsum('gts,bnsgc->bntgc', w, vg) + sg_bs[:, :C].T.astype(v.dtype)[None, None, :, :, None]
    return u * mixed.reshape(B, T, W)


def mixer_block(h, pos, S0, w_in, ret_gn, sg_ln_g, sg_ln_b, sg_ws, sg_bs, w_a_out, w_b_out, w_o):
    B, T, _ = h.shape
    z = h @ w_in
    q, k, v, gsw, sg_uv, ga, gb = jnp.split(z, SPLITS, axis=-1)
    q = rope(q.reshape(B, T, RET_HEADS, RET_DK), pos)
    k = rope(k.reshape(B, T, RET_HEADS, RET_DK), pos) * (RET_DK ** -0.5)
    o, S_new = retention(q, k, v.reshape(B, T, RET_HEADS, RET_DV), S0)
    a = (jax.nn.silu(gsw) * head_norm(o, ret_gn)) @ w_a_out
    sg = jax.nn.gelu(sg_uv)
    u, vv = jnp.split(sg, 2, axis=-1)
    vv = layer_norm(vv, sg_ln_g, sg_ln_b)
    b = spatial_gating(u, vv, sg_ws, sg_bs) @ w_b_out
    merged = jax.nn.sigmoid(ga) * a + jax.nn.sigmoid(gb) * b
    return merged @ w_o, S_new, vv


def mem_kv(mem, w_ck, w_cv):
    B, M, _ = mem.shape
    k = (mem @ w_ck).reshape(B, M, XA_HEADS, XA_DH)
    v = (mem @ w_cv).reshape(B, M, XA_HEADS, XA_DH)
    return k, v


def cross_attend(h, mk, mv, w_cq, w_co):
    B, T, _ = h.shape
    q = (h @ w_cq).reshape(B, T, XA_HEADS, XA_DH)
    s = jnp.einsum('bthd,bmhd->bhtm', q, mk).astype(jnp.float32) * (XA_DH ** -0.5)
    p = jax.nn.softmax(s, axis=-1).astype(mv.dtype)
    o = jnp.einsum('bhtm,bmhd->bthd', p, mv).reshape(B, T, XA_W)
    return o @ w_co


def hier_moe(h, w_rg, b_rg, w_re, b_re, w_e1, w_e3, w_e2):
    B, T, Dm = h.shape
    t = h.reshape(B * T, Dm)
    g_logits = (t @ w_rg).astype(jnp.float32) + b_rg.astype(jnp.float32)
    g_prob = jax.nn.softmax(g_logits, axis=-1)
    g_sel = jnp.argmax(g_logits, axis=-1)
    g_w = jnp.take_along_axis(g_prob, g_sel[:, None], axis=-1)
    e_logits = ((t @ w_re).astype(jnp.float32) + b_re.astype(jnp.float32)).reshape(-1, MOE_GROUPS, MOE_PER_GROUP)
    e_in = jnp.take_along_axis(e_logits, g_sel[:, None, None], axis=1)[:, 0]
    top_v, top_i = lax.top_k(e_in, MOE_TOPK)
    top_w = jax.nn.softmax(top_v, axis=-1) * g_w
    expert_id = g_sel[:, None] * MOE_PER_GROUP + top_i
    combine = jnp.sum(jax.nn.one_hot(expert_id, MOE_EXPERTS, dtype=jnp.float32) * top_w[..., None], axis=1)
    combine = combine.astype(t.dtype)
    y = jnp.zeros_like(t)
    for e in range(MOE_EXPERTS):
        he = jax.nn.silu(t @ w_e1[e]) * (t @ w_e3[e])
        y = y + (he @ w_e2[e]) * combine[:, e:e + 1]
    return y.reshape(B, T, Dm)


def decoder_layer(x, pos, S0, mk, mv, norm_mix, w_in, ret_gn, sg_ln_g, sg_ln_b, sg_ws, sg_bs,
                  w_a_out, w_b_out, w_o, norm_xa, w_cq, w_co, norm_moe, w_rg, b_rg, w_re, b_re,
                  w_e1, w_e3, w_e2):
    m, S_new, sg_v = mixer_block(rms_norm(x, norm_mix), pos, S0, w_in, ret_gn, sg_ln_g, sg_ln_b,
                                 sg_ws, sg_bs, w_a_out, w_b_out, w_o)
    x = x + m
    x = x + cross_attend(rms_norm(x, norm_xa), mk, mv, w_cq, w_co)
    x = x + hier_moe(rms_norm(x, norm_moe), w_rg, b_rg, w_re, b_re, w_e1, w_e3, w_e2)
    return x, S_new, sg_v


def setup_inputs(seed: int = 0) -> dict:
    key = jax.random.key(seed)
    ks = jax.random.split(key, 40)
    f32 = jnp.float32

    def nrm(k, shape, scale):
        return jax.random.normal(k, shape, f32) * scale

    def gain(k, shape):
        return 1.0 + 0.05 * jax.random.normal(k, shape, f32)

    L = DEPTH
    return {
        "x_prompt": nrm(ks[0], (BATCH, SEQ, D_MODEL), 1.0),
        "x_sample": nrm(ks[1], (DEC_BATCH, DEC_SEQ, D_MODEL), 1.0),
        "mem_prompt": nrm(ks[2], (BATCH, MEM_LEN, D_MODEL), 1.0),
        "state_ret": nrm(ks[3], (L, DEC_BATCH, RET_HEADS, RET_DK, RET_DV), 0.5),
        "cache_mem_k": nrm(ks[4], (L, DEC_BATCH, MEM_LEN, XA_HEADS, XA_DH), 1.0),
        "cache_mem_v": nrm(ks[5], (L, DEC_BATCH, MEM_LEN, XA_HEADS, XA_DH), 1.0),
        "norm_mix": gain(ks[6], (L, D_MODEL)),
        "w_in": nrm(ks[7], (L, D_MODEL, IN_WIDTH), D_MODEL ** -0.5),
        "ret_gn": gain(ks[8], (L, RET_V)),
        "sg_ln_g": gain(ks[9], (L, SG_WIDTH)),
        "sg_ln_b": nrm(ks[10], (L, SG_WIDTH), 0.02),
        "sg_ws": nrm(ks[11], (L, SG_GROUPS, SG_CHUNK, SG_CHUNK), SG_CHUNK ** -0.5),
        "sg_bs": gain(ks[12], (L, SG_GROUPS, SG_CHUNK)),
        "w_a_out": nrm(ks[13], (L, RET_V, D_MODEL), RET_V ** -0.5),
        "w_b_out": nrm(ks[14], (L, SG_WIDTH, D_MODEL), SG_WIDTH ** -0.5),
        "w_o": nrm(ks[15], (L, D_MODEL, D_MODEL), D_MODEL ** -0.5),
        "norm_xa": gain(ks[16], (L, D_MODEL)),
        "w_cq": nrm(ks[17], (L, D_MODEL, XA_W), D_MODEL ** -0.5),
        "w_ck": nrm(ks[18], (L, D_MODEL, XA_W), D_MODEL ** -0.5),
        "w_cv": nrm(ks[19], (L, D_MODEL, XA_W), D_MODEL ** -0.5),
        "w_co": nrm(ks[20], (L, XA_W, D_MODEL), XA_W ** -0.5),
        "norm_moe": gain(ks[21], (L, D_MODEL)),
        "w_rg": nrm(ks[22], (L, D_MODEL, MOE_GROUPS), D_MODEL ** -0.5),
        "b_rg": nrm(ks[23], (L, MOE_GROUPS), 0.01),
        "w_re": nrm(ks[24], (L, D_MODEL, MOE_EXPERTS), D_MODEL ** -0.5),
        "b_re": nrm(ks[25], (L, MOE_EXPERTS), 0.01),
        "w_e1": nrm(ks[26], (L, MOE_EXPERTS, D_MODEL, MOE_FF), D_MODEL ** -0.5),
        "w_e3": nrm(ks[27], (L, MOE_EXPERTS, D_MODEL, MOE_FF), D_MODEL ** -0.5),
        "w_e2": nrm(ks[28], (L, MOE_EXPERTS, MOE_FF, D_MODEL), MOE_FF ** -0.5),
        "norm_f": gain(ks[29], (D_MODEL,)),
    }


def reference(x_prompt, x_sample, mem_prompt, state_ret, cache_mem_k, cache_mem_v,
              norm_mix, w_in, ret_gn, sg_ln_g, sg_ln_b, sg_ws, sg_bs, w_a_out, w_b_out, w_o,
              norm_xa, w_cq, w_ck, w_cv, w_co, norm_moe, w_rg, b_rg, w_re, b_re,
              w_e1, w_e3, w_e2, norm_f):
    pos_p = jnp.arange(SEQ, dtype=jnp.int32)
    pos_s = PAST_LEN + jnp.arange(DEC_SEQ, dtype=jnp.int32)
    xp = x_prompt
    xs = x_sample
    ret_p_list, mk_p_list, mv_p_list, ret_s_list, sgv_s_list = [], [], [], [], []
    for l in range(DEPTH):
        lw = (norm_mix[l], w_in[l], ret_gn[l], sg_ln_g[l], sg_ln_b[l], sg_ws[l], sg_bs[l],
              w_a_out[l], w_b_out[l], w_o[l], norm_xa[l], w_cq[l], w_co[l], norm_moe[l],
              w_rg[l], b_rg[l], w_re[l], b_re[l], w_e1[l], w_e3[l], w_e2[l])
        mk_p, mv_p = mem_kv(mem_prompt, w_ck[l], w_cv[l])
        S0_p = jnp.zeros((xp.shape[0], RET_HEADS, RET_DK, RET_DV), xp.dtype)
        xp, S_p, _ = decoder_layer(xp, pos_p, S0_p, mk_p, mv_p, *lw)
        xs, S_s, sgv_s = decoder_layer(xs, pos_s, state_ret[l], cache_mem_k[l], cache_mem_v[l], *lw)
        ret_p_list.append(S_p)
        mk_p_list.append(mk_p)
        mv_p_list.append(mv_p)
        ret_s_list.append(S_s)
        sgv_s_list.append(sgv_s)
    y_prompt = rms_norm(xp, norm_f)
    y_sample = rms_norm(xs, norm_f)
    ret_prompt = jnp.stack(ret_p_list)
    mem_k_prompt = jnp.stack(mk_p_list)
    mem_v_prompt = jnp.stack(mv_p_list)
    ret_sample = jnp.stack(ret_s_list)
    sg_v_sample = jnp.stack(sgv_s_list)
    return (y_prompt, y_sample, ret_prompt, mem_k_prompt, mem_v_prompt, ret_sample, sg_v_sample)
```

```python
import functools

import jax
import jax.numpy as jnp
from jax import lax
from jax.experimental import pallas as pl
from jax.experimental.pallas import tpu as pltpu

F32 = jnp.float32
BF16 = jnp.bfloat16

D_MODEL = 2048
BATCH = 4
SEQ = 2048
DEC_BATCH = 128
DEC_SEQ = 4
PAST_LEN = 16384
RET_HEADS = 8
RET_DK = 128
RET_DV = 256
RET_CHUNK = 128
ROPE_BASE = 10000.0
RET_QK = RET_HEADS * RET_DK
RET_V = RET_HEADS * RET_DV
SG_GROUPS = 4
SG_WIDTH = 2048
SG_CHUNK = 128
MEM_LEN = 256
XA_HEADS = 4
XA_DH = 128
XA_W = XA_HEADS * XA_DH
MOE_GROUPS = 4
MOE_PER_GROUP = 8
MOE_EXPERTS = MOE_GROUPS * MOE_PER_GROUP
MOE_TOPK = 2
MOE_FF = 512
EPS = 1e-6
IN_WIDTH = 2 * RET_QK + 2 * RET_V + 2 * SG_WIDTH + 2 * D_MODEL

N_P = BATCH * SEQ
N_S = DEC_BATCH * DEC_SEQ
N_ALL = N_P + N_S
ROW_TILE = 512
N_TILES = N_ALL // ROW_TILE
NP_TILES = N_P // ROW_TILE
MERGE_TILE = 256
MERGE_P_TILES = N_P // MERGE_TILE

IN_TM = 1024
IN_TN = 1024
IN_MT = N_P // IN_TM + 1
IN_NT = IN_WIDTH // IN_TN
ZB_V, ZB_GSW, ZB_U, ZB_VV, ZB_GA, ZB_GB = 1, 2, 3, 4, 5, 6

LOG_DEC_SEQ = 2
LOG_SG_CHUNK = 7
SAMPLE_BB = 8
SAMPLE_ROWS = SAMPLE_BB * DEC_SEQ
SAMPLE_STEPS = DEC_BATCH // SAMPLE_BB

MOE_TM = 256
MOE_NT = (N_ALL * MOE_TOPK + MOE_EXPERTS * (MOE_TM - 1) + MOE_TM - 1) // MOE_TM
MOE_ROWS = MOE_NT * MOE_TM
ROUTE_LANES = 128

VMEM_LIMIT = 56 * 1024 * 1024


def _params(n_axes, vmem=VMEM_LIMIT):
    return pltpu.CompilerParams(dimension_semantics=("arbitrary",) * n_axes,
                                vmem_limit_bytes=vmem)


def _rms(x, g):
    ms = jnp.mean(x * x, axis=-1, keepdims=True)
    return (x * lax.rsqrt(ms + EPS)) * g


def _dot(a, b):
    return jnp.dot(a, b, preferred_element_type=F32)


def _dot_nt(a, b):
    return lax.dot_general(a, b, (((1,), (1,)), ((), ())), preferred_element_type=F32)


def _dot_tn(a, b):
    return lax.dot_general(a, b, (((0,), (0,)), ((), ())), preferred_element_type=F32)


def _norm_kernel(xp_ref, xs_ref, g_ref, h_ref):
    i = pl.program_id(0)

    @pl.when(i < NP_TILES)
    def _():
        h_ref[...] = _rms(xp_ref[...], g_ref[...]).astype(BF16)

    @pl.when(i == NP_TILES)
    def _():
        h_ref[...] = _rms(xs_ref[...], g_ref[...]).astype(BF16)


def _norm_rows(xp, xs, g):
    return pl.pallas_call(
        _norm_kernel,
        grid=(N_TILES,),
        in_specs=[
            pl.BlockSpec((ROW_TILE, D_MODEL), lambda i: (jnp.minimum(i, NP_TILES - 1), 0)),
            pl.BlockSpec((ROW_TILE, D_MODEL), lambda i: (0, 0)),
            pl.BlockSpec((1, D_MODEL), lambda i: (0, 0)),
        ],
        out_specs=pl.BlockSpec((ROW_TILE, D_MODEL), lambda i: (i, 0)),
        out_shape=jax.ShapeDtypeStruct((N_ALL, D_MODEL), BF16),
        compiler_params=_params(1),
        name="norm_rows",
    )(xp, xs, g)


def _inproj_kernel(h_ref, w_ref, cos_ref, sin_ref, z_ref, wb_ref, acc_ref):
    j = pl.program_id(0)
    i = pl.program_id(1)

    @pl.when(i == 0)
    def _():
        wb_ref[...] = w_ref[...].astype(BF16)

    def epilogue(rows):
        @pl.when(j < 2)
        def _():
            scale = jnp.where(j == 1, RET_DK ** -0.5, 1.0).astype(F32)
            c = cos_ref[0:rows, :]
            s = sin_ref[0:rows, :]
            for hb in range(IN_TN // RET_DK):
                cols = slice(hb * RET_DK, (hb + 1) * RET_DK)
                a = acc_ref[0:rows, cols]
                r = pltpu.roll(a, RET_DK // 2, axis=1)
                z_ref[0:rows, cols] = ((a * c + r * s) * scale).astype(BF16)

        @pl.when((j >= 2) & (j < 4))
        def _():
            z_ref[0:rows, :] = acc_ref[0:rows, :].astype(BF16)

        @pl.when((j >= 4) & (j < 6))
        def _():
            z_ref[0:rows, :] = jax.nn.silu(acc_ref[0:rows, :]).astype(BF16)

        @pl.when((j >= 6) & (j < 10))
        def _():
            z_ref[0:rows, :] = jax.nn.gelu(acc_ref[0:rows, :]).astype(BF16)

        @pl.when(j >= 10)
        def _():
            z_ref[0:rows, :] = jax.nn.sigmoid(acc_ref[0:rows, :]).astype(BF16)

    @pl.when(i < IN_MT - 1)
    def _():
        acc_ref[...] = _dot(h_ref[...], wb_ref[...])
        epilogue(IN_TM)

    @pl.when(i == IN_MT - 1)
    def _():
        acc_ref[0:N_S, :] = _dot(h_ref[0:N_S, :], wb_ref[...])
        epilogue(N_S)


def _inproj(h, w_in, cos_t, sin_t):
    tab_idx = lambda j, i: (jnp.where(i < IN_MT - 1, i % (SEQ // IN_TM), SEQ // IN_TM), 0)
    return pl.pallas_call(
        _inproj_kernel,
        grid=(IN_NT, IN_MT),
        in_specs=[
            pl.BlockSpec((IN_TM, D_MODEL), lambda j, i: (i, 0)),
            pl.BlockSpec((D_MODEL, IN_TN), lambda j, i: (0, j)),
            pl.BlockSpec((IN_TM, RET_DK), tab_idx),
            pl.BlockSpec((IN_TM, RET_DK), tab_idx),
        ],
        out_specs=pl.BlockSpec((IN_TM, IN_TN), lambda j, i: (i, j)),
        out_shape=jax.ShapeDtypeStruct((N_ALL, IN_WIDTH), BF16),
        scratch_shapes=[pltpu.VMEM((D_MODEL, IN_TN), BF16), pltpu.VMEM((IN_TM, IN_TN), F32)],
        compiler_params=_params(2),
        name="in_proj",
    )(h, w_in, cos_t, sin_t)


def _rope_tables():
    half = RET_DK // 2
    inv = ROPE_BASE ** (-jnp.arange(half, dtype=F32) / half)

    def tab(pos):
        ang = pos.astype(F32)[:, None] * inv[None, :]
        c, s = jnp.cos(ang), jnp.sin(ang)
        return jnp.concatenate([c, c], -1), jnp.concatenate([-s, s], -1)

    cp, sp = tab(jnp.arange(SEQ, dtype=jnp.int32))
    cs, ss = tab(PAST_LEN + jnp.arange(DEC_SEQ, dtype=jnp.int32))
    cs = jnp.tile(cs, (DEC_BATCH, 1))
    ss = jnp.tile(ss, (DEC_BATCH, 1))
    pad = jnp.zeros((IN_TM - N_S, RET_DK), F32)
    return jnp.concatenate([cp, cs, pad], 0), jnp.concatenate([sp, ss, pad], 0)


def _decay_tables(chunk):
    lg = jnp.log1p(-jnp.power(2.0, -5.0 - jnp.arange(RET_HEADS, dtype=F32)))
    idx = jnp.arange(chunk, dtype=F32)
    rel = idx[:, None] - idx[None, :]
    dmask = jnp.where(rel >= 0, jnp.exp(lg[:, None, None] * jnp.maximum(rel, 0.0)), 0.0).astype(F32)
    xi = jnp.exp(lg[:, None] * (idx[None, :] + 1.0)).astype(F32)
    zeta = jnp.exp(lg[:, None] * (chunk - 1.0 - idx[None, :])).astype(F32)
    gc = jnp.exp(lg * chunk).astype(F32)
    return dmask, xi, zeta, gc


def _head_norm_gate(o, gn, gate):
    mu = jnp.mean(o, axis=-1, keepdims=True)
    d = o - mu
    var = jnp.mean(d * d, axis=-1, keepdims=True)
    y = (d * lax.rsqrt(var + EPS)) * gn
    return (gate.astype(F32) * y).astype(BF16)


def _ret_prompt_kernel(gc_ref, q_ref, k_ref, v_ref, gs_ref, gn_ref, dm_ref, xi_ref, zt_ref,
                       a_ref, sfin_ref, s_ref):
    c = pl.program_id(1)

    @pl.when(c == 0)
    def _():
        s_ref[...] = jnp.zeros_like(s_ref)

    for h in range(RET_HEADS):
        kc = slice(h * RET_DK, (h + 1) * RET_DK)
        vc = slice(h * RET_DV, (h + 1) * RET_DV)
        qh = q_ref[:, kc]
        kh = k_ref[:, kc]
        vh = v_ref[:, vc]
        inner = _dot_nt(qh, kh) * dm_ref[h]
        o = _dot(inner.astype(BF16), vh)
        s_old = s_ref[h]
        xi = xi_ref[h]
        o = o + _dot(qh, s_old.astype(BF16)) * jnp.concatenate([xi, xi], axis=1)
        kz = (kh.astype(F32) * zt_ref[h]).astype(BF16)
        s_ref[h] = gc_ref[h] * s_old + _dot_tn(kz, vh)
        a_ref[:, vc] = _head_norm_gate(o, gn_ref[:, vc], gs_ref[:, vc])

    @pl.when(c == pl.num_programs(1) - 1)
    def _():
        sfin_ref[0, 0] = s_ref[...]


def _ret_prompt(z, ret_gn, chunk):
    dmask, xi, zeta, gc = _decay_tables(chunk)
    xi_b = jnp.broadcast_to(xi[:, :, None], (RET_HEADS, chunk, RET_DK))
    zeta_b = jnp.broadcast_to(zeta[:, :, None], (RET_HEADS, chunk, RET_DK))
    n_chunks = SEQ // chunk
    row = lambda b, c: b * n_chunks + c
    const3 = lambda b, c: (0, 0, 0)
    return pl.pallas_call(
        _ret_prompt_kernel,
        grid=(BATCH, n_chunks),
        in_specs=[
            pl.BlockSpec(memory_space=pltpu.SMEM),
            pl.BlockSpec((chunk, RET_QK), lambda b, c: (row(b, c), 0)),
            pl.BlockSpec((chunk, RET_QK), lambda b, c: (row(b, c), 1)),
            pl.BlockSpec((chunk, RET_V), lambda b, c: (row(b, c), ZB_V)),
            pl.BlockSpec((chunk, RET_V), lambda b, c: (row(b, c), ZB_GSW)),
            pl.BlockSpec((1, RET_V), lambda b, c: (0, 0)),
            pl.BlockSpec((RET_HEADS, chunk, chunk), const3),
            pl.BlockSpec((RET_HEADS, chunk, RET_DK), const3),
            pl.BlockSpec((RET_HEADS, chunk, RET_DK), const3),
        ],
        out_specs=[
            pl.BlockSpec((chunk, RET_V), lambda b, c: (row(b, c), 0)),
            pl.BlockSpec((1, 1, RET_HEADS, RET_DK, RET_DV), lambda b, c: (0, b, 0, 0, 0)),
        ],
        out_shape=[
            jax.ShapeDtypeStruct((N_P, RET_V), BF16),
            jax.ShapeDtypeStruct((1, BATCH, RET_HEADS, RET_DK, RET_DV), F32),
        ],
        scratch_shapes=[pltpu.VMEM((RET_HEADS, RET_DK, RET_DV), F32)],
        compiler_params=_params(2),
        name="ret_prompt",
    )(gc, z, z, z, z, ret_gn, dmask, xi_b, zeta_b)


def _ret_sample_kernel(gc_ref, q_ref, k_ref, v_ref, gs_ref, gn_ref, dm_ref, xi_ref, zt_ref, s0_ref,
                       a_ref, s1_ref, o_ref):
    rows_k = lax.broadcasted_iota(jnp.int32, (SAMPLE_ROWS, RET_DK), 0) >> LOG_DEC_SEQ
    rows_v = lax.broadcasted_iota(jnp.int32, (SAMPLE_ROWS, RET_DV), 0) >> LOG_DEC_SEQ
    for h in range(RET_HEADS):
        kc = slice(h * RET_DK, (h + 1) * RET_DK)
        vc = slice(h * RET_DV, (h + 1) * RET_DV)
        qh = q_ref[:, kc]
        kh = k_ref[:, kc]
        vh = v_ref[:, vc]
        inner = _dot_nt(qh, kh) * dm_ref[h]
        o_ref[:, vc] = _dot(inner.astype(BF16), vh)
        xi = xi_ref[h]
        xi2 = jnp.concatenate([xi, xi], axis=1)
        kz = kh.astype(F32) * zt_ref[h]
        gch = gc_ref[h]

        def body(b, carry):
            s_old = s0_ref[0, b, h]
            cross = _dot(qh, s_old.astype(BF16)) * xi2
            o_ref[:, vc] += jnp.where(rows_v == b, cross, 0.0)
            kz_b = jnp.where(rows_k == b, kz, 0.0).astype(BF16)
            s1_ref[0, b, h] = gch * s_old + _dot_tn(kz_b, vh)
            return carry

        lax.fori_loop(0, SAMPLE_BB, body, 0)
        a_ref[:, vc] = _head_norm_gate(o_ref[:, vc], gn_ref[:, vc], gs_ref[:, vc])


def _ret_sample(z, ret_gn, state):
    dmask, xi, zeta, gc = _decay_tables(DEC_SEQ)
    eye = jnp.eye(SAMPLE_BB, dtype=F32)
    dm_big = jax.vmap(lambda m: jnp.kron(eye, m))(dmask)
    xi_b = jnp.broadcast_to(jnp.tile(xi, (1, SAMPLE_BB))[:, :, None], (RET_HEADS, SAMPLE_ROWS, RET_DK))
    zeta_b = jnp.broadcast_to(jnp.tile(zeta, (1, SAMPLE_BB))[:, :, None], (RET_HEADS, SAMPLE_ROWS, RET_DK))
    base = N_P // SAMPLE_ROWS
    const3 = lambda g: (0, 0, 0)
    st_spec = pl.BlockSpec((1, SAMPLE_BB, RET_HEADS, RET_DK, RET_DV), lambda g: (0, g, 0, 0, 0))
    return pl.pallas_call(
        _ret_sample_kernel,
        grid=(SAMPLE_STEPS,),
        in_specs=[
            pl.BlockSpec(memory_space=pltpu.SMEM),
            pl.BlockSpec((SAMPLE_ROWS, RET_QK), lambda g: (base + g, 0)),
            pl.BlockSpec((SAMPLE_ROWS, RET_QK), lambda g: (base + g, 1)),
            pl.BlockSpec((SAMPLE_ROWS, RET_V), lambda g: (base + g, ZB_V)),
            pl.BlockSpec((SAMPLE_ROWS, RET_V), lambda g: (base + g, ZB_GSW)),
            pl.BlockSpec((1, RET_V), lambda g: (0, 0)),
            pl.BlockSpec((RET_HEADS, SAMPLE_ROWS, SAMPLE_ROWS), const3),
            pl.BlockSpec((RET_HEADS, SAMPLE_ROWS, RET_DK), const3),
            pl.BlockSpec((RET_HEADS, SAMPLE_ROWS, RET_DK), const3),
            st_spec,
        ],
        out_specs=[
            pl.BlockSpec((SAMPLE_ROWS, RET_V), lambda g: (g, 0)),
            st_spec,
        ],
        out_shape=[
            jax.ShapeDtypeStruct((N_S, RET_V), BF16),
            jax.ShapeDtypeStruct((1, DEC_BATCH, RET_HEADS, RET_DK, RET_DV), F32),
        ],
        scratch_shapes=[pltpu.VMEM((SAMPLE_ROWS, RET_V), F32)],
        compiler_params=_params(1),
        name="ret_sample",
    )(gc, z, z, z, z, ret_gn, dm_big, xi_b, zeta_b, state)


def _sgate_kernel(u_ref, v_ref, lg_ref, lb_ref, w_ref, bias_ref, b_ref, sgv_ref):
    i = pl.program_id(0)
    is_sample = i == NP_TILES
    v = v_ref[...].astype(F32)
    mu = jnp.mean(v, axis=-1, keepdims=True)
    d = v - mu
    var = jnp.mean(d * d, axis=-1, keepdims=True)
    vln = (d * lax.rsqrt(var + EPS)) * lg_ref[...] + lb_ref[...]

    @pl.when(is_sample)
    def _():
        sgv_ref[...] = vln

    r = lax.broadcasted_iota(jnp.int32, (ROW_TILE, ROW_TILE), 0)
    c = lax.broadcasted_iota(jnp.int32, (ROW_TILE, ROW_TILE), 1)
    r_chunk = jnp.where(is_sample, r >> LOG_DEC_SEQ, r >> LOG_SG_CHUNK)
    c_chunk = jnp.where(is_sample, c >> LOG_DEC_SEQ, c >> LOG_SG_CHUNK)
    keep = (r_chunk == c_chunk) & (c <= r)
    gw = SG_WIDTH // SG_GROUPS
    for g in range(SG_GROUPS):
        cols = slice(g * gw, (g + 1) * gw)
        w = jnp.where(keep, w_ref[0, g], 0.0).astype(BF16)
        bias = bias_ref[0, g]
        mixed = _dot(w, vln[:, cols].astype(BF16)) + jnp.concatenate([bias] * (gw // 128), axis=1)
        b_ref[:, cols] = (u_ref[:, cols].astype(F32) * mixed).astype(BF16)


def _sgate(z, ln_g, ln_b, sg_ws, sg_bs):
    reps = ROW_TILE // SG_CHUNK
    w_p = jnp.tile(sg_ws, (1, reps, reps))
    w_s = jnp.tile(sg_ws[:, :DEC_SEQ, :DEC_SEQ], (1, ROW_TILE // DEC_SEQ, ROW_TILE // DEC_SEQ))
    w_t = jnp.stack([w_p, w_s])
    b_p = jnp.tile(sg_bs, (1, reps))
    b_s = jnp.tile(sg_bs[:, :DEC_SEQ], (1, ROW_TILE // DEC_SEQ))
    b_t = jnp.broadcast_to(jnp.stack([b_p, b_s])[..., None], (2, SG_GROUPS, ROW_TILE, 128))
    kind = lambda i: jnp.where(i == NP_TILES, 1, 0)
    return pl.pallas_call(
        _sgate_kernel,
        grid=(N_TILES,),
        in_specs=[
            pl.BlockSpec((ROW_TILE, SG_WIDTH), lambda i: (i, ZB_U)),
            pl.BlockSpec((ROW_TILE, SG_WIDTH), lambda i: (i, ZB_VV)),
            pl.BlockSpec((1, SG_WIDTH), lambda i: (0, 0)),
            pl.BlockSpec((1, SG_WIDTH), lambda i: (0, 0)),
            pl.BlockSpec((1, SG_GROUPS, ROW_TILE, ROW_TILE), lambda i: (kind(i), 0, 0, 0)),
            pl.BlockSpec((1, SG_GROUPS, ROW_TILE, 128), lambda i: (kind(i), 0, 0, 0)),
        ],
        out_specs=[
            pl.BlockSpec((ROW_TILE, SG_WIDTH), lambda i: (i, 0)),
            pl.BlockSpec((N_S, SG_WIDTH), lambda i: (0, 0)),
        ],
        out_shape=[
            jax.ShapeDtypeStruct((N_ALL, SG_WIDTH), BF16),
            jax.ShapeDtypeStruct((N_S, SG_WIDTH), F32),
        ],
        compiler_params=_params(1),
        name="spatial_gate",
    )(z, z, ln_g, ln_b, w_t, b_t)


def _merge_kernel(ap_ref, as_ref, b_ref, ga_ref, gb_ref, xp_ref, xs_ref, wa_ref, wb_ref, wo_ref,
                  g_ref, wq_ref, x1_ref, q_ref):
    i = pl.program_id(0)

    def run(a_in, x):
        a = _dot(a_in, wa_ref[...])
        b = _dot(b_ref[...], wb_ref[...])
        merged = ga_ref[...].astype(F32) * a + gb_ref[...].astype(F32) * b
        x1 = x + _dot(merged.astype(BF16), wo_ref[...])
        x1_ref[...] = x1
        q_ref[...] = _dot(_rms(x1, g_ref[...]).astype(BF16), wq_ref[...]).astype(BF16)

    @pl.when(i < MERGE_P_TILES)
    def _():
        run(ap_ref[...], xp_ref[...])

    @pl.when(i >= MERGE_P_TILES)
    def _():
        run(as_ref[...], xs_ref[...])


def _resident(shape):
    return pl.BlockSpec(shape, lambda i: (0,) * len(shape), pipeline_mode=pl.Buffered(1))


def _merge(a_p, a_s, b_all, z, xp, xs, wa, wb, wo, g_xa, wq):
    tm = MERGE_TILE
    prompt_tile = lambda i: (jnp.minimum(i, MERGE_P_TILES - 1), 0)
    sample_tile = lambda i: (jnp.maximum(i - MERGE_P_TILES, 0), 0)
    return pl.pallas_call(
        _merge_kernel,
        grid=(N_ALL // tm,),
        in_specs=[
            pl.BlockSpec((tm, RET_V), prompt_tile),
            pl.BlockSpec((tm, RET_V), sample_tile),
            pl.BlockSpec((tm, SG_WIDTH), lambda i: (i, 0)),
            pl.BlockSpec((tm, D_MODEL), lambda i: (i, ZB_GA)),
            pl.BlockSpec((tm, D_MODEL), lambda i: (i, ZB_GB)),
            pl.BlockSpec((tm, D_MODEL), prompt_tile),
            pl.BlockSpec((tm, D_MODEL), sample_tile),
            _resident((RET_V, D_MODEL)),
            _resident((SG_WIDTH, D_MODEL)),
            _resident((D_MODEL, D_MODEL)),
            pl.BlockSpec((1, D_MODEL), lambda i: (0, 0)),
            _resident((D_MODEL, XA_W)),
        ],
        out_specs=[
            pl.BlockSpec((tm, D_MODEL), lambda i: (i, 0)),
            pl.BlockSpec((tm, XA_W), lambda i: (i, 0)),
        ],
        out_shape=[
            jax.ShapeDtypeStruct((N_ALL, D_MODEL), F32),
            jax.ShapeDtypeStruct((N_ALL, XA_W), BF16),
        ],
        compiler_params=_params(1),
        name="merge_proj",
    )(a_p, a_s, b_all, z, z, xp, xs, wa, wb, wo, g_xa, wq)


def _memkv_kernel(m_ref, wk_ref, wv_ref, k_ref, v_ref, kb_ref, vb_ref):
    m = m_ref[...].astype(BF16)
    k = _dot(m, wk_ref[...].astype(BF16))
    v = _dot(m, wv_ref[...].astype(BF16))
    k_ref[...] = k
    v_ref[...] = v
    kb_ref[...] = k.astype(BF16)
    vb_ref[...] = v.astype(BF16)


def _memkv(mem, w_ck, w_cv):
    rows = BATCH * MEM_LEN
    spec = pl.BlockSpec((MEM_LEN, XA_W), lambda b: (b, 0))
    wspec = pl.BlockSpec((D_MODEL, XA_W), lambda b: (0, 0))
    return pl.pallas_call(
        _memkv_kernel,
        grid=(BATCH,),
        in_specs=[pl.BlockSpec((MEM_LEN, D_MODEL), lambda b: (b, 0)), wspec, wspec],
        out_specs=[spec, spec, spec, spec],
        out_shape=[jax.ShapeDtypeStruct((rows, XA_W), F32)] * 2 + [jax.ShapeDtypeStruct((rows, XA_W), BF16)] * 2,
        compiler_params=_params(1),
        name="mem_kv",
    )(mem, w_ck, w_cv)


def _softmax_rows(s):
    m = jnp.max(s, axis=-1, keepdims=True)
    e = jnp.exp(s - m)
    return e / jnp.sum(e, axis=-1, keepdims=True)


def _xattn_prompt_kernel(q_ref, k_ref, v_ref, o_ref):
    for h in range(XA_HEADS):
        cols = slice(h * XA_DH, (h + 1) * XA_DH)
        s = _dot_nt(q_ref[:, cols], k_ref[:, cols]) * (XA_DH ** -0.5)
        p = _softmax_rows(s)
        o_ref[:, cols] = _dot(p.astype(BF16), v_ref[:, cols]).astype(BF16)


def _xattn_prompt(q, kb, vb):
    per_b = SEQ // ROW_TILE
    kv = pl.BlockSpec((MEM_LEN, XA_W), lambda i: (i // per_b, 0))
    return pl.pallas_call(
        _xattn_prompt_kernel,
        grid=(NP_TILES,),
        in_specs=[pl.BlockSpec((ROW_TILE, XA_W), lambda i: (i, 0)), kv, kv],
        out_specs=pl.BlockSpec((ROW_TILE, XA_W), lambda i: (i, 0)),
        out_shape=jax.ShapeDtypeStruct((N_P, XA_W), BF16),
        compiler_params=_params(1),
        name="xattn_prompt",
    )(q, kb, vb)


def _xattn_sample_kernel(q_ref, k_ref, v_ref, o_ref, acc_ref):
    n_q = XA_HEADS * SAMPLE_ROWS
    n_kv = MEM_LEN * XA_HEADS
    q4 = jnp.concatenate([q_ref[:, h * XA_DH:(h + 1) * XA_DH] for h in range(XA_HEADS)], axis=0)
    r = lax.broadcasted_iota(jnp.int32, (n_q, n_kv), 0)
    c = lax.broadcasted_iota(jnp.int32, (n_q, n_kv), 1)
    head_ok = (r >> 5) == (c & (XA_HEADS - 1))
    row_b = (lax.broadcasted_iota(jnp.int32, (n_q, XA_DH), 0) & (SAMPLE_ROWS - 1)) >> LOG_DEC_SEQ
    acc_ref[...] = jnp.zeros_like(acc_ref)

    def body(b, carry):
        kb = k_ref[b].astype(BF16)
        vb = v_ref[b].astype(BF16)
        s = _dot_nt(q4, kb) * (XA_DH ** -0.5)
        p = _softmax_rows(jnp.where(head_ok, s, -1e30))
        o = _dot(p.astype(BF16), vb)
        acc_ref[...] += jnp.where(row_b == b, o, 0.0)
        return carry

    lax.fori_loop(0, SAMPLE_BB, body, 0)
    for h in range(XA_HEADS):
        o_ref[h] = acc_ref[h * SAMPLE_ROWS:(h + 1) * SAMPLE_ROWS, :]


def _xattn_sample(q, ck, cv):
    base = N_P // SAMPLE_ROWS
    kv = pl.BlockSpec((SAMPLE_BB, MEM_LEN * XA_HEADS, XA_DH), lambda g: (g, 0, 0))
    return pl.pallas_call(
        _xattn_sample_kernel,
        grid=(SAMPLE_STEPS,),
        in_specs=[pl.BlockSpec((SAMPLE_ROWS, XA_W), lambda g: (base + g, 0)), kv, kv],
        out_specs=pl.BlockSpec((XA_HEADS, SAMPLE_ROWS, XA_DH), lambda g: (0, g, 0)),
        out_shape=jax.ShapeDtypeStruct((XA_HEADS, N_S, XA_DH), F32),
        scratch_shapes=[pltpu.VMEM((XA_HEADS * SAMPLE_ROWS, XA_DH), F32)],
        compiler_params=_params(1),
        name="xattn_sample",
    )(q, ck, cv)


def _route_kernel(op_ref, os_ref, x1_ref, wo_ref, g_ref, wr_ref, br_ref,
                  x2_ref, hm_ref, route_ref, cnt_ref, run_ref):
    i = pl.program_id(0)

    @pl.when(i == 0)
    def _():
        run_ref[...] = jnp.zeros_like(run_ref)

    def run(o):
        x2 = x1_ref[...] + _dot(o, wo_ref[...])
        x2_ref[...] = x2
        hm = _rms(x2, g_ref[...])
        hm_ref[...] = hm
        logits = _dot(hm.astype(BF16), wr_ref[...]) + br_ref[...]
        lane = lax.broadcasted_iota(jnp.int32, logits.shape, 1)
        neg = jnp.float32(-jnp.inf)
        big = jnp.int32(1 << 20)
        is_g = lane < MOE_GROUPS
        gl = jnp.where(is_g, logits, neg)
        gmax = jnp.max(gl, axis=-1, keepdims=True)
        g_sel = jnp.min(jnp.where(gl == gmax, lane, big), axis=-1, keepdims=True)
        g_w = 1.0 / jnp.sum(jnp.where(is_g, jnp.exp(logits - gmax), 0.0), axis=-1, keepdims=True)
        e_lane = lane - MOE_GROUPS
        in_grp = (e_lane >= 0) & (e_lane < MOE_EXPERTS) & ((e_lane >> 3) == g_sel)
        el = jnp.where(in_grp, logits, neg)
        v0 = jnp.max(el, axis=-1, keepdims=True)
        i0 = jnp.min(jnp.where(el == v0, lane, big), axis=-1, keepdims=True)
        el1 = jnp.where(lane == i0, neg, el)
        v1 = jnp.max(el1, axis=-1, keepdims=True)
        i1 = jnp.min(jnp.where(el1 == v1, lane, big), axis=-1, keepdims=True)
        ex = jnp.exp(v1 - v0)
        den = 1.0 + ex
        w0 = (1.0 / den) * g_w
        w1 = (ex / den) * g_w
        a0 = (lane == i0).astype(F32)
        a1 = (lane == i1).astype(F32)
        a = a0 + a1
        rr = lax.broadcasted_iota(jnp.int32, (ROW_TILE, ROW_TILE), 0)
        cc = lax.broadcasted_iota(jnp.int32, (ROW_TILE, ROW_TILE), 1)
        lower = jnp.where(cc < rr, 1.0, 0.0).astype(BF16)
        before = _dot(lower, a.astype(BF16)) + run_ref[...]
        rank0 = jnp.sum(before * a0, axis=-1, keepdims=True)
        rank1 = jnp.sum(before * a1, axis=-1, keepdims=True)
        run_ref[...] += jnp.sum(a, axis=0, keepdims=True)
        e0 = (i0 - MOE_GROUPS).astype(F32)
        e1 = (i1 - MOE_GROUPS).astype(F32)
        route = jnp.where(lane == 0, e0, 0.0)
        route = jnp.where(lane == 1, e1, route)
        route = jnp.where(lane == 2, rank0, route)
        route = jnp.where(lane == 3, rank1, route)
        route = jnp.where(lane == 4, w0, route)
        route = jnp.where(lane == 5, w1, route)
        route_ref[...] = route
        cnt_ref[...] = run_ref[...]

    @pl.when(i < NP_TILES)
    def _():
        run(op_ref[...])

    @pl.when(i == NP_TILES)
    def _():
        run(jnp.concatenate([os_ref[h] for h in range(XA_HEADS)], axis=1).astype(BF16))


def _route(o_p, o_s, x1, w_co, g_moe, w_r, b_r):
    return pl.pallas_call(
        _route_kernel,
        grid=(N_TILES,),
        in_specs=[
            pl.BlockSpec((ROW_TILE, XA_W), lambda i: (jnp.minimum(i, NP_TILES - 1), 0)),
            pl.BlockSpec((XA_HEADS, N_S, XA_DH), lambda i: (0, 0, 0)),
            pl.BlockSpec((ROW_TILE, D_MODEL), lambda i: (i, 0)),
            _resident((XA_W, D_MODEL)),
            pl.BlockSpec((1, D_MODEL), lambda i: (0, 0)),
            _resident((D_MODEL, ROUTE_LANES)),
            pl.BlockSpec((1, ROUTE_LANES), lambda i: (0, 0)),
        ],
        out_specs=[
            pl.BlockSpec((ROW_TILE, D_MODEL), lambda i: (i, 0)),
            pl.BlockSpec((ROW_TILE, D_MODEL), lambda i: (i, 0)),
            pl.BlockSpec((ROW_TILE, ROUTE_LANES), lambda i: (i, 0)),
            pl.BlockSpec((1, ROUTE_LANES), lambda i: (0, 0)),
        ],
        out_shape=[
            jax.ShapeDtypeStruct((N_ALL, D_MODEL), F32),
            jax.ShapeDtypeStruct((N_ALL, D_MODEL), F32),
            jax.ShapeDtypeStruct((N_ALL, ROUTE_LANES), F32),
            jax.ShapeDtypeStruct((1, ROUTE_LANES), F32),
        ],
        scratch_shapes=[pltpu.VMEM((1, ROUTE_LANES), F32)],
        compiler_params=_params(1),
        name="xa_out_route",
    )(o_p, o_s, x1, w_co, g_moe, w_r, b_r)


def _expert_kernel(te_ref, nu_ref, src_ref, hm_ref, w1_ref, w3_ref, w2_ref, y_ref, xbuf, sem):
    i = pl.program_id(0)
    n_used = nu_ref[0]

    def row_copy(tile, slot, r):
        tok = src_ref[tile * MOE_TM + r]
        return pltpu.make_async_copy(hm_ref.at[pl.ds(tok, 1), :], xbuf.at[slot, pl.ds(r, 1), :], sem.at[slot])

    def start_gather(tile, slot):
        def body(r, carry):
            row_copy(tile, slot, r).start()
            return carry
        lax.fori_loop(0, MOE_TM, body, 0, unroll=8)

    def wait_gather(tile, slot):
        def body(r, carry):
            row_copy(tile, slot, r).wait()
            return carry
        lax.fori_loop(0, MOE_TM, body, 0, unroll=8)

    @pl.when(i == 0)
    def _():
        start_gather(0, 0)

    @pl.when(i < n_used)
    def _():
        slot = i % 2
        wait_gather(i, slot)

        @pl.when(i + 1 < n_used)
        def _():
            start_gather(i + 1, 1 - slot)

        x = xbuf[slot].astype(BF16)
        h1 = _dot(x, w1_ref[0].astype(BF16))
        h3 = _dot(x, w3_ref[0].astype(BF16))
        he = (jax.nn.silu(h1) * h3).astype(BF16)
        y_ref[...] = _dot(he, w2_ref[0].astype(BF16))

    @pl.when(i >= n_used)
    def _():
        y_ref[...] = jnp.zeros_like(y_ref)


def _experts(tile_expert, n_used, src_tok, hm, w_e1, w_e3, w_e2):
    def tile(i, te, nu, src):
        return jnp.minimum(i, nu[0] - 1)

    grid_spec = pltpu.PrefetchScalarGridSpec(
        num_scalar_prefetch=3,
        grid=(MOE_NT,),
        in_specs=[
            pl.BlockSpec(memory_space=pl.ANY),
            pl.BlockSpec((1, D_MODEL, MOE_FF), lambda i, te, nu, src: (te[tile(i, te, nu, src)], 0, 0)),
            pl.BlockSpec((1, D_MODEL, MOE_FF), lambda i, te, nu, src: (te[tile(i, te, nu, src)], 0, 0)),
            pl.BlockSpec((1, MOE_FF, D_MODEL), lambda i, te, nu, src: (te[tile(i, te, nu, src)], 0, 0)),
        ],
        out_specs=pl.BlockSpec((MOE_TM, D_MODEL), lambda i, te, nu, src: (i, 0)),
        scratch_shapes=[pltpu.VMEM((2, MOE_TM, D_MODEL), F32), pltpu.SemaphoreType.DMA((2,))],
    )
    return pl.pallas_call(
        _expert_kernel,
        grid_spec=grid_spec,
        out_shape=jax.ShapeDtypeStruct((MOE_ROWS, D_MODEL), F32),
        compiler_params=_params(1),
        name="experts",
    )(tile_expert, n_used, src_tok, hm, w_e1, w_e3, w_e2)


def _combine_kernel(pos_ref, ys_ref, x2_ref, route_ref, g_ref, yp_ref, ysm_ref, gbuf, sem):
    i = pl.program_id(0)

    def row_copy(r, k):
        p = pos_ref[(i * ROW_TILE + r) * MOE_TOPK + k]
        return pltpu.make_async_copy(ys_ref.at[pl.ds(p, 1), :], gbuf.at[k, pl.ds(r, 1), :], sem.at[k])

    def start(r, carry):
        row_copy(r, 0).start()
        row_copy(r, 1).start()
        return carry

    def wait(r, carry):
        row_copy(r, 0).wait()
        row_copy(r, 1).wait()
        return carry

    lax.fori_loop(0, ROW_TILE, start, 0, unroll=8)
    lax.fori_loop(0, ROW_TILE, wait, 0, unroll=8)

    route = route_ref[...]
    lane = lax.broadcasted_iota(jnp.int32, route.shape, 1)
    w0 = jnp.sum(jnp.where(lane == 4, route, 0.0), axis=-1, keepdims=True)
    w1 = jnp.sum(jnp.where(lane == 5, route, 0.0), axis=-1, keepdims=True)
    x3 = x2_ref[...] + (gbuf[0] * w0 + gbuf[1] * w1)
    y = _rms(x3, g_ref[...])

    @pl.when(i < NP_TILES)
    def _():
        yp_ref[...] = y

    @pl.when(i == NP_TILES)
    def _():
        ysm_ref[...] = y


def _combine(pos, ys, x2, route, g_f):
    grid_spec = pltpu.PrefetchScalarGridSpec(
        num_scalar_prefetch=1,
        grid=(N_TILES,),
        in_specs=[
            pl.BlockSpec(memory_space=pl.ANY),
            pl.BlockSpec((ROW_TILE, D_MODEL), lambda i, pos: (i, 0)),
            pl.BlockSpec((ROW_TILE, ROUTE_LANES), lambda i, pos: (i, 0)),
            pl.BlockSpec((1, D_MODEL), lambda i, pos: (0, 0)),
        ],
        out_specs=[
            pl.BlockSpec((ROW_TILE, D_MODEL), lambda i, pos: (jnp.minimum(i, NP_TILES - 1), 0)),
            pl.BlockSpec((ROW_TILE, D_MODEL), lambda i, pos: (0, 0)),
        ],
        scratch_shapes=[pltpu.VMEM((MOE_TOPK, ROW_TILE, D_MODEL), F32), pltpu.SemaphoreType.DMA((MOE_TOPK,))],
    )
    return pl.pallas_call(
        _combine_kernel,
        grid_spec=grid_spec,
        out_shape=[
            jax.ShapeDtypeStruct((N_P, D_MODEL), F32),
            jax.ShapeDtypeStruct((N_S, D_MODEL), F32),
        ],
        compiler_params=_params(1),
        name="combine_norm",
    )(pos, ys, x2, route, g_f)


def _routing_tables(route, counts):
    r = route[:N_ALL]
    e = r[:, 0:2].astype(jnp.int32)
    rank = r[:, 2:4].astype(jnp.int32)
    cnt = counts[0, MOE_GROUPS:MOE_GROUPS + MOE_EXPERTS].astype(jnp.int32)
    tiles = (cnt + MOE_TM - 1) // MOE_TM
    tile_end = jnp.cumsum(tiles)
    tile_start = tile_end - tiles
    n_used = tile_end[-1]
    pos = (tile_start * MOE_TM)[e] + rank
    tok = jnp.broadcast_to(jnp.arange(N_ALL, dtype=jnp.int32)[:, None], (N_ALL, MOE_TOPK))
    src_tok = jnp.zeros((MOE_ROWS,), jnp.int32).at[pos.reshape(-1)].set(tok.reshape(-1))
    t = jnp.arange(MOE_NT, dtype=jnp.int32)
    te = jnp.sum((t[:, None] >= tile_end[None, :]).astype(jnp.int32), axis=1)
    te_last = jnp.sum((n_used - 1 >= tile_end).astype(jnp.int32))
    tile_expert = jnp.where(t < n_used, jnp.minimum(te, MOE_EXPERTS - 1), te_last).astype(jnp.int32)
    return tile_expert, n_used.reshape(1).astype(jnp.int32), src_tok, pos.reshape(-1)


def kernel(x_prompt, x_sample, mem_prompt, state_ret, cache_mem_k, cache_mem_v, norm_mix, w_in, ret_gn,
           sg_ln_g, sg_ln_b, sg_ws, sg_bs, w_a_out, w_b_out, w_o, norm_xa, w_cq, w_ck, w_cv, w_co, norm_moe,
           w_rg, b_rg, w_re, b_re, w_e1, w_e3, w_e2, norm_f):
    xp = x_prompt.reshape(N_P, D_MODEL)
    xs = x_sample.reshape(N_S, D_MODEL)

    h = _norm_rows(xp, xs, norm_mix)
    cos_t, sin_t = _rope_tables()
    z = _inproj(h, w_in[0], cos_t, sin_t)

    a_p, ret_p = _ret_prompt(z, ret_gn, RET_CHUNK)
    a_s, ret_s = _ret_sample(z, ret_gn, state_ret)

    b_all, sgv = _sgate(z, sg_ln_g, sg_ln_b, sg_ws[0], sg_bs[0])

    x1, qx = _merge(a_p, a_s, b_all, z, xp, xs, w_a_out[0].astype(BF16), w_b_out[0].astype(BF16),
                    w_o[0].astype(BF16), norm_xa, w_cq[0].astype(BF16))

    mk, mv, mkb, mvb = _memkv(mem_prompt.reshape(BATCH * MEM_LEN, D_MODEL), w_ck[0], w_cv[0])
    o_p = _xattn_prompt(qx, mkb, mvb)
    ck = cache_mem_k.reshape(DEC_BATCH, MEM_LEN * XA_HEADS, XA_DH)
    cv = cache_mem_v.reshape(DEC_BATCH, MEM_LEN * XA_HEADS, XA_DH)
    o_s = _xattn_sample(qx, ck, cv)

    w_r = jnp.zeros((D_MODEL, ROUTE_LANES), F32)
    w_r = w_r.at[:, :MOE_GROUPS].set(w_rg[0]).at[:, MOE_GROUPS:MOE_GROUPS + MOE_EXPERTS].set(w_re[0])
    b_r = jnp.zeros((1, ROUTE_LANES), F32)
    b_r = b_r.at[0, :MOE_GROUPS].set(b_rg[0]).at[0, MOE_GROUPS:MOE_GROUPS + MOE_EXPERTS].set(b_re[0])
    x2, hm, route, counts = _route(o_p, o_s, x1, w_co[0].astype(BF16), norm_moe, w_r.astype(BF16), b_r)

    tile_expert, n_used, src_tok, pos = _routing_tables(route, counts)
    ys = _experts(tile_expert, n_used, src_tok, hm, w_e1[0], w_e3[0], w_e2[0])
    y_p, y_s = _combine(pos, ys, x2, route, norm_f.reshape(1, D_MODEL))

    return (y_p.reshape(BATCH, SEQ, D_MODEL),
            y_s.reshape(DEC_BATCH, DEC_SEQ, D_MODEL),
            ret_p,
            mk.reshape(1, BATCH, MEM_LEN, XA_HEADS, XA_DH),
            mv.reshape(1, BATCH, MEM_LEN, XA_HEADS, XA_DH),
            ret_s,
            sgv.reshape(1, DEC_BATCH, DEC_SEQ, SG_WIDTH))
```

```python
import functools

import jax
import jax.numpy as jnp
from jax import lax
from jax.experimental import pallas as pl
from jax.experimental.pallas import tpu as pltpu

F32 = jnp.float32
BF16 = jnp.bfloat16

D_MODEL = 2048
BATCH = 4
SEQ = 2048
DEC_BATCH = 128
DEC_SEQ = 4
PAST_LEN = 16384
RET_HEADS = 8
RET_DK = 128
RET_DV = 256
RET_CHUNK = 128
ROPE_BASE = 10000.0
RET_QK = RET_HEADS * RET_DK
RET_V = RET_HEADS * RET_DV
SG_GROUPS = 4
SG_WIDTH = 2048
SG_CHUNK = 128
MEM_LEN = 256
XA_HEADS = 4
XA_DH = 128
XA_W = XA_HEADS * XA_DH
MOE_GROUPS = 4
MOE_PER_GROUP = 8
MOE_EXPERTS = MOE_GROUPS * MOE_PER_GROUP
MOE_TOPK = 2
MOE_FF = 512
EPS = 1e-6
IN_WIDTH = 2 * RET_QK + 2 * RET_V + 2 * SG_WIDTH + 2 * D_MODEL

N_P = BATCH * SEQ
N_S = DEC_BATCH * DEC_SEQ
N_ALL = N_P + N_S
ROW_TILE = 512
N_TILES = N_ALL // ROW_TILE
NP_TILES = N_P // ROW_TILE
MERGE_TILE = 256
MERGE_P_TILES = N_P // MERGE_TILE

IN_TM = 1024
IN_TN = 1024
IN_MT = N_P // IN_TM + 1
IN_NT = IN_WIDTH // IN_TN

LOG_DEC_SEQ = 2
LOG_SG_CHUNK = 7
SAMPLE_BB = 8
SAMPLE_ROWS = SAMPLE_BB * DEC_SEQ
SAMPLE_STEPS = DEC_BATCH // SAMPLE_BB

MOE_TM = 256
MOE_NT = (N_ALL * MOE_TOPK + MOE_EXPERTS * (MOE_TM - 1) + MOE_TM - 1) // MOE_TM
MOE_ROWS = MOE_NT * MOE_TM
ROUTE_LANES = 128
ROW_SUB, ROW_LANE = 16, 128
W_SLOTS = 3

VMEM_LIMIT = 56 * 1024 * 1024


def _params(n_axes, vmem=VMEM_LIMIT):
    return pltpu.CompilerParams(dimension_semantics=("arbitrary",) * n_axes,
                                vmem_limit_bytes=vmem)


def _rms(x, g):
    ms = jnp.mean(x * x, axis=-1, keepdims=True)
    return (x * lax.rsqrt(ms + EPS)) * g


def _dot(a, b):
    return jnp.dot(a, b, preferred_element_type=F32)


def _to_row_tiles(x):
    return x.astype(BF16).reshape(x.shape[0], ROW_SUB, ROW_LANE)


def _from_row_tiles(t):
    return t.reshape(t.shape[0], D_MODEL)


def _dot_nt(a, b):
    return lax.dot_general(a, b, (((1,), (1,)), ((), ())), preferred_element_type=F32)


def _dot_tn(a, b):
    return lax.dot_general(a, b, (((0,), (0,)), ((), ())), preferred_element_type=F32)


def _norm_kernel(xp_ref, xs_ref, g_ref, h_ref):
    i = pl.program_id(0)

    @pl.when(i < NP_TILES)
    def _():
        h_ref[...] = _rms(xp_ref[...], g_ref[...]).astype(BF16)

    @pl.when(i == NP_TILES)
    def _():
        h_ref[...] = _rms(xs_ref[...], g_ref[...]).astype(BF16)


def _norm_rows(xp, xs, g):
    return pl.pallas_call(
        _norm_kernel,
        grid=(N_TILES,),
        in_specs=[
            pl.BlockSpec((ROW_TILE, D_MODEL), lambda i: (jnp.minimum(i, NP_TILES - 1), 0)),
            pl.BlockSpec((ROW_TILE, D_MODEL), lambda i: (0, 0)),
            pl.BlockSpec((1, D_MODEL), lambda i: (0, 0)),
        ],
        out_specs=pl.BlockSpec((ROW_TILE, D_MODEL), lambda i: (i, 0)),
        out_shape=jax.ShapeDtypeStruct((N_ALL, D_MODEL), BF16),
        compiler_params=_params(1),
        name="norm_rows",
    )(xp, xs, g)


def _inproj_kernel(kind, h_ref, w_ref, *rest):
    if kind == "rope":
        cos_ref, sin_ref, z_ref, wb_ref = rest
    else:
        z_ref, wb_ref = rest
    j = pl.program_id(0)
    i = pl.program_id(1)

    @pl.when(i == 0)
    def _():
        wb_ref[...] = w_ref[...].astype(BF16)

    def tile(rows):
        acc = _dot(h_ref[0:rows, :], wb_ref[...])
        if kind == "rope":
            scale = jnp.where(j == 1, RET_DK ** -0.5, 1.0).astype(F32)
            c = cos_ref[0:rows, :]
            s = sin_ref[0:rows, :]
            for hb in range(IN_TN // RET_DK):
                cols = slice(hb * RET_DK, (hb + 1) * RET_DK)
                a = acc[:, cols]
                r = pltpu.roll(a, RET_DK // 2, axis=1)
                z_ref[0:rows, cols] = ((a * c + r * s) * scale).astype(BF16)
        elif kind == "copy":
            z_ref[0:rows, :] = acc.astype(BF16)
        elif kind == "silu":
            z_ref[0:rows, :] = jax.nn.silu(acc).astype(BF16)
        elif kind == "gelu":
            z_ref[0:rows, :] = jax.nn.gelu(acc).astype(BF16)
        else:
            z_ref[0:rows, :] = jax.nn.sigmoid(acc).astype(BF16)

    @pl.when(i < IN_MT - 1)
    def _():
        tile(IN_TM)

    @pl.when(i == IN_MT - 1)
    def _():
        tile(N_S)


def _inproj(kind, col0, width, h, w_in, tables=()):
    tab_idx = lambda j, i: (jnp.where(i < IN_MT - 1, i % (SEQ // IN_TM), SEQ // IN_TM), 0)
    j0 = col0 // IN_TN
    return pl.pallas_call(
        functools.partial(_inproj_kernel, kind),
        grid=(width // IN_TN, IN_MT),
        in_specs=[
            pl.BlockSpec((IN_TM, D_MODEL), lambda j, i: (i, 0)),
            pl.BlockSpec((D_MODEL, IN_TN), lambda j, i: (0, j0 + j)),
        ] + [pl.BlockSpec((IN_TM, RET_DK), tab_idx) for _ in tables],
        out_specs=pl.BlockSpec((IN_TM, IN_TN), lambda j, i: (i, j)),
        out_shape=jax.ShapeDtypeStruct((N_ALL, width), BF16),
        scratch_shapes=[pltpu.VMEM((D_MODEL, IN_TN), BF16)],
        compiler_params=_params(2),
        name="in_proj_" + kind,
    )(h, w_in, *tables)


def _rope_tables():
    half = RET_DK // 2
    inv = ROPE_BASE ** (-jnp.arange(half, dtype=F32) / half)

    def tab(pos):
        ang = pos.astype(F32)[:, None] * inv[None, :]
        c, s = jnp.cos(ang), jnp.sin(ang)
        return jnp.concatenate([c, c], -1), jnp.concatenate([-s, s], -1)

    cp, sp = tab(jnp.arange(SEQ, dtype=jnp.int32))
    cs, ss = tab(PAST_LEN + jnp.arange(DEC_SEQ, dtype=jnp.int32))
    cs = jnp.tile(cs, (DEC_BATCH, 1))
    ss = jnp.tile(ss, (DEC_BATCH, 1))
    pad = jnp.zeros((IN_TM - N_S, RET_DK), F32)
    return jnp.concatenate([cp, cs, pad], 0), jnp.concatenate([sp, ss, pad], 0)


def _decay_tables(chunk):
    lg = jnp.log1p(-jnp.power(2.0, -5.0 - jnp.arange(RET_HEADS, dtype=F32)))
    idx = jnp.arange(chunk, dtype=F32)
    rel = idx[:, None] - idx[None, :]
    dmask = jnp.where(rel >= 0, jnp.exp(lg[:, None, None] * jnp.maximum(rel, 0.0)), 0.0).astype(F32)
    xi = jnp.exp(lg[:, None] * (idx[None, :] + 1.0)).astype(F32)
    zeta = jnp.exp(lg[:, None] * (chunk - 1.0 - idx[None, :])).astype(F32)
    gc = jnp.exp(lg * chunk).astype(F32)
    return dmask, xi, zeta, gc


def _head_norm_gate(o, gn, gate):
    mu = jnp.mean(o, axis=-1, keepdims=True)
    d = o - mu
    var = jnp.mean(d * d, axis=-1, keepdims=True)
    y = (d * lax.rsqrt(var + EPS)) * gn
    return (gate.astype(F32) * y).astype(BF16)


def _ret_prompt_kernel(gc_ref, q_ref, k_ref, v_ref, gs_ref, gn_ref, dm_ref, xi_ref, zt_ref,
                       a_ref, sfin_ref, s_ref):
    c = pl.program_id(1)

    @pl.when(c == 0)
    def _():
        s_ref[...] = jnp.zeros_like(s_ref)

    for h in range(RET_HEADS):
        kc = slice(h * RET_DK, (h + 1) * RET_DK)
        vc = slice(h * RET_DV, (h + 1) * RET_DV)
        qh = q_ref[:, kc]
        kh = k_ref[:, kc]
        vh = v_ref[:, vc]
        inner = _dot_nt(qh, kh) * dm_ref[h]
        o = _dot(inner.astype(BF16), vh)
        s_old = s_ref[h]
        xi = xi_ref[h]
        o = o + _dot(qh, s_old.astype(BF16)) * jnp.concatenate([xi, xi], axis=1)
        kz = (kh.astype(F32) * zt_ref[h]).astype(BF16)
        s_ref[h] = gc_ref[h] * s_old + _dot_tn(kz, vh)
        a_ref[:, vc] = _head_norm_gate(o, gn_ref[:, vc], gs_ref[:, vc])

    @pl.when(c == pl.num_programs(1) - 1)
    def _():
        sfin_ref[0, 0] = s_ref[...]


def _ret_prompt(qk, v, gs, ret_gn, chunk):
    dmask, xi, zeta, gc = _decay_tables(chunk)
    xi_b = jnp.broadcast_to(xi[:, :, None], (RET_HEADS, chunk, RET_DK))
    zeta_b = jnp.broadcast_to(zeta[:, :, None], (RET_HEADS, chunk, RET_DK))
    n_chunks = SEQ // chunk
    row = lambda b, c: b * n_chunks + c
    const3 = lambda b, c: (0, 0, 0)
    return pl.pallas_call(
        _ret_prompt_kernel,
        grid=(BATCH, n_chunks),
        in_specs=[
            pl.BlockSpec(memory_space=pltpu.SMEM),
            pl.BlockSpec((chunk, RET_QK), lambda b, c: (row(b, c), 0)),
            pl.BlockSpec((chunk, RET_QK), lambda b, c: (row(b, c), 1)),
            pl.BlockSpec((chunk, RET_V), lambda b, c: (row(b, c), 0)),
            pl.BlockSpec((chunk, RET_V), lambda b, c: (row(b, c), 0)),
            pl.BlockSpec((1, RET_V), lambda b, c: (0, 0)),
            pl.BlockSpec((RET_HEADS, chunk, chunk), const3),
            pl.BlockSpec((RET_HEADS, chunk, RET_DK), const3),
            pl.BlockSpec((RET_HEADS, chunk, RET_DK), const3),
        ],
        out_specs=[
            pl.BlockSpec((chunk, RET_V), lambda b, c: (row(b, c), 0)),
            pl.BlockSpec((1, 1, RET_HEADS, RET_DK, RET_DV), lambda b, c: (0, b, 0, 0, 0)),
        ],
        out_shape=[
            jax.ShapeDtypeStruct((N_P, RET_V), BF16),
            jax.ShapeDtypeStruct((1, BATCH, RET_HEADS, RET_DK, RET_DV), F32),
        ],
        scratch_shapes=[pltpu.VMEM((RET_HEADS, RET_DK, RET_DV), F32)],
        compiler_params=_params(2),
        name="ret_prompt",
    )(gc, qk, qk, v, gs, ret_gn, dmask, xi_b, zeta_b)


def _ret_sample_kernel(gc_ref, q_ref, k_ref, v_ref, gs_ref, gn_ref, dm_ref, xi_ref, zt_ref, s0_ref,
                       a_ref, s1_ref, o_ref):
    rows_k = lax.broadcasted_iota(jnp.int32, (SAMPLE_ROWS, RET_DK), 0) >> LOG_DEC_SEQ
    rows_v = lax.broadcasted_iota(jnp.int32, (SAMPLE_ROWS, RET_DV), 0) >> LOG_DEC_SEQ
    for h in range(RET_HEADS):
        kc = slice(h * RET_DK, (h + 1) * RET_DK)
        vc = slice(h * RET_DV, (h + 1) * RET_DV)
        qh = q_ref[:, kc]
        kh = k_ref[:, kc]
        vh = v_ref[:, vc]
        inner = _dot_nt(qh, kh) * dm_ref[h]
        o_ref[:, vc] = _dot(inner.astype(BF16), vh)
        xi = xi_ref[h]
        xi2 = jnp.concatenate([xi, xi], axis=1)
        kz = kh.astype(F32) * zt_ref[h]
        gch = gc_ref[h]

        def body(b, carry):
            s_old = s0_ref[0, b, h]
            cross = _dot(qh, s_old.astype(BF16)) * xi2
            o_ref[:, vc] += jnp.where(rows_v == b, cross, 0.0)
            kz_b = jnp.where(rows_k == b, kz, 0.0).astype(BF16)
            s1_ref[0, b, h] = gch * s_old + _dot_tn(kz_b, vh)
            return carry

        lax.fori_loop(0, SAMPLE_BB, body, 0)
        a_ref[:, vc] = _head_norm_gate(o_ref[:, vc], gn_ref[:, vc], gs_ref[:, vc])


def _ret_sample(qk, v, gs, ret_gn, state):
    dmask, xi, zeta, gc = _decay_tables(DEC_SEQ)
    eye = jnp.eye(SAMPLE_BB, dtype=F32)
    dm_big = jax.vmap(lambda m: jnp.kron(eye, m))(dmask)
    xi_b = jnp.broadcast_to(jnp.tile(xi, (1, SAMPLE_BB))[:, :, None], (RET_HEADS, SAMPLE_ROWS, RET_DK))
    zeta_b = jnp.broadcast_to(jnp.tile(zeta, (1, SAMPLE_BB))[:, :, None], (RET_HEADS, SAMPLE_ROWS, RET_DK))
    base = N_P // SAMPLE_ROWS
    const3 = lambda g: (0, 0, 0)
    st_spec = pl.BlockSpec((1, SAMPLE_BB, RET_HEADS, RET_DK, RET_DV), lambda g: (0, g, 0, 0, 0))
    return pl.pallas_call(
        _ret_sample_kernel,
        grid=(SAMPLE_STEPS,),
        in_specs=[
            pl.BlockSpec(memory_space=pltpu.SMEM),
            pl.BlockSpec((SAMPLE_ROWS, RET_QK), lambda g: (base + g, 0)),
            pl.BlockSpec((SAMPLE_ROWS, RET_QK), lambda g: (base + g, 1)),
            pl.BlockSpec((SAMPLE_ROWS, RET_V), lambda g: (base + g, 0)),
            pl.BlockSpec((SAMPLE_ROWS, RET_V), lambda g: (base + g, 0)),
            pl.BlockSpec((1, RET_V), lambda g: (0, 0)),
            pl.BlockSpec((RET_HEADS, SAMPLE_ROWS, SAMPLE_ROWS), const3),
            pl.BlockSpec((RET_HEADS, SAMPLE_ROWS, RET_DK), const3),
            pl.BlockSpec((RET_HEADS, SAMPLE_ROWS, RET_DK), const3),
            st_spec,
        ],
        out_specs=[
            pl.BlockSpec((SAMPLE_ROWS, RET_V), lambda g: (g, 0)),
            st_spec,
        ],
        out_shape=[
            jax.ShapeDtypeStruct((N_S, RET_V), BF16),
            jax.ShapeDtypeStruct((1, DEC_BATCH, RET_HEADS, RET_DK, RET_DV), F32),
        ],
        scratch_shapes=[pltpu.VMEM((SAMPLE_ROWS, RET_V), F32)],
        compiler_params=_params(1),
        name="ret_sample",
    )(gc, qk, qk, v, gs, ret_gn, dm_big, xi_b, zeta_b, state)


def _sgate_kernel(u_ref, v_ref, lg_ref, lb_ref, wp_ref, bp_ref, ws_ref, bs_ref, b_ref, sgv_ref, vln_ref):
    i = pl.program_id(0)
    v = v_ref[...].astype(F32)
    mu = jnp.mean(v, axis=-1, keepdims=True)
    d = v - mu
    var = jnp.mean(d * d, axis=-1, keepdims=True)
    vln_ref[...] = (d * lax.rsqrt(var + EPS)) * lg_ref[...] + lb_ref[...]
    gw = SG_WIDTH // SG_GROUPS
    lane_reps = gw // 128

    @pl.when(i < NP_TILES)
    def _():
        r = lax.broadcasted_iota(jnp.int32, (SG_CHUNK, SG_CHUNK), 0)
        c = lax.broadcasted_iota(jnp.int32, (SG_CHUNK, SG_CHUNK), 1)
        for g in range(SG_GROUPS):
            cols = slice(g * gw, (g + 1) * gw)
            w = jnp.where(c <= r, wp_ref[g], 0.0).astype(BF16)
            bias = jnp.concatenate([bp_ref[g]] * lane_reps, axis=1)
            for ch in range(ROW_TILE // SG_CHUNK):
                rows = slice(ch * SG_CHUNK, (ch + 1) * SG_CHUNK)
                mixed = _dot(w, vln_ref[rows, cols].astype(BF16)) + bias
                b_ref[rows, cols] = (u_ref[rows, cols].astype(F32) * mixed).astype(BF16)

    @pl.when(i == NP_TILES)
    def _():
        sgv_ref[...] = vln_ref[...]
        r = lax.broadcasted_iota(jnp.int32, (ROW_TILE, ROW_TILE), 0)
        c = lax.broadcasted_iota(jnp.int32, (ROW_TILE, ROW_TILE), 1)
        keep = ((r >> LOG_DEC_SEQ) == (c >> LOG_DEC_SEQ)) & (c <= r)
        for g in range(SG_GROUPS):
            cols = slice(g * gw, (g + 1) * gw)
            w_rows = jnp.concatenate([ws_ref[g]] * (ROW_TILE // 8), axis=0)
            w_full = jnp.concatenate([w_rows] * (ROW_TILE // 128), axis=1)
            w = jnp.where(keep, w_full, 0.0).astype(BF16)
            b_rows = jnp.concatenate([bs_ref[g]] * (ROW_TILE // 8), axis=0)
            bias = jnp.concatenate([b_rows] * lane_reps, axis=1)
            mixed = _dot(w, vln_ref[:, cols].astype(BF16)) + bias
            b_ref[:, cols] = (u_ref[:, cols].astype(F32) * mixed).astype(BF16)


def _sgate(uv, ln_g, ln_b, sg_ws, sg_bs):
    b_p = jnp.broadcast_to(sg_bs[:, :, None], (SG_GROUPS, SG_CHUNK, 128))
    w_s = jnp.tile(sg_ws[:, :DEC_SEQ, :DEC_SEQ], (1, 8 // DEC_SEQ, 128 // DEC_SEQ))
    b_s = jnp.broadcast_to(jnp.tile(sg_bs[:, :DEC_SEQ], (1, 8 // DEC_SEQ))[:, :, None], (SG_GROUPS, 8, 128))
    const3 = lambda i: (0, 0, 0)
    return pl.pallas_call(
        _sgate_kernel,
        grid=(N_TILES,),
        in_specs=[
            pl.BlockSpec((ROW_TILE, SG_WIDTH), lambda i: (i, 0)),
            pl.BlockSpec((ROW_TILE, SG_WIDTH), lambda i: (i, 1)),
            pl.BlockSpec((1, SG_WIDTH), lambda i: (0, 0)),
            pl.BlockSpec((1, SG_WIDTH), lambda i: (0, 0)),
            pl.BlockSpec((SG_GROUPS, SG_CHUNK, SG_CHUNK), const3),
            pl.BlockSpec((SG_GROUPS, SG_CHUNK, 128), const3),
            pl.BlockSpec((SG_GROUPS, 8, 128), const3),
            pl.BlockSpec((SG_GROUPS, 8, 128), const3),
        ],
        out_specs=[
            pl.BlockSpec((ROW_TILE, SG_WIDTH), lambda i: (i, 0)),
            pl.BlockSpec((N_S, SG_WIDTH), lambda i: (0, 0)),
        ],
        out_shape=[
            jax.ShapeDtypeStruct((N_ALL, SG_WIDTH), BF16),
            jax.ShapeDtypeStruct((N_S, SG_WIDTH), F32),
        ],
        scratch_shapes=[pltpu.VMEM((ROW_TILE, SG_WIDTH), F32)],
        compiler_params=_params(1),
        name="spatial_gate",
    )(uv, uv, ln_g, ln_b, sg_ws, b_p, w_s, b_s)


def _merge_kernel(ap_ref, as_ref, b_ref, ga_ref, gb_ref, xp_ref, xs_ref, wa_ref, wb_ref, wo_ref,
                  g_ref, wq_ref, x1_ref, q_ref):
    i = pl.program_id(0)

    def run(a_in, x):
        a = _dot(a_in, wa_ref[...])
        b = _dot(b_ref[...], wb_ref[...])
        merged = ga_ref[...].astype(F32) * a + gb_ref[...].astype(F32) * b
        x1 = x + _dot(merged.astype(BF16), wo_ref[...])
        x1_ref[...] = x1
        q_ref[...] = _dot(_rms(x1, g_ref[...]).astype(BF16), wq_ref[...]).astype(BF16)

    @pl.when(i < MERGE_P_TILES)
    def _():
        run(ap_ref[...], xp_ref[...])

    @pl.when(i >= MERGE_P_TILES)
    def _():
        run(as_ref[...], xs_ref[...])


def _resident(shape):
    return pl.BlockSpec(shape, lambda i: (0,) * len(shape), pipeline_mode=pl.Buffered(1))


def _merge(a_p, a_s, b_all, gab, xp, xs, wa, wb, wo, g_xa, wq):
    tm = MERGE_TILE
    prompt_tile = lambda i: (jnp.minimum(i, MERGE_P_TILES - 1), 0)
    sample_tile = lambda i: (jnp.maximum(i - MERGE_P_TILES, 0), 0)
    return pl.pallas_call(
        _merge_kernel,
        grid=(N_ALL // tm,),
        in_specs=[
            pl.BlockSpec((tm, RET_V), prompt_tile),
            pl.BlockSpec((tm, RET_V), sample_tile),
            pl.BlockSpec((tm, SG_WIDTH), lambda i: (i, 0)),
            pl.BlockSpec((tm, D_MODEL), lambda i: (i, 0)),
            pl.BlockSpec((tm, D_MODEL), lambda i: (i, 1)),
            pl.BlockSpec((tm, D_MODEL), prompt_tile),
            pl.BlockSpec((tm, D_MODEL), sample_tile),
            _resident((RET_V, D_MODEL)),
            _resident((SG_WIDTH, D_MODEL)),
            _resident((D_MODEL, D_MODEL)),
            pl.BlockSpec((1, D_MODEL), lambda i: (0, 0)),
            _resident((D_MODEL, XA_W)),
        ],
        out_specs=[
            pl.BlockSpec((tm, D_MODEL), lambda i: (i, 0)),
            pl.BlockSpec((tm, XA_W), lambda i: (i, 0)),
        ],
        out_shape=[
            jax.ShapeDtypeStruct((N_ALL, D_MODEL), F32),
            jax.ShapeDtypeStruct((N_ALL, XA_W), BF16),
        ],
        compiler_params=_params(1),
        name="merge_proj",
    )(a_p, a_s, b_all, gab, gab, xp, xs, wa, wb, wo, g_xa, wq)


def _memkv_kernel(m_ref, wk_ref, wv_ref, k_ref, v_ref, kb_ref, vb_ref):
    m = m_ref[...].astype(BF16)
    k = _dot(m, wk_ref[...].astype(BF16))
    v = _dot(m, wv_ref[...].astype(BF16))
    k_ref[...] = k
    v_ref[...] = v
    kb_ref[...] = k.astype(BF16)
    vb_ref[...] = v.astype(BF16)


def _memkv(mem, w_ck, w_cv):
    rows = BATCH * MEM_LEN
    spec = pl.BlockSpec((MEM_LEN, XA_W), lambda b: (b, 0))
    wspec = pl.BlockSpec((D_MODEL, XA_W), lambda b: (0, 0))
    return pl.pallas_call(
        _memkv_kernel,
        grid=(BATCH,),
        in_specs=[pl.BlockSpec((MEM_LEN, D_MODEL), lambda b: (b, 0)), wspec, wspec],
        out_specs=[spec, spec, spec, spec],
        out_shape=[jax.ShapeDtypeStruct((rows, XA_W), F32)] * 2 + [jax.ShapeDtypeStruct((rows, XA_W), BF16)] * 2,
        compiler_params=_params(1),
        name="mem_kv",
    )(mem, w_ck, w_cv)


def _softmax_rows(s):
    m = jnp.max(s, axis=-1, keepdims=True)
    e = jnp.exp(s - m)
    return e / jnp.sum(e, axis=-1, keepdims=True)


def _xattn_prompt_kernel(q_ref, k_ref, v_ref, o_ref):
    for h in range(XA_HEADS):
        cols = slice(h * XA_DH, (h + 1) * XA_DH)
        s = _dot_nt(q_ref[:, cols], k_ref[:, cols]) * (XA_DH ** -0.5)
        p = _softmax_rows(s)
        o_ref[:, cols] = _dot(p.astype(BF16), v_ref[:, cols]).astype(BF16)


def _xattn_prompt(q, kb, vb):
    per_b = SEQ // ROW_TILE
    kv = pl.BlockSpec((MEM_LEN, XA_W), lambda i: (i // per_b, 0))
    return pl.pallas_call(
        _xattn_prompt_kernel,
        grid=(NP_TILES,),
        in_specs=[pl.BlockSpec((ROW_TILE, XA_W), lambda i: (i, 0)), kv, kv],
        out_specs=pl.BlockSpec((ROW_TILE, XA_W), lambda i: (i, 0)),
        out_shape=jax.ShapeDtypeStruct((N_P, XA_W), BF16),
        compiler_params=_params(1),
        name="xattn_prompt",
    )(q, kb, vb)


def _xattn_sample_kernel(q_ref, k_ref, v_ref, o_ref, acc_ref):
    n_q = XA_HEADS * SAMPLE_ROWS
    n_kv = MEM_LEN * XA_HEADS
    q4 = jnp.concatenate([q_ref[:, h * XA_DH:(h + 1) * XA_DH] for h in range(XA_HEADS)], axis=0)
    r = lax.broadcasted_iota(jnp.int32, (n_q, n_kv), 0)
    c = lax.broadcasted_iota(jnp.int32, (n_q, n_kv), 1)
    head_ok = (r >> 5) == (c & (XA_HEADS - 1))
    row_b = (lax.broadcasted_iota(jnp.int32, (n_q, XA_DH), 0) & (SAMPLE_ROWS - 1)) >> LOG_DEC_SEQ
    acc_ref[...] = jnp.zeros_like(acc_ref)

    def body(b, carry):
        kb = k_ref[b].astype(BF16)
        vb = v_ref[b].astype(BF16)
        s = _dot_nt(q4, kb) * (XA_DH ** -0.5)
        p = _softmax_rows(jnp.where(head_ok, s, -1e30))
        o = _dot(p.astype(BF16), vb)
        acc_ref[...] += jnp.where(row_b == b, o, 0.0)
        return carry

    lax.fori_loop(0, SAMPLE_BB, body, 0)
    for h in range(XA_HEADS):
        o_ref[h] = acc_ref[h * SAMPLE_ROWS:(h + 1) * SAMPLE_ROWS, :]


def _xattn_sample(q, ck, cv):
    base = N_P // SAMPLE_ROWS
    kv = pl.BlockSpec((SAMPLE_BB, MEM_LEN * XA_HEADS, XA_DH), lambda g: (g, 0, 0))
    return pl.pallas_call(
        _xattn_sample_kernel,
        grid=(SAMPLE_STEPS,),
        in_specs=[pl.BlockSpec((SAMPLE_ROWS, XA_W), lambda g: (base + g, 0)), kv, kv],
        out_specs=pl.BlockSpec((XA_HEADS, SAMPLE_ROWS, XA_DH), lambda g: (0, g, 0)),
        out_shape=jax.ShapeDtypeStruct((XA_HEADS, N_S, XA_DH), F32),
        scratch_shapes=[pltpu.VMEM((XA_HEADS * SAMPLE_ROWS, XA_DH), F32)],
        compiler_params=_params(1),
        name="xattn_sample",
    )(q, ck, cv)


def _route_kernel(op_ref, os_ref, x1_ref, wo_ref, g_ref, wr_ref, br_ref,
                  x2_ref, hm_ref, route_ref, cnt_ref, run_ref):
    i = pl.program_id(0)

    @pl.when(i == 0)
    def _():
        run_ref[...] = jnp.zeros_like(run_ref)

    def run(o):
        x2 = x1_ref[...] + _dot(o, wo_ref[...])
        x2_ref[...] = x2
        hm = _rms(x2, g_ref[...])
        hm_ref[...] = _to_row_tiles(hm)
        logits = _dot(hm.astype(BF16), wr_ref[...]) + br_ref[...]
        lane = lax.broadcasted_iota(jnp.int32, logits.shape, 1)
        neg = jnp.float32(-jnp.inf)
        big = jnp.int32(1 << 20)
        is_g = lane < MOE_GROUPS
        gl = jnp.where(is_g, logits, neg)
        gmax = jnp.max(gl, axis=-1, keepdims=True)
        g_sel = jnp.min(jnp.where(gl == gmax, lane, big), axis=-1, keepdims=True)
        g_w = 1.0 / jnp.sum(jnp.where(is_g, jnp.exp(logits - gmax), 0.0), axis=-1, keepdims=True)
        e_lane = lane - MOE_GROUPS
        in_grp = (e_lane >= 0) & (e_lane < MOE_EXPERTS) & ((e_lane >> 3) == g_sel)
        el = jnp.where(in_grp, logits, neg)
        v0 = jnp.max(el, axis=-1, keepdims=True)
        i0 = jnp.min(jnp.where(el == v0, lane, big), axis=-1, keepdims=True)
        el1 = jnp.where(lane == i0, neg, el)
        v1 = jnp.max(el1, axis=-1, keepdims=True)
        i1 = jnp.min(jnp.where(el1 == v1, lane, big), axis=-1, keepdims=True)
        ex = jnp.exp(v1 - v0)
        den = 1.0 + ex
        w0 = (1.0 / den) * g_w
        w1 = (ex / den) * g_w
        a0 = (lane == i0).astype(F32)
        a1 = (lane == i1).astype(F32)
        a = a0 + a1
        rr = lax.broadcasted_iota(jnp.int32, (ROW_TILE, ROW_TILE), 0)
        cc = lax.broadcasted_iota(jnp.int32, (ROW_TILE, ROW_TILE), 1)
        lower = jnp.where(cc < rr, 1.0, 0.0).astype(BF16)
        before = _dot(lower, a.astype(BF16)) + run_ref[...]
        rank0 = jnp.sum(before * a0, axis=-1, keepdims=True)
        rank1 = jnp.sum(before * a1, axis=-1, keepdims=True)
        run_ref[...] += jnp.sum(a, axis=0, keepdims=True)
        e0 = (i0 - MOE_GROUPS).astype(F32)
        e1 = (i1 - MOE_GROUPS).astype(F32)
        route = jnp.where(lane == 0, e0, 0.0)
        route = jnp.where(lane == 1, e1, route)
        route = jnp.where(lane == 2, rank0, route)
        route = jnp.where(lane == 3, rank1, route)
        route = jnp.where(lane == 4, w0, route)
        route = jnp.where(lane == 5, w1, route)
        route_ref[...] = route
        cnt_ref[...] = run_ref[...]

    @pl.when(i < NP_TILES)
    def _():
        run(op_ref[...])

    @pl.when(i == NP_TILES)
    def _():
        run(jnp.concatenate([os_ref[h] for h in range(XA_HEADS)], axis=1).astype(BF16))


def _route(o_p, o_s, x1, w_co, g_moe, w_r, b_r):
    return pl.pallas_call(
        _route_kernel,
        grid=(N_TILES,),
        in_specs=[
            pl.BlockSpec((ROW_TILE, XA_W), lambda i: (jnp.minimum(i, NP_TILES - 1), 0)),
            pl.BlockSpec((XA_HEADS, N_S, XA_DH), lambda i: (0, 0, 0)),
            pl.BlockSpec((ROW_TILE, D_MODEL), lambda i: (i, 0)),
            _resident((XA_W, D_MODEL)),
            pl.BlockSpec((1, D_MODEL), lambda i: (0, 0)),
            _resident((D_MODEL, ROUTE_LANES)),
            pl.BlockSpec((1, ROUTE_LANES), lambda i: (0, 0)),
        ],
        out_specs=[
            pl.BlockSpec((ROW_TILE, D_MODEL), lambda i: (i, 0)),
            pl.BlockSpec((ROW_TILE, ROW_SUB, ROW_LANE), lambda i: (i, 0, 0)),
            pl.BlockSpec((ROW_TILE, ROUTE_LANES), lambda i: (i, 0)),
            pl.BlockSpec((1, ROUTE_LANES), lambda i: (0, 0)),
        ],
        out_shape=[
            jax.ShapeDtypeStruct((N_ALL, D_MODEL), F32),
            jax.ShapeDtypeStruct((N_ALL, ROW_SUB, ROW_LANE), BF16),
            jax.ShapeDtypeStruct((N_ALL, ROUTE_LANES), F32),
            jax.ShapeDtypeStruct((1, ROUTE_LANES), F32),
        ],
        scratch_shapes=[pltpu.VMEM((1, ROUTE_LANES), F32)],
        compiler_params=_params(1),
        name="xa_out_route",
    )(o_p, o_s, x1, w_co, g_moe, w_r, b_r)


def _positions_kernel(route_ref, cnt_ref, pos_ref):
    route = route_ref[...]
    lane = lax.broadcasted_iota(jnp.int32, route.shape, 1)
    lane_f = lane.astype(F32)
    tiles = jnp.floor((cnt_ref[...] + (MOE_TM - 1)) * (1.0 / MOE_TM))
    lr = lax.broadcasted_iota(jnp.int32, (ROUTE_LANES, ROUTE_LANES), 0)
    lc = lax.broadcasted_iota(jnp.int32, (ROUTE_LANES, ROUTE_LANES), 1)
    before = jnp.where(lr < lc, 1.0, 0.0).astype(BF16)
    tiles8 = jnp.broadcast_to(tiles, (8, ROUTE_LANES)).astype(BF16)
    start = _dot(tiles8, before)[0:1, :] * MOE_TM

    def col(k):
        return jnp.sum(jnp.where(lane == k, route, 0.0), axis=-1, keepdims=True)

    def first_row(e):
        return jnp.sum(jnp.where(lane_f == e + MOE_GROUPS, start, 0.0), axis=-1, keepdims=True)

    p0 = first_row(col(0)) + col(2)
    p1 = first_row(col(1)) + col(3)
    p = jnp.where(lane == 0, p0, jnp.where(lane == 1, p1, 0.0))
    pos_ref[...] = p.T[0:8, :].astype(jnp.int32)


def _positions(route, counts):
    return pl.pallas_call(
        _positions_kernel,
        grid=(N_TILES,),
        in_specs=[
            pl.BlockSpec((ROW_TILE, ROUTE_LANES), lambda i: (i, 0)),
            pl.BlockSpec((1, ROUTE_LANES), lambda i: (0, 0)),
        ],
        out_specs=pl.BlockSpec((8, ROW_TILE), lambda i: (0, i)),
        out_shape=jax.ShapeDtypeStruct((8, N_ALL), jnp.int32),
        compiler_params=_params(1),
        name="positions",
    )(route, counts)


def _dispatch_kernel(pos_ref, zrow_ref, zon_ref, nu_ref, hm_ref, xs_ref, zbuf, sem, zsem):
    i = pl.program_id(0)

    def zero_tile(row):
        return pltpu.make_async_copy(zbuf, xs_ref.at[pl.ds(pl.multiple_of(row, MOE_TM), MOE_TM)], zsem)

    @pl.when(i == 0)
    def _():
        zbuf[...] = jnp.zeros_like(zbuf)
        for e in range(MOE_EXPERTS):
            @pl.when(zon_ref[e] > 0)
            def _():
                zero_tile(zrow_ref[e]).start()

        def start_tail(t, carry):
            zero_tile(t * MOE_TM).start()
            return carry

        def wait_tail(t, carry):
            zero_tile(t * MOE_TM).wait()
            return carry

        lax.fori_loop(nu_ref[0], MOE_NT, start_tail, 0)
        for e in range(MOE_EXPERTS):
            @pl.when(zon_ref[e] > 0)
            def _():
                zero_tile(zrow_ref[e]).wait()
        lax.fori_loop(nu_ref[0], MOE_NT, wait_tail, 0)

    base = i * ROW_TILE

    def row_copy(r, k, dst_row):
        return pltpu.make_async_copy(hm_ref.at[r], xs_ref.at[dst_row], sem.at[k])

    def start(r, carry):
        for k in range(MOE_TOPK):
            row_copy(r, k, pos_ref[k, base + r]).start()
        return carry

    def wait(r, carry):
        for k in range(MOE_TOPK):
            row_copy(r, k, 0).wait()
        return carry

    lax.fori_loop(0, ROW_TILE, start, 0, unroll=8)
    lax.fori_loop(0, ROW_TILE, wait, 0, unroll=8)


def _dispatch(pos_t, zero_row, zero_on, n_used, hmw):
    grid_spec = pltpu.PrefetchScalarGridSpec(
        num_scalar_prefetch=4,
        grid=(N_TILES,),
        in_specs=[pl.BlockSpec((ROW_TILE, ROW_SUB, ROW_LANE), lambda i, *_: (i, 0, 0))],
        out_specs=pl.BlockSpec(memory_space=pl.ANY),
        scratch_shapes=[pltpu.VMEM((MOE_TM, ROW_SUB, ROW_LANE), BF16),
                        pltpu.SemaphoreType.DMA((MOE_TOPK,)), pltpu.SemaphoreType.DMA(())],
    )
    return pl.pallas_call(
        _dispatch_kernel,
        grid_spec=grid_spec,
        out_shape=jax.ShapeDtypeStruct((MOE_ROWS, ROW_SUB, ROW_LANE), BF16),
        compiler_params=_params(1),
        name="dispatch",
    )(pos_t, zero_row, zero_on, n_used, hmw)


def _expert_kernel(nu_ref, first_ref, ord_ref, oe_ref, no_ref, x_ref, w1_hbm, w3_hbm, w2_hbm, y_ref,
                   w1b, w3b, w2b, sem):
    i = pl.program_id(0)
    n_used = nu_ref[0]
    n_ord = no_ref[0]

    def weight_copies(k):
        e = oe_ref[k]
        slot = k % W_SLOTS
        return (pltpu.make_async_copy(w1_hbm.at[e], w1b.at[slot], sem.at[0, slot]),
                pltpu.make_async_copy(w3_hbm.at[e], w3b.at[slot], sem.at[1, slot]),
                pltpu.make_async_copy(w2_hbm.at[e], w2b.at[slot], sem.at[2, slot]))

    def start_weights(k):
        for cp in weight_copies(k):
            cp.start()

    @pl.when(i == 0)
    def _():
        for k in range(W_SLOTS - 1):
            @pl.when(k < n_ord)
            def _():
                start_weights(k)

    @pl.when(i < n_used)
    def _():
        k = ord_ref[i]

        @pl.when(first_ref[i] > 0)
        def _():
            for cp in weight_copies(k):
                cp.wait()

            @pl.when(k + (W_SLOTS - 1) < n_ord)
            def _():
                start_weights(k + (W_SLOTS - 1))

        slot = k % W_SLOTS
        x = _from_row_tiles(x_ref[...])
        h1 = _dot(x, w1b[slot].astype(BF16))
        h3 = _dot(x, w3b[slot].astype(BF16))
        he = (jax.nn.silu(h1) * h3).astype(BF16)
        y_ref[...] = _to_row_tiles(_dot(he, w2b[slot].astype(BF16)))

    @pl.when(i >= n_used)
    def _():
        y_ref[...] = jnp.zeros_like(y_ref)


def _experts(sched, xs, w_e1, w_e3, w_e2):
    grid_spec = pltpu.PrefetchScalarGridSpec(
        num_scalar_prefetch=5,
        grid=(MOE_NT,),
        in_specs=[
            pl.BlockSpec((MOE_TM, ROW_SUB, ROW_LANE), lambda i, nu, *_: (jnp.minimum(i, nu[0] - 1), 0, 0)),
            pl.BlockSpec(memory_space=pl.ANY),
            pl.BlockSpec(memory_space=pl.ANY),
            pl.BlockSpec(memory_space=pl.ANY),
        ],
        out_specs=pl.BlockSpec((MOE_TM, ROW_SUB, ROW_LANE), lambda i, *_: (i, 0, 0)),
        scratch_shapes=[
            pltpu.VMEM((W_SLOTS, D_MODEL, MOE_FF), F32),
            pltpu.VMEM((W_SLOTS, D_MODEL, MOE_FF), F32),
            pltpu.VMEM((W_SLOTS, MOE_FF, D_MODEL), F32),
            pltpu.SemaphoreType.DMA((3, W_SLOTS)),
        ],
    )
    return pl.pallas_call(
        _expert_kernel,
        grid_spec=grid_spec,
        out_shape=jax.ShapeDtypeStruct((MOE_ROWS, ROW_SUB, ROW_LANE), BF16),
        compiler_params=_params(1),
        name="experts",
    )(sched["n_used"], sched["tile_first"], sched["tile_ord"], sched["ord_expert"], sched["n_ord"],
      xs, w_e1, w_e3, w_e2)


def _combine_kernel(pos_ref, ys_ref, x2_ref, route_ref, g_ref, yp_ref, ysm_ref, gbuf, sem):
    i = pl.program_id(0)
    base = i * ROW_TILE

    def row_copy(r, k, src_row):
        return pltpu.make_async_copy(ys_ref.at[src_row], gbuf.at[k, r], sem.at[k])

    def start(r, carry):
        for k in range(MOE_TOPK):
            row_copy(r, k, pos_ref[k, base + r]).start()
        return carry

    def wait(r, carry):
        for k in range(MOE_TOPK):
            row_copy(r, k, 0).wait()
        return carry

    lax.fori_loop(0, ROW_TILE, start, 0, unroll=8)
    lax.fori_loop(0, ROW_TILE, wait, 0, unroll=8)

    route = route_ref[...]
    lane = lax.broadcasted_iota(jnp.int32, route.shape, 1)
    w0 = jnp.sum(jnp.where(lane == 4, route, 0.0), axis=-1, keepdims=True)
    w1 = jnp.sum(jnp.where(lane == 5, route, 0.0), axis=-1, keepdims=True)
    g0 = _from_row_tiles(gbuf[0]).astype(F32)
    g1 = _from_row_tiles(gbuf[1]).astype(F32)
    x3 = x2_ref[...] + (g0 * w0 + g1 * w1)
    y = _rms(x3, g_ref[...])

    @pl.when(i < NP_TILES)
    def _():
        yp_ref[...] = y

    @pl.when(i == NP_TILES)
    def _():
        ysm_ref[...] = y


def _combine(pos, ys, x2, route, g_f):
    grid_spec = pltpu.PrefetchScalarGridSpec(
        num_scalar_prefetch=1,
        grid=(N_TILES,),
        in_specs=[
            pl.BlockSpec(memory_space=pl.ANY),
            pl.BlockSpec((ROW_TILE, D_MODEL), lambda i, pos: (i, 0)),
            pl.BlockSpec((ROW_TILE, ROUTE_LANES), lambda i, pos: (i, 0)),
            pl.BlockSpec((1, D_MODEL), lambda i, pos: (0, 0)),
        ],
        out_specs=[
            pl.BlockSpec((ROW_TILE, D_MODEL), lambda i, pos: (jnp.minimum(i, NP_TILES - 1), 0)),
            pl.BlockSpec((ROW_TILE, D_MODEL), lambda i, pos: (0, 0)),
        ],
        scratch_shapes=[pltpu.VMEM((MOE_TOPK, ROW_TILE, ROW_SUB, ROW_LANE), BF16),
                        pltpu.SemaphoreType.DMA((MOE_TOPK,))],
    )
    return pl.pallas_call(
        _combine_kernel,
        grid_spec=grid_spec,
        out_shape=[
            jax.ShapeDtypeStruct((N_P, D_MODEL), F32),
            jax.ShapeDtypeStruct((N_S, D_MODEL), F32),
        ],
        compiler_params=_params(1),
        name="combine_norm",
    )(pos, ys, x2, route, g_f)


def _expert_schedule(counts):
    i32 = jnp.int32
    cnt = counts[0, MOE_GROUPS:MOE_GROUPS + MOE_EXPERTS].astype(i32)
    tiles = (cnt + MOE_TM - 1) // MOE_TM
    tile_end = jnp.cumsum(tiles)
    tile_start = tile_end - tiles
    n_used = tile_end[-1]
    t = jnp.arange(MOE_NT, dtype=i32)
    tile_expert = jnp.minimum(jnp.sum((t[:, None] >= tile_end[None, :]).astype(i32), axis=1), MOE_EXPERTS - 1)
    used = tiles > 0
    ord_of = jnp.cumsum(used.astype(i32)) - 1
    experts = jnp.arange(MOE_EXPERTS, dtype=i32)
    ord_expert = jnp.zeros((MOE_EXPERTS,), i32).at[jnp.where(used, ord_of, MOE_EXPERTS)].set(experts, mode="drop")
    return {
        "n_used": n_used.reshape(1).astype(i32),
        "tile_first": ((t == tile_start[tile_expert]) & (t < n_used)).astype(i32),
        "tile_ord": ord_of[tile_expert].astype(i32),
        "ord_expert": ord_expert,
        "n_ord": jnp.sum(used.astype(i32)).reshape(1),
        "zero_row": (jnp.maximum(tile_end - 1, 0) * MOE_TM).astype(i32),
        "zero_on": used.astype(i32),
    }


def kernel(x_prompt, x_sample, mem_prompt, state_ret, cache_mem_k, cache_mem_v, norm_mix, w_in, ret_gn,
           sg_ln_g, sg_ln_b, sg_ws, sg_bs, w_a_out, w_b_out, w_o, norm_xa, w_cq, w_ck, w_cv, w_co, norm_moe,
           w_rg, b_rg, w_re, b_re, w_e1, w_e3, w_e2, norm_f):
    xp = x_prompt.reshape(N_P, D_MODEL)
    xs = x_sample.reshape(N_S, D_MODEL)

    h = _norm_rows(xp, xs, norm_mix)
    w = w_in[0]
    qk = _inproj("rope", 0, 2 * RET_QK, h, w, _rope_tables())
    v = _inproj("copy", 2 * RET_QK, RET_V, h, w)
    gs = _inproj("silu", 2 * RET_QK + RET_V, RET_V, h, w)
    uv = _inproj("gelu", 2 * RET_QK + 2 * RET_V, 2 * SG_WIDTH, h, w)
    gab = _inproj("sigmoid", 2 * RET_QK + 2 * RET_V + 2 * SG_WIDTH, 2 * D_MODEL, h, w)

    a_p, ret_p = _ret_prompt(qk, v, gs, ret_gn, RET_CHUNK)
    a_s, ret_s = _ret_sample(qk, v, gs, ret_gn, state_ret)

    b_all, sgv = _sgate(uv, sg_ln_g, sg_ln_b, sg_ws[0], sg_bs[0])

    x1, qx = _merge(a_p, a_s, b_all, gab, xp, xs, w_a_out[0].astype(BF16), w_b_out[0].astype(BF16),
                    w_o[0].astype(BF16), norm_xa, w_cq[0].astype(BF16))

    mk, mv, mkb, mvb = _memkv(mem_prompt.reshape(BATCH * MEM_LEN, D_MODEL), w_ck[0], w_cv[0])
    o_p = _xattn_prompt(qx, mkb, mvb)
    ck = cache_mem_k.reshape(DEC_BATCH, MEM_LEN * XA_HEADS, XA_DH)
    cv = cache_mem_v.reshape(DEC_BATCH, MEM_LEN * XA_HEADS, XA_DH)
    o_s = _xattn_sample(qx, ck, cv)

    w_r = jnp.zeros((D_MODEL, ROUTE_LANES), F32)
    w_r = w_r.at[:, :MOE_GROUPS].set(w_rg[0]).at[:, MOE_GROUPS:MOE_GROUPS + MOE_EXPERTS].set(w_re[0])
    b_r = jnp.zeros((1, ROUTE_LANES), F32)
    b_r = b_r.at[0, :MOE_GROUPS].set(b_rg[0]).at[0, MOE_GROUPS:MOE_GROUPS + MOE_EXPERTS].set(b_re[0])
    x2, hmw, route, counts = _route(o_p, o_s, x1, w_co[0].astype(BF16), norm_moe, w_r.astype(BF16), b_r)

    pos_t = _positions(route, counts)
    sched = _expert_schedule(counts)
    xs_sorted = _dispatch(pos_t, sched["zero_row"], sched["zero_on"], sched["n_used"], hmw)
    ys = _experts(sched, xs_sorted, w_e1[0], w_e3[0], w_e2[0])
    y_p, y_s = _combine(pos_t, ys, x2, route, norm_f.reshape(1, D_MODEL))

    return (y_p.reshape(BATCH, SEQ, D_MODEL),
            y_s.reshape(DEC_BATCH, DEC_SEQ, D_MODEL),
            ret_p,
            mk.reshape(1, BATCH, MEM_LEN, XA_HEADS, XA_DH),
            mv.reshape(1, BATCH, MEM_LEN, XA_HEADS, XA_DH),
            ret_s,
            sgv.reshape(1, DEC_BATCH, DEC_SEQ, SG_WIDTH))
```

```python
import functools

import jax
import jax.numpy as jnp
from jax import lax
from jax.experimental import pallas as pl
from jax.experimental.pallas import tpu as pltpu

F32 = jnp.float32
BF16 = jnp.bfloat16

D_MODEL = 2048
BATCH = 4
SEQ = 2048
DEC_BATCH = 128
DEC_SEQ = 4
PAST_LEN = 16384
RET_HEADS = 8
RET_DK = 128
RET_DV = 256
RET_CHUNK = 128
ROPE_BASE = 10000.0
RET_QK = RET_HEADS * RET_DK
RET_V = RET_HEADS * RET_DV
SG_GROUPS = 4
SG_WIDTH = 2048
SG_CHUNK = 128
MEM_LEN = 256
XA_HEADS = 4
XA_DH = 128
XA_W = XA_HEADS * XA_DH
MOE_GROUPS = 4
MOE_PER_GROUP = 8
MOE_EXPERTS = MOE_GROUPS * MOE_PER_GROUP
MOE_TOPK = 2
MOE_FF = 512
EPS = 1e-6
IN_WIDTH = 2 * RET_QK + 2 * RET_V + 2 * SG_WIDTH + 2 * D_MODEL

N_P = BATCH * SEQ
N_S = DEC_BATCH * DEC_SEQ
N_ALL = N_P + N_S
ROW_TILE = 512
N_TILES = N_ALL // ROW_TILE
NP_TILES = N_P // ROW_TILE
MERGE_TILE = 256
MERGE_P_TILES = N_P // MERGE_TILE

IN_TM = 1024
IN_TM_ROPE = 1024
IN_TN = 1024

PROMPT_CHUNK = 256
LOG_DEC_SEQ = 2
LOG_SG_CHUNK = 7
SAMPLE_BB = 8
SAMPLE_ROWS = SAMPLE_BB * DEC_SEQ
SAMPLE_STEPS = DEC_BATCH // SAMPLE_BB

MOE_TM = 256
MOE_NT = (N_ALL * MOE_TOPK + MOE_EXPERTS * (MOE_TM - 1) + MOE_TM - 1) // MOE_TM
MOE_ROWS = MOE_NT * MOE_TM
ROUTE_LANES = 128
ROW_SUB, ROW_LANE = 16, 128
W_SLOTS = 3

VMEM_LIMIT = 56 * 1024 * 1024


def _params(n_axes, vmem=VMEM_LIMIT):
    return pltpu.CompilerParams(dimension_semantics=("arbitrary",) * n_axes,
                                vmem_limit_bytes=vmem)


def _rms(x, g):
    ms = jnp.mean(x * x, axis=-1, keepdims=True)
    return (x * lax.rsqrt(ms + EPS)) * g


def _dot(a, b):
    return jnp.dot(a, b, preferred_element_type=F32)


def _to_row_tiles(x):
    return x.astype(BF16).reshape(x.shape[0], ROW_SUB, ROW_LANE)


def _from_row_tiles(t):
    return t.reshape(t.shape[0], D_MODEL)


def _dot_nt(a, b):
    return lax.dot_general(a, b, (((1,), (1,)), ((), ())), preferred_element_type=F32)


def _dot_tn(a, b):
    return lax.dot_general(a, b, (((0,), (0,)), ((), ())), preferred_element_type=F32)


def _norm_kernel(xp_ref, xs_ref, g_ref, h_ref):
    i = pl.program_id(0)

    @pl.when(i < NP_TILES)
    def _():
        h_ref[...] = _rms(xp_ref[...], g_ref[...]).astype(BF16)

    @pl.when(i == NP_TILES)
    def _():
        h_ref[...] = _rms(xs_ref[...], g_ref[...]).astype(BF16)


def _norm_rows(xp, xs, g):
    return pl.pallas_call(
        _norm_kernel,
        grid=(N_TILES,),
        in_specs=[
            pl.BlockSpec((ROW_TILE, D_MODEL), lambda i: (jnp.minimum(i, NP_TILES - 1), 0)),
            pl.BlockSpec((ROW_TILE, D_MODEL), lambda i: (0, 0)),
            pl.BlockSpec((1, D_MODEL), lambda i: (0, 0)),
        ],
        out_specs=pl.BlockSpec((ROW_TILE, D_MODEL), lambda i: (i, 0)),
        out_shape=jax.ShapeDtypeStruct((N_ALL, D_MODEL), BF16),
        compiler_params=_params(1),
        name="norm_rows",
    )(xp, xs, g)


def _inproj_kernel(kind, tm, h_ref, w_ref, *rest):
    if kind == "rope":
        cos_ref, sin_ref, z_ref, wb_ref = rest
    else:
        z_ref, wb_ref = rest
    j = pl.program_id(0)
    i = pl.program_id(1)
    last = N_P // tm

    @pl.when(i == 0)
    def _():
        wb_ref[...] = w_ref[...].astype(BF16)

    def tile(rows):
        acc = _dot(h_ref[0:rows, :], wb_ref[...])
        if kind == "rope":
            scale = jnp.where(j == 1, RET_DK ** -0.5, 1.0).astype(F32)
            c = cos_ref[0:rows, :]
            s = sin_ref[0:rows, :]
            for hb in range(IN_TN // RET_DK):
                cols = slice(hb * RET_DK, (hb + 1) * RET_DK)
                a = acc[:, cols]
                r = pltpu.roll(a, RET_DK // 2, axis=1)
                z_ref[0:rows, cols] = ((a * c + r * s) * scale).astype(BF16)
        elif kind == "copy":
            z_ref[0:rows, :] = acc.astype(BF16)
        elif kind == "silu":
            z_ref[0:rows, :] = jax.nn.silu(acc).astype(BF16)
        elif kind == "gelu":
            z_ref[0:rows, :] = jax.nn.gelu(acc).astype(BF16)
        else:
            z_ref[0:rows, :] = jax.nn.sigmoid(acc).astype(BF16)

    @pl.when(i < last)
    def _():
        tile(tm)

    @pl.when(i == last)
    def _():
        tile(N_S)


def _inproj(kind, tm, col0, width, h, w_in, tables=()):
    last = N_P // tm
    tab_idx = lambda j, i: (jnp.where(i < last, i % (SEQ // tm), SEQ // tm), 0)
    j0 = col0 // IN_TN
    return pl.pallas_call(
        functools.partial(_inproj_kernel, kind, tm),
        grid=(width // IN_TN, last + 1),
        in_specs=[
            pl.BlockSpec((tm, D_MODEL), lambda j, i: (i, 0)),
            pl.BlockSpec((D_MODEL, IN_TN), lambda j, i: (0, j0 + j)),
        ] + [pl.BlockSpec((tm, RET_DK), tab_idx) for _ in tables],
        out_specs=pl.BlockSpec((tm, IN_TN), lambda j, i: (i, j)),
        out_shape=jax.ShapeDtypeStruct((N_ALL, width), BF16),
        scratch_shapes=[pltpu.VMEM((D_MODEL, IN_TN), BF16)],
        compiler_params=_params(2),
        name="in_proj_" + kind,
    )(h, w_in, *tables)


def _rope_tables(tm):
    half = RET_DK // 2
    inv = ROPE_BASE ** (-jnp.arange(half, dtype=F32) / half)

    def tab(pos):
        ang = pos.astype(F32)[:, None] * inv[None, :]
        c, s = jnp.cos(ang), jnp.sin(ang)
        return jnp.concatenate([c, c], -1), jnp.concatenate([-s, s], -1)

    cp, sp = tab(jnp.arange(SEQ, dtype=jnp.int32))
    cs, ss = tab(PAST_LEN + jnp.arange(DEC_SEQ, dtype=jnp.int32))
    cs = jnp.tile(cs, (DEC_BATCH, 1))
    ss = jnp.tile(ss, (DEC_BATCH, 1))
    pad = jnp.zeros((tm - N_S, RET_DK), F32)
    return jnp.concatenate([cp, cs, pad], 0), jnp.concatenate([sp, ss, pad], 0)


def _decay_tables(chunk):
    lg = jnp.log1p(-jnp.power(2.0, -5.0 - jnp.arange(RET_HEADS, dtype=F32)))
    idx = jnp.arange(chunk, dtype=F32)
    rel = idx[:, None] - idx[None, :]
    dmask = jnp.where(rel >= 0, jnp.exp(lg[:, None, None] * jnp.maximum(rel, 0.0)), 0.0).astype(F32)
    xi = jnp.exp(lg[:, None] * (idx[None, :] + 1.0)).astype(F32)
    zeta = jnp.exp(lg[:, None] * (chunk - 1.0 - idx[None, :])).astype(F32)
    gc = jnp.exp(lg * chunk).astype(F32)
    return dmask, xi, zeta, gc


def _head_norm_gate(o, gn, gate):
    mu = jnp.mean(o, axis=-1, keepdims=True)
    d = o - mu
    var = jnp.mean(d * d, axis=-1, keepdims=True)
    y = (d * lax.rsqrt(var + EPS)) * gn
    return (gate.astype(F32) * y).astype(BF16)


def _ret_prompt_kernel(gc_ref, q_ref, k_ref, v_ref, gs_ref, gn_ref, dm_ref, xi_ref, zt_ref,
                       a_ref, sfin_ref, s_ref):
    c = pl.program_id(1)

    @pl.when(c == 0)
    def _():
        s_ref[...] = jnp.zeros_like(s_ref)

    for h in range(RET_HEADS):
        kc = slice(h * RET_DK, (h + 1) * RET_DK)
        vc = slice(h * RET_DV, (h + 1) * RET_DV)
        qh = q_ref[:, kc]
        kh = k_ref[:, kc]
        vh = v_ref[:, vc]
        inner = _dot_nt(qh, kh) * dm_ref[h]
        o = _dot(inner.astype(BF16), vh)
        s_old = s_ref[h]
        xi = xi_ref[h]
        o = o + _dot(qh, s_old.astype(BF16)) * jnp.concatenate([xi, xi], axis=1)
        kz = (kh.astype(F32) * zt_ref[h]).astype(BF16)
        s_ref[h] = gc_ref[h] * s_old + _dot_tn(kz, vh)
        a_ref[:, vc] = _head_norm_gate(o, gn_ref[:, vc], gs_ref[:, vc])

    @pl.when(c == pl.num_programs(1) - 1)
    def _():
        sfin_ref[0, 0] = s_ref[...]


def _ret_prompt(qk, v, gs, ret_gn, chunk):
    dmask, xi, zeta, gc = _decay_tables(chunk)
    xi_b = jnp.broadcast_to(xi[:, :, None], (RET_HEADS, chunk, RET_DK))
    zeta_b = jnp.broadcast_to(zeta[:, :, None], (RET_HEADS, chunk, RET_DK))
    n_chunks = SEQ // chunk
    row = lambda b, c: b * n_chunks + c
    const3 = lambda b, c: (0, 0, 0)
    return pl.pallas_call(
        _ret_prompt_kernel,
        grid=(BATCH, n_chunks),
        in_specs=[
            pl.BlockSpec(memory_space=pltpu.SMEM),
            pl.BlockSpec((chunk, RET_QK), lambda b, c: (row(b, c), 0)),
            pl.BlockSpec((chunk, RET_QK), lambda b, c: (row(b, c), 1)),
            pl.BlockSpec((chunk, RET_V), lambda b, c: (row(b, c), 0)),
            pl.BlockSpec((chunk, RET_V), lambda b, c: (row(b, c), 0)),
            pl.BlockSpec((1, RET_V), lambda b, c: (0, 0)),
            pl.BlockSpec((RET_HEADS, chunk, chunk), const3),
            pl.BlockSpec((RET_HEADS, chunk, RET_DK), const3),
            pl.BlockSpec((RET_HEADS, chunk, RET_DK), const3),
        ],
        out_specs=[
            pl.BlockSpec((chunk, RET_V), lambda b, c: (row(b, c), 0)),
            pl.BlockSpec((1, 1, RET_HEADS, RET_DK, RET_DV), lambda b, c: (0, b, 0, 0, 0)),
        ],
        out_shape=[
            jax.ShapeDtypeStruct((N_P, RET_V), BF16),
            jax.ShapeDtypeStruct((1, BATCH, RET_HEADS, RET_DK, RET_DV), F32),
        ],
        scratch_shapes=[pltpu.VMEM((RET_HEADS, RET_DK, RET_DV), F32)],
        compiler_params=_params(2),
        name="ret_prompt",
    )(gc, qk, qk, v, gs, ret_gn, dmask, xi_b, zeta_b)


def _ret_sample_kernel(gc_ref, q_ref, k_ref, v_ref, gs_ref, gn_ref, dm_ref, xi_ref, zt_ref, s0_ref,
                       a_ref, s1_ref):
    rows_k = lax.broadcasted_iota(jnp.int32, (SAMPLE_ROWS, RET_DK), 0) >> LOG_DEC_SEQ
    rows_v = lax.broadcasted_iota(jnp.int32, (SAMPLE_ROWS, RET_DV), 0) >> LOG_DEC_SEQ
    for h in range(RET_HEADS):
        kc = slice(h * RET_DK, (h + 1) * RET_DK)
        vc = slice(h * RET_DV, (h + 1) * RET_DV)
        qh = q_ref[:, kc]
        kh = k_ref[:, kc]
        vh = v_ref[:, vc]
        inner = _dot_nt(qh, kh) * dm_ref[h]
        o = _dot(inner.astype(BF16), vh)
        xi = xi_ref[h]
        xi2 = jnp.concatenate([xi, xi], axis=1)
        kz = kh.astype(F32) * zt_ref[h]
        gch = gc_ref[h]
        for b in range(SAMPLE_BB):
            s_old = s0_ref[0, b, h]
            cross = _dot(qh, s_old.astype(BF16)) * xi2
            o = o + jnp.where(rows_v == b, cross, 0.0)
            kz_b = jnp.where(rows_k == b, kz, 0.0).astype(BF16)
            s1_ref[0, b, h] = gch * s_old + _dot_tn(kz_b, vh)
        a_ref[:, vc] = _head_norm_gate(o, gn_ref[:, vc], gs_ref[:, vc])


def _ret_sample(qk, v, gs, ret_gn, state):
    dmask, xi, zeta, gc = _decay_tables(DEC_SEQ)
    eye = jnp.eye(SAMPLE_BB, dtype=F32)
    dm_big = jax.vmap(lambda m: jnp.kron(eye, m))(dmask)
    xi_b = jnp.broadcast_to(jnp.tile(xi, (1, SAMPLE_BB))[:, :, None], (RET_HEADS, SAMPLE_ROWS, RET_DK))
    zeta_b = jnp.broadcast_to(jnp.tile(zeta, (1, SAMPLE_BB))[:, :, None], (RET_HEADS, SAMPLE_ROWS, RET_DK))
    base = N_P // SAMPLE_ROWS
    const3 = lambda g: (0, 0, 0)
    st_spec = pl.BlockSpec((1, SAMPLE_BB, RET_HEADS, RET_DK, RET_DV), lambda g: (0, g, 0, 0, 0))
    return pl.pallas_call(
        _ret_sample_kernel,
        grid=(SAMPLE_STEPS,),
        in_specs=[
            pl.BlockSpec(memory_space=pltpu.SMEM),
            pl.BlockSpec((SAMPLE_ROWS, RET_QK), lambda g: (base + g, 0)),
            pl.BlockSpec((SAMPLE_ROWS, RET_QK), lambda g: (base + g, 1)),
            pl.BlockSpec((SAMPLE_ROWS, RET_V), lambda g: (base + g, 0)),
            pl.BlockSpec((SAMPLE_ROWS, RET_V), lambda g: (base + g, 0)),
            pl.BlockSpec((1, RET_V), lambda g: (0, 0)),
            pl.BlockSpec((RET_HEADS, SAMPLE_ROWS, SAMPLE_ROWS), const3),
            pl.BlockSpec((RET_HEADS, SAMPLE_ROWS, RET_DK), const3),
            pl.BlockSpec((RET_HEADS, SAMPLE_ROWS, RET_DK), const3),
            st_spec,
        ],
        out_specs=[
            pl.BlockSpec((SAMPLE_ROWS, RET_V), lambda g: (g, 0)),
            st_spec,
        ],
        out_shape=[
            jax.ShapeDtypeStruct((N_S, RET_V), BF16),
            jax.ShapeDtypeStruct((1, DEC_BATCH, RET_HEADS, RET_DK, RET_DV), F32),
        ],
        compiler_params=_params(1),
        name="ret_sample",
    )(gc, qk, qk, v, gs, ret_gn, dm_big, xi_b, zeta_b, state)


def _sgate_kernel(u_ref, v_ref, lg_ref, lb_ref, wp_ref, bp_ref, ws_ref, bs_ref, b_ref, sgv_ref, vln_ref):
    i = pl.program_id(0)
    v = v_ref[...].astype(F32)
    mu = jnp.mean(v, axis=-1, keepdims=True)
    d = v - mu
    var = jnp.mean(d * d, axis=-1, keepdims=True)
    vln_ref[...] = (d * lax.rsqrt(var + EPS)) * lg_ref[...] + lb_ref[...]
    gw = SG_WIDTH // SG_GROUPS
    lane_reps = gw // 128

    @pl.when(i < NP_TILES)
    def _():
        r = lax.broadcasted_iota(jnp.int32, (SG_CHUNK, SG_CHUNK), 0)
        c = lax.broadcasted_iota(jnp.int32, (SG_CHUNK, SG_CHUNK), 1)
        for g in range(SG_GROUPS):
            cols = slice(g * gw, (g + 1) * gw)
            w = jnp.where(c <= r, wp_ref[g], 0.0).astype(BF16)
            bias = jnp.concatenate([bp_ref[g]] * lane_reps, axis=1)
            for ch in range(ROW_TILE // SG_CHUNK):
                rows = slice(ch * SG_CHUNK, (ch + 1) * SG_CHUNK)
                mixed = _dot(w, vln_ref[rows, cols].astype(BF16)) + bias
                b_ref[rows, cols] = (u_ref[rows, cols].astype(F32) * mixed).astype(BF16)

    @pl.when(i == NP_TILES)
    def _():
        sgv_ref[...] = vln_ref[...]
        r = lax.broadcasted_iota(jnp.int32, (ROW_TILE, ROW_TILE), 0)
        c = lax.broadcasted_iota(jnp.int32, (ROW_TILE, ROW_TILE), 1)
        keep = ((r >> LOG_DEC_SEQ) == (c >> LOG_DEC_SEQ)) & (c <= r)
        for g in range(SG_GROUPS):
            cols = slice(g * gw, (g + 1) * gw)
            w_rows = jnp.concatenate([ws_ref[g]] * (ROW_TILE // 8), axis=0)
            w_full = jnp.concatenate([w_rows] * (ROW_TILE // 128), axis=1)
            w = jnp.where(keep, w_full, 0.0).astype(BF16)
            b_rows = jnp.concatenate([bs_ref[g]] * (ROW_TILE // 8), axis=0)
            bias = jnp.concatenate([b_rows] * lane_reps, axis=1)
            mixed = _dot(w, vln_ref[:, cols].astype(BF16)) + bias
            b_ref[:, cols] = (u_ref[:, cols].astype(F32) * mixed).astype(BF16)


def _sgate(uv, ln_g, ln_b, sg_ws, sg_bs):
    b_p = jnp.broadcast_to(sg_bs[:, :, None], (SG_GROUPS, SG_CHUNK, 128))
    w_s = jnp.tile(sg_ws[:, :DEC_SEQ, :DEC_SEQ], (1, 8 // DEC_SEQ, 128 // DEC_SEQ))
    b_s = jnp.broadcast_to(jnp.tile(sg_bs[:, :DEC_SEQ], (1, 8 // DEC_SEQ))[:, :, None], (SG_GROUPS, 8, 128))
    const3 = lambda i: (0, 0, 0)
    return pl.pallas_call(
        _sgate_kernel,
        grid=(N_TILES,),
        in_specs=[
            pl.BlockSpec((ROW_TILE, SG_WIDTH), lambda i: (i, 0)),
            pl.BlockSpec((ROW_TILE, SG_WIDTH), lambda i: (i, 1)),
            pl.BlockSpec((1, SG_WIDTH), lambda i: (0, 0)),
            pl.BlockSpec((1, SG_WIDTH), lambda i: (0, 0)),
            pl.BlockSpec((SG_GROUPS, SG_CHUNK, SG_CHUNK), const3),
            pl.BlockSpec((SG_GROUPS, SG_CHUNK, 128), const3),
            pl.BlockSpec((SG_GROUPS, 8, 128), const3),
            pl.BlockSpec((SG_GROUPS, 8, 128), const3),
        ],
        out_specs=[
            pl.BlockSpec((ROW_TILE, SG_WIDTH), lambda i: (i, 0)),
            pl.BlockSpec((N_S, SG_WIDTH), lambda i: (0, 0)),
        ],
        out_shape=[
            jax.ShapeDtypeStruct((N_ALL, SG_WIDTH), BF16),
            jax.ShapeDtypeStruct((N_S, SG_WIDTH), F32),
        ],
        scratch_shapes=[pltpu.VMEM((ROW_TILE, SG_WIDTH), F32)],
        compiler_params=_params(1),
        name="spatial_gate",
    )(uv, uv, ln_g, ln_b, sg_ws, b_p, w_s, b_s)


def _merge_kernel(ap_ref, as_ref, b_ref, ga_ref, gb_ref, xp_ref, xs_ref, wa_ref, wb_ref, wo_ref,
                  g_ref, wq_ref, x1_ref, q_ref):
    i = pl.program_id(0)

    def run(a_in, x):
        a = _dot(a_in, wa_ref[...])
        b = _dot(b_ref[...], wb_ref[...])
        merged = ga_ref[...].astype(F32) * a + gb_ref[...].astype(F32) * b
        x1 = x + _dot(merged.astype(BF16), wo_ref[...])
        x1_ref[...] = x1
        q_ref[...] = _dot(_rms(x1, g_ref[...]).astype(BF16), wq_ref[...]).astype(BF16)

    @pl.when(i < MERGE_P_TILES)
    def _():
        run(ap_ref[...], xp_ref[...])

    @pl.when(i >= MERGE_P_TILES)
    def _():
        run(as_ref[...], xs_ref[...])


def _resident(shape):
    return pl.BlockSpec(shape, lambda i: (0,) * len(shape), pipeline_mode=pl.Buffered(1))


def _merge(a_p, a_s, b_all, gab, xp, xs, wa, wb, wo, g_xa, wq):
    tm = MERGE_TILE
    prompt_tile = lambda i: (jnp.minimum(i, MERGE_P_TILES - 1), 0)
    sample_tile = lambda i: (jnp.maximum(i - MERGE_P_TILES, 0), 0)
    return pl.pallas_call(
        _merge_kernel,
        grid=(N_ALL // tm,),
        in_specs=[
            pl.BlockSpec((tm, RET_V), prompt_tile),
            pl.BlockSpec((tm, RET_V), sample_tile),
            pl.BlockSpec((tm, SG_WIDTH), lambda i: (i, 0)),
            pl.BlockSpec((tm, D_MODEL), lambda i: (i, 0)),
            pl.BlockSpec((tm, D_MODEL), lambda i: (i, 1)),
            pl.BlockSpec((tm, D_MODEL), prompt_tile),
            pl.BlockSpec((tm, D_MODEL), sample_tile),
            _resident((RET_V, D_MODEL)),
            _resident((SG_WIDTH, D_MODEL)),
            _resident((D_MODEL, D_MODEL)),
            pl.BlockSpec((1, D_MODEL), lambda i: (0, 0)),
            _resident((D_MODEL, XA_W)),
        ],
        out_specs=[
            pl.BlockSpec((tm, D_MODEL), lambda i: (i, 0)),
            pl.BlockSpec((tm, XA_W), lambda i: (i, 0)),
        ],
        out_shape=[
            jax.ShapeDtypeStruct((N_ALL, D_MODEL), F32),
            jax.ShapeDtypeStruct((N_ALL, XA_W), BF16),
        ],
        compiler_params=_params(1),
        name="merge_proj",
    )(a_p, a_s, b_all, gab, gab, xp, xs, wa, wb, wo, g_xa, wq)


def _memkv_kernel(m_ref, wk_ref, wv_ref, k_ref, v_ref, kb_ref, vb_ref):
    m = m_ref[...].astype(BF16)
    k = _dot(m, wk_ref[...].astype(BF16))
    v = _dot(m, wv_ref[...].astype(BF16))
    k_ref[...] = k
    v_ref[...] = v
    kb_ref[...] = k.astype(BF16)
    vb_ref[...] = v.astype(BF16)


def _memkv(mem, w_ck, w_cv):
    rows = BATCH * MEM_LEN
    spec = pl.BlockSpec((MEM_LEN, XA_W), lambda b: (b, 0))
    wspec = pl.BlockSpec((D_MODEL, XA_W), lambda b: (0, 0))
    return pl.pallas_call(
        _memkv_kernel,
        grid=(BATCH,),
        in_specs=[pl.BlockSpec((MEM_LEN, D_MODEL), lambda b: (b, 0)), wspec, wspec],
        out_specs=[spec, spec, spec, spec],
        out_shape=[jax.ShapeDtypeStruct((rows, XA_W), F32)] * 2 + [jax.ShapeDtypeStruct((rows, XA_W), BF16)] * 2,
        compiler_params=_params(1),
        name="mem_kv",
    )(mem, w_ck, w_cv)


def _softmax_rows(s):
    m = jnp.max(s, axis=-1, keepdims=True)
    e = jnp.exp(s - m)
    return e / jnp.sum(e, axis=-1, keepdims=True)


def _xattn_prompt_kernel(q_ref, k_ref, v_ref, o_ref):
    for h in range(XA_HEADS):
        cols = slice(h * XA_DH, (h + 1) * XA_DH)
        s = _dot_nt(q_ref[:, cols], k_ref[:, cols]) * (XA_DH ** -0.5)
        p = _softmax_rows(s)
        o_ref[:, cols] = _dot(p.astype(BF16), v_ref[:, cols]).astype(BF16)


def _xattn_prompt(q, kb, vb):
    per_b = SEQ // ROW_TILE
    kv = pl.BlockSpec((MEM_LEN, XA_W), lambda i: (i // per_b, 0))
    return pl.pallas_call(
        _xattn_prompt_kernel,
        grid=(NP_TILES,),
        in_specs=[pl.BlockSpec((ROW_TILE, XA_W), lambda i: (i, 0)), kv, kv],
        out_specs=pl.BlockSpec((ROW_TILE, XA_W), lambda i: (i, 0)),
        out_shape=jax.ShapeDtypeStruct((N_P, XA_W), BF16),
        compiler_params=_params(1),
        name="xattn_prompt",
    )(q, kb, vb)


def _xattn_sample_kernel(q_ref, k_ref, v_ref, o_ref):
    n_q = XA_HEADS * DEC_SEQ
    n_kv = MEM_LEN * XA_HEADS
    qf = q_ref[...].astype(F32)
    r = lax.broadcasted_iota(jnp.int32, (n_q, n_kv), 0)
    c = lax.broadcasted_iota(jnp.int32, (n_q, n_kv), 1)
    head_ok = (r >> LOG_DEC_SEQ) == (c & (XA_HEADS - 1))
    for b in range(SAMPLE_BB):
        rows = slice(b * DEC_SEQ, (b + 1) * DEC_SEQ)
        qb = jnp.concatenate([qf[rows, h * XA_DH:(h + 1) * XA_DH] for h in range(XA_HEADS)], axis=0)
        s = _dot_nt(qb.astype(BF16), k_ref[b].astype(BF16)) * (XA_DH ** -0.5)
        p = _softmax_rows(jnp.where(head_ok, s, -1e30))
        o = _dot(p.astype(BF16), v_ref[b].astype(BF16))
        for h in range(XA_HEADS):
            o_ref[h, rows, :] = o[h * DEC_SEQ:(h + 1) * DEC_SEQ, :]


def _xattn_sample(q, ck, cv):
    base = N_P // SAMPLE_ROWS
    kv = pl.BlockSpec((SAMPLE_BB, MEM_LEN * XA_HEADS, XA_DH), lambda g: (g, 0, 0))
    return pl.pallas_call(
        _xattn_sample_kernel,
        grid=(SAMPLE_STEPS,),
        in_specs=[pl.BlockSpec((SAMPLE_ROWS, XA_W), lambda g: (base + g, 0)), kv, kv],
        out_specs=pl.BlockSpec((XA_HEADS, SAMPLE_ROWS, XA_DH), lambda g: (0, g, 0)),
        out_shape=jax.ShapeDtypeStruct((XA_HEADS, N_S, XA_DH), F32),
        compiler_params=_params(1),
        name="xattn_sample",
    )(q, ck, cv)


def _route_kernel(op_ref, os_ref, x1_ref, wo_ref, g_ref, wr_ref, br_ref,
                  x2_ref, hm_ref, route_ref, cnt_ref, run_ref):
    i = pl.program_id(0)

    @pl.when(i == 0)
    def _():
        run_ref[...] = jnp.zeros_like(run_ref)

    def run(o):
        x2 = x1_ref[...] + _dot(o, wo_ref[...])
        x2_ref[...] = x2
        hm = _rms(x2, g_ref[...])
        hm_ref[...] = _to_row_tiles(hm)
        logits = _dot(hm.astype(BF16), wr_ref[...]) + br_ref[...]
        lane = lax.broadcasted_iota(jnp.int32, logits.shape, 1)
        neg = jnp.float32(-jnp.inf)
        big = jnp.int32(1 << 20)
        is_g = lane < MOE_GROUPS
        gl = jnp.where(is_g, logits, neg)
        gmax = jnp.max(gl, axis=-1, keepdims=True)
        g_sel = jnp.min(jnp.where(gl == gmax, lane, big), axis=-1, keepdims=True)
        g_w = 1.0 / jnp.sum(jnp.where(is_g, jnp.exp(logits - gmax), 0.0), axis=-1, keepdims=True)
        e_lane = lane - MOE_GROUPS
        in_grp = (e_lane >= 0) & (e_lane < MOE_EXPERTS) & ((e_lane >> 3) == g_sel)
        el = jnp.where(in_grp, logits, neg)
        v0 = jnp.max(el, axis=-1, keepdims=True)
        i0 = jnp.min(jnp.where(el == v0, lane, big), axis=-1, keepdims=True)
        el1 = jnp.where(lane == i0, neg, el)
        v1 = jnp.max(el1, axis=-1, keepdims=True)
        i1 = jnp.min(jnp.where(el1 == v1, lane, big), axis=-1, keepdims=True)
        ex = jnp.exp(v1 - v0)
        den = 1.0 + ex
        w0 = (1.0 / den) * g_w
        w1 = (ex / den) * g_w
        a0 = (lane == i0).astype(F32)
        a1 = (lane == i1).astype(F32)
        a = a0 + a1
        rr = lax.broadcasted_iota(jnp.int32, (ROW_TILE, ROW_TILE), 0)
        cc = lax.broadcasted_iota(jnp.int32, (ROW_TILE, ROW_TILE), 1)
        lower = jnp.where(cc < rr, 1.0, 0.0).astype(BF16)
        before = _dot(lower, a.astype(BF16)) + run_ref[...]
        rank0 = jnp.sum(before * a0, axis=-1, keepdims=True)
        rank1 = jnp.sum(before * a1, axis=-1, keepdims=True)
        run_ref[...] += jnp.sum(a, axis=0, keepdims=True)
        e0 = (i0 - MOE_GROUPS).astype(F32)
        e1 = (i1 - MOE_GROUPS).astype(F32)
        route = jnp.where(lane == 0, e0, 0.0)
        route = jnp.where(lane == 1, e1, route)
        route = jnp.where(lane == 2, rank0, route)
        route = jnp.where(lane == 3, rank1, route)
        route = jnp.where(lane == 4, w0, route)
        route = jnp.where(lane == 5, w1, route)
        route_ref[...] = route
        cnt_ref[...] = run_ref[...]

    @pl.when(i < NP_TILES)
    def _():
        run(op_ref[...])

    @pl.when(i == NP_TILES)
    def _():
        run(jnp.concatenate([os_ref[h] for h in range(XA_HEADS)], axis=1).astype(BF16))


def _route(o_p, o_s, x1, w_co, g_moe, w_r, b_r):
    return pl.pallas_call(
        _route_kernel,
        grid=(N_TILES,),
        in_specs=[
            pl.BlockSpec((ROW_TILE, XA_W), lambda i: (jnp.minimum(i, NP_TILES - 1), 0)),
            pl.BlockSpec((XA_HEADS, N_S, XA_DH), lambda i: (0, 0, 0)),
            pl.BlockSpec((ROW_TILE, D_MODEL), lambda i: (i, 0)),
            _resident((XA_W, D_MODEL)),
            pl.BlockSpec((1, D_MODEL), lambda i: (0, 0)),
            _resident((D_MODEL, ROUTE_LANES)),
            pl.BlockSpec((1, ROUTE_LANES), lambda i: (0, 0)),
        ],
        out_specs=[
            pl.BlockSpec((ROW_TILE, D_MODEL), lambda i: (i, 0)),
            pl.BlockSpec((ROW_TILE, ROW_SUB, ROW_LANE), lambda i: (i, 0, 0)),
            pl.BlockSpec((ROW_TILE, ROUTE_LANES), lambda i: (i, 0)),
            pl.BlockSpec((1, ROUTE_LANES), lambda i: (0, 0)),
        ],
        out_shape=[
            jax.ShapeDtypeStruct((N_ALL, D_MODEL), F32),
            jax.ShapeDtypeStruct((N_ALL, ROW_SUB, ROW_LANE), BF16),
            jax.ShapeDtypeStruct((N_ALL, ROUTE_LANES), F32),
            jax.ShapeDtypeStruct((1, ROUTE_LANES), F32),
        ],
        scratch_shapes=[pltpu.VMEM((1, ROUTE_LANES), F32)],
        compiler_params=_params(1),
        name="xa_out_route",
    )(o_p, o_s, x1, w_co, g_moe, w_r, b_r)


def _positions_kernel(route_ref, cnt_ref, pos_ref):
    route = route_ref[...]
    lane = lax.broadcasted_iota(jnp.int32, route.shape, 1)
    lane_f = lane.astype(F32)
    tiles = jnp.floor((cnt_ref[...] + (MOE_TM - 1)) * (1.0 / MOE_TM))
    lr = lax.broadcasted_iota(jnp.int32, (ROUTE_LANES, ROUTE_LANES), 0)
    lc = lax.broadcasted_iota(jnp.int32, (ROUTE_LANES, ROUTE_LANES), 1)
    before = jnp.where(lr < lc, 1.0, 0.0).astype(BF16)
    tiles8 = jnp.broadcast_to(tiles, (8, ROUTE_LANES)).astype(BF16)
    start = _dot(tiles8, before)[0:1, :] * MOE_TM

    def col(k):
        return jnp.sum(jnp.where(lane == k, route, 0.0), axis=-1, keepdims=True)

    def first_row(e):
        return jnp.sum(jnp.where(lane_f == e + MOE_GROUPS, start, 0.0), axis=-1, keepdims=True)

    p0 = first_row(col(0)) + col(2)
    p1 = first_row(col(1)) + col(3)
    p = jnp.where(lane == 0, p0, jnp.where(lane == 1, p1, 0.0))
    pos_ref[...] = p.T[0:8, :].astype(jnp.int32)


def _positions(route, counts):
    return pl.pallas_call(
        _positions_kernel,
        grid=(N_TILES,),
        in_specs=[
            pl.BlockSpec((ROW_TILE, ROUTE_LANES), lambda i: (i, 0)),
            pl.BlockSpec((1, ROUTE_LANES), lambda i: (0, 0)),
        ],
        out_specs=pl.BlockSpec((8, ROW_TILE), lambda i: (0, i)),
        out_shape=jax.ShapeDtypeStruct((8, N_ALL), jnp.int32),
        compiler_params=_params(1),
        name="positions",
    )(route, counts)


def _dispatch_kernel(pos_ref, zrow_ref, zon_ref, nu_ref, hm_ref, xs_ref, zbuf, sem, zsem):
    i = pl.program_id(0)

    def zero_tile(row):
        return pltpu.make_async_copy(zbuf, xs_ref.at[pl.ds(pl.multiple_of(row, MOE_TM), MOE_TM)], zsem)

    @pl.when(i == 0)
    def _():
        zbuf[...] = jnp.zeros_like(zbuf)
        for e in range(MOE_EXPERTS):
            @pl.when(zon_ref[e] > 0)
            def _():
                zero_tile(zrow_ref[e]).start()

        def start_tail(t, carry):
            zero_tile(t * MOE_TM).start()
            return carry

        def wait_tail(t, carry):
            zero_tile(t * MOE_TM).wait()
            return carry

        lax.fori_loop(nu_ref[0], MOE_NT, start_tail, 0)
        for e in range(MOE_EXPERTS):
            @pl.when(zon_ref[e] > 0)
            def _():
                zero_tile(zrow_ref[e]).wait()
        lax.fori_loop(nu_ref[0], MOE_NT, wait_tail, 0)

    base = i * ROW_TILE

    def row_copy(r, k, dst_row):
        return pltpu.make_async_copy(hm_ref.at[r], xs_ref.at[dst_row], sem.at[k])

    def start(r, carry):
        for k in range(MOE_TOPK):
            row_copy(r, k, pos_ref[k, base + r]).start(priority=k)
        return carry

    lax.fori_loop(0, ROW_TILE, start, 0, unroll=8)
    for k in range(MOE_TOPK):
        pltpu.make_async_copy(hm_ref, xs_ref.at[pl.ds(0, ROW_TILE)], sem.at[k]).wait()


def _dispatch(pos_t, zero_row, zero_on, n_used, hmw):
    grid_spec = pltpu.PrefetchScalarGridSpec(
        num_scalar_prefetch=4,
        grid=(N_TILES,),
        in_specs=[pl.BlockSpec((ROW_TILE, ROW_SUB, ROW_LANE), lambda i, *_: (i, 0, 0))],
        out_specs=pl.BlockSpec(memory_space=pl.ANY),
        scratch_shapes=[pltpu.VMEM((MOE_TM, ROW_SUB, ROW_LANE), BF16),
                        pltpu.SemaphoreType.DMA((MOE_TOPK,)), pltpu.SemaphoreType.DMA(())],
    )
    return pl.pallas_call(
        _dispatch_kernel,
        grid_spec=grid_spec,
        out_shape=jax.ShapeDtypeStruct((MOE_ROWS, ROW_SUB, ROW_LANE), BF16),
        compiler_params=_params(1),
        name="dispatch",
    )(pos_t, zero_row, zero_on, n_used, hmw)


def _expert_kernel(nu_ref, first_ref, ord_ref, oe_ref, no_ref, x_ref, w1_hbm, w3_hbm, w2_hbm, y_ref,
                   w1b, w3b, w2b, sem):
    i = pl.program_id(0)
    n_used = nu_ref[0]
    n_ord = no_ref[0]

    def weight_copies(k):
        e = oe_ref[k]
        slot = k % W_SLOTS
        return (pltpu.make_async_copy(w1_hbm.at[e], w1b.at[slot], sem.at[0, slot]),
                pltpu.make_async_copy(w3_hbm.at[e], w3b.at[slot], sem.at[1, slot]),
                pltpu.make_async_copy(w2_hbm.at[e], w2b.at[slot], sem.at[2, slot]))

    def start_weights(k):
        for cp in weight_copies(k):
            cp.start(priority=1)

    @pl.when(i == 0)
    def _():
        for k in range(W_SLOTS - 1):
            @pl.when(k < n_ord)
            def _():
                start_weights(k)

    @pl.when(i < n_used)
    def _():
        k = ord_ref[i]

        @pl.when(first_ref[i] > 0)
        def _():
            for cp in weight_copies(k):
                cp.wait()

            @pl.when(k + (W_SLOTS - 1) < n_ord)
            def _():
                start_weights(k + (W_SLOTS - 1))

        slot = k % W_SLOTS
        x = _from_row_tiles(x_ref[...])
        h1 = _dot(x, w1b[slot].astype(BF16))
        h3 = _dot(x, w3b[slot].astype(BF16))
        he = (jax.nn.silu(h1) * h3).astype(BF16)
        y_ref[...] = _to_row_tiles(_dot(he, w2b[slot].astype(BF16)))

    @pl.when(i >= n_used)
    def _():
        y_ref[...] = jnp.zeros_like(y_ref)


def _experts(sched, xs, w_e1, w_e3, w_e2):
    grid_spec = pltpu.PrefetchScalarGridSpec(
        num_scalar_prefetch=5,
        grid=(MOE_NT,),
        in_specs=[
            pl.BlockSpec((MOE_TM, ROW_SUB, ROW_LANE), lambda i, nu, *_: (jnp.minimum(i, nu[0] - 1), 0, 0)),
            pl.BlockSpec(memory_space=pl.ANY),
            pl.BlockSpec(memory_space=pl.ANY),
            pl.BlockSpec(memory_space=pl.ANY),
        ],
        out_specs=pl.BlockSpec((MOE_TM, ROW_SUB, ROW_LANE), lambda i, *_: (i, 0, 0)),
        scratch_shapes=[
            pltpu.VMEM((W_SLOTS, D_MODEL, MOE_FF), F32),
            pltpu.VMEM((W_SLOTS, D_MODEL, MOE_FF), F32),
            pltpu.VMEM((W_SLOTS, MOE_FF, D_MODEL), F32),
            pltpu.SemaphoreType.DMA((3, W_SLOTS)),
        ],
    )
    return pl.pallas_call(
        _expert_kernel,
        grid_spec=grid_spec,
        out_shape=jax.ShapeDtypeStruct((MOE_ROWS, ROW_SUB, ROW_LANE), BF16),
        compiler_params=_params(1),
        name="experts",
    )(sched["n_used"], sched["tile_first"], sched["tile_ord"], sched["ord_expert"], sched["n_ord"],
      xs, w_e1, w_e3, w_e2)


def _combine_kernel(pos_ref, ys_ref, x2_ref, route_ref, g_ref, yp_ref, ysm_ref, gbuf, sem):
    i = pl.program_id(0)

    def start_gather(tile, slot):
        def body(r, carry):
            for k in range(MOE_TOPK):
                src_row = pos_ref[k, tile * ROW_TILE + r]
                pltpu.make_async_copy(ys_ref.at[src_row], gbuf.at[slot, k, r], sem.at[slot, k]).start(priority=k)
            return carry
        lax.fori_loop(0, ROW_TILE, body, 0, unroll=8)

    @pl.when(i == 0)
    def _():
        start_gather(0, 0)

    slot = i % 2

    @pl.when(i + 1 < pl.num_programs(0))
    def _():
        start_gather(i + 1, 1 - slot)

    for k in range(MOE_TOPK):
        pltpu.make_async_copy(ys_ref.at[pl.ds(0, ROW_TILE)], gbuf.at[slot, k], sem.at[slot, k]).wait()

    route = route_ref[...]
    lane = lax.broadcasted_iota(jnp.int32, route.shape, 1)
    w0 = jnp.sum(jnp.where(lane == 4, route, 0.0), axis=-1, keepdims=True)
    w1 = jnp.sum(jnp.where(lane == 5, route, 0.0), axis=-1, keepdims=True)
    g0 = _from_row_tiles(gbuf[slot, 0]).astype(F32)
    g1 = _from_row_tiles(gbuf[slot, 1]).astype(F32)
    x3 = x2_ref[...] + (g0 * w0 + g1 * w1)
    y = _rms(x3, g_ref[...])

    @pl.when(i < NP_TILES)
    def _():
        yp_ref[...] = y

    @pl.when(i == NP_TILES)
    def _():
        ysm_ref[...] = y


def _combine(pos, ys, x2, route, g_f):
    grid_spec = pltpu.PrefetchScalarGridSpec(
        num_scalar_prefetch=1,
        grid=(N_TILES,),
        in_specs=[
            pl.BlockSpec(memory_space=pl.ANY),
            pl.BlockSpec((ROW_TILE, D_MODEL), lambda i, pos: (i, 0)),
            pl.BlockSpec((ROW_TILE, ROUTE_LANES), lambda i, pos: (i, 0)),
            pl.BlockSpec((1, D_MODEL), lambda i, pos: (0, 0)),
        ],
        out_specs=[
            pl.BlockSpec((ROW_TILE, D_MODEL), lambda i, pos: (jnp.minimum(i, NP_TILES - 1), 0)),
            pl.BlockSpec((ROW_TILE, D_MODEL), lambda i, pos: (0, 0)),
        ],
        scratch_shapes=[pltpu.VMEM((2, MOE_TOPK, ROW_TILE, ROW_SUB, ROW_LANE), BF16),
                        pltpu.SemaphoreType.DMA((2, MOE_TOPK))],
    )
    return pl.pallas_call(
        _combine_kernel,
        grid_spec=grid_spec,
        out_shape=[
            jax.ShapeDtypeStruct((N_P, D_MODEL), F32),
            jax.ShapeDtypeStruct((N_S, D_MODEL), F32),
        ],
        compiler_params=_params(1),
        name="combine_norm",
    )(pos, ys, x2, route, g_f)


def _expert_schedule(counts):
    i32 = jnp.int32
    cnt = counts[0, MOE_GROUPS:MOE_GROUPS + MOE_EXPERTS].astype(i32)
    tiles = (cnt + MOE_TM - 1) // MOE_TM
    tile_end = jnp.cumsum(tiles)
    tile_start = tile_end - tiles
    n_used = tile_end[-1]
    t = jnp.arange(MOE_NT, dtype=i32)
    tile_expert = jnp.minimum(jnp.sum((t[:, None] >= tile_end[None, :]).astype(i32), axis=1), MOE_EXPERTS - 1)
    used = tiles > 0
    ord_of = jnp.cumsum(used.astype(i32)) - 1
    experts = jnp.arange(MOE_EXPERTS, dtype=i32)
    ord_expert = jnp.zeros((MOE_EXPERTS,), i32).at[jnp.where(used, ord_of, MOE_EXPERTS)].set(experts, mode="drop")
    return {
        "n_used": n_used.reshape(1).astype(i32),
        "tile_first": ((t == tile_start[tile_expert]) & (t < n_used)).astype(i32),
        "tile_ord": ord_of[tile_expert].astype(i32),
        "ord_expert": ord_expert,
        "n_ord": jnp.sum(used.astype(i32)).reshape(1),
        "zero_row": (jnp.maximum(tile_end - 1, 0) * MOE_TM).astype(i32),
        "zero_on": used.astype(i32),
    }


def kernel(x_prompt, x_sample, mem_prompt, state_ret, cache_mem_k, cache_mem_v, norm_mix, w_in, ret_gn,
           sg_ln_g, sg_ln_b, sg_ws, sg_bs, w_a_out, w_b_out, w_o, norm_xa, w_cq, w_ck, w_cv, w_co, norm_moe,
           w_rg, b_rg, w_re, b_re, w_e1, w_e3, w_e2, norm_f):
    xp = x_prompt.reshape(N_P, D_MODEL)
    xs = x_sample.reshape(N_S, D_MODEL)

    h = _norm_rows(xp, xs, norm_mix)
    w = w_in[0]
    qk = _inproj("rope", IN_TM_ROPE, 0, 2 * RET_QK, h, w, _rope_tables(IN_TM_ROPE))
    v = _inproj("copy", IN_TM, 2 * RET_QK, RET_V, h, w)
    gs = _inproj("silu", IN_TM, 2 * RET_QK + RET_V, RET_V, h, w)
    uv = _inproj("gelu", IN_TM, 2 * RET_QK + 2 * RET_V, 2 * SG_WIDTH, h, w)
    gab = _inproj("sigmoid", IN_TM, 2 * RET_QK + 2 * RET_V + 2 * SG_WIDTH, 2 * D_MODEL, h, w)

    a_p, ret_p = _ret_prompt(qk, v, gs, ret_gn, PROMPT_CHUNK)
    a_s, ret_s = _ret_sample(qk, v, gs, ret_gn, state_ret)

    b_all, sgv = _sgate(uv, sg_ln_g, sg_ln_b, sg_ws[0], sg_bs[0])

    x1, qx = _merge(a_p, a_s, b_all, gab, xp, xs, w_a_out[0].astype(BF16), w_b_out[0].astype(BF16),
                    w_o[0].astype(BF16), norm_xa, w_cq[0].astype(BF16))

    mk, mv, mkb, mvb = _memkv(mem_prompt.reshape(BATCH * MEM_LEN, D_MODEL), w_ck[0], w_cv[0])
    o_p = _xattn_prompt(qx, mkb, mvb)
    ck = cache_mem_k.reshape(DEC_BATCH, MEM_LEN * XA_HEADS, XA_DH)
    cv = cache_mem_v.reshape(DEC_BATCH, MEM_LEN * XA_HEADS, XA_DH)
    o_s = _xattn_sample(qx, ck, cv)

    w_r = jnp.zeros((D_MODEL, ROUTE_LANES), F32)
    w_r = w_r.at[:, :MOE_GROUPS].set(w_rg[0]).at[:, MOE_GROUPS:MOE_GROUPS + MOE_EXPERTS].set(w_re[0])
    b_r = jnp.zeros((1, ROUTE_LANES), F32)
    b_r = b_r.at[0, :MOE_GROUPS].set(b_rg[0]).at[0, MOE_GROUPS:MOE_GROUPS + MOE_EXPERTS].set(b_re[0])
    x2, hmw, route, counts = _route(o_p, o_s, x1, w_co[0].astype(BF16), norm_moe, w_r.astype(BF16), b_r)

    pos_t = _positions(route, counts)
    sched = _expert_schedule(counts)
    xs_sorted = _dispatch(pos_t, sched["zero_row"], sched["zero_on"], sched["n_used"], hmw)
    ys = _experts(sched, xs_sorted, w_e1[0], w_e3[0], w_e2[0])
    y_p, y_s = _combine(pos_t, ys, x2, route, norm_f.reshape(1, D_MODEL))

    return (y_p.reshape(BATCH, SEQ, D_MODEL),
            y_s.reshape(DEC_BATCH, DEC_SEQ, D_MODEL),
            ret_p,
            mk.reshape(1, BATCH, MEM_LEN, XA_HEADS, XA_DH),
            mv.reshape(1, BATCH, MEM_LEN, XA_HEADS, XA_DH),
            ret_s,
            sgv.reshape(1, DEC_BATCH, DEC_SEQ, SG_WIDTH))
```

```python
import functools

import jax
import jax.numpy as jnp
from jax import lax
from jax.experimental import pallas as pl
from jax.experimental.pallas import tpu as pltpu

F32 = jnp.float32
BF16 = jnp.bfloat16

D_MODEL = 2048
BATCH = 4
SEQ = 2048
DEC_BATCH = 128
DEC_SEQ = 4
PAST_LEN = 16384
RET_HEADS = 8
RET_DK = 128
RET_DV = 256
RET_CHUNK = 128
ROPE_BASE = 10000.0
RET_QK = RET_HEADS * RET_DK
RET_V = RET_HEADS * RET_DV
SG_GROUPS = 4
SG_WIDTH = 2048
SG_CHUNK = 128
MEM_LEN = 256
XA_HEADS = 4
XA_DH = 128
XA_W = XA_HEADS * XA_DH
MOE_GROUPS = 4
MOE_PER_GROUP = 8
MOE_EXPERTS = MOE_GROUPS * MOE_PER_GROUP
MOE_TOPK = 2
MOE_FF = 512
EPS = 1e-6
IN_WIDTH = 2 * RET_QK + 2 * RET_V + 2 * SG_WIDTH + 2 * D_MODEL

N_P = BATCH * SEQ
N_S = DEC_BATCH * DEC_SEQ
N_ALL = N_P + N_S
ROW_TILE = 512
N_TILES = N_ALL // ROW_TILE
NP_TILES = N_P // ROW_TILE
MERGE_TILE = 256
MERGE_P_TILES = N_P // MERGE_TILE

IN_TM = 1024
IN_TM_ROPE = 1024
IN_TN = 1024

PROMPT_CHUNK = 256
RET_BB = DEC_BATCH // (BATCH * (SEQ // PROMPT_CHUNK))
RET_ROWS = RET_BB * DEC_SEQ
LOG_DEC_SEQ = 2
LOG_SG_CHUNK = 7
SAMPLE_BB = DEC_BATCH // NP_TILES
SAMPLE_ROWS = SAMPLE_BB * DEC_SEQ

MOE_TM = 256
MOE_NT = (N_ALL * MOE_TOPK + MOE_EXPERTS * (MOE_TM - 1) + MOE_TM - 1) // MOE_TM
MOE_ROWS = MOE_NT * MOE_TM
ROUTE_LANES = 128
ROW_SUB, ROW_LANE = 16, 128
W_SLOTS = 3

VMEM_LIMIT = 56 * 1024 * 1024


def _params(n_axes, vmem=VMEM_LIMIT):
    return pltpu.CompilerParams(dimension_semantics=("arbitrary",) * n_axes,
                                vmem_limit_bytes=vmem)


def _rms(x, g):
    ms = jnp.mean(x * x, axis=-1, keepdims=True)
    return (x * lax.rsqrt(ms + EPS)) * g


def _dot(a, b):
    return jnp.dot(a, b, preferred_element_type=F32)


def _sigmoid(x):
    return 0.5 * jnp.tanh(0.5 * x) + 0.5


def _to_row_tiles(x):
    return x.astype(BF16).reshape(x.shape[0], ROW_SUB, ROW_LANE)


def _from_row_tiles(t):
    return t.reshape(t.shape[0], D_MODEL)


def _dot_nt(a, b):
    return lax.dot_general(a, b, (((1,), (1,)), ((), ())), preferred_element_type=F32)


def _dot_tn(a, b):
    return lax.dot_general(a, b, (((0,), (0,)), ((), ())), preferred_element_type=F32)


def _norm_kernel(xp_ref, xs_ref, g_ref, h_ref):
    i = pl.program_id(0)

    @pl.when(i < NP_TILES)
    def _():
        h_ref[...] = _rms(xp_ref[...], g_ref[...]).astype(BF16)

    @pl.when(i == NP_TILES)
    def _():
        h_ref[...] = _rms(xs_ref[...], g_ref[...]).astype(BF16)


def _norm_rows(xp, xs, g):
    return pl.pallas_call(
        _norm_kernel,
        grid=(N_TILES,),
        in_specs=[
            pl.BlockSpec((ROW_TILE, D_MODEL), lambda i: (jnp.minimum(i, NP_TILES - 1), 0)),
            pl.BlockSpec((ROW_TILE, D_MODEL), lambda i: (0, 0)),
            pl.BlockSpec((1, D_MODEL), lambda i: (0, 0)),
        ],
        out_specs=pl.BlockSpec((ROW_TILE, D_MODEL), lambda i: (i, 0)),
        out_shape=jax.ShapeDtypeStruct((N_ALL, D_MODEL), BF16),
        compiler_params=_params(1),
        name="norm_rows",
    )(xp, xs, g)


def _inproj_kernel(kind, tm, h_ref, w_ref, *rest):
    if kind == "rope":
        cos_ref, sin_ref, z_ref, wb_ref = rest
    else:
        z_ref, wb_ref = rest
    j = pl.program_id(0)
    i = pl.program_id(1)
    last = N_P // tm

    @pl.when(i == 0)
    def _():
        wb_ref[...] = w_ref[...].astype(BF16)

    def tile(rows):
        acc = _dot(h_ref[0:rows, :], wb_ref[...])
        if kind == "rope":
            scale = jnp.where(j == 1, RET_DK ** -0.5, 1.0).astype(F32)
            c = cos_ref[0:rows, :]
            s = sin_ref[0:rows, :]
            for hb in range(IN_TN // RET_DK):
                cols = slice(hb * RET_DK, (hb + 1) * RET_DK)
                a = acc[:, cols]
                r = pltpu.roll(a, RET_DK // 2, axis=1)
                z_ref[0:rows, cols] = ((a * c + r * s) * scale).astype(BF16)
        elif kind == "copy":
            z_ref[0:rows, :] = acc.astype(BF16)
        elif kind == "silu":
            z_ref[0:rows, :] = (acc * _sigmoid(acc)).astype(BF16)
        elif kind == "gelu":
            z_ref[0:rows, :] = jax.nn.gelu(acc).astype(BF16)
        else:
            z_ref[0:rows, :] = _sigmoid(acc).astype(BF16)

    @pl.when(i < last)
    def _():
        tile(tm)

    @pl.when(i == last)
    def _():
        tile(N_S)


def _inproj(kind, tm, col0, width, h, w_in, tables=()):
    last = N_P // tm
    tab_idx = lambda j, i: (jnp.where(i < last, i % (SEQ // tm), SEQ // tm), 0)
    j0 = col0 // IN_TN
    return pl.pallas_call(
        functools.partial(_inproj_kernel, kind, tm),
        grid=(width // IN_TN, last + 1),
        in_specs=[
            pl.BlockSpec((tm, D_MODEL), lambda j, i: (i, 0)),
            pl.BlockSpec((D_MODEL, IN_TN), lambda j, i: (0, j0 + j)),
        ] + [pl.BlockSpec((tm, RET_DK), tab_idx) for _ in tables],
        out_specs=pl.BlockSpec((tm, IN_TN), lambda j, i: (i, j)),
        out_shape=jax.ShapeDtypeStruct((N_ALL, width), BF16),
        scratch_shapes=[pltpu.VMEM((D_MODEL, IN_TN), BF16)],
        compiler_params=_params(2),
        name="in_proj_" + kind,
    )(h, w_in, *tables)


def _rope_tables(tm):
    half = RET_DK // 2
    inv = ROPE_BASE ** (-jnp.arange(half, dtype=F32) / half)

    def tab(pos):
        ang = pos.astype(F32)[:, None] * inv[None, :]
        c, s = jnp.cos(ang), jnp.sin(ang)
        return jnp.concatenate([c, c], -1), jnp.concatenate([-s, s], -1)

    cp, sp = tab(jnp.arange(SEQ, dtype=jnp.int32))
    cs, ss = tab(PAST_LEN + jnp.arange(DEC_SEQ, dtype=jnp.int32))
    cs = jnp.tile(cs, (DEC_BATCH, 1))
    ss = jnp.tile(ss, (DEC_BATCH, 1))
    pad = jnp.zeros((tm - N_S, RET_DK), F32)
    return jnp.concatenate([cp, cs, pad], 0), jnp.concatenate([sp, ss, pad], 0)


def _decay_tables(chunk):
    lg = jnp.log1p(-jnp.power(2.0, -5.0 - jnp.arange(RET_HEADS, dtype=F32)))
    idx = jnp.arange(chunk, dtype=F32)
    rel = idx[:, None] - idx[None, :]
    dmask = jnp.where(rel >= 0, jnp.exp(lg[:, None, None] * jnp.maximum(rel, 0.0)), 0.0).astype(F32)
    xi = jnp.exp(lg[:, None] * (idx[None, :] + 1.0)).astype(F32)
    zeta = jnp.exp(lg[:, None] * (chunk - 1.0 - idx[None, :])).astype(F32)
    gc = jnp.exp(lg * chunk).astype(F32)
    return dmask, xi, zeta, gc


def _head_norm_gate(o, gn, gate):
    mu = jnp.mean(o, axis=-1, keepdims=True)
    d = o - mu
    var = jnp.mean(d * d, axis=-1, keepdims=True)
    y = (d * lax.rsqrt(var + EPS)) * gn
    return (gate.astype(F32) * y).astype(BF16)


def _ret_kernel(gcp_ref, gcs_ref, q_ref, k_ref, v_ref, gs_ref, gn_ref, dm_ref, xi_ref, zt_ref,
                qs_ref, ks_ref, vs_ref, gss_ref, dms_ref, xis_ref, zts_ref, s0_ref,
                a_ref, sfin_ref, as_ref, s1_ref, s_ref):
    c = pl.program_id(1)

    @pl.when(c == 0)
    def _():
        s_ref[...] = jnp.zeros_like(s_ref)

    rows_k = lax.broadcasted_iota(jnp.int32, (RET_ROWS, RET_DK), 0) >> LOG_DEC_SEQ
    rows_v = lax.broadcasted_iota(jnp.int32, (RET_ROWS, RET_DV), 0) >> LOG_DEC_SEQ
    for h in range(RET_HEADS):
        kc = slice(h * RET_DK, (h + 1) * RET_DK)
        vc = slice(h * RET_DV, (h + 1) * RET_DV)
        gn = gn_ref[:, vc]

        qh = q_ref[:, kc]
        kh = k_ref[:, kc]
        vh = v_ref[:, vc]
        inner = _dot_nt(qh, kh) * dm_ref[h]
        o = _dot(inner.astype(BF16), vh)
        s_old = s_ref[h]
        xi = xi_ref[h]
        o = o + _dot(qh, s_old.astype(BF16)) * jnp.concatenate([xi, xi], axis=1)
        kz = (kh.astype(F32) * zt_ref[h]).astype(BF16)
        s_ref[h] = gcp_ref[h] * s_old + _dot_tn(kz, vh)
        a_ref[:, vc] = _head_norm_gate(o, gn, gs_ref[:, vc])

        qh = qs_ref[:, kc]
        kh = ks_ref[:, kc]
        vh = vs_ref[:, vc]
        inner = _dot_nt(qh, kh) * dms_ref[h]
        o = _dot(inner.astype(BF16), vh)
        xi = xis_ref[h]
        xi2 = jnp.concatenate([xi, xi], axis=1)
        kz = kh.astype(F32) * zts_ref[h]
        gch = gcs_ref[h]
        for b in range(RET_BB):
            s_old = s0_ref[0, b, h]
            cross = _dot(qh, s_old.astype(BF16)) * xi2
            o = o + jnp.where(rows_v == b, cross, 0.0)
            kz_b = jnp.where(rows_k == b, kz, 0.0).astype(BF16)
            s1_ref[0, b, h] = gch * s_old + _dot_tn(kz_b, vh)
        as_ref[:, vc] = _head_norm_gate(o, gn, gss_ref[:, vc])

    @pl.when(c == pl.num_programs(1) - 1)
    def _():
        sfin_ref[0, 0] = s_ref[...]


def _retention(qk, v, gs, ret_gn, state):
    chunk = PROMPT_CHUNK
    n_chunks = SEQ // chunk
    dmask, xi, zeta, gc = _decay_tables(chunk)
    xi_b = jnp.broadcast_to(xi[:, :, None], (RET_HEADS, chunk, RET_DK))
    zeta_b = jnp.broadcast_to(zeta[:, :, None], (RET_HEADS, chunk, RET_DK))
    dmask_s, xi_s, zeta_s, gc_s = _decay_tables(DEC_SEQ)
    eye = jnp.eye(RET_BB, dtype=F32)
    dm_big = jax.vmap(lambda m: jnp.kron(eye, m))(dmask_s)
    xi_sb = jnp.broadcast_to(jnp.tile(xi_s, (1, RET_BB))[:, :, None], (RET_HEADS, RET_ROWS, RET_DK))
    zeta_sb = jnp.broadcast_to(jnp.tile(zeta_s, (1, RET_BB))[:, :, None], (RET_HEADS, RET_ROWS, RET_DK))

    row = lambda b, c: b * n_chunks + c
    srow = lambda b, c: N_P // RET_ROWS + row(b, c)
    const3 = lambda b, c: (0, 0, 0)
    smem = pl.BlockSpec(memory_space=pltpu.SMEM)
    st_spec = pl.BlockSpec((1, RET_BB, RET_HEADS, RET_DK, RET_DV), lambda b, c: (0, row(b, c), 0, 0, 0))
    return pl.pallas_call(
        _ret_kernel,
        grid=(BATCH, n_chunks),
        in_specs=[
            smem, smem,
            pl.BlockSpec((chunk, RET_QK), lambda b, c: (row(b, c), 0)),
            pl.BlockSpec((chunk, RET_QK), lambda b, c: (row(b, c), 1)),
            pl.BlockSpec((chunk, RET_V), lambda b, c: (row(b, c), 0)),
            pl.BlockSpec((chunk, RET_V), lambda b, c: (row(b, c), 0)),
            pl.BlockSpec((1, RET_V), lambda b, c: (0, 0)),
            pl.BlockSpec((RET_HEADS, chunk, chunk), const3),
            pl.BlockSpec((RET_HEADS, chunk, RET_DK), const3),
            pl.BlockSpec((RET_HEADS, chunk, RET_DK), const3),
            pl.BlockSpec((RET_ROWS, RET_QK), lambda b, c: (srow(b, c), 0)),
            pl.BlockSpec((RET_ROWS, RET_QK), lambda b, c: (srow(b, c), 1)),
            pl.BlockSpec((RET_ROWS, RET_V), lambda b, c: (srow(b, c), 0)),
            pl.BlockSpec((RET_ROWS, RET_V), lambda b, c: (srow(b, c), 0)),
            pl.BlockSpec((RET_HEADS, RET_ROWS, RET_ROWS), const3),
            pl.BlockSpec((RET_HEADS, RET_ROWS, RET_DK), const3),
            pl.BlockSpec((RET_HEADS, RET_ROWS, RET_DK), const3),
            st_spec,
        ],
        out_specs=[
            pl.BlockSpec((chunk, RET_V), lambda b, c: (row(b, c), 0)),
            pl.BlockSpec((1, 1, RET_HEADS, RET_DK, RET_DV), lambda b, c: (0, b, 0, 0, 0)),
            pl.BlockSpec((RET_ROWS, RET_V), lambda b, c: (row(b, c), 0)),
            st_spec,
        ],
        out_shape=[
            jax.ShapeDtypeStruct((N_P, RET_V), BF16),
            jax.ShapeDtypeStruct((1, BATCH, RET_HEADS, RET_DK, RET_DV), F32),
            jax.ShapeDtypeStruct((N_S, RET_V), BF16),
            jax.ShapeDtypeStruct((1, DEC_BATCH, RET_HEADS, RET_DK, RET_DV), F32),
        ],
        scratch_shapes=[pltpu.VMEM((RET_HEADS, RET_DK, RET_DV), F32)],
        compiler_params=_params(2),
        name="retention",
    )(gc, gc_s, qk, qk, v, gs, ret_gn, dmask, xi_b, zeta_b, qk, qk, v, gs, dm_big, xi_sb, zeta_sb, state)


def _sgate_kernel(u_ref, v_ref, lg_ref, lb_ref, wp_ref, bp_ref, ws_ref, bs_ref, b_ref, sgv_ref, vln_ref):
    i = pl.program_id(0)
    v = v_ref[...].astype(F32)
    mu = jnp.mean(v, axis=-1, keepdims=True)
    d = v - mu
    var = jnp.mean(d * d, axis=-1, keepdims=True)
    vln_ref[...] = (d * lax.rsqrt(var + EPS)) * lg_ref[...] + lb_ref[...]
    gw = SG_WIDTH // SG_GROUPS
    lane_reps = gw // 128

    @pl.when(i < NP_TILES)
    def _():
        r = lax.broadcasted_iota(jnp.int32, (SG_CHUNK, SG_CHUNK), 0)
        c = lax.broadcasted_iota(jnp.int32, (SG_CHUNK, SG_CHUNK), 1)
        for g in range(SG_GROUPS):
            cols = slice(g * gw, (g + 1) * gw)
            w = jnp.where(c <= r, wp_ref[g], 0.0).astype(BF16)
            bias = jnp.concatenate([bp_ref[g]] * lane_reps, axis=1)
            for ch in range(ROW_TILE // SG_CHUNK):
                rows = slice(ch * SG_CHUNK, (ch + 1) * SG_CHUNK)
                mixed = _dot(w, vln_ref[rows, cols].astype(BF16)) + bias
                b_ref[rows, cols] = (u_ref[rows, cols].astype(F32) * mixed).astype(BF16)

    @pl.when(i == NP_TILES)
    def _():
        sgv_ref[...] = vln_ref[...]
        r = lax.broadcasted_iota(jnp.int32, (ROW_TILE, ROW_TILE), 0)
        c = lax.broadcasted_iota(jnp.int32, (ROW_TILE, ROW_TILE), 1)
        keep = ((r >> LOG_DEC_SEQ) == (c >> LOG_DEC_SEQ)) & (c <= r)
        for g in range(SG_GROUPS):
            cols = slice(g * gw, (g + 1) * gw)
            w_rows = jnp.concatenate([ws_ref[g]] * (ROW_TILE // 8), axis=0)
            w_full = jnp.concatenate([w_rows] * (ROW_TILE // 128), axis=1)
            w = jnp.where(keep, w_full, 0.0).astype(BF16)
            b_rows = jnp.concatenate([bs_ref[g]] * (ROW_TILE // 8), axis=0)
            bias = jnp.concatenate([b_rows] * lane_reps, axis=1)
            mixed = _dot(w, vln_ref[:, cols].astype(BF16)) + bias
            b_ref[:, cols] = (u_ref[:, cols].astype(F32) * mixed).astype(BF16)


def _sgate(uv, ln_g, ln_b, sg_ws, sg_bs):
    b_p = jnp.broadcast_to(sg_bs[:, :, None], (SG_GROUPS, SG_CHUNK, 128))
    w_s = jnp.tile(sg_ws[:, :DEC_SEQ, :DEC_SEQ], (1, 8 // DEC_SEQ, 128 // DEC_SEQ))
    b_s = jnp.broadcast_to(jnp.tile(sg_bs[:, :DEC_SEQ], (1, 8 // DEC_SEQ))[:, :, None], (SG_GROUPS, 8, 128))
    const3 = lambda i: (0, 0, 0)
    return pl.pallas_call(
        _sgate_kernel,
        grid=(N_TILES,),
        in_specs=[
            pl.BlockSpec((ROW_TILE, SG_WIDTH), lambda i: (i, 0)),
            pl.BlockSpec((ROW_TILE, SG_WIDTH), lambda i: (i, 1)),
            pl.BlockSpec((1, SG_WIDTH), lambda i: (0, 0)),
            pl.BlockSpec((1, SG_WIDTH), lambda i: (0, 0)),
            pl.BlockSpec((SG_GROUPS, SG_CHUNK, SG_CHUNK), const3),
            pl.BlockSpec((SG_GROUPS, SG_CHUNK, 128), const3),
            pl.BlockSpec((SG_GROUPS, 8, 128), const3),
            pl.BlockSpec((SG_GROUPS, 8, 128), const3),
        ],
        out_specs=[
            pl.BlockSpec((ROW_TILE, SG_WIDTH), lambda i: (i, 0)),
            pl.BlockSpec((N_S, SG_WIDTH), lambda i: (0, 0)),
        ],
        out_shape=[
            jax.ShapeDtypeStruct((N_ALL, SG_WIDTH), BF16),
            jax.ShapeDtypeStruct((N_S, SG_WIDTH), F32),
        ],
        scratch_shapes=[pltpu.VMEM((ROW_TILE, SG_WIDTH), F32)],
        compiler_params=_params(1),
        name="spatial_gate",
    )(uv, uv, ln_g, ln_b, sg_ws, b_p, w_s, b_s)


def _merge_kernel(ap_ref, as_ref, b_ref, ga_ref, gb_ref, xp_ref, xs_ref, wa_ref, wb_ref, wo_ref,
                  g_ref, wq_ref, x1_ref, q_ref):
    i = pl.program_id(0)

    def run(a_in, x):
        a = _dot(a_in, wa_ref[...])
        b = _dot(b_ref[...], wb_ref[...])
        merged = ga_ref[...].astype(F32) * a + gb_ref[...].astype(F32) * b
        x1 = x + _dot(merged.astype(BF16), wo_ref[...])
        x1_ref[...] = x1
        q_ref[...] = _dot(_rms(x1, g_ref[...]).astype(BF16), wq_ref[...]).astype(BF16)

    @pl.when(i < MERGE_P_TILES)
    def _():
        run(ap_ref[...], xp_ref[...])

    @pl.when(i >= MERGE_P_TILES)
    def _():
        run(as_ref[...], xs_ref[...])


def _resident(shape):
    return pl.BlockSpec(shape, lambda i: (0,) * len(shape), pipeline_mode=pl.Buffered(1))


def _merge(a_p, a_s, b_all, gab, xp, xs, wa, wb, wo, g_xa, wq):
    tm = MERGE_TILE
    prompt_tile = lambda i: (jnp.minimum(i, MERGE_P_TILES - 1), 0)
    sample_tile = lambda i: (jnp.maximum(i - MERGE_P_TILES, 0), 0)
    return pl.pallas_call(
        _merge_kernel,
        grid=(N_ALL // tm,),
        in_specs=[
            pl.BlockSpec((tm, RET_V), prompt_tile),
            pl.BlockSpec((tm, RET_V), sample_tile),
            pl.BlockSpec((tm, SG_WIDTH), lambda i: (i, 0)),
            pl.BlockSpec((tm, D_MODEL), lambda i: (i, 0)),
            pl.BlockSpec((tm, D_MODEL), lambda i: (i, 1)),
            pl.BlockSpec((tm, D_MODEL), prompt_tile),
            pl.BlockSpec((tm, D_MODEL), sample_tile),
            _resident((RET_V, D_MODEL)),
            _resident((SG_WIDTH, D_MODEL)),
            _resident((D_MODEL, D_MODEL)),
            pl.BlockSpec((1, D_MODEL), lambda i: (0, 0)),
            _resident((D_MODEL, XA_W)),
        ],
        out_specs=[
            pl.BlockSpec((tm, D_MODEL), lambda i: (i, 0)),
            pl.BlockSpec((tm, XA_W), lambda i: (i, 0)),
        ],
        out_shape=[
            jax.ShapeDtypeStruct((N_ALL, D_MODEL), F32),
            jax.ShapeDtypeStruct((N_ALL, XA_W), BF16),
        ],
        compiler_params=_params(1),
        name="merge_proj",
    )(a_p, a_s, b_all, gab, gab, xp, xs, wa, wb, wo, g_xa, wq)


def _memkv_kernel(m_ref, wk_ref, wv_ref, k_ref, v_ref, kb_ref, vb_ref):
    m = m_ref[...].astype(BF16)
    k = _dot(m, wk_ref[...].astype(BF16))
    v = _dot(m, wv_ref[...].astype(BF16))
    k_ref[...] = k
    v_ref[...] = v
    kb_ref[...] = k.astype(BF16)
    vb_ref[...] = v.astype(BF16)


def _memkv(mem, w_ck, w_cv):
    rows = BATCH * MEM_LEN
    spec = pl.BlockSpec((MEM_LEN, XA_W), lambda b: (b, 0))
    wspec = pl.BlockSpec((D_MODEL, XA_W), lambda b: (0, 0))
    return pl.pallas_call(
        _memkv_kernel,
        grid=(BATCH,),
        in_specs=[pl.BlockSpec((MEM_LEN, D_MODEL), lambda b: (b, 0)), wspec, wspec],
        out_specs=[spec, spec, spec, spec],
        out_shape=[jax.ShapeDtypeStruct((rows, XA_W), F32)] * 2 + [jax.ShapeDtypeStruct((rows, XA_W), BF16)] * 2,
        compiler_params=_params(1),
        name="mem_kv",
    )(mem, w_ck, w_cv)


def _softmax_rows(s):
    m = jnp.max(s, axis=-1, keepdims=True)
    e = jnp.exp(s - m)
    return e / jnp.sum(e, axis=-1, keepdims=True)


def _xattn_prompt_kernel(q_ref, k_ref, v_ref, o_ref):
    for h in range(XA_HEADS):
        cols = slice(h * XA_DH, (h + 1) * XA_DH)
        s = _dot_nt(q_ref[:, cols], k_ref[:, cols]) * (XA_DH ** -0.5)
        p = _softmax_rows(s)
        o_ref[:, cols] = _dot(p.astype(BF16), v_ref[:, cols]).astype(BF16)


def _xattn_prompt(q, kb, vb):
    per_b = SEQ // ROW_TILE
    kv = pl.BlockSpec((MEM_LEN, XA_W), lambda i: (i // per_b, 0))
    return pl.pallas_call(
        _xattn_prompt_kernel,
        grid=(NP_TILES,),
        in_specs=[pl.BlockSpec((ROW_TILE, XA_W), lambda i: (i, 0)), kv, kv],
        out_specs=pl.BlockSpec((ROW_TILE, XA_W), lambda i: (i, 0)),
        out_shape=jax.ShapeDtypeStruct((N_P, XA_W), BF16),
        compiler_params=_params(1),
        name="xattn_prompt",
    )(q, kb, vb)


def _xattn_sample_entries(q_ref, k_ref, v_ref, o_ref):
    n_q = XA_HEADS * DEC_SEQ
    n_kv = MEM_LEN * XA_HEADS
    qf = q_ref[...].astype(F32)
    r = lax.broadcasted_iota(jnp.int32, (n_q, n_kv), 0)
    c = lax.broadcasted_iota(jnp.int32, (n_q, n_kv), 1)
    head_ok = (r >> LOG_DEC_SEQ) == (c & (XA_HEADS - 1))
    for b in range(SAMPLE_BB):
        rows = slice(b * DEC_SEQ, (b + 1) * DEC_SEQ)
        qb = jnp.concatenate([qf[rows, h * XA_DH:(h + 1) * XA_DH] for h in range(XA_HEADS)], axis=0)
        s = _dot_nt(qb.astype(BF16), k_ref[b].astype(BF16)) * (XA_DH ** -0.5)
        p = _softmax_rows(jnp.where(head_ok, s, -1e30))
        o = _dot(p.astype(BF16), v_ref[b].astype(BF16))
        for h in range(XA_HEADS):
            o_ref[h, rows, :] = o[h * DEC_SEQ:(h + 1) * DEC_SEQ, :]


def _route_kernel(op_ref, qs_ref, ck_ref, cv_ref, x1_ref, wo_ref, g_ref, wr_ref, br_ref,
                  x2_ref, hm_ref, route_ref, cnt_ref, run_ref, os_ref):
    i = pl.program_id(0)

    @pl.when(i == 0)
    def _():
        run_ref[...] = jnp.zeros_like(run_ref)

    def run(o):
        x2 = x1_ref[...] + _dot(o, wo_ref[...])
        x2_ref[...] = x2
        hm = _rms(x2, g_ref[...])
        hm_ref[...] = _to_row_tiles(hm)
        logits = _dot(hm.astype(BF16), wr_ref[...]) + br_ref[...]
        lane = lax.broadcasted_iota(jnp.int32, logits.shape, 1)
        neg = jnp.float32(-jnp.inf)
        big = jnp.int32(1 << 20)
        is_g = lane < MOE_GROUPS
        gl = jnp.where(is_g, logits, neg)
        gmax = jnp.max(gl, axis=-1, keepdims=True)
        g_sel = jnp.min(jnp.where(gl == gmax, lane, big), axis=-1, keepdims=True)
        g_w = 1.0 / jnp.sum(jnp.where(is_g, jnp.exp(logits - gmax), 0.0), axis=-1, keepdims=True)
        e_lane = lane - MOE_GROUPS
        in_grp = (e_lane >= 0) & (e_lane < MOE_EXPERTS) & ((e_lane >> 3) == g_sel)
        el = jnp.where(in_grp, logits, neg)
        v0 = jnp.max(el, axis=-1, keepdims=True)
        i0 = jnp.min(jnp.where(el == v0, lane, big), axis=-1, keepdims=True)
        el1 = jnp.where(lane == i0, neg, el)
        v1 = jnp.max(el1, axis=-1, keepdims=True)
        i1 = jnp.min(jnp.where(el1 == v1, lane, big), axis=-1, keepdims=True)
        ex = jnp.exp(v1 - v0)
        den = 1.0 + ex
        w0 = (1.0 / den) * g_w
        w1 = (ex / den) * g_w
        a0 = (lane == i0).astype(F32)
        a1 = (lane == i1).astype(F32)
        a = a0 + a1
        rr = lax.broadcasted_iota(jnp.int32, (ROW_TILE, ROW_TILE), 0)
        cc = lax.broadcasted_iota(jnp.int32, (ROW_TILE, ROW_TILE), 1)
        lower = jnp.where(cc < rr, 1.0, 0.0).astype(BF16)
        before = _dot(lower, a.astype(BF16)) + run_ref[...]
        rank0 = jnp.sum(before * a0, axis=-1, keepdims=True)
        rank1 = jnp.sum(before * a1, axis=-1, keepdims=True)
        run_ref[...] += jnp.sum(a, axis=0, keepdims=True)
        e0 = (i0 - MOE_GROUPS).astype(F32)
        e1 = (i1 - MOE_GROUPS).astype(F32)
        route = jnp.where(lane == 0, e0, 0.0)
        route = jnp.where(lane == 1, e1, route)
        route = jnp.where(lane == 2, rank0, route)
        route = jnp.where(lane == 3, rank1, route)
        route = jnp.where(lane == 4, w0, route)
        route = jnp.where(lane == 5, w1, route)
        route_ref[...] = route
        cnt_ref[...] = run_ref[...]

    @pl.when(i < NP_TILES)
    def _():
        _xattn_sample_entries(qs_ref, ck_ref, cv_ref, os_ref.at[i])
        run(op_ref[...])

    @pl.when(i == NP_TILES)
    def _():
        heads = [os_ref[:, h].reshape(N_S, XA_DH) for h in range(XA_HEADS)]
        run(jnp.concatenate(heads, axis=1).astype(BF16))


def _route(o_p, qx, ck, cv, x1, w_co, g_moe, w_r, b_r):
    prompt_step = lambda i: jnp.minimum(i, NP_TILES - 1)
    kv = pl.BlockSpec((SAMPLE_BB, MEM_LEN * XA_HEADS, XA_DH), lambda i: (prompt_step(i), 0, 0))
    return pl.pallas_call(
        _route_kernel,
        grid=(N_TILES,),
        in_specs=[
            pl.BlockSpec((ROW_TILE, XA_W), lambda i: (prompt_step(i), 0)),
            pl.BlockSpec((SAMPLE_ROWS, XA_W), lambda i: (N_P // SAMPLE_ROWS + prompt_step(i), 0)),
            kv, kv,
            pl.BlockSpec((ROW_TILE, D_MODEL), lambda i: (i, 0)),
            _resident((XA_W, D_MODEL)),
            pl.BlockSpec((1, D_MODEL), lambda i: (0, 0)),
            _resident((D_MODEL, ROUTE_LANES)),
            pl.BlockSpec((1, ROUTE_LANES), lambda i: (0, 0)),
        ],
        out_specs=[
            pl.BlockSpec((ROW_TILE, D_MODEL), lambda i: (i, 0)),
            pl.BlockSpec((ROW_TILE, ROW_SUB, ROW_LANE), lambda i: (i, 0, 0)),
            pl.BlockSpec((ROW_TILE, ROUTE_LANES), lambda i: (i, 0)),
            pl.BlockSpec((1, ROUTE_LANES), lambda i: (0, 0)),
        ],
        out_shape=[
            jax.ShapeDtypeStruct((N_ALL, D_MODEL), F32),
            jax.ShapeDtypeStruct((N_ALL, ROW_SUB, ROW_LANE), BF16),
            jax.ShapeDtypeStruct((N_ALL, ROUTE_LANES), F32),
            jax.ShapeDtypeStruct((1, ROUTE_LANES), F32),
        ],
        scratch_shapes=[pltpu.VMEM((1, ROUTE_LANES), F32),
                        pltpu.VMEM((NP_TILES, XA_HEADS, SAMPLE_ROWS, XA_DH), F32)],
        compiler_params=_params(1),
        name="xa_out_route",
    )(o_p, qx, ck, cv, x1, w_co, g_moe, w_r, b_r)


def _positions_kernel(route_ref, cnt_ref, pos_ref):
    route = route_ref[...]
    lane = lax.broadcasted_iota(jnp.int32, route.shape, 1)
    lane_f = lane.astype(F32)
    tiles = jnp.floor((cnt_ref[...] + (MOE_TM - 1)) * (1.0 / MOE_TM))
    lr = lax.broadcasted_iota(jnp.int32, (ROUTE_LANES, ROUTE_LANES), 0)
    lc = lax.broadcasted_iota(jnp.int32, (ROUTE_LANES, ROUTE_LANES), 1)
    before = jnp.where(lr < lc, 1.0, 0.0).astype(BF16)
    tiles8 = jnp.broadcast_to(tiles, (8, ROUTE_LANES)).astype(BF16)
    start = _dot(tiles8, before)[0:1, :] * MOE_TM

    def col(k):
        return jnp.sum(jnp.where(lane == k, route, 0.0), axis=-1, keepdims=True)

    def first_row(e):
        return jnp.sum(jnp.where(lane_f == e + MOE_GROUPS, start, 0.0), axis=-1, keepdims=True)

    p0 = first_row(col(0)) + col(2)
    p1 = first_row(col(1)) + col(3)
    p = jnp.where(lane == 0, p0, jnp.where(lane == 1, p1, 0.0))
    pos_ref[...] = p.T[0:8, :].astype(jnp.int32)


def _positions(route, counts):
    rows = N_ALL // 4
    return pl.pallas_call(
        _positions_kernel,
        grid=(4,),
        in_specs=[
            pl.BlockSpec((rows, ROUTE_LANES), lambda i: (i, 0)),
            pl.BlockSpec((1, ROUTE_LANES), lambda i: (0, 0)),
        ],
        out_specs=pl.BlockSpec((8, rows), lambda i: (0, i)),
        out_shape=jax.ShapeDtypeStruct((8, N_ALL), jnp.int32),
        compiler_params=_params(1),
        name="positions",
    )(route, counts)


def _dispatch_kernel(pos_ref, zrow_ref, zon_ref, nu_ref, hm_ref, xs_ref, zbuf, sem, zsem):
    i = pl.program_id(0)

    def zero_tile(row):
        return pltpu.make_async_copy(zbuf, xs_ref.at[pl.ds(pl.multiple_of(row, MOE_TM), MOE_TM)], zsem)

    @pl.when(i == 0)
    def _():
        zbuf[...] = jnp.zeros_like(zbuf)
        for e in range(MOE_EXPERTS):
            @pl.when(zon_ref[e] > 0)
            def _():
                zero_tile(zrow_ref[e]).start()

        def start_tail(t, carry):
            zero_tile(t * MOE_TM).start()
            return carry

        def wait_tail(t, carry):
            zero_tile(t * MOE_TM).wait()
            return carry

        lax.fori_loop(nu_ref[0], MOE_NT, start_tail, 0)
        for e in range(MOE_EXPERTS):
            @pl.when(zon_ref[e] > 0)
            def _():
                zero_tile(zrow_ref[e]).wait()
        lax.fori_loop(nu_ref[0], MOE_NT, wait_tail, 0)

    base = i * ROW_TILE

    def row_copy(r, k, dst_row):
        return pltpu.make_async_copy(hm_ref.at[r], xs_ref.at[dst_row], sem.at[k])

    def start(r, carry):
        for k in range(MOE_TOPK):
            row_copy(r, k, pos_ref[k, base + r]).start(priority=k)
        return carry

    lax.fori_loop(0, ROW_TILE, start, 0, unroll=8)
    for k in range(MOE_TOPK):
        pltpu.make_async_copy(hm_ref, xs_ref.at[pl.ds(0, ROW_TILE)], sem.at[k]).wait()


def _dispatch(pos_t, zero_row, zero_on, n_used, hmw):
    grid_spec = pltpu.PrefetchScalarGridSpec(
        num_scalar_prefetch=4,
        grid=(N_TILES,),
        in_specs=[pl.BlockSpec((ROW_TILE, ROW_SUB, ROW_LANE), lambda i, *_: (i, 0, 0))],
        out_specs=pl.BlockSpec(memory_space=pl.ANY),
        scratch_shapes=[pltpu.VMEM((MOE_TM, ROW_SUB, ROW_LANE), BF16),
                        pltpu.SemaphoreType.DMA((MOE_TOPK,)), pltpu.SemaphoreType.DMA(())],
    )
    return pl.pallas_call(
        _dispatch_kernel,
        grid_spec=grid_spec,
        out_shape=jax.ShapeDtypeStruct((MOE_ROWS, ROW_SUB, ROW_LANE), BF16),
        compiler_params=_params(1),
        name="dispatch",
    )(pos_t, zero_row, zero_on, n_used, hmw)


def _expert_kernel(nu_ref, first_ref, ord_ref, oe_ref, no_ref, x_ref, w1_hbm, w3_hbm, w2_hbm, y_ref,
                   w1b, w3b, w2b, sem):
    i = pl.program_id(0)
    n_used = nu_ref[0]
    n_ord = no_ref[0]

    def weight_copies(k):
        e = oe_ref[k]
        slot = k % W_SLOTS
        return (pltpu.make_async_copy(w1_hbm.at[e], w1b.at[slot], sem.at[0, slot]),
                pltpu.make_async_copy(w3_hbm.at[e], w3b.at[slot], sem.at[1, slot]),
                pltpu.make_async_copy(w2_hbm.at[e], w2b.at[slot], sem.at[2, slot]))

    def start_weights(k):
        for cp in weight_copies(k):
            cp.start(priority=1)

    @pl.when(i == 0)
    def _():
        for k in range(W_SLOTS - 1):
            @pl.when(k < n_ord)
            def _():
                start_weights(k)

    @pl.when(i < n_used)
    def _():
        k = ord_ref[i]

        @pl.when(first_ref[i] > 0)
        def _():
            for cp in weight_copies(k):
                cp.wait()

            @pl.when(k + (W_SLOTS - 1) < n_ord)
            def _():
                start_weights(k + (W_SLOTS - 1))

        slot = k % W_SLOTS
        x = _from_row_tiles(x_ref[...])
        h1 = _dot(x, w1b[slot].astype(BF16))
        h3 = _dot(x, w3b[slot].astype(BF16))
        he = (jax.nn.silu(h1) * h3).astype(BF16)
        y_ref[...] = _to_row_tiles(_dot(he, w2b[slot].astype(BF16)))

    @pl.when(i >= n_used)
    def _():
        y_ref[...] = jnp.zeros_like(y_ref)


def _experts(sched, xs, w_e1, w_e3, w_e2):
    grid_spec = pltpu.PrefetchScalarGridSpec(
        num_scalar_prefetch=5,
        grid=(MOE_NT,),
        in_specs=[
            pl.BlockSpec((MOE_TM, ROW_SUB, ROW_LANE), lambda i, nu, *_: (jnp.minimum(i, nu[0] - 1), 0, 0)),
            pl.BlockSpec(memory_space=pl.ANY),
            pl.BlockSpec(memory_space=pl.ANY),
            pl.BlockSpec(memory_space=pl.ANY),
        ],
        out_specs=pl.BlockSpec((MOE_TM, ROW_SUB, ROW_LANE), lambda i, *_: (i, 0, 0)),
        scratch_shapes=[
            pltpu.VMEM((W_SLOTS, D_MODEL, MOE_FF), F32),
            pltpu.VMEM((W_SLOTS, D_MODEL, MOE_FF), F32),
            pltpu.VMEM((W_SLOTS, MOE_FF, D_MODEL), F32),
            pltpu.SemaphoreType.DMA((3, W_SLOTS)),
        ],
    )
    return pl.pallas_call(
        _expert_kernel,
        grid_spec=grid_spec,
        out_shape=jax.ShapeDtypeStruct((MOE_ROWS, ROW_SUB, ROW_LANE), BF16),
        compiler_params=_params(1),
        name="experts",
    )(sched["n_used"], sched["tile_first"], sched["tile_ord"], sched["ord_expert"], sched["n_ord"],
      xs, w_e1, w_e3, w_e2)


def _combine_kernel(pos_ref, ys_ref, x2_ref, route_ref, g_ref, yp_ref, ysm_ref, gbuf, sem):
    i = pl.program_id(0)

    def start_gather(tile, slot):
        def body(r, carry):
            for k in range(MOE_TOPK):
                src_row = pos_ref[k, tile * ROW_TILE + r]
                pltpu.make_async_copy(ys_ref.at[src_row], gbuf.at[slot, k, r], sem.at[slot, k]).start(priority=k)
            return carry
        lax.fori_loop(0, ROW_TILE, body, 0, unroll=8)

    @pl.when(i == 0)
    def _():
        start_gather(0, 0)

    slot = i % 2

    @pl.when(i + 1 < pl.num_programs(0))
    def _():
        start_gather(i + 1, 1 - slot)

    for k in range(MOE_TOPK):
        pltpu.make_async_copy(ys_ref.at[pl.ds(0, ROW_TILE)], gbuf.at[slot, k], sem.at[slot, k]).wait()

    route = route_ref[...]
    lane = lax.broadcasted_iota(jnp.int32, route.shape, 1)
    w0 = jnp.sum(jnp.where(lane == 4, route, 0.0), axis=-1, keepdims=True)
    w1 = jnp.sum(jnp.where(lane == 5, route, 0.0), axis=-1, keepdims=True)
    g0 = _from_row_tiles(gbuf[slot, 0]).astype(F32)
    g1 = _from_row_tiles(gbuf[slot, 1]).astype(F32)
    x3 = x2_ref[...] + (g0 * w0 + g1 * w1)
    y = _rms(x3, g_ref[...])

    @pl.when(i < NP_TILES)
    def _():
        yp_ref[...] = y

    @pl.when(i == NP_TILES)
    def _():
        ysm_ref[...] = y


def _combine(pos, ys, x2, route, g_f):
    grid_spec = pltpu.PrefetchScalarGridSpec(
        num_scalar_prefetch=1,
        grid=(N_TILES,),
        in_specs=[
            pl.BlockSpec(memory_space=pl.ANY),
            pl.BlockSpec((ROW_TILE, D_MODEL), lambda i, pos: (i, 0)),
            pl.BlockSpec((ROW_TILE, ROUTE_LANES), lambda i, pos: (i, 0)),
            pl.BlockSpec((1, D_MODEL), lambda i, pos: (0, 0)),
        ],
        out_specs=[
            pl.BlockSpec((ROW_TILE, D_MODEL), lambda i, pos: (jnp.minimum(i, NP_TILES - 1), 0)),
            pl.BlockSpec((ROW_TILE, D_MODEL), lambda i, pos: (0, 0)),
        ],
        scratch_shapes=[pltpu.VMEM((2, MOE_TOPK, ROW_TILE, ROW_SUB, ROW_LANE), BF16),
                        pltpu.SemaphoreType.DMA((2, MOE_TOPK))],
    )
    return pl.pallas_call(
        _combine_kernel,
        grid_spec=grid_spec,
        out_shape=[
            jax.ShapeDtypeStruct((N_P, D_MODEL), F32),
            jax.ShapeDtypeStruct((N_S, D_MODEL), F32),
        ],
        compiler_params=_params(1),
        name="combine_norm",
    )(pos, ys, x2, route, g_f)


def _expert_schedule(counts):
    i32 = jnp.int32
    cnt = counts[0, MOE_GROUPS:MOE_GROUPS + MOE_EXPERTS].astype(i32)
    tiles = (cnt + MOE_TM - 1) // MOE_TM
    tile_end = jnp.cumsum(tiles)
    tile_start = tile_end - tiles
    n_used = tile_end[-1]
    t = jnp.arange(MOE_NT, dtype=i32)
    tile_expert = jnp.minimum(jnp.sum((t[:, None] >= tile_end[None, :]).astype(i32), axis=1), MOE_EXPERTS - 1)
    used = tiles > 0
    ord_of = jnp.cumsum(used.astype(i32)) - 1
    experts = jnp.arange(MOE_EXPERTS, dtype=i32)
    ord_expert = jnp.zeros((MOE_EXPERTS,), i32).at[jnp.where(used, ord_of, MOE_EXPERTS)].set(experts, mode="drop")
    return {
        "n_used": n_used.reshape(1).astype(i32),
        "tile_first": ((t == tile_start[tile_expert]) & (t < n_used)).astype(i32),
        "tile_ord": ord_of[tile_expert].astype(i32),
        "ord_expert": ord_expert,
        "n_ord": jnp.sum(used.astype(i32)).reshape(1),
        "zero_row": (jnp.maximum(tile_end - 1, 0) * MOE_TM).astype(i32),
        "zero_on": used.astype(i32),
    }


def kernel(x_prompt, x_sample, mem_prompt, state_ret, cache_mem_k, cache_mem_v, norm_mix, w_in, ret_gn,
           sg_ln_g, sg_ln_b, sg_ws, sg_bs, w_a_out, w_b_out, w_o, norm_xa, w_cq, w_ck, w_cv, w_co, norm_moe,
           w_rg, b_rg, w_re, b_re, w_e1, w_e3, w_e2, norm_f):
    xp = x_prompt.reshape(N_P, D_MODEL)
    xs = x_sample.reshape(N_S, D_MODEL)

    h = _norm_rows(xp, xs, norm_mix)
    w = w_in[0]
    qk = _inproj("rope", IN_TM_ROPE, 0, 2 * RET_QK, h, w, _rope_tables(IN_TM_ROPE))
    v = _inproj("copy", IN_TM, 2 * RET_QK, RET_V, h, w)
    gs = _inproj("silu", IN_TM, 2 * RET_QK + RET_V, RET_V, h, w)
    uv = _inproj("gelu", IN_TM, 2 * RET_QK + 2 * RET_V, 2 * SG_WIDTH, h, w)
    gab = _inproj("sigmoid", IN_TM, 2 * RET_QK + 2 * RET_V + 2 * SG_WIDTH, 2 * D_MODEL, h, w)

    a_p, ret_p, a_s, ret_s = _retention(qk, v, gs, ret_gn, state_ret)

    b_all, sgv = _sgate(uv, sg_ln_g, sg_ln_b, sg_ws[0], sg_bs[0])

    x1, qx = _merge(a_p, a_s, b_all, gab, xp, xs, w_a_out[0].astype(BF16), w_b_out[0].astype(BF16),
                    w_o[0].astype(BF16), norm_xa, w_cq[0].astype(BF16))

    mk, mv, mkb, mvb = _memkv(mem_prompt.reshape(BATCH * MEM_LEN, D_MODEL), w_ck[0], w_cv[0])
    o_p = _xattn_prompt(qx, mkb, mvb)
    ck = cache_mem_k.reshape(DEC_BATCH, MEM_LEN * XA_HEADS, XA_DH)
    cv = cache_mem_v.reshape(DEC_BATCH, MEM_LEN * XA_HEADS, XA_DH)

    w_r = jnp.zeros((D_MODEL, ROUTE_LANES), F32)
    w_r = w_r.at[:, :MOE_GROUPS].set(w_rg[0]).at[:, MOE_GROUPS:MOE_GROUPS + MOE_EXPERTS].set(w_re[0])
    b_r = jnp.zeros((1, ROUTE_LANES), F32)
    b_r = b_r.at[0, :MOE_GROUPS].set(b_rg[0]).at[0, MOE_GROUPS:MOE_GROUPS + MOE_EXPERTS].set(b_re[0])
    x2, hmw, route, counts = _route(o_p, qx, ck, cv, x1, w_co[0].astype(BF16), norm_moe, w_r.astype(BF16), b_r)

    pos_t = _positions(route, counts)
    sched = _expert_schedule(counts)
    xs_sorted = _dispatch(pos_t, sched["zero_row"], sched["zero_on"], sched["n_used"], hmw)
    ys = _experts(sched, xs_sorted, w_e1[0], w_e3[0], w_e2[0])
    y_p, y_s = _combine(pos_t, ys, x2, route, norm_f.reshape(1, D_MODEL))

    return (y_p.reshape(BATCH, SEQ, D_MODEL),
            y_s.reshape(DEC_BATCH, DEC_SEQ, D_MODEL),
            ret_p,
            mk.reshape(1, BATCH, MEM_LEN, XA_HEADS, XA_DH),
            mv.reshape(1, BATCH, MEM_LEN, XA_HEADS, XA_DH),
            ret_s,
            sgv.reshape(1, DEC_BATCH, DEC_SEQ, SG_WIDTH))
```

```python
import functools

import jax
import jax.numpy as jnp
from jax import lax
from jax.experimental import pallas as pl
from jax.experimental.pallas import tpu as pltpu

F32 = jnp.float32
BF16 = jnp.bfloat16

D_MODEL = 2048
BATCH = 4
SEQ = 2048
DEC_BATCH = 128
DEC_SEQ = 4
PAST_LEN = 16384
RET_HEADS = 8
RET_DK = 128
RET_DV = 256
RET_CHUNK = 128
ROPE_BASE = 10000.0
RET_QK = RET_HEADS * RET_DK
RET_V = RET_HEADS * RET_DV
SG_GROUPS = 4
SG_WIDTH = 2048
SG_CHUNK = 128
MEM_LEN = 256
XA_HEADS = 4
XA_DH = 128
XA_W = XA_HEADS * XA_DH
MOE_GROUPS = 4
MOE_PER_GROUP = 8
MOE_EXPERTS = MOE_GROUPS * MOE_PER_GROUP
MOE_TOPK = 2
MOE_FF = 512
EPS = 1e-6
IN_WIDTH = 2 * RET_QK + 2 * RET_V + 2 * SG_WIDTH + 2 * D_MODEL

N_P = BATCH * SEQ
N_S = DEC_BATCH * DEC_SEQ
N_ALL = N_P + N_S
ROW_TILE = 512
N_TILES = N_ALL // ROW_TILE
NP_TILES = N_P // ROW_TILE
MERGE_TILE = 256
MERGE_P_TILES = N_P // MERGE_TILE

IN_TM = 1024
IN_TM_ROPE = 1024
IN_TN = 1024

PROMPT_CHUNK = 256
RET_BB = DEC_BATCH // (BATCH * (SEQ // PROMPT_CHUNK))
RET_ROWS = RET_BB * DEC_SEQ
LOG_DEC_SEQ = 2
LOG_SG_CHUNK = 7
SAMPLE_BB = DEC_BATCH // NP_TILES
SAMPLE_ROWS = SAMPLE_BB * DEC_SEQ

MOE_TM = 256
MOE_NT = (N_ALL * MOE_TOPK + MOE_EXPERTS * (MOE_TM - 1) + MOE_TM - 1) // MOE_TM
MOE_ROWS = MOE_NT * MOE_TM
ROUTE_LANES = 128
ROW_SUB, ROW_LANE = 16, 128
W_SLOTS = 3

VMEM_LIMIT = 56 * 1024 * 1024


def _params(n_axes, vmem=VMEM_LIMIT):
    return pltpu.CompilerParams(dimension_semantics=("arbitrary",) * n_axes,
                                vmem_limit_bytes=vmem)


def _rms(x, g):
    ms = jnp.mean(x * x, axis=-1, keepdims=True)
    return (x * lax.rsqrt(ms + EPS)) * g


def _dot(a, b):
    return jnp.dot(a, b, preferred_element_type=F32)


def _sigmoid(x):
    return 0.5 * jnp.tanh(0.5 * x) + 0.5


def _to_row_tiles(x):
    return x.astype(BF16).reshape(x.shape[0], ROW_SUB, ROW_LANE)


def _from_row_tiles(t):
    return t.reshape(t.shape[0], D_MODEL)


def _dot_nt(a, b):
    return lax.dot_general(a, b, (((1,), (1,)), ((), ())), preferred_element_type=F32)


def _dot_tn(a, b):
    return lax.dot_general(a, b, (((0,), (0,)), ((), ())), preferred_element_type=F32)


def _norm_kernel(xp_ref, xs_ref, g_ref, h_ref):
    i = pl.program_id(0)

    @pl.when(i < NP_TILES)
    def _():
        h_ref[...] = _rms(xp_ref[...], g_ref[...]).astype(BF16)

    @pl.when(i == NP_TILES)
    def _():
        h_ref[...] = _rms(xs_ref[...], g_ref[...]).astype(BF16)


def _norm_rows(xp, xs, g):
    return pl.pallas_call(
        _norm_kernel,
        grid=(N_TILES,),
        in_specs=[
            pl.BlockSpec((ROW_TILE, D_MODEL), lambda i: (jnp.minimum(i, NP_TILES - 1), 0)),
            pl.BlockSpec((ROW_TILE, D_MODEL), lambda i: (0, 0)),
            pl.BlockSpec((1, D_MODEL), lambda i: (0, 0)),
        ],
        out_specs=pl.BlockSpec((ROW_TILE, D_MODEL), lambda i: (i, 0)),
        out_shape=jax.ShapeDtypeStruct((N_ALL, D_MODEL), BF16),
        compiler_params=_params(1),
        name="norm_rows",
    )(xp, xs, g)


def _inproj_kernel(kind, tm, h_ref, w_ref, *rest):
    if kind == "rope":
        cos_ref, sin_ref, z_ref, wb_ref = rest
    else:
        z_ref, wb_ref = rest
    j = pl.program_id(0)
    i = pl.program_id(1)
    last = N_P // tm

    @pl.when(i == 0)
    def _():
        wb_ref[...] = w_ref[...].astype(BF16)

    def tile(rows):
        acc = _dot(h_ref[0:rows, :], wb_ref[...])
        if kind == "rope":
            scale = jnp.where(j == 1, RET_DK ** -0.5, 1.0).astype(F32)
            c = cos_ref[0:rows, :]
            s = sin_ref[0:rows, :]
            for hb in range(IN_TN // RET_DK):
                cols = slice(hb * RET_DK, (hb + 1) * RET_DK)
                a = acc[:, cols]
                r = pltpu.roll(a, RET_DK // 2, axis=1)
                z_ref[0:rows, cols] = ((a * c + r * s) * scale).astype(BF16)
        elif kind == "copy":
            z_ref[0:rows, :] = acc.astype(BF16)
        elif kind == "silu":
            z_ref[0:rows, :] = (acc * _sigmoid(acc)).astype(BF16)
        elif kind == "gelu":
            z_ref[0:rows, :] = jax.nn.gelu(acc).astype(BF16)
        else:
            z_ref[0:rows, :] = _sigmoid(acc).astype(BF16)

    @pl.when(i < last)
    def _():
        tile(tm)

    @pl.when(i == last)
    def _():
        tile(N_S)


def _inproj(kind, tm, col0, width, h, w_in, tables=()):
    last = N_P // tm
    tab_idx = lambda j, i: (jnp.where(i < last, i % (SEQ // tm), SEQ // tm), 0)
    j0 = col0 // IN_TN
    return pl.pallas_call(
        functools.partial(_inproj_kernel, kind, tm),
        grid=(width // IN_TN, last + 1),
        in_specs=[
            pl.BlockSpec((tm, D_MODEL), lambda j, i: (i, 0)),
            pl.BlockSpec((D_MODEL, IN_TN), lambda j, i: (0, j0 + j)),
        ] + [pl.BlockSpec((tm, RET_DK), tab_idx) for _ in tables],
        out_specs=pl.BlockSpec((tm, IN_TN), lambda j, i: (i, j)),
        out_shape=jax.ShapeDtypeStruct((N_ALL, width), BF16),
        scratch_shapes=[pltpu.VMEM((D_MODEL, IN_TN), BF16)],
        compiler_params=_params(2),
        name="in_proj_" + kind,
    )(h, w_in, *tables)


def _rope_tables(tm):
    half = RET_DK // 2
    inv = ROPE_BASE ** (-jnp.arange(half, dtype=F32) / half)

    def tab(pos):
        ang = pos.astype(F32)[:, None] * inv[None, :]
        c, s = jnp.cos(ang), jnp.sin(ang)
        return jnp.concatenate([c, c], -1), jnp.concatenate([-s, s], -1)

    cp, sp = tab(jnp.arange(SEQ, dtype=jnp.int32))
    cs, ss = tab(PAST_LEN + jnp.arange(DEC_SEQ, dtype=jnp.int32))
    cs = jnp.tile(cs, (DEC_BATCH, 1))
    ss = jnp.tile(ss, (DEC_BATCH, 1))
    pad = jnp.zeros((tm - N_S, RET_DK), F32)
    return jnp.concatenate([cp, cs, pad], 0), jnp.concatenate([sp, ss, pad], 0)


def _decay_tables(chunk):
    lg = jnp.log1p(-jnp.power(2.0, -5.0 - jnp.arange(RET_HEADS, dtype=F32)))
    idx = jnp.arange(chunk, dtype=F32)
    rel = idx[:, None] - idx[None, :]
    dmask = jnp.where(rel >= 0, jnp.exp(lg[:, None, None] * jnp.maximum(rel, 0.0)), 0.0).astype(F32)
    xi = jnp.exp(lg[:, None] * (idx[None, :] + 1.0)).astype(F32)
    zeta = jnp.exp(lg[:, None] * (chunk - 1.0 - idx[None, :])).astype(F32)
    gc = jnp.exp(lg * chunk).astype(F32)
    return dmask, xi, zeta, gc


def _head_norm_gate(o, gn, gate):
    mu = jnp.mean(o, axis=-1, keepdims=True)
    d = o - mu
    var = jnp.mean(d * d, axis=-1, keepdims=True)
    y = (d * lax.rsqrt(var + EPS)) * gn
    return (gate.astype(F32) * y).astype(BF16)


def _ret_kernel(gcp_ref, gcs_ref, q_ref, k_ref, v_ref, gs_ref, gn_ref, dm_ref, xi_ref, zt_ref,
                qs_ref, ks_ref, vs_ref, gss_ref, dms_ref, xis_ref, zts_ref, s0_ref,
                a_ref, sfin_ref, as_ref, s1_ref, s_ref):
    c = pl.program_id(1)

    @pl.when(c == 0)
    def _():
        s_ref[...] = jnp.zeros_like(s_ref)

    rows_k = lax.broadcasted_iota(jnp.int32, (RET_ROWS, RET_DK), 0) >> LOG_DEC_SEQ
    rows_v = lax.broadcasted_iota(jnp.int32, (RET_ROWS, RET_DV), 0) >> LOG_DEC_SEQ
    for h in range(RET_HEADS):
        kc = slice(h * RET_DK, (h + 1) * RET_DK)
        vc = slice(h * RET_DV, (h + 1) * RET_DV)
        gn = gn_ref[:, vc]

        qh = q_ref[:, kc]
        kh = k_ref[:, kc]
        vh = v_ref[:, vc]
        inner = _dot_nt(qh, kh) * dm_ref[h]
        o = _dot(inner.astype(BF16), vh)
        s_old = s_ref[h]
        xi = xi_ref[h]
        o = o + _dot(qh, s_old.astype(BF16)) * jnp.concatenate([xi, xi], axis=1)
        kz = (kh.astype(F32) * zt_ref[h]).astype(BF16)
        s_ref[h] = gcp_ref[h] * s_old + _dot_tn(kz, vh)
        a_ref[:, vc] = _head_norm_gate(o, gn, gs_ref[:, vc])

        qh = qs_ref[:, kc]
        kh = ks_ref[:, kc]
        vh = vs_ref[:, vc]
        inner = _dot_nt(qh, kh) * dms_ref[h]
        o = _dot(inner.astype(BF16), vh)
        xi = xis_ref[h]
        xi2 = jnp.concatenate([xi, xi], axis=1)
        kz = kh.astype(F32) * zts_ref[h]
        gch = gcs_ref[h]
        for b in range(RET_BB):
            s_old = s0_ref[0, b, h]
            cross = _dot(qh, s_old.astype(BF16)) * xi2
            o = o + jnp.where(rows_v == b, cross, 0.0)
            kz_b = jnp.where(rows_k == b, kz, 0.0).astype(BF16)
            s1_ref[0, b, h] = gch * s_old + _dot_tn(kz_b, vh)
        as_ref[:, vc] = _head_norm_gate(o, gn, gss_ref[:, vc])

    @pl.when(c == pl.num_programs(1) - 1)
    def _():
        sfin_ref[0, 0] = s_ref[...]


def _retention(qk, v, gs, ret_gn, state):
    chunk = PROMPT_CHUNK
    n_chunks = SEQ // chunk
    dmask, xi, zeta, gc = _decay_tables(chunk)
    xi_b = jnp.broadcast_to(xi[:, :, None], (RET_HEADS, chunk, RET_DK))
    zeta_b = jnp.broadcast_to(zeta[:, :, None], (RET_HEADS, chunk, RET_DK))
    dmask_s, xi_s, zeta_s, gc_s = _decay_tables(DEC_SEQ)
    eye = jnp.eye(RET_BB, dtype=F32)
    dm_big = jax.vmap(lambda m: jnp.kron(eye, m))(dmask_s)
    xi_sb = jnp.broadcast_to(jnp.tile(xi_s, (1, RET_BB))[:, :, None], (RET_HEADS, RET_ROWS, RET_DK))
    zeta_sb = jnp.broadcast_to(jnp.tile(zeta_s, (1, RET_BB))[:, :, None], (RET_HEADS, RET_ROWS, RET_DK))

    row = lambda b, c: b * n_chunks + c
    srow = lambda b, c: N_P // RET_ROWS + row(b, c)
    const3 = lambda b, c: (0, 0, 0)
    smem = pl.BlockSpec(memory_space=pltpu.SMEM)
    st_spec = pl.BlockSpec((1, RET_BB, RET_HEADS, RET_DK, RET_DV), lambda b, c: (0, row(b, c), 0, 0, 0))
    return pl.pallas_call(
        _ret_kernel,
        grid=(BATCH, n_chunks),
        in_specs=[
            smem, smem,
            pl.BlockSpec((chunk, RET_QK), lambda b, c: (row(b, c), 0)),
            pl.BlockSpec((chunk, RET_QK), lambda b, c: (row(b, c), 1)),
            pl.BlockSpec((chunk, RET_V), lambda b, c: (row(b, c), 0)),
            pl.BlockSpec((chunk, RET_V), lambda b, c: (row(b, c), 0)),
            pl.BlockSpec((1, RET_V), lambda b, c: (0, 0)),
            pl.BlockSpec((RET_HEADS, chunk, chunk), const3),
            pl.BlockSpec((RET_HEADS, chunk, RET_DK), const3),
            pl.BlockSpec((RET_HEADS, chunk, RET_DK), const3),
            pl.BlockSpec((RET_ROWS, RET_QK), lambda b, c: (srow(b, c), 0)),
            pl.BlockSpec((RET_ROWS, RET_QK), lambda b, c: (srow(b, c), 1)),
            pl.BlockSpec((RET_ROWS, RET_V), lambda b, c: (srow(b, c), 0)),
            pl.BlockSpec((RET_ROWS, RET_V), lambda b, c: (srow(b, c), 0)),
            pl.BlockSpec((RET_HEADS, RET_ROWS, RET_ROWS), const3),
            pl.BlockSpec((RET_HEADS, RET_ROWS, RET_DK), const3),
            pl.BlockSpec((RET_HEADS, RET_ROWS, RET_DK), const3),
            st_spec,
        ],
        out_specs=[
            pl.BlockSpec((chunk, RET_V), lambda b, c: (row(b, c), 0)),
            pl.BlockSpec((1, 1, RET_HEADS, RET_DK, RET_DV), lambda b, c: (0, b, 0, 0, 0)),
            pl.BlockSpec((RET_ROWS, RET_V), lambda b, c: (row(b, c), 0)),
            st_spec,
        ],
        out_shape=[
            jax.ShapeDtypeStruct((N_P, RET_V), BF16),
            jax.ShapeDtypeStruct((1, BATCH, RET_HEADS, RET_DK, RET_DV), F32),
            jax.ShapeDtypeStruct((N_S, RET_V), BF16),
            jax.ShapeDtypeStruct((1, DEC_BATCH, RET_HEADS, RET_DK, RET_DV), F32),
        ],
        scratch_shapes=[pltpu.VMEM((RET_HEADS, RET_DK, RET_DV), F32)],
        compiler_params=_params(2),
        name="retention",
    )(gc, gc_s, qk, qk, v, gs, ret_gn, dmask, xi_b, zeta_b, qk, qk, v, gs, dm_big, xi_sb, zeta_sb, state)


def _sgate_kernel(u_ref, v_ref, lg_ref, lb_ref, wp_ref, bp_ref, ws_ref, bs_ref, b_ref, sgv_ref, vln_ref):
    i = pl.program_id(0)
    v = v_ref[...].astype(F32)
    mu = jnp.mean(v, axis=-1, keepdims=True)
    d = v - mu
    var = jnp.mean(d * d, axis=-1, keepdims=True)
    vln_ref[...] = (d * lax.rsqrt(var + EPS)) * lg_ref[...] + lb_ref[...]
    gw = SG_WIDTH // SG_GROUPS
    lane_reps = gw // 128

    @pl.when(i < NP_TILES)
    def _():
        r = lax.broadcasted_iota(jnp.int32, (SG_CHUNK, SG_CHUNK), 0)
        c = lax.broadcasted_iota(jnp.int32, (SG_CHUNK, SG_CHUNK), 1)
        for g in range(SG_GROUPS):
            cols = slice(g * gw, (g + 1) * gw)
            w = jnp.where(c <= r, wp_ref[g], 0.0).astype(BF16)
            bias = jnp.concatenate([bp_ref[g]] * lane_reps, axis=1)
            for ch in range(ROW_TILE // SG_CHUNK):
                rows = slice(ch * SG_CHUNK, (ch + 1) * SG_CHUNK)
                mixed = _dot(w, vln_ref[rows, cols].astype(BF16)) + bias
                b_ref[rows, cols] = (u_ref[rows, cols].astype(F32) * mixed).astype(BF16)

    @pl.when(i == NP_TILES)
    def _():
        sgv_ref[...] = vln_ref[...]
        r = lax.broadcasted_iota(jnp.int32, (ROW_TILE, ROW_TILE), 0)
        c = lax.broadcasted_iota(jnp.int32, (ROW_TILE, ROW_TILE), 1)
        keep = ((r >> LOG_DEC_SEQ) == (c >> LOG_DEC_SEQ)) & (c <= r)
        for g in range(SG_GROUPS):
            cols = slice(g * gw, (g + 1) * gw)
            w_rows = jnp.concatenate([ws_ref[g]] * (ROW_TILE // 8), axis=0)
            w_full = jnp.concatenate([w_rows] * (ROW_TILE // 128), axis=1)
            w = jnp.where(keep, w_full, 0.0).astype(BF16)
            b_rows = jnp.concatenate([bs_ref[g]] * (ROW_TILE // 8), axis=0)
            bias = jnp.concatenate([b_rows] * lane_reps, axis=1)
            mixed = _dot(w, vln_ref[:, cols].astype(BF16)) + bias
            b_ref[:, cols] = (u_ref[:, cols].astype(F32) * mixed).astype(BF16)


def _sgate(uv, ln_g, ln_b, sg_ws, sg_bs):
    b_p = jnp.broadcast_to(sg_bs[:, :, None], (SG_GROUPS, SG_CHUNK, 128))
    w_s = jnp.tile(sg_ws[:, :DEC_SEQ, :DEC_SEQ], (1, 8 // DEC_SEQ, 128 // DEC_SEQ))
    b_s = jnp.broadcast_to(jnp.tile(sg_bs[:, :DEC_SEQ], (1, 8 // DEC_SEQ))[:, :, None], (SG_GROUPS, 8, 128))
    const3 = lambda i: (0, 0, 0)
    return pl.pallas_call(
        _sgate_kernel,
        grid=(N_TILES,),
        in_specs=[
            pl.BlockSpec((ROW_TILE, SG_WIDTH), lambda i: (i, 0)),
            pl.BlockSpec((ROW_TILE, SG_WIDTH), lambda i: (i, 1)),
            pl.BlockSpec((1, SG_WIDTH), lambda i: (0, 0)),
            pl.BlockSpec((1, SG_WIDTH), lambda i: (0, 0)),
            pl.BlockSpec((SG_GROUPS, SG_CHUNK, SG_CHUNK), const3),
            pl.BlockSpec((SG_GROUPS, SG_CHUNK, 128), const3),
            pl.BlockSpec((SG_GROUPS, 8, 128), const3),
            pl.BlockSpec((SG_GROUPS, 8, 128), const3),
        ],
        out_specs=[
            pl.BlockSpec((ROW_TILE, SG_WIDTH), lambda i: (i, 0)),
            pl.BlockSpec((N_S, SG_WIDTH), lambda i: (0, 0)),
        ],
        out_shape=[
            jax.ShapeDtypeStruct((N_ALL, SG_WIDTH), BF16),
            jax.ShapeDtypeStruct((N_S, SG_WIDTH), F32),
        ],
        scratch_shapes=[pltpu.VMEM((ROW_TILE, SG_WIDTH), F32)],
        compiler_params=_params(1),
        name="spatial_gate",
    )(uv, uv, ln_g, ln_b, sg_ws, b_p, w_s, b_s)


def _merge_kernel(ap_ref, as_ref, b_ref, ga_ref, gb_ref, xp_ref, xs_ref, wa_ref, wb_ref, wo_ref,
                  g_ref, wq_ref, x1_ref, q_ref):
    i = pl.program_id(0)

    def run(a_in, x):
        a = _dot(a_in, wa_ref[...])
        b = _dot(b_ref[...], wb_ref[...])
        merged = ga_ref[...].astype(F32) * a + gb_ref[...].astype(F32) * b
        x1 = x + _dot(merged.astype(BF16), wo_ref[...])
        x1_ref[...] = x1
        q_ref[...] = _dot(_rms(x1, g_ref[...]).astype(BF16), wq_ref[...]).astype(BF16)

    @pl.when(i < MERGE_P_TILES)
    def _():
        run(ap_ref[...], xp_ref[...])

    @pl.when(i >= MERGE_P_TILES)
    def _():
        run(as_ref[...], xs_ref[...])


def _resident(shape):
    return pl.BlockSpec(shape, lambda i: (0,) * len(shape), pipeline_mode=pl.Buffered(1))


def _merge(a_p, a_s, b_all, gab, xp, xs, wa, wb, wo, g_xa, wq):
    tm = MERGE_TILE
    prompt_tile = lambda i: (jnp.minimum(i, MERGE_P_TILES - 1), 0)
    sample_tile = lambda i: (jnp.maximum(i - MERGE_P_TILES, 0), 0)
    return pl.pallas_call(
        _merge_kernel,
        grid=(N_ALL // tm,),
        in_specs=[
            pl.BlockSpec((tm, RET_V), prompt_tile),
            pl.BlockSpec((tm, RET_V), sample_tile),
            pl.BlockSpec((tm, SG_WIDTH), lambda i: (i, 0)),
            pl.BlockSpec((tm, D_MODEL), lambda i: (i, 0)),
            pl.BlockSpec((tm, D_MODEL), lambda i: (i, 1)),
            pl.BlockSpec((tm, D_MODEL), prompt_tile),
            pl.BlockSpec((tm, D_MODEL), sample_tile),
            _resident((RET_V, D_MODEL)),
            _resident((SG_WIDTH, D_MODEL)),
            _resident((D_MODEL, D_MODEL)),
            pl.BlockSpec((1, D_MODEL), lambda i: (0, 0)),
            _resident((D_MODEL, XA_W)),
        ],
        out_specs=[
            pl.BlockSpec((tm, D_MODEL), lambda i: (i, 0)),
            pl.BlockSpec((tm, XA_W), lambda i: (i, 0)),
        ],
        out_shape=[
            jax.ShapeDtypeStruct((N_ALL, D_MODEL), F32),
            jax.ShapeDtypeStruct((N_ALL, XA_W), BF16),
        ],
        compiler_params=_params(1),
        name="merge_proj",
    )(a_p, a_s, b_all, gab, gab, xp, xs, wa, wb, wo, g_xa, wq)


def _memkv_kernel(m_ref, wk_ref, wv_ref, k_ref, v_ref, kb_ref, vb_ref):
    m = m_ref[...].astype(BF16)
    k = _dot(m, wk_ref[...].astype(BF16))
    v = _dot(m, wv_ref[...].astype(BF16))
    k_ref[...] = k
    v_ref[...] = v
    kb_ref[...] = k.astype(BF16)
    vb_ref[...] = v.astype(BF16)


def _memkv(mem, w_ck, w_cv):
    rows = BATCH * MEM_LEN
    spec = pl.BlockSpec((MEM_LEN, XA_W), lambda b: (b, 0))
    wspec = pl.BlockSpec((D_MODEL, XA_W), lambda b: (0, 0))
    return pl.pallas_call(
        _memkv_kernel,
        grid=(BATCH,),
        in_specs=[pl.BlockSpec((MEM_LEN, D_MODEL), lambda b: (b, 0)), wspec, wspec],
        out_specs=[spec, spec, spec, spec],
        out_shape=[jax.ShapeDtypeStruct((rows, XA_W), F32)] * 2 + [jax.ShapeDtypeStruct((rows, XA_W), BF16)] * 2,
        compiler_params=_params(1),
        name="mem_kv",
    )(mem, w_ck, w_cv)


def _softmax_rows(s):
    m = jnp.max(s, axis=-1, keepdims=True)
    e = jnp.exp(s - m)
    return e / jnp.sum(e, axis=-1, keepdims=True)


def _xattn_prompt_kernel(q_ref, k_ref, v_ref, o_ref):
    for h in range(XA_HEADS):
        cols = slice(h * XA_DH, (h + 1) * XA_DH)
        s = _dot_nt(q_ref[:, cols], k_ref[:, cols]) * (XA_DH ** -0.5)
        p = _softmax_rows(s)
        o_ref[:, cols] = _dot(p.astype(BF16), v_ref[:, cols]).astype(BF16)


def _xattn_prompt(q, kb, vb):
    per_b = SEQ // ROW_TILE
    kv = pl.BlockSpec((MEM_LEN, XA_W), lambda i: (i // per_b, 0))
    return pl.pallas_call(
        _xattn_prompt_kernel,
        grid=(NP_TILES,),
        in_specs=[pl.BlockSpec((ROW_TILE, XA_W), lambda i: (i, 0)), kv, kv],
        out_specs=pl.BlockSpec((ROW_TILE, XA_W), lambda i: (i, 0)),
        out_shape=jax.ShapeDtypeStruct((N_P, XA_W), BF16),
        compiler_params=_params(1),
        name="xattn_prompt",
    )(q, kb, vb)


def _xattn_sample_entries(q_ref, k_ref, v_ref, o_ref):
    n_q = XA_HEADS * DEC_SEQ
    n_kv = MEM_LEN * XA_HEADS
    qf = q_ref[...].astype(F32)
    r = lax.broadcasted_iota(jnp.int32, (n_q, n_kv), 0)
    c = lax.broadcasted_iota(jnp.int32, (n_q, n_kv), 1)
    head_ok = (r >> LOG_DEC_SEQ) == (c & (XA_HEADS - 1))
    for b in range(SAMPLE_BB):
        rows = slice(b * DEC_SEQ, (b + 1) * DEC_SEQ)
        qb = jnp.concatenate([qf[rows, h * XA_DH:(h + 1) * XA_DH] for h in range(XA_HEADS)], axis=0)
        s = _dot_nt(qb.astype(BF16), k_ref[b].astype(BF16)) * (XA_DH ** -0.5)
        p = _softmax_rows(jnp.where(head_ok, s, -1e30))
        o = _dot(p.astype(BF16), v_ref[b].astype(BF16))
        for h in range(XA_HEADS):
            o_ref[h, rows, :] = o[h * DEC_SEQ:(h + 1) * DEC_SEQ, :]


def _route_kernel(op_ref, qs_ref, ck_ref, cv_ref, x1_ref, wo_ref, g_ref, wr_ref, br_ref,
                  x2_ref, hm_ref, route_ref, cnt_ref, run_ref, os_ref):
    i = pl.program_id(0)

    @pl.when(i == 0)
    def _():
        run_ref[...] = jnp.zeros_like(run_ref)

    def run(o):
        x2 = x1_ref[...] + _dot(o, wo_ref[...])
        x2_ref[...] = x2
        hm = _rms(x2, g_ref[...])
        hm_ref[...] = _to_row_tiles(hm)
        logits = _dot(hm.astype(BF16), wr_ref[...]) + br_ref[...]
        lane = lax.broadcasted_iota(jnp.int32, logits.shape, 1)
        neg = jnp.float32(-jnp.inf)
        big = jnp.int32(1 << 20)
        is_g = lane < MOE_GROUPS
        gl = jnp.where(is_g, logits, neg)
        gmax = jnp.max(gl, axis=-1, keepdims=True)
        g_sel = jnp.min(jnp.where(gl == gmax, lane, big), axis=-1, keepdims=True)
        g_w = 1.0 / jnp.sum(jnp.where(is_g, jnp.exp(logits - gmax), 0.0), axis=-1, keepdims=True)
        e_lane = lane - MOE_GROUPS
        in_grp = (e_lane >= 0) & (e_lane < MOE_EXPERTS) & ((e_lane >> 3) == g_sel)
        el = jnp.where(in_grp, logits, neg)
        v0 = jnp.max(el, axis=-1, keepdims=True)
        i0 = jnp.min(jnp.where(el == v0, lane, big), axis=-1, keepdims=True)
        el1 = jnp.where(lane == i0, neg, el)
        v1 = jnp.max(el1, axis=-1, keepdims=True)
        i1 = jnp.min(jnp.where(el1 == v1, lane, big), axis=-1, keepdims=True)
        ex = jnp.exp(v1 - v0)
        den = 1.0 + ex
        w0 = (1.0 / den) * g_w
        w1 = (ex / den) * g_w
        a0 = (lane == i0).astype(F32)
        a1 = (lane == i1).astype(F32)
        a = a0 + a1
        rr = lax.broadcasted_iota(jnp.int32, (ROW_TILE, ROW_TILE), 0)
        cc = lax.broadcasted_iota(jnp.int32, (ROW_TILE, ROW_TILE), 1)
        lower = jnp.where(cc < rr, 1.0, 0.0).astype(BF16)
        before = _dot(lower, a.astype(BF16)) + run_ref[...]
        rank0 = jnp.sum(before * a0, axis=-1, keepdims=True)
        rank1 = jnp.sum(before * a1, axis=-1, keepdims=True)
        run_ref[...] += jnp.sum(a, axis=0, keepdims=True)
        e0 = (i0 - MOE_GROUPS).astype(F32)
        e1 = (i1 - MOE_GROUPS).astype(F32)
        route = jnp.where(lane == 0, e0, 0.0)
        route = jnp.where(lane == 1, e1, route)
        route = jnp.where(lane == 2, rank0, route)
        route = jnp.where(lane == 3, rank1, route)
        route = jnp.where(lane == 4, w0, route)
        route = jnp.where(lane == 5, w1, route)
        route_ref[...] = route
        cnt_ref[...] = run_ref[...]

    @pl.when(i < NP_TILES)
    def _():
        _xattn_sample_entries(qs_ref, ck_ref, cv_ref, os_ref.at[i])
        run(op_ref[...])

    @pl.when(i == NP_TILES)
    def _():
        heads = [os_ref[:, h].reshape(N_S, XA_DH) for h in range(XA_HEADS)]
        run(jnp.concatenate(heads, axis=1).astype(BF16))


def _route(o_p, qx, ck, cv, x1, w_co, g_moe, w_r, b_r):
    prompt_step = lambda i: jnp.minimum(i, NP_TILES - 1)
    kv = pl.BlockSpec((SAMPLE_BB, MEM_LEN * XA_HEADS, XA_DH), lambda i: (prompt_step(i), 0, 0))
    return pl.pallas_call(
        _route_kernel,
        grid=(N_TILES,),
        in_specs=[
            pl.BlockSpec((ROW_TILE, XA_W), lambda i: (prompt_step(i), 0)),
            pl.BlockSpec((SAMPLE_ROWS, XA_W), lambda i: (N_P // SAMPLE_ROWS + prompt_step(i), 0)),
            kv, kv,
            pl.BlockSpec((ROW_TILE, D_MODEL), lambda i: (i, 0)),
            _resident((XA_W, D_MODEL)),
            pl.BlockSpec((1, D_MODEL), lambda i: (0, 0)),
            _resident((D_MODEL, ROUTE_LANES)),
            pl.BlockSpec((1, ROUTE_LANES), lambda i: (0, 0)),
        ],
        out_specs=[
            pl.BlockSpec((ROW_TILE, D_MODEL), lambda i: (i, 0)),
            pl.BlockSpec((ROW_TILE, ROW_SUB, ROW_LANE), lambda i: (i, 0, 0)),
            pl.BlockSpec((ROW_TILE, ROUTE_LANES), lambda i: (i, 0)),
            pl.BlockSpec((1, ROUTE_LANES), lambda i: (0, 0)),
        ],
        out_shape=[
            jax.ShapeDtypeStruct((N_ALL, D_MODEL), F32),
            jax.ShapeDtypeStruct((N_ALL, ROW_SUB, ROW_LANE), BF16),
            jax.ShapeDtypeStruct((N_ALL, ROUTE_LANES), F32),
            jax.ShapeDtypeStruct((1, ROUTE_LANES), F32),
        ],
        scratch_shapes=[pltpu.VMEM((1, ROUTE_LANES), F32),
                        pltpu.VMEM((NP_TILES, XA_HEADS, SAMPLE_ROWS, XA_DH), F32)],
        compiler_params=_params(1),
        name="xa_out_route",
    )(o_p, qx, ck, cv, x1, w_co, g_moe, w_r, b_r)


def _positions_kernel(route_ref, cnt_ref, pos_ref):
    route = route_ref[...]
    lane = lax.broadcasted_iota(jnp.int32, route.shape, 1)
    lane_f = lane.astype(F32)
    tiles = jnp.floor((cnt_ref[...] + (MOE_TM - 1)) * (1.0 / MOE_TM))
    lr = lax.broadcasted_iota(jnp.int32, (ROUTE_LANES, ROUTE_LANES), 0)
    lc = lax.broadcasted_iota(jnp.int32, (ROUTE_LANES, ROUTE_LANES), 1)
    before = jnp.where(lr < lc, 1.0, 0.0).astype(BF16)
    tiles8 = jnp.broadcast_to(tiles, (8, ROUTE_LANES)).astype(BF16)
    start = _dot(tiles8, before)[0:1, :] * MOE_TM

    def col(k):
        return jnp.sum(jnp.where(lane == k, route, 0.0), axis=-1, keepdims=True)

    def first_row(e):
        return jnp.sum(jnp.where(lane_f == e + MOE_GROUPS, start, 0.0), axis=-1, keepdims=True)

    p0 = first_row(col(0)) + col(2)
    p1 = first_row(col(1)) + col(3)
    p = jnp.where(lane == 0, p0, jnp.where(lane == 1, p1, 0.0))
    pos_ref[...] = p.T[0:8, :].astype(jnp.int32)


def _positions(route, counts):
    rows = N_ALL // 4
    return pl.pallas_call(
        _positions_kernel,
        grid=(4,),
        in_specs=[
            pl.BlockSpec((rows, ROUTE_LANES), lambda i: (i, 0)),
            pl.BlockSpec((1, ROUTE_LANES), lambda i: (0, 0)),
        ],
        out_specs=pl.BlockSpec((8, rows), lambda i: (0, i)),
        out_shape=jax.ShapeDtypeStruct((8, N_ALL), jnp.int32),
        compiler_params=_params(1),
        name="positions",
    )(route, counts)


def _dispatch_kernel(pos_ref, zrow_ref, zon_ref, nu_ref, hm_ref, xs_ref, zbuf, sem, zsem, tbuf, tsem):
    i = pl.program_id(0)

    def zero_tile(row):
        return pltpu.make_async_copy(zbuf, xs_ref.at[pl.ds(pl.multiple_of(row, MOE_TM), MOE_TM)], zsem)

    @pl.when(i == 0)
    def _():
        zbuf[...] = jnp.zeros_like(zbuf)
        for e in range(MOE_EXPERTS):
            @pl.when(zon_ref[e] > 0)
            def _():
                zero_tile(zrow_ref[e]).start()

        def start_tail(t, carry):
            zero_tile(t * MOE_TM).start()
            return carry

        def wait_tail(t, carry):
            zero_tile(t * MOE_TM).wait()
            return carry

        lax.fori_loop(nu_ref[0], MOE_NT, start_tail, 0)
        for e in range(MOE_EXPERTS):
            @pl.when(zon_ref[e] > 0)
            def _():
                zero_tile(zrow_ref[e]).wait()
        lax.fori_loop(nu_ref[0], MOE_NT, wait_tail, 0)

    n_steps = pl.num_programs(0)

    def tile_copy(t):
        return pltpu.make_async_copy(hm_ref.at[pl.ds(t * ROW_TILE, ROW_TILE)], tbuf.at[t % 3], tsem.at[t % 3])

    def wait_rows(t):
        for k in range(MOE_TOPK):
            pltpu.make_async_copy(tbuf.at[t % 3], xs_ref.at[pl.ds(0, ROW_TILE)], sem.at[t % 2, k]).wait()

    @pl.when(i == 0)
    def _():
        tile_copy(0).start()

    tile_copy(i).wait()

    @pl.when(i + 1 < n_steps)
    def _():
        tile_copy(i + 1).start()

    slot = i % 3
    par = i % 2
    base = i * ROW_TILE

    def start(r, carry):
        for k in range(MOE_TOPK):
            dst_row = pos_ref[k, base + r]
            pltpu.make_async_copy(tbuf.at[slot, r], xs_ref.at[dst_row], sem.at[par, k]).start(priority=k)
        return carry

    lax.fori_loop(0, ROW_TILE, start, 0, unroll=8)

    @pl.when(i > 0)
    def _():
        wait_rows(i - 1)

    @pl.when(i == n_steps - 1)
    def _():
        wait_rows(i)


def _dispatch(pos_t, zero_row, zero_on, n_used, hmw):
    grid_spec = pltpu.PrefetchScalarGridSpec(
        num_scalar_prefetch=4,
        grid=(N_TILES,),
        in_specs=[pl.BlockSpec(memory_space=pl.ANY)],
        out_specs=pl.BlockSpec(memory_space=pl.ANY),
        scratch_shapes=[pltpu.VMEM((MOE_TM, ROW_SUB, ROW_LANE), BF16),
                        pltpu.SemaphoreType.DMA((2, MOE_TOPK)), pltpu.SemaphoreType.DMA(()),
                        pltpu.VMEM((3, ROW_TILE, ROW_SUB, ROW_LANE), BF16), pltpu.SemaphoreType.DMA((3,))],
    )
    return pl.pallas_call(
        _dispatch_kernel,
        grid_spec=grid_spec,
        out_shape=jax.ShapeDtypeStruct((MOE_ROWS, ROW_SUB, ROW_LANE), BF16),
        compiler_params=_params(1),
        name="dispatch",
    )(pos_t, zero_row, zero_on, n_used, hmw)


def _expert_kernel(nu_ref, first_ref, ord_ref, oe_ref, no_ref, half_ref, x_ref, w1_hbm, w3_hbm, w2_hbm, y_ref,
                   w1b, w3b, w2b, sem):
    i = pl.program_id(0)
    n_used = nu_ref[0]
    n_ord = no_ref[0]

    def weight_copies(k):
        e = oe_ref[k]
        slot = k % W_SLOTS
        return (pltpu.make_async_copy(w1_hbm.at[e], w1b.at[slot], sem.at[0, slot]),
                pltpu.make_async_copy(w3_hbm.at[e], w3b.at[slot], sem.at[1, slot]),
                pltpu.make_async_copy(w2_hbm.at[e], w2b.at[slot], sem.at[2, slot]))

    def start_weights(k):
        for cp in weight_copies(k):
            cp.start(priority=1)

    @pl.when(i == 0)
    def _():
        for k in range(W_SLOTS - 1):
            @pl.when(k < n_ord)
            def _():
                start_weights(k)

    @pl.when(i < n_used)
    def _():
        k = ord_ref[i]

        @pl.when(first_ref[i] > 0)
        def _():
            for cp in weight_copies(k):
                cp.wait()

            @pl.when(k + (W_SLOTS - 1) < n_ord)
            def _():
                start_weights(k + (W_SLOTS - 1))

        slot = k % W_SLOTS

        def swiglu(rows):
            x = _from_row_tiles(x_ref[0:rows])
            h1 = _dot(x, w1b[slot].astype(BF16))
            h3 = _dot(x, w3b[slot].astype(BF16))
            he = (h1 * _sigmoid(h1) * h3).astype(BF16)
            y_ref[0:rows] = _to_row_tiles(_dot(he, w2b[slot].astype(BF16)))

        @pl.when(half_ref[i] == 0)
        def _():
            swiglu(MOE_TM)

        @pl.when(half_ref[i] > 0)
        def _():
            swiglu(MOE_TM // 2)
            y_ref[MOE_TM // 2:] = jnp.zeros((MOE_TM // 2, ROW_SUB, ROW_LANE), BF16)

    @pl.when(i >= n_used)
    def _():
        y_ref[...] = jnp.zeros_like(y_ref)


def _experts(sched, xs, w_e1, w_e3, w_e2):
    grid_spec = pltpu.PrefetchScalarGridSpec(
        num_scalar_prefetch=6,
        grid=(MOE_NT,),
        in_specs=[
            pl.BlockSpec((MOE_TM, ROW_SUB, ROW_LANE), lambda i, nu, *_: (jnp.minimum(i, nu[0] - 1), 0, 0)),
            pl.BlockSpec(memory_space=pl.ANY),
            pl.BlockSpec(memory_space=pl.ANY),
            pl.BlockSpec(memory_space=pl.ANY),
        ],
        out_specs=pl.BlockSpec((MOE_TM, ROW_SUB, ROW_LANE), lambda i, *_: (i, 0, 0)),
        scratch_shapes=[
            pltpu.VMEM((W_SLOTS, D_MODEL, MOE_FF), F32),
            pltpu.VMEM((W_SLOTS, D_MODEL, MOE_FF), F32),
            pltpu.VMEM((W_SLOTS, MOE_FF, D_MODEL), F32),
            pltpu.SemaphoreType.DMA((3, W_SLOTS)),
        ],
    )
    return pl.pallas_call(
        _expert_kernel,
        grid_spec=grid_spec,
        out_shape=jax.ShapeDtypeStruct((MOE_ROWS, ROW_SUB, ROW_LANE), BF16),
        compiler_params=_params(1),
        name="experts",
    )(sched["n_used"], sched["tile_first"], sched["tile_ord"], sched["ord_expert"], sched["n_ord"],
      sched["tile_half"], xs, w_e1, w_e3, w_e2)


def _combine_kernel(pos_ref, ys_ref, x2_ref, route_ref, g_ref, yp_ref, ysm_ref, gbuf, sem):
    i = pl.program_id(0)

    def start_gather(tile, slot):
        def body(r, carry):
            for k in range(MOE_TOPK):
                src_row = pos_ref[k, tile * ROW_TILE + r]
                pltpu.make_async_copy(ys_ref.at[src_row], gbuf.at[slot, k, r], sem.at[slot, k]).start(priority=k)
            return carry
        lax.fori_loop(0, ROW_TILE, body, 0, unroll=8)

    @pl.when(i == 0)
    def _():
        start_gather(0, 0)

    slot = i % 2

    @pl.when(i + 1 < pl.num_programs(0))
    def _():
        start_gather(i + 1, 1 - slot)

    for k in range(MOE_TOPK):
        pltpu.make_async_copy(ys_ref.at[pl.ds(0, ROW_TILE)], gbuf.at[slot, k], sem.at[slot, k]).wait()

    route = route_ref[...]
    lane = lax.broadcasted_iota(jnp.int32, route.shape, 1)
    w0 = jnp.sum(jnp.where(lane == 4, route, 0.0), axis=-1, keepdims=True)
    w1 = jnp.sum(jnp.where(lane == 5, route, 0.0), axis=-1, keepdims=True)
    g0 = _from_row_tiles(gbuf[slot, 0]).astype(F32)
    g1 = _from_row_tiles(gbuf[slot, 1]).astype(F32)
    x3 = x2_ref[...] + (g0 * w0 + g1 * w1)
    y = _rms(x3, g_ref[...])

    @pl.when(i < NP_TILES)
    def _():
        yp_ref[...] = y

    @pl.when(i == NP_TILES)
    def _():
        ysm_ref[...] = y


def _combine(pos, ys, x2, route, g_f):
    grid_spec = pltpu.PrefetchScalarGridSpec(
        num_scalar_prefetch=1,
        grid=(N_TILES,),
        in_specs=[
            pl.BlockSpec(memory_space=pl.ANY),
            pl.BlockSpec((ROW_TILE, D_MODEL), lambda i, pos: (i, 0)),
            pl.BlockSpec((ROW_TILE, ROUTE_LANES), lambda i, pos: (i, 0)),
            pl.BlockSpec((1, D_MODEL), lambda i, pos: (0, 0)),
        ],
        out_specs=[
            pl.BlockSpec((ROW_TILE, D_MODEL), lambda i, pos: (jnp.minimum(i, NP_TILES - 1), 0)),
            pl.BlockSpec((ROW_TILE, D_MODEL), lambda i, pos: (0, 0)),
        ],
        scratch_shapes=[pltpu.VMEM((2, MOE_TOPK, ROW_TILE, ROW_SUB, ROW_LANE), BF16),
                        pltpu.SemaphoreType.DMA((2, MOE_TOPK))],
    )
    return pl.pallas_call(
        _combine_kernel,
        grid_spec=grid_spec,
        out_shape=[
            jax.ShapeDtypeStruct((N_P, D_MODEL), F32),
            jax.ShapeDtypeStruct((N_S, D_MODEL), F32),
        ],
        compiler_params=_params(1),
        name="combine_norm",
    )(pos, ys, x2, route, g_f)


def _expert_schedule(counts):
    i32 = jnp.int32
    cnt = counts[0, MOE_GROUPS:MOE_GROUPS + MOE_EXPERTS].astype(i32)
    tiles = (cnt + MOE_TM - 1) // MOE_TM
    tile_end = jnp.cumsum(tiles)
    tile_start = tile_end - tiles
    n_used = tile_end[-1]
    t = jnp.arange(MOE_NT, dtype=i32)
    tile_expert = jnp.minimum(jnp.sum((t[:, None] >= tile_end[None, :]).astype(i32), axis=1), MOE_EXPERTS - 1)
    used = tiles > 0
    ord_of = jnp.cumsum(used.astype(i32)) - 1
    experts = jnp.arange(MOE_EXPERTS, dtype=i32)
    ord_expert = jnp.zeros((MOE_EXPERTS,), i32).at[jnp.where(used, ord_of, MOE_EXPERTS)].set(experts, mode="drop")
    return {
        "n_used": n_used.reshape(1).astype(i32),
        "tile_first": ((t == tile_start[tile_expert]) & (t < n_used)).astype(i32),
        "tile_ord": ord_of[tile_expert].astype(i32),
        "tile_half": (cnt[tile_expert] - (t - tile_start[tile_expert]) * MOE_TM <= MOE_TM // 2).astype(i32),
        "ord_expert": ord_expert,
        "n_ord": jnp.sum(used.astype(i32)).reshape(1),
        "zero_row": (jnp.maximum(tile_end - 1, 0) * MOE_TM).astype(i32),
        "zero_on": used.astype(i32),
    }


def kernel(x_prompt, x_sample, mem_prompt, state_ret, cache_mem_k, cache_mem_v, norm_mix, w_in, ret_gn,
           sg_ln_g, sg_ln_b, sg_ws, sg_bs, w_a_out, w_b_out, w_o, norm_xa, w_cq, w_ck, w_cv, w_co, norm_moe,
           w_rg, b_rg, w_re, b_re, w_e1, w_e3, w_e2, norm_f):
    xp = x_prompt.reshape(N_P, D_MODEL)
    xs = x_sample.reshape(N_S, D_MODEL)

    h = _norm_rows(xp, xs, norm_mix)
    w = w_in[0]
    qk = _inproj("rope", IN_TM_ROPE, 0, 2 * RET_QK, h, w, _rope_tables(IN_TM_ROPE))
    v = _inproj("copy", IN_TM, 2 * RET_QK, RET_V, h, w)
    gs = _inproj("silu", IN_TM, 2 * RET_QK + RET_V, RET_V, h, w)
    uv = _inproj("gelu", IN_TM, 2 * RET_QK + 2 * RET_V, 2 * SG_WIDTH, h, w)
    gab = _inproj("sigmoid", IN_TM, 2 * RET_QK + 2 * RET_V + 2 * SG_WIDTH, 2 * D_MODEL, h, w)

    a_p, ret_p, a_s, ret_s = _retention(qk, v, gs, ret_gn, state_ret)

    b_all, sgv = _sgate(uv, sg_ln_g, sg_ln_b, sg_ws[0], sg_bs[0])

    x1, qx = _merge(a_p, a_s, b_all, gab, xp, xs, w_a_out[0].astype(BF16), w_b_out[0].astype(BF16),
                    w_o[0].astype(BF16), norm_xa, w_cq[0].astype(BF16))

    mk, mv, mkb, mvb = _memkv(mem_prompt.reshape(BATCH * MEM_LEN, D_MODEL), w_ck[0], w_cv[0])
    o_p = _xattn_prompt(qx, mkb, mvb)
    ck = cache_mem_k.reshape(DEC_BATCH, MEM_LEN * XA_HEADS, XA_DH)
    cv = cache_mem_v.reshape(DEC_BATCH, MEM_LEN * XA_HEADS, XA_DH)

    w_r = jnp.zeros((D_MODEL, ROUTE_LANES), F32)
    w_r = w_r.at[:, :MOE_GROUPS].set(w_rg[0]).at[:, MOE_GROUPS:MOE_GROUPS + MOE_EXPERTS].set(w_re[0])
    b_r = jnp.zeros((1, ROUTE_LANES), F32)
    b_r = b_r.at[0, :MOE_GROUPS].set(b_rg[0]).at[0, MOE_GROUPS:MOE_GROUPS + MOE_EXPERTS].set(b_re[0])
    x2, hmw, route, counts = _route(o_p, qx, ck, cv, x1, w_co[0].astype(BF16), norm_moe, w_r.astype(BF16), b_r)

    pos_t = _positions(route, counts)
    sched = _expert_schedule(counts)
    xs_sorted = _dispatch(pos_t, sched["zero_row"], sched["zero_on"], sched["n_used"], hmw)
    ys = _experts(sched, xs_sorted, w_e1[0], w_e3[0], w_e2[0])
    y_p, y_s = _combine(pos_t, ys, x2, route, norm_f.reshape(1, D_MODEL))

    return (y_p.reshape(BATCH, SEQ, D_MODEL),
            y_s.reshape(DEC_BATCH, DEC_SEQ, D_MODEL),
            ret_p,
            mk.reshape(1, BATCH, MEM_LEN, XA_HEADS, XA_DH),
            mv.reshape(1, BATCH, MEM_LEN, XA_HEADS, XA_DH),
            ret_s,
            sgv.reshape(1, DEC_BATCH, DEC_SEQ, SG_WIDTH))
```

```python
import functools

import jax
import jax.numpy as jnp
from jax import lax
from jax.experimental import pallas as pl
from jax.experimental.pallas import tpu as pltpu

F32 = jnp.float32
BF16 = jnp.bfloat16

D_MODEL = 2048
BATCH = 4
SEQ = 2048
DEC_BATCH = 128
DEC_SEQ = 4
PAST_LEN = 16384
RET_HEADS = 8
RET_DK = 128
RET_DV = 256
RET_CHUNK = 128
ROPE_BASE = 10000.0
RET_QK = RET_HEADS * RET_DK
RET_V = RET_HEADS * RET_DV
SG_GROUPS = 4
SG_WIDTH = 2048
SG_CHUNK = 128
MEM_LEN = 256
XA_HEADS = 4
XA_DH = 128
XA_W = XA_HEADS * XA_DH
MOE_GROUPS = 4
MOE_PER_GROUP = 8
MOE_EXPERTS = MOE_GROUPS * MOE_PER_GROUP
MOE_TOPK = 2
MOE_FF = 512
EPS = 1e-6
IN_WIDTH = 2 * RET_QK + 2 * RET_V + 2 * SG_WIDTH + 2 * D_MODEL

N_P = BATCH * SEQ
N_S = DEC_BATCH * DEC_SEQ
N_ALL = N_P + N_S
ROW_TILE = 512
N_TILES = N_ALL // ROW_TILE
NP_TILES = N_P // ROW_TILE
MERGE_TILE = 256
MERGE_P_TILES = N_P // MERGE_TILE

IN_TM = 1024
IN_TM_ROPE = 1024
IN_TN = 1024

PROMPT_CHUNK = 256
RET_BB = DEC_BATCH // (BATCH * (SEQ // PROMPT_CHUNK))
RET_ROWS = RET_BB * DEC_SEQ
LOG_DEC_SEQ = 2
LOG_SG_CHUNK = 7
SAMPLE_BB = DEC_BATCH // NP_TILES
SAMPLE_ROWS = SAMPLE_BB * DEC_SEQ

MOE_TM = 256
MOE_NT = (N_ALL * MOE_TOPK + MOE_EXPERTS * (MOE_TM - 1) + MOE_TM - 1) // MOE_TM
MOE_ROWS = MOE_NT * MOE_TM
ROUTE_LANES = 128
ROW_SUB, ROW_LANE = 16, 128
W_SLOTS = 3

VMEM_LIMIT = 56 * 1024 * 1024


def _params(n_axes, vmem=VMEM_LIMIT):
    return pltpu.CompilerParams(dimension_semantics=("arbitrary",) * n_axes,
                                vmem_limit_bytes=vmem)


def _rms(x, g):
    ms = jnp.mean(x * x, axis=-1, keepdims=True)
    return (x * lax.rsqrt(ms + EPS)) * g


def _dot(a, b):
    return jnp.dot(a, b, preferred_element_type=F32)


def _sigmoid(x):
    return 0.5 * jnp.tanh(0.5 * x) + 0.5


def _to_row_tiles(x):
    return x.astype(BF16).reshape(x.shape[0], ROW_SUB, ROW_LANE)


def _from_row_tiles(t):
    return t.reshape(t.shape[0], D_MODEL)


def _dot_nt(a, b):
    return lax.dot_general(a, b, (((1,), (1,)), ((), ())), preferred_element_type=F32)


def _dot_tn(a, b):
    return lax.dot_general(a, b, (((0,), (0,)), ((), ())), preferred_element_type=F32)


def _norm_kernel(xp_ref, xs_ref, g_ref, h_ref):
    i = pl.program_id(0)

    @pl.when(i < NP_TILES)
    def _():
        h_ref[...] = _rms(xp_ref[...], g_ref[...]).astype(BF16)

    @pl.when(i == NP_TILES)
    def _():
        h_ref[...] = _rms(xs_ref[...], g_ref[...]).astype(BF16)


def _norm_rows(xp, xs, g):
    return pl.pallas_call(
        _norm_kernel,
        grid=(N_TILES,),
        in_specs=[
            pl.BlockSpec((ROW_TILE, D_MODEL), lambda i: (jnp.minimum(i, NP_TILES - 1), 0)),
            pl.BlockSpec((ROW_TILE, D_MODEL), lambda i: (0, 0)),
            pl.BlockSpec((1, D_MODEL), lambda i: (0, 0)),
        ],
        out_specs=pl.BlockSpec((ROW_TILE, D_MODEL), lambda i: (i, 0)),
        out_shape=jax.ShapeDtypeStruct((N_ALL, D_MODEL), BF16),
        compiler_params=_params(1),
        name="norm_rows",
    )(xp, xs, g)


def _inproj_kernel(kind, tm, n_side, h_ref, w_ref, *rest):
    if kind == "rope":
        cos_ref, sin_ref, side_ref, z_ref, side_out_ref, wb_ref = rest
    else:
        side_ref, z_ref, side_out_ref, wb_ref = rest
    j = pl.program_id(0)
    i = pl.program_id(1)
    last = N_P // tm

    @pl.when(i == 0)
    def _():
        wb_ref[...] = w_ref[...].astype(BF16)

    @pl.when(j * (last + 1) + i < n_side)
    def _():
        side_out_ref[...] = side_ref[...].astype(BF16)

    def tile(rows):
        acc = _dot(h_ref[0:rows, :], wb_ref[...])
        if kind == "rope":
            scale = jnp.where(j == 1, RET_DK ** -0.5, 1.0).astype(F32)
            c = cos_ref[0:rows, :]
            s = sin_ref[0:rows, :]
            for hb in range(IN_TN // RET_DK):
                cols = slice(hb * RET_DK, (hb + 1) * RET_DK)
                a = acc[:, cols]
                r = pltpu.roll(a, RET_DK // 2, axis=1)
                z_ref[0:rows, cols] = ((a * c + r * s) * scale).astype(BF16)
        elif kind == "copy":
            z_ref[0:rows, :] = acc.astype(BF16)
        elif kind == "silu":
            z_ref[0:rows, :] = (acc * _sigmoid(acc)).astype(BF16)
        elif kind == "gelu":
            z_ref[0:rows, :] = jax.nn.gelu(acc).astype(BF16)
        else:
            z_ref[0:rows, :] = _sigmoid(acc).astype(BF16)

    @pl.when(i < last)
    def _():
        tile(tm)

    @pl.when(i == last)
    def _():
        tile(N_S)


def _inproj(kind, tm, col0, width, h, w_in, side, side_first, n_side, tables=()):
    last = N_P // tm
    n_j = width // IN_TN
    assert n_side <= n_j * (last + 1)
    tab_idx = lambda j, i: (jnp.where(i < last, i % (SEQ // tm), SEQ // tm), 0)
    j0 = col0 // IN_TN
    side_blk = lambda j, i: jnp.minimum(j * (last + 1) + i, n_side - 1)
    blk = (1,) + side.shape[1:]
    return pl.pallas_call(
        functools.partial(_inproj_kernel, kind, tm, n_side),
        grid=(n_j, last + 1),
        in_specs=[
            pl.BlockSpec((tm, D_MODEL), lambda j, i: (i, 0)),
            pl.BlockSpec((D_MODEL, IN_TN), lambda j, i: (0, j0 + j)),
        ] + [pl.BlockSpec((tm, RET_DK), tab_idx) for _ in tables] + [
            pl.BlockSpec(blk, lambda j, i: (side_first + side_blk(j, i), 0, 0)),
        ],
        out_specs=[
            pl.BlockSpec((tm, IN_TN), lambda j, i: (i, j)),
            pl.BlockSpec(blk, lambda j, i: (side_blk(j, i), 0, 0)),
        ],
        out_shape=[
            jax.ShapeDtypeStruct((N_ALL, width), BF16),
            jax.ShapeDtypeStruct((n_side,) + side.shape[1:], BF16),
        ],
        scratch_shapes=[pltpu.VMEM((D_MODEL, IN_TN), BF16)],
        compiler_params=_params(2),
        name="in_proj_" + kind,
    )(h, w_in, *tables, side)


def _rope_tables(tm):
    half = RET_DK // 2
    inv = ROPE_BASE ** (-jnp.arange(half, dtype=F32) / half)

    def tab(pos):
        ang = pos.astype(F32)[:, None] * inv[None, :]
        c, s = jnp.cos(ang), jnp.sin(ang)
        return jnp.concatenate([c, c], -1), jnp.concatenate([-s, s], -1)

    cp, sp = tab(jnp.arange(SEQ, dtype=jnp.int32))
    cs, ss = tab(PAST_LEN + jnp.arange(DEC_SEQ, dtype=jnp.int32))
    cs = jnp.tile(cs, (DEC_BATCH, 1))
    ss = jnp.tile(ss, (DEC_BATCH, 1))
    pad = jnp.zeros((tm - N_S, RET_DK), F32)
    return jnp.concatenate([cp, cs, pad], 0), jnp.concatenate([sp, ss, pad], 0)


def _decay_tables(chunk):
    lg = jnp.log1p(-jnp.power(2.0, -5.0 - jnp.arange(RET_HEADS, dtype=F32)))
    idx = jnp.arange(chunk, dtype=F32)
    rel = idx[:, None] - idx[None, :]
    dmask = jnp.where(rel >= 0, jnp.exp(lg[:, None, None] * jnp.maximum(rel, 0.0)), 0.0).astype(F32)
    xi = jnp.exp(lg[:, None] * (idx[None, :] + 1.0)).astype(F32)
    zeta = jnp.exp(lg[:, None] * (chunk - 1.0 - idx[None, :])).astype(F32)
    gc = jnp.exp(lg * chunk).astype(F32)
    return dmask, xi, zeta, gc


def _head_norm_gate(o, gn, gate):
    mu = jnp.mean(o, axis=-1, keepdims=True)
    d = o - mu
    var = jnp.mean(d * d, axis=-1, keepdims=True)
    y = (d * lax.rsqrt(var + EPS)) * gn
    return (gate.astype(F32) * y).astype(BF16)


def _ret_kernel(gcp_ref, gcs_ref, q_ref, k_ref, v_ref, gs_ref, gn_ref, dm_ref, xi_ref, zt_ref,
                qs_ref, ks_ref, vs_ref, gss_ref, dms_ref, xis_ref, zts_ref, s0_ref,
                a_ref, sfin_ref, as_ref, s1_ref, s_ref):
    c = pl.program_id(1)

    @pl.when(c == 0)
    def _():
        s_ref[...] = jnp.zeros_like(s_ref)

    rows_k = lax.broadcasted_iota(jnp.int32, (RET_ROWS, RET_DK), 0) >> LOG_DEC_SEQ
    rows_v = lax.broadcasted_iota(jnp.int32, (RET_ROWS, RET_DV), 0) >> LOG_DEC_SEQ
    for h in range(RET_HEADS):
        kc = slice(h * RET_DK, (h + 1) * RET_DK)
        vc = slice(h * RET_DV, (h + 1) * RET_DV)
        gn = gn_ref[:, vc]

        qh = q_ref[:, kc]
        kh = k_ref[:, kc]
        vh = v_ref[:, vc]
        inner = _dot_nt(qh, kh) * dm_ref[h]
        o = _dot(inner.astype(BF16), vh)
        s_old = s_ref[h]
        xi = xi_ref[h]
        o = o + _dot(qh, s_old.astype(BF16)) * jnp.concatenate([xi, xi], axis=1)
        kz = (kh.astype(F32) * zt_ref[h]).astype(BF16)
        s_ref[h] = gcp_ref[h] * s_old + _dot_tn(kz, vh)
        a_ref[:, vc] = _head_norm_gate(o, gn, gs_ref[:, vc])

        qh = qs_ref[:, kc]
        kh = ks_ref[:, kc]
        vh = vs_ref[:, vc]
        inner = _dot_nt(qh, kh) * dms_ref[h]
        o = _dot(inner.astype(BF16), vh)
        xi = xis_ref[h]
        xi2 = jnp.concatenate([xi, xi], axis=1)
        kz = kh.astype(F32) * zts_ref[h]
        gch = gcs_ref[h]
        for b in range(RET_BB):
            s_old = s0_ref[0, b, h]
            cross = _dot(qh, s_old.astype(BF16)) * xi2
            o = o + jnp.where(rows_v == b, cross, 0.0)
            kz_b = jnp.where(rows_k == b, kz, 0.0).astype(BF16)
            s1_ref[0, b, h] = gch * s_old + _dot_tn(kz_b, vh)
        as_ref[:, vc] = _head_norm_gate(o, gn, gss_ref[:, vc])

    @pl.when(c == pl.num_programs(1) - 1)
    def _():
        sfin_ref[0, 0] = s_ref[...]


def _retention(qk, v, gs, ret_gn, state):
    chunk = PROMPT_CHUNK
    n_chunks = SEQ // chunk
    dmask, xi, zeta, gc = _decay_tables(chunk)
    xi_b = jnp.broadcast_to(xi[:, :, None], (RET_HEADS, chunk, RET_DK))
    zeta_b = jnp.broadcast_to(zeta[:, :, None], (RET_HEADS, chunk, RET_DK))
    dmask_s, xi_s, zeta_s, gc_s = _decay_tables(DEC_SEQ)
    eye = jnp.eye(RET_BB, dtype=F32)
    dm_big = jax.vmap(lambda m: jnp.kron(eye, m))(dmask_s)
    xi_sb = jnp.broadcast_to(jnp.tile(xi_s, (1, RET_BB))[:, :, None], (RET_HEADS, RET_ROWS, RET_DK))
    zeta_sb = jnp.broadcast_to(jnp.tile(zeta_s, (1, RET_BB))[:, :, None], (RET_HEADS, RET_ROWS, RET_DK))

    row = lambda b, c: b * n_chunks + c
    srow = lambda b, c: N_P // RET_ROWS + row(b, c)
    const3 = lambda b, c: (0, 0, 0)
    smem = pl.BlockSpec(memory_space=pltpu.SMEM)
    st_spec = pl.BlockSpec((1, RET_BB, RET_HEADS, RET_DK, RET_DV), lambda b, c: (0, row(b, c), 0, 0, 0))
    return pl.pallas_call(
        _ret_kernel,
        grid=(BATCH, n_chunks),
        in_specs=[
            smem, smem,
            pl.BlockSpec((chunk, RET_QK), lambda b, c: (row(b, c), 0)),
            pl.BlockSpec((chunk, RET_QK), lambda b, c: (row(b, c), 1)),
            pl.BlockSpec((chunk, RET_V), lambda b, c: (row(b, c), 0)),
            pl.BlockSpec((chunk, RET_V), lambda b, c: (row(b, c), 0)),
            pl.BlockSpec((1, RET_V), lambda b, c: (0, 0)),
            pl.BlockSpec((RET_HEADS, chunk, chunk), const3),
            pl.BlockSpec((RET_HEADS, chunk, RET_DK), const3),
            pl.BlockSpec((RET_HEADS, chunk, RET_DK), const3),
            pl.BlockSpec((RET_ROWS, RET_QK), lambda b, c: (srow(b, c), 0)),
            pl.BlockSpec((RET_ROWS, RET_QK), lambda b, c: (srow(b, c), 1)),
            pl.BlockSpec((RET_ROWS, RET_V), lambda b, c: (srow(b, c), 0)),
            pl.BlockSpec((RET_ROWS, RET_V), lambda b, c: (srow(b, c), 0)),
            pl.BlockSpec((RET_HEADS, RET_ROWS, RET_ROWS), const3),
            pl.BlockSpec((RET_HEADS, RET_ROWS, RET_DK), const3),
            pl.BlockSpec((RET_HEADS, RET_ROWS, RET_DK), const3),
            st_spec,
        ],
        out_specs=[
            pl.BlockSpec((chunk, RET_V), lambda b, c: (row(b, c), 0)),
            pl.BlockSpec((1, 1, RET_HEADS, RET_DK, RET_DV), lambda b, c: (0, b, 0, 0, 0)),
            pl.BlockSpec((RET_ROWS, RET_V), lambda b, c: (row(b, c), 0)),
            st_spec,
        ],
        out_shape=[
            jax.ShapeDtypeStruct((N_P, RET_V), BF16),
            jax.ShapeDtypeStruct((1, BATCH, RET_HEADS, RET_DK, RET_DV), F32),
            jax.ShapeDtypeStruct((N_S, RET_V), BF16),
            jax.ShapeDtypeStruct((1, DEC_BATCH, RET_HEADS, RET_DK, RET_DV), F32),
        ],
        scratch_shapes=[pltpu.VMEM((RET_HEADS, RET_DK, RET_DV), F32)],
        compiler_params=_params(2),
        name="retention",
    )(gc, gc_s, qk, qk, v, gs, ret_gn, dmask, xi_b, zeta_b, qk, qk, v, gs, dm_big, xi_sb, zeta_sb, state)


def _sgate_kernel(u_ref, v_ref, lg_ref, lb_ref, wp_ref, bp_ref, ws_ref, bs_ref, b_ref, sgv_ref, vln_ref):
    i = pl.program_id(0)
    v = v_ref[...].astype(F32)
    mu = jnp.mean(v, axis=-1, keepdims=True)
    d = v - mu
    var = jnp.mean(d * d, axis=-1, keepdims=True)
    vln_ref[...] = (d * lax.rsqrt(var + EPS)) * lg_ref[...] + lb_ref[...]
    gw = SG_WIDTH // SG_GROUPS
    lane_reps = gw // 128

    @pl.when(i < NP_TILES)
    def _():
        r = lax.broadcasted_iota(jnp.int32, (SG_CHUNK, SG_CHUNK), 0)
        c = lax.broadcasted_iota(jnp.int32, (SG_CHUNK, SG_CHUNK), 1)
        for g in range(SG_GROUPS):
            cols = slice(g * gw, (g + 1) * gw)
            w = jnp.where(c <= r, wp_ref[g], 0.0).astype(BF16)
            bias = jnp.concatenate([bp_ref[g]] * lane_reps, axis=1)
            for ch in range(ROW_TILE // SG_CHUNK):
                rows = slice(ch * SG_CHUNK, (ch + 1) * SG_CHUNK)
                mixed = _dot(w, vln_ref[rows, cols].astype(BF16)) + bias
                b_ref[rows, cols] = (u_ref[rows, cols].astype(F32) * mixed).astype(BF16)

    @pl.when(i == NP_TILES)
    def _():
        sgv_ref[...] = vln_ref[...]
        r = lax.broadcasted_iota(jnp.int32, (ROW_TILE, ROW_TILE), 0)
        c = lax.broadcasted_iota(jnp.int32, (ROW_TILE, ROW_TILE), 1)
        keep = ((r >> LOG_DEC_SEQ) == (c >> LOG_DEC_SEQ)) & (c <= r)
        for g in range(SG_GROUPS):
            cols = slice(g * gw, (g + 1) * gw)
            w_rows = jnp.concatenate([ws_ref[g]] * (ROW_TILE // 8), axis=0)
            w_full = jnp.concatenate([w_rows] * (ROW_TILE // 128), axis=1)
            w = jnp.where(keep, w_full, 0.0).astype(BF16)
            b_rows = jnp.concatenate([bs_ref[g]] * (ROW_TILE // 8), axis=0)
            bias = jnp.concatenate([b_rows] * lane_reps, axis=1)
            mixed = _dot(w, vln_ref[:, cols].astype(BF16)) + bias
            b_ref[:, cols] = (u_ref[:, cols].astype(F32) * mixed).astype(BF16)


def _sgate(uv, ln_g, ln_b, sg_ws, sg_bs):
    b_p = jnp.broadcast_to(sg_bs[:, :, None], (SG_GROUPS, SG_CHUNK, 128))
    w_s = jnp.tile(sg_ws[:, :DEC_SEQ, :DEC_SEQ], (1, 8 // DEC_SEQ, 128 // DEC_SEQ))
    b_s = jnp.broadcast_to(jnp.tile(sg_bs[:, :DEC_SEQ], (1, 8 // DEC_SEQ))[:, :, None], (SG_GROUPS, 8, 128))
    const3 = lambda i: (0, 0, 0)
    return pl.pallas_call(
        _sgate_kernel,
        grid=(N_TILES,),
        in_specs=[
            pl.BlockSpec((ROW_TILE, SG_WIDTH), lambda i: (i, 0)),
            pl.BlockSpec((ROW_TILE, SG_WIDTH), lambda i: (i, 1)),
            pl.BlockSpec((1, SG_WIDTH), lambda i: (0, 0)),
            pl.BlockSpec((1, SG_WIDTH), lambda i: (0, 0)),
            pl.BlockSpec((SG_GROUPS, SG_CHUNK, SG_CHUNK), const3),
            pl.BlockSpec((SG_GROUPS, SG_CHUNK, 128), const3),
            pl.BlockSpec((SG_GROUPS, 8, 128), const3),
            pl.BlockSpec((SG_GROUPS, 8, 128), const3),
        ],
        out_specs=[
            pl.BlockSpec((ROW_TILE, SG_WIDTH), lambda i: (i, 0)),
            pl.BlockSpec((N_S, SG_WIDTH), lambda i: (0, 0)),
        ],
        out_shape=[
            jax.ShapeDtypeStruct((N_ALL, SG_WIDTH), BF16),
            jax.ShapeDtypeStruct((N_S, SG_WIDTH), F32),
        ],
        scratch_shapes=[pltpu.VMEM((ROW_TILE, SG_WIDTH), F32)],
        compiler_params=_params(1),
        name="spatial_gate",
    )(uv, uv, ln_g, ln_b, sg_ws, b_p, w_s, b_s)


def _merge_kernel(ap_ref, as_ref, b_ref, ga_ref, gb_ref, xp_ref, xs_ref, wa_ref, wb_ref, wo_ref,
                  g_ref, wq_ref, x1_ref, q_ref):
    i = pl.program_id(0)

    def run(a_in, x):
        a = _dot(a_in, wa_ref[...])
        b = _dot(b_ref[...], wb_ref[...])
        merged = ga_ref[...].astype(F32) * a + gb_ref[...].astype(F32) * b
        x1 = x + _dot(merged.astype(BF16), wo_ref[...])
        x1_ref[...] = x1
        q_ref[...] = _dot(_rms(x1, g_ref[...]).astype(BF16), wq_ref[...]).astype(BF16)

    @pl.when(i < MERGE_P_TILES)
    def _():
        run(ap_ref[...], xp_ref[...])

    @pl.when(i >= MERGE_P_TILES)
    def _():
        run(as_ref[...], xs_ref[...])


def _resident(shape):
    return pl.BlockSpec(shape, lambda i: (0,) * len(shape), pipeline_mode=pl.Buffered(1))


def _merge(a_p, a_s, b_all, gab, xp, xs, wa, wb, wo, g_xa, wq):
    tm = MERGE_TILE
    prompt_tile = lambda i: (jnp.minimum(i, MERGE_P_TILES - 1), 0)
    sample_tile = lambda i: (jnp.maximum(i - MERGE_P_TILES, 0), 0)
    return pl.pallas_call(
        _merge_kernel,
        grid=(N_ALL // tm,),
        in_specs=[
            pl.BlockSpec((tm, RET_V), prompt_tile),
            pl.BlockSpec((tm, RET_V), sample_tile),
            pl.BlockSpec((tm, SG_WIDTH), lambda i: (i, 0)),
            pl.BlockSpec((tm, D_MODEL), lambda i: (i, 0)),
            pl.BlockSpec((tm, D_MODEL), lambda i: (i, 1)),
            pl.BlockSpec((tm, D_MODEL), prompt_tile),
            pl.BlockSpec((tm, D_MODEL), sample_tile),
            _resident((RET_V, D_MODEL)),
            _resident((SG_WIDTH, D_MODEL)),
            _resident((D_MODEL, D_MODEL)),
            pl.BlockSpec((1, D_MODEL), lambda i: (0, 0)),
            _resident((D_MODEL, XA_W)),
        ],
        out_specs=[
            pl.BlockSpec((tm, D_MODEL), lambda i: (i, 0)),
            pl.BlockSpec((tm, XA_W), lambda i: (i, 0)),
        ],
        out_shape=[
            jax.ShapeDtypeStruct((N_ALL, D_MODEL), F32),
            jax.ShapeDtypeStruct((N_ALL, XA_W), BF16),
        ],
        compiler_params=_params(1),
        name="merge_proj",
    )(a_p, a_s, b_all, gab, gab, xp, xs, wa, wb, wo, g_xa, wq)


def _memkv_kernel(m_ref, wk_ref, wv_ref, k_ref, v_ref, kb_ref, vb_ref):
    m = m_ref[...].astype(BF16)
    k = _dot(m, wk_ref[...].astype(BF16))
    v = _dot(m, wv_ref[...].astype(BF16))
    k_ref[...] = k
    v_ref[...] = v
    kb_ref[...] = k.astype(BF16)
    vb_ref[...] = v.astype(BF16)


def _memkv(mem, w_ck, w_cv):
    rows = BATCH * MEM_LEN
    spec = pl.BlockSpec((MEM_LEN, XA_W), lambda b: (b, 0))
    wspec = pl.BlockSpec((D_MODEL, XA_W), lambda b: (0, 0))
    return pl.pallas_call(
        _memkv_kernel,
        grid=(BATCH,),
        in_specs=[pl.BlockSpec((MEM_LEN, D_MODEL), lambda b: (b, 0)), wspec, wspec],
        out_specs=[spec, spec, spec, spec],
        out_shape=[jax.ShapeDtypeStruct((rows, XA_W), F32)] * 2 + [jax.ShapeDtypeStruct((rows, XA_W), BF16)] * 2,
        compiler_params=_params(1),
        name="mem_kv",
    )(mem, w_ck, w_cv)


def _softmax_rows(s):
    m = jnp.max(s, axis=-1, keepdims=True)
    e = jnp.exp(s - m)
    return e / jnp.sum(e, axis=-1, keepdims=True)


def _xattn_prompt_kernel(q_ref, k_ref, v_ref, o_ref):
    for h in range(XA_HEADS):
        cols = slice(h * XA_DH, (h + 1) * XA_DH)
        s = _dot_nt(q_ref[:, cols], k_ref[:, cols]) * (XA_DH ** -0.5)
        p = _softmax_rows(s)
        o_ref[:, cols] = _dot(p.astype(BF16), v_ref[:, cols]).astype(BF16)


def _xattn_prompt(q, kb, vb):
    per_b = SEQ // ROW_TILE
    kv = pl.BlockSpec((MEM_LEN, XA_W), lambda i: (i // per_b, 0))
    return pl.pallas_call(
        _xattn_prompt_kernel,
        grid=(NP_TILES,),
        in_specs=[pl.BlockSpec((ROW_TILE, XA_W), lambda i: (i, 0)), kv, kv],
        out_specs=pl.BlockSpec((ROW_TILE, XA_W), lambda i: (i, 0)),
        out_shape=jax.ShapeDtypeStruct((N_P, XA_W), BF16),
        compiler_params=_params(1),
        name="xattn_prompt",
    )(q, kb, vb)


def _xattn_sample_entries(q_ref, k_ref, v_ref, o_ref):
    n_q = XA_HEADS * DEC_SEQ
    n_kv = MEM_LEN * XA_HEADS
    qf = q_ref[...].astype(F32)
    r = lax.broadcasted_iota(jnp.int32, (n_q, n_kv), 0)
    c = lax.broadcasted_iota(jnp.int32, (n_q, n_kv), 1)
    head_ok = (r >> LOG_DEC_SEQ) == (c & (XA_HEADS - 1))
    for b in range(SAMPLE_BB):
        rows = slice(b * DEC_SEQ, (b + 1) * DEC_SEQ)
        qb = jnp.concatenate([qf[rows, h * XA_DH:(h + 1) * XA_DH] for h in range(XA_HEADS)], axis=0)
        s = _dot_nt(qb.astype(BF16), k_ref[b].astype(BF16)) * (XA_DH ** -0.5)
        p = _softmax_rows(jnp.where(head_ok, s, -1e30))
        o = _dot(p.astype(BF16), v_ref[b].astype(BF16))
        for h in range(XA_HEADS):
            o_ref[h, rows, :] = o[h * DEC_SEQ:(h + 1) * DEC_SEQ, :]


def _route_kernel(op_ref, qs_ref, ck_ref, cv_ref, x1_ref, wo_ref, g_ref, wr_ref, br_ref,
                  x2_ref, hm_ref, route_ref, cnt_ref, run_ref, os_ref):
    i = pl.program_id(0)

    @pl.when(i == 0)
    def _():
        run_ref[...] = jnp.zeros_like(run_ref)

    def run(o):
        x2 = x1_ref[...] + _dot(o, wo_ref[...])
        x2_ref[...] = x2
        hm = _rms(x2, g_ref[...])
        hm_ref[...] = _to_row_tiles(hm)
        logits = _dot(hm.astype(BF16), wr_ref[...]) + br_ref[...]
        lane = lax.broadcasted_iota(jnp.int32, logits.shape, 1)
        neg = jnp.float32(-jnp.inf)
        big = jnp.int32(1 << 20)
        is_g = lane < MOE_GROUPS
        gl = jnp.where(is_g, logits, neg)
        gmax = jnp.max(gl, axis=-1, keepdims=True)
        g_sel = jnp.min(jnp.where(gl == gmax, lane, big), axis=-1, keepdims=True)
        g_w = 1.0 / jnp.sum(jnp.where(is_g, jnp.exp(logits - gmax), 0.0), axis=-1, keepdims=True)
        e_lane = lane - MOE_GROUPS
        in_grp = (e_lane >= 0) & (e_lane < MOE_EXPERTS) & ((e_lane >> 3) == g_sel)
        el = jnp.where(in_grp, logits, neg)
        v0 = jnp.max(el, axis=-1, keepdims=True)
        i0 = jnp.min(jnp.where(el == v0, lane, big), axis=-1, keepdims=True)
        el1 = jnp.where(lane == i0, neg, el)
        v1 = jnp.max(el1, axis=-1, keepdims=True)
        i1 = jnp.min(jnp.where(el1 == v1, lane, big), axis=-1, keepdims=True)
        ex = jnp.exp(v1 - v0)
        den = 1.0 + ex
        w0 = (1.0 / den) * g_w
        w1 = (ex / den) * g_w
        a0 = (lane == i0).astype(F32)
        a1 = (lane == i1).astype(F32)
        a = a0 + a1
        rr = lax.broadcasted_iota(jnp.int32, (ROW_TILE, ROW_TILE), 0)
        cc = lax.broadcasted_iota(jnp.int32, (ROW_TILE, ROW_TILE), 1)
        lower = jnp.where(cc < rr, 1.0, 0.0).astype(BF16)
        before = _dot(lower, a.astype(BF16)) + run_ref[...]
        rank0 = jnp.sum(before * a0, axis=-1, keepdims=True)
        rank1 = jnp.sum(before * a1, axis=-1, keepdims=True)
        run_ref[...] += jnp.sum(a, axis=0, keepdims=True)
        e0 = (i0 - MOE_GROUPS).astype(F32)
        e1 = (i1 - MOE_GROUPS).astype(F32)
        route = jnp.where(lane == 0, e0, 0.0)
        route = jnp.where(lane == 1, e1, route)
        route = jnp.where(lane == 2, rank0, route)
        route = jnp.where(lane == 3, rank1, route)
        route = jnp.where(lane == 4, w0, route)
        route = jnp.where(lane == 5, w1, route)
        route_ref[...] = route
        cnt_ref[...] = run_ref[...]

    @pl.when(i < NP_TILES)
    def _():
        _xattn_sample_entries(qs_ref, ck_ref, cv_ref, os_ref.at[i])
        run(op_ref[...])

    @pl.when(i == NP_TILES)
    def _():
        heads = [os_ref[:, h].reshape(N_S, XA_DH) for h in range(XA_HEADS)]
        run(jnp.concatenate(heads, axis=1).astype(BF16))


def _route(o_p, qx, ck, cv, x1, w_co, g_moe, w_r, b_r):
    prompt_step = lambda i: jnp.minimum(i, NP_TILES - 1)
    kv = pl.BlockSpec((SAMPLE_BB, MEM_LEN * XA_HEADS, XA_DH), lambda i: (prompt_step(i), 0, 0))
    return pl.pallas_call(
        _route_kernel,
        grid=(N_TILES,),
        in_specs=[
            pl.BlockSpec((ROW_TILE, XA_W), lambda i: (prompt_step(i), 0)),
            pl.BlockSpec((SAMPLE_ROWS, XA_W), lambda i: (N_P // SAMPLE_ROWS + prompt_step(i), 0)),
            kv, kv,
            pl.BlockSpec((ROW_TILE, D_MODEL), lambda i: (i, 0)),
            _resident((XA_W, D_MODEL)),
            pl.BlockSpec((1, D_MODEL), lambda i: (0, 0)),
            _resident((D_MODEL, ROUTE_LANES)),
            pl.BlockSpec((1, ROUTE_LANES), lambda i: (0, 0)),
        ],
        out_specs=[
            pl.BlockSpec((ROW_TILE, D_MODEL), lambda i: (i, 0)),
            pl.BlockSpec((ROW_TILE, ROW_SUB, ROW_LANE), lambda i: (i, 0, 0)),
            pl.BlockSpec((ROW_TILE, ROUTE_LANES), lambda i: (i, 0)),
            pl.BlockSpec((1, ROUTE_LANES), lambda i: (0, 0)),
        ],
        out_shape=[
            jax.ShapeDtypeStruct((N_ALL, D_MODEL), F32),
            jax.ShapeDtypeStruct((N_ALL, ROW_SUB, ROW_LANE), BF16),
            jax.ShapeDtypeStruct((N_ALL, ROUTE_LANES), F32),
            jax.ShapeDtypeStruct((1, ROUTE_LANES), F32),
        ],
        scratch_shapes=[pltpu.VMEM((1, ROUTE_LANES), F32),
                        pltpu.VMEM((NP_TILES, XA_HEADS, SAMPLE_ROWS, XA_DH), F32)],
        compiler_params=_params(1),
        name="xa_out_route",
    )(o_p, qx, ck, cv, x1, w_co, g_moe, w_r, b_r)


def _positions_kernel(route_ref, cnt_ref, pos_ref):
    route = route_ref[...]
    lane = lax.broadcasted_iota(jnp.int32, route.shape, 1)
    lane_f = lane.astype(F32)
    tiles = jnp.floor((cnt_ref[...] + (MOE_TM - 1)) * (1.0 / MOE_TM))
    lr = lax.broadcasted_iota(jnp.int32, (ROUTE_LANES, ROUTE_LANES), 0)
    lc = lax.broadcasted_iota(jnp.int32, (ROUTE_LANES, ROUTE_LANES), 1)
    before = jnp.where(lr < lc, 1.0, 0.0).astype(BF16)
    tiles8 = jnp.broadcast_to(tiles, (8, ROUTE_LANES)).astype(BF16)
    start = _dot(tiles8, before)[0:1, :] * MOE_TM

    def col(k):
        return jnp.sum(jnp.where(lane == k, route, 0.0), axis=-1, keepdims=True)

    def first_row(e):
        return jnp.sum(jnp.where(lane_f == e + MOE_GROUPS, start, 0.0), axis=-1, keepdims=True)

    p0 = first_row(col(0)) + col(2)
    p1 = first_row(col(1)) + col(3)
    p = jnp.where(lane == 0, p0, jnp.where(lane == 1, p1, 0.0))
    pos_ref[...] = p.T[0:8, :].astype(jnp.int32)


def _positions(route, counts):
    rows = N_ALL // 4
    return pl.pallas_call(
        _positions_kernel,
        grid=(4,),
        in_specs=[
            pl.BlockSpec((rows, ROUTE_LANES), lambda i: (i, 0)),
            pl.BlockSpec((1, ROUTE_LANES), lambda i: (0, 0)),
        ],
        out_specs=pl.BlockSpec((8, rows), lambda i: (0, i)),
        out_shape=jax.ShapeDtypeStruct((8, N_ALL), jnp.int32),
        compiler_params=_params(1),
        name="positions",
    )(route, counts)


def _dispatch_kernel(pos_ref, zrow_ref, zon_ref, nu_ref, hm_ref, xs_ref, zbuf, sem, zsem, tbuf, tsem):
    i = pl.program_id(0)

    def zero_tile(row):
        return pltpu.make_async_copy(zbuf, xs_ref.at[pl.ds(pl.multiple_of(row, MOE_TM), MOE_TM)], zsem)

    @pl.when(i == 0)
    def _():
        zbuf[...] = jnp.zeros_like(zbuf)
        for e in range(MOE_EXPERTS):
            @pl.when(zon_ref[e] > 0)
            def _():
                zero_tile(zrow_ref[e]).start()

        def start_tail(t, carry):
            zero_tile(t * MOE_TM).start()
            return carry

        def wait_tail(t, carry):
            zero_tile(t * MOE_TM).wait()
            return carry

        lax.fori_loop(nu_ref[0], MOE_NT, start_tail, 0)
        for e in range(MOE_EXPERTS):
            @pl.when(zon_ref[e] > 0)
            def _():
                zero_tile(zrow_ref[e]).wait()
        lax.fori_loop(nu_ref[0], MOE_NT, wait_tail, 0)

    n_steps = pl.num_programs(0)

    def tile_copy(t):
        return pltpu.make_async_copy(hm_ref.at[pl.ds(t * ROW_TILE, ROW_TILE)], tbuf.at[t % 3], tsem.at[t % 3])

    def wait_rows(t):
        for k in range(MOE_TOPK):
            pltpu.make_async_copy(tbuf.at[t % 3], xs_ref.at[pl.ds(0, ROW_TILE)], sem.at[t % 2, k]).wait()

    @pl.when(i == 0)
    def _():
        tile_copy(0).start()

    tile_copy(i).wait()

    @pl.when(i + 1 < n_steps)
    def _():
        tile_copy(i + 1).start()

    slot = i % 3
    par = i % 2
    base = i * ROW_TILE

    def start(r, carry):
        for k in range(MOE_TOPK):
            dst_row = pos_ref[k, base + r]
            pltpu.make_async_copy(tbuf.at[slot, r], xs_ref.at[dst_row], sem.at[par, k]).start(priority=k)
        return carry

    lax.fori_loop(0, ROW_TILE, start, 0, unroll=8)

    @pl.when(i > 0)
    def _():
        wait_rows(i - 1)

    @pl.when(i == n_steps - 1)
    def _():
        wait_rows(i)


def _dispatch(pos_t, zero_row, zero_on, n_used, hmw):
    grid_spec = pltpu.PrefetchScalarGridSpec(
        num_scalar_prefetch=4,
        grid=(N_TILES,),
        in_specs=[pl.BlockSpec(memory_space=pl.ANY)],
        out_specs=pl.BlockSpec(memory_space=pl.ANY),
        scratch_shapes=[pltpu.VMEM((MOE_TM, ROW_SUB, ROW_LANE), BF16),
                        pltpu.SemaphoreType.DMA((2, MOE_TOPK)), pltpu.SemaphoreType.DMA(()),
                        pltpu.VMEM((3, ROW_TILE, ROW_SUB, ROW_LANE), BF16), pltpu.SemaphoreType.DMA((3,))],
    )
    return pl.pallas_call(
        _dispatch_kernel,
        grid_spec=grid_spec,
        out_shape=jax.ShapeDtypeStruct((MOE_ROWS, ROW_SUB, ROW_LANE), BF16),
        compiler_params=_params(1),
        name="dispatch",
    )(pos_t, zero_row, zero_on, n_used, hmw)


def _expert_kernel(nu_ref, first_ref, ord_ref, oe_ref, no_ref, half_ref, x_ref, w1_hbm, w3_hbm, w2lo_hbm, w2hi_hbm,
                   y_ref, w1b, w3b, w2b, sem):
    i = pl.program_id(0)
    n_used = nu_ref[0]
    n_ord = no_ref[0]
    n_lo = w2lo_hbm.shape[0]

    def weight_copies(k, e, e2_hbm, e2):
        slot = k % W_SLOTS
        return (pltpu.make_async_copy(w1_hbm.at[e], w1b.at[slot], sem.at[0, slot]),
                pltpu.make_async_copy(w3_hbm.at[e], w3b.at[slot], sem.at[1, slot]),
                pltpu.make_async_copy(e2_hbm.at[e2], w2b.at[slot], sem.at[2, slot]))

    def start_weights(k):
        e = oe_ref[k]

        @pl.when(e < n_lo)
        def _():
            for cp in weight_copies(k, e, w2lo_hbm, e):
                cp.start(priority=1)

        @pl.when(e >= n_lo)
        def _():
            for cp in weight_copies(k, e, w2hi_hbm, e - n_lo):
                cp.start(priority=1)

    @pl.when(i == 0)
    def _():
        for k in range(W_SLOTS - 1):
            @pl.when(k < n_ord)
            def _():
                start_weights(k)

    @pl.when(i < n_used)
    def _():
        k = ord_ref[i]

        @pl.when(first_ref[i] > 0)
        def _():
            for cp in weight_copies(k, 0, w2lo_hbm, 0):
                cp.wait()

            @pl.when(k + (W_SLOTS - 1) < n_ord)
            def _():
                start_weights(k + (W_SLOTS - 1))

        slot = k % W_SLOTS

        def swiglu(rows):
            x = _from_row_tiles(x_ref[0:rows])
            h1 = _dot(x, w1b[slot])
            h3 = _dot(x, w3b[slot])
            he = (h1 * _sigmoid(h1) * h3).astype(BF16)
            y_ref[0:rows] = _to_row_tiles(_dot(he, w2b[slot]))

        @pl.when(half_ref[i] == 0)
        def _():
            swiglu(MOE_TM)

        @pl.when(half_ref[i] > 0)
        def _():
            swiglu(MOE_TM // 2)
            y_ref[MOE_TM // 2:] = jnp.zeros((MOE_TM // 2, ROW_SUB, ROW_LANE), BF16)

    @pl.when(i >= n_used)
    def _():
        y_ref[...] = jnp.zeros_like(y_ref)


def _experts(sched, xs, w_e1, w_e3, w_e2_lo, w_e2_hi):
    grid_spec = pltpu.PrefetchScalarGridSpec(
        num_scalar_prefetch=6,
        grid=(MOE_NT,),
        in_specs=[
            pl.BlockSpec((MOE_TM, ROW_SUB, ROW_LANE), lambda i, nu, *_: (jnp.minimum(i, nu[0] - 1), 0, 0)),
            pl.BlockSpec(memory_space=pl.ANY),
            pl.BlockSpec(memory_space=pl.ANY),
            pl.BlockSpec(memory_space=pl.ANY),
            pl.BlockSpec(memory_space=pl.ANY),
        ],
        out_specs=pl.BlockSpec((MOE_TM, ROW_SUB, ROW_LANE), lambda i, *_: (i, 0, 0)),
        scratch_shapes=[
            pltpu.VMEM((W_SLOTS, D_MODEL, MOE_FF), BF16),
            pltpu.VMEM((W_SLOTS, D_MODEL, MOE_FF), BF16),
            pltpu.VMEM((W_SLOTS, MOE_FF, D_MODEL), BF16),
            pltpu.SemaphoreType.DMA((3, W_SLOTS)),
        ],
    )
    return pl.pallas_call(
        _expert_kernel,
        grid_spec=grid_spec,
        out_shape=jax.ShapeDtypeStruct((MOE_ROWS, ROW_SUB, ROW_LANE), BF16),
        compiler_params=_params(1),
        name="experts",
    )(sched["n_used"], sched["tile_first"], sched["tile_ord"], sched["ord_expert"], sched["n_ord"],
      sched["tile_half"], xs, w_e1, w_e3, w_e2_lo, w_e2_hi)


def _combine_kernel(pos_ref, ys_ref, x2_ref, route_ref, g_ref, yp_ref, ysm_ref, gbuf, sem):
    i = pl.program_id(0)

    def start_gather(tile, slot):
        def body(r, carry):
            for k in range(MOE_TOPK):
                src_row = pos_ref[k, tile * ROW_TILE + r]
                pltpu.make_async_copy(ys_ref.at[src_row], gbuf.at[slot, k, r], sem.at[slot, k]).start(priority=k)
            return carry
        lax.fori_loop(0, ROW_TILE, body, 0, unroll=8)

    @pl.when(i == 0)
    def _():
        start_gather(0, 0)

    slot = i % 2

    @pl.when(i + 1 < pl.num_programs(0))
    def _():
        start_gather(i + 1, 1 - slot)

    for k in range(MOE_TOPK):
        pltpu.make_async_copy(ys_ref.at[pl.ds(0, ROW_TILE)], gbuf.at[slot, k], sem.at[slot, k]).wait()

    route = route_ref[...]
    lane = lax.broadcasted_iota(jnp.int32, route.shape, 1)
    w0 = jnp.sum(jnp.where(lane == 4, route, 0.0), axis=-1, keepdims=True)
    w1 = jnp.sum(jnp.where(lane == 5, route, 0.0), axis=-1, keepdims=True)
    g0 = _from_row_tiles(gbuf[slot, 0]).astype(F32)
    g1 = _from_row_tiles(gbuf[slot, 1]).astype(F32)
    x3 = x2_ref[...] + (g0 * w0 + g1 * w1)
    y = _rms(x3, g_ref[...])

    @pl.when(i < NP_TILES)
    def _():
        yp_ref[...] = y

    @pl.when(i == NP_TILES)
    def _():
        ysm_ref[...] = y


def _combine(pos, ys, x2, route, g_f):
    grid_spec = pltpu.PrefetchScalarGridSpec(
        num_scalar_prefetch=1,
        grid=(N_TILES,),
        in_specs=[
            pl.BlockSpec(memory_space=pl.ANY),
            pl.BlockSpec((ROW_TILE, D_MODEL), lambda i, pos: (i, 0)),
            pl.BlockSpec((ROW_TILE, ROUTE_LANES), lambda i, pos: (i, 0)),
            pl.BlockSpec((1, D_MODEL), lambda i, pos: (0, 0)),
        ],
        out_specs=[
            pl.BlockSpec((ROW_TILE, D_MODEL), lambda i, pos: (jnp.minimum(i, NP_TILES - 1), 0)),
            pl.BlockSpec((ROW_TILE, D_MODEL), lambda i, pos: (0, 0)),
        ],
        scratch_shapes=[pltpu.VMEM((2, MOE_TOPK, ROW_TILE, ROW_SUB, ROW_LANE), BF16),
                        pltpu.SemaphoreType.DMA((2, MOE_TOPK))],
    )
    return pl.pallas_call(
        _combine_kernel,
        grid_spec=grid_spec,
        out_shape=[
            jax.ShapeDtypeStruct((N_P, D_MODEL), F32),
            jax.ShapeDtypeStruct((N_S, D_MODEL), F32),
        ],
        compiler_params=_params(1),
        name="combine_norm",
    )(pos, ys, x2, route, g_f)


def _expert_schedule(counts):
    i32 = jnp.int32
    cnt = counts[0, MOE_GROUPS:MOE_GROUPS + MOE_EXPERTS].astype(i32)
    tiles = (cnt + MOE_TM - 1) // MOE_TM
    tile_end = jnp.cumsum(tiles)
    tile_start = tile_end - tiles
    n_used = tile_end[-1]
    t = jnp.arange(MOE_NT, dtype=i32)
    tile_expert = jnp.minimum(jnp.sum((t[:, None] >= tile_end[None, :]).astype(i32), axis=1), MOE_EXPERTS - 1)
    used = tiles > 0
    ord_of = jnp.cumsum(used.astype(i32)) - 1
    experts = jnp.arange(MOE_EXPERTS, dtype=i32)
    ord_expert = jnp.zeros((MOE_EXPERTS,), i32).at[jnp.where(used, ord_of, MOE_EXPERTS)].set(experts, mode="drop")
    return {
        "n_used": n_used.reshape(1).astype(i32),
        "tile_first": ((t == tile_start[tile_expert]) & (t < n_used)).astype(i32),
        "tile_ord": ord_of[tile_expert].astype(i32),
        "tile_half": (cnt[tile_expert] - (t - tile_start[tile_expert]) * MOE_TM <= MOE_TM // 2).astype(i32),
        "ord_expert": ord_expert,
        "n_ord": jnp.sum(used.astype(i32)).reshape(1),
        "zero_row": (jnp.maximum(tile_end - 1, 0) * MOE_TM).astype(i32),
        "zero_on": used.astype(i32),
    }


def kernel(x_prompt, x_sample, mem_prompt, state_ret, cache_mem_k, cache_mem_v, norm_mix, w_in, ret_gn,
           sg_ln_g, sg_ln_b, sg_ws, sg_bs, w_a_out, w_b_out, w_o, norm_xa, w_cq, w_ck, w_cv, w_co, norm_moe,
           w_rg, b_rg, w_re, b_re, w_e1, w_e3, w_e2, norm_f):
    xp = x_prompt.reshape(N_P, D_MODEL)
    xs = x_sample.reshape(N_S, D_MODEL)

    h = _norm_rows(xp, xs, norm_mix)
    w = w_in[0]
    half_e = MOE_EXPERTS // 2
    wa_blocks = w_a_out[0].reshape(4, RET_V // 4, D_MODEL)
    qk, w2_lo = _inproj("rope", IN_TM_ROPE, 0, 2 * RET_QK, h, w, w_e2[0], 0, half_e, _rope_tables(IN_TM_ROPE))
    v, w2_hi = _inproj("copy", IN_TM, 2 * RET_QK, RET_V, h, w, w_e2[0], half_e, half_e)
    gs, wa = _inproj("silu", IN_TM, 2 * RET_QK + RET_V, RET_V, h, w, wa_blocks, 0, 4)
    uv, w1_bf = _inproj("gelu", IN_TM, 2 * RET_QK + 2 * RET_V, 2 * SG_WIDTH, h, w, w_e1[0], 0, MOE_EXPERTS)
    gab, w3_bf = _inproj("sigmoid", IN_TM, 2 * RET_QK + 2 * RET_V + 2 * SG_WIDTH, 2 * D_MODEL, h, w,
                         w_e3[0], 0, MOE_EXPERTS)

    a_p, ret_p, a_s, ret_s = _retention(qk, v, gs, ret_gn, state_ret)

    b_all, sgv = _sgate(uv, sg_ln_g, sg_ln_b, sg_ws[0], sg_bs[0])

    x1, qx = _merge(a_p, a_s, b_all, gab, xp, xs, wa.reshape(RET_V, D_MODEL), w_b_out[0].astype(BF16),
                    w_o[0].astype(BF16), norm_xa, w_cq[0].astype(BF16))

    mk, mv, mkb, mvb = _memkv(mem_prompt.reshape(BATCH * MEM_LEN, D_MODEL), w_ck[0], w_cv[0])
    o_p = _xattn_prompt(qx, mkb, mvb)
    ck = cache_mem_k.reshape(DEC_BATCH, MEM_LEN * XA_HEADS, XA_DH)
    cv = cache_mem_v.reshape(DEC_BATCH, MEM_LEN * XA_HEADS, XA_DH)

    w_r = jnp.zeros((D_MODEL, ROUTE_LANES), F32)
    w_r = w_r.at[:, :MOE_GROUPS].set(w_rg[0]).at[:, MOE_GROUPS:MOE_GROUPS + MOE_EXPERTS].set(w_re[0])
    b_r = jnp.zeros((1, ROUTE_LANES), F32)
    b_r = b_r.at[0, :MOE_GROUPS].set(b_rg[0]).at[0, MOE_GROUPS:MOE_GROUPS + MOE_EXPERTS].set(b_re[0])
    x2, hmw, route, counts = _route(o_p, qx, ck, cv, x1, w_co[0].astype(BF16), norm_moe, w_r.astype(BF16), b_r)

    pos_t = _positions(route, counts)
    sched = _expert_schedule(counts)
    xs_sorted = _dispatch(pos_t, sched["zero_row"], sched["zero_on"], sched["n_used"], hmw)
    ys = _experts(sched, xs_sorted, w1_bf, w3_bf, w2_lo, w2_hi)
    y_p, y_s = _combine(pos_t, ys, x2, route, norm_f.reshape(1, D_MODEL))

    return (y_p.reshape(BATCH, SEQ, D_MODEL),
            y_s.reshape(DEC_BATCH, DEC_SEQ, D_MODEL),
            ret_p,
            mk.reshape(1, BATCH, MEM_LEN, XA_HEADS, XA_DH),
            mv.reshape(1, BATCH, MEM_LEN, XA_HEADS, XA_DH),
            ret_s,
            sgv.reshape(1, DEC_BATCH, DEC_SEQ, SG_WIDTH))
```

```python
import functools

import jax
import jax.numpy as jnp
from jax import lax
from jax.experimental import pallas as pl
from jax.experimental.pallas import tpu as pltpu

F32 = jnp.float32
BF16 = jnp.bfloat16

D_MODEL = 2048
BATCH = 4
SEQ = 2048
DEC_BATCH = 128
DEC_SEQ = 4
PAST_LEN = 16384
RET_HEADS = 8
RET_DK = 128
RET_DV = 256
RET_CHUNK = 128
ROPE_BASE = 10000.0
RET_QK = RET_HEADS * RET_DK
RET_V = RET_HEADS * RET_DV
SG_GROUPS = 4
SG_WIDTH = 2048
SG_CHUNK = 128
MEM_LEN = 256
XA_HEADS = 4
XA_DH = 128
XA_W = XA_HEADS * XA_DH
MOE_GROUPS = 4
MOE_PER_GROUP = 8
MOE_EXPERTS = MOE_GROUPS * MOE_PER_GROUP
MOE_TOPK = 2
MOE_FF = 512
EPS = 1e-6
IN_WIDTH = 2 * RET_QK + 2 * RET_V + 2 * SG_WIDTH + 2 * D_MODEL

N_P = BATCH * SEQ
N_S = DEC_BATCH * DEC_SEQ
N_ALL = N_P + N_S
ROW_TILE = 512
N_TILES = N_ALL // ROW_TILE
NP_TILES = N_P // ROW_TILE
MERGE_TILE = 256
MERGE_P_TILES = N_P // MERGE_TILE

IN_TM = 1024
IN_TM_ROPE = 1024
IN_TN = 1024

PROMPT_CHUNK = 256
RET_BB = DEC_BATCH // (BATCH * (SEQ // PROMPT_CHUNK))
RET_ROWS = RET_BB * DEC_SEQ
LOG_DEC_SEQ = 2
LOG_SG_CHUNK = 7
SAMPLE_BB = DEC_BATCH // NP_TILES
SAMPLE_ROWS = SAMPLE_BB * DEC_SEQ

MOE_TM = 256
MOE_NT = (N_ALL * MOE_TOPK + MOE_EXPERTS * (MOE_TM - 1) + MOE_TM - 1) // MOE_TM
MOE_ROWS = MOE_NT * MOE_TM
ROUTE_LANES = 128
ROW_SUB, ROW_LANE = 16, 128
W_SLOTS = 3

VMEM_LIMIT = 56 * 1024 * 1024


def _params(n_axes, vmem=VMEM_LIMIT):
    return pltpu.CompilerParams(dimension_semantics=("arbitrary",) * n_axes,
                                vmem_limit_bytes=vmem)


def _rms(x, g):
    ms = jnp.mean(x * x, axis=-1, keepdims=True)
    return (x * lax.rsqrt(ms + EPS)) * g


def _dot(a, b):
    return jnp.dot(a, b, preferred_element_type=F32)


def _sigmoid(x):
    return 0.5 * jnp.tanh(0.5 * x) + 0.5


def _to_row_tiles(x):
    return x.astype(BF16).reshape(x.shape[0], ROW_SUB, ROW_LANE)


def _from_row_tiles(t):
    return t.reshape(t.shape[0], D_MODEL)


def _dot_nt(a, b):
    return lax.dot_general(a, b, (((1,), (1,)), ((), ())), preferred_element_type=F32)


def _dot_tn(a, b):
    return lax.dot_general(a, b, (((0,), (0,)), ((), ())), preferred_element_type=F32)


def _norm_kernel(xp_ref, xs_ref, g_ref, h_ref):
    i = pl.program_id(0)

    @pl.when(i < NP_TILES)
    def _():
        h_ref[...] = _rms(xp_ref[...], g_ref[...]).astype(BF16)

    @pl.when(i == NP_TILES)
    def _():
        h_ref[...] = _rms(xs_ref[...], g_ref[...]).astype(BF16)


def _norm_rows(xp, xs, g):
    return pl.pallas_call(
        _norm_kernel,
        grid=(N_TILES,),
        in_specs=[
            pl.BlockSpec((ROW_TILE, D_MODEL), lambda i: (jnp.minimum(i, NP_TILES - 1), 0)),
            pl.BlockSpec((ROW_TILE, D_MODEL), lambda i: (0, 0)),
            pl.BlockSpec((1, D_MODEL), lambda i: (0, 0)),
        ],
        out_specs=pl.BlockSpec((ROW_TILE, D_MODEL), lambda i: (i, 0)),
        out_shape=jax.ShapeDtypeStruct((N_ALL, D_MODEL), BF16),
        compiler_params=_params(1),
        name="norm_rows",
    )(xp, xs, g)


def _inproj_kernel(kind, tm, h_ref, w_ref, *rest):
    if kind == "rope":
        cos_ref, sin_ref, z_ref, wb_ref = rest
    else:
        z_ref, wb_ref = rest
    j = pl.program_id(0)
    i = pl.program_id(1)
    last = N_P // tm

    @pl.when(i == 0)
    def _():
        wb_ref[...] = w_ref[...].astype(BF16)

    def tile(rows):
        acc = _dot(h_ref[0:rows, :], wb_ref[...])
        if kind == "rope":
            scale = jnp.where(j == 1, RET_DK ** -0.5, 1.0).astype(F32)
            c = cos_ref[0:rows, :]
            s = sin_ref[0:rows, :]
            for hb in range(IN_TN // RET_DK):
                cols = slice(hb * RET_DK, (hb + 1) * RET_DK)
                a = acc[:, cols]
                r = pltpu.roll(a, RET_DK // 2, axis=1)
                z_ref[0:rows, cols] = ((a * c + r * s) * scale).astype(BF16)
        elif kind == "copy":
            z_ref[0:rows, :] = acc.astype(BF16)
        elif kind == "silu":
            z_ref[0:rows, :] = (acc * _sigmoid(acc)).astype(BF16)
        elif kind == "gelu":
            z_ref[0:rows, :] = jax.nn.gelu(acc).astype(BF16)
        else:
            z_ref[0:rows, :] = _sigmoid(acc).astype(BF16)

    @pl.when(i < last)
    def _():
        tile(tm)

    @pl.when(i == last)
    def _():
        tile(N_S)


def _inproj(kind, tm, col0, width, h, w_in, tables=()):
    last = N_P // tm
    tab_idx = lambda j, i: (jnp.where(i < last, i % (SEQ // tm), SEQ // tm), 0)
    j0 = col0 // IN_TN
    return pl.pallas_call(
        functools.partial(_inproj_kernel, kind, tm),
        grid=(width // IN_TN, last + 1),
        in_specs=[
            pl.BlockSpec((tm, D_MODEL), lambda j, i: (i, 0)),
            pl.BlockSpec((D_MODEL, IN_TN), lambda j, i: (0, j0 + j)),
        ] + [pl.BlockSpec((tm, RET_DK), tab_idx) for _ in tables],
        out_specs=pl.BlockSpec((tm, IN_TN), lambda j, i: (i, j)),
        out_shape=jax.ShapeDtypeStruct((N_ALL, width), BF16),
        scratch_shapes=[pltpu.VMEM((D_MODEL, IN_TN), BF16)],
        compiler_params=_params(2),
        name="in_proj_" + kind,
    )(h, w_in, *tables)


def _rope_tables(tm):
    half = RET_DK // 2
    inv = ROPE_BASE ** (-jnp.arange(half, dtype=F32) / half)

    def tab(pos):
        ang = pos.astype(F32)[:, None] * inv[None, :]
        c, s = jnp.cos(ang), jnp.sin(ang)
        return jnp.concatenate([c, c], -1), jnp.concatenate([-s, s], -1)

    cp, sp = tab(jnp.arange(SEQ, dtype=jnp.int32))
    cs, ss = tab(PAST_LEN + jnp.arange(DEC_SEQ, dtype=jnp.int32))
    cs = jnp.tile(cs, (DEC_BATCH, 1))
    ss = jnp.tile(ss, (DEC_BATCH, 1))
    pad = jnp.zeros((tm - N_S, RET_DK), F32)
    return jnp.concatenate([cp, cs, pad], 0), jnp.concatenate([sp, ss, pad], 0)


def _decay_tables(chunk):
    lg = jnp.log1p(-jnp.power(2.0, -5.0 - jnp.arange(RET_HEADS, dtype=F32)))
    idx = jnp.arange(chunk, dtype=F32)
    rel = idx[:, None] - idx[None, :]
    dmask = jnp.where(rel >= 0, jnp.exp(lg[:, None, None] * jnp.maximum(rel, 0.0)), 0.0).astype(F32)
    xi = jnp.exp(lg[:, None] * (idx[None, :] + 1.0)).astype(F32)
    zeta = jnp.exp(lg[:, None] * (chunk - 1.0 - idx[None, :])).astype(F32)
    gc = jnp.exp(lg * chunk).astype(F32)
    return dmask, xi, zeta, gc


def _head_norm_gate(o, gn, gate):
    mu = jnp.mean(o, axis=-1, keepdims=True)
    d = o - mu
    var = jnp.mean(d * d, axis=-1, keepdims=True)
    y = (d * lax.rsqrt(var + EPS)) * gn
    return (gate.astype(F32) * y).astype(BF16)


def _ret_kernel(gcp_ref, gcs_ref, q_ref, k_ref, v_ref, gs_ref, gn_ref, dm_ref, xi_ref, zt_ref,
                qs_ref, ks_ref, vs_ref, gss_ref, dms_ref, xis_ref, zts_ref, s0_ref,
                a_ref, sfin_ref, as_ref, s1_ref, s_ref):
    c = pl.program_id(1)

    @pl.when(c == 0)
    def _():
        s_ref[...] = jnp.zeros_like(s_ref)

    rows_k = lax.broadcasted_iota(jnp.int32, (RET_ROWS, RET_DK), 0) >> LOG_DEC_SEQ
    rows_v = lax.broadcasted_iota(jnp.int32, (RET_ROWS, RET_DV), 0) >> LOG_DEC_SEQ
    for h in range(RET_HEADS):
        kc = slice(h * RET_DK, (h + 1) * RET_DK)
        vc = slice(h * RET_DV, (h + 1) * RET_DV)
        gn = gn_ref[:, vc]

        qh = q_ref[:, kc]
        kh = k_ref[:, kc]
        vh = v_ref[:, vc]
        inner = _dot_nt(qh, kh) * dm_ref[h]
        o = _dot(inner.astype(BF16), vh)
        s_old = s_ref[h]
        xi = xi_ref[h]
        o = o + _dot(qh, s_old.astype(BF16)) * jnp.concatenate([xi, xi], axis=1)
        kz = (kh.astype(F32) * zt_ref[h]).astype(BF16)
        s_ref[h] = gcp_ref[h] * s_old + _dot_tn(kz, vh)
        a_ref[:, vc] = _head_norm_gate(o, gn, gs_ref[:, vc])

        qh = qs_ref[:, kc]
        kh = ks_ref[:, kc]
        vh = vs_ref[:, vc]
        inner = _dot_nt(qh, kh) * dms_ref[h]
        o = _dot(inner.astype(BF16), vh)
        xi = xis_ref[h]
        xi2 = jnp.concatenate([xi, xi], axis=1)
        kz = kh.astype(F32) * zts_ref[h]
        gch = gcs_ref[h]
        for b in range(RET_BB):
            s_old = s0_ref[0, b, h]
            cross = _dot(qh, s_old.astype(BF16)) * xi2
            o = o + jnp.where(rows_v == b, cross, 0.0)
            kz_b = jnp.where(rows_k == b, kz, 0.0).astype(BF16)
            s1_ref[0, b, h] = gch * s_old + _dot_tn(kz_b, vh)
        as_ref[:, vc] = _head_norm_gate(o, gn, gss_ref[:, vc])

    @pl.when(c == pl.num_programs(1) - 1)
    def _():
        sfin_ref[0, 0] = s_ref[...]


def _retention(qk, v, gs, ret_gn, state):
    chunk = PROMPT_CHUNK
    n_chunks = SEQ // chunk
    dmask, xi, zeta, gc = _decay_tables(chunk)
    xi_b = jnp.broadcast_to(xi[:, :, None], (RET_HEADS, chunk, RET_DK))
    zeta_b = jnp.broadcast_to(zeta[:, :, None], (RET_HEADS, chunk, RET_DK))
    dmask_s, xi_s, zeta_s, gc_s = _decay_tables(DEC_SEQ)
    eye = jnp.eye(RET_BB, dtype=F32)
    dm_big = jax.vmap(lambda m: jnp.kron(eye, m))(dmask_s)
    xi_sb = jnp.broadcast_to(jnp.tile(xi_s, (1, RET_BB))[:, :, None], (RET_HEADS, RET_ROWS, RET_DK))
    zeta_sb = jnp.broadcast_to(jnp.tile(zeta_s, (1, RET_BB))[:, :, None], (RET_HEADS, RET_ROWS, RET_DK))

    row = lambda b, c: b * n_chunks + c
    srow = lambda b, c: N_P // RET_ROWS + row(b, c)
    const3 = lambda b, c: (0, 0, 0)
    smem = pl.BlockSpec(memory_space=pltpu.SMEM)
    st_spec = pl.BlockSpec((1, RET_BB, RET_HEADS, RET_DK, RET_DV), lambda b, c: (0, row(b, c), 0, 0, 0))
    return pl.pallas_call(
        _ret_kernel,
        grid=(BATCH, n_chunks),
        in_specs=[
            smem, smem,
            pl.BlockSpec((chunk, RET_QK), lambda b, c: (row(b, c), 0)),
            pl.BlockSpec((chunk, RET_QK), lambda b, c: (row(b, c), 1)),
            pl.BlockSpec((chunk, RET_V), lambda b, c: (row(b, c), 0)),
            pl.BlockSpec((chunk, RET_V), lambda b, c: (row(b, c), 0)),
            pl.BlockSpec((1, RET_V), lambda b, c: (0, 0)),
            pl.BlockSpec((RET_HEADS, chunk, chunk), const3),
            pl.BlockSpec((RET_HEADS, chunk, RET_DK), const3),
            pl.BlockSpec((RET_HEADS, chunk, RET_DK), const3),
            pl.BlockSpec((RET_ROWS, RET_QK), lambda b, c: (srow(b, c), 0)),
            pl.BlockSpec((RET_ROWS, RET_QK), lambda b, c: (srow(b, c), 1)),
            pl.BlockSpec((RET_ROWS, RET_V), lambda b, c: (srow(b, c), 0)),
            pl.BlockSpec((RET_ROWS, RET_V), lambda b, c: (srow(b, c), 0)),
            pl.BlockSpec((RET_HEADS, RET_ROWS, RET_ROWS), const3),
            pl.BlockSpec((RET_HEADS, RET_ROWS, RET_DK), const3),
            pl.BlockSpec((RET_HEADS, RET_ROWS, RET_DK), const3),
            st_spec,
        ],
        out_specs=[
            pl.BlockSpec((chunk, RET_V), lambda b, c: (row(b, c), 0)),
            pl.BlockSpec((1, 1, RET_HEADS, RET_DK, RET_DV), lambda b, c: (0, b, 0, 0, 0)),
            pl.BlockSpec((RET_ROWS, RET_V), lambda b, c: (row(b, c), 0)),
            st_spec,
        ],
        out_shape=[
            jax.ShapeDtypeStruct((N_P, RET_V), BF16),
            jax.ShapeDtypeStruct((1, BATCH, RET_HEADS, RET_DK, RET_DV), F32),
            jax.ShapeDtypeStruct((N_S, RET_V), BF16),
            jax.ShapeDtypeStruct((1, DEC_BATCH, RET_HEADS, RET_DK, RET_DV), F32),
        ],
        scratch_shapes=[pltpu.VMEM((RET_HEADS, RET_DK, RET_DV), F32)],
        compiler_params=_params(2),
        name="retention",
    )(gc, gc_s, qk, qk, v, gs, ret_gn, dmask, xi_b, zeta_b, qk, qk, v, gs, dm_big, xi_sb, zeta_sb, state)


def _sgate_kernel(u_ref, v_ref, lg_ref, lb_ref, wp_ref, bp_ref, ws_ref, bs_ref, b_ref, sgv_ref, vln_ref):
    i = pl.program_id(0)
    v = v_ref[...].astype(F32)
    mu = jnp.mean(v, axis=-1, keepdims=True)
    d = v - mu
    var = jnp.mean(d * d, axis=-1, keepdims=True)
    vln_ref[...] = (d * lax.rsqrt(var + EPS)) * lg_ref[...] + lb_ref[...]
    gw = SG_WIDTH // SG_GROUPS
    lane_reps = gw // 128

    @pl.when(i < NP_TILES)
    def _():
        r = lax.broadcasted_iota(jnp.int32, (SG_CHUNK, SG_CHUNK), 0)
        c = lax.broadcasted_iota(jnp.int32, (SG_CHUNK, SG_CHUNK), 1)
        for g in range(SG_GROUPS):
            cols = slice(g * gw, (g + 1) * gw)
            w = jnp.where(c <= r, wp_ref[g], 0.0).astype(BF16)
            bias = jnp.concatenate([bp_ref[g]] * lane_reps, axis=1)
            for ch in range(ROW_TILE // SG_CHUNK):
                rows = slice(ch * SG_CHUNK, (ch + 1) * SG_CHUNK)
                mixed = _dot(w, vln_ref[rows, cols].astype(BF16)) + bias
                b_ref[rows, cols] = (u_ref[rows, cols].astype(F32) * mixed).astype(BF16)

    @pl.when(i == NP_TILES)
    def _():
        sgv_ref[...] = vln_ref[...]
        r = lax.broadcasted_iota(jnp.int32, (ROW_TILE, ROW_TILE), 0)
        c = lax.broadcasted_iota(jnp.int32, (ROW_TILE, ROW_TILE), 1)
        keep = ((r >> LOG_DEC_SEQ) == (c >> LOG_DEC_SEQ)) & (c <= r)
        for g in range(SG_GROUPS):
            cols = slice(g * gw, (g + 1) * gw)
            w_rows = jnp.concatenate([ws_ref[g]] * (ROW_TILE // 8), axis=0)
            w_full = jnp.concatenate([w_rows] * (ROW_TILE // 128), axis=1)
            w = jnp.where(keep, w_full, 0.0).astype(BF16)
            b_rows = jnp.concatenate([bs_ref[g]] * (ROW_TILE // 8), axis=0)
            bias = jnp.concatenate([b_rows] * lane_reps, axis=1)
            mixed = _dot(w, vln_ref[:, cols].astype(BF16)) + bias
            b_ref[:, cols] = (u_ref[:, cols].astype(F32) * mixed).astype(BF16)


def _sgate(uv, ln_g, ln_b, sg_ws, sg_bs):
    b_p = jnp.broadcast_to(sg_bs[:, :, None], (SG_GROUPS, SG_CHUNK, 128))
    w_s = jnp.tile(sg_ws[:, :DEC_SEQ, :DEC_SEQ], (1, 8 // DEC_SEQ, 128 // DEC_SEQ))
    b_s = jnp.broadcast_to(jnp.tile(sg_bs[:, :DEC_SEQ], (1, 8 // DEC_SEQ))[:, :, None], (SG_GROUPS, 8, 128))
    const3 = lambda i: (0, 0, 0)
    return pl.pallas_call(
        _sgate_kernel,
        grid=(N_TILES,),
        in_specs=[
            pl.BlockSpec((ROW_TILE, SG_WIDTH), lambda i: (i, 0)),
            pl.BlockSpec((ROW_TILE, SG_WIDTH), lambda i: (i, 1)),
            pl.BlockSpec((1, SG_WIDTH), lambda i: (0, 0)),
            pl.BlockSpec((1, SG_WIDTH), lambda i: (0, 0)),
            pl.BlockSpec((SG_GROUPS, SG_CHUNK, SG_CHUNK), const3),
            pl.BlockSpec((SG_GROUPS, SG_CHUNK, 128), const3),
            pl.BlockSpec((SG_GROUPS, 8, 128), const3),
            pl.BlockSpec((SG_GROUPS, 8, 128), const3),
        ],
        out_specs=[
            pl.BlockSpec((ROW_TILE, SG_WIDTH), lambda i: (i, 0)),
            pl.BlockSpec((N_S, SG_WIDTH), lambda i: (0, 0)),
        ],
        out_shape=[
            jax.ShapeDtypeStruct((N_ALL, SG_WIDTH), BF16),
            jax.ShapeDtypeStruct((N_S, SG_WIDTH), F32),
        ],
        scratch_shapes=[pltpu.VMEM((ROW_TILE, SG_WIDTH), F32)],
        compiler_params=_params(1),
        name="spatial_gate",
    )(uv, uv, ln_g, ln_b, sg_ws, b_p, w_s, b_s)


def _merge_kernel(ap_ref, as_ref, b_ref, ga_ref, gb_ref, xp_ref, xs_ref, wa_ref, wb_ref, wo_ref,
                  g_ref, wq_ref, x1_ref, q_ref):
    i = pl.program_id(0)

    def run(a_in, x):
        a = _dot(a_in, wa_ref[...])
        b = _dot(b_ref[...], wb_ref[...])
        merged = ga_ref[...].astype(F32) * a + gb_ref[...].astype(F32) * b
        x1 = x + _dot(merged.astype(BF16), wo_ref[...])
        x1_ref[...] = x1
        q_ref[...] = _dot(_rms(x1, g_ref[...]).astype(BF16), wq_ref[...]).astype(BF16)

    @pl.when(i < MERGE_P_TILES)
    def _():
        run(ap_ref[...], xp_ref[...])

    @pl.when(i >= MERGE_P_TILES)
    def _():
        run(as_ref[...], xs_ref[...])


def _resident(shape):
    return pl.BlockSpec(shape, lambda i: (0,) * len(shape), pipeline_mode=pl.Buffered(1))


def _merge(a_p, a_s, b_all, gab, xp, xs, wa, wb, wo, g_xa, wq):
    tm = MERGE_TILE
    prompt_tile = lambda i: (jnp.minimum(i, MERGE_P_TILES - 1), 0)
    sample_tile = lambda i: (jnp.maximum(i - MERGE_P_TILES, 0), 0)
    return pl.pallas_call(
        _merge_kernel,
        grid=(N_ALL // tm,),
        in_specs=[
            pl.BlockSpec((tm, RET_V), prompt_tile),
            pl.BlockSpec((tm, RET_V), sample_tile),
            pl.BlockSpec((tm, SG_WIDTH), lambda i: (i, 0)),
            pl.BlockSpec((tm, D_MODEL), lambda i: (i, 0)),
            pl.BlockSpec((tm, D_MODEL), lambda i: (i, 1)),
            pl.BlockSpec((tm, D_MODEL), prompt_tile),
            pl.BlockSpec((tm, D_MODEL), sample_tile),
            _resident((RET_V, D_MODEL)),
            _resident((SG_WIDTH, D_MODEL)),
            _resident((D_MODEL, D_MODEL)),
            pl.BlockSpec((1, D_MODEL), lambda i: (0, 0)),
            _resident((D_MODEL, XA_W)),
        ],
        out_specs=[
            pl.BlockSpec((tm, D_MODEL), lambda i: (i, 0)),
            pl.BlockSpec((tm, XA_W), lambda i: (i, 0)),
        ],
        out_shape=[
            jax.ShapeDtypeStruct((N_ALL, D_MODEL), F32),
            jax.ShapeDtypeStruct((N_ALL, XA_W), BF16),
        ],
        compiler_params=_params(1),
        name="merge_proj",
    )(a_p, a_s, b_all, gab, gab, xp, xs, wa, wb, wo, g_xa, wq)


def _memkv_kernel(m_ref, wk_ref, wv_ref, k_ref, v_ref, kb_ref, vb_ref):
    m = m_ref[...].astype(BF16)
    k = _dot(m, wk_ref[...].astype(BF16))
    v = _dot(m, wv_ref[...].astype(BF16))
    k_ref[...] = k
    v_ref[...] = v
    kb_ref[...] = k.astype(BF16)
    vb_ref[...] = v.astype(BF16)


def _memkv(mem, w_ck, w_cv):
    rows = BATCH * MEM_LEN
    spec = pl.BlockSpec((MEM_LEN, XA_W), lambda b: (b, 0))
    wspec = pl.BlockSpec((D_MODEL, XA_W), lambda b: (0, 0))
    return pl.pallas_call(
        _memkv_kernel,
        grid=(BATCH,),
        in_specs=[pl.BlockSpec((MEM_LEN, D_MODEL), lambda b: (b, 0)), wspec, wspec],
        out_specs=[spec, spec, spec, spec],
        out_shape=[jax.ShapeDtypeStruct((rows, XA_W), F32)] * 2 + [jax.ShapeDtypeStruct((rows, XA_W), BF16)] * 2,
        compiler_params=_params(1),
        name="mem_kv",
    )(mem, w_ck, w_cv)


def _softmax_rows(s):
    m = jnp.max(s, axis=-1, keepdims=True)
    e = jnp.exp(s - m)
    return e / jnp.sum(e, axis=-1, keepdims=True)


def _xattn_prompt_rows(q_ref, k_ref, v_ref):
    heads = []
    for h in range(XA_HEADS):
        cols = slice(h * XA_DH, (h + 1) * XA_DH)
        s = _dot_nt(q_ref[:, cols], k_ref[:, cols]) * (XA_DH ** -0.5)
        p = _softmax_rows(s)
        heads.append(_dot(p.astype(BF16), v_ref[:, cols]).astype(BF16))
    return jnp.concatenate(heads, axis=1)


def _xattn_sample_entries(q_ref, k_ref, v_ref, o_ref):
    n_q = XA_HEADS * DEC_SEQ
    n_kv = MEM_LEN * XA_HEADS
    qf = q_ref[...].astype(F32)
    r = lax.broadcasted_iota(jnp.int32, (n_q, n_kv), 0)
    c = lax.broadcasted_iota(jnp.int32, (n_q, n_kv), 1)
    head_ok = (r >> LOG_DEC_SEQ) == (c & (XA_HEADS - 1))
    for b in range(SAMPLE_BB):
        rows = slice(b * DEC_SEQ, (b + 1) * DEC_SEQ)
        qb = jnp.concatenate([qf[rows, h * XA_DH:(h + 1) * XA_DH] for h in range(XA_HEADS)], axis=0)
        s = _dot_nt(qb.astype(BF16), k_ref[b].astype(BF16)) * (XA_DH ** -0.5)
        p = _softmax_rows(jnp.where(head_ok, s, -1e30))
        o = _dot(p.astype(BF16), v_ref[b].astype(BF16))
        for h in range(XA_HEADS):
            o_ref[h, rows, :] = o[h * DEC_SEQ:(h + 1) * DEC_SEQ, :]


def _route_kernel(qp_ref, mk_ref, mv_ref, qs_ref, ck_ref, cv_ref, x1_ref, wo_ref, g_ref, wr_ref, br_ref,
                  x2_ref, hm_ref, route_ref, cnt_ref, run_ref, os_ref):
    i = pl.program_id(0)

    @pl.when(i == 0)
    def _():
        run_ref[...] = jnp.zeros_like(run_ref)

    def run(o):
        x2 = x1_ref[...] + _dot(o, wo_ref[...])
        x2_ref[...] = x2
        hm = _rms(x2, g_ref[...])
        hm_ref[...] = _to_row_tiles(hm)
        logits = _dot(hm.astype(BF16), wr_ref[...]) + br_ref[...]
        lane = lax.broadcasted_iota(jnp.int32, logits.shape, 1)
        lane_f = lane.astype(F32)
        neg = jnp.float32(-jnp.inf)
        big = jnp.float32(1 << 20)
        is_g = lane < MOE_GROUPS
        gl = jnp.where(is_g, logits, neg)
        gmax = jnp.max(gl, axis=-1, keepdims=True)
        g_sel = jnp.min(jnp.where(gl == gmax, lane_f, big), axis=-1, keepdims=True)
        g_w = 1.0 / jnp.sum(jnp.where(is_g, jnp.exp(logits - gmax), 0.0), axis=-1, keepdims=True)
        e_lane = lane - MOE_GROUPS
        e_group = (e_lane >> 3).astype(F32)
        in_grp = (e_lane >= 0) & (e_lane < MOE_EXPERTS) & (e_group == g_sel)
        el = jnp.where(in_grp, logits, neg)
        v0 = jnp.max(el, axis=-1, keepdims=True)
        i0 = jnp.min(jnp.where(el == v0, lane_f, big), axis=-1, keepdims=True)
        el1 = jnp.where(lane_f == i0, neg, el)
        v1 = jnp.max(el1, axis=-1, keepdims=True)
        i1 = jnp.min(jnp.where(el1 == v1, lane_f, big), axis=-1, keepdims=True)
        ex = jnp.exp(v1 - v0)
        den = 1.0 + ex
        w0 = (1.0 / den) * g_w
        w1 = (ex / den) * g_w
        a0 = (lane_f == i0).astype(F32)
        a1 = (lane_f == i1).astype(F32)
        a = a0 + a1
        rr = lax.broadcasted_iota(jnp.int32, (ROW_TILE, ROW_TILE), 0)
        cc = lax.broadcasted_iota(jnp.int32, (ROW_TILE, ROW_TILE), 1)
        lower = jnp.where(cc < rr, 1.0, 0.0).astype(BF16)
        before = _dot(lower, a.astype(BF16)) + run_ref[...]
        rank0 = jnp.sum(before * a0, axis=-1, keepdims=True)
        rank1 = jnp.sum(before * a1, axis=-1, keepdims=True)
        run_ref[...] += jnp.sum(a, axis=0, keepdims=True)
        e0 = i0 - MOE_GROUPS
        e1 = i1 - MOE_GROUPS
        route = jnp.where(lane == 0, e0, 0.0)
        route = jnp.where(lane == 1, e1, route)
        route = jnp.where(lane == 2, rank0, route)
        route = jnp.where(lane == 3, rank1, route)
        route = jnp.where(lane == 4, w0, route)
        route = jnp.where(lane == 5, w1, route)
        route_ref[...] = route
        cnt_ref[...] = run_ref[...]

    @pl.when(i < NP_TILES)
    def _():
        _xattn_sample_entries(qs_ref, ck_ref, cv_ref, os_ref.at[i])
        run(_xattn_prompt_rows(qp_ref, mk_ref, mv_ref))

    @pl.when(i == NP_TILES)
    def _():
        heads = [os_ref[:, h].reshape(N_S, XA_DH) for h in range(XA_HEADS)]
        run(jnp.concatenate(heads, axis=1).astype(BF16))


def _route(qx, mkb, mvb, ck, cv, x1, w_co, g_moe, w_r, b_r):
    prompt_step = lambda i: jnp.minimum(i, NP_TILES - 1)
    kv = pl.BlockSpec((SAMPLE_BB, MEM_LEN * XA_HEADS, XA_DH), lambda i: (prompt_step(i), 0, 0))
    mem_kv = pl.BlockSpec((MEM_LEN, XA_W), lambda i: (prompt_step(i) // (SEQ // ROW_TILE), 0))
    return pl.pallas_call(
        _route_kernel,
        grid=(N_TILES,),
        in_specs=[
            pl.BlockSpec((ROW_TILE, XA_W), lambda i: (prompt_step(i), 0)),
            mem_kv, mem_kv,
            pl.BlockSpec((SAMPLE_ROWS, XA_W), lambda i: (N_P // SAMPLE_ROWS + prompt_step(i), 0)),
            kv, kv,
            pl.BlockSpec((ROW_TILE, D_MODEL), lambda i: (i, 0)),
            _resident((XA_W, D_MODEL)),
            pl.BlockSpec((1, D_MODEL), lambda i: (0, 0)),
            _resident((D_MODEL, ROUTE_LANES)),
            pl.BlockSpec((1, ROUTE_LANES), lambda i: (0, 0)),
        ],
        out_specs=[
            pl.BlockSpec((ROW_TILE, D_MODEL), lambda i: (i, 0)),
            pl.BlockSpec((ROW_TILE, ROW_SUB, ROW_LANE), lambda i: (i, 0, 0)),
            pl.BlockSpec((ROW_TILE, ROUTE_LANES), lambda i: (i, 0)),
            pl.BlockSpec((1, ROUTE_LANES), lambda i: (0, 0)),
        ],
        out_shape=[
            jax.ShapeDtypeStruct((N_ALL, D_MODEL), F32),
            jax.ShapeDtypeStruct((N_ALL, ROW_SUB, ROW_LANE), BF16),
            jax.ShapeDtypeStruct((N_ALL, ROUTE_LANES), F32),
            jax.ShapeDtypeStruct((1, ROUTE_LANES), F32),
        ],
        scratch_shapes=[pltpu.VMEM((1, ROUTE_LANES), F32),
                        pltpu.VMEM((NP_TILES, XA_HEADS, SAMPLE_ROWS, XA_DH), F32)],
        compiler_params=_params(1),
        name="xa_out_route",
    )(qx, mkb, mvb, qx, ck, cv, x1, w_co, g_moe, w_r, b_r)


def _positions_kernel(route_ref, cnt_ref, pos_ref):
    route = route_ref[...]
    lane = lax.broadcasted_iota(jnp.int32, route.shape, 1)
    lane_f = lane.astype(F32)
    tiles = jnp.floor((cnt_ref[...] + (MOE_TM - 1)) * (1.0 / MOE_TM))
    lr = lax.broadcasted_iota(jnp.int32, (ROUTE_LANES, ROUTE_LANES), 0)
    lc = lax.broadcasted_iota(jnp.int32, (ROUTE_LANES, ROUTE_LANES), 1)
    before = jnp.where(lr < lc, 1.0, 0.0).astype(BF16)
    tiles8 = jnp.broadcast_to(tiles, (8, ROUTE_LANES)).astype(BF16)
    start = _dot(tiles8, before)[0:1, :] * MOE_TM

    def col(k):
        return jnp.sum(jnp.where(lane == k, route, 0.0), axis=-1, keepdims=True)

    def first_row(e):
        return jnp.sum(jnp.where(lane_f == e + MOE_GROUPS, start, 0.0), axis=-1, keepdims=True)

    p0 = first_row(col(0)) + col(2)
    p1 = first_row(col(1)) + col(3)
    p = jnp.where(lane == 0, p0, jnp.where(lane == 1, p1, 0.0))
    pos_ref[...] = p.T[0:8, :].astype(jnp.int32)


def _positions(route, counts):
    rows = N_ALL // 4
    return pl.pallas_call(
        _positions_kernel,
        grid=(4,),
        in_specs=[
            pl.BlockSpec((rows, ROUTE_LANES), lambda i: (i, 0)),
            pl.BlockSpec((1, ROUTE_LANES), lambda i: (0, 0)),
        ],
        out_specs=pl.BlockSpec((8, rows), lambda i: (0, i)),
        out_shape=jax.ShapeDtypeStruct((8, N_ALL), jnp.int32),
        compiler_params=_params(1),
        name="positions",
    )(route, counts)


def _dispatch_kernel(pos_ref, zrow_ref, zon_ref, nu_ref, hm_ref, xs_ref, zbuf, sem, zsem, tbuf, tsem):
    i = pl.program_id(0)

    def zero_tile(row):
        return pltpu.make_async_copy(zbuf, xs_ref.at[pl.ds(pl.multiple_of(row, MOE_TM), MOE_TM)], zsem)

    @pl.when(i == 0)
    def _():
        zbuf[...] = jnp.zeros_like(zbuf)
        for e in range(MOE_EXPERTS):
            @pl.when(zon_ref[e] > 0)
            def _():
                zero_tile(zrow_ref[e]).start()

        def start_tail(t, carry):
            zero_tile(t * MOE_TM).start()
            return carry

        def wait_tail(t, carry):
            zero_tile(t * MOE_TM).wait()
            return carry

        lax.fori_loop(nu_ref[0], MOE_NT, start_tail, 0)
        for e in range(MOE_EXPERTS):
            @pl.when(zon_ref[e] > 0)
            def _():
                zero_tile(zrow_ref[e]).wait()
        lax.fori_loop(nu_ref[0], MOE_NT, wait_tail, 0)

    n_steps = pl.num_programs(0)

    def tile_copy(t):
        return pltpu.make_async_copy(hm_ref.at[pl.ds(t * ROW_TILE, ROW_TILE)], tbuf.at[t % 3], tsem.at[t % 3])

    def wait_rows(t):
        for k in range(MOE_TOPK):
            pltpu.make_async_copy(tbuf.at[t % 3], xs_ref.at[pl.ds(0, ROW_TILE)], sem.at[t % 2, k]).wait()

    @pl.when(i == 0)
    def _():
        tile_copy(0).start()

    tile_copy(i).wait()

    @pl.when(i + 1 < n_steps)
    def _():
        tile_copy(i + 1).start()

    slot = i % 3
    par = i % 2
    base = i * ROW_TILE

    def start(r, carry):
        for k in range(MOE_TOPK):
            dst_row = pos_ref[k, base + r]
            pltpu.make_async_copy(tbuf.at[slot, r], xs_ref.at[dst_row], sem.at[par, k]).start(priority=k)
        return carry

    lax.fori_loop(0, ROW_TILE, start, 0, unroll=8)

    @pl.when(i > 0)
    def _():
        wait_rows(i - 1)

    @pl.when(i == n_steps - 1)
    def _():
        wait_rows(i)


def _dispatch(pos_t, zero_row, zero_on, n_used, hmw):
    grid_spec = pltpu.PrefetchScalarGridSpec(
        num_scalar_prefetch=4,
        grid=(N_TILES,),
        in_specs=[pl.BlockSpec(memory_space=pl.ANY)],
        out_specs=pl.BlockSpec(memory_space=pl.ANY),
        scratch_shapes=[pltpu.VMEM((MOE_TM, ROW_SUB, ROW_LANE), BF16),
                        pltpu.SemaphoreType.DMA((2, MOE_TOPK)), pltpu.SemaphoreType.DMA(()),
                        pltpu.VMEM((3, ROW_TILE, ROW_SUB, ROW_LANE), BF16), pltpu.SemaphoreType.DMA((3,))],
    )
    return pl.pallas_call(
        _dispatch_kernel,
        grid_spec=grid_spec,
        out_shape=jax.ShapeDtypeStruct((MOE_ROWS, ROW_SUB, ROW_LANE), BF16),
        compiler_params=_params(1),
        name="dispatch",
    )(pos_t, zero_row, zero_on, n_used, hmw)


def _expert_kernel(nu_ref, first_ref, ord_ref, oe_ref, no_ref, half_ref, x_ref, w1_hbm, w3_hbm, w2_hbm, y_ref,
                   w1b, w3b, w2b, sem):
    i = pl.program_id(0)
    n_used = nu_ref[0]
    n_ord = no_ref[0]

    def weight_copies(k):
        e = oe_ref[k]
        slot = k % W_SLOTS
        return (pltpu.make_async_copy(w1_hbm.at[e], w1b.at[slot], sem.at[0, slot]),
                pltpu.make_async_copy(w3_hbm.at[e], w3b.at[slot], sem.at[1, slot]),
                pltpu.make_async_copy(w2_hbm.at[e], w2b.at[slot], sem.at[2, slot]))

    def start_weights(k):
        for cp in weight_copies(k):
            cp.start(priority=1)

    @pl.when(i == 0)
    def _():
        for k in range(W_SLOTS - 1):
            @pl.when(k < n_ord)
            def _():
                start_weights(k)

    @pl.when(i < n_used)
    def _():
        k = ord_ref[i]

        @pl.when(first_ref[i] > 0)
        def _():
            for cp in weight_copies(k):
                cp.wait()

            @pl.when(k + (W_SLOTS - 1) < n_ord)
            def _():
                start_weights(k + (W_SLOTS - 1))

        slot = k % W_SLOTS

        def swiglu(rows):
            x = _from_row_tiles(x_ref[0:rows])
            h1 = _dot(x, w1b[slot].astype(BF16))
            h3 = _dot(x, w3b[slot].astype(BF16))
            he = (h1 * _sigmoid(h1) * h3).astype(BF16)
            y_ref[0:rows] = _to_row_tiles(_dot(he, w2b[slot].astype(BF16)))

        @pl.when(half_ref[i] == 0)
        def _():
            swiglu(MOE_TM)

        @pl.when(half_ref[i] > 0)
        def _():
            swiglu(MOE_TM // 2)
            y_ref[MOE_TM // 2:] = jnp.zeros((MOE_TM // 2, ROW_SUB, ROW_LANE), BF16)

    @pl.when(i >= n_used)
    def _():
        y_ref[...] = jnp.zeros_like(y_ref)


def _experts(sched, xs, w_e1, w_e3, w_e2):
    grid_spec = pltpu.PrefetchScalarGridSpec(
        num_scalar_prefetch=6,
        grid=(MOE_NT,),
        in_specs=[
            pl.BlockSpec((MOE_TM, ROW_SUB, ROW_LANE), lambda i, nu, *_: (jnp.minimum(i, nu[0] - 1), 0, 0)),
            pl.BlockSpec(memory_space=pl.ANY),
            pl.BlockSpec(memory_space=pl.ANY),
            pl.BlockSpec(memory_space=pl.ANY),
        ],
        out_specs=pl.BlockSpec((MOE_TM, ROW_SUB, ROW_LANE), lambda i, *_: (i, 0, 0)),
        scratch_shapes=[
            pltpu.VMEM((W_SLOTS, D_MODEL, MOE_FF), F32),
            pltpu.VMEM((W_SLOTS, D_MODEL, MOE_FF), F32),
            pltpu.VMEM((W_SLOTS, MOE_FF, D_MODEL), F32),
            pltpu.SemaphoreType.DMA((3, W_SLOTS)),
        ],
    )
    return pl.pallas_call(
        _expert_kernel,
        grid_spec=grid_spec,
        out_shape=jax.ShapeDtypeStruct((MOE_ROWS, ROW_SUB, ROW_LANE), BF16),
        compiler_params=_params(1),
        name="experts",
    )(sched["n_used"], sched["tile_first"], sched["tile_ord"], sched["ord_expert"], sched["n_ord"],
      sched["tile_half"], xs, w_e1, w_e3, w_e2)


def _combine_kernel(pos_ref, ys_ref, x2_ref, route_ref, g_ref, yp_ref, ysm_ref, gbuf, sem):
    i = pl.program_id(0)

    def start_gather(tile, slot):
        def body(r, carry):
            for k in range(MOE_TOPK):
                src_row = pos_ref[k, tile * ROW_TILE + r]
                pltpu.make_async_copy(ys_ref.at[src_row], gbuf.at[slot, k, r], sem.at[slot, k]).start(priority=k)
            return carry
        lax.fori_loop(0, ROW_TILE, body, 0, unroll=8)

    @pl.when(i == 0)
    def _():
        start_gather(0, 0)

    slot = i % 2

    @pl.when(i + 1 < pl.num_programs(0))
    def _():
        start_gather(i + 1, 1 - slot)

    for k in range(MOE_TOPK):
        pltpu.make_async_copy(ys_ref.at[pl.ds(0, ROW_TILE)], gbuf.at[slot, k], sem.at[slot, k]).wait()

    route = route_ref[...]
    lane = lax.broadcasted_iota(jnp.int32, route.shape, 1)
    w0 = jnp.sum(jnp.where(lane == 4, route, 0.0), axis=-1, keepdims=True)
    w1 = jnp.sum(jnp.where(lane == 5, route, 0.0), axis=-1, keepdims=True)
    g0 = _from_row_tiles(gbuf[slot, 0]).astype(F32)
    g1 = _from_row_tiles(gbuf[slot, 1]).astype(F32)
    x3 = x2_ref[...] + (g0 * w0 + g1 * w1)
    y = _rms(x3, g_ref[...])

    @pl.when(i < NP_TILES)
    def _():
        yp_ref[...] = y

    @pl.when(i == NP_TILES)
    def _():
        ysm_ref[...] = y


def _combine(pos, ys, x2, route, g_f):
    grid_spec = pltpu.PrefetchScalarGridSpec(
        num_scalar_prefetch=1,
        grid=(N_TILES,),
        in_specs=[
            pl.BlockSpec(memory_space=pl.ANY),
            pl.BlockSpec((ROW_TILE, D_MODEL), lambda i, pos: (i, 0)),
            pl.BlockSpec((ROW_TILE, ROUTE_LANES), lambda i, pos: (i, 0)),
            pl.BlockSpec((1, D_MODEL), lambda i, pos: (0, 0)),
        ],
        out_specs=[
            pl.BlockSpec((ROW_TILE, D_MODEL), lambda i, pos: (jnp.minimum(i, NP_TILES - 1), 0)),
            pl.BlockSpec((ROW_TILE, D_MODEL), lambda i, pos: (0, 0)),
        ],
        scratch_shapes=[pltpu.VMEM((2, MOE_TOPK, ROW_TILE, ROW_SUB, ROW_LANE), BF16),
                        pltpu.SemaphoreType.DMA((2, MOE_TOPK))],
    )
    return pl.pallas_call(
        _combine_kernel,
        grid_spec=grid_spec,
        out_shape=[
            jax.ShapeDtypeStruct((N_P, D_MODEL), F32),
            jax.ShapeDtypeStruct((N_S, D_MODEL), F32),
        ],
        compiler_params=_params(1),
        name="combine_norm",
    )(pos, ys, x2, route, g_f)


def _expert_schedule(counts):
    i32 = jnp.int32
    cnt = counts[0, MOE_GROUPS:MOE_GROUPS + MOE_EXPERTS].astype(i32)
    tiles = (cnt + MOE_TM - 1) // MOE_TM
    tile_end = jnp.cumsum(tiles)
    tile_start = tile_end - tiles
    n_used = tile_end[-1]
    t = jnp.arange(MOE_NT, dtype=i32)
    tile_expert = jnp.minimum(jnp.sum((t[:, None] >= tile_end[None, :]).astype(i32), axis=1), MOE_EXPERTS - 1)
    used = tiles > 0
    ord_of = jnp.cumsum(used.astype(i32)) - 1
    experts = jnp.arange(MOE_EXPERTS, dtype=i32)
    ord_expert = jnp.zeros((MOE_EXPERTS,), i32).at[jnp.where(used, ord_of, MOE_EXPERTS)].set(experts, mode="drop")
    return {
        "n_used": n_used.reshape(1).astype(i32),
        "tile_first": ((t == tile_start[tile_expert]) & (t < n_used)).astype(i32),
        "tile_ord": ord_of[tile_expert].astype(i32),
        "tile_half": (cnt[tile_expert] - (t - tile_start[tile_expert]) * MOE_TM <= MOE_TM // 2).astype(i32),
        "ord_expert": ord_expert,
        "n_ord": jnp.sum(used.astype(i32)).reshape(1),
        "zero_row": (jnp.maximum(tile_end - 1, 0) * MOE_TM).astype(i32),
        "zero_on": used.astype(i32),
    }


def kernel(x_prompt, x_sample, mem_prompt, state_ret, cache_mem_k, cache_mem_v, norm_mix, w_in, ret_gn,
           sg_ln_g, sg_ln_b, sg_ws, sg_bs, w_a_out, w_b_out, w_o, norm_xa, w_cq, w_ck, w_cv, w_co, norm_moe,
           w_rg, b_rg, w_re, b_re, w_e1, w_e3, w_e2, norm_f):
    xp = x_prompt.reshape(N_P, D_MODEL)
    xs = x_sample.reshape(N_S, D_MODEL)

    h = _norm_rows(xp, xs, norm_mix)
    w = w_in[0]
    qk = _inproj("rope", IN_TM_ROPE, 0, 2 * RET_QK, h, w, _rope_tables(IN_TM_ROPE))
    v = _inproj("copy", IN_TM, 2 * RET_QK, RET_V, h, w)
    gs = _inproj("silu", IN_TM, 2 * RET_QK + RET_V, RET_V, h, w)
    uv = _inproj("gelu", IN_TM, 2 * RET_QK + 2 * RET_V, 2 * SG_WIDTH, h, w)
    gab = _inproj("sigmoid", IN_TM, 2 * RET_QK + 2 * RET_V + 2 * SG_WIDTH, 2 * D_MODEL, h, w)

    a_p, ret_p, a_s, ret_s = _retention(qk, v, gs, ret_gn, state_ret)

    b_all, sgv = _sgate(uv, sg_ln_g, sg_ln_b, sg_ws[0], sg_bs[0])

    x1, qx = _merge(a_p, a_s, b_all, gab, xp, xs, w_a_out[0].astype(BF16), w_b_out[0].astype(BF16),
                    w_o[0].astype(BF16), norm_xa, w_cq[0].astype(BF16))

    mk, mv, mkb, mvb = _memkv(mem_prompt.reshape(BATCH * MEM_LEN, D_MODEL), w_ck[0], w_cv[0])
    ck = cache_mem_k.reshape(DEC_BATCH, MEM_LEN * XA_HEADS, XA_DH)
    cv = cache_mem_v.reshape(DEC_BATCH, MEM_LEN * XA_HEADS, XA_DH)

    w_r = jnp.zeros((D_MODEL, ROUTE_LANES), F32)
    w_r = w_r.at[:, :MOE_GROUPS].set(w_rg[0]).at[:, MOE_GROUPS:MOE_GROUPS + MOE_EXPERTS].set(w_re[0])
    b_r = jnp.zeros((1, ROUTE_LANES), F32)
    b_r = b_r.at[0, :MOE_GROUPS].set(b_rg[0]).at[0, MOE_GROUPS:MOE_GROUPS + MOE_EXPERTS].set(b_re[0])
    x2, hmw, route, counts = _route(qx, mkb, mvb, ck, cv, x1, w_co[0].astype(BF16), norm_moe,
                                    w_r.astype(BF16), b_r)

    pos_t = _positions(route, counts)
    sched = _expert_schedule(counts)
    xs_sorted = _dispatch(pos_t, sched["zero_row"], sched["zero_on"], sched["n_used"], hmw)
    ys = _experts(sched, xs_sorted, w_e1[0], w_e3[0], w_e2[0])
    y_p, y_s = _combine(pos_t, ys, x2, route, norm_f.reshape(1, D_MODEL))

    return (y_p.reshape(BATCH, SEQ, D_MODEL),
            y_s.reshape(DEC_BATCH, DEC_SEQ, D_MODEL),
            ret_p,
            mk.reshape(1, BATCH, MEM_LEN, XA_HEADS, XA_DH),
            mv.reshape(1, BATCH, MEM_LEN, XA_HEADS, XA_DH),
            ret_s,
            sgv.reshape(1, DEC_BATCH, DEC_SEQ, SG_WIDTH))
```

```python
import functools

import jax
import jax.numpy as jnp
from jax import lax
from jax.experimental import pallas as pl
from jax.experimental.pallas import tpu as pltpu

F32 = jnp.float32
BF16 = jnp.bfloat16

D_MODEL = 2048
BATCH = 4
SEQ = 2048
DEC_BATCH = 128
DEC_SEQ = 4
PAST_LEN = 16384
RET_HEADS = 8
RET_DK = 128
RET_DV = 256
RET_CHUNK = 128
ROPE_BASE = 10000.0
RET_QK = RET_HEADS * RET_DK
RET_V = RET_HEADS * RET_DV
SG_GROUPS = 4
SG_WIDTH = 2048
SG_CHUNK = 128
MEM_LEN = 256
XA_HEADS = 4
XA_DH = 128
XA_W = XA_HEADS * XA_DH
MOE_GROUPS = 4
MOE_PER_GROUP = 8
MOE_EXPERTS = MOE_GROUPS * MOE_PER_GROUP
MOE_TOPK = 2
MOE_FF = 512
EPS = 1e-6
IN_WIDTH = 2 * RET_QK + 2 * RET_V + 2 * SG_WIDTH + 2 * D_MODEL

N_P = BATCH * SEQ
N_S = DEC_BATCH * DEC_SEQ
N_ALL = N_P + N_S
ROW_TILE = 512
N_TILES = N_ALL // ROW_TILE
NP_TILES = N_P // ROW_TILE
MERGE_TILE = 256
MERGE_P_TILES = N_P // MERGE_TILE

IN_TM = 1024
IN_TM_ROPE = 1024
IN_TN = 1024

PROMPT_CHUNK = 256
RET_BB = DEC_BATCH // (BATCH * (SEQ // PROMPT_CHUNK))
RET_ROWS = RET_BB * DEC_SEQ
LOG_DEC_SEQ = 2
LOG_SG_CHUNK = 7
SAMPLE_BB = DEC_BATCH // NP_TILES
SAMPLE_ROWS = SAMPLE_BB * DEC_SEQ

MOE_TM = 256
MOE_NT = (N_ALL * MOE_TOPK + MOE_EXPERTS * (MOE_TM - 1) + MOE_TM - 1) // MOE_TM
MOE_ROWS = MOE_NT * MOE_TM
ROUTE_LANES = 128
ROW_SUB, ROW_LANE = 16, 128
W_SLOTS = 3

VMEM_LIMIT = 56 * 1024 * 1024


def _params(n_axes, vmem=VMEM_LIMIT):
    return pltpu.CompilerParams(dimension_semantics=("arbitrary",) * n_axes,
                                vmem_limit_bytes=vmem)


def _rms(x, g):
    ms = jnp.mean(x * x, axis=-1, keepdims=True)
    return (x * lax.rsqrt(ms + EPS)) * g


def _dot(a, b):
    return jnp.dot(a, b, preferred_element_type=F32)


def _sigmoid(x):
    return 0.5 * jnp.tanh(0.5 * x) + 0.5


def _to_row_tiles(x):
    return x.astype(BF16).reshape(x.shape[0], ROW_SUB, ROW_LANE)


def _from_row_tiles(t):
    return t.reshape(t.shape[0], D_MODEL)


def _dot_nt(a, b):
    return lax.dot_general(a, b, (((1,), (1,)), ((), ())), preferred_element_type=F32)


def _dot_tn(a, b):
    return lax.dot_general(a, b, (((0,), (0,)), ((), ())), preferred_element_type=F32)


def _norm_kernel(xp_ref, xs_ref, g_ref, h_ref):
    i = pl.program_id(0)

    @pl.when(i < NP_TILES)
    def _():
        h_ref[...] = _rms(xp_ref[...], g_ref[...]).astype(BF16)

    @pl.when(i == NP_TILES)
    def _():
        h_ref[...] = _rms(xs_ref[...], g_ref[...]).astype(BF16)


def _norm_rows(xp, xs, g):
    return pl.pallas_call(
        _norm_kernel,
        grid=(N_TILES,),
        in_specs=[
            pl.BlockSpec((ROW_TILE, D_MODEL), lambda i: (jnp.minimum(i, NP_TILES - 1), 0)),
            pl.BlockSpec((ROW_TILE, D_MODEL), lambda i: (0, 0)),
            pl.BlockSpec((1, D_MODEL), lambda i: (0, 0)),
        ],
        out_specs=pl.BlockSpec((ROW_TILE, D_MODEL), lambda i: (i, 0)),
        out_shape=jax.ShapeDtypeStruct((N_ALL, D_MODEL), BF16),
        compiler_params=_params(1),
        name="norm_rows",
    )(xp, xs, g)


def _inproj_kernel(kind, tm, h_ref, w_ref, *rest):
    if kind == "rope":
        cos_ref, sin_ref, z_ref, wb_ref = rest
    else:
        z_ref, wb_ref = rest
    j = pl.program_id(0)
    i = pl.program_id(1)
    last = N_P // tm

    del wb_ref

    def tile(rows):
        acc = _dot(h_ref[0:rows, :], w_ref[...].astype(BF16))
        if kind == "rope":
            scale = jnp.where(j == 1, RET_DK ** -0.5, 1.0).astype(F32)
            c = cos_ref[0:rows, :]
            s = sin_ref[0:rows, :]
            for hb in range(IN_TN // RET_DK):
                cols = slice(hb * RET_DK, (hb + 1) * RET_DK)
                a = acc[:, cols]
                r = pltpu.roll(a, RET_DK // 2, axis=1)
                z_ref[0:rows, cols] = ((a * c + r * s) * scale).astype(BF16)
        elif kind == "copy":
            z_ref[0:rows, :] = acc.astype(BF16)
        elif kind == "silu":
            z_ref[0:rows, :] = (acc * _sigmoid(acc)).astype(BF16)
        elif kind == "gelu":
            z_ref[0:rows, :] = jax.nn.gelu(acc).astype(BF16)
        else:
            z_ref[0:rows, :] = _sigmoid(acc).astype(BF16)

    @pl.when(i < last)
    def _():
        tile(tm)

    @pl.when(i == last)
    def _():
        tile(N_S)


def _inproj(kind, tm, col0, width, h, w_in, tables=()):
    last = N_P // tm
    tab_idx = lambda j, i: (jnp.where(i < last, i % (SEQ // tm), SEQ // tm), 0)
    j0 = col0 // IN_TN
    return pl.pallas_call(
        functools.partial(_inproj_kernel, kind, tm),
        grid=(width // IN_TN, last + 1),
        in_specs=[
            pl.BlockSpec((tm, D_MODEL), lambda j, i: (i, 0)),
            pl.BlockSpec((D_MODEL, IN_TN), lambda j, i: (0, j0 + j)),
        ] + [pl.BlockSpec((tm, RET_DK), tab_idx) for _ in tables],
        out_specs=pl.BlockSpec((tm, IN_TN), lambda j, i: (i, j)),
        out_shape=jax.ShapeDtypeStruct((N_ALL, width), BF16),
        scratch_shapes=[pltpu.VMEM((D_MODEL, IN_TN), BF16)],
        compiler_params=_params(2),
        name="in_proj_" + kind,
    )(h, w_in, *tables)


def _rope_tables(tm):
    half = RET_DK // 2
    inv = ROPE_BASE ** (-jnp.arange(half, dtype=F32) / half)

    def tab(pos):
        ang = pos.astype(F32)[:, None] * inv[None, :]
        c, s = jnp.cos(ang), jnp.sin(ang)
        return jnp.concatenate([c, c], -1), jnp.concatenate([-s, s], -1)

    cp, sp = tab(jnp.arange(SEQ, dtype=jnp.int32))
    cs, ss = tab(PAST_LEN + jnp.arange(DEC_SEQ, dtype=jnp.int32))
    cs = jnp.tile(cs, (DEC_BATCH, 1))
    ss = jnp.tile(ss, (DEC_BATCH, 1))
    pad = jnp.zeros((tm - N_S, RET_DK), F32)
    return jnp.concatenate([cp, cs, pad], 0), jnp.concatenate([sp, ss, pad], 0)


def _decay_tables(chunk):
    lg = jnp.log1p(-jnp.power(2.0, -5.0 - jnp.arange(RET_HEADS, dtype=F32)))
    idx = jnp.arange(chunk, dtype=F32)
    rel = idx[:, None] - idx[None, :]
    dmask = jnp.where(rel >= 0, jnp.exp(lg[:, None, None] * jnp.maximum(rel, 0.0)), 0.0).astype(F32)
    xi = jnp.exp(lg[:, None] * (idx[None, :] + 1.0)).astype(F32)
    zeta = jnp.exp(lg[:, None] * (chunk - 1.0 - idx[None, :])).astype(F32)
    gc = jnp.exp(lg * chunk).astype(F32)
    return dmask, xi, zeta, gc


def _head_norm_gate(o, gn, gate):
    mu = jnp.mean(o, axis=-1, keepdims=True)
    d = o - mu
    var = jnp.mean(d * d, axis=-1, keepdims=True)
    y = (d * lax.rsqrt(var + EPS)) * gn
    return (gate.astype(F32) * y).astype(BF16)


def _ret_kernel(gcp_ref, gcs_ref, q_ref, k_ref, v_ref, gs_ref, gn_ref, dm_ref, xi_ref, zt_ref,
                qs_ref, ks_ref, vs_ref, gss_ref, dms_ref, xis_ref, zts_ref, s0_ref,
                a_ref, sfin_ref, as_ref, s1_ref, s_ref):
    c = pl.program_id(1)

    @pl.when(c == 0)
    def _():
        s_ref[...] = jnp.zeros_like(s_ref)

    rows_k = lax.broadcasted_iota(jnp.int32, (RET_ROWS, RET_DK), 0) >> LOG_DEC_SEQ
    rows_v = lax.broadcasted_iota(jnp.int32, (RET_ROWS, RET_DV), 0) >> LOG_DEC_SEQ
    for h in range(RET_HEADS):
        kc = slice(h * RET_DK, (h + 1) * RET_DK)
        vc = slice(h * RET_DV, (h + 1) * RET_DV)
        gn = gn_ref[:, vc]

        qh = q_ref[:, kc]
        kh = k_ref[:, kc]
        vh = v_ref[:, vc]
        inner = _dot_nt(qh, kh) * dm_ref[h]
        o = _dot(inner.astype(BF16), vh)
        s_old = s_ref[h]
        xi = xi_ref[h]
        o = o + _dot(qh, s_old.astype(BF16)) * jnp.concatenate([xi, xi], axis=1)
        kz = (kh.astype(F32) * zt_ref[h]).astype(BF16)
        s_ref[h] = gcp_ref[h] * s_old + _dot_tn(kz, vh)
        a_ref[:, vc] = _head_norm_gate(o, gn, gs_ref[:, vc])

        qh = qs_ref[:, kc]
        kh = ks_ref[:, kc]
        vh = vs_ref[:, vc]
        inner = _dot_nt(qh, kh) * dms_ref[h]
        o = _dot(inner.astype(BF16), vh)
        xi = xis_ref[h]
        xi2 = jnp.concatenate([xi, xi], axis=1)
        kz = kh.astype(F32) * zts_ref[h]
        gch = gcs_ref[h]
        for b in range(RET_BB):
            s_old = s0_ref[0, b, h]
            cross = _dot(qh, s_old.astype(BF16)) * xi2
            o = o + jnp.where(rows_v == b, cross, 0.0)
            kz_b = jnp.where(rows_k == b, kz, 0.0).astype(BF16)
            s1_ref[0, b, h] = gch * s_old + _dot_tn(kz_b, vh)
        as_ref[:, vc] = _head_norm_gate(o, gn, gss_ref[:, vc])

    @pl.when(c == pl.num_programs(1) - 1)
    def _():
        sfin_ref[0, 0] = s_ref[...]


def _retention(qk, v, gs, ret_gn, state):
    chunk = PROMPT_CHUNK
    n_chunks = SEQ // chunk
    dmask, xi, zeta, gc = _decay_tables(chunk)
    xi_b = jnp.broadcast_to(xi[:, :, None], (RET_HEADS, chunk, RET_DK))
    zeta_b = jnp.broadcast_to(zeta[:, :, None], (RET_HEADS, chunk, RET_DK))
    dmask_s, xi_s, zeta_s, gc_s = _decay_tables(DEC_SEQ)
    eye = jnp.eye(RET_BB, dtype=F32)
    dm_big = jax.vmap(lambda m: jnp.kron(eye, m))(dmask_s)
    xi_sb = jnp.broadcast_to(jnp.tile(xi_s, (1, RET_BB))[:, :, None], (RET_HEADS, RET_ROWS, RET_DK))
    zeta_sb = jnp.broadcast_to(jnp.tile(zeta_s, (1, RET_BB))[:, :, None], (RET_HEADS, RET_ROWS, RET_DK))

    row = lambda b, c: b * n_chunks + c
    srow = lambda b, c: N_P // RET_ROWS + row(b, c)
    const3 = lambda b, c: (0, 0, 0)
    smem = pl.BlockSpec(memory_space=pltpu.SMEM)
    st_spec = pl.BlockSpec((1, RET_BB, RET_HEADS, RET_DK, RET_DV), lambda b, c: (0, row(b, c), 0, 0, 0))
    return pl.pallas_call(
        _ret_kernel,
        grid=(BATCH, n_chunks),
        in_specs=[
            smem, smem,
            pl.BlockSpec((chunk, RET_QK), lambda b, c: (row(b, c), 0)),
            pl.BlockSpec((chunk, RET_QK), lambda b, c: (row(b, c), 1)),
            pl.BlockSpec((chunk, RET_V), lambda b, c: (row(b, c), 0)),
            pl.BlockSpec((chunk, RET_V), lambda b, c: (row(b, c), 0)),
            pl.BlockSpec((1, RET_V), lambda b, c: (0, 0)),
            pl.BlockSpec((RET_HEADS, chunk, chunk), const3),
            pl.BlockSpec((RET_HEADS, chunk, RET_DK), const3),
            pl.BlockSpec((RET_HEADS, chunk, RET_DK), const3),
            pl.BlockSpec((RET_ROWS, RET_QK), lambda b, c: (srow(b, c), 0)),
            pl.BlockSpec((RET_ROWS, RET_QK), lambda b, c: (srow(b, c), 1)),
            pl.BlockSpec((RET_ROWS, RET_V), lambda b, c: (srow(b, c), 0)),
            pl.BlockSpec((RET_ROWS, RET_V), lambda b, c: (srow(b, c), 0)),
            pl.BlockSpec((RET_HEADS, RET_ROWS, RET_ROWS), const3),
            pl.BlockSpec((RET_HEADS, RET_ROWS, RET_DK), const3),
            pl.BlockSpec((RET_HEADS, RET_ROWS, RET_DK), const3),
            st_spec,
        ],
        out_specs=[
            pl.BlockSpec((chunk, RET_V), lambda b, c: (row(b, c), 0)),
            pl.BlockSpec((1, 1, RET_HEADS, RET_DK, RET_DV), lambda b, c: (0, b, 0, 0, 0)),
            pl.BlockSpec((RET_ROWS, RET_V), lambda b, c: (row(b, c), 0)),
            st_spec,
        ],
        out_shape=[
            jax.ShapeDtypeStruct((N_P, RET_V), BF16),
            jax.ShapeDtypeStruct((1, BATCH, RET_HEADS, RET_DK, RET_DV), F32),
            jax.ShapeDtypeStruct((N_S, RET_V), BF16),
            jax.ShapeDtypeStruct((1, DEC_BATCH, RET_HEADS, RET_DK, RET_DV), F32),
        ],
        scratch_shapes=[pltpu.VMEM((RET_HEADS, RET_DK, RET_DV), F32)],
        compiler_params=_params(2),
        name="retention",
    )(gc, gc_s, qk, qk, v, gs, ret_gn, dmask, xi_b, zeta_b, qk, qk, v, gs, dm_big, xi_sb, zeta_sb, state)


def _sgate_kernel(u_ref, v_ref, lg_ref, lb_ref, wp_ref, bp_ref, ws_ref, bs_ref, b_ref, sgv_ref, vln_ref):
    i = pl.program_id(0)
    v = v_ref[...].astype(F32)
    mu = jnp.mean(v, axis=-1, keepdims=True)
    d = v - mu
    var = jnp.mean(d * d, axis=-1, keepdims=True)
    vln_ref[...] = (d * lax.rsqrt(var + EPS)) * lg_ref[...] + lb_ref[...]
    gw = SG_WIDTH // SG_GROUPS
    lane_reps = gw // 128

    @pl.when(i < NP_TILES)
    def _():
        r = lax.broadcasted_iota(jnp.int32, (SG_CHUNK, SG_CHUNK), 0)
        c = lax.broadcasted_iota(jnp.int32, (SG_CHUNK, SG_CHUNK), 1)
        for g in range(SG_GROUPS):
            cols = slice(g * gw, (g + 1) * gw)
            w = jnp.where(c <= r, wp_ref[g], 0.0).astype(BF16)
            bias = jnp.concatenate([bp_ref[g]] * lane_reps, axis=1)
            chunks = [slice(ch * SG_CHUNK, (ch + 1) * SG_CHUNK) for ch in range(ROW_TILE // SG_CHUNK)]
            mixed = [_dot(w, vln_ref[rows, cols].astype(BF16)) for rows in chunks]
            for rows, m in zip(chunks, mixed):
                b_ref[rows, cols] = (u_ref[rows, cols].astype(F32) * (m + bias)).astype(BF16)

    @pl.when(i == NP_TILES)
    def _():
        sgv_ref[...] = vln_ref[...]
        r = lax.broadcasted_iota(jnp.int32, (ROW_TILE, ROW_TILE), 0)
        c = lax.broadcasted_iota(jnp.int32, (ROW_TILE, ROW_TILE), 1)
        keep = ((r >> LOG_DEC_SEQ) == (c >> LOG_DEC_SEQ)) & (c <= r)
        for g in range(SG_GROUPS):
            cols = slice(g * gw, (g + 1) * gw)
            w_rows = jnp.concatenate([ws_ref[g]] * (ROW_TILE // 8), axis=0)
            w_full = jnp.concatenate([w_rows] * (ROW_TILE // 128), axis=1)
            w = jnp.where(keep, w_full, 0.0).astype(BF16)
            b_rows = jnp.concatenate([bs_ref[g]] * (ROW_TILE // 8), axis=0)
            bias = jnp.concatenate([b_rows] * lane_reps, axis=1)
            mixed = _dot(w, vln_ref[:, cols].astype(BF16)) + bias
            b_ref[:, cols] = (u_ref[:, cols].astype(F32) * mixed).astype(BF16)


def _sgate(uv, ln_g, ln_b, sg_ws, sg_bs):
    b_p = jnp.broadcast_to(sg_bs[:, :, None], (SG_GROUPS, SG_CHUNK, 128))
    w_s = jnp.tile(sg_ws[:, :DEC_SEQ, :DEC_SEQ], (1, 8 // DEC_SEQ, 128 // DEC_SEQ))
    b_s = jnp.broadcast_to(jnp.tile(sg_bs[:, :DEC_SEQ], (1, 8 // DEC_SEQ))[:, :, None], (SG_GROUPS, 8, 128))
    const3 = lambda i: (0, 0, 0)
    return pl.pallas_call(
        _sgate_kernel,
        grid=(N_TILES,),
        in_specs=[
            pl.BlockSpec((ROW_TILE, SG_WIDTH), lambda i: (i, 0)),
            pl.BlockSpec((ROW_TILE, SG_WIDTH), lambda i: (i, 1)),
            pl.BlockSpec((1, SG_WIDTH), lambda i: (0, 0)),
            pl.BlockSpec((1, SG_WIDTH), lambda i: (0, 0)),
            pl.BlockSpec((SG_GROUPS, SG_CHUNK, SG_CHUNK), const3),
            pl.BlockSpec((SG_GROUPS, SG_CHUNK, 128), const3),
            pl.BlockSpec((SG_GROUPS, 8, 128), const3),
            pl.BlockSpec((SG_GROUPS, 8, 128), const3),
        ],
        out_specs=[
            pl.BlockSpec((ROW_TILE, SG_WIDTH), lambda i: (i, 0)),
            pl.BlockSpec((N_S, SG_WIDTH), lambda i: (0, 0)),
        ],
        out_shape=[
            jax.ShapeDtypeStruct((N_ALL, SG_WIDTH), BF16),
            jax.ShapeDtypeStruct((N_S, SG_WIDTH), F32),
        ],
        scratch_shapes=[pltpu.VMEM((ROW_TILE, SG_WIDTH), F32)],
        compiler_params=_params(1),
        name="spatial_gate",
    )(uv, uv, ln_g, ln_b, sg_ws, b_p, w_s, b_s)


def _merge_kernel(ap_ref, as_ref, b_ref, ga_ref, gb_ref, xp_ref, xs_ref, wa_ref, wb_ref, wo_ref,
                  g_ref, wq_ref, x1_ref, q_ref):
    i = pl.program_id(0)

    def run(a_in, x):
        a = _dot(a_in, wa_ref[...])
        b = _dot(b_ref[...], wb_ref[...])
        merged = ga_ref[...].astype(F32) * a + gb_ref[...].astype(F32) * b
        x1 = x + _dot(merged.astype(BF16), wo_ref[...])
        x1_ref[...] = x1
        q_ref[...] = _dot(_rms(x1, g_ref[...]).astype(BF16), wq_ref[...]).astype(BF16)

    @pl.when(i < MERGE_P_TILES)
    def _():
        run(ap_ref[...], xp_ref[...])

    @pl.when(i >= MERGE_P_TILES)
    def _():
        run(as_ref[...], xs_ref[...])


def _resident(shape):
    return pl.BlockSpec(shape, lambda i: (0,) * len(shape), pipeline_mode=pl.Buffered(1))


def _merge(a_p, a_s, b_all, gab, xp, xs, wa, wb, wo, g_xa, wq):
    tm = MERGE_TILE
    prompt_tile = lambda i: (jnp.minimum(i, MERGE_P_TILES - 1), 0)
    sample_tile = lambda i: (jnp.maximum(i - MERGE_P_TILES, 0), 0)
    return pl.pallas_call(
        _merge_kernel,
        grid=(N_ALL // tm,),
        in_specs=[
            pl.BlockSpec((tm, RET_V), prompt_tile),
            pl.BlockSpec((tm, RET_V), sample_tile),
            pl.BlockSpec((tm, SG_WIDTH), lambda i: (i, 0)),
            pl.BlockSpec((tm, D_MODEL), lambda i: (i, 0)),
            pl.BlockSpec((tm, D_MODEL), lambda i: (i, 1)),
            pl.BlockSpec((tm, D_MODEL), prompt_tile),
            pl.BlockSpec((tm, D_MODEL), sample_tile),
            _resident((RET_V, D_MODEL)),
            _resident((SG_WIDTH, D_MODEL)),
            _resident((D_MODEL, D_MODEL)),
            pl.BlockSpec((1, D_MODEL), lambda i: (0, 0)),
            _resident((D_MODEL, XA_W)),
        ],
        out_specs=[
            pl.BlockSpec((tm, D_MODEL), lambda i: (i, 0)),
            pl.BlockSpec((tm, XA_W), lambda i: (i, 0)),
        ],
        out_shape=[
            jax.ShapeDtypeStruct((N_ALL, D_MODEL), F32),
            jax.ShapeDtypeStruct((N_ALL, XA_W), BF16),
        ],
        compiler_params=_params(1),
        name="merge_proj",
    )(a_p, a_s, b_all, gab, gab, xp, xs, wa, wb, wo, g_xa, wq)


def _memkv_kernel(m_ref, wk_ref, wv_ref, k_ref, v_ref, kb_ref, vb_ref):
    m = m_ref[...].astype(BF16)
    k = _dot(m, wk_ref[...].astype(BF16))
    v = _dot(m, wv_ref[...].astype(BF16))
    k_ref[...] = k
    v_ref[...] = v
    kb_ref[...] = k.astype(BF16)
    vb_ref[...] = v.astype(BF16)


def _memkv(mem, w_ck, w_cv):
    rows = BATCH * MEM_LEN
    spec = pl.BlockSpec((MEM_LEN, XA_W), lambda b: (b, 0))
    wspec = pl.BlockSpec((D_MODEL, XA_W), lambda b: (0, 0))
    return pl.pallas_call(
        _memkv_kernel,
        grid=(BATCH,),
        in_specs=[pl.BlockSpec((MEM_LEN, D_MODEL), lambda b: (b, 0)), wspec, wspec],
        out_specs=[spec, spec, spec, spec],
        out_shape=[jax.ShapeDtypeStruct((rows, XA_W), F32)] * 2 + [jax.ShapeDtypeStruct((rows, XA_W), BF16)] * 2,
        compiler_params=_params(1),
        name="mem_kv",
    )(mem, w_ck, w_cv)


def _softmax_rows(s):
    m = jnp.max(s, axis=-1, keepdims=True)
    e = jnp.exp(s - m)
    return e / jnp.sum(e, axis=-1, keepdims=True)


def _xattn_prompt_rows(q_ref, k_ref, v_ref):
    cols = [slice(h * XA_DH, (h + 1) * XA_DH) for h in range(XA_HEADS)]
    scores = [_dot_nt(q_ref[:, c], k_ref[:, c]) * (XA_DH ** -0.5) for c in cols]
    probs = [_softmax_rows(s).astype(BF16) for s in scores]
    heads = [_dot(p, v_ref[:, c]).astype(BF16) for p, c in zip(probs, cols)]
    return jnp.concatenate(heads, axis=1)


def _xattn_sample_scores(q_ref, k_ref):
    qf = q_ref[...].astype(F32)
    scores = []
    for b in range(SAMPLE_BB):
        rows = slice(b * DEC_SEQ, (b + 1) * DEC_SEQ)
        qb = jnp.concatenate([qf[rows, h * XA_DH:(h + 1) * XA_DH] for h in range(XA_HEADS)], axis=0)
        scores.append(_dot_nt(qb.astype(BF16), k_ref[b].astype(BF16)) * (XA_DH ** -0.5))
    return scores


def _xattn_sample_finish(scores, v_ref, o_ref):
    n_q = XA_HEADS * DEC_SEQ
    n_kv = MEM_LEN * XA_HEADS
    r = lax.broadcasted_iota(jnp.int32, (n_q, n_kv), 0)
    c = lax.broadcasted_iota(jnp.int32, (n_q, n_kv), 1)
    head_ok = (r >> LOG_DEC_SEQ) == (c & (XA_HEADS - 1))
    probs = [_softmax_rows(jnp.where(head_ok, s, -1e30)).astype(BF16) for s in scores]
    for b in range(SAMPLE_BB):
        rows = slice(b * DEC_SEQ, (b + 1) * DEC_SEQ)
        o = _dot(probs[b], v_ref[b].astype(BF16))
        for h in range(XA_HEADS):
            o_ref[h, rows, :] = o[h * DEC_SEQ:(h + 1) * DEC_SEQ, :]


def _route_kernel(qp_ref, mk_ref, mv_ref, qs_ref, ck_ref, cv_ref, x1_ref, wo_ref, g_ref, wr_ref, br_ref,
                  x2_ref, hm_ref, route_ref, cnt_ref, run_ref, os_ref):
    i = pl.program_id(0)

    @pl.when(i == 0)
    def _():
        run_ref[...] = jnp.zeros_like(run_ref)

    def run(o, between=lambda: None):
        x2 = x1_ref[...] + _dot(o, wo_ref[...])
        x2_ref[...] = x2
        hm = _rms(x2, g_ref[...])
        hm_ref[...] = _to_row_tiles(hm)
        logits = _dot(hm.astype(BF16), wr_ref[...]) + br_ref[...]
        between()
        lane = lax.broadcasted_iota(jnp.int32, logits.shape, 1)
        lane_f = lane.astype(F32)
        neg = jnp.float32(-jnp.inf)
        big = jnp.float32(1 << 20)
        is_g = lane < MOE_GROUPS
        gl = jnp.where(is_g, logits, neg)
        gmax = jnp.max(gl, axis=-1, keepdims=True)
        g_sel = jnp.min(jnp.where(gl == gmax, lane_f, big), axis=-1, keepdims=True)
        g_w = 1.0 / jnp.sum(jnp.where(is_g, jnp.exp(logits - gmax), 0.0), axis=-1, keepdims=True)
        e_lane = lane - MOE_GROUPS
        e_group = (e_lane >> 3).astype(F32)
        in_grp = (e_lane >= 0) & (e_lane < MOE_EXPERTS) & (e_group == g_sel)
        el = jnp.where(in_grp, logits, neg)
        v0 = jnp.max(el, axis=-1, keepdims=True)
        i0 = jnp.min(jnp.where(el == v0, lane_f, big), axis=-1, keepdims=True)
        el1 = jnp.where(lane_f == i0, neg, el)
        v1 = jnp.max(el1, axis=-1, keepdims=True)
        i1 = jnp.min(jnp.where(el1 == v1, lane_f, big), axis=-1, keepdims=True)
        ex = jnp.exp(v1 - v0)
        den = 1.0 + ex
        w0 = (1.0 / den) * g_w
        w1 = (ex / den) * g_w
        a0 = (lane_f == i0).astype(F32)
        a1 = (lane_f == i1).astype(F32)
        a = a0 + a1
        rr = lax.broadcasted_iota(jnp.int32, (ROW_TILE, ROW_TILE), 0)
        cc = lax.broadcasted_iota(jnp.int32, (ROW_TILE, ROW_TILE), 1)
        lower = jnp.where(cc < rr, 1.0, 0.0).astype(BF16)
        before = _dot(lower, a.astype(BF16)) + run_ref[...]
        rank0 = jnp.sum(before * a0, axis=-1, keepdims=True)
        rank1 = jnp.sum(before * a1, axis=-1, keepdims=True)
        run_ref[...] += jnp.sum(a, axis=0, keepdims=True)
        e0 = i0 - MOE_GROUPS
        e1 = i1 - MOE_GROUPS
        route = jnp.where(lane == 0, e0, 0.0)
        route = jnp.where(lane == 1, e1, route)
        route = jnp.where(lane == 2, rank0, route)
        route = jnp.where(lane == 3, rank1, route)
        route = jnp.where(lane == 4, w0, route)
        route = jnp.where(lane == 5, w1, route)
        route_ref[...] = route
        cnt_ref[...] = run_ref[...]

    @pl.when(i < NP_TILES)
    def _():
        scores = _xattn_sample_scores(qs_ref, ck_ref)
        run(_xattn_prompt_rows(qp_ref, mk_ref, mv_ref),
            between=lambda: _xattn_sample_finish(scores, cv_ref, os_ref.at[i]))

    @pl.when(i == NP_TILES)
    def _():
        heads = [os_ref[:, h].reshape(N_S, XA_DH) for h in range(XA_HEADS)]
        run(jnp.concatenate(heads, axis=1).astype(BF16))


def _route(qx, mkb, mvb, ck, cv, x1, w_co, g_moe, w_r, b_r):
    prompt_step = lambda i: jnp.minimum(i, NP_TILES - 1)
    kv = pl.BlockSpec((SAMPLE_BB, MEM_LEN * XA_HEADS, XA_DH), lambda i: (prompt_step(i), 0, 0))
    mem_kv = pl.BlockSpec((MEM_LEN, XA_W), lambda i: (prompt_step(i) // (SEQ // ROW_TILE), 0))
    return pl.pallas_call(
        _route_kernel,
        grid=(N_TILES,),
        in_specs=[
            pl.BlockSpec((ROW_TILE, XA_W), lambda i: (prompt_step(i), 0)),
            mem_kv, mem_kv,
            pl.BlockSpec((SAMPLE_ROWS, XA_W), lambda i: (N_P // SAMPLE_ROWS + prompt_step(i), 0)),
            kv, kv,
            pl.BlockSpec((ROW_TILE, D_MODEL), lambda i: (i, 0)),
            _resident((XA_W, D_MODEL)),
            pl.BlockSpec((1, D_MODEL), lambda i: (0, 0)),
            _resident((D_MODEL, ROUTE_LANES)),
            pl.BlockSpec((1, ROUTE_LANES), lambda i: (0, 0)),
        ],
        out_specs=[
            pl.BlockSpec((ROW_TILE, D_MODEL), lambda i: (i, 0)),
            pl.BlockSpec((ROW_TILE, ROW_SUB, ROW_LANE), lambda i: (i, 0, 0)),
            pl.BlockSpec((ROW_TILE, ROUTE_LANES), lambda i: (i, 0)),
            pl.BlockSpec((1, ROUTE_LANES), lambda i: (0, 0)),
        ],
        out_shape=[
            jax.ShapeDtypeStruct((N_ALL, D_MODEL), F32),
            jax.ShapeDtypeStruct((N_ALL, ROW_SUB, ROW_LANE), BF16),
            jax.ShapeDtypeStruct((N_ALL, ROUTE_LANES), F32),
            jax.ShapeDtypeStruct((1, ROUTE_LANES), F32),
        ],
        scratch_shapes=[pltpu.VMEM((1, ROUTE_LANES), F32),
                        pltpu.VMEM((NP_TILES, XA_HEADS, SAMPLE_ROWS, XA_DH), F32)],
        compiler_params=_params(1),
        name="xa_out_route",
    )(qx, mkb, mvb, qx, ck, cv, x1, w_co, g_moe, w_r, b_r)


def _positions_kernel(route_ref, cnt_ref, pos_ref):
    route = route_ref[...]
    lane = lax.broadcasted_iota(jnp.int32, route.shape, 1)
    lane_f = lane.astype(F32)
    tiles = jnp.floor((cnt_ref[...] + (MOE_TM - 1)) * (1.0 / MOE_TM))
    lr = lax.broadcasted_iota(jnp.int32, (ROUTE_LANES, ROUTE_LANES), 0)
    lc = lax.broadcasted_iota(jnp.int32, (ROUTE_LANES, ROUTE_LANES), 1)
    before = jnp.where(lr < lc, 1.0, 0.0).astype(BF16)
    tiles8 = jnp.broadcast_to(tiles, (8, ROUTE_LANES)).astype(BF16)
    start = _dot(tiles8, before)[0:1, :] * MOE_TM

    def col(k):
        return jnp.sum(jnp.where(lane == k, route, 0.0), axis=-1, keepdims=True)

    def first_row(e):
        return jnp.sum(jnp.where(lane_f == e + MOE_GROUPS, start, 0.0), axis=-1, keepdims=True)

    p0 = first_row(col(0)) + col(2)
    p1 = first_row(col(1)) + col(3)
    p = jnp.where(lane == 0, p0, jnp.where(lane == 1, p1, 0.0))
    pos_ref[...] = p.T[0:8, :].astype(jnp.int32)


def _positions(route, counts):
    rows = N_ALL // 4
    return pl.pallas_call(
        _positions_kernel,
        grid=(4,),
        in_specs=[
            pl.BlockSpec((rows, ROUTE_LANES), lambda i: (i, 0)),
            pl.BlockSpec((1, ROUTE_LANES), lambda i: (0, 0)),
        ],
        out_specs=pl.BlockSpec((8, rows), lambda i: (0, i)),
        out_shape=jax.ShapeDtypeStruct((8, N_ALL), jnp.int32),
        compiler_params=_params(1),
        name="positions",
    )(route, counts)


def _dispatch_kernel(pos_ref, zrow_ref, zon_ref, nu_ref, hm_ref, xs_ref, zbuf, sem, zsem, tbuf, tsem):
    i = pl.program_id(0)

    def zero_tile(row):
        return pltpu.make_async_copy(zbuf, xs_ref.at[pl.ds(pl.multiple_of(row, MOE_TM), MOE_TM)], zsem)

    @pl.when(i == 0)
    def _():
        zbuf[...] = jnp.zeros_like(zbuf)
        for e in range(MOE_EXPERTS):
            @pl.when(zon_ref[e] > 0)
            def _():
                zero_tile(zrow_ref[e]).start()

        def start_tail(t, carry):
            zero_tile(t * MOE_TM).start()
            return carry

        def wait_tail(t, carry):
            zero_tile(t * MOE_TM).wait()
            return carry

        lax.fori_loop(nu_ref[0], MOE_NT, start_tail, 0)
        for e in range(MOE_EXPERTS):
            @pl.when(zon_ref[e] > 0)
            def _():
                zero_tile(zrow_ref[e]).wait()
        lax.fori_loop(nu_ref[0], MOE_NT, wait_tail, 0)

    n_steps = pl.num_programs(0)

    def tile_copy(t):
        return pltpu.make_async_copy(hm_ref.at[pl.ds(t * ROW_TILE, ROW_TILE)], tbuf.at[t % 3], tsem.at[t % 3])

    def wait_rows(t):
        for k in range(MOE_TOPK):
            pltpu.make_async_copy(tbuf.at[t % 3], xs_ref.at[pl.ds(0, ROW_TILE)], sem.at[t % 2, k]).wait()

    @pl.when(i == 0)
    def _():
        tile_copy(0).start()

    tile_copy(i).wait()

    @pl.when(i + 1 < n_steps)
    def _():
        tile_copy(i + 1).start()

    slot = i % 3
    par = i % 2
    base = i * ROW_TILE

    def start(r, carry):
        for k in range(MOE_TOPK):
            dst_row = pos_ref[k, base + r]
            pltpu.make_async_copy(tbuf.at[slot, r], xs_ref.at[dst_row], sem.at[par, k]).start(priority=k)
        return carry

    lax.fori_loop(0, ROW_TILE, start, 0, unroll=8)

    @pl.when(i > 0)
    def _():
        wait_rows(i - 1)

    @pl.when(i == n_steps - 1)
    def _():
        wait_rows(i)


def _dispatch(pos_t, zero_row, zero_on, n_used, hmw):
    grid_spec = pltpu.PrefetchScalarGridSpec(
        num_scalar_prefetch=4,
        grid=(N_TILES,),
        in_specs=[pl.BlockSpec(memory_space=pl.ANY)],
        out_specs=pl.BlockSpec(memory_space=pl.ANY),
        scratch_shapes=[pltpu.VMEM((MOE_TM, ROW_SUB, ROW_LANE), BF16),
                        pltpu.SemaphoreType.DMA((2, MOE_TOPK)), pltpu.SemaphoreType.DMA(()),
                        pltpu.VMEM((3, ROW_TILE, ROW_SUB, ROW_LANE), BF16), pltpu.SemaphoreType.DMA((3,))],
    )
    return pl.pallas_call(
        _dispatch_kernel,
        grid_spec=grid_spec,
        out_shape=jax.ShapeDtypeStruct((MOE_ROWS, ROW_SUB, ROW_LANE), BF16),
        compiler_params=_params(1),
        name="dispatch",
    )(pos_t, zero_row, zero_on, n_used, hmw)


def _expert_kernel(nu_ref, first_ref, ord_ref, oe_ref, no_ref, half_ref, x_ref, w1_hbm, w3_hbm, w2_hbm, y_ref,
                   w1b, w3b, w2b, sem):
    i = pl.program_id(0)
    n_used = nu_ref[0]
    n_ord = no_ref[0]

    def weight_copies(k):
        e = oe_ref[k]
        slot = k % W_SLOTS
        return (pltpu.make_async_copy(w1_hbm.at[e], w1b.at[slot], sem.at[0, slot]),
                pltpu.make_async_copy(w3_hbm.at[e], w3b.at[slot], sem.at[1, slot]),
                pltpu.make_async_copy(w2_hbm.at[e], w2b.at[slot], sem.at[2, slot]))

    def start_weights(k):
        for cp in weight_copies(k):
            cp.start(priority=1)

    @pl.when(i == 0)
    def _():
        for k in range(W_SLOTS - 1):
            @pl.when(k < n_ord)
            def _():
                start_weights(k)

    @pl.when(i < n_used)
    def _():
        k = ord_ref[i]

        @pl.when(first_ref[i] > 0)
        def _():
            for cp in weight_copies(k):
                cp.wait()

            @pl.when(k + (W_SLOTS - 1) < n_ord)
            def _():
                start_weights(k + (W_SLOTS - 1))

        slot = k % W_SLOTS

        def swiglu(rows):
            x = _from_row_tiles(x_ref[0:rows])
            h1 = _dot(x, w1b[slot].astype(BF16))
            h3 = _dot(x, w3b[slot].astype(BF16))
            he = (h1 * _sigmoid(h1) * h3).astype(BF16)
            y_ref[0:rows] = _to_row_tiles(_dot(he, w2b[slot].astype(BF16)))

        @pl.when(half_ref[i] == 0)
        def _():
            swiglu(MOE_TM)

        @pl.when(half_ref[i] > 0)
        def _():
            swiglu(MOE_TM // 2)
            y_ref[MOE_TM // 2:] = jnp.zeros((MOE_TM // 2, ROW_SUB, ROW_LANE), BF16)

    @pl.when(i >= n_used)
    def _():
        y_ref[...] = jnp.zeros_like(y_ref)


def _experts(sched, xs, w_e1, w_e3, w_e2):
    grid_spec = pltpu.PrefetchScalarGridSpec(
        num_scalar_prefetch=6,
        grid=(MOE_NT,),
        in_specs=[
            pl.BlockSpec((MOE_TM, ROW_SUB, ROW_LANE), lambda i, nu, *_: (jnp.minimum(i, nu[0] - 1), 0, 0)),
            pl.BlockSpec(memory_space=pl.ANY),
            pl.BlockSpec(memory_space=pl.ANY),
            pl.BlockSpec(memory_space=pl.ANY),
        ],
        out_specs=pl.BlockSpec((MOE_TM, ROW_SUB, ROW_LANE), lambda i, *_: (i, 0, 0)),
        scratch_shapes=[
            pltpu.VMEM((W_SLOTS, D_MODEL, MOE_FF), F32),
            pltpu.VMEM((W_SLOTS, D_MODEL, MOE_FF), F32),
            pltpu.VMEM((W_SLOTS, MOE_FF, D_MODEL), F32),
            pltpu.SemaphoreType.DMA((3, W_SLOTS)),
        ],
    )
    return pl.pallas_call(
        _expert_kernel,
        grid_spec=grid_spec,
        out_shape=jax.ShapeDtypeStruct((MOE_ROWS, ROW_SUB, ROW_LANE), BF16),
        compiler_params=_params(1),
        name="experts",
    )(sched["n_used"], sched["tile_first"], sched["tile_ord"], sched["ord_expert"], sched["n_ord"],
      sched["tile_half"], xs, w_e1, w_e3, w_e2)


def _combine_kernel(pos_ref, ys_ref, x2_ref, route_ref, g_ref, yp_ref, ysm_ref, gbuf, sem):
    i = pl.program_id(0)

    def start_gather(tile, slot):
        def body(r, carry):
            for k in range(MOE_TOPK):
                src_row = pos_ref[k, tile * ROW_TILE + r]
                pltpu.make_async_copy(ys_ref.at[src_row], gbuf.at[slot, k, r], sem.at[slot, k]).start(priority=k)
            return carry
        lax.fori_loop(0, ROW_TILE, body, 0, unroll=8)

    @pl.when(i == 0)
    def _():
        start_gather(0, 0)

    slot = i % 2

    @pl.when(i + 1 < pl.num_programs(0))
    def _():
        start_gather(i + 1, 1 - slot)

    for k in range(MOE_TOPK):
        pltpu.make_async_copy(ys_ref.at[pl.ds(0, ROW_TILE)], gbuf.at[slot, k], sem.at[slot, k]).wait()

    route = route_ref[...]
    lane = lax.broadcasted_iota(jnp.int32, route.shape, 1)
    w0 = jnp.sum(jnp.where(lane == 4, route, 0.0), axis=-1, keepdims=True)
    w1 = jnp.sum(jnp.where(lane == 5, route, 0.0), axis=-1, keepdims=True)
    g0 = _from_row_tiles(gbuf[slot, 0]).astype(F32)
    g1 = _from_row_tiles(gbuf[slot, 1]).astype(F32)
    x3 = x2_ref[...] + (g0 * w0 + g1 * w1)
    y = _rms(x3, g_ref[...])

    @pl.when(i < NP_TILES)
    def _():
        yp_ref[...] = y

    @pl.when(i == NP_TILES)
    def _():
        ysm_ref[...] = y


def _combine(pos, ys, x2, route, g_f):
    grid_spec = pltpu.PrefetchScalarGridSpec(
        num_scalar_prefetch=1,
        grid=(N_TILES,),
        in_specs=[
            pl.BlockSpec(memory_space=pl.ANY),
            pl.BlockSpec((ROW_TILE, D_MODEL), lambda i, pos: (i, 0)),
            pl.BlockSpec((ROW_TILE, ROUTE_LANES), lambda i, pos: (i, 0)),
            pl.BlockSpec((1, D_MODEL), lambda i, pos: (0, 0)),
        ],
        out_specs=[
            pl.BlockSpec((ROW_TILE, D_MODEL), lambda i, pos: (jnp.minimum(i, NP_TILES - 1), 0)),
            pl.BlockSpec((ROW_TILE, D_MODEL), lambda i, pos: (0, 0)),
        ],
        scratch_shapes=[pltpu.VMEM((2, MOE_TOPK, ROW_TILE, ROW_SUB, ROW_LANE), BF16),
                        pltpu.SemaphoreType.DMA((2, MOE_TOPK))],
    )
    return pl.pallas_call(
        _combine_kernel,
        grid_spec=grid_spec,
        out_shape=[
            jax.ShapeDtypeStruct((N_P, D_MODEL), F32),
            jax.ShapeDtypeStruct((N_S, D_MODEL), F32),
        ],
        compiler_params=_params(1),
        name="combine_norm",
    )(pos, ys, x2, route, g_f)


def _expert_schedule(counts):
    i32 = jnp.int32
    cnt = counts[0, MOE_GROUPS:MOE_GROUPS + MOE_EXPERTS].astype(i32)
    tiles = (cnt + MOE_TM - 1) // MOE_TM
    tile_end = jnp.cumsum(tiles)
    tile_start = tile_end - tiles
    n_used = tile_end[-1]
    t = jnp.arange(MOE_NT, dtype=i32)
    tile_expert = jnp.minimum(jnp.sum((t[:, None] >= tile_end[None, :]).astype(i32), axis=1), MOE_EXPERTS - 1)
    used = tiles > 0
    ord_of = jnp.cumsum(used.astype(i32)) - 1
    experts = jnp.arange(MOE_EXPERTS, dtype=i32)
    ord_expert = jnp.zeros((MOE_EXPERTS,), i32).at[jnp.where(used, ord_of, MOE_EXPERTS)].set(experts, mode="drop")
    return {
        "n_used": n_used.reshape(1).astype(i32),
        "tile_first": ((t == tile_start[tile_expert]) & (t < n_used)).astype(i32),
        "tile_ord": ord_of[tile_expert].astype(i32),
        "tile_half": (cnt[tile_expert] - (t - tile_start[tile_expert]) * MOE_TM <= MOE_TM // 2).astype(i32),
        "ord_expert": ord_expert,
        "n_ord": jnp.sum(used.astype(i32)).reshape(1),
        "zero_row": (jnp.maximum(tile_end - 1, 0) * MOE_TM).astype(i32),
        "zero_on": used.astype(i32),
    }


def kernel(x_prompt, x_sample, mem_prompt, state_ret, cache_mem_k, cache_mem_v, norm_mix, w_in, ret_gn,
           sg_ln_g, sg_ln_b, sg_ws, sg_bs, w_a_out, w_b_out, w_o, norm_xa, w_cq, w_ck, w_cv, w_co, norm_moe,
           w_rg, b_rg, w_re, b_re, w_e1, w_e3, w_e2, norm_f):
    xp = x_prompt.reshape(N_P, D_MODEL)
    xs = x_sample.reshape(N_S, D_MODEL)

    h = _norm_rows(xp, xs, norm_mix)
    w = w_in[0]
    qk = _inproj("rope", IN_TM_ROPE, 0, 2 * RET_QK, h, w, _rope_tables(IN_TM_ROPE))
    v = _inproj("copy", IN_TM, 2 * RET_QK, RET_V, h, w)
    gs = _inproj("silu", IN_TM, 2 * RET_QK + RET_V, RET_V, h, w)
    uv = _inproj("gelu", IN_TM, 2 * RET_QK + 2 * RET_V, 2 * SG_WIDTH, h, w)
    gab = _inproj("sigmoid", IN_TM, 2 * RET_QK + 2 * RET_V + 2 * SG_WIDTH, 2 * D_MODEL, h, w)

    a_p, ret_p, a_s, ret_s = _retention(qk, v, gs, ret_gn, state_ret)

    b_all, sgv = _sgate(uv, sg_ln_g, sg_ln_b, sg_ws[0], sg_bs[0])

    x1, qx = _merge(a_p, a_s, b_all, gab, xp, xs, w_a_out[0].astype(BF16), w_b_out[0].astype(BF16),
                    w_o[0].astype(BF16), norm_xa, w_cq[0].astype(BF16))

    mk, mv, mkb, mvb = _memkv(mem_prompt.reshape(BATCH * MEM_LEN, D_MODEL), w_ck[0], w_cv[0])
    ck = cache_mem_k.reshape(DEC_BATCH, MEM_LEN * XA_HEADS, XA_DH)
    cv = cache_mem_v.reshape(DEC_BATCH, MEM_LEN * XA_HEADS, XA_DH)

    w_r = jnp.zeros((D_MODEL, ROUTE_LANES), F32)
    w_r = w_r.at[:, :MOE_GROUPS].set(w_rg[0]).at[:, MOE_GROUPS:MOE_GROUPS + MOE_EXPERTS].set(w_re[0])
    b_r = jnp.zeros((1, ROUTE_LANES), F32)
    b_r = b_r.at[0, :MOE_GROUPS].set(b_rg[0]).at[0, MOE_GROUPS:MOE_GROUPS + MOE_EXPERTS].set(b_re[0])
    x2, hmw, route, counts = _route(qx, mkb, mvb, ck, cv, x1, w_co[0].astype(BF16), norm_moe,
                                    w_r.astype(BF16), b_r)

    pos_t = _positions(route, counts)
    sched = _expert_schedule(counts)
    xs_sorted = _dispatch(pos_t, sched["zero_row"], sched["zero_on"], sched["n_used"], hmw)
    ys = _experts(sched, xs_sorted, w_e1[0], w_e3[0], w_e2[0])
    y_p, y_s = _combine(pos_t, ys, x2, route, norm_f.reshape(1, D_MODEL))

    return (y_p.reshape(BATCH, SEQ, D_MODEL),
            y_s.reshape(DEC_BATCH, DEC_SEQ, D_MODEL),
            ret_p,
            mk.reshape(1, BATCH, MEM_LEN, XA_HEADS, XA_DH),
            mv.reshape(1, BATCH, MEM_LEN, XA_HEADS, XA_DH),
            ret_s,
            sgv.reshape(1, DEC_BATCH, DEC_SEQ, SG_WIDTH))
```

```python
import functools

import jax
import jax.numpy as jnp
from jax import lax
from jax.experimental import pallas as pl
from jax.experimental.pallas import tpu as pltpu

F32 = jnp.float32
BF16 = jnp.bfloat16

D_MODEL = 2048
BATCH = 4
SEQ = 2048
DEC_BATCH = 128
DEC_SEQ = 4
PAST_LEN = 16384
RET_HEADS = 8
RET_DK = 128
RET_DV = 256
RET_CHUNK = 128
ROPE_BASE = 10000.0
RET_QK = RET_HEADS * RET_DK
RET_V = RET_HEADS * RET_DV
SG_GROUPS = 4
SG_WIDTH = 2048
SG_CHUNK = 128
MEM_LEN = 256
XA_HEADS = 4
XA_DH = 128
XA_W = XA_HEADS * XA_DH
MOE_GROUPS = 4
MOE_PER_GROUP = 8
MOE_EXPERTS = MOE_GROUPS * MOE_PER_GROUP
MOE_TOPK = 2
MOE_FF = 512
EPS = 1e-6
IN_WIDTH = 2 * RET_QK + 2 * RET_V + 2 * SG_WIDTH + 2 * D_MODEL

N_P = BATCH * SEQ
N_S = DEC_BATCH * DEC_SEQ
N_ALL = N_P + N_S
ROW_TILE = 512
N_TILES = N_ALL // ROW_TILE
NP_TILES = N_P // ROW_TILE
MERGE_TILE = 256
MERGE_P_TILES = N_P // MERGE_TILE

IN_TM = 1024
IN_TM_ROPE = 1024
IN_TN = 2048

PROMPT_CHUNK = 256
RET_BB = DEC_BATCH // (BATCH * (SEQ // PROMPT_CHUNK))
RET_ROWS = RET_BB * DEC_SEQ
LOG_DEC_SEQ = 2
LOG_SG_CHUNK = 7
SAMPLE_BB = DEC_BATCH // NP_TILES
SAMPLE_ROWS = SAMPLE_BB * DEC_SEQ

MOE_TM = 256
MOE_NT = (N_ALL * MOE_TOPK + MOE_EXPERTS * (MOE_TM - 1) + MOE_TM - 1) // MOE_TM
MOE_ROWS = MOE_NT * MOE_TM
ROUTE_LANES = 128
ROW_SUB, ROW_LANE = 16, 128
W_SLOTS = 3

VMEM_LIMIT = 56 * 1024 * 1024


def _params(n_axes, vmem=VMEM_LIMIT):
    return pltpu.CompilerParams(dimension_semantics=("arbitrary",) * n_axes,
                                vmem_limit_bytes=vmem)


def _rms(x, g):
    ms = jnp.mean(x * x, axis=-1, keepdims=True)
    return (x * lax.rsqrt(ms + EPS)) * g


def _dot(a, b):
    return jnp.dot(a, b, preferred_element_type=F32)


def _sigmoid(x):
    return 0.5 * jnp.tanh(0.5 * x) + 0.5


def _to_row_tiles(x):
    return x.astype(BF16).reshape(x.shape[0], ROW_SUB, ROW_LANE)


def _from_row_tiles(t):
    return t.reshape(t.shape[0], D_MODEL)


def _dot_nt(a, b):
    return lax.dot_general(a, b, (((1,), (1,)), ((), ())), preferred_element_type=F32)


def _dot_tn(a, b):
    return lax.dot_general(a, b, (((0,), (0,)), ((), ())), preferred_element_type=F32)


def _norm_kernel(xp_ref, xs_ref, g_ref, h_ref):
    i = pl.program_id(0)

    @pl.when(i < NP_TILES)
    def _():
        h_ref[...] = _rms(xp_ref[...], g_ref[...]).astype(BF16)

    @pl.when(i == NP_TILES)
    def _():
        h_ref[...] = _rms(xs_ref[...], g_ref[...]).astype(BF16)


def _norm_rows(xp, xs, g):
    return pl.pallas_call(
        _norm_kernel,
        grid=(N_TILES,),
        in_specs=[
            pl.BlockSpec((ROW_TILE, D_MODEL), lambda i: (jnp.minimum(i, NP_TILES - 1), 0)),
            pl.BlockSpec((ROW_TILE, D_MODEL), lambda i: (0, 0)),
            pl.BlockSpec((1, D_MODEL), lambda i: (0, 0)),
        ],
        out_specs=pl.BlockSpec((ROW_TILE, D_MODEL), lambda i: (i, 0)),
        out_shape=jax.ShapeDtypeStruct((N_ALL, D_MODEL), BF16),
        compiler_params=_params(1),
        name="norm_rows",
    )(xp, xs, g)


def _inproj_kernel(kind, tm, h_ref, w_ref, *rest):
    if kind == "rope":
        cos_ref, sin_ref, z_ref, wb_ref = rest
    else:
        z_ref, wb_ref = rest
    j = pl.program_id(0)
    i = pl.program_id(1)
    last = N_P // tm

    del wb_ref

    def tile(rows):
        acc = _dot(h_ref[0:rows, :], w_ref[...].astype(BF16))
        if kind == "rope":
            c = cos_ref[0:rows, :]
            s = sin_ref[0:rows, :]
            heads_per_block = IN_TN // RET_DK
            for hb in range(heads_per_block):
                scale = jnp.where(j * heads_per_block + hb >= RET_HEADS, RET_DK ** -0.5, 1.0).astype(F32)
                cols = slice(hb * RET_DK, (hb + 1) * RET_DK)
                a = acc[:, cols]
                r = pltpu.roll(a, RET_DK // 2, axis=1)
                z_ref[0:rows, cols] = ((a * c + r * s) * scale).astype(BF16)
        elif kind == "copy":
            z_ref[0:rows, :] = acc.astype(BF16)
        elif kind == "silu":
            z_ref[0:rows, :] = (acc * _sigmoid(acc)).astype(BF16)
        elif kind == "gelu":
            z_ref[0:rows, :] = jax.nn.gelu(acc).astype(BF16)
        else:
            z_ref[0:rows, :] = _sigmoid(acc).astype(BF16)

    @pl.when(i < last)
    def _():
        tile(tm)

    @pl.when(i == last)
    def _():
        tile(N_S)


def _inproj(kind, tm, col0, width, h, w_in, tables=()):
    last = N_P // tm
    tab_idx = lambda j, i: (jnp.where(i < last, i % (SEQ // tm), SEQ // tm), 0)
    j0 = col0 // IN_TN
    return pl.pallas_call(
        functools.partial(_inproj_kernel, kind, tm),
        grid=(width // IN_TN, last + 1),
        in_specs=[
            pl.BlockSpec((tm, D_MODEL), lambda j, i: (i, 0)),
            pl.BlockSpec((D_MODEL, IN_TN), lambda j, i: (0, j0 + j)),
        ] + [pl.BlockSpec((tm, RET_DK), tab_idx) for _ in tables],
        out_specs=pl.BlockSpec((tm, IN_TN), lambda j, i: (i, j)),
        out_shape=jax.ShapeDtypeStruct((N_ALL, width), BF16),
        scratch_shapes=[pltpu.VMEM((D_MODEL, IN_TN), BF16)],
        compiler_params=_params(2),
        name="in_proj_" + kind,
    )(h, w_in, *tables)


def _rope_tables(tm):
    half = RET_DK // 2
    inv = ROPE_BASE ** (-jnp.arange(half, dtype=F32) / half)

    def tab(pos):
        ang = pos.astype(F32)[:, None] * inv[None, :]
        c, s = jnp.cos(ang), jnp.sin(ang)
        return jnp.concatenate([c, c], -1), jnp.concatenate([-s, s], -1)

    cp, sp = tab(jnp.arange(SEQ, dtype=jnp.int32))
    cs, ss = tab(PAST_LEN + jnp.arange(DEC_SEQ, dtype=jnp.int32))
    cs = jnp.tile(cs, (DEC_BATCH, 1))
    ss = jnp.tile(ss, (DEC_BATCH, 1))
    pad = jnp.zeros((tm - N_S, RET_DK), F32)
    return jnp.concatenate([cp, cs, pad], 0), jnp.concatenate([sp, ss, pad], 0)


def _decay_tables(chunk):
    lg = jnp.log1p(-jnp.power(2.0, -5.0 - jnp.arange(RET_HEADS, dtype=F32)))
    idx = jnp.arange(chunk, dtype=F32)
    rel = idx[:, None] - idx[None, :]
    dmask = jnp.where(rel >= 0, jnp.exp(lg[:, None, None] * jnp.maximum(rel, 0.0)), 0.0).astype(F32)
    xi = jnp.exp(lg[:, None] * (idx[None, :] + 1.0)).astype(F32)
    zeta = jnp.exp(lg[:, None] * (chunk - 1.0 - idx[None, :])).astype(F32)
    gc = jnp.exp(lg * chunk).astype(F32)
    return dmask, xi, zeta, gc


def _head_norm_gate(o, gn, gate):
    mu = jnp.mean(o, axis=-1, keepdims=True)
    d = o - mu
    var = jnp.mean(d * d, axis=-1, keepdims=True)
    y = (d * lax.rsqrt(var + EPS)) * gn
    return (gate.astype(F32) * y).astype(BF16)


def _ret_kernel(gcp_ref, gcs_ref, q_ref, k_ref, v_ref, gs_ref, gn_ref, dm_ref, xi_ref, zt_ref,
                qs_ref, ks_ref, vs_ref, gss_ref, dms_ref, xis_ref, zts_ref, s0_ref,
                a_ref, sfin_ref, as_ref, s1_ref, s_ref):
    c = pl.program_id(1)

    @pl.when(c == 0)
    def _():
        s_ref[...] = jnp.zeros_like(s_ref)

    rows_k = lax.broadcasted_iota(jnp.int32, (RET_ROWS, RET_DK), 0) >> LOG_DEC_SEQ
    rows_v = lax.broadcasted_iota(jnp.int32, (RET_ROWS, RET_DV), 0) >> LOG_DEC_SEQ
    for h in range(RET_HEADS):
        kc = slice(h * RET_DK, (h + 1) * RET_DK)
        vc = slice(h * RET_DV, (h + 1) * RET_DV)
        gn = gn_ref[:, vc]

        qh = q_ref[:, kc]
        kh = k_ref[:, kc]
        vh = v_ref[:, vc]
        inner = _dot_nt(qh, kh) * dm_ref[h]
        o = _dot(inner.astype(BF16), vh)
        s_old = s_ref[h]
        xi = xi_ref[h]
        o = o + _dot(qh, s_old.astype(BF16)) * jnp.concatenate([xi, xi], axis=1)
        kz = (kh.astype(F32) * zt_ref[h]).astype(BF16)
        s_ref[h] = gcp_ref[h] * s_old + _dot_tn(kz, vh)
        a_ref[:, vc] = _head_norm_gate(o, gn, gs_ref[:, vc])

        qh = qs_ref[:, kc]
        kh = ks_ref[:, kc]
        vh = vs_ref[:, vc]
        inner = _dot_nt(qh, kh) * dms_ref[h]
        o = _dot(inner.astype(BF16), vh)
        xi = xis_ref[h]
        xi2 = jnp.concatenate([xi, xi], axis=1)
        kz = kh.astype(F32) * zts_ref[h]
        gch = gcs_ref[h]
        for b in range(RET_BB):
            s_old = s0_ref[0, b, h]
            cross = _dot(qh, s_old.astype(BF16)) * xi2
            o = o + jnp.where(rows_v == b, cross, 0.0)
            kz_b = jnp.where(rows_k == b, kz, 0.0).astype(BF16)
            s1_ref[0, b, h] = gch * s_old + _dot_tn(kz_b, vh)
        as_ref[:, vc] = _head_norm_gate(o, gn, gss_ref[:, vc])

    @pl.when(c == pl.num_programs(1) - 1)
    def _():
        sfin_ref[0, 0] = s_ref[...]


def _retention(qk, v, gs, ret_gn, state):
    chunk = PROMPT_CHUNK
    n_chunks = SEQ // chunk
    dmask, xi, zeta, gc = _decay_tables(chunk)
    xi_b = jnp.broadcast_to(xi[:, :, None], (RET_HEADS, chunk, RET_DK))
    zeta_b = jnp.broadcast_to(zeta[:, :, None], (RET_HEADS, chunk, RET_DK))
    dmask_s, xi_s, zeta_s, gc_s = _decay_tables(DEC_SEQ)
    eye = jnp.eye(RET_BB, dtype=F32)
    dm_big = jax.vmap(lambda m: jnp.kron(eye, m))(dmask_s)
    xi_sb = jnp.broadcast_to(jnp.tile(xi_s, (1, RET_BB))[:, :, None], (RET_HEADS, RET_ROWS, RET_DK))
    zeta_sb = jnp.broadcast_to(jnp.tile(zeta_s, (1, RET_BB))[:, :, None], (RET_HEADS, RET_ROWS, RET_DK))

    row = lambda b, c: b * n_chunks + c
    srow = lambda b, c: N_P // RET_ROWS + row(b, c)
    const3 = lambda b, c: (0, 0, 0)
    smem = pl.BlockSpec(memory_space=pltpu.SMEM)
    st_spec = pl.BlockSpec((1, RET_BB, RET_HEADS, RET_DK, RET_DV), lambda b, c: (0, row(b, c), 0, 0, 0))
    return pl.pallas_call(
        _ret_kernel,
        grid=(BATCH, n_chunks),
        in_specs=[
            smem, smem,
            pl.BlockSpec((chunk, RET_QK), lambda b, c: (row(b, c), 0)),
            pl.BlockSpec((chunk, RET_QK), lambda b, c: (row(b, c), 1)),
            pl.BlockSpec((chunk, RET_V), lambda b, c: (row(b, c), 0)),
            pl.BlockSpec((chunk, RET_V), lambda b, c: (row(b, c), 0)),
            pl.BlockSpec((1, RET_V), lambda b, c: (0, 0)),
            pl.BlockSpec((RET_HEADS, chunk, chunk), const3),
            pl.BlockSpec((RET_HEADS, chunk, RET_DK), const3),
            pl.BlockSpec((RET_HEADS, chunk, RET_DK), const3),
            pl.BlockSpec((RET_ROWS, RET_QK), lambda b, c: (srow(b, c), 0)),
            pl.BlockSpec((RET_ROWS, RET_QK), lambda b, c: (srow(b, c), 1)),
            pl.BlockSpec((RET_ROWS, RET_V), lambda b, c: (srow(b, c), 0)),
            pl.BlockSpec((RET_ROWS, RET_V), lambda b, c: (srow(b, c), 0)),
            pl.BlockSpec((RET_HEADS, RET_ROWS, RET_ROWS), const3),
            pl.BlockSpec((RET_HEADS, RET_ROWS, RET_DK), const3),
            pl.BlockSpec((RET_HEADS, RET_ROWS, RET_DK), const3),
            st_spec,
        ],
        out_specs=[
            pl.BlockSpec((chunk, RET_V), lambda b, c: (row(b, c), 0)),
            pl.BlockSpec((1, 1, RET_HEADS, RET_DK, RET_DV), lambda b, c: (0, b, 0, 0, 0)),
            pl.BlockSpec((RET_ROWS, RET_V), lambda b, c: (row(b, c), 0)),
            st_spec,
        ],
        out_shape=[
            jax.ShapeDtypeStruct((N_P, RET_V), BF16),
            jax.ShapeDtypeStruct((1, BATCH, RET_HEADS, RET_DK, RET_DV), F32),
            jax.ShapeDtypeStruct((N_S, RET_V), BF16),
            jax.ShapeDtypeStruct((1, DEC_BATCH, RET_HEADS, RET_DK, RET_DV), F32),
        ],
        scratch_shapes=[pltpu.VMEM((RET_HEADS, RET_DK, RET_DV), F32)],
        compiler_params=_params(2),
        name="retention",
    )(gc, gc_s, qk, qk, v, gs, ret_gn, dmask, xi_b, zeta_b, qk, qk, v, gs, dm_big, xi_sb, zeta_sb, state)


def _sgate_kernel(u_ref, v_ref, lg_ref, lb_ref, wp_ref, bp_ref, ws_ref, bs_ref, b_ref, sgv_ref, vln_ref):
    i = pl.program_id(0)
    v = v_ref[...].astype(F32)
    mu = jnp.mean(v, axis=-1, keepdims=True)
    d = v - mu
    var = jnp.mean(d * d, axis=-1, keepdims=True)
    vln_ref[...] = (d * lax.rsqrt(var + EPS)) * lg_ref[...] + lb_ref[...]
    gw = SG_WIDTH // SG_GROUPS
    lane_reps = gw // 128

    @pl.when(i < NP_TILES)
    def _():
        r = lax.broadcasted_iota(jnp.int32, (SG_CHUNK, SG_CHUNK), 0)
        c = lax.broadcasted_iota(jnp.int32, (SG_CHUNK, SG_CHUNK), 1)
        for g in range(SG_GROUPS):
            cols = slice(g * gw, (g + 1) * gw)
            w = jnp.where(c <= r, wp_ref[g], 0.0).astype(BF16)
            bias = jnp.concatenate([bp_ref[g]] * lane_reps, axis=1)
            chunks = [slice(ch * SG_CHUNK, (ch + 1) * SG_CHUNK) for ch in range(ROW_TILE // SG_CHUNK)]
            mixed = [_dot(w, vln_ref[rows, cols].astype(BF16)) for rows in chunks]
            for rows, m in zip(chunks, mixed):
                b_ref[rows, cols] = (u_ref[rows, cols].astype(F32) * (m + bias)).astype(BF16)

    @pl.when(i == NP_TILES)
    def _():
        sgv_ref[...] = vln_ref[...]
        r = lax.broadcasted_iota(jnp.int32, (ROW_TILE, ROW_TILE), 0)
        c = lax.broadcasted_iota(jnp.int32, (ROW_TILE, ROW_TILE), 1)
        keep = ((r >> LOG_DEC_SEQ) == (c >> LOG_DEC_SEQ)) & (c <= r)
        for g in range(SG_GROUPS):
            cols = slice(g * gw, (g + 1) * gw)
            w_rows = jnp.concatenate([ws_ref[g]] * (ROW_TILE // 8), axis=0)
            w_full = jnp.concatenate([w_rows] * (ROW_TILE // 128), axis=1)
            w = jnp.where(keep, w_full, 0.0).astype(BF16)
            b_rows = jnp.concatenate([bs_ref[g]] * (ROW_TILE // 8), axis=0)
            bias = jnp.concatenate([b_rows] * lane_reps, axis=1)
            mixed = _dot(w, vln_ref[:, cols].astype(BF16)) + bias
            b_ref[:, cols] = (u_ref[:, cols].astype(F32) * mixed).astype(BF16)


def _sgate(uv, ln_g, ln_b, sg_ws, sg_bs):
    b_p = jnp.broadcast_to(sg_bs[:, :, None], (SG_GROUPS, SG_CHUNK, 128))
    w_s = jnp.tile(sg_ws[:, :DEC_SEQ, :DEC_SEQ], (1, 8 // DEC_SEQ, 128 // DEC_SEQ))
    b_s = jnp.broadcast_to(jnp.tile(sg_bs[:, :DEC_SEQ], (1, 8 // DEC_SEQ))[:, :, None], (SG_GROUPS, 8, 128))
    const3 = lambda i: (0, 0, 0)
    return pl.pallas_call(
        _sgate_kernel,
        grid=(N_TILES,),
        in_specs=[
            pl.BlockSpec((ROW_TILE, SG_WIDTH), lambda i: (i, 0)),
            pl.BlockSpec((ROW_TILE, SG_WIDTH), lambda i: (i, 1)),
            pl.BlockSpec((1, SG_WIDTH), lambda i: (0, 0)),
            pl.BlockSpec((1, SG_WIDTH), lambda i: (0, 0)),
            pl.BlockSpec((SG_GROUPS, SG_CHUNK, SG_CHUNK), const3),
            pl.BlockSpec((SG_GROUPS, SG_CHUNK, 128), const3),
            pl.BlockSpec((SG_GROUPS, 8, 128), const3),
            pl.BlockSpec((SG_GROUPS, 8, 128), const3),
        ],
        out_specs=[
            pl.BlockSpec((ROW_TILE, SG_WIDTH), lambda i: (i, 0)),
            pl.BlockSpec((N_S, SG_WIDTH), lambda i: (0, 0)),
        ],
        out_shape=[
            jax.ShapeDtypeStruct((N_ALL, SG_WIDTH), BF16),
            jax.ShapeDtypeStruct((N_S, SG_WIDTH), F32),
        ],
        scratch_shapes=[pltpu.VMEM((ROW_TILE, SG_WIDTH), F32)],
        compiler_params=_params(1),
        name="spatial_gate",
    )(uv, uv, ln_g, ln_b, sg_ws, b_p, w_s, b_s)


def _merge_kernel(ap_ref, as_ref, b_ref, ga_ref, gb_ref, xp_ref, xs_ref, wa_ref, wb_ref, wo_ref,
                  g_ref, wq_ref, x1_ref, q_ref):
    i = pl.program_id(0)

    def run(a_in, x):
        a = _dot(a_in, wa_ref[...])
        b = _dot(b_ref[...], wb_ref[...])
        merged = ga_ref[...].astype(F32) * a + gb_ref[...].astype(F32) * b
        x1 = x + _dot(merged.astype(BF16), wo_ref[...])
        x1_ref[...] = x1
        q_ref[...] = _dot(_rms(x1, g_ref[...]).astype(BF16), wq_ref[...]).astype(BF16)

    @pl.when(i < MERGE_P_TILES)
    def _():
        run(ap_ref[...], xp_ref[...])

    @pl.when(i >= MERGE_P_TILES)
    def _():
        run(as_ref[...], xs_ref[...])


def _resident(shape):
    return pl.BlockSpec(shape, lambda i: (0,) * len(shape), pipeline_mode=pl.Buffered(1))


def _merge(a_p, a_s, b_all, gab, xp, xs, wa, wb, wo, g_xa, wq):
    tm = MERGE_TILE
    prompt_tile = lambda i: (jnp.minimum(i, MERGE_P_TILES - 1), 0)
    sample_tile = lambda i: (jnp.maximum(i - MERGE_P_TILES, 0), 0)
    return pl.pallas_call(
        _merge_kernel,
        grid=(N_ALL // tm,),
        in_specs=[
            pl.BlockSpec((tm, RET_V), prompt_tile),
            pl.BlockSpec((tm, RET_V), sample_tile),
            pl.BlockSpec((tm, SG_WIDTH), lambda i: (i, 0)),
            pl.BlockSpec((tm, D_MODEL), lambda i: (i, 0)),
            pl.BlockSpec((tm, D_MODEL), lambda i: (i, 1)),
            pl.BlockSpec((tm, D_MODEL), prompt_tile),
            pl.BlockSpec((tm, D_MODEL), sample_tile),
            _resident((RET_V, D_MODEL)),
            _resident((SG_WIDTH, D_MODEL)),
            _resident((D_MODEL, D_MODEL)),
            pl.BlockSpec((1, D_MODEL), lambda i: (0, 0)),
            _resident((D_MODEL, XA_W)),
        ],
        out_specs=[
            pl.BlockSpec((tm, D_MODEL), lambda i: (i, 0)),
            pl.BlockSpec((tm, XA_W), lambda i: (i, 0)),
        ],
        out_shape=[
            jax.ShapeDtypeStruct((N_ALL, D_MODEL), F32),
            jax.ShapeDtypeStruct((N_ALL, XA_W), BF16),
        ],
        compiler_params=_params(1),
        name="merge_proj",
    )(a_p, a_s, b_all, gab, gab, xp, xs, wa, wb, wo, g_xa, wq)


def _memkv_kernel(m_ref, wk_ref, wv_ref, k_ref, v_ref, kb_ref, vb_ref):
    m = m_ref[...].astype(BF16)
    k = _dot(m, wk_ref[...].astype(BF16))
    v = _dot(m, wv_ref[...].astype(BF16))
    k_ref[...] = k
    v_ref[...] = v
    kb_ref[...] = k.astype(BF16)
    vb_ref[...] = v.astype(BF16)


def _memkv(mem, w_ck, w_cv):
    rows = BATCH * MEM_LEN
    spec = pl.BlockSpec((MEM_LEN, XA_W), lambda b: (b, 0))
    wspec = pl.BlockSpec((D_MODEL, XA_W), lambda b: (0, 0))
    return pl.pallas_call(
        _memkv_kernel,
        grid=(BATCH,),
        in_specs=[pl.BlockSpec((MEM_LEN, D_MODEL), lambda b: (b, 0)), wspec, wspec],
        out_specs=[spec, spec, spec, spec],
        out_shape=[jax.ShapeDtypeStruct((rows, XA_W), F32)] * 2 + [jax.ShapeDtypeStruct((rows, XA_W), BF16)] * 2,
        compiler_params=_params(1),
        name="mem_kv",
    )(mem, w_ck, w_cv)


def _softmax_rows(s):
    m = jnp.max(s, axis=-1, keepdims=True)
    e = jnp.exp(s - m)
    return e / jnp.sum(e, axis=-1, keepdims=True)


def _xattn_prompt_rows(q_ref, k_ref, v_ref):
    cols = [slice(h * XA_DH, (h + 1) * XA_DH) for h in range(XA_HEADS)]
    scores = [_dot_nt(q_ref[:, c], k_ref[:, c]) * (XA_DH ** -0.5) for c in cols]
    probs = [_softmax_rows(s).astype(BF16) for s in scores]
    heads = [_dot(p, v_ref[:, c]).astype(BF16) for p, c in zip(probs, cols)]
    return jnp.concatenate(heads, axis=1)


def _xattn_sample_scores(q_ref, k_ref):
    qf = q_ref[...].astype(F32)
    scores = []
    for b in range(SAMPLE_BB):
        rows = slice(b * DEC_SEQ, (b + 1) * DEC_SEQ)
        qb = jnp.concatenate([qf[rows, h * XA_DH:(h + 1) * XA_DH] for h in range(XA_HEADS)], axis=0)
        scores.append(_dot_nt(qb.astype(BF16), k_ref[b].astype(BF16)) * (XA_DH ** -0.5))
    return scores


def _xattn_sample_finish(scores, v_ref, o_ref):
    n_q = XA_HEADS * DEC_SEQ
    n_kv = MEM_LEN * XA_HEADS
    r = lax.broadcasted_iota(jnp.int32, (n_q, n_kv), 0)
    c = lax.broadcasted_iota(jnp.int32, (n_q, n_kv), 1)
    head_ok = (r >> LOG_DEC_SEQ) == (c & (XA_HEADS - 1))
    probs = [_softmax_rows(jnp.where(head_ok, s, -1e30)).astype(BF16) for s in scores]
    for b in range(SAMPLE_BB):
        rows = slice(b * DEC_SEQ, (b + 1) * DEC_SEQ)
        o = _dot(probs[b], v_ref[b].astype(BF16))
        for h in range(XA_HEADS):
            o_ref[h, rows, :] = o[h * DEC_SEQ:(h + 1) * DEC_SEQ, :]


def _route_kernel(qp_ref, mk_ref, mv_ref, qs_ref, ck_ref, cv_ref, x1_ref, wo_ref, g_ref, wr_ref, br_ref,
                  x2_ref, hm_ref, route_ref, cnt_ref, run_ref, os_ref):
    i = pl.program_id(0)

    @pl.when(i == 0)
    def _():
        run_ref[...] = jnp.zeros_like(run_ref)

    def run(o, between=lambda: None):
        x2 = x1_ref[...] + _dot(o, wo_ref[...])
        x2_ref[...] = x2
        hm = _rms(x2, g_ref[...])
        hm_ref[...] = _to_row_tiles(hm)
        logits = _dot(hm.astype(BF16), wr_ref[...]) + br_ref[...]
        between()
        lane = lax.broadcasted_iota(jnp.int32, logits.shape, 1)
        lane_f = lane.astype(F32)
        neg = jnp.float32(-jnp.inf)
        big = jnp.float32(1 << 20)
        is_g = lane < MOE_GROUPS
        gl = jnp.where(is_g, logits, neg)
        gmax = jnp.max(gl, axis=-1, keepdims=True)
        g_sel = jnp.min(jnp.where(gl == gmax, lane_f, big), axis=-1, keepdims=True)
        g_w = 1.0 / jnp.sum(jnp.where(is_g, jnp.exp(logits - gmax), 0.0), axis=-1, keepdims=True)
        e_lane = lane - MOE_GROUPS
        e_group = (e_lane >> 3).astype(F32)
        in_grp = (e_lane >= 0) & (e_lane < MOE_EXPERTS) & (e_group == g_sel)
        el = jnp.where(in_grp, logits, neg)
        v0 = jnp.max(el, axis=-1, keepdims=True)
        i0 = jnp.min(jnp.where(el == v0, lane_f, big), axis=-1, keepdims=True)
        el1 = jnp.where(lane_f == i0, neg, el)
        v1 = jnp.max(el1, axis=-1, keepdims=True)
        i1 = jnp.min(jnp.where(el1 == v1, lane_f, big), axis=-1, keepdims=True)
        ex = jnp.exp(v1 - v0)
        den = 1.0 + ex
        w0 = (1.0 / den) * g_w
        w1 = (ex / den) * g_w
        a0 = (lane_f == i0).astype(F32)
        a1 = (lane_f == i1).astype(F32)
        a = a0 + a1
        rr = lax.broadcasted_iota(jnp.int32, (ROW_TILE, ROW_TILE), 0)
        cc = lax.broadcasted_iota(jnp.int32, (ROW_TILE, ROW_TILE), 1)
        lower = jnp.where(cc < rr, 1.0, 0.0).astype(BF16)
        before = _dot(lower, a.astype(BF16)) + run_ref[...]
        rank0 = jnp.sum(before * a0, axis=-1, keepdims=True)
        rank1 = jnp.sum(before * a1, axis=-1, keepdims=True)
        run_ref[...] += jnp.sum(a, axis=0, keepdims=True)
        e0 = i0 - MOE_GROUPS
        e1 = i1 - MOE_GROUPS
        route = jnp.where(lane == 0, e0, 0.0)
        route = jnp.where(lane == 1, e1, route)
        route = jnp.where(lane == 2, rank0, route)
        route = jnp.where(lane == 3, rank1, route)
        route = jnp.where(lane == 4, w0, route)
        route = jnp.where(lane == 5, w1, route)
        route_ref[...] = route
        cnt_ref[...] = run_ref[...]

    @pl.when(i < NP_TILES)
    def _():
        scores = _xattn_sample_scores(qs_ref, ck_ref)
        run(_xattn_prompt_rows(qp_ref, mk_ref, mv_ref),
            between=lambda: _xattn_sample_finish(scores, cv_ref, os_ref.at[i]))

    @pl.when(i == NP_TILES)
    def _():
        heads = [os_ref[:, h].reshape(N_S, XA_DH) for h in range(XA_HEADS)]
        run(jnp.concatenate(heads, axis=1).astype(BF16))


def _route(qx, mkb, mvb, ck, cv, x1, w_co, g_moe, w_r, b_r):
    prompt_step = lambda i: jnp.minimum(i, NP_TILES - 1)
    kv = pl.BlockSpec((SAMPLE_BB, MEM_LEN * XA_HEADS, XA_DH), lambda i: (prompt_step(i), 0, 0))
    mem_kv = pl.BlockSpec((MEM_LEN, XA_W), lambda i: (prompt_step(i) // (SEQ // ROW_TILE), 0))
    return pl.pallas_call(
        _route_kernel,
        grid=(N_TILES,),
        in_specs=[
            pl.BlockSpec((ROW_TILE, XA_W), lambda i: (prompt_step(i), 0)),
            mem_kv, mem_kv,
            pl.BlockSpec((SAMPLE_ROWS, XA_W), lambda i: (N_P // SAMPLE_ROWS + prompt_step(i), 0)),
            kv, kv,
            pl.BlockSpec((ROW_TILE, D_MODEL), lambda i: (i, 0)),
            _resident((XA_W, D_MODEL)),
            pl.BlockSpec((1, D_MODEL), lambda i: (0, 0)),
            _resident((D_MODEL, ROUTE_LANES)),
            pl.BlockSpec((1, ROUTE_LANES), lambda i: (0, 0)),
        ],
        out_specs=[
            pl.BlockSpec((ROW_TILE, D_MODEL), lambda i: (i, 0)),
            pl.BlockSpec((ROW_TILE, ROW_SUB, ROW_LANE), lambda i: (i, 0, 0)),
            pl.BlockSpec((ROW_TILE, ROUTE_LANES), lambda i: (i, 0)),
            pl.BlockSpec((1, ROUTE_LANES), lambda i: (0, 0)),
        ],
        out_shape=[
            jax.ShapeDtypeStruct((N_ALL, D_MODEL), F32),
            jax.ShapeDtypeStruct((N_ALL, ROW_SUB, ROW_LANE), BF16),
            jax.ShapeDtypeStruct((N_ALL, ROUTE_LANES), F32),
            jax.ShapeDtypeStruct((1, ROUTE_LANES), F32),
        ],
        scratch_shapes=[pltpu.VMEM((1, ROUTE_LANES), F32),
                        pltpu.VMEM((NP_TILES, XA_HEADS, SAMPLE_ROWS, XA_DH), F32)],
        compiler_params=_params(1),
        name="xa_out_route",
    )(qx, mkb, mvb, qx, ck, cv, x1, w_co, g_moe, w_r, b_r)


def _positions_kernel(route_ref, cnt_ref, pos_ref):
    route = route_ref[...]
    lane = lax.broadcasted_iota(jnp.int32, route.shape, 1)
    lane_f = lane.astype(F32)
    tiles = jnp.floor((cnt_ref[...] + (MOE_TM - 1)) * (1.0 / MOE_TM))
    lr = lax.broadcasted_iota(jnp.int32, (ROUTE_LANES, ROUTE_LANES), 0)
    lc = lax.broadcasted_iota(jnp.int32, (ROUTE_LANES, ROUTE_LANES), 1)
    before = jnp.where(lr < lc, 1.0, 0.0).astype(BF16)
    tiles8 = jnp.broadcast_to(tiles, (8, ROUTE_LANES)).astype(BF16)
    start = _dot(tiles8, before)[0:1, :] * MOE_TM

    def col(k):
        return jnp.sum(jnp.where(lane == k, route, 0.0), axis=-1, keepdims=True)

    def first_row(e):
        return jnp.sum(jnp.where(lane_f == e + MOE_GROUPS, start, 0.0), axis=-1, keepdims=True)

    p0 = first_row(col(0)) + col(2)
    p1 = first_row(col(1)) + col(3)
    p = jnp.where(lane == 0, p0, jnp.where(lane == 1, p1, 0.0))
    pos_ref[...] = p.T[0:8, :].astype(jnp.int32)


def _positions(route, counts):
    rows = N_ALL // 4
    return pl.pallas_call(
        _positions_kernel,
        grid=(4,),
        in_specs=[
            pl.BlockSpec((rows, ROUTE_LANES), lambda i: (i, 0)),
            pl.BlockSpec((1, ROUTE_LANES), lambda i: (0, 0)),
        ],
        out_specs=pl.BlockSpec((8, rows), lambda i: (0, i)),
        out_shape=jax.ShapeDtypeStruct((8, N_ALL), jnp.int32),
        compiler_params=_params(1),
        name="positions",
    )(route, counts)


def _dispatch_kernel(pos_ref, zrow_ref, zon_ref, nu_ref, hm_ref, xs_ref, zbuf, sem, zsem, tbuf, tsem):
    i = pl.program_id(0)

    def zero_tile(row):
        return pltpu.make_async_copy(zbuf, xs_ref.at[pl.ds(pl.multiple_of(row, MOE_TM), MOE_TM)], zsem)

    @pl.when(i == 0)
    def _():
        zbuf[...] = jnp.zeros_like(zbuf)
        for e in range(MOE_EXPERTS):
            @pl.when(zon_ref[e] > 0)
            def _():
                zero_tile(zrow_ref[e]).start()

        def start_tail(t, carry):
            zero_tile(t * MOE_TM).start()
            return carry

        def wait_tail(t, carry):
            zero_tile(t * MOE_TM).wait()
            return carry

        lax.fori_loop(nu_ref[0], MOE_NT, start_tail, 0)
        for e in range(MOE_EXPERTS):
            @pl.when(zon_ref[e] > 0)
            def _():
                zero_tile(zrow_ref[e]).wait()
        lax.fori_loop(nu_ref[0], MOE_NT, wait_tail, 0)

    n_steps = pl.num_programs(0)

    def tile_copy(t):
        return pltpu.make_async_copy(hm_ref.at[pl.ds(t * ROW_TILE, ROW_TILE)], tbuf.at[t % 3], tsem.at[t % 3])

    def wait_rows(t):
        for k in range(MOE_TOPK):
            pltpu.make_async_copy(tbuf.at[t % 3], xs_ref.at[pl.ds(0, ROW_TILE)], sem.at[t % 2, k]).wait()

    @pl.when(i == 0)
    def _():
        tile_copy(0).start()

    tile_copy(i).wait()

    @pl.when(i + 1 < n_steps)
    def _():
        tile_copy(i + 1).start()

    slot = i % 3
    par = i % 2
    base = i * ROW_TILE

    def start(r, carry):
        for k in range(MOE_TOPK):
            dst_row = pos_ref[k, base + r]
            pltpu.make_async_copy(tbuf.at[slot, r], xs_ref.at[dst_row], sem.at[par, k]).start(priority=k)
        return carry

    lax.fori_loop(0, ROW_TILE, start, 0, unroll=8)

    @pl.when(i > 0)
    def _():
        wait_rows(i - 1)

    @pl.when(i == n_steps - 1)
    def _():
        wait_rows(i)


def _dispatch(pos_t, zero_row, zero_on, n_used, hmw):
    grid_spec = pltpu.PrefetchScalarGridSpec(
        num_scalar_prefetch=4,
        grid=(N_TILES,),
        in_specs=[pl.BlockSpec(memory_space=pl.ANY)],
        out_specs=pl.BlockSpec(memory_space=pl.ANY),
        scratch_shapes=[pltpu.VMEM((MOE_TM, ROW_SUB, ROW_LANE), BF16),
                        pltpu.SemaphoreType.DMA((2, MOE_TOPK)), pltpu.SemaphoreType.DMA(()),
                        pltpu.VMEM((3, ROW_TILE, ROW_SUB, ROW_LANE), BF16), pltpu.SemaphoreType.DMA((3,))],
    )
    return pl.pallas_call(
        _dispatch_kernel,
        grid_spec=grid_spec,
        out_shape=jax.ShapeDtypeStruct((MOE_ROWS, ROW_SUB, ROW_LANE), BF16),
        compiler_params=_params(1),
        name="dispatch",
    )(pos_t, zero_row, zero_on, n_used, hmw)


def _expert_kernel(nu_ref, first_ref, ord_ref, oe_ref, no_ref, half_ref, x_ref, w1_hbm, w3_hbm, w2_hbm, y_ref,
                   w1b, w3b, w2b, sem):
    i = pl.program_id(0)
    n_used = nu_ref[0]
    n_ord = no_ref[0]

    def weight_copies(k):
        e = oe_ref[k]
        slot = k % W_SLOTS
        return (pltpu.make_async_copy(w1_hbm.at[e], w1b.at[slot], sem.at[0, slot]),
                pltpu.make_async_copy(w3_hbm.at[e], w3b.at[slot], sem.at[1, slot]),
                pltpu.make_async_copy(w2_hbm.at[e], w2b.at[slot], sem.at[2, slot]))

    def start_weights(k):
        for cp in weight_copies(k):
            cp.start(priority=1)

    @pl.when(i == 0)
    def _():
        for k in range(W_SLOTS - 1):
            @pl.when(k < n_ord)
            def _():
                start_weights(k)

    @pl.when(i < n_used)
    def _():
        k = ord_ref[i]

        @pl.when(first_ref[i] > 0)
        def _():
            for cp in weight_copies(k):
                cp.wait()

            @pl.when(k + (W_SLOTS - 1) < n_ord)
            def _():
                start_weights(k + (W_SLOTS - 1))

        slot = k % W_SLOTS

        def swiglu(rows):
            x = _from_row_tiles(x_ref[0:rows])
            h1 = _dot(x, w1b[slot].astype(BF16))
            h3 = _dot(x, w3b[slot].astype(BF16))
            he = (h1 * _sigmoid(h1) * h3).astype(BF16)
            y_ref[0:rows] = _to_row_tiles(_dot(he, w2b[slot].astype(BF16)))

        @pl.when(half_ref[i] == 0)
        def _():
            swiglu(MOE_TM)

        @pl.when(half_ref[i] > 0)
        def _():
            swiglu(MOE_TM // 2)
            y_ref[MOE_TM // 2:] = jnp.zeros((MOE_TM // 2, ROW_SUB, ROW_LANE), BF16)

    @pl.when(i >= n_used)
    def _():
        y_ref[...] = jnp.zeros_like(y_ref)


def _experts(sched, xs, w_e1, w_e3, w_e2):
    grid_spec = pltpu.PrefetchScalarGridSpec(
        num_scalar_prefetch=6,
        grid=(MOE_NT,),
        in_specs=[
            pl.BlockSpec((MOE_TM, ROW_SUB, ROW_LANE), lambda i, nu, *_: (jnp.minimum(i, nu[0] - 1), 0, 0)),
            pl.BlockSpec(memory_space=pl.ANY),
            pl.BlockSpec(memory_space=pl.ANY),
            pl.BlockSpec(memory_space=pl.ANY),
        ],
        out_specs=pl.BlockSpec((MOE_TM, ROW_SUB, ROW_LANE), lambda i, *_: (i, 0, 0)),
        scratch_shapes=[
            pltpu.VMEM((W_SLOTS, D_MODEL, MOE_FF), F32),
            pltpu.VMEM((W_SLOTS, D_MODEL, MOE_FF), F32),
            pltpu.VMEM((W_SLOTS, MOE_FF, D_MODEL), F32),
            pltpu.SemaphoreType.DMA((3, W_SLOTS)),
        ],
    )
    return pl.pallas_call(
        _expert_kernel,
        grid_spec=grid_spec,
        out_shape=jax.ShapeDtypeStruct((MOE_ROWS, ROW_SUB, ROW_LANE), BF16),
        compiler_params=_params(1),
        name="experts",
    )(sched["n_used"], sched["tile_first"], sched["tile_ord"], sched["ord_expert"], sched["n_ord"],
      sched["tile_half"], xs, w_e1, w_e3, w_e2)


def _combine_kernel(pos_ref, ys_ref, x2_ref, route_ref, g_ref, yp_ref, ysm_ref, gbuf, sem):
    i = pl.program_id(0)

    def start_gather(tile, slot):
        def body(r, carry):
            for k in range(MOE_TOPK):
                src_row = pos_ref[k, tile * ROW_TILE + r]
                pltpu.make_async_copy(ys_ref.at[src_row], gbuf.at[slot, k, r], sem.at[slot, k]).start(priority=k)
            return carry
        lax.fori_loop(0, ROW_TILE, body, 0, unroll=8)

    @pl.when(i == 0)
    def _():
        start_gather(0, 0)

    slot = i % 2

    @pl.when(i + 1 < pl.num_programs(0))
    def _():
        start_gather(i + 1, 1 - slot)

    for k in range(MOE_TOPK):
        pltpu.make_async_copy(ys_ref.at[pl.ds(0, ROW_TILE)], gbuf.at[slot, k], sem.at[slot, k]).wait()

    route = route_ref[...]
    lane = lax.broadcasted_iota(jnp.int32, route.shape, 1)
    w0 = jnp.sum(jnp.where(lane == 4, route, 0.0), axis=-1, keepdims=True)
    w1 = jnp.sum(jnp.where(lane == 5, route, 0.0), axis=-1, keepdims=True)
    g0 = _from_row_tiles(gbuf[slot, 0]).astype(F32)
    g1 = _from_row_tiles(gbuf[slot, 1]).astype(F32)
    x3 = x2_ref[...] + (g0 * w0 + g1 * w1)
    y = _rms(x3, g_ref[...])

    @pl.when(i < NP_TILES)
    def _():
        yp_ref[...] = y

    @pl.when(i == NP_TILES)
    def _():
        ysm_ref[...] = y


def _combine(pos, ys, x2, route, g_f):
    grid_spec = pltpu.PrefetchScalarGridSpec(
        num_scalar_prefetch=1,
        grid=(N_TILES,),
        in_specs=[
            pl.BlockSpec(memory_space=pl.ANY),
            pl.BlockSpec((ROW_TILE, D_MODEL), lambda i, pos: (i, 0)),
            pl.BlockSpec((ROW_TILE, ROUTE_LANES), lambda i, pos: (i, 0)),
            pl.BlockSpec((1, D_MODEL), lambda i, pos: (0, 0)),
        ],
        out_specs=[
            pl.BlockSpec((ROW_TILE, D_MODEL), lambda i, pos: (jnp.minimum(i, NP_TILES - 1), 0)),
            pl.BlockSpec((ROW_TILE, D_MODEL), lambda i, pos: (0, 0)),
        ],
        scratch_shapes=[pltpu.VMEM((2, MOE_TOPK, ROW_TILE, ROW_SUB, ROW_LANE), BF16),
                        pltpu.SemaphoreType.DMA((2, MOE_TOPK))],
    )
    return pl.pallas_call(
        _combine_kernel,
        grid_spec=grid_spec,
        out_shape=[
            jax.ShapeDtypeStruct((N_P, D_MODEL), F32),
            jax.ShapeDtypeStruct((N_S, D_MODEL), F32),
        ],
        compiler_params=_params(1),
        name="combine_norm",
    )(pos, ys, x2, route, g_f)


def _expert_schedule(counts):
    i32 = jnp.int32
    cnt = counts[0, MOE_GROUPS:MOE_GROUPS + MOE_EXPERTS].astype(i32)
    tiles = (cnt + MOE_TM - 1) // MOE_TM
    tile_end = jnp.cumsum(tiles)
    tile_start = tile_end - tiles
    n_used = tile_end[-1]
    t = jnp.arange(MOE_NT, dtype=i32)
    tile_expert = jnp.minimum(jnp.sum((t[:, None] >= tile_end[None, :]).astype(i32), axis=1), MOE_EXPERTS - 1)
    used = tiles > 0
    ord_of = jnp.cumsum(used.astype(i32)) - 1
    experts = jnp.arange(MOE_EXPERTS, dtype=i32)
    ord_expert = jnp.zeros((MOE_EXPERTS,), i32).at[jnp.where(used, ord_of, MOE_EXPERTS)].set(experts, mode="drop")
    return {
        "n_used": n_used.reshape(1).astype(i32),
        "tile_first": ((t == tile_start[tile_expert]) & (t < n_used)).astype(i32),
        "tile_ord": ord_of[tile_expert].astype(i32),
        "tile_half": (cnt[tile_expert] - (t - tile_start[tile_expert]) * MOE_TM <= MOE_TM // 2).astype(i32),
        "ord_expert": ord_expert,
        "n_ord": jnp.sum(used.astype(i32)).reshape(1),
        "zero_row": (jnp.maximum(tile_end - 1, 0) * MOE_TM).astype(i32),
        "zero_on": used.astype(i32),
    }


def kernel(x_prompt, x_sample, mem_prompt, state_ret, cache_mem_k, cache_mem_v, norm_mix, w_in, ret_gn,
           sg_ln_g, sg_ln_b, sg_ws, sg_bs, w_a_out, w_b_out, w_o, norm_xa, w_cq, w_ck, w_cv, w_co, norm_moe,
           w_rg, b_rg, w_re, b_re, w_e1, w_e3, w_e2, norm_f):
    xp = x_prompt.reshape(N_P, D_MODEL)
    xs = x_sample.reshape(N_S, D_MODEL)

    h = _norm_rows(xp, xs, norm_mix)
    w = w_in[0]
    qk = _inproj("rope", IN_TM_ROPE, 0, 2 * RET_QK, h, w, _rope_tables(IN_TM_ROPE))
    v = _inproj("copy", IN_TM, 2 * RET_QK, RET_V, h, w)
    gs = _inproj("silu", IN_TM, 2 * RET_QK + RET_V, RET_V, h, w)
    uv = _inproj("gelu", IN_TM, 2 * RET_QK + 2 * RET_V, 2 * SG_WIDTH, h, w)
    gab = _inproj("sigmoid", IN_TM, 2 * RET_QK + 2 * RET_V + 2 * SG_WIDTH, 2 * D_MODEL, h, w)

    a_p, ret_p, a_s, ret_s = _retention(qk, v, gs, ret_gn, state_ret)

    b_all, sgv = _sgate(uv, sg_ln_g, sg_ln_b, sg_ws[0], sg_bs[0])

    x1, qx = _merge(a_p, a_s, b_all, gab, xp, xs, w_a_out[0].astype(BF16), w_b_out[0].astype(BF16),
                    w_o[0].astype(BF16), norm_xa, w_cq[0].astype(BF16))

    mk, mv, mkb, mvb = _memkv(mem_prompt.reshape(BATCH * MEM_LEN, D_MODEL), w_ck[0], w_cv[0])
    ck = cache_mem_k.reshape(DEC_BATCH, MEM_LEN * XA_HEADS, XA_DH)
    cv = cache_mem_v.reshape(DEC_BATCH, MEM_LEN * XA_HEADS, XA_DH)

    w_r = jnp.zeros((D_MODEL, ROUTE_LANES), F32)
    w_r = w_r.at[:, :MOE_GROUPS].set(w_rg[0]).at[:, MOE_GROUPS:MOE_GROUPS + MOE_EXPERTS].set(w_re[0])
    b_r = jnp.zeros((1, ROUTE_LANES), F32)
    b_r = b_r.at[0, :MOE_GROUPS].set(b_rg[0]).at[0, MOE_GROUPS:MOE_GROUPS + MOE_EXPERTS].set(b_re[0])
    x2, hmw, route, counts = _route(qx, mkb, mvb, ck, cv, x1, w_co[0].astype(BF16), norm_moe,
                                    w_r.astype(BF16), b_r)

    pos_t = _positions(route, counts)
    sched = _expert_schedule(counts)
    xs_sorted = _dispatch(pos_t, sched["zero_row"], sched["zero_on"], sched["n_used"], hmw)
    ys = _experts(sched, xs_sorted, w_e1[0], w_e3[0], w_e2[0])
    y_p, y_s = _combine(pos_t, ys, x2, route, norm_f.reshape(1, D_MODEL))

    return (y_p.reshape(BATCH, SEQ, D_MODEL),
            y_s.reshape(DEC_BATCH, DEC_SEQ, D_MODEL),
            ret_p,
            mk.reshape(1, BATCH, MEM_LEN, XA_HEADS, XA_DH),
            mv.reshape(1, BATCH, MEM_LEN, XA_HEADS, XA_DH),
            ret_s,
            sgv.reshape(1, DEC_BATCH, DEC_SEQ, SG_WIDTH))
```

```python
import functools

import jax
import jax.numpy as jnp
from jax import lax
from jax.experimental import pallas as pl
from jax.experimental.pallas import tpu as pltpu

F32 = jnp.float32
BF16 = jnp.bfloat16

D_MODEL = 2048
BATCH = 4
SEQ = 2048
DEC_BATCH = 128
DEC_SEQ = 4
PAST_LEN = 16384
RET_HEADS = 8
RET_DK = 128
RET_DV = 256
RET_CHUNK = 128
ROPE_BASE = 10000.0
RET_QK = RET_HEADS * RET_DK
RET_V = RET_HEADS * RET_DV
SG_GROUPS = 4
SG_WIDTH = 2048
SG_CHUNK = 128
MEM_LEN = 256
XA_HEADS = 4
XA_DH = 128
XA_W = XA_HEADS * XA_DH
MOE_GROUPS = 4
MOE_PER_GROUP = 8
MOE_EXPERTS = MOE_GROUPS * MOE_PER_GROUP
MOE_TOPK = 2
MOE_FF = 512
EPS = 1e-6
IN_WIDTH = 2 * RET_QK + 2 * RET_V + 2 * SG_WIDTH + 2 * D_MODEL

N_P = BATCH * SEQ
N_S = DEC_BATCH * DEC_SEQ
N_ALL = N_P + N_S
ROW_TILE = 512
N_TILES = N_ALL // ROW_TILE
NP_TILES = N_P // ROW_TILE
MERGE_TILE = 256
MERGE_P_TILES = N_P // MERGE_TILE

IN_TM = 1024
IN_TM_ROPE = 1024
IN_TN = 2048

PROMPT_CHUNK = 256
RET_BB = DEC_BATCH // (BATCH * (SEQ // PROMPT_CHUNK))
RET_ROWS = RET_BB * DEC_SEQ
LOG_DEC_SEQ = 2
LOG_SG_CHUNK = 7
SAMPLE_BB = DEC_BATCH // NP_TILES
SAMPLE_ROWS = SAMPLE_BB * DEC_SEQ

MOE_TM = 256
MOE_NT = (N_ALL * MOE_TOPK + MOE_EXPERTS * (MOE_TM - 1) + MOE_TM - 1) // MOE_TM
MOE_ROWS = MOE_NT * MOE_TM
ROUTE_LANES = 128
ROW_SUB, ROW_LANE = 16, 128
W_SLOTS = 3

VMEM_LIMIT = 56 * 1024 * 1024


def _params(n_axes, vmem=VMEM_LIMIT):
    return pltpu.CompilerParams(dimension_semantics=("arbitrary",) * n_axes,
                                vmem_limit_bytes=vmem)


def _rms(x, g):
    ms = jnp.mean(x * x, axis=-1, keepdims=True)
    return (x * lax.rsqrt(ms + EPS)) * g


def _dot(a, b):
    return jnp.dot(a, b, preferred_element_type=F32)


def _sigmoid(x):
    return 0.5 * jnp.tanh(0.5 * x) + 0.5


def _to_row_tiles(x):
    return x.astype(BF16).reshape(x.shape[0], ROW_SUB, ROW_LANE)


def _from_row_tiles(t):
    return t.reshape(t.shape[0], D_MODEL)


def _dot_nt(a, b):
    return lax.dot_general(a, b, (((1,), (1,)), ((), ())), preferred_element_type=F32)


def _dot_tn(a, b):
    return lax.dot_general(a, b, (((0,), (0,)), ((), ())), preferred_element_type=F32)


def _norm_kernel(xp_ref, xs_ref, g_ref, h_ref):
    i = pl.program_id(0)

    @pl.when(i < NP_TILES)
    def _():
        h_ref[...] = _rms(xp_ref[...], g_ref[...]).astype(BF16)

    @pl.when(i == NP_TILES)
    def _():
        h_ref[...] = _rms(xs_ref[...], g_ref[...]).astype(BF16)


def _norm_rows(xp, xs, g):
    return pl.pallas_call(
        _norm_kernel,
        grid=(N_TILES,),
        in_specs=[
            pl.BlockSpec((ROW_TILE, D_MODEL), lambda i: (jnp.minimum(i, NP_TILES - 1), 0)),
            pl.BlockSpec((ROW_TILE, D_MODEL), lambda i: (0, 0)),
            pl.BlockSpec((1, D_MODEL), lambda i: (0, 0)),
        ],
        out_specs=pl.BlockSpec((ROW_TILE, D_MODEL), lambda i: (i, 0)),
        out_shape=jax.ShapeDtypeStruct((N_ALL, D_MODEL), BF16),
        compiler_params=_params(1),
        name="norm_rows",
    )(xp, xs, g)


def _inproj_kernel(kind, tm, h_ref, w_ref, *rest):
    if kind == "rope":
        cos_ref, sin_ref, z_ref, wb_ref = rest
    else:
        z_ref, wb_ref = rest
    j = pl.program_id(0)
    i = pl.program_id(1)
    last = N_P // tm

    del wb_ref

    def tile(rows):
        acc = _dot(h_ref[0:rows, :], w_ref[...].astype(BF16))
        if kind == "rope":
            c = cos_ref[0:rows, :]
            s = sin_ref[0:rows, :]
            heads_per_block = IN_TN // RET_DK
            for hb in range(heads_per_block):
                scale = jnp.where(j * heads_per_block + hb >= RET_HEADS, RET_DK ** -0.5, 1.0).astype(F32)
                cols = slice(hb * RET_DK, (hb + 1) * RET_DK)
                a = acc[:, cols]
                r = pltpu.roll(a, RET_DK // 2, axis=1)
                z_ref[0:rows, cols] = ((a * c + r * s) * scale).astype(BF16)
        elif kind == "copy":
            z_ref[0:rows, :] = acc.astype(BF16)
        elif kind == "silu":
            z_ref[0:rows, :] = (acc * _sigmoid(acc)).astype(BF16)
        elif kind == "gelu":
            z_ref[0:rows, :] = jax.nn.gelu(acc).astype(BF16)
        else:
            z_ref[0:rows, :] = _sigmoid(acc).astype(BF16)

    @pl.when(i < last)
    def _():
        tile(tm)

    @pl.when(i == last)
    def _():
        tile(N_S)


def _inproj(kind, tm, col0, width, h, w_in, tables=()):
    last = N_P // tm
    tab_idx = lambda j, i: (jnp.where(i < last, i % (SEQ // tm), SEQ // tm), 0)
    j0 = col0 // IN_TN
    return pl.pallas_call(
        functools.partial(_inproj_kernel, kind, tm),
        grid=(width // IN_TN, last + 1),
        in_specs=[
            pl.BlockSpec((tm, D_MODEL), lambda j, i: (i, 0)),
            pl.BlockSpec((D_MODEL, IN_TN), lambda j, i: (0, j0 + j)),
        ] + [pl.BlockSpec((tm, RET_DK), tab_idx) for _ in tables],
        out_specs=pl.BlockSpec((tm, IN_TN), lambda j, i: (i, j)),
        out_shape=jax.ShapeDtypeStruct((N_ALL, width), BF16),
        scratch_shapes=[pltpu.VMEM((D_MODEL, IN_TN), BF16)],
        compiler_params=_params(2),
        name="in_proj_" + kind,
    )(h, w_in, *tables)


def _rope_tables(tm):
    half = RET_DK // 2
    inv = ROPE_BASE ** (-jnp.arange(half, dtype=F32) / half)

    def tab(pos):
        ang = pos.astype(F32)[:, None] * inv[None, :]
        c, s = jnp.cos(ang), jnp.sin(ang)
        return jnp.concatenate([c, c], -1), jnp.concatenate([-s, s], -1)

    cp, sp = tab(jnp.arange(SEQ, dtype=jnp.int32))
    cs, ss = tab(PAST_LEN + jnp.arange(DEC_SEQ, dtype=jnp.int32))
    cs = jnp.tile(cs, (DEC_BATCH, 1))
    ss = jnp.tile(ss, (DEC_BATCH, 1))
    pad = jnp.zeros((tm - N_S, RET_DK), F32)
    return jnp.concatenate([cp, cs, pad], 0), jnp.concatenate([sp, ss, pad], 0)


def _decay_tables(chunk):
    lg = jnp.log1p(-jnp.power(2.0, -5.0 - jnp.arange(RET_HEADS, dtype=F32)))
    idx = jnp.arange(chunk, dtype=F32)
    rel = idx[:, None] - idx[None, :]
    dmask = jnp.where(rel >= 0, jnp.exp(lg[:, None, None] * jnp.maximum(rel, 0.0)), 0.0).astype(F32)
    xi = jnp.exp(lg[:, None] * (idx[None, :] + 1.0)).astype(F32)
    zeta = jnp.exp(lg[:, None] * (chunk - 1.0 - idx[None, :])).astype(F32)
    gc = jnp.exp(lg * chunk).astype(F32)
    return dmask, xi, zeta, gc


def _head_norm(o, gn):
    mu = jnp.mean(o, axis=-1, keepdims=True)
    d = o - mu
    var = jnp.mean(d * d, axis=-1, keepdims=True)
    return ((d * lax.rsqrt(var + EPS)) * gn).astype(BF16)


def _ret_kernel(gcp_ref, gcs_ref, q_ref, k_ref, v_ref, gn_ref, dm_ref, xi_ref, zt_ref,
                qs_ref, ks_ref, vs_ref, dms_ref, xis_ref, zts_ref, s0_ref,
                a_ref, sfin_ref, as_ref, s1_ref, s_ref):
    c = pl.program_id(1)

    @pl.when(c == 0)
    def _():
        s_ref[...] = jnp.zeros_like(s_ref)

    rows_k = lax.broadcasted_iota(jnp.int32, (RET_ROWS, RET_DK), 0) >> LOG_DEC_SEQ
    rows_v = lax.broadcasted_iota(jnp.int32, (RET_ROWS, RET_DV), 0) >> LOG_DEC_SEQ
    for h in range(RET_HEADS):
        kc = slice(h * RET_DK, (h + 1) * RET_DK)
        vc = slice(h * RET_DV, (h + 1) * RET_DV)
        gn = gn_ref[:, vc]

        qh = q_ref[:, kc]
        kh = k_ref[:, kc]
        vh = v_ref[:, vc]
        inner = _dot_nt(qh, kh) * dm_ref[h]
        o = _dot(inner.astype(BF16), vh)
        s_old = s_ref[h]
        xi = xi_ref[h]
        o = o + _dot(qh, s_old.astype(BF16)) * jnp.concatenate([xi, xi], axis=1)
        kz = (kh.astype(F32) * zt_ref[h]).astype(BF16)
        s_ref[h] = gcp_ref[h] * s_old + _dot_tn(kz, vh)
        a_ref[:, vc] = _head_norm(o, gn)

        qh = qs_ref[:, kc]
        kh = ks_ref[:, kc]
        vh = vs_ref[:, vc]
        inner = _dot_nt(qh, kh) * dms_ref[h]
        o = _dot(inner.astype(BF16), vh)
        xi = xis_ref[h]
        xi2 = jnp.concatenate([xi, xi], axis=1)
        kz = kh.astype(F32) * zts_ref[h]
        gch = gcs_ref[h]
        for b in range(RET_BB):
            s_old = s0_ref[0, b, h]
            cross = _dot(qh, s_old.astype(BF16)) * xi2
            o = o + jnp.where(rows_v == b, cross, 0.0)
            kz_b = jnp.where(rows_k == b, kz, 0.0).astype(BF16)
            s1_ref[0, b, h] = gch * s_old + _dot_tn(kz_b, vh)
        as_ref[:, vc] = _head_norm(o, gn)

    @pl.when(c == pl.num_programs(1) - 1)
    def _():
        sfin_ref[0, 0] = s_ref[...]


def _retention(qk, v, ret_gn, state):
    chunk = PROMPT_CHUNK
    n_chunks = SEQ // chunk
    dmask, xi, zeta, gc = _decay_tables(chunk)
    xi_b = jnp.broadcast_to(xi[:, :, None], (RET_HEADS, chunk, RET_DK))
    zeta_b = jnp.broadcast_to(zeta[:, :, None], (RET_HEADS, chunk, RET_DK))
    dmask_s, xi_s, zeta_s, gc_s = _decay_tables(DEC_SEQ)
    eye = jnp.eye(RET_BB, dtype=F32)
    dm_big = jax.vmap(lambda m: jnp.kron(eye, m))(dmask_s)
    xi_sb = jnp.broadcast_to(jnp.tile(xi_s, (1, RET_BB))[:, :, None], (RET_HEADS, RET_ROWS, RET_DK))
    zeta_sb = jnp.broadcast_to(jnp.tile(zeta_s, (1, RET_BB))[:, :, None], (RET_HEADS, RET_ROWS, RET_DK))

    row = lambda b, c: b * n_chunks + c
    srow = lambda b, c: N_P // RET_ROWS + row(b, c)
    const3 = lambda b, c: (0, 0, 0)
    smem = pl.BlockSpec(memory_space=pltpu.SMEM)
    st_spec = pl.BlockSpec((1, RET_BB, RET_HEADS, RET_DK, RET_DV), lambda b, c: (0, row(b, c), 0, 0, 0))
    return pl.pallas_call(
        _ret_kernel,
        grid=(BATCH, n_chunks),
        in_specs=[
            smem, smem,
            pl.BlockSpec((chunk, RET_QK), lambda b, c: (row(b, c), 0)),
            pl.BlockSpec((chunk, RET_QK), lambda b, c: (row(b, c), 1)),
            pl.BlockSpec((chunk, RET_V), lambda b, c: (row(b, c), 0)),
            pl.BlockSpec((1, RET_V), lambda b, c: (0, 0)),
            pl.BlockSpec((RET_HEADS, chunk, chunk), const3),
            pl.BlockSpec((RET_HEADS, chunk, RET_DK), const3),
            pl.BlockSpec((RET_HEADS, chunk, RET_DK), const3),
            pl.BlockSpec((RET_ROWS, RET_QK), lambda b, c: (srow(b, c), 0)),
            pl.BlockSpec((RET_ROWS, RET_QK), lambda b, c: (srow(b, c), 1)),
            pl.BlockSpec((RET_ROWS, RET_V), lambda b, c: (srow(b, c), 0)),
            pl.BlockSpec((RET_HEADS, RET_ROWS, RET_ROWS), const3),
            pl.BlockSpec((RET_HEADS, RET_ROWS, RET_DK), const3),
            pl.BlockSpec((RET_HEADS, RET_ROWS, RET_DK), const3),
            st_spec,
        ],
        out_specs=[
            pl.BlockSpec((chunk, RET_V), lambda b, c: (row(b, c), 0)),
            pl.BlockSpec((1, 1, RET_HEADS, RET_DK, RET_DV), lambda b, c: (0, b, 0, 0, 0)),
            pl.BlockSpec((RET_ROWS, RET_V), lambda b, c: (row(b, c), 0)),
            st_spec,
        ],
        out_shape=[
            jax.ShapeDtypeStruct((N_P, RET_V), BF16),
            jax.ShapeDtypeStruct((1, BATCH, RET_HEADS, RET_DK, RET_DV), F32),
            jax.ShapeDtypeStruct((N_S, RET_V), BF16),
            jax.ShapeDtypeStruct((1, DEC_BATCH, RET_HEADS, RET_DK, RET_DV), F32),
        ],
        scratch_shapes=[pltpu.VMEM((RET_HEADS, RET_DK, RET_DV), F32)],
        compiler_params=_params(2),
        name="retention",
    )(gc, gc_s, qk, qk, v, ret_gn, dmask, xi_b, zeta_b, qk, qk, v, dm_big, xi_sb, zeta_sb, state)


def _sgate_kernel(u_ref, v_ref, lg_ref, lb_ref, wp_ref, bp_ref, ws_ref, bs_ref, b_ref, sgv_ref, vln_ref):
    i = pl.program_id(0)
    v = v_ref[...].astype(F32)
    mu = jnp.mean(v, axis=-1, keepdims=True)
    d = v - mu
    var = jnp.mean(d * d, axis=-1, keepdims=True)
    vln_ref[...] = (d * lax.rsqrt(var + EPS)) * lg_ref[...] + lb_ref[...]
    gw = SG_WIDTH // SG_GROUPS
    lane_reps = gw // 128

    @pl.when(i < NP_TILES)
    def _():
        r = lax.broadcasted_iota(jnp.int32, (SG_CHUNK, SG_CHUNK), 0)
        c = lax.broadcasted_iota(jnp.int32, (SG_CHUNK, SG_CHUNK), 1)
        for g in range(SG_GROUPS):
            cols = slice(g * gw, (g + 1) * gw)
            w = jnp.where(c <= r, wp_ref[g], 0.0).astype(BF16)
            bias = jnp.concatenate([bp_ref[g]] * lane_reps, axis=1)
            chunks = [slice(ch * SG_CHUNK, (ch + 1) * SG_CHUNK) for ch in range(ROW_TILE // SG_CHUNK)]
            mixed = [_dot(w, vln_ref[rows, cols].astype(BF16)) for rows in chunks]
            for rows, m in zip(chunks, mixed):
                b_ref[rows, cols] = (u_ref[rows, cols].astype(F32) * (m + bias)).astype(BF16)

    @pl.when(i == NP_TILES)
    def _():
        sgv_ref[...] = vln_ref[...]
        r = lax.broadcasted_iota(jnp.int32, (ROW_TILE, ROW_TILE), 0)
        c = lax.broadcasted_iota(jnp.int32, (ROW_TILE, ROW_TILE), 1)
        keep = ((r >> LOG_DEC_SEQ) == (c >> LOG_DEC_SEQ)) & (c <= r)
        for g in range(SG_GROUPS):
            cols = slice(g * gw, (g + 1) * gw)
            w_rows = jnp.concatenate([ws_ref[g]] * (ROW_TILE // 8), axis=0)
            w_full = jnp.concatenate([w_rows] * (ROW_TILE // 128), axis=1)
            w = jnp.where(keep, w_full, 0.0).astype(BF16)
            b_rows = jnp.concatenate([bs_ref[g]] * (ROW_TILE // 8), axis=0)
            bias = jnp.concatenate([b_rows] * lane_reps, axis=1)
            mixed = _dot(w, vln_ref[:, cols].astype(BF16)) + bias
            b_ref[:, cols] = (u_ref[:, cols].astype(F32) * mixed).astype(BF16)


def _sgate(uv, ln_g, ln_b, sg_ws, sg_bs):
    b_p = jnp.broadcast_to(sg_bs[:, :, None], (SG_GROUPS, SG_CHUNK, 128))
    w_s = jnp.tile(sg_ws[:, :DEC_SEQ, :DEC_SEQ], (1, 8 // DEC_SEQ, 128 // DEC_SEQ))
    b_s = jnp.broadcast_to(jnp.tile(sg_bs[:, :DEC_SEQ], (1, 8 // DEC_SEQ))[:, :, None], (SG_GROUPS, 8, 128))
    const3 = lambda i: (0, 0, 0)
    return pl.pallas_call(
        _sgate_kernel,
        grid=(N_TILES,),
        in_specs=[
            pl.BlockSpec((ROW_TILE, SG_WIDTH), lambda i: (i, 0)),
            pl.BlockSpec((ROW_TILE, SG_WIDTH), lambda i: (i, 1)),
            pl.BlockSpec((1, SG_WIDTH), lambda i: (0, 0)),
            pl.BlockSpec((1, SG_WIDTH), lambda i: (0, 0)),
            pl.BlockSpec((SG_GROUPS, SG_CHUNK, SG_CHUNK), const3),
            pl.BlockSpec((SG_GROUPS, SG_CHUNK, 128), const3),
            pl.BlockSpec((SG_GROUPS, 8, 128), const3),
            pl.BlockSpec((SG_GROUPS, 8, 128), const3),
        ],
        out_specs=[
            pl.BlockSpec((ROW_TILE, SG_WIDTH), lambda i: (i, 0)),
            pl.BlockSpec((N_S, SG_WIDTH), lambda i: (0, 0)),
        ],
        out_shape=[
            jax.ShapeDtypeStruct((N_ALL, SG_WIDTH), BF16),
            jax.ShapeDtypeStruct((N_S, SG_WIDTH), F32),
        ],
        scratch_shapes=[pltpu.VMEM((ROW_TILE, SG_WIDTH), F32)],
        compiler_params=_params(1),
        name="spatial_gate",
    )(uv, uv, ln_g, ln_b, sg_ws, b_p, w_s, b_s)


def _merge_kernel(ap_ref, as_ref, gs_ref, b_ref, ga_ref, gb_ref, xp_ref, xs_ref, wa_ref, wb_ref, wo_ref,
                  g_ref, wq_ref, x1_ref, q_ref):
    i = pl.program_id(0)

    def run(a_norm, x):
        b = _dot(b_ref[...], wb_ref[...])
        a_in = (gs_ref[...].astype(F32) * a_norm.astype(F32)).astype(BF16)
        a = _dot(a_in, wa_ref[...])
        merged = ga_ref[...].astype(F32) * a + gb_ref[...].astype(F32) * b
        x1 = x + _dot(merged.astype(BF16), wo_ref[...])
        x1_ref[...] = x1
        q_ref[...] = _dot(_rms(x1, g_ref[...]).astype(BF16), wq_ref[...]).astype(BF16)

    @pl.when(i < MERGE_P_TILES)
    def _():
        run(ap_ref[...], xp_ref[...])

    @pl.when(i >= MERGE_P_TILES)
    def _():
        run(as_ref[...], xs_ref[...])


def _resident(shape):
    return pl.BlockSpec(shape, lambda i: (0,) * len(shape), pipeline_mode=pl.Buffered(1))


def _merge(a_p, a_s, gs, b_all, gab, xp, xs, wa, wb, wo, g_xa, wq):
    tm = MERGE_TILE
    prompt_tile = lambda i: (jnp.minimum(i, MERGE_P_TILES - 1), 0)
    sample_tile = lambda i: (jnp.maximum(i - MERGE_P_TILES, 0), 0)
    return pl.pallas_call(
        _merge_kernel,
        grid=(N_ALL // tm,),
        in_specs=[
            pl.BlockSpec((tm, RET_V), prompt_tile),
            pl.BlockSpec((tm, RET_V), sample_tile),
            pl.BlockSpec((tm, RET_V), lambda i: (i, 0)),
            pl.BlockSpec((tm, SG_WIDTH), lambda i: (i, 0)),
            pl.BlockSpec((tm, D_MODEL), lambda i: (i, 0)),
            pl.BlockSpec((tm, D_MODEL), lambda i: (i, 1)),
            pl.BlockSpec((tm, D_MODEL), prompt_tile),
            pl.BlockSpec((tm, D_MODEL), sample_tile),
            _resident((RET_V, D_MODEL)),
            _resident((SG_WIDTH, D_MODEL)),
            _resident((D_MODEL, D_MODEL)),
            pl.BlockSpec((1, D_MODEL), lambda i: (0, 0)),
            _resident((D_MODEL, XA_W)),
        ],
        out_specs=[
            pl.BlockSpec((tm, D_MODEL), lambda i: (i, 0)),
            pl.BlockSpec((tm, XA_W), lambda i: (i, 0)),
        ],
        out_shape=[
            jax.ShapeDtypeStruct((N_ALL, D_MODEL), F32),
            jax.ShapeDtypeStruct((N_ALL, XA_W), BF16),
        ],
        compiler_params=_params(1),
        name="merge_proj",
    )(a_p, a_s, gs, b_all, gab, gab, xp, xs, wa, wb, wo, g_xa, wq)


def _memkv_kernel(m_ref, wk_ref, wv_ref, k_ref, v_ref, kb_ref, vb_ref):
    m = m_ref[...].astype(BF16)
    k = _dot(m, wk_ref[...].astype(BF16))
    v = _dot(m, wv_ref[...].astype(BF16))
    k_ref[...] = k
    v_ref[...] = v
    kb_ref[...] = k.astype(BF16)
    vb_ref[...] = v.astype(BF16)


def _memkv(mem, w_ck, w_cv):
    rows = BATCH * MEM_LEN
    spec = pl.BlockSpec((MEM_LEN, XA_W), lambda b: (b, 0))
    wspec = pl.BlockSpec((D_MODEL, XA_W), lambda b: (0, 0))
    return pl.pallas_call(
        _memkv_kernel,
        grid=(BATCH,),
        in_specs=[pl.BlockSpec((MEM_LEN, D_MODEL), lambda b: (b, 0)), wspec, wspec],
        out_specs=[spec, spec, spec, spec],
        out_shape=[jax.ShapeDtypeStruct((rows, XA_W), F32)] * 2 + [jax.ShapeDtypeStruct((rows, XA_W), BF16)] * 2,
        compiler_params=_params(1),
        name="mem_kv",
    )(mem, w_ck, w_cv)


def _softmax_rows(s):
    m = jnp.max(s, axis=-1, keepdims=True)
    e = jnp.exp(s - m)
    return e / jnp.sum(e, axis=-1, keepdims=True)


def _xattn_prompt_rows(q_ref, k_ref, v_ref, after_scores=lambda: None):
    cols = [slice(h * XA_DH, (h + 1) * XA_DH) for h in range(XA_HEADS)]
    scores = [_dot_nt(q_ref[:, c], k_ref[:, c]) * (XA_DH ** -0.5) for c in cols]
    after_scores()
    probs = [_softmax_rows(s).astype(BF16) for s in scores]
    heads = [_dot(p, v_ref[:, c]).astype(BF16) for p, c in zip(probs, cols)]
    return jnp.concatenate(heads, axis=1)


def _xattn_sample_scores(q_ref, k_ref):
    qf = q_ref[...].astype(F32)
    scores = []
    for b in range(SAMPLE_BB):
        rows = slice(b * DEC_SEQ, (b + 1) * DEC_SEQ)
        qb = jnp.concatenate([qf[rows, h * XA_DH:(h + 1) * XA_DH] for h in range(XA_HEADS)], axis=0)
        scores.append(_dot_nt(qb.astype(BF16), k_ref[b].astype(BF16)) * (XA_DH ** -0.5))
    return scores


def _xattn_sample_finish(scores, v_ref, o_ref):
    n_q = XA_HEADS * DEC_SEQ
    n_kv = MEM_LEN * XA_HEADS
    r = lax.broadcasted_iota(jnp.int32, (n_q, n_kv), 0)
    c = lax.broadcasted_iota(jnp.int32, (n_q, n_kv), 1)
    head_ok = (r >> LOG_DEC_SEQ) == (c & (XA_HEADS - 1))
    probs = [_softmax_rows(jnp.where(head_ok, s, -1e30)).astype(BF16) for s in scores]
    for b in range(SAMPLE_BB):
        rows = slice(b * DEC_SEQ, (b + 1) * DEC_SEQ)
        o = _dot(probs[b], v_ref[b].astype(BF16))
        for h in range(XA_HEADS):
            o_ref[h, rows, :] = o[h * DEC_SEQ:(h + 1) * DEC_SEQ, :]


def _route_kernel(qp_ref, mk_ref, mv_ref, qs_ref, ck_ref, cv_ref, x1_ref, wo_ref, g_ref, wr_ref, br_ref,
                  x2_ref, hm_ref, route_ref, cnt_ref, run_ref, os_ref):
    i = pl.program_id(0)

    @pl.when(i == 0)
    def _():
        run_ref[...] = jnp.zeros_like(run_ref)

    def run(o, between=lambda: None):
        x2 = x1_ref[...] + _dot(o, wo_ref[...])
        x2_ref[...] = x2
        hm = _rms(x2, g_ref[...])
        hm_ref[...] = _to_row_tiles(hm)
        logits = _dot(hm.astype(BF16), wr_ref[...]) + br_ref[...]
        lane = lax.broadcasted_iota(jnp.int32, logits.shape, 1)
        lane_f = lane.astype(F32)
        neg = jnp.float32(-jnp.inf)
        big = jnp.float32(1 << 20)
        is_g = lane < MOE_GROUPS
        gl = jnp.where(is_g, logits, neg)
        gmax = jnp.max(gl, axis=-1, keepdims=True)
        g_sel = jnp.min(jnp.where(gl == gmax, lane_f, big), axis=-1, keepdims=True)
        g_w = 1.0 / jnp.sum(jnp.where(is_g, jnp.exp(logits - gmax), 0.0), axis=-1, keepdims=True)
        between()
        e_lane = lane - MOE_GROUPS
        e_group = (e_lane >> 3).astype(F32)
        in_grp = (e_lane >= 0) & (e_lane < MOE_EXPERTS) & (e_group == g_sel)
        el = jnp.where(in_grp, logits, neg)
        v0 = jnp.max(el, axis=-1, keepdims=True)
        i0 = jnp.min(jnp.where(el == v0, lane_f, big), axis=-1, keepdims=True)
        el1 = jnp.where(lane_f == i0, neg, el)
        v1 = jnp.max(el1, axis=-1, keepdims=True)
        i1 = jnp.min(jnp.where(el1 == v1, lane_f, big), axis=-1, keepdims=True)
        ex = jnp.exp(v1 - v0)
        den = 1.0 + ex
        w0 = (1.0 / den) * g_w
        w1 = (ex / den) * g_w
        a0 = (lane_f == i0).astype(F32)
        a1 = (lane_f == i1).astype(F32)
        a = a0 + a1
        rr = lax.broadcasted_iota(jnp.int32, (ROW_TILE, ROW_TILE), 0)
        cc = lax.broadcasted_iota(jnp.int32, (ROW_TILE, ROW_TILE), 1)
        lower = jnp.where(cc < rr, 1.0, 0.0).astype(BF16)
        before = _dot(lower, a.astype(BF16)) + run_ref[...]
        rank0 = jnp.sum(before * a0, axis=-1, keepdims=True)
        rank1 = jnp.sum(before * a1, axis=-1, keepdims=True)
        run_ref[...] += jnp.sum(a, axis=0, keepdims=True)
        e0 = i0 - MOE_GROUPS
        e1 = i1 - MOE_GROUPS
        route = jnp.where(lane == 0, e0, 0.0)
        route = jnp.where(lane == 1, e1, route)
        route = jnp.where(lane == 2, rank0, route)
        route = jnp.where(lane == 3, rank1, route)
        route = jnp.where(lane == 4, w0, route)
        route = jnp.where(lane == 5, w1, route)
        route_ref[...] = route
        cnt_ref[...] = run_ref[...]

    @pl.when(i < NP_TILES)
    def _():
        scores = []
        o_prompt = _xattn_prompt_rows(qp_ref, mk_ref, mv_ref,
                                      after_scores=lambda: scores.extend(_xattn_sample_scores(qs_ref, ck_ref)))
        run(o_prompt, between=lambda: _xattn_sample_finish(scores, cv_ref, os_ref.at[i]))

    @pl.when(i == NP_TILES)
    def _():
        heads = [os_ref[:, h].reshape(N_S, XA_DH) for h in range(XA_HEADS)]
        run(jnp.concatenate(heads, axis=1).astype(BF16))


def _route(qx, mkb, mvb, ck, cv, x1, w_co, g_moe, w_r, b_r):
    prompt_step = lambda i: jnp.minimum(i, NP_TILES - 1)
    kv = pl.BlockSpec((SAMPLE_BB, MEM_LEN * XA_HEADS, XA_DH), lambda i: (prompt_step(i), 0, 0))
    mem_kv = pl.BlockSpec((MEM_LEN, XA_W), lambda i: (prompt_step(i) // (SEQ // ROW_TILE), 0))
    return pl.pallas_call(
        _route_kernel,
        grid=(N_TILES,),
        in_specs=[
            pl.BlockSpec((ROW_TILE, XA_W), lambda i: (prompt_step(i), 0)),
            mem_kv, mem_kv,
            pl.BlockSpec((SAMPLE_ROWS, XA_W), lambda i: (N_P // SAMPLE_ROWS + prompt_step(i), 0)),
            kv, kv,
            pl.BlockSpec((ROW_TILE, D_MODEL), lambda i: (i, 0)),
            _resident((XA_W, D_MODEL)),
            pl.BlockSpec((1, D_MODEL), lambda i: (0, 0)),
            _resident((D_MODEL, ROUTE_LANES)),
            pl.BlockSpec((1, ROUTE_LANES), lambda i: (0, 0)),
        ],
        out_specs=[
            pl.BlockSpec((ROW_TILE, D_MODEL), lambda i: (i, 0)),
            pl.BlockSpec((ROW_TILE, ROW_SUB, ROW_LANE), lambda i: (i, 0, 0)),
            pl.BlockSpec((ROW_TILE, ROUTE_LANES), lambda i: (i, 0)),
            pl.BlockSpec((1, ROUTE_LANES), lambda i: (0, 0)),
        ],
        out_shape=[
            jax.ShapeDtypeStruct((N_ALL, D_MODEL), F32),
            jax.ShapeDtypeStruct((N_ALL, ROW_SUB, ROW_LANE), BF16),
            jax.ShapeDtypeStruct((N_ALL, ROUTE_LANES), F32),
            jax.ShapeDtypeStruct((1, ROUTE_LANES), F32),
        ],
        scratch_shapes=[pltpu.VMEM((1, ROUTE_LANES), F32),
                        pltpu.VMEM((NP_TILES, XA_HEADS, SAMPLE_ROWS, XA_DH), F32)],
        compiler_params=_params(1),
        name="xa_out_route",
    )(qx, mkb, mvb, qx, ck, cv, x1, w_co, g_moe, w_r, b_r)


def _positions_kernel(route_ref, cnt_ref, pos_ref):
    route = route_ref[...]
    lane = lax.broadcasted_iota(jnp.int32, route.shape, 1)
    lane_f = lane.astype(F32)
    tiles = jnp.floor((cnt_ref[...] + (MOE_TM - 1)) * (1.0 / MOE_TM))
    lr = lax.broadcasted_iota(jnp.int32, (ROUTE_LANES, ROUTE_LANES), 0)
    lc = lax.broadcasted_iota(jnp.int32, (ROUTE_LANES, ROUTE_LANES), 1)
    before = jnp.where(lr < lc, 1.0, 0.0).astype(BF16)
    tiles8 = jnp.broadcast_to(tiles, (8, ROUTE_LANES)).astype(BF16)
    start = _dot(tiles8, before)[0:1, :] * MOE_TM

    def col(k):
        return jnp.sum(jnp.where(lane == k, route, 0.0), axis=-1, keepdims=True)

    def first_row(e):
        return jnp.sum(jnp.where(lane_f == e + MOE_GROUPS, start, 0.0), axis=-1, keepdims=True)

    p0 = first_row(col(0)) + col(2)
    p1 = first_row(col(1)) + col(3)
    p = jnp.where(lane == 0, p0, jnp.where(lane == 1, p1, 0.0))
    pos_ref[...] = p.T[0:8, :].astype(jnp.int32)


def _positions(route, counts):
    rows = N_ALL // 4
    return pl.pallas_call(
        _positions_kernel,
        grid=(4,),
        in_specs=[
            pl.BlockSpec((rows, ROUTE_LANES), lambda i: (i, 0)),
            pl.BlockSpec((1, ROUTE_LANES), lambda i: (0, 0)),
        ],
        out_specs=pl.BlockSpec((8, rows), lambda i: (0, i)),
        out_shape=jax.ShapeDtypeStruct((8, N_ALL), jnp.int32),
        compiler_params=_params(1),
        name="positions",
    )(route, counts)


def _dispatch_kernel(pos_ref, zrow_ref, zon_ref, nu_ref, hm_ref, xs_ref, zbuf, sem, zsem, tbuf, tsem):
    i = pl.program_id(0)

    def zero_tile(row):
        return pltpu.make_async_copy(zbuf, xs_ref.at[pl.ds(pl.multiple_of(row, MOE_TM), MOE_TM)], zsem)

    @pl.when(i == 0)
    def _():
        zbuf[...] = jnp.zeros_like(zbuf)
        for e in range(MOE_EXPERTS):
            @pl.when(zon_ref[e] > 0)
            def _():
                zero_tile(zrow_ref[e]).start()

        def start_tail(t, carry):
            zero_tile(t * MOE_TM).start()
            return carry

        def wait_tail(t, carry):
            zero_tile(t * MOE_TM).wait()
            return carry

        lax.fori_loop(nu_ref[0], MOE_NT, start_tail, 0)
        for e in range(MOE_EXPERTS):
            @pl.when(zon_ref[e] > 0)
            def _():
                zero_tile(zrow_ref[e]).wait()
        lax.fori_loop(nu_ref[0], MOE_NT, wait_tail, 0)

    n_steps = pl.num_programs(0)

    def tile_copy(t):
        return pltpu.make_async_copy(hm_ref.at[pl.ds(t * ROW_TILE, ROW_TILE)], tbuf.at[t % 3], tsem.at[t % 3])

    def wait_rows(t):
        for k in range(MOE_TOPK):
            pltpu.make_async_copy(tbuf.at[t % 3], xs_ref.at[pl.ds(0, ROW_TILE)], sem.at[t % 2, k]).wait()

    @pl.when(i == 0)
    def _():
        tile_copy(0).start()

    tile_copy(i).wait()

    @pl.when(i + 1 < n_steps)
    def _():
        tile_copy(i + 1).start()

    slot = i % 3
    par = i % 2
    base = i * ROW_TILE

    def start(r, carry):
        for k in range(MOE_TOPK):
            dst_row = pos_ref[k, base + r]
            pltpu.make_async_copy(tbuf.at[slot, r], xs_ref.at[dst_row], sem.at[par, k]).start(priority=k)
        return carry

    lax.fori_loop(0, ROW_TILE, start, 0, unroll=8)

    @pl.when(i > 0)
    def _():
        wait_rows(i - 1)

    @pl.when(i == n_steps - 1)
    def _():
        wait_rows(i)


def _dispatch(pos_t, zero_row, zero_on, n_used, hmw):
    grid_spec = pltpu.PrefetchScalarGridSpec(
        num_scalar_prefetch=4,
        grid=(N_TILES,),
        in_specs=[pl.BlockSpec(memory_space=pl.ANY)],
        out_specs=pl.BlockSpec(memory_space=pl.ANY),
        scratch_shapes=[pltpu.VMEM((MOE_TM, ROW_SUB, ROW_LANE), BF16),
                        pltpu.SemaphoreType.DMA((2, MOE_TOPK)), pltpu.SemaphoreType.DMA(()),
                        pltpu.VMEM((3, ROW_TILE, ROW_SUB, ROW_LANE), BF16), pltpu.SemaphoreType.DMA((3,))],
    )
    return pl.pallas_call(
        _dispatch_kernel,
        grid_spec=grid_spec,
        out_shape=jax.ShapeDtypeStruct((MOE_ROWS, ROW_SUB, ROW_LANE), BF16),
        compiler_params=_params(1),
        name="dispatch",
    )(pos_t, zero_row, zero_on, n_used, hmw)


def _expert_kernel(nu_ref, first_ref, ord_ref, oe_ref, no_ref, half_ref, x_ref, w1_hbm, w3_hbm, w2_hbm, y_ref,
                   w1b, w3b, w2b, sem):
    i = pl.program_id(0)
    n_used = nu_ref[0]
    n_ord = no_ref[0]

    def weight_copies(k):
        e = oe_ref[k]
        slot = k % W_SLOTS
        return (pltpu.make_async_copy(w1_hbm.at[e], w1b.at[slot], sem.at[0, slot]),
                pltpu.make_async_copy(w3_hbm.at[e], w3b.at[slot], sem.at[1, slot]),
                pltpu.make_async_copy(w2_hbm.at[e], w2b.at[slot], sem.at[2, slot]))

    def start_weights(k):
        for cp in weight_copies(k):
            cp.start(priority=1)

    @pl.when(i == 0)
    def _():
        for k in range(W_SLOTS - 1):
            @pl.when(k < n_ord)
            def _():
                start_weights(k)

    @pl.when(i < n_used)
    def _():
        k = ord_ref[i]

        @pl.when(first_ref[i] > 0)
        def _():
            for cp in weight_copies(k):
                cp.wait()

            @pl.when(k + (W_SLOTS - 1) < n_ord)
            def _():
                start_weights(k + (W_SLOTS - 1))

        slot = k % W_SLOTS

        def swiglu(rows):
            x = _from_row_tiles(x_ref[0:rows])
            h1 = _dot(x, w1b[slot].astype(BF16))
            h3 = _dot(x, w3b[slot].astype(BF16))
            he = (h1 * _sigmoid(h1) * h3).astype(BF16)
            y_ref[0:rows] = _to_row_tiles(_dot(he, w2b[slot].astype(BF16)))

        @pl.when(half_ref[i] == 0)
        def _():
            swiglu(MOE_TM)

        @pl.when(half_ref[i] > 0)
        def _():
            swiglu(MOE_TM // 2)
            y_ref[MOE_TM // 2:] = jnp.zeros((MOE_TM // 2, ROW_SUB, ROW_LANE), BF16)

    @pl.when(i >= n_used)
    def _():
        y_ref[...] = jnp.zeros_like(y_ref)


def _experts(sched, xs, w_e1, w_e3, w_e2):
    grid_spec = pltpu.PrefetchScalarGridSpec(
        num_scalar_prefetch=6,
        grid=(MOE_NT,),
        in_specs=[
            pl.BlockSpec((MOE_TM, ROW_SUB, ROW_LANE), lambda i, nu, *_: (jnp.minimum(i, nu[0] - 1), 0, 0)),
            pl.BlockSpec(memory_space=pl.ANY),
            pl.BlockSpec(memory_space=pl.ANY),
            pl.BlockSpec(memory_space=pl.ANY),
        ],
        out_specs=pl.BlockSpec((MOE_TM, ROW_SUB, ROW_LANE), lambda i, *_: (i, 0, 0)),
        scratch_shapes=[
            pltpu.VMEM((W_SLOTS, D_MODEL, MOE_FF), F32),
            pltpu.VMEM((W_SLOTS, D_MODEL, MOE_FF), F32),
            pltpu.VMEM((W_SLOTS, MOE_FF, D_MODEL), F32),
            pltpu.SemaphoreType.DMA((3, W_SLOTS)),
        ],
    )
    return pl.pallas_call(
        _expert_kernel,
        grid_spec=grid_spec,
        out_shape=jax.ShapeDtypeStruct((MOE_ROWS, ROW_SUB, ROW_LANE), BF16),
        compiler_params=_params(1),
        name="experts",
    )(sched["n_used"], sched["tile_first"], sched["tile_ord"], sched["ord_expert"], sched["n_ord"],
      sched["tile_half"], xs, w_e1, w_e3, w_e2)


def _combine_kernel(pos_ref, ys_ref, x2_ref, route_ref, g_ref, yp_ref, ysm_ref, gbuf, sem):
    i = pl.program_id(0)

    def start_gather(tile, slot):
        def body(r, carry):
            for k in range(MOE_TOPK):
                src_row = pos_ref[k, tile * ROW_TILE + r]
                pltpu.make_async_copy(ys_ref.at[src_row], gbuf.at[slot, k, r], sem.at[slot, k]).start(priority=k)
            return carry
        lax.fori_loop(0, ROW_TILE, body, 0, unroll=8)

    @pl.when(i == 0)
    def _():
        start_gather(0, 0)

    slot = i % 2

    @pl.when(i + 1 < pl.num_programs(0))
    def _():
        start_gather(i + 1, 1 - slot)

    for k in range(MOE_TOPK):
        pltpu.make_async_copy(ys_ref.at[pl.ds(0, ROW_TILE)], gbuf.at[slot, k], sem.at[slot, k]).wait()

    route = route_ref[...]
    lane = lax.broadcasted_iota(jnp.int32, route.shape, 1)
    w0 = jnp.sum(jnp.where(lane == 4, route, 0.0), axis=-1, keepdims=True)
    w1 = jnp.sum(jnp.where(lane == 5, route, 0.0), axis=-1, keepdims=True)
    g0 = _from_row_tiles(gbuf[slot, 0]).astype(F32)
    g1 = _from_row_tiles(gbuf[slot, 1]).astype(F32)
    x3 = x2_ref[...] + (g0 * w0 + g1 * w1)
    y = _rms(x3, g_ref[...])

    @pl.when(i < NP_TILES)
    def _():
        yp_ref[...] = y

    @pl.when(i == NP_TILES)
    def _():
        ysm_ref[...] = y


def _combine(pos, ys, x2, route, g_f):
    grid_spec = pltpu.PrefetchScalarGridSpec(
        num_scalar_prefetch=1,
        grid=(N_TILES,),
        in_specs=[
            pl.BlockSpec(memory_space=pl.ANY),
            pl.BlockSpec((ROW_TILE, D_MODEL), lambda i, pos: (i, 0)),
            pl.BlockSpec((ROW_TILE, ROUTE_LANES), lambda i, pos: (i, 0)),
            pl.BlockSpec((1, D_MODEL), lambda i, pos: (0, 0)),
        ],
        out_specs=[
            pl.BlockSpec((ROW_TILE, D_MODEL), lambda i, pos: (jnp.minimum(i, NP_TILES - 1), 0)),
            pl.BlockSpec((ROW_TILE, D_MODEL), lambda i, pos: (0, 0)),
        ],
        scratch_shapes=[pltpu.VMEM((2, MOE_TOPK, ROW_TILE, ROW_SUB, ROW_LANE), BF16),
                        pltpu.SemaphoreType.DMA((2, MOE_TOPK))],
    )
    return pl.pallas_call(
        _combine_kernel,
        grid_spec=grid_spec,
        out_shape=[
            jax.ShapeDtypeStruct((N_P, D_MODEL), F32),
            jax.ShapeDtypeStruct((N_S, D_MODEL), F32),
        ],
        compiler_params=_params(1),
        name="combine_norm",
    )(pos, ys, x2, route, g_f)


def _expert_schedule(counts):
    i32 = jnp.int32
    cnt = counts[0, MOE_GROUPS:MOE_GROUPS + MOE_EXPERTS].astype(i32)
    tiles = (cnt + MOE_TM - 1) // MOE_TM
    tile_end = jnp.cumsum(tiles)
    tile_start = tile_end - tiles
    n_used = tile_end[-1]
    t = jnp.arange(MOE_NT, dtype=i32)
    tile_expert = jnp.minimum(jnp.sum((t[:, None] >= tile_end[None, :]).astype(i32), axis=1), MOE_EXPERTS - 1)
    used = tiles > 0
    ord_of = jnp.cumsum(used.astype(i32)) - 1
    experts = jnp.arange(MOE_EXPERTS, dtype=i32)
    ord_expert = jnp.zeros((MOE_EXPERTS,), i32).at[jnp.where(used, ord_of, MOE_EXPERTS)].set(experts, mode="drop")
    return {
        "n_used": n_used.reshape(1).astype(i32),
        "tile_first": ((t == tile_start[tile_expert]) & (t < n_used)).astype(i32),
        "tile_ord": ord_of[tile_expert].astype(i32),
        "tile_half": (cnt[tile_expert] - (t - tile_start[tile_expert]) * MOE_TM <= MOE_TM // 2).astype(i32),
        "ord_expert": ord_expert,
        "n_ord": jnp.sum(used.astype(i32)).reshape(1),
        "zero_row": (jnp.maximum(tile_end - 1, 0) * MOE_TM).astype(i32),
        "zero_on": used.astype(i32),
    }


def kernel(x_prompt, x_sample, mem_prompt, state_ret, cache_mem_k, cache_mem_v, norm_mix, w_in, ret_gn,
           sg_ln_g, sg_ln_b, sg_ws, sg_bs, w_a_out, w_b_out, w_o, norm_xa, w_cq, w_ck, w_cv, w_co, norm_moe,
           w_rg, b_rg, w_re, b_re, w_e1, w_e3, w_e2, norm_f):
    xp = x_prompt.reshape(N_P, D_MODEL)
    xs = x_sample.reshape(N_S, D_MODEL)

    h = _norm_rows(xp, xs, norm_mix)
    w = w_in[0]
    qk = _inproj("rope", IN_TM_ROPE, 0, 2 * RET_QK, h, w, _rope_tables(IN_TM_ROPE))
    v = _inproj("copy", IN_TM, 2 * RET_QK, RET_V, h, w)
    gs = _inproj("silu", IN_TM, 2 * RET_QK + RET_V, RET_V, h, w)
    uv = _inproj("gelu", IN_TM, 2 * RET_QK + 2 * RET_V, 2 * SG_WIDTH, h, w)
    gab = _inproj("sigmoid", IN_TM, 2 * RET_QK + 2 * RET_V + 2 * SG_WIDTH, 2 * D_MODEL, h, w)

    a_p, ret_p, a_s, ret_s = _retention(qk, v, ret_gn, state_ret)

    b_all, sgv = _sgate(uv, sg_ln_g, sg_ln_b, sg_ws[0], sg_bs[0])

    x1, qx = _merge(a_p, a_s, gs, b_all, gab, xp, xs, w_a_out[0].astype(BF16), w_b_out[0].astype(BF16),
                    w_o[0].astype(BF16), norm_xa, w_cq[0].astype(BF16))

    mk, mv, mkb, mvb = _memkv(mem_prompt.reshape(BATCH * MEM_LEN, D_MODEL), w_ck[0], w_cv[0])
    ck = cache_mem_k.reshape(DEC_BATCH, MEM_LEN * XA_HEADS, XA_DH)
    cv = cache_mem_v.reshape(DEC_BATCH, MEM_LEN * XA_HEADS, XA_DH)

    pad = ROUTE_LANES - MOE_GROUPS - MOE_EXPERTS
    w_r = jnp.concatenate([w_rg[0], w_re[0], jnp.zeros((D_MODEL, pad), F32)], axis=1)
    b_r = jnp.concatenate([b_rg[0], b_re[0], jnp.zeros((pad,), F32)]).reshape(1, ROUTE_LANES)
    x2, hmw, route, counts = _route(qx, mkb, mvb, ck, cv, x1, w_co[0].astype(BF16), norm_moe,
                                    w_r.astype(BF16), b_r)

    pos_t = _positions(route, counts)
    sched = _expert_schedule(counts)
    xs_sorted = _dispatch(pos_t, sched["zero_row"], sched["zero_on"], sched["n_used"], hmw)
    ys = _experts(sched, xs_sorted, w_e1[0], w_e3[0], w_e2[0])
    y_p, y_s = _combine(pos_t, ys, x2, route, norm_f.reshape(1, D_MODEL))

    return (y_p.reshape(BATCH, SEQ, D_MODEL),
            y_s.reshape(DEC_BATCH, DEC_SEQ, D_MODEL),
            ret_p,
            mk.reshape(1, BATCH, MEM_LEN, XA_HEADS, XA_DH),
            mv.reshape(1, BATCH, MEM_LEN, XA_HEADS, XA_DH),
            ret_s,
            sgv.reshape(1, DEC_BATCH, DEC_SEQ, SG_WIDTH))
```

```python
import functools

import jax
import jax.numpy as jnp
from jax import lax
from jax.experimental import pallas as pl
from jax.experimental.pallas import tpu as pltpu

F32 = jnp.float32
BF16 = jnp.bfloat16

D_MODEL = 2048
BATCH = 4
SEQ = 2048
DEC_BATCH = 128
DEC_SEQ = 4
PAST_LEN = 16384
RET_HEADS = 8
RET_DK = 128
RET_DV = 256
RET_CHUNK = 128
ROPE_BASE = 10000.0
RET_QK = RET_HEADS * RET_DK
RET_V = RET_HEADS * RET_DV
SG_GROUPS = 4
SG_WIDTH = 2048
SG_CHUNK = 128
MEM_LEN = 256
XA_HEADS = 4
XA_DH = 128
XA_W = XA_HEADS * XA_DH
MOE_GROUPS = 4
MOE_PER_GROUP = 8
MOE_EXPERTS = MOE_GROUPS * MOE_PER_GROUP
MOE_TOPK = 2
MOE_FF = 512
EPS = 1e-6
IN_WIDTH = 2 * RET_QK + 2 * RET_V + 2 * SG_WIDTH + 2 * D_MODEL

N_P = BATCH * SEQ
N_S = DEC_BATCH * DEC_SEQ
N_ALL = N_P + N_S
ROW_TILE = 512
N_TILES = N_ALL // ROW_TILE
NP_TILES = N_P // ROW_TILE
MERGE_TILE = 256
MERGE_P_TILES = N_P // MERGE_TILE

IN_TM = 1024
IN_TM_ROPE = 1024
IN_TN = 2048

PROMPT_CHUNK = 256
RET_BB = DEC_BATCH // (BATCH * (SEQ // PROMPT_CHUNK))
RET_ROWS = RET_BB * DEC_SEQ
LOG_DEC_SEQ = 2
LOG_SG_CHUNK = 7
SAMPLE_BB = DEC_BATCH // NP_TILES
SAMPLE_ROWS = SAMPLE_BB * DEC_SEQ

MOE_TM = 256
LOG_MOE_TM = 8
MOE_NT = (N_ALL * MOE_TOPK + MOE_EXPERTS * (MOE_TM - 1) + MOE_TM - 1) // MOE_TM
MOE_ROWS = MOE_NT * MOE_TM
ROUTE_LANES = 128
ROW_SUB, ROW_LANE = 16, 128
W_SLOTS = 3

VMEM_LIMIT = 56 * 1024 * 1024


def _params(n_axes, vmem=VMEM_LIMIT):
    return pltpu.CompilerParams(dimension_semantics=("arbitrary",) * n_axes,
                                vmem_limit_bytes=vmem)


def _rms(x, g):
    ms = jnp.mean(x * x, axis=-1, keepdims=True)
    return (x * lax.rsqrt(ms + EPS)) * g


def _dot(a, b):
    return jnp.dot(a, b, preferred_element_type=F32)


def _sigmoid(x):
    return 0.5 * jnp.tanh(0.5 * x) + 0.5


def _to_row_tiles(x):
    return x.astype(BF16).reshape(x.shape[0], ROW_SUB, ROW_LANE)


def _from_row_tiles(t):
    return t.reshape(t.shape[0], D_MODEL)


def _dot_nt(a, b):
    return lax.dot_general(a, b, (((1,), (1,)), ((), ())), preferred_element_type=F32)


def _dot_tn(a, b):
    return lax.dot_general(a, b, (((0,), (0,)), ((), ())), preferred_element_type=F32)


def _norm_kernel(xp_ref, xs_ref, g_ref, h_ref):
    i = pl.program_id(0)

    @pl.when(i < NP_TILES)
    def _():
        h_ref[...] = _rms(xp_ref[...], g_ref[...]).astype(BF16)

    @pl.when(i == NP_TILES)
    def _():
        h_ref[...] = _rms(xs_ref[...], g_ref[...]).astype(BF16)


def _norm_rows(xp, xs, g):
    return pl.pallas_call(
        _norm_kernel,
        grid=(N_TILES,),
        in_specs=[
            pl.BlockSpec((ROW_TILE, D_MODEL), lambda i: (jnp.minimum(i, NP_TILES - 1), 0)),
            pl.BlockSpec((ROW_TILE, D_MODEL), lambda i: (0, 0)),
            pl.BlockSpec((1, D_MODEL), lambda i: (0, 0)),
        ],
        out_specs=pl.BlockSpec((ROW_TILE, D_MODEL), lambda i: (i, 0)),
        out_shape=jax.ShapeDtypeStruct((N_ALL, D_MODEL), BF16),
        compiler_params=_params(1),
        name="norm_rows",
    )(xp, xs, g)


def _inproj_kernel(kind, tm, h_ref, w_ref, *rest):
    if kind == "rope":
        cos_ref, sin_ref, z_ref, wb_ref = rest
    else:
        z_ref, wb_ref = rest
    j = pl.program_id(0)
    i = pl.program_id(1)
    last = N_P // tm

    del wb_ref

    def tile(rows):
        acc = _dot(h_ref[0:rows, :], w_ref[...].astype(BF16))
        if kind == "rope":
            c = cos_ref[0:rows, :]
            s = sin_ref[0:rows, :]
            heads_per_block = IN_TN // RET_DK
            for hb in range(heads_per_block):
                scale = jnp.where(j * heads_per_block + hb >= RET_HEADS, RET_DK ** -0.5, 1.0).astype(F32)
                cols = slice(hb * RET_DK, (hb + 1) * RET_DK)
                a = acc[:, cols]
                r = pltpu.roll(a, RET_DK // 2, axis=1)
                z_ref[0:rows, cols] = ((a * c + r * s) * scale).astype(BF16)
        elif kind == "copy":
            z_ref[0:rows, :] = acc.astype(BF16)
        elif kind == "silu":
            z_ref[0:rows, :] = (acc * _sigmoid(acc)).astype(BF16)
        elif kind == "gelu":
            z_ref[0:rows, :] = jax.nn.gelu(acc).astype(BF16)
        else:
            z_ref[0:rows, :] = _sigmoid(acc).astype(BF16)

    @pl.when(i < last)
    def _():
        tile(tm)

    @pl.when(i == last)
    def _():
        tile(N_S)


def _inproj(kind, tm, col0, width, h, w_in, tables=()):
    last = N_P // tm
    tab_idx = lambda j, i: (jnp.where(i < last, i % (SEQ // tm), SEQ // tm), 0)
    j0 = col0 // IN_TN
    return pl.pallas_call(
        functools.partial(_inproj_kernel, kind, tm),
        grid=(width // IN_TN, last + 1),
        in_specs=[
            pl.BlockSpec((tm, D_MODEL), lambda j, i: (i, 0)),
            pl.BlockSpec((D_MODEL, IN_TN), lambda j, i: (0, j0 + j)),
        ] + [pl.BlockSpec((tm, RET_DK), tab_idx) for _ in tables],
        out_specs=pl.BlockSpec((tm, IN_TN), lambda j, i: (i, j)),
        out_shape=jax.ShapeDtypeStruct((N_ALL, width), BF16),
        scratch_shapes=[pltpu.VMEM((D_MODEL, IN_TN), BF16)],
        compiler_params=_params(2),
        name="in_proj_" + kind,
    )(h, w_in, *tables)


def _rope_tables(tm):
    half = RET_DK // 2
    inv = ROPE_BASE ** (-jnp.arange(half, dtype=F32) / half)

    def tab(pos):
        ang = pos.astype(F32)[:, None] * inv[None, :]
        c, s = jnp.cos(ang), jnp.sin(ang)
        return jnp.concatenate([c, c], -1), jnp.concatenate([-s, s], -1)

    cp, sp = tab(jnp.arange(SEQ, dtype=jnp.int32))
    cs, ss = tab(PAST_LEN + jnp.arange(DEC_SEQ, dtype=jnp.int32))
    cs = jnp.tile(cs, (DEC_BATCH, 1))
    ss = jnp.tile(ss, (DEC_BATCH, 1))
    pad = jnp.zeros((tm - N_S, RET_DK), F32)
    return jnp.concatenate([cp, cs, pad], 0), jnp.concatenate([sp, ss, pad], 0)


def _decay_tables(chunk):
    lg = jnp.log1p(-jnp.power(2.0, -5.0 - jnp.arange(RET_HEADS, dtype=F32)))
    idx = jnp.arange(chunk, dtype=F32)
    rel = idx[:, None] - idx[None, :]
    dmask = jnp.where(rel >= 0, jnp.exp(lg[:, None, None] * jnp.maximum(rel, 0.0)), 0.0).astype(F32)
    xi = jnp.exp(lg[:, None] * (idx[None, :] + 1.0)).astype(F32)
    zeta = jnp.exp(lg[:, None] * (chunk - 1.0 - idx[None, :])).astype(F32)
    gc = jnp.exp(lg * chunk).astype(F32)
    return dmask, xi, zeta, gc


def _head_norm(o, gn):
    mu = jnp.mean(o, axis=-1, keepdims=True)
    d = o - mu
    var = jnp.mean(d * d, axis=-1, keepdims=True)
    return ((d * lax.rsqrt(var + EPS)) * gn).astype(BF16)


def _ret_kernel(gcp_ref, gcs_ref, q_ref, k_ref, v_ref, gn_ref, dm_ref, xi_ref, zt_ref,
                qs_ref, ks_ref, vs_ref, dms_ref, xis_ref, zts_ref, s0_ref,
                a_ref, sfin_ref, as_ref, s1_ref, s_ref):
    c = pl.program_id(1)

    @pl.when(c == 0)
    def _():
        s_ref[...] = jnp.zeros_like(s_ref)

    rows_k = lax.broadcasted_iota(jnp.int32, (RET_ROWS, RET_DK), 0) >> LOG_DEC_SEQ
    rows_v = lax.broadcasted_iota(jnp.int32, (RET_ROWS, RET_DV), 0) >> LOG_DEC_SEQ
    for h in range(RET_HEADS):
        kc = slice(h * RET_DK, (h + 1) * RET_DK)
        vc = slice(h * RET_DV, (h + 1) * RET_DV)
        gn = gn_ref[:, vc]

        qh = q_ref[:, kc]
        kh = k_ref[:, kc]
        vh = v_ref[:, vc]
        inner = _dot_nt(qh, kh) * dm_ref[h]
        o = _dot(inner.astype(BF16), vh)
        s_old = s_ref[h]
        xi = xi_ref[h]
        o = o + _dot(qh, s_old.astype(BF16)) * jnp.concatenate([xi, xi], axis=1)
        kz = (kh.astype(F32) * zt_ref[h]).astype(BF16)
        s_ref[h] = gcp_ref[h] * s_old + _dot_tn(kz, vh)
        a_ref[:, vc] = _head_norm(o, gn)

        qh = qs_ref[:, kc]
        kh = ks_ref[:, kc]
        vh = vs_ref[:, vc]
        inner = _dot_nt(qh, kh) * dms_ref[h]
        o = _dot(inner.astype(BF16), vh)
        xi = xis_ref[h]
        xi2 = jnp.concatenate([xi, xi], axis=1)
        kz = kh.astype(F32) * zts_ref[h]
        gch = gcs_ref[h]
        for b in range(RET_BB):
            s_old = s0_ref[0, b, h]
            cross = _dot(qh, s_old.astype(BF16)) * xi2
            o = o + jnp.where(rows_v == b, cross, 0.0)
            kz_b = jnp.where(rows_k == b, kz, 0.0).astype(BF16)
            s1_ref[0, b, h] = gch * s_old + _dot_tn(kz_b, vh)
        as_ref[:, vc] = _head_norm(o, gn)

    @pl.when(c == pl.num_programs(1) - 1)
    def _():
        sfin_ref[0, 0] = s_ref[...]


def _retention(qk, v, ret_gn, state):
    chunk = PROMPT_CHUNK
    n_chunks = SEQ // chunk
    dmask, xi, zeta, gc = _decay_tables(chunk)
    xi_b = jnp.broadcast_to(xi[:, :, None], (RET_HEADS, chunk, RET_DK))
    zeta_b = jnp.broadcast_to(zeta[:, :, None], (RET_HEADS, chunk, RET_DK))
    dmask_s, xi_s, zeta_s, gc_s = _decay_tables(DEC_SEQ)
    eye = jnp.eye(RET_BB, dtype=F32)
    dm_big = jax.vmap(lambda m: jnp.kron(eye, m))(dmask_s)
    xi_sb = jnp.broadcast_to(jnp.tile(xi_s, (1, RET_BB))[:, :, None], (RET_HEADS, RET_ROWS, RET_DK))
    zeta_sb = jnp.broadcast_to(jnp.tile(zeta_s, (1, RET_BB))[:, :, None], (RET_HEADS, RET_ROWS, RET_DK))

    row = lambda b, c: b * n_chunks + c
    srow = lambda b, c: N_P // RET_ROWS + row(b, c)
    const3 = lambda b, c: (0, 0, 0)
    smem = pl.BlockSpec(memory_space=pltpu.SMEM)
    st_spec = pl.BlockSpec((1, RET_BB, RET_HEADS, RET_DK, RET_DV), lambda b, c: (0, row(b, c), 0, 0, 0))
    return pl.pallas_call(
        _ret_kernel,
        grid=(BATCH, n_chunks),
        in_specs=[
            smem, smem,
            pl.BlockSpec((chunk, RET_QK), lambda b, c: (row(b, c), 0)),
            pl.BlockSpec((chunk, RET_QK), lambda b, c: (row(b, c), 1)),
            pl.BlockSpec((chunk, RET_V), lambda b, c: (row(b, c), 0)),
            pl.BlockSpec((1, RET_V), lambda b, c: (0, 0)),
            pl.BlockSpec((RET_HEADS, chunk, chunk), const3),
            pl.BlockSpec((RET_HEADS, chunk, RET_DK), const3),
            pl.BlockSpec((RET_HEADS, chunk, RET_DK), const3),
            pl.BlockSpec((RET_ROWS, RET_QK), lambda b, c: (srow(b, c), 0)),
            pl.BlockSpec((RET_ROWS, RET_QK), lambda b, c: (srow(b, c), 1)),
            pl.BlockSpec((RET_ROWS, RET_V), lambda b, c: (srow(b, c), 0)),
            pl.BlockSpec((RET_HEADS, RET_ROWS, RET_ROWS), const3),
            pl.BlockSpec((RET_HEADS, RET_ROWS, RET_DK), const3),
            pl.BlockSpec((RET_HEADS, RET_ROWS, RET_DK), const3),
            st_spec,
        ],
        out_specs=[
            pl.BlockSpec((chunk, RET_V), lambda b, c: (row(b, c), 0)),
            pl.BlockSpec((1, 1, RET_HEADS, RET_DK, RET_DV), lambda b, c: (0, b, 0, 0, 0)),
            pl.BlockSpec((RET_ROWS, RET_V), lambda b, c: (row(b, c), 0)),
            st_spec,
        ],
        out_shape=[
            jax.ShapeDtypeStruct((N_P, RET_V), BF16),
            jax.ShapeDtypeStruct((1, BATCH, RET_HEADS, RET_DK, RET_DV), F32),
            jax.ShapeDtypeStruct((N_S, RET_V), BF16),
            jax.ShapeDtypeStruct((1, DEC_BATCH, RET_HEADS, RET_DK, RET_DV), F32),
        ],
        scratch_shapes=[pltpu.VMEM((RET_HEADS, RET_DK, RET_DV), F32)],
        compiler_params=_params(2),
        name="retention",
    )(gc, gc_s, qk, qk, v, ret_gn, dmask, xi_b, zeta_b, qk, qk, v, dm_big, xi_sb, zeta_sb, state)


def _sgate_kernel(u_ref, v_ref, lg_ref, lb_ref, wp_ref, bp_ref, ws_ref, bs_ref, b_ref, sgv_ref, vln_ref):
    i = pl.program_id(0)
    v = v_ref[...].astype(F32)
    mu = jnp.mean(v, axis=-1, keepdims=True)
    d = v - mu
    var = jnp.mean(d * d, axis=-1, keepdims=True)
    vln_ref[...] = (d * lax.rsqrt(var + EPS)) * lg_ref[...] + lb_ref[...]
    gw = SG_WIDTH // SG_GROUPS
    lane_reps = gw // 128

    @pl.when(i < NP_TILES)
    def _():
        r = lax.broadcasted_iota(jnp.int32, (SG_CHUNK, SG_CHUNK), 0)
        c = lax.broadcasted_iota(jnp.int32, (SG_CHUNK, SG_CHUNK), 1)
        for g in range(SG_GROUPS):
            cols = slice(g * gw, (g + 1) * gw)
            w = jnp.where(c <= r, wp_ref[g], 0.0).astype(BF16)
            bias = jnp.concatenate([bp_ref[g]] * lane_reps, axis=1)
            chunks = [slice(ch * SG_CHUNK, (ch + 1) * SG_CHUNK) for ch in range(ROW_TILE // SG_CHUNK)]
            mixed = [_dot(w, vln_ref[rows, cols].astype(BF16)) for rows in chunks]
            for rows, m in zip(chunks, mixed):
                b_ref[rows, cols] = (u_ref[rows, cols].astype(F32) * (m + bias)).astype(BF16)

    @pl.when(i == NP_TILES)
    def _():
        sgv_ref[...] = vln_ref[...]
        r = lax.broadcasted_iota(jnp.int32, (ROW_TILE, ROW_TILE), 0)
        c = lax.broadcasted_iota(jnp.int32, (ROW_TILE, ROW_TILE), 1)
        keep = ((r >> LOG_DEC_SEQ) == (c >> LOG_DEC_SEQ)) & (c <= r)
        for g in range(SG_GROUPS):
            cols = slice(g * gw, (g + 1) * gw)
            w_rows = jnp.concatenate([ws_ref[g]] * (ROW_TILE // 8), axis=0)
            w_full = jnp.concatenate([w_rows] * (ROW_TILE // 128), axis=1)
            w = jnp.where(keep, w_full, 0.0).astype(BF16)
            b_rows = jnp.concatenate([bs_ref[g]] * (ROW_TILE // 8), axis=0)
            bias = jnp.concatenate([b_rows] * lane_reps, axis=1)
            mixed = _dot(w, vln_ref[:, cols].astype(BF16)) + bias
            b_ref[:, cols] = (u_ref[:, cols].astype(F32) * mixed).astype(BF16)


def _sgate(uv, ln_g, ln_b, sg_ws, sg_bs):
    b_p = jnp.broadcast_to(sg_bs[:, :, None], (SG_GROUPS, SG_CHUNK, 128))
    w_s = jnp.tile(sg_ws[:, :DEC_SEQ, :DEC_SEQ], (1, 8 // DEC_SEQ, 128 // DEC_SEQ))
    b_s = jnp.broadcast_to(jnp.tile(sg_bs[:, :DEC_SEQ], (1, 8 // DEC_SEQ))[:, :, None], (SG_GROUPS, 8, 128))
    const3 = lambda i: (0, 0, 0)
    return pl.pallas_call(
        _sgate_kernel,
        grid=(N_TILES,),
        in_specs=[
            pl.BlockSpec((ROW_TILE, SG_WIDTH), lambda i: (i, 0)),
            pl.BlockSpec((ROW_TILE, SG_WIDTH), lambda i: (i, 1)),
            pl.BlockSpec((1, SG_WIDTH), lambda i: (0, 0)),
            pl.BlockSpec((1, SG_WIDTH), lambda i: (0, 0)),
            pl.BlockSpec((SG_GROUPS, SG_CHUNK, SG_CHUNK), const3),
            pl.BlockSpec((SG_GROUPS, SG_CHUNK, 128), const3),
            pl.BlockSpec((SG_GROUPS, 8, 128), const3),
            pl.BlockSpec((SG_GROUPS, 8, 128), const3),
        ],
        out_specs=[
            pl.BlockSpec((ROW_TILE, SG_WIDTH), lambda i: (i, 0)),
            pl.BlockSpec((N_S, SG_WIDTH), lambda i: (0, 0)),
        ],
        out_shape=[
            jax.ShapeDtypeStruct((N_ALL, SG_WIDTH), BF16),
            jax.ShapeDtypeStruct((N_S, SG_WIDTH), F32),
        ],
        scratch_shapes=[pltpu.VMEM((ROW_TILE, SG_WIDTH), F32)],
        compiler_params=_params(1),
        name="spatial_gate",
    )(uv, uv, ln_g, ln_b, sg_ws, b_p, w_s, b_s)


def _merge_kernel(ap_ref, as_ref, gs_ref, b_ref, ga_ref, gb_ref, xp_ref, xs_ref, wa_ref, wb_ref, wo_ref,
                  g_ref, wq_ref, x1_ref, q_ref):
    i = pl.program_id(0)

    def run(a_norm, x):
        b = _dot(b_ref[...], wb_ref[...])
        a_in = (gs_ref[...].astype(F32) * a_norm.astype(F32)).astype(BF16)
        a = _dot(a_in, wa_ref[...])
        merged = ga_ref[...].astype(F32) * a + gb_ref[...].astype(F32) * b
        x1 = x + _dot(merged.astype(BF16), wo_ref[...])
        x1_ref[...] = x1
        q_ref[...] = _dot(_rms(x1, g_ref[...]).astype(BF16), wq_ref[...]).astype(BF16)

    @pl.when(i < MERGE_P_TILES)
    def _():
        run(ap_ref[...], xp_ref[...])

    @pl.when(i >= MERGE_P_TILES)
    def _():
        run(as_ref[...], xs_ref[...])


def _resident(shape):
    return pl.BlockSpec(shape, lambda i: (0,) * len(shape), pipeline_mode=pl.Buffered(1))


def _merge(a_p, a_s, gs, b_all, gab, xp, xs, wa, wb, wo, g_xa, wq):
    tm = MERGE_TILE
    prompt_tile = lambda i: (jnp.minimum(i, MERGE_P_TILES - 1), 0)
    sample_tile = lambda i: (jnp.maximum(i - MERGE_P_TILES, 0), 0)
    return pl.pallas_call(
        _merge_kernel,
        grid=(N_ALL // tm,),
        in_specs=[
            pl.BlockSpec((tm, RET_V), prompt_tile),
            pl.BlockSpec((tm, RET_V), sample_tile),
            pl.BlockSpec((tm, RET_V), lambda i: (i, 0)),
            pl.BlockSpec((tm, SG_WIDTH), lambda i: (i, 0)),
            pl.BlockSpec((tm, D_MODEL), lambda i: (i, 0)),
            pl.BlockSpec((tm, D_MODEL), lambda i: (i, 1)),
            pl.BlockSpec((tm, D_MODEL), prompt_tile),
            pl.BlockSpec((tm, D_MODEL), sample_tile),
            _resident((RET_V, D_MODEL)),
            _resident((SG_WIDTH, D_MODEL)),
            _resident((D_MODEL, D_MODEL)),
            pl.BlockSpec((1, D_MODEL), lambda i: (0, 0)),
            _resident((D_MODEL, XA_W)),
        ],
        out_specs=[
            pl.BlockSpec((tm, D_MODEL), lambda i: (i, 0)),
            pl.BlockSpec((tm, XA_W), lambda i: (i, 0)),
        ],
        out_shape=[
            jax.ShapeDtypeStruct((N_ALL, D_MODEL), F32),
            jax.ShapeDtypeStruct((N_ALL, XA_W), BF16),
        ],
        compiler_params=_params(1),
        name="merge_proj",
    )(a_p, a_s, gs, b_all, gab, gab, xp, xs, wa, wb, wo, g_xa, wq)


def _memkv_kernel(m_ref, wk_ref, wv_ref, k_ref, v_ref, kb_ref, vb_ref):
    m = m_ref[...].astype(BF16)
    k = _dot(m, wk_ref[...].astype(BF16))
    v = _dot(m, wv_ref[...].astype(BF16))
    k_ref[...] = k
    v_ref[...] = v
    kb_ref[...] = k.astype(BF16)
    vb_ref[...] = v.astype(BF16)


def _memkv(mem, w_ck, w_cv):
    rows = BATCH * MEM_LEN
    spec = pl.BlockSpec((MEM_LEN, XA_W), lambda b: (b, 0))
    wspec = pl.BlockSpec((D_MODEL, XA_W), lambda b: (0, 0))
    return pl.pallas_call(
        _memkv_kernel,
        grid=(BATCH,),
        in_specs=[pl.BlockSpec((MEM_LEN, D_MODEL), lambda b: (b, 0)), wspec, wspec],
        out_specs=[spec, spec, spec, spec],
        out_shape=[jax.ShapeDtypeStruct((rows, XA_W), F32)] * 2 + [jax.ShapeDtypeStruct((rows, XA_W), BF16)] * 2,
        compiler_params=_params(1),
        name="mem_kv",
    )(mem, w_ck, w_cv)


def _softmax_rows(s):
    m = jnp.max(s, axis=-1, keepdims=True)
    e = jnp.exp(s - m)
    return e / jnp.sum(e, axis=-1, keepdims=True)


def _xattn_prompt_rows(q_ref, k_ref, v_ref, after_scores=lambda: None):
    cols = [slice(h * XA_DH, (h + 1) * XA_DH) for h in range(XA_HEADS)]
    scores = [_dot_nt(q_ref[:, c], k_ref[:, c]) * (XA_DH ** -0.5) for c in cols]
    after_scores()
    probs = [_softmax_rows(s).astype(BF16) for s in scores]
    heads = [_dot(p, v_ref[:, c]).astype(BF16) for p, c in zip(probs, cols)]
    return jnp.concatenate(heads, axis=1)


def _xattn_sample_scores(q_ref, k_ref):
    qf = q_ref[...].astype(F32)
    scores = []
    for b in range(SAMPLE_BB):
        rows = slice(b * DEC_SEQ, (b + 1) * DEC_SEQ)
        qb = jnp.concatenate([qf[rows, h * XA_DH:(h + 1) * XA_DH] for h in range(XA_HEADS)], axis=0)
        scores.append(_dot_nt(qb.astype(BF16), k_ref[b].astype(BF16)) * (XA_DH ** -0.5))
    return scores


def _xattn_sample_finish(scores, v_ref, o_ref):
    n_q = XA_HEADS * DEC_SEQ
    n_kv = MEM_LEN * XA_HEADS
    r = lax.broadcasted_iota(jnp.int32, (n_q, n_kv), 0)
    c = lax.broadcasted_iota(jnp.int32, (n_q, n_kv), 1)
    head_ok = (r >> LOG_DEC_SEQ) == (c & (XA_HEADS - 1))
    probs = [_softmax_rows(jnp.where(head_ok, s, -1e30)).astype(BF16) for s in scores]
    for b in range(SAMPLE_BB):
        rows = slice(b * DEC_SEQ, (b + 1) * DEC_SEQ)
        o = _dot(probs[b], v_ref[b].astype(BF16))
        for h in range(XA_HEADS):
            o_ref[h, rows, :] = o[h * DEC_SEQ:(h + 1) * DEC_SEQ, :]


def _route_kernel(qp_ref, mk_ref, mv_ref, qs_ref, ck_ref, cv_ref, x1_ref, wo_ref, g_ref, wr_ref, br_ref,
                  x2_ref, hm_ref, route_ref, cnt_ref, cnti_ref, run_ref, os_ref):
    i = pl.program_id(0)

    @pl.when(i == 0)
    def _():
        run_ref[...] = jnp.zeros_like(run_ref)

    def run(o, between=lambda: None):
        x2 = x1_ref[...] + _dot(o, wo_ref[...])
        x2_ref[...] = x2
        hm = _rms(x2, g_ref[...])
        hm_ref[...] = _to_row_tiles(hm)
        logits = _dot(hm.astype(BF16), wr_ref[...]) + br_ref[...]
        lane = lax.broadcasted_iota(jnp.int32, logits.shape, 1)
        lane_f = lane.astype(F32)
        neg = jnp.float32(-jnp.inf)
        big = jnp.float32(1 << 20)
        is_g = lane < MOE_GROUPS
        gl = jnp.where(is_g, logits, neg)
        gmax = jnp.max(gl, axis=-1, keepdims=True)
        g_sel = jnp.min(jnp.where(gl == gmax, lane_f, big), axis=-1, keepdims=True)
        g_w = 1.0 / jnp.sum(jnp.where(is_g, jnp.exp(logits - gmax), 0.0), axis=-1, keepdims=True)
        between()
        e_lane = lane - MOE_GROUPS
        e_group = (e_lane >> 3).astype(F32)
        in_grp = (e_lane >= 0) & (e_lane < MOE_EXPERTS) & (e_group == g_sel)
        el = jnp.where(in_grp, logits, neg)
        v0 = jnp.max(el, axis=-1, keepdims=True)
        i0 = jnp.min(jnp.where(el == v0, lane_f, big), axis=-1, keepdims=True)
        el1 = jnp.where(lane_f == i0, neg, el)
        v1 = jnp.max(el1, axis=-1, keepdims=True)
        i1 = jnp.min(jnp.where(el1 == v1, lane_f, big), axis=-1, keepdims=True)
        ex = jnp.exp(v1 - v0)
        den = 1.0 + ex
        w0 = (1.0 / den) * g_w
        w1 = (ex / den) * g_w
        a0 = (lane_f == i0).astype(F32)
        a1 = (lane_f == i1).astype(F32)
        a = a0 + a1
        rr = lax.broadcasted_iota(jnp.int32, (ROW_TILE, ROW_TILE), 0)
        cc = lax.broadcasted_iota(jnp.int32, (ROW_TILE, ROW_TILE), 1)
        lower = jnp.where(cc < rr, 1.0, 0.0).astype(BF16)
        before = _dot(lower, a.astype(BF16)) + run_ref[...]
        rank0 = jnp.sum(before * a0, axis=-1, keepdims=True)
        rank1 = jnp.sum(before * a1, axis=-1, keepdims=True)
        run_ref[...] += jnp.sum(a, axis=0, keepdims=True)
        e0 = i0 - MOE_GROUPS
        e1 = i1 - MOE_GROUPS
        route = jnp.where(lane == 0, e0, 0.0)
        route = jnp.where(lane == 1, e1, route)
        route = jnp.where(lane == 2, rank0, route)
        route = jnp.where(lane == 3, rank1, route)
        route = jnp.where(lane == 4, w0, route)
        route = jnp.where(lane == 5, w1, route)
        route_ref[...] = route
        cnt_ref[...] = run_ref[...]
        cnti_ref[...] = run_ref[...].astype(jnp.int32)

    @pl.when(i < NP_TILES)
    def _():
        scores = []
        o_prompt = _xattn_prompt_rows(qp_ref, mk_ref, mv_ref,
                                      after_scores=lambda: scores.extend(_xattn_sample_scores(qs_ref, ck_ref)))
        run(o_prompt, between=lambda: _xattn_sample_finish(scores, cv_ref, os_ref.at[i]))

    @pl.when(i == NP_TILES)
    def _():
        heads = [os_ref[:, h].reshape(N_S, XA_DH) for h in range(XA_HEADS)]
        run(jnp.concatenate(heads, axis=1).astype(BF16))


def _route(qx, mkb, mvb, ck, cv, x1, w_co, g_moe, w_r, b_r):
    prompt_step = lambda i: jnp.minimum(i, NP_TILES - 1)
    kv = pl.BlockSpec((SAMPLE_BB, MEM_LEN * XA_HEADS, XA_DH), lambda i: (prompt_step(i), 0, 0))
    mem_kv = pl.BlockSpec((MEM_LEN, XA_W), lambda i: (prompt_step(i) // (SEQ // ROW_TILE), 0))
    return pl.pallas_call(
        _route_kernel,
        grid=(N_TILES,),
        in_specs=[
            pl.BlockSpec((ROW_TILE, XA_W), lambda i: (prompt_step(i), 0)),
            mem_kv, mem_kv,
            pl.BlockSpec((SAMPLE_ROWS, XA_W), lambda i: (N_P // SAMPLE_ROWS + prompt_step(i), 0)),
            kv, kv,
            pl.BlockSpec((ROW_TILE, D_MODEL), lambda i: (i, 0)),
            _resident((XA_W, D_MODEL)),
            pl.BlockSpec((1, D_MODEL), lambda i: (0, 0)),
            _resident((D_MODEL, ROUTE_LANES)),
            pl.BlockSpec((1, ROUTE_LANES), lambda i: (0, 0)),
        ],
        out_specs=[
            pl.BlockSpec((ROW_TILE, D_MODEL), lambda i: (i, 0)),
            pl.BlockSpec((ROW_TILE, ROW_SUB, ROW_LANE), lambda i: (i, 0, 0)),
            pl.BlockSpec((ROW_TILE, ROUTE_LANES), lambda i: (i, 0)),
            pl.BlockSpec((1, ROUTE_LANES), lambda i: (0, 0)),
            pl.BlockSpec((1, ROUTE_LANES), lambda i: (0, 0)),
        ],
        out_shape=[
            jax.ShapeDtypeStruct((N_ALL, D_MODEL), F32),
            jax.ShapeDtypeStruct((N_ALL, ROW_SUB, ROW_LANE), BF16),
            jax.ShapeDtypeStruct((N_ALL, ROUTE_LANES), F32),
            jax.ShapeDtypeStruct((1, ROUTE_LANES), F32),
            jax.ShapeDtypeStruct((1, ROUTE_LANES), jnp.int32),
        ],
        scratch_shapes=[pltpu.VMEM((1, ROUTE_LANES), F32),
                        pltpu.VMEM((NP_TILES, XA_HEADS, SAMPLE_ROWS, XA_DH), F32)],
        compiler_params=_params(1),
        name="xa_out_route",
    )(qx, mkb, mvb, qx, ck, cv, x1, w_co, g_moe, w_r, b_r)


def _positions_kernel(route_ref, cnt_ref, pos_ref):
    route = route_ref[...]
    lane = lax.broadcasted_iota(jnp.int32, route.shape, 1)
    lane_f = lane.astype(F32)
    tiles = jnp.floor((cnt_ref[...] + (MOE_TM - 1)) * (1.0 / MOE_TM))
    lr = lax.broadcasted_iota(jnp.int32, (ROUTE_LANES, ROUTE_LANES), 0)
    lc = lax.broadcasted_iota(jnp.int32, (ROUTE_LANES, ROUTE_LANES), 1)
    before = jnp.where(lr < lc, 1.0, 0.0).astype(BF16)
    tiles8 = jnp.broadcast_to(tiles, (8, ROUTE_LANES)).astype(BF16)
    start = _dot(tiles8, before)[0:1, :] * MOE_TM

    def col(k):
        return jnp.sum(jnp.where(lane == k, route, 0.0), axis=-1, keepdims=True)

    def first_row(e):
        return jnp.sum(jnp.where(lane_f == e + MOE_GROUPS, start, 0.0), axis=-1, keepdims=True)

    p0 = first_row(col(0)) + col(2)
    p1 = first_row(col(1)) + col(3)
    p = jnp.where(lane == 0, p0, jnp.where(lane == 1, p1, 0.0))
    pos_ref[...] = p.T[0:8, :].astype(jnp.int32)


def _positions(route, counts):
    rows = N_ALL // 4
    return pl.pallas_call(
        _positions_kernel,
        grid=(4,),
        in_specs=[
            pl.BlockSpec((rows, ROUTE_LANES), lambda i: (i, 0)),
            pl.BlockSpec((1, ROUTE_LANES), lambda i: (0, 0)),
        ],
        out_specs=pl.BlockSpec((8, rows), lambda i: (0, i)),
        out_shape=jax.ShapeDtypeStruct((8, N_ALL), jnp.int32),
        compiler_params=_params(1),
        name="positions",
    )(route, counts)


def _dispatch_kernel(pos_ref, zrow_ref, zon_ref, nu_ref, hm_ref, xs_ref, zbuf, sem, zsem, tbuf, tsem):
    i = pl.program_id(0)

    def zero_tile(row):
        return pltpu.make_async_copy(zbuf, xs_ref.at[pl.ds(pl.multiple_of(row, MOE_TM), MOE_TM)], zsem)

    @pl.when(i == 0)
    def _():
        zbuf[...] = jnp.zeros_like(zbuf)
        for e in range(MOE_EXPERTS):
            @pl.when(zon_ref[e] > 0)
            def _():
                zero_tile(zrow_ref[e]).start()

        def start_tail(t, carry):
            zero_tile(t * MOE_TM).start()
            return carry

        def wait_tail(t, carry):
            zero_tile(t * MOE_TM).wait()
            return carry

        lax.fori_loop(nu_ref[0], MOE_NT, start_tail, 0)
        for e in range(MOE_EXPERTS):
            @pl.when(zon_ref[e] > 0)
            def _():
                zero_tile(zrow_ref[e]).wait()
        lax.fori_loop(nu_ref[0], MOE_NT, wait_tail, 0)

    n_steps = pl.num_programs(0)

    def tile_copy(t):
        return pltpu.make_async_copy(hm_ref.at[pl.ds(t * ROW_TILE, ROW_TILE)], tbuf.at[t % 3], tsem.at[t % 3])

    def wait_rows(t):
        for k in range(MOE_TOPK):
            pltpu.make_async_copy(tbuf.at[t % 3], xs_ref.at[pl.ds(0, ROW_TILE)], sem.at[t % 2, k]).wait()

    @pl.when(i == 0)
    def _():
        tile_copy(0).start()

    tile_copy(i).wait()

    @pl.when(i + 1 < n_steps)
    def _():
        tile_copy(i + 1).start()

    slot = i % 3
    par = i % 2
    base = i * ROW_TILE

    def start(r, carry):
        for k in range(MOE_TOPK):
            dst_row = pos_ref[k, base + r]
            pltpu.make_async_copy(tbuf.at[slot, r], xs_ref.at[dst_row], sem.at[par, k]).start(priority=k)
        return carry

    lax.fori_loop(0, ROW_TILE, start, 0, unroll=8)

    @pl.when(i > 0)
    def _():
        wait_rows(i - 1)

    @pl.when(i == n_steps - 1)
    def _():
        wait_rows(i)


def _dispatch(pos_t, zero_row, zero_on, n_used, hmw):
    grid_spec = pltpu.PrefetchScalarGridSpec(
        num_scalar_prefetch=4,
        grid=(N_TILES,),
        in_specs=[pl.BlockSpec(memory_space=pl.ANY)],
        out_specs=pl.BlockSpec(memory_space=pl.ANY),
        scratch_shapes=[pltpu.VMEM((MOE_TM, ROW_SUB, ROW_LANE), BF16),
                        pltpu.SemaphoreType.DMA((2, MOE_TOPK)), pltpu.SemaphoreType.DMA(()),
                        pltpu.VMEM((3, ROW_TILE, ROW_SUB, ROW_LANE), BF16), pltpu.SemaphoreType.DMA((3,))],
    )
    return pl.pallas_call(
        _dispatch_kernel,
        grid_spec=grid_spec,
        out_shape=jax.ShapeDtypeStruct((MOE_ROWS, ROW_SUB, ROW_LANE), BF16),
        compiler_params=_params(1),
        name="dispatch",
    )(pos_t, zero_row, zero_on, n_used, hmw)


def _expert_kernel(nu_ref, first_ref, ord_ref, oe_ref, no_ref, half_ref, x_ref, w1_hbm, w3_hbm, w2_hbm, y_ref,
                   w1b, w3b, w2b, sem):
    i = pl.program_id(0)
    n_used = nu_ref[0]
    n_ord = no_ref[0]

    def weight_copies(k):
        e = oe_ref[k]
        slot = k % W_SLOTS
        return (pltpu.make_async_copy(w1_hbm.at[e], w1b.at[slot], sem.at[0, slot]),
                pltpu.make_async_copy(w3_hbm.at[e], w3b.at[slot], sem.at[1, slot]),
                pltpu.make_async_copy(w2_hbm.at[e], w2b.at[slot], sem.at[2, slot]))

    def start_weights(k):
        for cp in weight_copies(k):
            cp.start(priority=1)

    @pl.when(i == 0)
    def _():
        for k in range(W_SLOTS - 1):
            @pl.when(k < n_ord)
            def _():
                start_weights(k)

    @pl.when(i < n_used)
    def _():
        k = ord_ref[i]

        @pl.when(first_ref[i] > 0)
        def _():
            for cp in weight_copies(k):
                cp.wait()

            @pl.when(k + (W_SLOTS - 1) < n_ord)
            def _():
                start_weights(k + (W_SLOTS - 1))

        slot = k % W_SLOTS

        def swiglu(rows):
            x = _from_row_tiles(x_ref[0:rows])
            h1 = _dot(x, w1b[slot].astype(BF16))
            h3 = _dot(x, w3b[slot].astype(BF16))
            he = (h1 * _sigmoid(h1) * h3).astype(BF16)
            y_ref[0:rows] = _to_row_tiles(_dot(he, w2b[slot].astype(BF16)))

        @pl.when(half_ref[i] == 0)
        def _():
            swiglu(MOE_TM)

        @pl.when(half_ref[i] > 0)
        def _():
            swiglu(MOE_TM // 2)
            y_ref[MOE_TM // 2:] = jnp.zeros((MOE_TM // 2, ROW_SUB, ROW_LANE), BF16)

    @pl.when(i >= n_used)
    def _():
        y_ref[...] = jnp.zeros_like(y_ref)


def _experts(sched, xs, w_e1, w_e3, w_e2):
    grid_spec = pltpu.PrefetchScalarGridSpec(
        num_scalar_prefetch=6,
        grid=(MOE_NT,),
        in_specs=[
            pl.BlockSpec((MOE_TM, ROW_SUB, ROW_LANE), lambda i, nu, *_: (jnp.minimum(i, nu[0] - 1), 0, 0)),
            pl.BlockSpec(memory_space=pl.ANY),
            pl.BlockSpec(memory_space=pl.ANY),
            pl.BlockSpec(memory_space=pl.ANY),
        ],
        out_specs=pl.BlockSpec((MOE_TM, ROW_SUB, ROW_LANE), lambda i, *_: (i, 0, 0)),
        scratch_shapes=[
            pltpu.VMEM((W_SLOTS, D_MODEL, MOE_FF), F32),
            pltpu.VMEM((W_SLOTS, D_MODEL, MOE_FF), F32),
            pltpu.VMEM((W_SLOTS, MOE_FF, D_MODEL), F32),
            pltpu.SemaphoreType.DMA((3, W_SLOTS)),
        ],
    )
    return pl.pallas_call(
        _expert_kernel,
        grid_spec=grid_spec,
        out_shape=jax.ShapeDtypeStruct((MOE_ROWS, ROW_SUB, ROW_LANE), BF16),
        compiler_params=_params(1),
        name="experts",
    )(sched["n_used"], sched["tile_first"], sched["tile_ord"], sched["ord_expert"], sched["n_ord"],
      sched["tile_half"], xs, w_e1, w_e3, w_e2)


def _combine_kernel(pos_ref, ys_ref, x2_ref, route_ref, g_ref, yp_ref, ysm_ref, gbuf, sem):
    i = pl.program_id(0)

    def start_gather(tile, slot):
        def body(r, carry):
            for k in range(MOE_TOPK):
                src_row = pos_ref[k, tile * ROW_TILE + r]
                pltpu.make_async_copy(ys_ref.at[src_row], gbuf.at[slot, k, r], sem.at[slot, k]).start(priority=k)
            return carry
        lax.fori_loop(0, ROW_TILE, body, 0, unroll=8)

    @pl.when(i == 0)
    def _():
        start_gather(0, 0)

    slot = i % 2

    @pl.when(i + 1 < pl.num_programs(0))
    def _():
        start_gather(i + 1, 1 - slot)

    for k in range(MOE_TOPK):
        pltpu.make_async_copy(ys_ref.at[pl.ds(0, ROW_TILE)], gbuf.at[slot, k], sem.at[slot, k]).wait()

    route = route_ref[...]
    lane = lax.broadcasted_iota(jnp.int32, route.shape, 1)
    w0 = jnp.sum(jnp.where(lane == 4, route, 0.0), axis=-1, keepdims=True)
    w1 = jnp.sum(jnp.where(lane == 5, route, 0.0), axis=-1, keepdims=True)
    g0 = _from_row_tiles(gbuf[slot, 0]).astype(F32)
    g1 = _from_row_tiles(gbuf[slot, 1]).astype(F32)
    x3 = x2_ref[...] + (g0 * w0 + g1 * w1)
    y = _rms(x3, g_ref[...])

    @pl.when(i < NP_TILES)
    def _():
        yp_ref[...] = y

    @pl.when(i == NP_TILES)
    def _():
        ysm_ref[...] = y


def _combine(pos, ys, x2, route, g_f):
    grid_spec = pltpu.PrefetchScalarGridSpec(
        num_scalar_prefetch=1,
        grid=(N_TILES,),
        in_specs=[
            pl.BlockSpec(memory_space=pl.ANY),
            pl.BlockSpec((ROW_TILE, D_MODEL), lambda i, pos: (i, 0)),
            pl.BlockSpec((ROW_TILE, ROUTE_LANES), lambda i, pos: (i, 0)),
            pl.BlockSpec((1, D_MODEL), lambda i, pos: (0, 0)),
        ],
        out_specs=[
            pl.BlockSpec((ROW_TILE, D_MODEL), lambda i, pos: (jnp.minimum(i, NP_TILES - 1), 0)),
            pl.BlockSpec((ROW_TILE, D_MODEL), lambda i, pos: (0, 0)),
        ],
        scratch_shapes=[pltpu.VMEM((2, MOE_TOPK, ROW_TILE, ROW_SUB, ROW_LANE), BF16),
                        pltpu.SemaphoreType.DMA((2, MOE_TOPK))],
    )
    return pl.pallas_call(
        _combine_kernel,
        grid_spec=grid_spec,
        out_shape=[
            jax.ShapeDtypeStruct((N_P, D_MODEL), F32),
            jax.ShapeDtypeStruct((N_S, D_MODEL), F32),
        ],
        compiler_params=_params(1),
        name="combine_norm",
    )(pos, ys, x2, route, g_f)


_SCHEDULE_FIELDS = ("n_used", "tile_first", "tile_ord", "tile_half", "ord_expert", "n_ord", "zero_row", "zero_on")


def _schedule_kernel(cnt_ref, nu_ref, first_ref, ord_ref, half_ref, oe_ref, no_ref, zrow_ref, zon_ref):
    i32 = jnp.int32

    def clear_tile(t, carry):
        first_ref[t] = i32(0)
        ord_ref[t] = i32(0)
        half_ref[t] = i32(0)
        return carry

    lax.fori_loop(0, MOE_NT, clear_tile, 0)

    def clear_expert(e, carry):
        oe_ref[e] = i32(0)
        return carry

    lax.fori_loop(0, MOE_EXPERTS, clear_expert, 0)

    def expert(e, carry):
        t0, k = carry
        c = cnt_ref[0, MOE_GROUPS + e]
        nt = lax.shift_right_logical(c + (MOE_TM - 1), i32(LOG_MOE_TM))
        used = jnp.where(nt > 0, i32(1), i32(0))
        zon_ref[e] = used
        zrow_ref[e] = jnp.maximum(t0 + nt - 1, 0) * MOE_TM

        @pl.when(nt > 0)
        def _():
            oe_ref[k] = e

        def tile(j, cc):
            t = t0 + j
            first_ref[t] = jnp.where(j == 0, i32(1), i32(0))
            ord_ref[t] = k
            half_ref[t] = jnp.where(c - j * MOE_TM <= MOE_TM // 2, i32(1), i32(0))
            return cc

        lax.fori_loop(0, nt, tile, 0)
        return t0 + nt, k + used

    n_tiles, n_experts = lax.fori_loop(0, MOE_EXPERTS, expert, (i32(0), i32(0)))
    nu_ref[0] = n_tiles
    no_ref[0] = n_experts


def _expert_schedule(counts_i32):
    smem = pl.BlockSpec(memory_space=pltpu.SMEM)
    sizes = (1, MOE_NT, MOE_NT, MOE_NT, MOE_EXPERTS, 1, MOE_EXPERTS, MOE_EXPERTS)
    outs = pl.pallas_call(
        _schedule_kernel,
        in_specs=[smem],
        out_specs=[smem] * len(sizes),
        out_shape=[jax.ShapeDtypeStruct((n,), jnp.int32) for n in sizes],
        name="expert_schedule",
    )(counts_i32)
    return dict(zip(_SCHEDULE_FIELDS, outs))


def kernel(x_prompt, x_sample, mem_prompt, state_ret, cache_mem_k, cache_mem_v, norm_mix, w_in, ret_gn,
           sg_ln_g, sg_ln_b, sg_ws, sg_bs, w_a_out, w_b_out, w_o, norm_xa, w_cq, w_ck, w_cv, w_co, norm_moe,
           w_rg, b_rg, w_re, b_re, w_e1, w_e3, w_e2, norm_f):
    xp = x_prompt.reshape(N_P, D_MODEL)
    xs = x_sample.reshape(N_S, D_MODEL)

    h = _norm_rows(xp, xs, norm_mix)
    w = w_in[0]
    qk = _inproj("rope", IN_TM_ROPE, 0, 2 * RET_QK, h, w, _rope_tables(IN_TM_ROPE))
    v = _inproj("copy", IN_TM, 2 * RET_QK, RET_V, h, w)
    gs = _inproj("silu", IN_TM, 2 * RET_QK + RET_V, RET_V, h, w)
    uv = _inproj("gelu", IN_TM, 2 * RET_QK + 2 * RET_V, 2 * SG_WIDTH, h, w)
    gab = _inproj("sigmoid", IN_TM, 2 * RET_QK + 2 * RET_V + 2 * SG_WIDTH, 2 * D_MODEL, h, w)

    a_p, ret_p, a_s, ret_s = _retention(qk, v, ret_gn, state_ret)

    b_all, sgv = _sgate(uv, sg_ln_g, sg_ln_b, sg_ws[0], sg_bs[0])

    x1, qx = _merge(a_p, a_s, gs, b_all, gab, xp, xs, w_a_out[0].astype(BF16), w_b_out[0].astype(BF16),
                    w_o[0].astype(BF16), norm_xa, w_cq[0].astype(BF16))

    mk, mv, mkb, mvb = _memkv(mem_prompt.reshape(BATCH * MEM_LEN, D_MODEL), w_ck[0], w_cv[0])
    ck = cache_mem_k.reshape(DEC_BATCH, MEM_LEN * XA_HEADS, XA_DH)
    cv = cache_mem_v.reshape(DEC_BATCH, MEM_LEN * XA_HEADS, XA_DH)

    pad = ROUTE_LANES - MOE_GROUPS - MOE_EXPERTS
    w_r = jnp.concatenate([w_rg[0], w_re[0], jnp.zeros((D_MODEL, pad), F32)], axis=1)
    b_r = jnp.concatenate([b_rg[0], b_re[0], jnp.zeros((pad,), F32)]).reshape(1, ROUTE_LANES)
    x2, hmw, route, counts, counts_i32 = _route(qx, mkb, mvb, ck, cv, x1, w_co[0].astype(BF16), norm_moe,
                                                w_r.astype(BF16), b_r)

    pos_t = _positions(route, counts)
    sched = _expert_schedule(counts_i32)
    xs_sorted = _dispatch(pos_t, sched["zero_row"], sched["zero_on"], sched["n_used"], hmw)
    ys = _experts(sched, xs_sorted, w_e1[0], w_e3[0], w_e2[0])
    y_p, y_s = _combine(pos_t, ys, x2, route, norm_f.reshape(1, D_MODEL))

    return (y_p.reshape(BATCH, SEQ, D_MODEL),
            y_s.reshape(DEC_BATCH, DEC_SEQ, D_MODEL),
            ret_p,
            mk.reshape(1, BATCH, MEM_LEN, XA_HEADS, XA_DH),
            mv.reshape(1, BATCH, MEM_LEN, XA_HEADS, XA_DH),
            ret_s,
            sgv.reshape(1, DEC_BATCH, DEC_SEQ, SG_WIDTH))
```

```python
import functools

import jax
import jax.numpy as jnp
from jax import lax
from jax.experimental import pallas as pl
from jax.experimental.pallas import tpu as pltpu

F32 = jnp.float32
BF16 = jnp.bfloat16

D_MODEL = 2048
BATCH = 4
SEQ = 2048
DEC_BATCH = 128
DEC_SEQ = 4
PAST_LEN = 16384
RET_HEADS = 8
RET_DK = 128
RET_DV = 256
RET_CHUNK = 128
ROPE_BASE = 10000.0
RET_QK = RET_HEADS * RET_DK
RET_V = RET_HEADS * RET_DV
SG_GROUPS = 4
SG_WIDTH = 2048
SG_CHUNK = 128
MEM_LEN = 256
XA_HEADS = 4
XA_DH = 128
XA_W = XA_HEADS * XA_DH
MOE_GROUPS = 4
MOE_PER_GROUP = 8
MOE_EXPERTS = MOE_GROUPS * MOE_PER_GROUP
MOE_TOPK = 2
MOE_FF = 512
EPS = 1e-6
IN_WIDTH = 2 * RET_QK + 2 * RET_V + 2 * SG_WIDTH + 2 * D_MODEL

N_P = BATCH * SEQ
N_S = DEC_BATCH * DEC_SEQ
N_ALL = N_P + N_S
ROW_TILE = 512
N_TILES = N_ALL // ROW_TILE
NP_TILES = N_P // ROW_TILE
MERGE_TILE = 256
MERGE_P_TILES = N_P // MERGE_TILE

IN_TM = 1024
IN_TM_ROPE = 1024
IN_TN = 2048
MERGE_W_BLOCKS = 4

PROMPT_CHUNK = 256
RET_BB = DEC_BATCH // (BATCH * (SEQ // PROMPT_CHUNK))
RET_ROWS = RET_BB * DEC_SEQ
LOG_DEC_SEQ = 2
LOG_SG_CHUNK = 7
SAMPLE_BB = DEC_BATCH // NP_TILES
SAMPLE_ROWS = SAMPLE_BB * DEC_SEQ

MOE_TM = 256
LOG_MOE_TM = 8
MOE_NT = (N_ALL * MOE_TOPK + MOE_EXPERTS * (MOE_TM - 1) + MOE_TM - 1) // MOE_TM
MOE_ROWS = MOE_NT * MOE_TM
ROUTE_LANES = 128
ROW_SUB, ROW_LANE = 16, 128
W_SLOTS = 3

VMEM_LIMIT = 56 * 1024 * 1024


def _params(n_axes, vmem=VMEM_LIMIT):
    return pltpu.CompilerParams(dimension_semantics=("arbitrary",) * n_axes,
                                vmem_limit_bytes=vmem)


def _rms(x, g):
    ms = jnp.mean(x * x, axis=-1, keepdims=True)
    return (x * lax.rsqrt(ms + EPS)) * g


def _dot(a, b):
    return jnp.dot(a, b, preferred_element_type=F32)


def _sigmoid(x):
    return 0.5 * jnp.tanh(0.5 * x) + 0.5


def _to_row_tiles(x):
    return x.astype(BF16).reshape(x.shape[0], ROW_SUB, ROW_LANE)


def _from_row_tiles(t):
    return t.reshape(t.shape[0], D_MODEL)


def _dot_nt(a, b):
    return lax.dot_general(a, b, (((1,), (1,)), ((), ())), preferred_element_type=F32)


def _dot_tn(a, b):
    return lax.dot_general(a, b, (((0,), (0,)), ((), ())), preferred_element_type=F32)


def _norm_kernel(xp_ref, xs_ref, g_ref, h_ref):
    i = pl.program_id(0)

    @pl.when(i < NP_TILES)
    def _():
        h_ref[...] = _rms(xp_ref[...], g_ref[...]).astype(BF16)

    @pl.when(i == NP_TILES)
    def _():
        h_ref[...] = _rms(xs_ref[...], g_ref[...]).astype(BF16)


def _norm_rows(xp, xs, g):
    return pl.pallas_call(
        _norm_kernel,
        grid=(N_TILES,),
        in_specs=[
            pl.BlockSpec((ROW_TILE, D_MODEL), lambda i: (jnp.minimum(i, NP_TILES - 1), 0)),
            pl.BlockSpec((ROW_TILE, D_MODEL), lambda i: (0, 0)),
            pl.BlockSpec((1, D_MODEL), lambda i: (0, 0)),
        ],
        out_specs=pl.BlockSpec((ROW_TILE, D_MODEL), lambda i: (i, 0)),
        out_shape=jax.ShapeDtypeStruct((N_ALL, D_MODEL), BF16),
        compiler_params=_params(1),
        name="norm_rows",
    )(xp, xs, g)


def _inproj_kernel(kind, tm, n_side, h_ref, w_ref, *rest):
    rest = list(rest)
    cos_ref, sin_ref = (rest.pop(0), rest.pop(0)) if kind == "rope" else (None, None)
    side_ref = rest.pop(0) if n_side else None
    z_ref = rest.pop(0)
    side_out_ref = rest.pop(0) if n_side else None
    j = pl.program_id(0)
    i = pl.program_id(1)
    last = N_P // tm

    if n_side:
        @pl.when(j * (last + 1) + i < n_side)
        def _():
            side_out_ref[...] = side_ref[...].astype(BF16)

    def tile(rows):
        acc = _dot(h_ref[0:rows, :], w_ref[...].astype(BF16))
        if kind == "rope":
            c = cos_ref[0:rows, :]
            s = sin_ref[0:rows, :]
            heads_per_block = IN_TN // RET_DK
            for hb in range(heads_per_block):
                scale = jnp.where(j * heads_per_block + hb >= RET_HEADS, RET_DK ** -0.5, 1.0).astype(F32)
                cols = slice(hb * RET_DK, (hb + 1) * RET_DK)
                a = acc[:, cols]
                r = pltpu.roll(a, RET_DK // 2, axis=1)
                z_ref[0:rows, cols] = ((a * c + r * s) * scale).astype(BF16)
        elif kind == "copy":
            z_ref[0:rows, :] = acc.astype(BF16)
        elif kind == "silu":
            z_ref[0:rows, :] = (acc * _sigmoid(acc)).astype(BF16)
        elif kind == "gelu":
            z_ref[0:rows, :] = jax.nn.gelu(acc).astype(BF16)
        else:
            z_ref[0:rows, :] = _sigmoid(acc).astype(BF16)

    @pl.when(i < last)
    def _():
        tile(tm)

    @pl.when(i == last)
    def _():
        tile(N_S)


def _inproj(kind, tm, col0, width, h, w_in, tables=(), side=None):
    last = N_P // tm
    n_j = width // IN_TN
    tab_idx = lambda j, i: (jnp.where(i < last, i % (SEQ // tm), SEQ // tm), 0)
    j0 = col0 // IN_TN
    in_specs = [
        pl.BlockSpec((tm, D_MODEL), lambda j, i: (i, 0)),
        pl.BlockSpec((D_MODEL, IN_TN), lambda j, i: (0, j0 + j)),
    ] + [pl.BlockSpec((tm, RET_DK), tab_idx) for _ in tables]
    out_specs = [pl.BlockSpec((tm, IN_TN), lambda j, i: (i, j))]
    out_shape = [jax.ShapeDtypeStruct((N_ALL, width), BF16)]
    n_side = 0 if side is None else side.shape[0]
    if n_side:
        assert n_side <= n_j * (last + 1)
        blk = (1,) + side.shape[1:]
        side_blk = lambda j, i: (jnp.minimum(j * (last + 1) + i, n_side - 1), 0, 0)
        in_specs.append(pl.BlockSpec(blk, side_blk))
        out_specs.append(pl.BlockSpec(blk, side_blk))
        out_shape.append(jax.ShapeDtypeStruct(side.shape, BF16))
    outs = pl.pallas_call(
        functools.partial(_inproj_kernel, kind, tm, n_side),
        grid=(n_j, last + 1),
        in_specs=in_specs,
        out_specs=out_specs,
        out_shape=out_shape,
        compiler_params=_params(2),
        name="in_proj_" + kind,
    )(h, w_in, *tables, *(() if side is None else (side,)))
    return outs if n_side else outs[0]


def _rope_tables(tm):
    half = RET_DK // 2
    inv = ROPE_BASE ** (-jnp.arange(half, dtype=F32) / half)

    def tab(pos):
        ang = pos.astype(F32)[:, None] * inv[None, :]
        c, s = jnp.cos(ang), jnp.sin(ang)
        return jnp.concatenate([c, c], -1), jnp.concatenate([-s, s], -1)

    cp, sp = tab(jnp.arange(SEQ, dtype=jnp.int32))
    cs, ss = tab(PAST_LEN + jnp.arange(DEC_SEQ, dtype=jnp.int32))
    cs = jnp.tile(cs, (DEC_BATCH, 1))
    ss = jnp.tile(ss, (DEC_BATCH, 1))
    pad = jnp.zeros((tm - N_S, RET_DK), F32)
    return jnp.concatenate([cp, cs, pad], 0), jnp.concatenate([sp, ss, pad], 0)


def _decay_tables(chunk):
    lg = jnp.log1p(-jnp.power(2.0, -5.0 - jnp.arange(RET_HEADS, dtype=F32)))
    idx = jnp.arange(chunk, dtype=F32)
    rel = idx[:, None] - idx[None, :]
    dmask = jnp.where(rel >= 0, jnp.exp(lg[:, None, None] * jnp.maximum(rel, 0.0)), 0.0).astype(F32)
    xi = jnp.exp(lg[:, None] * (idx[None, :] + 1.0)).astype(F32)
    zeta = jnp.exp(lg[:, None] * (chunk - 1.0 - idx[None, :])).astype(F32)
    gc = jnp.exp(lg * chunk).astype(F32)
    return dmask, xi, zeta, gc


def _head_norm(o, gn):
    mu = jnp.mean(o, axis=-1, keepdims=True)
    d = o - mu
    var = jnp.mean(d * d, axis=-1, keepdims=True)
    return ((d * lax.rsqrt(var + EPS)) * gn).astype(BF16)


def _ret_kernel(gcp_ref, gcs_ref, q_ref, k_ref, v_ref, gn_ref, dm_ref, xi_ref, zt_ref,
                qs_ref, ks_ref, vs_ref, dms_ref, xis_ref, zts_ref, s0_ref,
                a_ref, sfin_ref, as_ref, s1_ref, s_ref):
    c = pl.program_id(1)

    @pl.when(c == 0)
    def _():
        s_ref[...] = jnp.zeros_like(s_ref)

    rows_k = lax.broadcasted_iota(jnp.int32, (RET_ROWS, RET_DK), 0) >> LOG_DEC_SEQ
    rows_v = lax.broadcasted_iota(jnp.int32, (RET_ROWS, RET_DV), 0) >> LOG_DEC_SEQ
    for h in range(RET_HEADS):
        kc = slice(h * RET_DK, (h + 1) * RET_DK)
        vc = slice(h * RET_DV, (h + 1) * RET_DV)
        gn = gn_ref[:, vc]

        qh = q_ref[:, kc]
        kh = k_ref[:, kc]
        vh = v_ref[:, vc]
        inner = _dot_nt(qh, kh) * dm_ref[h]
        o = _dot(inner.astype(BF16), vh)
        s_old = s_ref[h]
        xi = xi_ref[h]
        o = o + _dot(qh, s_old.astype(BF16)) * jnp.concatenate([xi, xi], axis=1)
        kz = (kh.astype(F32) * zt_ref[h]).astype(BF16)
        s_ref[h] = gcp_ref[h] * s_old + _dot_tn(kz, vh)
        a_ref[:, vc] = _head_norm(o, gn)

        qh = qs_ref[:, kc]
        kh = ks_ref[:, kc]
        vh = vs_ref[:, vc]
        inner = _dot_nt(qh, kh) * dms_ref[h]
        o = _dot(inner.astype(BF16), vh)
        xi = xis_ref[h]
        xi2 = jnp.concatenate([xi, xi], axis=1)
        kz = kh.astype(F32) * zts_ref[h]
        gch = gcs_ref[h]
        for b in range(RET_BB):
            s_old = s0_ref[0, b, h]
            cross = _dot(qh, s_old.astype(BF16)) * xi2
            o = o + jnp.where(rows_v == b, cross, 0.0)
            kz_b = jnp.where(rows_k == b, kz, 0.0).astype(BF16)
            s1_ref[0, b, h] = gch * s_old + _dot_tn(kz_b, vh)
        as_ref[:, vc] = _head_norm(o, gn)

    @pl.when(c == pl.num_programs(1) - 1)
    def _():
        sfin_ref[0, 0] = s_ref[...]


def _retention(qk, v, ret_gn, state):
    chunk = PROMPT_CHUNK
    n_chunks = SEQ // chunk
    dmask, xi, zeta, gc = _decay_tables(chunk)
    xi_b = jnp.broadcast_to(xi[:, :, None], (RET_HEADS, chunk, RET_DK))
    zeta_b = jnp.broadcast_to(zeta[:, :, None], (RET_HEADS, chunk, RET_DK))
    dmask_s, xi_s, zeta_s, gc_s = _decay_tables(DEC_SEQ)
    eye = jnp.eye(RET_BB, dtype=F32)
    dm_big = jax.vmap(lambda m: jnp.kron(eye, m))(dmask_s)
    xi_sb = jnp.broadcast_to(jnp.tile(xi_s, (1, RET_BB))[:, :, None], (RET_HEADS, RET_ROWS, RET_DK))
    zeta_sb = jnp.broadcast_to(jnp.tile(zeta_s, (1, RET_BB))[:, :, None], (RET_HEADS, RET_ROWS, RET_DK))

    row = lambda b, c: b * n_chunks + c
    srow = lambda b, c: N_P // RET_ROWS + row(b, c)
    const3 = lambda b, c: (0, 0, 0)
    smem = pl.BlockSpec(memory_space=pltpu.SMEM)
    st_spec = pl.BlockSpec((1, RET_BB, RET_HEADS, RET_DK, RET_DV), lambda b, c: (0, row(b, c), 0, 0, 0))
    return pl.pallas_call(
        _ret_kernel,
        grid=(BATCH, n_chunks),
        in_specs=[
            smem, smem,
            pl.BlockSpec((chunk, RET_QK), lambda b, c: (row(b, c), 0)),
            pl.BlockSpec((chunk, RET_QK), lambda b, c: (row(b, c), 1)),
            pl.BlockSpec((chunk, RET_V), lambda b, c: (row(b, c), 0)),
            pl.BlockSpec((1, RET_V), lambda b, c: (0, 0)),
            pl.BlockSpec((RET_HEADS, chunk, chunk), const3),
            pl.BlockSpec((RET_HEADS, chunk, RET_DK), const3),
            pl.BlockSpec((RET_HEADS, chunk, RET_DK), const3),
            pl.BlockSpec((RET_ROWS, RET_QK), lambda b, c: (srow(b, c), 0)),
            pl.BlockSpec((RET_ROWS, RET_QK), lambda b, c: (srow(b, c), 1)),
            pl.BlockSpec((RET_ROWS, RET_V), lambda b, c: (srow(b, c), 0)),
            pl.BlockSpec((RET_HEADS, RET_ROWS, RET_ROWS), const3),
            pl.BlockSpec((RET_HEADS, RET_ROWS, RET_DK), const3),
            pl.BlockSpec((RET_HEADS, RET_ROWS, RET_DK), const3),
            st_spec,
        ],
        out_specs=[
            pl.BlockSpec((chunk, RET_V), lambda b, c: (row(b, c), 0)),
            pl.BlockSpec((1, 1, RET_HEADS, RET_DK, RET_DV), lambda b, c: (0, b, 0, 0, 0)),
            pl.BlockSpec((RET_ROWS, RET_V), lambda b, c: (row(b, c), 0)),
            st_spec,
        ],
        out_shape=[
            jax.ShapeDtypeStruct((N_P, RET_V), BF16),
            jax.ShapeDtypeStruct((1, BATCH, RET_HEADS, RET_DK, RET_DV), F32),
            jax.ShapeDtypeStruct((N_S, RET_V), BF16),
            jax.ShapeDtypeStruct((1, DEC_BATCH, RET_HEADS, RET_DK, RET_DV), F32),
        ],
        scratch_shapes=[pltpu.VMEM((RET_HEADS, RET_DK, RET_DV), F32)],
        compiler_params=_params(2),
        name="retention",
    )(gc, gc_s, qk, qk, v, ret_gn, dmask, xi_b, zeta_b, qk, qk, v, dm_big, xi_sb, zeta_sb, state)


def _sgate_kernel(u_ref, v_ref, lg_ref, lb_ref, wp_ref, bp_ref, ws_ref, bs_ref, b_ref, sgv_ref, vln_ref):
    i = pl.program_id(0)
    v = v_ref[...].astype(F32)
    mu = jnp.mean(v, axis=-1, keepdims=True)
    d = v - mu
    var = jnp.mean(d * d, axis=-1, keepdims=True)
    vln_ref[...] = (d * lax.rsqrt(var + EPS)) * lg_ref[...] + lb_ref[...]
    gw = SG_WIDTH // SG_GROUPS
    lane_reps = gw // 128

    @pl.when(i < NP_TILES)
    def _():
        r = lax.broadcasted_iota(jnp.int32, (SG_CHUNK, SG_CHUNK), 0)
        c = lax.broadcasted_iota(jnp.int32, (SG_CHUNK, SG_CHUNK), 1)
        for g in range(SG_GROUPS):
            cols = slice(g * gw, (g + 1) * gw)
            w = jnp.where(c <= r, wp_ref[g], 0.0).astype(BF16)
            bias = jnp.concatenate([bp_ref[g]] * lane_reps, axis=1)
            chunks = [slice(ch * SG_CHUNK, (ch + 1) * SG_CHUNK) for ch in range(ROW_TILE // SG_CHUNK)]
            mixed = [_dot(w, vln_ref[rows, cols].astype(BF16)) for rows in chunks]
            for rows, m in zip(chunks, mixed):
                b_ref[rows, cols] = (u_ref[rows, cols].astype(F32) * (m + bias)).astype(BF16)

    @pl.when(i == NP_TILES)
    def _():
        sgv_ref[...] = vln_ref[...]
        r = lax.broadcasted_iota(jnp.int32, (ROW_TILE, ROW_TILE), 0)
        c = lax.broadcasted_iota(jnp.int32, (ROW_TILE, ROW_TILE), 1)
        keep = ((r >> LOG_DEC_SEQ) == (c >> LOG_DEC_SEQ)) & (c <= r)
        for g in range(SG_GROUPS):
            cols = slice(g * gw, (g + 1) * gw)
            w_rows = jnp.concatenate([ws_ref[g]] * (ROW_TILE // 8), axis=0)
            w_full = jnp.concatenate([w_rows] * (ROW_TILE // 128), axis=1)
            w = jnp.where(keep, w_full, 0.0).astype(BF16)
            b_rows = jnp.concatenate([bs_ref[g]] * (ROW_TILE // 8), axis=0)
            bias = jnp.concatenate([b_rows] * lane_reps, axis=1)
            mixed = _dot(w, vln_ref[:, cols].astype(BF16)) + bias
            b_ref[:, cols] = (u_ref[:, cols].astype(F32) * mixed).astype(BF16)


def _sgate(uv, ln_g, ln_b, sg_ws, sg_bs):
    b_p = jnp.broadcast_to(sg_bs[:, :, None], (SG_GROUPS, SG_CHUNK, 128))
    w_s = jnp.tile(sg_ws[:, :DEC_SEQ, :DEC_SEQ], (1, 8 // DEC_SEQ, 128 // DEC_SEQ))
    b_s = jnp.broadcast_to(jnp.tile(sg_bs[:, :DEC_SEQ], (1, 8 // DEC_SEQ))[:, :, None], (SG_GROUPS, 8, 128))
    const3 = lambda i: (0, 0, 0)
    return pl.pallas_call(
        _sgate_kernel,
        grid=(N_TILES,),
        in_specs=[
            pl.BlockSpec((ROW_TILE, SG_WIDTH), lambda i: (i, 0)),
            pl.BlockSpec((ROW_TILE, SG_WIDTH), lambda i: (i, 1)),
            pl.BlockSpec((1, SG_WIDTH), lambda i: (0, 0)),
            pl.BlockSpec((1, SG_WIDTH), lambda i: (0, 0)),
            pl.BlockSpec((SG_GROUPS, SG_CHUNK, SG_CHUNK), const3),
            pl.BlockSpec((SG_GROUPS, SG_CHUNK, 128), const3),
            pl.BlockSpec((SG_GROUPS, 8, 128), const3),
            pl.BlockSpec((SG_GROUPS, 8, 128), const3),
        ],
        out_specs=[
            pl.BlockSpec((ROW_TILE, SG_WIDTH), lambda i: (i, 0)),
            pl.BlockSpec((N_S, SG_WIDTH), lambda i: (0, 0)),
        ],
        out_shape=[
            jax.ShapeDtypeStruct((N_ALL, SG_WIDTH), BF16),
            jax.ShapeDtypeStruct((N_S, SG_WIDTH), F32),
        ],
        scratch_shapes=[pltpu.VMEM((ROW_TILE, SG_WIDTH), F32)],
        compiler_params=_params(1),
        name="spatial_gate",
    )(uv, uv, ln_g, ln_b, sg_ws, b_p, w_s, b_s)


def _merge_kernel(ap_ref, as_ref, gs_ref, b_ref, ga_ref, gb_ref, xp_ref, xs_ref, wa_ref, wb_ref, wo_ref,
                  g_ref, wq_ref, x1_ref, q_ref):
    i = pl.program_id(0)

    def run(a_norm, x):
        b = _dot(b_ref[...], wb_ref[...])
        a_in = (gs_ref[...].astype(F32) * a_norm.astype(F32)).astype(BF16)
        a = _dot(a_in, wa_ref[...])
        merged = ga_ref[...].astype(F32) * a + gb_ref[...].astype(F32) * b
        x1 = x + _dot(merged.astype(BF16), wo_ref[...])
        x1_ref[...] = x1
        q_ref[...] = _dot(_rms(x1, g_ref[...]).astype(BF16), wq_ref[...]).astype(BF16)

    @pl.when(i < MERGE_P_TILES)
    def _():
        run(ap_ref[...], xp_ref[...])

    @pl.when(i >= MERGE_P_TILES)
    def _():
        run(as_ref[...], xs_ref[...])


def _resident(shape):
    return pl.BlockSpec(shape, lambda i: (0,) * len(shape), pipeline_mode=pl.Buffered(1))


def _merge(a_p, a_s, gs, b_all, gab, xp, xs, wa, wb, wo, g_xa, wq):
    tm = MERGE_TILE
    prompt_tile = lambda i: (jnp.minimum(i, MERGE_P_TILES - 1), 0)
    sample_tile = lambda i: (jnp.maximum(i - MERGE_P_TILES, 0), 0)
    return pl.pallas_call(
        _merge_kernel,
        grid=(N_ALL // tm,),
        in_specs=[
            pl.BlockSpec((tm, RET_V), prompt_tile),
            pl.BlockSpec((tm, RET_V), sample_tile),
            pl.BlockSpec((tm, RET_V), lambda i: (i, 0)),
            pl.BlockSpec((tm, SG_WIDTH), lambda i: (i, 0)),
            pl.BlockSpec((tm, D_MODEL), lambda i: (i, 0)),
            pl.BlockSpec((tm, D_MODEL), lambda i: (i, 1)),
            pl.BlockSpec((tm, D_MODEL), prompt_tile),
            pl.BlockSpec((tm, D_MODEL), sample_tile),
            _resident((RET_V, D_MODEL)),
            _resident((SG_WIDTH, D_MODEL)),
            _resident((D_MODEL, D_MODEL)),
            pl.BlockSpec((1, D_MODEL), lambda i: (0, 0)),
            _resident((D_MODEL, XA_W)),
        ],
        out_specs=[
            pl.BlockSpec((tm, D_MODEL), lambda i: (i, 0)),
            pl.BlockSpec((tm, XA_W), lambda i: (i, 0)),
        ],
        out_shape=[
            jax.ShapeDtypeStruct((N_ALL, D_MODEL), F32),
            jax.ShapeDtypeStruct((N_ALL, XA_W), BF16),
        ],
        compiler_params=_params(1),
        name="merge_proj",
    )(a_p, a_s, gs, b_all, gab, gab, xp, xs, wa, wb, wo, g_xa, wq)


def _memkv_kernel(m_ref, wk_ref, wv_ref, k_ref, v_ref, kb_ref, vb_ref):
    m = m_ref[...].astype(BF16)
    k = _dot(m, wk_ref[...].astype(BF16))
    v = _dot(m, wv_ref[...].astype(BF16))
    k_ref[...] = k
    v_ref[...] = v
    kb_ref[...] = k.astype(BF16)
    vb_ref[...] = v.astype(BF16)


def _memkv(mem, w_ck, w_cv):
    rows = BATCH * MEM_LEN
    spec = pl.BlockSpec((MEM_LEN, XA_W), lambda b: (b, 0))
    wspec = pl.BlockSpec((D_MODEL, XA_W), lambda b: (0, 0))
    return pl.pallas_call(
        _memkv_kernel,
        grid=(BATCH,),
        in_specs=[pl.BlockSpec((MEM_LEN, D_MODEL), lambda b: (b, 0)), wspec, wspec],
        out_specs=[spec, spec, spec, spec],
        out_shape=[jax.ShapeDtypeStruct((rows, XA_W), F32)] * 2 + [jax.ShapeDtypeStruct((rows, XA_W), BF16)] * 2,
        compiler_params=_params(1),
        name="mem_kv",
    )(mem, w_ck, w_cv)


def _softmax_rows(s):
    m = jnp.max(s, axis=-1, keepdims=True)
    e = jnp.exp(s - m)
    return e / jnp.sum(e, axis=-1, keepdims=True)


def _xattn_prompt_rows(q_ref, k_ref, v_ref, after_scores=lambda: None):
    cols = [slice(h * XA_DH, (h + 1) * XA_DH) for h in range(XA_HEADS)]
    scores = [_dot_nt(q_ref[:, c], k_ref[:, c]) * (XA_DH ** -0.5) for c in cols]
    after_scores()
    probs = [_softmax_rows(s).astype(BF16) for s in scores]
    heads = [_dot(p, v_ref[:, c]).astype(BF16) for p, c in zip(probs, cols)]
    return jnp.concatenate(heads, axis=1)


def _xattn_sample_scores(q_ref, k_ref):
    qf = q_ref[...].astype(F32)
    scores = []
    for b in range(SAMPLE_BB):
        rows = slice(b * DEC_SEQ, (b + 1) * DEC_SEQ)
        qb = jnp.concatenate([qf[rows, h * XA_DH:(h + 1) * XA_DH] for h in range(XA_HEADS)], axis=0)
        scores.append(_dot_nt(qb.astype(BF16), k_ref[b].astype(BF16)) * (XA_DH ** -0.5))
    return scores


def _xattn_sample_finish(scores, v_ref, o_ref):
    n_q = XA_HEADS * DEC_SEQ
    n_kv = MEM_LEN * XA_HEADS
    r = lax.broadcasted_iota(jnp.int32, (n_q, n_kv), 0)
    c = lax.broadcasted_iota(jnp.int32, (n_q, n_kv), 1)
    head_ok = (r >> LOG_DEC_SEQ) == (c & (XA_HEADS - 1))
    probs = [_softmax_rows(jnp.where(head_ok, s, -1e30)).astype(BF16) for s in scores]
    for b in range(SAMPLE_BB):
        rows = slice(b * DEC_SEQ, (b + 1) * DEC_SEQ)
        o = _dot(probs[b], v_ref[b].astype(BF16))
        for h in range(XA_HEADS):
            o_ref[h, rows, :] = o[h * DEC_SEQ:(h + 1) * DEC_SEQ, :]


def _route_kernel(qp_ref, mk_ref, mv_ref, qs_ref, ck_ref, cv_ref, x1_ref, wo_ref, g_ref, wr_ref, br_ref,
                  x2_ref, hm_ref, route_ref, cnt_ref, cnti_ref, run_ref, os_ref):
    i = pl.program_id(0)

    @pl.when(i == 0)
    def _():
        run_ref[...] = jnp.zeros_like(run_ref)

    def run(o, between=lambda: None):
        x2 = x1_ref[...] + _dot(o, wo_ref[...])
        x2_ref[...] = x2
        hm = _rms(x2, g_ref[...])
        hm_ref[...] = _to_row_tiles(hm)
        logits = _dot(hm.astype(BF16), wr_ref[...]) + br_ref[...]
        lane = lax.broadcasted_iota(jnp.int32, logits.shape, 1)
        lane_f = lane.astype(F32)
        neg = jnp.float32(-jnp.inf)
        big = jnp.float32(1 << 20)
        is_g = lane < MOE_GROUPS
        gl = jnp.where(is_g, logits, neg)
        gmax = jnp.max(gl, axis=-1, keepdims=True)
        g_sel = jnp.min(jnp.where(gl == gmax, lane_f, big), axis=-1, keepdims=True)
        g_w = 1.0 / jnp.sum(jnp.where(is_g, jnp.exp(logits - gmax), 0.0), axis=-1, keepdims=True)
        between()
        e_lane = lane - MOE_GROUPS
        e_group = (e_lane >> 3).astype(F32)
        in_grp = (e_lane >= 0) & (e_lane < MOE_EXPERTS) & (e_group == g_sel)
        el = jnp.where(in_grp, logits, neg)
        v0 = jnp.max(el, axis=-1, keepdims=True)
        i0 = jnp.min(jnp.where(el == v0, lane_f, big), axis=-1, keepdims=True)
        el1 = jnp.where(lane_f == i0, neg, el)
        v1 = jnp.max(el1, axis=-1, keepdims=True)
        i1 = jnp.min(jnp.where(el1 == v1, lane_f, big), axis=-1, keepdims=True)
        ex = jnp.exp(v1 - v0)
        den = 1.0 + ex
        w0 = (1.0 / den) * g_w
        w1 = (ex / den) * g_w
        a0 = (lane_f == i0).astype(F32)
        a1 = (lane_f == i1).astype(F32)
        a = a0 + a1
        rr = lax.broadcasted_iota(jnp.int32, (ROW_TILE, ROW_TILE), 0)
        cc = lax.broadcasted_iota(jnp.int32, (ROW_TILE, ROW_TILE), 1)
        lower = jnp.where(cc < rr, 1.0, 0.0).astype(BF16)
        before = _dot(lower, a.astype(BF16)) + run_ref[...]
        rank0 = jnp.sum(before * a0, axis=-1, keepdims=True)
        rank1 = jnp.sum(before * a1, axis=-1, keepdims=True)
        run_ref[...] += jnp.sum(a, axis=0, keepdims=True)
        e0 = i0 - MOE_GROUPS
        e1 = i1 - MOE_GROUPS
        route = jnp.where(lane == 0, e0, 0.0)
        route = jnp.where(lane == 1, e1, route)
        route = jnp.where(lane == 2, rank0, route)
        route = jnp.where(lane == 3, rank1, route)
        route = jnp.where(lane == 4, w0, route)
        route = jnp.where(lane == 5, w1, route)
        route_ref[...] = route
        cnt_ref[...] = run_ref[...]
        cnti_ref[...] = run_ref[...].astype(jnp.int32)

    @pl.when(i < NP_TILES)
    def _():
        scores = []
        o_prompt = _xattn_prompt_rows(qp_ref, mk_ref, mv_ref,
                                      after_scores=lambda: scores.extend(_xattn_sample_scores(qs_ref, ck_ref)))
        run(o_prompt, between=lambda: _xattn_sample_finish(scores, cv_ref, os_ref.at[i]))

    @pl.when(i == NP_TILES)
    def _():
        heads = [os_ref[:, h].reshape(N_S, XA_DH) for h in range(XA_HEADS)]
        run(jnp.concatenate(heads, axis=1).astype(BF16))


def _route(qx, mkb, mvb, ck, cv, x1, w_co, g_moe, w_r, b_r):
    prompt_step = lambda i: jnp.minimum(i, NP_TILES - 1)
    kv = pl.BlockSpec((SAMPLE_BB, MEM_LEN * XA_HEADS, XA_DH), lambda i: (prompt_step(i), 0, 0))
    mem_kv = pl.BlockSpec((MEM_LEN, XA_W), lambda i: (prompt_step(i) // (SEQ // ROW_TILE), 0))
    return pl.pallas_call(
        _route_kernel,
        grid=(N_TILES,),
        in_specs=[
            pl.BlockSpec((ROW_TILE, XA_W), lambda i: (prompt_step(i), 0)),
            mem_kv, mem_kv,
            pl.BlockSpec((SAMPLE_ROWS, XA_W), lambda i: (N_P // SAMPLE_ROWS + prompt_step(i), 0)),
            kv, kv,
            pl.BlockSpec((ROW_TILE, D_MODEL), lambda i: (i, 0)),
            _resident((XA_W, D_MODEL)),
            pl.BlockSpec((1, D_MODEL), lambda i: (0, 0)),
            _resident((D_MODEL, ROUTE_LANES)),
            pl.BlockSpec((1, ROUTE_LANES), lambda i: (0, 0)),
        ],
        out_specs=[
            pl.BlockSpec((ROW_TILE, D_MODEL), lambda i: (i, 0)),
            pl.BlockSpec((ROW_TILE, ROW_SUB, ROW_LANE), lambda i: (i, 0, 0)),
            pl.BlockSpec((ROW_TILE, ROUTE_LANES), lambda i: (i, 0)),
            pl.BlockSpec((1, ROUTE_LANES), lambda i: (0, 0)),
            pl.BlockSpec((1, ROUTE_LANES), lambda i: (0, 0)),
        ],
        out_shape=[
            jax.ShapeDtypeStruct((N_ALL, D_MODEL), F32),
            jax.ShapeDtypeStruct((N_ALL, ROW_SUB, ROW_LANE), BF16),
            jax.ShapeDtypeStruct((N_ALL, ROUTE_LANES), F32),
            jax.ShapeDtypeStruct((1, ROUTE_LANES), F32),
            jax.ShapeDtypeStruct((1, ROUTE_LANES), jnp.int32),
        ],
        scratch_shapes=[pltpu.VMEM((1, ROUTE_LANES), F32),
                        pltpu.VMEM((NP_TILES, XA_HEADS, SAMPLE_ROWS, XA_DH), F32)],
        compiler_params=_params(1),
        name="xa_out_route",
    )(qx, mkb, mvb, qx, ck, cv, x1, w_co, g_moe, w_r, b_r)


def _positions_kernel(route_ref, cnt_ref, pos_ref):
    route = route_ref[...]
    lane = lax.broadcasted_iota(jnp.int32, route.shape, 1)
    lane_f = lane.astype(F32)
    tiles = jnp.floor((cnt_ref[...] + (MOE_TM - 1)) * (1.0 / MOE_TM))
    lr = lax.broadcasted_iota(jnp.int32, (ROUTE_LANES, ROUTE_LANES), 0)
    lc = lax.broadcasted_iota(jnp.int32, (ROUTE_LANES, ROUTE_LANES), 1)
    before = jnp.where(lr < lc, 1.0, 0.0).astype(BF16)
    tiles8 = jnp.broadcast_to(tiles, (8, ROUTE_LANES)).astype(BF16)
    start = _dot(tiles8, before)[0:1, :] * MOE_TM

    def col(k):
        return jnp.sum(jnp.where(lane == k, route, 0.0), axis=-1, keepdims=True)

    def first_row(e):
        return jnp.sum(jnp.where(lane_f == e + MOE_GROUPS, start, 0.0), axis=-1, keepdims=True)

    p0 = first_row(col(0)) + col(2)
    p1 = first_row(col(1)) + col(3)
    p = jnp.where(lane == 0, p0, jnp.where(lane == 1, p1, 0.0))
    pos_ref[...] = p.T[0:8, :].astype(jnp.int32)


def _positions(route, counts):
    rows = N_ALL // 4
    return pl.pallas_call(
        _positions_kernel,
        grid=(4,),
        in_specs=[
            pl.BlockSpec((rows, ROUTE_LANES), lambda i: (i, 0)),
            pl.BlockSpec((1, ROUTE_LANES), lambda i: (0, 0)),
        ],
        out_specs=pl.BlockSpec((8, rows), lambda i: (0, i)),
        out_shape=jax.ShapeDtypeStruct((8, N_ALL), jnp.int32),
        compiler_params=_params(1),
        name="positions",
    )(route, counts)


def _dispatch_kernel(pos_ref, zrow_ref, zon_ref, nu_ref, hm_ref, xs_ref, zbuf, sem, zsem, tbuf, tsem):
    i = pl.program_id(0)

    def zero_tile(row):
        return pltpu.make_async_copy(zbuf, xs_ref.at[pl.ds(pl.multiple_of(row, MOE_TM), MOE_TM)], zsem)

    @pl.when(i == 0)
    def _():
        zbuf[...] = jnp.zeros_like(zbuf)
        for e in range(MOE_EXPERTS):
            @pl.when(zon_ref[e] > 0)
            def _():
                zero_tile(zrow_ref[e]).start()

        def start_tail(t, carry):
            zero_tile(t * MOE_TM).start()
            return carry

        def wait_tail(t, carry):
            zero_tile(t * MOE_TM).wait()
            return carry

        lax.fori_loop(nu_ref[0], MOE_NT, start_tail, 0)
        for e in range(MOE_EXPERTS):
            @pl.when(zon_ref[e] > 0)
            def _():
                zero_tile(zrow_ref[e]).wait()
        lax.fori_loop(nu_ref[0], MOE_NT, wait_tail, 0)

    n_steps = pl.num_programs(0)

    def tile_copy(t):
        return pltpu.make_async_copy(hm_ref.at[pl.ds(t * ROW_TILE, ROW_TILE)], tbuf.at[t % 3], tsem.at[t % 3])

    def wait_rows(t):
        for k in range(MOE_TOPK):
            pltpu.make_async_copy(tbuf.at[t % 3], xs_ref.at[pl.ds(0, ROW_TILE)], sem.at[t % 2, k]).wait()

    @pl.when(i == 0)
    def _():
        tile_copy(0).start()

    tile_copy(i).wait()

    @pl.when(i + 1 < n_steps)
    def _():
        tile_copy(i + 1).start()

    slot = i % 3
    par = i % 2
    base = i * ROW_TILE

    def start(r, carry):
        for k in range(MOE_TOPK):
            dst_row = pos_ref[k, base + r]
            pltpu.make_async_copy(tbuf.at[slot, r], xs_ref.at[dst_row], sem.at[par, k]).start(priority=k)
        return carry

    lax.fori_loop(0, ROW_TILE, start, 0, unroll=8)

    @pl.when(i > 0)
    def _():
        wait_rows(i - 1)

    @pl.when(i == n_steps - 1)
    def _():
        wait_rows(i)


def _dispatch(pos_t, zero_row, zero_on, n_used, hmw):
    grid_spec = pltpu.PrefetchScalarGridSpec(
        num_scalar_prefetch=4,
        grid=(N_TILES,),
        in_specs=[pl.BlockSpec(memory_space=pl.ANY)],
        out_specs=pl.BlockSpec(memory_space=pl.ANY),
        scratch_shapes=[pltpu.VMEM((MOE_TM, ROW_SUB, ROW_LANE), BF16),
                        pltpu.SemaphoreType.DMA((2, MOE_TOPK)), pltpu.SemaphoreType.DMA(()),
                        pltpu.VMEM((3, ROW_TILE, ROW_SUB, ROW_LANE), BF16), pltpu.SemaphoreType.DMA((3,))],
    )
    return pl.pallas_call(
        _dispatch_kernel,
        grid_spec=grid_spec,
        out_shape=jax.ShapeDtypeStruct((MOE_ROWS, ROW_SUB, ROW_LANE), BF16),
        compiler_params=_params(1),
        name="dispatch",
    )(pos_t, zero_row, zero_on, n_used, hmw)


def _expert_kernel(nu_ref, first_ref, ord_ref, oe_ref, no_ref, half_ref, x_ref, w1_hbm, w3_hbm, w2_hbm, y_ref,
                   w1b, w3b, w2b, sem):
    i = pl.program_id(0)
    n_used = nu_ref[0]
    n_ord = no_ref[0]

    def weight_copies(k):
        e = oe_ref[k]
        slot = k % W_SLOTS
        return (pltpu.make_async_copy(w1_hbm.at[e], w1b.at[slot], sem.at[0, slot]),
                pltpu.make_async_copy(w3_hbm.at[e], w3b.at[slot], sem.at[1, slot]),
                pltpu.make_async_copy(w2_hbm.at[e], w2b.at[slot], sem.at[2, slot]))

    def start_weights(k):
        for cp in weight_copies(k):
            cp.start(priority=1)

    @pl.when(i == 0)
    def _():
        for k in range(W_SLOTS - 1):
            @pl.when(k < n_ord)
            def _():
                start_weights(k)

    @pl.when(i < n_used)
    def _():
        k = ord_ref[i]

        @pl.when(first_ref[i] > 0)
        def _():
            for cp in weight_copies(k):
                cp.wait()

            @pl.when(k + (W_SLOTS - 1) < n_ord)
            def _():
                start_weights(k + (W_SLOTS - 1))

        slot = k % W_SLOTS

        def swiglu(rows):
            x = _from_row_tiles(x_ref[0:rows])
            h1 = _dot(x, w1b[slot].astype(BF16))
            h3 = _dot(x, w3b[slot].astype(BF16))
            he = (h1 * _sigmoid(h1) * h3).astype(BF16)
            y_ref[0:rows] = _to_row_tiles(_dot(he, w2b[slot].astype(BF16)))

        @pl.when(half_ref[i] == 0)
        def _():
            swiglu(MOE_TM)

        @pl.when(half_ref[i] > 0)
        def _():
            swiglu(MOE_TM // 2)
            y_ref[MOE_TM // 2:] = jnp.zeros((MOE_TM // 2, ROW_SUB, ROW_LANE), BF16)

    @pl.when(i >= n_used)
    def _():
        y_ref[...] = jnp.zeros_like(y_ref)


def _experts(sched, xs, w_e1, w_e3, w_e2):
    grid_spec = pltpu.PrefetchScalarGridSpec(
        num_scalar_prefetch=6,
        grid=(MOE_NT,),
        in_specs=[
            pl.BlockSpec((MOE_TM, ROW_SUB, ROW_LANE), lambda i, nu, *_: (jnp.minimum(i, nu[0] - 1), 0, 0)),
            pl.BlockSpec(memory_space=pl.ANY),
            pl.BlockSpec(memory_space=pl.ANY),
            pl.BlockSpec(memory_space=pl.ANY),
        ],
        out_specs=pl.BlockSpec((MOE_TM, ROW_SUB, ROW_LANE), lambda i, *_: (i, 0, 0)),
        scratch_shapes=[
            pltpu.VMEM((W_SLOTS, D_MODEL, MOE_FF), F32),
            pltpu.VMEM((W_SLOTS, D_MODEL, MOE_FF), F32),
            pltpu.VMEM((W_SLOTS, MOE_FF, D_MODEL), F32),
            pltpu.SemaphoreType.DMA((3, W_SLOTS)),
        ],
    )
    return pl.pallas_call(
        _expert_kernel,
        grid_spec=grid_spec,
        out_shape=jax.ShapeDtypeStruct((MOE_ROWS, ROW_SUB, ROW_LANE), BF16),
        compiler_params=_params(1),
        name="experts",
    )(sched["n_used"], sched["tile_first"], sched["tile_ord"], sched["ord_expert"], sched["n_ord"],
      sched["tile_half"], xs, w_e1, w_e3, w_e2)


def _combine_kernel(pos_ref, ys_ref, x2_ref, route_ref, g_ref, yp_ref, ysm_ref, gbuf, sem):
    i = pl.program_id(0)

    def start_gather(tile, slot):
        def body(r, carry):
            for k in range(MOE_TOPK):
                src_row = pos_ref[k, tile * ROW_TILE + r]
                pltpu.make_async_copy(ys_ref.at[src_row], gbuf.at[slot, k, r], sem.at[slot, k]).start(priority=k)
            return carry
        lax.fori_loop(0, ROW_TILE, body, 0, unroll=8)

    @pl.when(i == 0)
    def _():
        start_gather(0, 0)

    slot = i % 2

    @pl.when(i + 1 < pl.num_programs(0))
    def _():
        start_gather(i + 1, 1 - slot)

    for k in range(MOE_TOPK):
        pltpu.make_async_copy(ys_ref.at[pl.ds(0, ROW_TILE)], gbuf.at[slot, k], sem.at[slot, k]).wait()

    route = route_ref[...]
    lane = lax.broadcasted_iota(jnp.int32, route.shape, 1)
    w0 = jnp.sum(jnp.where(lane == 4, route, 0.0), axis=-1, keepdims=True)
    w1 = jnp.sum(jnp.where(lane == 5, route, 0.0), axis=-1, keepdims=True)
    g0 = _from_row_tiles(gbuf[slot, 0]).astype(F32)
    g1 = _from_row_tiles(gbuf[slot, 1]).astype(F32)
    x3 = x2_ref[...] + (g0 * w0 + g1 * w1)
    y = _rms(x3, g_ref[...])

    @pl.when(i < NP_TILES)
    def _():
        yp_ref[...] = y

    @pl.when(i == NP_TILES)
    def _():
        ysm_ref[...] = y


def _combine(pos, ys, x2, route, g_f):
    grid_spec = pltpu.PrefetchScalarGridSpec(
        num_scalar_prefetch=1,
        grid=(N_TILES,),
        in_specs=[
            pl.BlockSpec(memory_space=pl.ANY),
            pl.BlockSpec((ROW_TILE, D_MODEL), lambda i, pos: (i, 0)),
            pl.BlockSpec((ROW_TILE, ROUTE_LANES), lambda i, pos: (i, 0)),
            pl.BlockSpec((1, D_MODEL), lambda i, pos: (0, 0)),
        ],
        out_specs=[
            pl.BlockSpec((ROW_TILE, D_MODEL), lambda i, pos: (jnp.minimum(i, NP_TILES - 1), 0)),
            pl.BlockSpec((ROW_TILE, D_MODEL), lambda i, pos: (0, 0)),
        ],
        scratch_shapes=[pltpu.VMEM((2, MOE_TOPK, ROW_TILE, ROW_SUB, ROW_LANE), BF16),
                        pltpu.SemaphoreType.DMA((2, MOE_TOPK))],
    )
    return pl.pallas_call(
        _combine_kernel,
        grid_spec=grid_spec,
        out_shape=[
            jax.ShapeDtypeStruct((N_P, D_MODEL), F32),
            jax.ShapeDtypeStruct((N_S, D_MODEL), F32),
        ],
        compiler_params=_params(1),
        name="combine_norm",
    )(pos, ys, x2, route, g_f)


_SCHEDULE_FIELDS = ("n_used", "tile_first", "tile_ord", "tile_half", "ord_expert", "n_ord", "zero_row", "zero_on")


def _schedule_kernel(cnt_ref, nu_ref, first_ref, ord_ref, half_ref, oe_ref, no_ref, zrow_ref, zon_ref):
    i32 = jnp.int32

    def clear_tile(t, carry):
        first_ref[t] = i32(0)
        ord_ref[t] = i32(0)
        half_ref[t] = i32(0)
        return carry

    lax.fori_loop(0, MOE_NT, clear_tile, 0)

    def clear_expert(e, carry):
        oe_ref[e] = i32(0)
        return carry

    lax.fori_loop(0, MOE_EXPERTS, clear_expert, 0)

    def expert(e, carry):
        t0, k = carry
        c = cnt_ref[0, MOE_GROUPS + e]
        nt = lax.shift_right_logical(c + (MOE_TM - 1), i32(LOG_MOE_TM))
        used = jnp.where(nt > 0, i32(1), i32(0))
        zon_ref[e] = used
        zrow_ref[e] = jnp.maximum(t0 + nt - 1, 0) * MOE_TM

        @pl.when(nt > 0)
        def _():
            oe_ref[k] = e

        def tile(j, cc):
            t = t0 + j
            first_ref[t] = jnp.where(j == 0, i32(1), i32(0))
            ord_ref[t] = k
            half_ref[t] = jnp.where(c - j * MOE_TM <= MOE_TM // 2, i32(1), i32(0))
            return cc

        lax.fori_loop(0, nt, tile, 0)
        return t0 + nt, k + used

    n_tiles, n_experts = lax.fori_loop(0, MOE_EXPERTS, expert, (i32(0), i32(0)))
    nu_ref[0] = n_tiles
    no_ref[0] = n_experts


def _expert_schedule(counts_i32):
    smem = pl.BlockSpec(memory_space=pltpu.SMEM)
    sizes = (1, MOE_NT, MOE_NT, MOE_NT, MOE_EXPERTS, 1, MOE_EXPERTS, MOE_EXPERTS)
    outs = pl.pallas_call(
        _schedule_kernel,
        in_specs=[smem],
        out_specs=[smem] * len(sizes),
        out_shape=[jax.ShapeDtypeStruct((n,), jnp.int32) for n in sizes],
        name="expert_schedule",
    )(counts_i32)
    return dict(zip(_SCHEDULE_FIELDS, outs))


def kernel(x_prompt, x_sample, mem_prompt, state_ret, cache_mem_k, cache_mem_v, norm_mix, w_in, ret_gn,
           sg_ln_g, sg_ln_b, sg_ws, sg_bs, w_a_out, w_b_out, w_o, norm_xa, w_cq, w_ck, w_cv, w_co, norm_moe,
           w_rg, b_rg, w_re, b_re, w_e1, w_e3, w_e2, norm_f):
    xp = x_prompt.reshape(N_P, D_MODEL)
    xs = x_sample.reshape(N_S, D_MODEL)

    h = _norm_rows(xp, xs, norm_mix)
    w = w_in[0]
    blocks = lambda m: m.reshape(MERGE_W_BLOCKS, D_MODEL // MERGE_W_BLOCKS, D_MODEL)
    qk, wa = _inproj("rope", IN_TM_ROPE, 0, 2 * RET_QK, h, w, _rope_tables(IN_TM_ROPE), side=blocks(w_a_out[0]))
    v, wb = _inproj("copy", IN_TM, 2 * RET_QK, RET_V, h, w, side=blocks(w_b_out[0]))
    gs, wo = _inproj("silu", IN_TM, 2 * RET_QK + RET_V, RET_V, h, w, side=blocks(w_o[0]))
    uv = _inproj("gelu", IN_TM, 2 * RET_QK + 2 * RET_V, 2 * SG_WIDTH, h, w)
    gab = _inproj("sigmoid", IN_TM, 2 * RET_QK + 2 * RET_V + 2 * SG_WIDTH, 2 * D_MODEL, h, w)

    a_p, ret_p, a_s, ret_s = _retention(qk, v, ret_gn, state_ret)

    b_all, sgv = _sgate(uv, sg_ln_g, sg_ln_b, sg_ws[0], sg_bs[0])

    square = lambda m: m.reshape(D_MODEL, D_MODEL)
    x1, qx = _merge(a_p, a_s, gs, b_all, gab, xp, xs, square(wa), square(wb), square(wo), norm_xa,
                    w_cq[0].astype(BF16))

    mk, mv, mkb, mvb = _memkv(mem_prompt.reshape(BATCH * MEM_LEN, D_MODEL), w_ck[0], w_cv[0])
    ck = cache_mem_k.reshape(DEC_BATCH, MEM_LEN * XA_HEADS, XA_DH)
    cv = cache_mem_v.reshape(DEC_BATCH, MEM_LEN * XA_HEADS, XA_DH)

    pad = ROUTE_LANES - MOE_GROUPS - MOE_EXPERTS
    w_r = jnp.concatenate([w_rg[0], w_re[0], jnp.zeros((D_MODEL, pad), F32)], axis=1)
    b_r = jnp.concatenate([b_rg[0], b_re[0], jnp.zeros((pad,), F32)]).reshape(1, ROUTE_LANES)
    x2, hmw, route, counts, counts_i32 = _route(qx, mkb, mvb, ck, cv, x1, w_co[0].astype(BF16), norm_moe,
                                                w_r.astype(BF16), b_r)

    pos_t = _positions(route, counts)
    sched = _expert_schedule(counts_i32)
    xs_sorted = _dispatch(pos_t, sched["zero_row"], sched["zero_on"], sched["n_used"], hmw)
    ys = _experts(sched, xs_sorted, w_e1[0], w_e3[0], w_e2[0])
    y_p, y_s = _combine(pos_t, ys, x2, route, norm_f.reshape(1, D_MODEL))

    return (y_p.reshape(BATCH, SEQ, D_MODEL),
            y_s.reshape(DEC_BATCH, DEC_SEQ, D_MODEL),
            ret_p,
            mk.reshape(1, BATCH, MEM_LEN, XA_HEADS, XA_DH),
            mv.reshape(1, BATCH, MEM_LEN, XA_HEADS, XA_DH),
            ret_s,
            sgv.reshape(1, DEC_BATCH, DEC_SEQ, SG_WIDTH))
```

```python
import functools

import jax
import jax.numpy as jnp
from jax import lax
from jax.experimental import pallas as pl
from jax.experimental.pallas import tpu as pltpu

F32 = jnp.float32
BF16 = jnp.bfloat16

D_MODEL = 2048
BATCH = 4
SEQ = 2048
DEC_BATCH = 128
DEC_SEQ = 4
PAST_LEN = 16384
RET_HEADS = 8
RET_DK = 128
RET_DV = 256
RET_CHUNK = 128
ROPE_BASE = 10000.0
RET_QK = RET_HEADS * RET_DK
RET_V = RET_HEADS * RET_DV
SG_GROUPS = 4
SG_WIDTH = 2048
SG_CHUNK = 128
MEM_LEN = 256
XA_HEADS = 4
XA_DH = 128
XA_W = XA_HEADS * XA_DH
MOE_GROUPS = 4
MOE_PER_GROUP = 8
MOE_EXPERTS = MOE_GROUPS * MOE_PER_GROUP
MOE_TOPK = 2
MOE_FF = 512
EPS = 1e-6
IN_WIDTH = 2 * RET_QK + 2 * RET_V + 2 * SG_WIDTH + 2 * D_MODEL

N_P = BATCH * SEQ
N_S = DEC_BATCH * DEC_SEQ
N_ALL = N_P + N_S
ROW_TILE = 512
N_TILES = N_ALL // ROW_TILE
NP_TILES = N_P // ROW_TILE
MERGE_TILE = 256
MERGE_P_TILES = N_P // MERGE_TILE

IN_TM = 1024
IN_TM_ROPE = 1024
IN_TN = 2048
MERGE_W_BLOCKS = 4

PROMPT_CHUNK = 256
RET_BB = DEC_BATCH // (BATCH * (SEQ // PROMPT_CHUNK))
RET_ROWS = RET_BB * DEC_SEQ
LOG_DEC_SEQ = 2
LOG_SG_CHUNK = 7
SAMPLE_BB = DEC_BATCH // NP_TILES
SAMPLE_ROWS = SAMPLE_BB * DEC_SEQ

MOE_TM = 256
LOG_MOE_TM = 8
MOE_NT = (N_ALL * MOE_TOPK + MOE_EXPERTS * (MOE_TM - 1) + MOE_TM - 1) // MOE_TM
MOE_ROWS = MOE_NT * MOE_TM
ROUTE_LANES = 128
ROW_SUB, ROW_LANE = 16, 128
W_SLOTS = 3

VMEM_LIMIT = 56 * 1024 * 1024


def _params(n_axes, vmem=VMEM_LIMIT):
    return pltpu.CompilerParams(dimension_semantics=("arbitrary",) * n_axes,
                                vmem_limit_bytes=vmem)


def _rms(x, g):
    ms = jnp.mean(x * x, axis=-1, keepdims=True)
    return (x * lax.rsqrt(ms + EPS)) * g


def _dot(a, b):
    return jnp.dot(a, b, preferred_element_type=F32)


def _sigmoid(x):
    return 0.5 * jnp.tanh(0.5 * x) + 0.5


def _to_row_tiles(x):
    return x.astype(BF16).reshape(x.shape[0], ROW_SUB, ROW_LANE)


def _from_row_tiles(t):
    return t.reshape(t.shape[0], D_MODEL)


def _dot_nt(a, b):
    return lax.dot_general(a, b, (((1,), (1,)), ((), ())), preferred_element_type=F32)


def _dot_tn(a, b):
    return lax.dot_general(a, b, (((0,), (0,)), ((), ())), preferred_element_type=F32)


def _norm_kernel(xp_ref, xs_ref, g_ref, h_ref):
    i = pl.program_id(0)

    @pl.when(i < NP_TILES)
    def _():
        h_ref[...] = _rms(xp_ref[...], g_ref[...]).astype(BF16)

    @pl.when(i == NP_TILES)
    def _():
        h_ref[...] = _rms(xs_ref[...].reshape(N_S, D_MODEL), g_ref[...]).astype(BF16)


def _norm_rows(xp, xs, g):
    return pl.pallas_call(
        _norm_kernel,
        grid=(N_TILES,),
        in_specs=[
            pl.BlockSpec((ROW_TILE, D_MODEL), lambda i: (jnp.minimum(i, NP_TILES - 1), 0)),
            pl.BlockSpec((DEC_BATCH, DEC_SEQ, D_MODEL), lambda i: (0, 0, 0)),
            pl.BlockSpec((1, D_MODEL), lambda i: (0, 0)),
        ],
        out_specs=pl.BlockSpec((ROW_TILE, D_MODEL), lambda i: (i, 0)),
        out_shape=jax.ShapeDtypeStruct((N_ALL, D_MODEL), BF16),
        compiler_params=_params(1),
        name="norm_rows",
    )(xp, xs, g)


def _inproj_kernel(kind, tm, n_side, h_ref, w_ref, *rest):
    rest = list(rest)
    cos_ref, sin_ref = (rest.pop(0), rest.pop(0)) if kind == "rope" else (None, None)
    side_ref = rest.pop(0) if n_side else None
    z_ref = rest.pop(0)
    side_out_ref = rest.pop(0) if n_side else None
    j = pl.program_id(0)
    i = pl.program_id(1)
    last = N_P // tm

    if n_side:
        @pl.when(j * (last + 1) + i < n_side)
        def _():
            side_out_ref[...] = side_ref[...].astype(BF16)

    def tile(rows):
        acc = _dot(h_ref[0:rows, :], w_ref[...].astype(BF16))
        if kind == "rope":
            c = cos_ref[0:rows, :]
            s = sin_ref[0:rows, :]
            heads_per_block = IN_TN // RET_DK
            for hb in range(heads_per_block):
                scale = jnp.where(j * heads_per_block + hb >= RET_HEADS, RET_DK ** -0.5, 1.0).astype(F32)
                cols = slice(hb * RET_DK, (hb + 1) * RET_DK)
                a = acc[:, cols]
                r = pltpu.roll(a, RET_DK // 2, axis=1)
                z_ref[0:rows, cols] = ((a * c + r * s) * scale).astype(BF16)
        elif kind == "copy":
            z_ref[0:rows, :] = acc.astype(BF16)
        elif kind == "silu":
            z_ref[0:rows, :] = (acc * _sigmoid(acc)).astype(BF16)
        elif kind == "gelu":
            z_ref[0:rows, :] = jax.nn.gelu(acc).astype(BF16)
        else:
            z_ref[0:rows, :] = _sigmoid(acc).astype(BF16)

    @pl.when(i < last)
    def _():
        tile(tm)

    @pl.when(i == last)
    def _():
        tile(N_S)


def _inproj(kind, tm, col0, width, h, w_in, tables=(), side=None):
    last = N_P // tm
    n_j = width // IN_TN
    tab_idx = lambda j, i: (jnp.where(i < last, i % (SEQ // tm), SEQ // tm), 0)
    j0 = col0 // IN_TN
    in_specs = [
        pl.BlockSpec((tm, D_MODEL), lambda j, i: (i, 0)),
        pl.BlockSpec((D_MODEL, IN_TN), lambda j, i: (0, j0 + j)),
    ] + [pl.BlockSpec((tm, RET_DK), tab_idx) for _ in tables]
    out_specs = [pl.BlockSpec((tm, IN_TN), lambda j, i: (i, j))]
    out_shape = [jax.ShapeDtypeStruct((N_ALL, width), BF16)]
    n_side = 0 if side is None else side.shape[0]
    if n_side:
        assert n_side <= n_j * (last + 1)
        blk = (1,) + side.shape[1:]
        side_blk = lambda j, i: (jnp.minimum(j * (last + 1) + i, n_side - 1), 0, 0)
        in_specs.append(pl.BlockSpec(blk, side_blk))
        out_specs.append(pl.BlockSpec(blk, side_blk))
        out_shape.append(jax.ShapeDtypeStruct(side.shape, BF16))
    outs = pl.pallas_call(
        functools.partial(_inproj_kernel, kind, tm, n_side),
        grid=(n_j, last + 1),
        in_specs=in_specs,
        out_specs=out_specs,
        out_shape=out_shape,
        compiler_params=_params(2),
        name="in_proj_" + kind,
    )(h, w_in, *tables, *(() if side is None else (side,)))
    return outs if n_side else outs[0]


def _rope_tables(tm):
    half = RET_DK // 2
    inv = ROPE_BASE ** (-jnp.arange(half, dtype=F32) / half)

    def tab(pos):
        ang = pos.astype(F32)[:, None] * inv[None, :]
        c, s = jnp.cos(ang), jnp.sin(ang)
        return jnp.concatenate([c, c], -1), jnp.concatenate([-s, s], -1)

    cp, sp = tab(jnp.arange(SEQ, dtype=jnp.int32))
    cs, ss = tab(PAST_LEN + jnp.arange(DEC_SEQ, dtype=jnp.int32))
    cs = jnp.tile(cs, (DEC_BATCH, 1))
    ss = jnp.tile(ss, (DEC_BATCH, 1))
    pad = jnp.zeros((tm - N_S, RET_DK), F32)
    return jnp.concatenate([cp, cs, pad], 0), jnp.concatenate([sp, ss, pad], 0)


def _decay_tables(chunk):
    lg = jnp.log1p(-jnp.power(2.0, -5.0 - jnp.arange(RET_HEADS, dtype=F32)))
    idx = jnp.arange(chunk, dtype=F32)
    rel = idx[:, None] - idx[None, :]
    dmask = jnp.where(rel >= 0, jnp.exp(lg[:, None, None] * jnp.maximum(rel, 0.0)), 0.0).astype(F32)
    xi = jnp.exp(lg[:, None] * (idx[None, :] + 1.0)).astype(F32)
    zeta = jnp.exp(lg[:, None] * (chunk - 1.0 - idx[None, :])).astype(F32)
    gc = jnp.exp(lg * chunk).astype(F32)
    return dmask, xi, zeta, gc


def _head_norm(o, gn):
    mu = jnp.mean(o, axis=-1, keepdims=True)
    d = o - mu
    var = jnp.mean(d * d, axis=-1, keepdims=True)
    return ((d * lax.rsqrt(var + EPS)) * gn).astype(BF16)


def _ret_kernel(gcp_ref, gcs_ref, q_ref, k_ref, v_ref, gn_ref, dm_ref, xi_ref, zt_ref,
                qs_ref, ks_ref, vs_ref, dms_ref, xis_ref, zts_ref, s0_ref,
                a_ref, sfin_ref, as_ref, s1_ref, s_ref):
    c = pl.program_id(1)

    @pl.when(c == 0)
    def _():
        s_ref[...] = jnp.zeros_like(s_ref)

    rows_k = lax.broadcasted_iota(jnp.int32, (RET_ROWS, RET_DK), 0) >> LOG_DEC_SEQ
    rows_v = lax.broadcasted_iota(jnp.int32, (RET_ROWS, RET_DV), 0) >> LOG_DEC_SEQ
    for h in range(RET_HEADS):
        kc = slice(h * RET_DK, (h + 1) * RET_DK)
        vc = slice(h * RET_DV, (h + 1) * RET_DV)
        gn = gn_ref[:, vc]

        qh = q_ref[:, kc]
        kh = k_ref[:, kc]
        vh = v_ref[:, vc]
        inner = _dot_nt(qh, kh) * dm_ref[h]
        o = _dot(inner.astype(BF16), vh)
        s_old = s_ref[h]
        xi = xi_ref[h]
        o = o + _dot(qh, s_old.astype(BF16)) * jnp.concatenate([xi, xi], axis=1)
        kz = (kh.astype(F32) * zt_ref[h]).astype(BF16)
        s_ref[h] = gcp_ref[h] * s_old + _dot_tn(kz, vh)
        a_ref[:, vc] = _head_norm(o, gn)

        qh = qs_ref[:, kc]
        kh = ks_ref[:, kc]
        vh = vs_ref[:, vc]
        inner = _dot_nt(qh, kh) * dms_ref[h]
        o = _dot(inner.astype(BF16), vh)
        xi = xis_ref[h]
        xi2 = jnp.concatenate([xi, xi], axis=1)
        kz = kh.astype(F32) * zts_ref[h]
        gch = gcs_ref[h]
        for b in range(RET_BB):
            s_old = s0_ref[0, b, h]
            cross = _dot(qh, s_old.astype(BF16)) * xi2
            o = o + jnp.where(rows_v == b, cross, 0.0)
            kz_b = jnp.where(rows_k == b, kz, 0.0).astype(BF16)
            s1_ref[0, b, h] = gch * s_old + _dot_tn(kz_b, vh)
        as_ref[:, vc] = _head_norm(o, gn)

    @pl.when(c == pl.num_programs(1) - 1)
    def _():
        sfin_ref[0, 0] = s_ref[...]


def _retention(qk, v, ret_gn, state):
    chunk = PROMPT_CHUNK
    n_chunks = SEQ // chunk
    dmask, xi, zeta, gc = _decay_tables(chunk)
    xi_b = jnp.broadcast_to(xi[:, :, None], (RET_HEADS, chunk, RET_DK))
    zeta_b = jnp.broadcast_to(zeta[:, :, None], (RET_HEADS, chunk, RET_DK))
    dmask_s, xi_s, zeta_s, gc_s = _decay_tables(DEC_SEQ)
    eye = jnp.eye(RET_BB, dtype=F32)
    dm_big = jax.vmap(lambda m: jnp.kron(eye, m))(dmask_s)
    xi_sb = jnp.broadcast_to(jnp.tile(xi_s, (1, RET_BB))[:, :, None], (RET_HEADS, RET_ROWS, RET_DK))
    zeta_sb = jnp.broadcast_to(jnp.tile(zeta_s, (1, RET_BB))[:, :, None], (RET_HEADS, RET_ROWS, RET_DK))

    row = lambda b, c: b * n_chunks + c
    srow = lambda b, c: N_P // RET_ROWS + row(b, c)
    const3 = lambda b, c: (0, 0, 0)
    smem = pl.BlockSpec(memory_space=pltpu.SMEM)
    st_spec = pl.BlockSpec((1, RET_BB, RET_HEADS, RET_DK, RET_DV), lambda b, c: (0, row(b, c), 0, 0, 0))
    return pl.pallas_call(
        _ret_kernel,
        grid=(BATCH, n_chunks),
        in_specs=[
            smem, smem,
            pl.BlockSpec((chunk, RET_QK), lambda b, c: (row(b, c), 0)),
            pl.BlockSpec((chunk, RET_QK), lambda b, c: (row(b, c), 1)),
            pl.BlockSpec((chunk, RET_V), lambda b, c: (row(b, c), 0)),
            pl.BlockSpec((1, RET_V), lambda b, c: (0, 0)),
            pl.BlockSpec((RET_HEADS, chunk, chunk), const3),
            pl.BlockSpec((RET_HEADS, chunk, RET_DK), const3),
            pl.BlockSpec((RET_HEADS, chunk, RET_DK), const3),
            pl.BlockSpec((RET_ROWS, RET_QK), lambda b, c: (srow(b, c), 0)),
            pl.BlockSpec((RET_ROWS, RET_QK), lambda b, c: (srow(b, c), 1)),
            pl.BlockSpec((RET_ROWS, RET_V), lambda b, c: (srow(b, c), 0)),
            pl.BlockSpec((RET_HEADS, RET_ROWS, RET_ROWS), const3),
            pl.BlockSpec((RET_HEADS, RET_ROWS, RET_DK), const3),
            pl.BlockSpec((RET_HEADS, RET_ROWS, RET_DK), const3),
            st_spec,
        ],
        out_specs=[
            pl.BlockSpec((chunk, RET_V), lambda b, c: (row(b, c), 0)),
            pl.BlockSpec((1, 1, RET_HEADS, RET_DK, RET_DV), lambda b, c: (0, b, 0, 0, 0)),
            pl.BlockSpec((RET_ROWS, RET_V), lambda b, c: (row(b, c), 0)),
            st_spec,
        ],
        out_shape=[
            jax.ShapeDtypeStruct((N_P, RET_V), BF16),
            jax.ShapeDtypeStruct((1, BATCH, RET_HEADS, RET_DK, RET_DV), F32),
            jax.ShapeDtypeStruct((N_S, RET_V), BF16),
            jax.ShapeDtypeStruct((1, DEC_BATCH, RET_HEADS, RET_DK, RET_DV), F32),
        ],
        scratch_shapes=[pltpu.VMEM((RET_HEADS, RET_DK, RET_DV), F32)],
        compiler_params=_params(2),
        name="retention",
    )(gc, gc_s, qk, qk, v, ret_gn, dmask, xi_b, zeta_b, qk, qk, v, dm_big, xi_sb, zeta_sb, state)


def _sgate_kernel(u_ref, v_ref, lg_ref, lb_ref, wp_ref, bp_ref, ws_ref, bs_ref, b_ref, sgv_ref, vln_ref):
    i = pl.program_id(0)
    v = v_ref[...].astype(F32)
    mu = jnp.mean(v, axis=-1, keepdims=True)
    d = v - mu
    var = jnp.mean(d * d, axis=-1, keepdims=True)
    vln_ref[...] = (d * lax.rsqrt(var + EPS)) * lg_ref[...] + lb_ref[...]
    gw = SG_WIDTH // SG_GROUPS
    lane_reps = gw // 128

    @pl.when(i < NP_TILES)
    def _():
        r = lax.broadcasted_iota(jnp.int32, (SG_CHUNK, SG_CHUNK), 0)
        c = lax.broadcasted_iota(jnp.int32, (SG_CHUNK, SG_CHUNK), 1)
        for g in range(SG_GROUPS):
            cols = slice(g * gw, (g + 1) * gw)
            w = jnp.where(c <= r, wp_ref[g], 0.0).astype(BF16)
            bias = jnp.concatenate([bp_ref[g]] * lane_reps, axis=1)
            chunks = [slice(ch * SG_CHUNK, (ch + 1) * SG_CHUNK) for ch in range(ROW_TILE // SG_CHUNK)]
            mixed = [_dot(w, vln_ref[rows, cols].astype(BF16)) for rows in chunks]
            for rows, m in zip(chunks, mixed):
                b_ref[rows, cols] = (u_ref[rows, cols].astype(F32) * (m + bias)).astype(BF16)

    @pl.when(i == NP_TILES)
    def _():
        sgv_ref[...] = vln_ref[...].reshape(DEC_BATCH, DEC_SEQ, SG_WIDTH)
        r = lax.broadcasted_iota(jnp.int32, (ROW_TILE, ROW_TILE), 0)
        c = lax.broadcasted_iota(jnp.int32, (ROW_TILE, ROW_TILE), 1)
        keep = ((r >> LOG_DEC_SEQ) == (c >> LOG_DEC_SEQ)) & (c <= r)
        for g in range(SG_GROUPS):
            cols = slice(g * gw, (g + 1) * gw)
            w_rows = jnp.concatenate([ws_ref[g]] * (ROW_TILE // 8), axis=0)
            w_full = jnp.concatenate([w_rows] * (ROW_TILE // 128), axis=1)
            w = jnp.where(keep, w_full, 0.0).astype(BF16)
            b_rows = jnp.concatenate([bs_ref[g]] * (ROW_TILE // 8), axis=0)
            bias = jnp.concatenate([b_rows] * lane_reps, axis=1)
            mixed = _dot(w, vln_ref[:, cols].astype(BF16)) + bias
            b_ref[:, cols] = (u_ref[:, cols].astype(F32) * mixed).astype(BF16)


def _sgate(uv, ln_g, ln_b, sg_ws, sg_bs):
    b_p = jnp.broadcast_to(sg_bs[:, :, None], (SG_GROUPS, SG_CHUNK, 128))
    w_s = jnp.tile(sg_ws[:, :DEC_SEQ, :DEC_SEQ], (1, 8 // DEC_SEQ, 128 // DEC_SEQ))
    b_s = jnp.broadcast_to(jnp.tile(sg_bs[:, :DEC_SEQ], (1, 8 // DEC_SEQ))[:, :, None], (SG_GROUPS, 8, 128))
    const3 = lambda i: (0, 0, 0)
    return pl.pallas_call(
        _sgate_kernel,
        grid=(N_TILES,),
        in_specs=[
            pl.BlockSpec((ROW_TILE, SG_WIDTH), lambda i: (i, 0)),
            pl.BlockSpec((ROW_TILE, SG_WIDTH), lambda i: (i, 1)),
            pl.BlockSpec((1, SG_WIDTH), lambda i: (0, 0)),
            pl.BlockSpec((1, SG_WIDTH), lambda i: (0, 0)),
            pl.BlockSpec((SG_GROUPS, SG_CHUNK, SG_CHUNK), const3),
            pl.BlockSpec((SG_GROUPS, SG_CHUNK, 128), const3),
            pl.BlockSpec((SG_GROUPS, 8, 128), const3),
            pl.BlockSpec((SG_GROUPS, 8, 128), const3),
        ],
        out_specs=[
            pl.BlockSpec((ROW_TILE, SG_WIDTH), lambda i: (i, 0)),
            pl.BlockSpec((DEC_BATCH, DEC_SEQ, SG_WIDTH), lambda i: (0, 0, 0)),
        ],
        out_shape=[
            jax.ShapeDtypeStruct((N_ALL, SG_WIDTH), BF16),
            jax.ShapeDtypeStruct((DEC_BATCH, DEC_SEQ, SG_WIDTH), F32),
        ],
        scratch_shapes=[pltpu.VMEM((ROW_TILE, SG_WIDTH), F32)],
        compiler_params=_params(1),
        name="spatial_gate",
    )(uv, uv, ln_g, ln_b, sg_ws, b_p, w_s, b_s)


def _merge_kernel(ap_ref, as_ref, gs_ref, b_ref, ga_ref, gb_ref, xp_ref, xs_ref, wa_ref, wb_ref, wo_ref,
                  g_ref, wq_ref, x1_ref, q_ref):
    i = pl.program_id(0)

    def run(a_norm, x):
        b = _dot(b_ref[...], wb_ref[...])
        a_in = (gs_ref[...].astype(F32) * a_norm.astype(F32)).astype(BF16)
        a = _dot(a_in, wa_ref[...])
        merged = ga_ref[...].astype(F32) * a + gb_ref[...].astype(F32) * b
        x1 = x + _dot(merged.astype(BF16), wo_ref[...])
        x1_ref[...] = x1
        q_ref[...] = _dot(_rms(x1, g_ref[...]).astype(BF16), wq_ref[...]).astype(BF16)

    @pl.when(i < MERGE_P_TILES)
    def _():
        run(ap_ref[...], xp_ref[...])

    @pl.when(i >= MERGE_P_TILES)
    def _():
        run(as_ref[...], xs_ref[...].reshape(MERGE_TILE, D_MODEL))


def _resident(shape):
    return pl.BlockSpec(shape, lambda i: (0,) * len(shape), pipeline_mode=pl.Buffered(1))


def _merge(a_p, a_s, gs, b_all, gab, xp, xs, wa, wb, wo, g_xa, wq):
    tm = MERGE_TILE
    prompt_tile = lambda i: (jnp.minimum(i, MERGE_P_TILES - 1), 0)
    sample_tile = lambda i: (jnp.maximum(i - MERGE_P_TILES, 0), 0)
    return pl.pallas_call(
        _merge_kernel,
        grid=(N_ALL // tm,),
        in_specs=[
            pl.BlockSpec((tm, RET_V), prompt_tile),
            pl.BlockSpec((tm, RET_V), sample_tile),
            pl.BlockSpec((tm, RET_V), lambda i: (i, 0)),
            pl.BlockSpec((tm, SG_WIDTH), lambda i: (i, 0)),
            pl.BlockSpec((tm, D_MODEL), lambda i: (i, 0)),
            pl.BlockSpec((tm, D_MODEL), lambda i: (i, 1)),
            pl.BlockSpec((tm, D_MODEL), prompt_tile),
            pl.BlockSpec((tm // DEC_SEQ, DEC_SEQ, D_MODEL), lambda i: (jnp.maximum(i - MERGE_P_TILES, 0), 0, 0)),
            _resident((RET_V, D_MODEL)),
            _resident((SG_WIDTH, D_MODEL)),
            _resident((D_MODEL, D_MODEL)),
            pl.BlockSpec((1, D_MODEL), lambda i: (0, 0)),
            _resident((D_MODEL, XA_W)),
        ],
        out_specs=[
            pl.BlockSpec((tm, D_MODEL), lambda i: (i, 0)),
            pl.BlockSpec((tm, XA_W), lambda i: (i, 0)),
        ],
        out_shape=[
            jax.ShapeDtypeStruct((N_ALL, D_MODEL), F32),
            jax.ShapeDtypeStruct((N_ALL, XA_W), BF16),
        ],
        compiler_params=_params(1),
        name="merge_proj",
    )(a_p, a_s, gs, b_all, gab, gab, xp, xs, wa, wb, wo, g_xa, wq)


def _memkv_kernel(m_ref, wk_ref, wv_ref, k_ref, v_ref, kb_ref, vb_ref):
    m = m_ref[...].astype(BF16)
    k = _dot(m, wk_ref[...].astype(BF16))
    v = _dot(m, wv_ref[...].astype(BF16))
    k_ref[0, 0] = k.reshape(MEM_LEN, XA_HEADS, XA_DH)
    v_ref[0, 0] = v.reshape(MEM_LEN, XA_HEADS, XA_DH)
    kb_ref[...] = k.astype(BF16)
    vb_ref[...] = v.astype(BF16)


def _memkv(mem, w_ck, w_cv):
    rows = BATCH * MEM_LEN
    spec = pl.BlockSpec((MEM_LEN, XA_W), lambda b: (b, 0))
    spec5 = pl.BlockSpec((1, 1, MEM_LEN, XA_HEADS, XA_DH), lambda b: (0, b, 0, 0, 0))
    shape5 = jax.ShapeDtypeStruct((1, BATCH, MEM_LEN, XA_HEADS, XA_DH), F32)
    wspec = pl.BlockSpec((D_MODEL, XA_W), lambda b: (0, 0))
    return pl.pallas_call(
        _memkv_kernel,
        grid=(BATCH,),
        in_specs=[pl.BlockSpec((MEM_LEN, D_MODEL), lambda b: (b, 0)), wspec, wspec],
        out_specs=[spec5, spec5, spec, spec],
        out_shape=[shape5, shape5] + [jax.ShapeDtypeStruct((rows, XA_W), BF16)] * 2,
        compiler_params=_params(1),
        name="mem_kv",
    )(mem, w_ck, w_cv)


def _softmax_rows(s):
    m = jnp.max(s, axis=-1, keepdims=True)
    e = jnp.exp(s - m)
    return e / jnp.sum(e, axis=-1, keepdims=True)


def _xattn_prompt_rows(q_ref, k_ref, v_ref, after_scores=lambda: None):
    cols = [slice(h * XA_DH, (h + 1) * XA_DH) for h in range(XA_HEADS)]
    scores = [_dot_nt(q_ref[:, c], k_ref[:, c]) * (XA_DH ** -0.5) for c in cols]
    after_scores()
    probs = [_softmax_rows(s).astype(BF16) for s in scores]
    heads = [_dot(p, v_ref[:, c]).astype(BF16) for p, c in zip(probs, cols)]
    return jnp.concatenate(heads, axis=1)


def _xattn_sample_scores(q_ref, k_ref):
    qf = q_ref[...].astype(F32)
    scores = []
    for b in range(SAMPLE_BB):
        rows = slice(b * DEC_SEQ, (b + 1) * DEC_SEQ)
        qb = jnp.concatenate([qf[rows, h * XA_DH:(h + 1) * XA_DH] for h in range(XA_HEADS)], axis=0)
        scores.append(_dot_nt(qb.astype(BF16), k_ref[b].astype(BF16)) * (XA_DH ** -0.5))
    return scores


def _xattn_sample_finish(scores, v_ref, o_ref):
    n_q = XA_HEADS * DEC_SEQ
    n_kv = MEM_LEN * XA_HEADS
    r = lax.broadcasted_iota(jnp.int32, (n_q, n_kv), 0)
    c = lax.broadcasted_iota(jnp.int32, (n_q, n_kv), 1)
    head_ok = (r >> LOG_DEC_SEQ) == (c & (XA_HEADS - 1))
    probs = [_softmax_rows(jnp.where(head_ok, s, -1e30)).astype(BF16) for s in scores]
    for b in range(SAMPLE_BB):
        rows = slice(b * DEC_SEQ, (b + 1) * DEC_SEQ)
        o = _dot(probs[b], v_ref[b].astype(BF16))
        for h in range(XA_HEADS):
            o_ref[h, rows, :] = o[h * DEC_SEQ:(h + 1) * DEC_SEQ, :]


def _route_kernel(qp_ref, mk_ref, mv_ref, qs_ref, ck_ref, cv_ref, x1_ref, wo_ref, g_ref, wr_ref, br_ref,
                  x2_ref, hm_ref, route_ref, cnt_ref, cnti_ref, run_ref, os_ref):
    i = pl.program_id(0)

    @pl.when(i == 0)
    def _():
        run_ref[...] = jnp.zeros_like(run_ref)

    def run(o, between=lambda: None):
        x2 = x1_ref[...] + _dot(o, wo_ref[...])
        x2_ref[...] = x2
        hm = _rms(x2, g_ref[...])
        hm_ref[...] = _to_row_tiles(hm)
        logits = _dot(hm.astype(BF16), wr_ref[...]) + br_ref[...]
        lane = lax.broadcasted_iota(jnp.int32, logits.shape, 1)
        lane_f = lane.astype(F32)
        neg = jnp.float32(-jnp.inf)
        big = jnp.float32(1 << 20)
        is_g = lane < MOE_GROUPS
        gl = jnp.where(is_g, logits, neg)
        gmax = jnp.max(gl, axis=-1, keepdims=True)
        g_sel = jnp.min(jnp.where(gl == gmax, lane_f, big), axis=-1, keepdims=True)
        g_w = 1.0 / jnp.sum(jnp.where(is_g, jnp.exp(logits - gmax), 0.0), axis=-1, keepdims=True)
        between()
        e_lane = lane - MOE_GROUPS
        e_group = (e_lane >> 3).astype(F32)
        in_grp = (e_lane >= 0) & (e_lane < MOE_EXPERTS) & (e_group == g_sel)
        el = jnp.where(in_grp, logits, neg)
        v0 = jnp.max(el, axis=-1, keepdims=True)
        i0 = jnp.min(jnp.where(el == v0, lane_f, big), axis=-1, keepdims=True)
        el1 = jnp.where(lane_f == i0, neg, el)
        v1 = jnp.max(el1, axis=-1, keepdims=True)
        i1 = jnp.min(jnp.where(el1 == v1, lane_f, big), axis=-1, keepdims=True)
        ex = jnp.exp(v1 - v0)
        den = 1.0 + ex
        w0 = (1.0 / den) * g_w
        w1 = (ex / den) * g_w
        a0 = (lane_f == i0).astype(F32)
        a1 = (lane_f == i1).astype(F32)
        a = a0 + a1
        rr = lax.broadcasted_iota(jnp.int32, (ROW_TILE, ROW_TILE), 0)
        cc = lax.broadcasted_iota(jnp.int32, (ROW_TILE, ROW_TILE), 1)
        lower = jnp.where(cc < rr, 1.0, 0.0).astype(BF16)
        before = _dot(lower, a.astype(BF16)) + run_ref[...]
        rank0 = jnp.sum(before * a0, axis=-1, keepdims=True)
        rank1 = jnp.sum(before * a1, axis=-1, keepdims=True)
        run_ref[...] += jnp.sum(a, axis=0, keepdims=True)
        e0 = i0 - MOE_GROUPS
        e1 = i1 - MOE_GROUPS
        route = jnp.where(lane == 0, e0, 0.0)
        route = jnp.where(lane == 1, e1, route)
        route = jnp.where(lane == 2, rank0, route)
        route = jnp.where(lane == 3, rank1, route)
        route = jnp.where(lane == 4, w0, route)
        route = jnp.where(lane == 5, w1, route)
        route_ref[...] = route
        cnt_ref[...] = run_ref[...]
        cnti_ref[...] = run_ref[...].astype(jnp.int32)

    @pl.when(i < NP_TILES)
    def _():
        scores = []
        o_prompt = _xattn_prompt_rows(qp_ref, mk_ref, mv_ref,
                                      after_scores=lambda: scores.extend(_xattn_sample_scores(qs_ref, ck_ref)))
        run(o_prompt, between=lambda: _xattn_sample_finish(scores, cv_ref, os_ref.at[i]))

    @pl.when(i == NP_TILES)
    def _():
        heads = [os_ref[:, h].reshape(N_S, XA_DH) for h in range(XA_HEADS)]
        run(jnp.concatenate(heads, axis=1).astype(BF16))


def _route(qx, mkb, mvb, ck, cv, x1, w_co, g_moe, w_r, b_r):
    prompt_step = lambda i: jnp.minimum(i, NP_TILES - 1)
    kv = pl.BlockSpec((SAMPLE_BB, MEM_LEN * XA_HEADS, XA_DH), lambda i: (prompt_step(i), 0, 0))
    mem_kv = pl.BlockSpec((MEM_LEN, XA_W), lambda i: (prompt_step(i) // (SEQ // ROW_TILE), 0))
    return pl.pallas_call(
        _route_kernel,
        grid=(N_TILES,),
        in_specs=[
            pl.BlockSpec((ROW_TILE, XA_W), lambda i: (prompt_step(i), 0)),
            mem_kv, mem_kv,
            pl.BlockSpec((SAMPLE_ROWS, XA_W), lambda i: (N_P // SAMPLE_ROWS + prompt_step(i), 0)),
            kv, kv,
            pl.BlockSpec((ROW_TILE, D_MODEL), lambda i: (i, 0)),
            _resident((XA_W, D_MODEL)),
            pl.BlockSpec((1, D_MODEL), lambda i: (0, 0)),
            _resident((D_MODEL, ROUTE_LANES)),
            pl.BlockSpec((1, ROUTE_LANES), lambda i: (0, 0)),
        ],
        out_specs=[
            pl.BlockSpec((ROW_TILE, D_MODEL), lambda i: (i, 0)),
            pl.BlockSpec((ROW_TILE, ROW_SUB, ROW_LANE), lambda i: (i, 0, 0)),
            pl.BlockSpec((ROW_TILE, ROUTE_LANES), lambda i: (i, 0)),
            pl.BlockSpec((1, ROUTE_LANES), lambda i: (0, 0)),
            pl.BlockSpec((1, ROUTE_LANES), lambda i: (0, 0)),
        ],
        out_shape=[
            jax.ShapeDtypeStruct((N_ALL, D_MODEL), F32),
            jax.ShapeDtypeStruct((N_ALL, ROW_SUB, ROW_LANE), BF16),
            jax.ShapeDtypeStruct((N_ALL, ROUTE_LANES), F32),
            jax.ShapeDtypeStruct((1, ROUTE_LANES), F32),
            jax.ShapeDtypeStruct((1, ROUTE_LANES), jnp.int32),
        ],
        scratch_shapes=[pltpu.VMEM((1, ROUTE_LANES), F32),
                        pltpu.VMEM((NP_TILES, XA_HEADS, SAMPLE_ROWS, XA_DH), F32)],
        compiler_params=_params(1),
        name="xa_out_route",
    )(qx, mkb, mvb, qx, ck, cv, x1, w_co, g_moe, w_r, b_r)


def _positions_kernel(route_ref, cnt_ref, pos_ref):
    route = route_ref[...]
    lane = lax.broadcasted_iota(jnp.int32, route.shape, 1)
    lane_f = lane.astype(F32)
    tiles = jnp.floor((cnt_ref[...] + (MOE_TM - 1)) * (1.0 / MOE_TM))
    lr = lax.broadcasted_iota(jnp.int32, (ROUTE_LANES, ROUTE_LANES), 0)
    lc = lax.broadcasted_iota(jnp.int32, (ROUTE_LANES, ROUTE_LANES), 1)
    before = jnp.where(lr < lc, 1.0, 0.0).astype(BF16)
    tiles8 = jnp.broadcast_to(tiles, (8, ROUTE_LANES)).astype(BF16)
    start = _dot(tiles8, before)[0:1, :] * MOE_TM

    def col(k):
        return jnp.sum(jnp.where(lane == k, route, 0.0), axis=-1, keepdims=True)

    def first_row(e):
        return jnp.sum(jnp.where(lane_f == e + MOE_GROUPS, start, 0.0), axis=-1, keepdims=True)

    p0 = first_row(col(0)) + col(2)
    p1 = first_row(col(1)) + col(3)
    p = jnp.where(lane == 0, p0, jnp.where(lane == 1, p1, 0.0))
    pos_ref[...] = p.T[0:8, :].astype(jnp.int32)


def _positions(route, counts):
    rows = N_ALL // 4
    return pl.pallas_call(
        _positions_kernel,
        grid=(4,),
        in_specs=[
            pl.BlockSpec((rows, ROUTE_LANES), lambda i: (i, 0)),
            pl.BlockSpec((1, ROUTE_LANES), lambda i: (0, 0)),
        ],
        out_specs=pl.BlockSpec((8, rows), lambda i: (0, i)),
        out_shape=jax.ShapeDtypeStruct((8, N_ALL), jnp.int32),
        compiler_params=_params(1),
        name="positions",
    )(route, counts)


def _dispatch_kernel(pos_ref, zrow_ref, zon_ref, nu_ref, hm_ref, xs_ref, zbuf, sem, zsem, tbuf, tsem):
    i = pl.program_id(0)

    def zero_tile(row):
        return pltpu.make_async_copy(zbuf, xs_ref.at[pl.ds(pl.multiple_of(row, MOE_TM), MOE_TM)], zsem)

    @pl.when(i == 0)
    def _():
        zbuf[...] = jnp.zeros_like(zbuf)
        for e in range(MOE_EXPERTS):
            @pl.when(zon_ref[e] > 0)
            def _():
                zero_tile(zrow_ref[e]).start()

        def start_tail(t, carry):
            zero_tile(t * MOE_TM).start()
            return carry

        def wait_tail(t, carry):
            zero_tile(t * MOE_TM).wait()
            return carry

        lax.fori_loop(nu_ref[0], MOE_NT, start_tail, 0)
        for e in range(MOE_EXPERTS):
            @pl.when(zon_ref[e] > 0)
            def _():
                zero_tile(zrow_ref[e]).wait()
        lax.fori_loop(nu_ref[0], MOE_NT, wait_tail, 0)

    n_steps = pl.num_programs(0)

    def tile_copy(t):
        return pltpu.make_async_copy(hm_ref.at[pl.ds(t * ROW_TILE, ROW_TILE)], tbuf.at[t % 3], tsem.at[t % 3])

    def wait_rows(t):
        for k in range(MOE_TOPK):
            pltpu.make_async_copy(tbuf.at[t % 3], xs_ref.at[pl.ds(0, ROW_TILE)], sem.at[t % 2, k]).wait()

    @pl.when(i == 0)
    def _():
        tile_copy(0).start()

    tile_copy(i).wait()

    @pl.when(i + 1 < n_steps)
    def _():
        tile_copy(i + 1).start()

    slot = i % 3
    par = i % 2
    base = i * ROW_TILE

    def start(r, carry):
        for k in range(MOE_TOPK):
            dst_row = pos_ref[k, base + r]
            pltpu.make_async_copy(tbuf.at[slot, r], xs_ref.at[dst_row], sem.at[par, k]).start(priority=k)
        return carry

    lax.fori_loop(0, ROW_TILE, start, 0, unroll=8)

    @pl.when(i > 0)
    def _():
        wait_rows(i - 1)

    @pl.when(i == n_steps - 1)
    def _():
        wait_rows(i)


def _dispatch(pos_t, zero_row, zero_on, n_used, hmw):
    grid_spec = pltpu.PrefetchScalarGridSpec(
        num_scalar_prefetch=4,
        grid=(N_TILES,),
        in_specs=[pl.BlockSpec(memory_space=pl.ANY)],
        out_specs=pl.BlockSpec(memory_space=pl.ANY),
        scratch_shapes=[pltpu.VMEM((MOE_TM, ROW_SUB, ROW_LANE), BF16),
                        pltpu.SemaphoreType.DMA((2, MOE_TOPK)), pltpu.SemaphoreType.DMA(()),
                        pltpu.VMEM((3, ROW_TILE, ROW_SUB, ROW_LANE), BF16), pltpu.SemaphoreType.DMA((3,))],
    )
    return pl.pallas_call(
        _dispatch_kernel,
        grid_spec=grid_spec,
        out_shape=jax.ShapeDtypeStruct((MOE_ROWS, ROW_SUB, ROW_LANE), BF16),
        compiler_params=_params(1),
        name="dispatch",
    )(pos_t, zero_row, zero_on, n_used, hmw)


def _expert_kernel(nu_ref, first_ref, ord_ref, oe_ref, no_ref, half_ref, x_ref, w1_hbm, w3_hbm, w2_hbm, y_ref,
                   w1b, w3b, w2b, sem):
    i = pl.program_id(0)
    n_used = nu_ref[0]
    n_ord = no_ref[0]

    def weight_copies(k):
        e = oe_ref[k]
        slot = k % W_SLOTS
        return (pltpu.make_async_copy(w1_hbm.at[e], w1b.at[slot], sem.at[0, slot]),
                pltpu.make_async_copy(w3_hbm.at[e], w3b.at[slot], sem.at[1, slot]),
                pltpu.make_async_copy(w2_hbm.at[e], w2b.at[slot], sem.at[2, slot]))

    def start_weights(k):
        for cp in weight_copies(k):
            cp.start(priority=1)

    @pl.when(i == 0)
    def _():
        for k in range(W_SLOTS - 1):
            @pl.when(k < n_ord)
            def _():
                start_weights(k)

    @pl.when(i < n_used)
    def _():
        k = ord_ref[i]

        @pl.when(first_ref[i] > 0)
        def _():
            for cp in weight_copies(k):
                cp.wait()

            @pl.when(k + (W_SLOTS - 1) < n_ord)
            def _():
                start_weights(k + (W_SLOTS - 1))

        slot = k % W_SLOTS

        def swiglu(rows):
            x = _from_row_tiles(x_ref[0:rows])
            h1 = _dot(x, w1b[slot].astype(BF16))
            h3 = _dot(x, w3b[slot].astype(BF16))
            he = (h1 * _sigmoid(h1) * h3).astype(BF16)
            y_ref[0:rows] = _to_row_tiles(_dot(he, w2b[slot].astype(BF16)))

        @pl.when(half_ref[i] == 0)
        def _():
            swiglu(MOE_TM)

        @pl.when(half_ref[i] > 0)
        def _():
            swiglu(MOE_TM // 2)
            y_ref[MOE_TM // 2:] = jnp.zeros((MOE_TM // 2, ROW_SUB, ROW_LANE), BF16)

    @pl.when(i >= n_used)
    def _():
        y_ref[...] = jnp.zeros_like(y_ref)


def _experts(sched, xs, w_e1, w_e3, w_e2):
    grid_spec = pltpu.PrefetchScalarGridSpec(
        num_scalar_prefetch=6,
        grid=(MOE_NT,),
        in_specs=[
            pl.BlockSpec((MOE_TM, ROW_SUB, ROW_LANE), lambda i, nu, *_: (jnp.minimum(i, nu[0] - 1), 0, 0)),
            pl.BlockSpec(memory_space=pl.ANY),
            pl.BlockSpec(memory_space=pl.ANY),
            pl.BlockSpec(memory_space=pl.ANY),
        ],
        out_specs=pl.BlockSpec((MOE_TM, ROW_SUB, ROW_LANE), lambda i, *_: (i, 0, 0)),
        scratch_shapes=[
            pltpu.VMEM((W_SLOTS, D_MODEL, MOE_FF), F32),
            pltpu.VMEM((W_SLOTS, D_MODEL, MOE_FF), F32),
            pltpu.VMEM((W_SLOTS, MOE_FF, D_MODEL), F32),
            pltpu.SemaphoreType.DMA((3, W_SLOTS)),
        ],
    )
    return pl.pallas_call(
        _expert_kernel,
        grid_spec=grid_spec,
        out_shape=jax.ShapeDtypeStruct((MOE_ROWS, ROW_SUB, ROW_LANE), BF16),
        compiler_params=_params(1),
        name="experts",
    )(sched["n_used"], sched["tile_first"], sched["tile_ord"], sched["ord_expert"], sched["n_ord"],
      sched["tile_half"], xs, w_e1, w_e3, w_e2)


def _combine_kernel(pos_ref, ys_ref, x2_ref, route_ref, g_ref, yp_ref, ysm_ref, gbuf, sem):
    i = pl.program_id(0)

    def start_gather(tile, slot):
        def body(r, carry):
            for k in range(MOE_TOPK):
                src_row = pos_ref[k, tile * ROW_TILE + r]
                pltpu.make_async_copy(ys_ref.at[src_row], gbuf.at[slot, k, r], sem.at[slot, k]).start(priority=k)
            return carry
        lax.fori_loop(0, ROW_TILE, body, 0, unroll=8)

    @pl.when(i == 0)
    def _():
        start_gather(0, 0)

    slot = i % 2

    @pl.when(i + 1 < pl.num_programs(0))
    def _():
        start_gather(i + 1, 1 - slot)

    for k in range(MOE_TOPK):
        pltpu.make_async_copy(ys_ref.at[pl.ds(0, ROW_TILE)], gbuf.at[slot, k], sem.at[slot, k]).wait()

    route = route_ref[...]
    lane = lax.broadcasted_iota(jnp.int32, route.shape, 1)
    w0 = jnp.sum(jnp.where(lane == 4, route, 0.0), axis=-1, keepdims=True)
    w1 = jnp.sum(jnp.where(lane == 5, route, 0.0), axis=-1, keepdims=True)
    g0 = _from_row_tiles(gbuf[slot, 0]).astype(F32)
    g1 = _from_row_tiles(gbuf[slot, 1]).astype(F32)
    x3 = x2_ref[...] + (g0 * w0 + g1 * w1)
    y = _rms(x3, g_ref[...])

    @pl.when(i < NP_TILES)
    def _():
        yp_ref[...] = y

    @pl.when(i == NP_TILES)
    def _():
        ysm_ref[...] = y.reshape(DEC_BATCH, DEC_SEQ, D_MODEL)


def _combine(pos, ys, x2, route, g_f):
    grid_spec = pltpu.PrefetchScalarGridSpec(
        num_scalar_prefetch=1,
        grid=(N_TILES,),
        in_specs=[
            pl.BlockSpec(memory_space=pl.ANY),
            pl.BlockSpec((ROW_TILE, D_MODEL), lambda i, pos: (i, 0)),
            pl.BlockSpec((ROW_TILE, ROUTE_LANES), lambda i, pos: (i, 0)),
            pl.BlockSpec((1, D_MODEL), lambda i, pos: (0, 0)),
        ],
        out_specs=[
            pl.BlockSpec((ROW_TILE, D_MODEL), lambda i, pos: (jnp.minimum(i, NP_TILES - 1), 0)),
            pl.BlockSpec((DEC_BATCH, DEC_SEQ, D_MODEL), lambda i, pos: (0, 0, 0)),
        ],
        scratch_shapes=[pltpu.VMEM((2, MOE_TOPK, ROW_TILE, ROW_SUB, ROW_LANE), BF16),
                        pltpu.SemaphoreType.DMA((2, MOE_TOPK))],
    )
    return pl.pallas_call(
        _combine_kernel,
        grid_spec=grid_spec,
        out_shape=[
            jax.ShapeDtypeStruct((N_P, D_MODEL), F32),
            jax.ShapeDtypeStruct((DEC_BATCH, DEC_SEQ, D_MODEL), F32),
        ],
        compiler_params=_params(1),
        name="combine_norm",
    )(pos, ys, x2, route, g_f)


_SCHEDULE_FIELDS = ("n_used", "tile_first", "tile_ord", "tile_half", "ord_expert", "n_ord", "zero_row", "zero_on")


def _schedule_kernel(cnt_ref, nu_ref, first_ref, ord_ref, half_ref, oe_ref, no_ref, zrow_ref, zon_ref):
    i32 = jnp.int32

    def clear_tile(t, carry):
        first_ref[t] = i32(0)
        ord_ref[t] = i32(0)
        half_ref[t] = i32(0)
        return carry

    lax.fori_loop(0, MOE_NT, clear_tile, 0)

    def clear_expert(e, carry):
        oe_ref[e] = i32(0)
        return carry

    lax.fori_loop(0, MOE_EXPERTS, clear_expert, 0)

    def expert(e, carry):
        t0, k = carry
        c = cnt_ref[0, MOE_GROUPS + e]
        nt = lax.shift_right_logical(c + (MOE_TM - 1), i32(LOG_MOE_TM))
        used = jnp.where(nt > 0, i32(1), i32(0))
        zon_ref[e] = used
        zrow_ref[e] = jnp.maximum(t0 + nt - 1, 0) * MOE_TM

        @pl.when(nt > 0)
        def _():
            oe_ref[k] = e

        def tile(j, cc):
            t = t0 + j
            first_ref[t] = jnp.where(j == 0, i32(1), i32(0))
            ord_ref[t] = k
            half_ref[t] = jnp.where(c - j * MOE_TM <= MOE_TM // 2, i32(1), i32(0))
            return cc

        lax.fori_loop(0, nt, tile, 0)
        return t0 + nt, k + used

    n_tiles, n_experts = lax.fori_loop(0, MOE_EXPERTS, expert, (i32(0), i32(0)))
    nu_ref[0] = n_tiles
    no_ref[0] = n_experts


def _expert_schedule(counts_i32):
    smem = pl.BlockSpec(memory_space=pltpu.SMEM)
    sizes = (1, MOE_NT, MOE_NT, MOE_NT, MOE_EXPERTS, 1, MOE_EXPERTS, MOE_EXPERTS)
    outs = pl.pallas_call(
        _schedule_kernel,
        in_specs=[smem],
        out_specs=[smem] * len(sizes),
        out_shape=[jax.ShapeDtypeStruct((n,), jnp.int32) for n in sizes],
        name="expert_schedule",
    )(counts_i32)
    return dict(zip(_SCHEDULE_FIELDS, outs))


def kernel(x_prompt, x_sample, mem_prompt, state_ret, cache_mem_k, cache_mem_v, norm_mix, w_in, ret_gn,
           sg_ln_g, sg_ln_b, sg_ws, sg_bs, w_a_out, w_b_out, w_o, norm_xa, w_cq, w_ck, w_cv, w_co, norm_moe,
           w_rg, b_rg, w_re, b_re, w_e1, w_e3, w_e2, norm_f):
    xp = x_prompt.reshape(N_P, D_MODEL)
    xs = x_sample

    h = _norm_rows(xp, xs, norm_mix)
    w = w_in[0]
    blocks = lambda m: m.reshape(MERGE_W_BLOCKS, D_MODEL // MERGE_W_BLOCKS, D_MODEL)
    qk, wa = _inproj("rope", IN_TM_ROPE, 0, 2 * RET_QK, h, w, _rope_tables(IN_TM_ROPE), side=blocks(w_a_out[0]))
    v, wb = _inproj("copy", IN_TM, 2 * RET_QK, RET_V, h, w, side=blocks(w_b_out[0]))
    gs, wo = _inproj("silu", IN_TM, 2 * RET_QK + RET_V, RET_V, h, w, side=blocks(w_o[0]))
    uv = _inproj("gelu", IN_TM, 2 * RET_QK + 2 * RET_V, 2 * SG_WIDTH, h, w)
    gab = _inproj("sigmoid", IN_TM, 2 * RET_QK + 2 * RET_V + 2 * SG_WIDTH, 2 * D_MODEL, h, w)

    a_p, ret_p, a_s, ret_s = _retention(qk, v, ret_gn, state_ret)

    b_all, sgv = _sgate(uv, sg_ln_g, sg_ln_b, sg_ws[0], sg_bs[0])

    square = lambda m: m.reshape(D_MODEL, D_MODEL)
    x1, qx = _merge(a_p, a_s, gs, b_all, gab, xp, xs, square(wa), square(wb), square(wo), norm_xa,
                    w_cq[0].astype(BF16))

    mk, mv, mkb, mvb = _memkv(mem_prompt.reshape(BATCH * MEM_LEN, D_MODEL), w_ck[0], w_cv[0])
    ck = cache_mem_k.reshape(DEC_BATCH, MEM_LEN * XA_HEADS, XA_DH)
    cv = cache_mem_v.reshape(DEC_BATCH, MEM_LEN * XA_HEADS, XA_DH)

    pad = ROUTE_LANES - MOE_GROUPS - MOE_EXPERTS
    w_r = jnp.concatenate([w_rg[0], w_re[0], jnp.zeros((D_MODEL, pad), F32)], axis=1)
    b_r = jnp.concatenate([b_rg[0], b_re[0], jnp.zeros((pad,), F32)]).reshape(1, ROUTE_LANES)
    x2, hmw, route, counts, counts_i32 = _route(qx, mkb, mvb, ck, cv, x1, w_co[0].astype(BF16), norm_moe,
                                                w_r.astype(BF16), b_r)

    pos_t = _positions(route, counts)
    sched = _expert_schedule(counts_i32)
    xs_sorted = _dispatch(pos_t, sched["zero_row"], sched["zero_on"], sched["n_used"], hmw)
    ys = _experts(sched, xs_sorted, w_e1[0], w_e3[0], w_e2[0])
    y_p, y_s = _combine(pos_t, ys, x2, route, norm_f.reshape(1, D_MODEL))

    return (y_p.reshape(BATCH, SEQ, D_MODEL), y_s, ret_p, mk, mv, ret_s,
            sgv.reshape(1, DEC_BATCH, DEC_SEQ, SG_WIDTH))
```

```python
import functools

import jax
import jax.numpy as jnp
import numpy as np
from jax import lax
from jax.experimental import pallas as pl
from jax.experimental.pallas import tpu as pltpu

F32 = jnp.float32
BF16 = jnp.bfloat16

D_MODEL = 2048
BATCH = 4
SEQ = 2048
DEC_BATCH = 128
DEC_SEQ = 4
PAST_LEN = 16384
RET_HEADS = 8
RET_DK = 128
RET_DV = 256
RET_CHUNK = 128
ROPE_BASE = 10000.0
RET_QK = RET_HEADS * RET_DK
RET_V = RET_HEADS * RET_DV
SG_GROUPS = 4
SG_WIDTH = 2048
SG_CHUNK = 128
MEM_LEN = 256
XA_HEADS = 4
XA_DH = 128
XA_W = XA_HEADS * XA_DH
MOE_GROUPS = 4
MOE_PER_GROUP = 8
MOE_EXPERTS = MOE_GROUPS * MOE_PER_GROUP
MOE_TOPK = 2
MOE_FF = 512
EPS = 1e-6
IN_WIDTH = 2 * RET_QK + 2 * RET_V + 2 * SG_WIDTH + 2 * D_MODEL

N_P = BATCH * SEQ
N_S = DEC_BATCH * DEC_SEQ
N_ALL = N_P + N_S
ROW_TILE = 512
N_TILES = N_ALL // ROW_TILE
NP_TILES = N_P // ROW_TILE
MERGE_TILE = 256
MERGE_P_TILES = N_P // MERGE_TILE

IN_TM = 1024
IN_TM_ROPE = 1024
IN_TN = 2048
MERGE_W_BLOCKS = 4

PROMPT_CHUNK = 256
RET_BB = DEC_BATCH // (BATCH * (SEQ // PROMPT_CHUNK))
RET_ROWS = RET_BB * DEC_SEQ
LOG_DEC_SEQ = 2
LOG_SG_CHUNK = 7
SAMPLE_BB = DEC_BATCH // NP_TILES
SAMPLE_ROWS = SAMPLE_BB * DEC_SEQ

MOE_TM = 256
LOG_MOE_TM = 8
MOE_NT = (N_ALL * MOE_TOPK + MOE_EXPERTS * (MOE_TM - 1) + MOE_TM - 1) // MOE_TM
MOE_ROWS = MOE_NT * MOE_TM
ROUTE_LANES = 128
ROW_SUB, ROW_LANE = 16, 128
W_SLOTS = 3

VMEM_LIMIT = 56 * 1024 * 1024


def _params(n_axes, vmem=VMEM_LIMIT):
    return pltpu.CompilerParams(dimension_semantics=("arbitrary",) * n_axes,
                                vmem_limit_bytes=vmem)


def _rms(x, g):
    ms = jnp.mean(x * x, axis=-1, keepdims=True)
    return (x * lax.rsqrt(ms + EPS)) * g


def _dot(a, b):
    return jnp.dot(a, b, preferred_element_type=F32)


def _sigmoid(x):
    return 0.5 * jnp.tanh(0.5 * x) + 0.5


_GELU_C0 = float(np.float32(np.sqrt(2.0 / np.pi)))


def _gelu_tanh(x):
    hx = 0.5 * x
    return hx + hx * jnp.tanh(x * (_GELU_C0 + (_GELU_C0 * 0.044715) * (x * x)))


def _to_row_tiles(x):
    return x.astype(BF16).reshape(x.shape[0], ROW_SUB, ROW_LANE)


def _from_row_tiles(t):
    return t.reshape(t.shape[0], D_MODEL)


def _dot_nt(a, b):
    return lax.dot_general(a, b, (((1,), (1,)), ((), ())), preferred_element_type=F32)


def _dot_tn(a, b):
    return lax.dot_general(a, b, (((0,), (0,)), ((), ())), preferred_element_type=F32)


def _norm_kernel(xp_ref, xs_ref, g_ref, h_ref):
    i = pl.program_id(0)

    @pl.when(i < NP_TILES)
    def _():
        h_ref[...] = _rms(xp_ref[...], g_ref[...]).astype(BF16)

    @pl.when(i == NP_TILES)
    def _():
        h_ref[...] = _rms(xs_ref[...].reshape(N_S, D_MODEL), g_ref[...]).astype(BF16)


def _norm_rows(xp, xs, g):
    return pl.pallas_call(
        _norm_kernel,
        grid=(N_TILES,),
        in_specs=[
            pl.BlockSpec((ROW_TILE, D_MODEL), lambda i: (jnp.minimum(i, NP_TILES - 1), 0)),
            pl.BlockSpec((DEC_BATCH, DEC_SEQ, D_MODEL), lambda i: (0, 0, 0)),
            pl.BlockSpec((1, D_MODEL), lambda i: (0, 0)),
        ],
        out_specs=pl.BlockSpec((ROW_TILE, D_MODEL), lambda i: (i, 0)),
        out_shape=jax.ShapeDtypeStruct((N_ALL, D_MODEL), BF16),
        compiler_params=_params(1),
        name="norm_rows",
    )(xp, xs, g)


def _inproj_kernel(kind, tm, n_side, h_ref, w_ref, *rest):
    rest = list(rest)
    cos_ref, sin_ref = (rest.pop(0), rest.pop(0)) if kind == "rope" else (None, None)
    side_ref = rest.pop(0) if n_side else None
    z_ref = rest.pop(0)
    side_out_ref = rest.pop(0) if n_side else None
    j = pl.program_id(0)
    i = pl.program_id(1)
    last = N_P // tm

    if n_side:
        @pl.when(j * (last + 1) + i < n_side)
        def _():
            side_out_ref[...] = side_ref[...].astype(BF16)

    def tile(rows):
        acc = _dot(h_ref[0:rows, :], w_ref[...].astype(BF16))
        if kind == "rope":
            c = cos_ref[0:rows, :]
            s = sin_ref[0:rows, :]
            heads_per_block = IN_TN // RET_DK
            for hb in range(heads_per_block):
                scale = jnp.where(j * heads_per_block + hb >= RET_HEADS, RET_DK ** -0.5, 1.0).astype(F32)
                cols = slice(hb * RET_DK, (hb + 1) * RET_DK)
                a = acc[:, cols]
                r = pltpu.roll(a, RET_DK // 2, axis=1)
                z_ref[0:rows, cols] = ((a * c + r * s) * scale).astype(BF16)
        elif kind == "copy":
            z_ref[0:rows, :] = acc.astype(BF16)
        elif kind == "silu":
            z_ref[0:rows, :] = (acc * _sigmoid(acc)).astype(BF16)
        elif kind == "gelu":
            z_ref[0:rows, :] = _gelu_tanh(acc).astype(BF16)
        else:
            z_ref[0:rows, :] = _sigmoid(acc).astype(BF16)

    @pl.when(i < last)
    def _():
        tile(tm)

    @pl.when(i == last)
    def _():
        tile(N_S)


def _inproj(kind, tm, col0, width, h, w_in, tables=(), side=None):
    last = N_P // tm
    n_j = width // IN_TN
    tab_idx = lambda j, i: (jnp.where(i < last, i % (SEQ // tm), SEQ // tm), 0)
    j0 = col0 // IN_TN
    in_specs = [
        pl.BlockSpec((tm, D_MODEL), lambda j, i: (i, 0)),
        pl.BlockSpec((D_MODEL, IN_TN), lambda j, i: (0, j0 + j)),
    ] + [pl.BlockSpec((tm, RET_DK), tab_idx) for _ in tables]
    out_specs = [pl.BlockSpec((tm, IN_TN), lambda j, i: (i, j))]
    out_shape = [jax.ShapeDtypeStruct((N_ALL, width), BF16)]
    n_side = 0 if side is None else side.shape[0]
    if n_side:
        assert n_side <= n_j * (last + 1)
        blk = (1,) + side.shape[1:]
        side_blk = lambda j, i: (jnp.minimum(j * (last + 1) + i, n_side - 1), 0, 0)
        in_specs.append(pl.BlockSpec(blk, side_blk))
        out_specs.append(pl.BlockSpec(blk, side_blk))
        out_shape.append(jax.ShapeDtypeStruct(side.shape, BF16))
    outs = pl.pallas_call(
        functools.partial(_inproj_kernel, kind, tm, n_side),
        grid=(n_j, last + 1),
        in_specs=in_specs,
        out_specs=out_specs,
        out_shape=out_shape,
        compiler_params=_params(2),
        name="in_proj_" + kind,
    )(h, w_in, *tables, *(() if side is None else (side,)))
    return outs if n_side else outs[0]


def _rope_tables(tm):
    half = RET_DK // 2
    inv = ROPE_BASE ** (-jnp.arange(half, dtype=F32) / half)

    def tab(pos):
        ang = pos.astype(F32)[:, None] * inv[None, :]
        c, s = jnp.cos(ang), jnp.sin(ang)
        return jnp.concatenate([c, c], -1), jnp.concatenate([-s, s], -1)

    cp, sp = tab(jnp.arange(SEQ, dtype=jnp.int32))
    cs, ss = tab(PAST_LEN + jnp.arange(DEC_SEQ, dtype=jnp.int32))
    cs = jnp.tile(cs, (DEC_BATCH, 1))
    ss = jnp.tile(ss, (DEC_BATCH, 1))
    pad = jnp.zeros((tm - N_S, RET_DK), F32)
    return jnp.concatenate([cp, cs, pad], 0), jnp.concatenate([sp, ss, pad], 0)


def _decay_tables(chunk):
    lg = jnp.log1p(-jnp.power(2.0, -5.0 - jnp.arange(RET_HEADS, dtype=F32)))
    idx = jnp.arange(chunk, dtype=F32)
    rel = idx[:, None] - idx[None, :]
    dmask = jnp.where(rel >= 0, jnp.exp(lg[:, None, None] * jnp.maximum(rel, 0.0)), 0.0).astype(F32)
    xi = jnp.exp(lg[:, None] * (idx[None, :] + 1.0)).astype(F32)
    zeta = jnp.exp(lg[:, None] * (chunk - 1.0 - idx[None, :])).astype(F32)
    gc = jnp.exp(lg * chunk).astype(F32)
    return dmask, xi, zeta, gc


def _head_norm(o, gn):
    mu = jnp.mean(o, axis=-1, keepdims=True)
    d = o - mu
    var = jnp.mean(d * d, axis=-1, keepdims=True)
    return ((d * lax.rsqrt(var + EPS)) * gn).astype(BF16)


def _ret_kernel(gcp_ref, gcs_ref, q_ref, k_ref, v_ref, gn_ref, dm_ref, xi_ref, zt_ref,
                qs_ref, ks_ref, vs_ref, dms_ref, xis_ref, zts_ref, s0_ref,
                a_ref, sfin_ref, as_ref, s1_ref, s_ref):
    c = pl.program_id(1)

    @pl.when(c == 0)
    def _():
        s_ref[...] = jnp.zeros_like(s_ref)

    rows_k = lax.broadcasted_iota(jnp.int32, (RET_ROWS, RET_DK), 0) >> LOG_DEC_SEQ
    rows_v = lax.broadcasted_iota(jnp.int32, (RET_ROWS, RET_DV), 0) >> LOG_DEC_SEQ
    for h in range(RET_HEADS):
        kc = slice(h * RET_DK, (h + 1) * RET_DK)
        vc = slice(h * RET_DV, (h + 1) * RET_DV)
        gn = gn_ref[:, vc]

        qh = q_ref[:, kc]
        kh = k_ref[:, kc]
        vh = v_ref[:, vc]
        inner = _dot_nt(qh, kh) * dm_ref[h]
        o = _dot(inner.astype(BF16), vh)
        s_old = s_ref[h]
        xi = xi_ref[h]
        o = o + _dot(qh, s_old.astype(BF16)) * jnp.concatenate([xi, xi], axis=1)
        kz = (kh.astype(F32) * zt_ref[h]).astype(BF16)
        s_ref[h] = gcp_ref[h] * s_old + _dot_tn(kz, vh)
        a_ref[:, vc] = _head_norm(o, gn)

        qh = qs_ref[:, kc]
        kh = ks_ref[:, kc]
        vh = vs_ref[:, vc]
        inner = _dot_nt(qh, kh) * dms_ref[h]
        o = _dot(inner.astype(BF16), vh)
        xi = xis_ref[h]
        xi2 = jnp.concatenate([xi, xi], axis=1)
        kz = kh.astype(F32) * zts_ref[h]
        gch = gcs_ref[h]
        for b in range(RET_BB):
            s_old = s0_ref[0, b, h]
            cross = _dot(qh, s_old.astype(BF16)) * xi2
            o = o + jnp.where(rows_v == b, cross, 0.0)
            kz_b = jnp.where(rows_k == b, kz, 0.0).astype(BF16)
            s1_ref[0, b, h] = gch * s_old + _dot_tn(kz_b, vh)
        as_ref[:, vc] = _head_norm(o, gn)

    @pl.when(c == pl.num_programs(1) - 1)
    def _():
        sfin_ref[0, 0] = s_ref[...]


def _retention(qk, v, ret_gn, state):
    chunk = PROMPT_CHUNK
    n_chunks = SEQ // chunk
    dmask, xi, zeta, gc = _decay_tables(chunk)
    xi_b = jnp.broadcast_to(xi[:, :, None], (RET_HEADS, chunk, RET_DK))
    zeta_b = jnp.broadcast_to(zeta[:, :, None], (RET_HEADS, chunk, RET_DK))
    dmask_s, xi_s, zeta_s, gc_s = _decay_tables(DEC_SEQ)
    eye = jnp.eye(RET_BB, dtype=F32)
    dm_big = jax.vmap(lambda m: jnp.kron(eye, m))(dmask_s)
    xi_sb = jnp.broadcast_to(jnp.tile(xi_s, (1, RET_BB))[:, :, None], (RET_HEADS, RET_ROWS, RET_DK))
    zeta_sb = jnp.broadcast_to(jnp.tile(zeta_s, (1, RET_BB))[:, :, None], (RET_HEADS, RET_ROWS, RET_DK))

    row = lambda b, c: b * n_chunks + c
    srow = lambda b, c: N_P // RET_ROWS + row(b, c)
    const3 = lambda b, c: (0, 0, 0)
    smem = pl.BlockSpec(memory_space=pltpu.SMEM)
    st_spec = pl.BlockSpec((1, RET_BB, RET_HEADS, RET_DK, RET_DV), lambda b, c: (0, row(b, c), 0, 0, 0))
    return pl.pallas_call(
        _ret_kernel,
        grid=(BATCH, n_chunks),
        in_specs=[
            smem, smem,
            pl.BlockSpec((chunk, RET_QK), lambda b, c: (row(b, c), 0)),
            pl.BlockSpec((chunk, RET_QK), lambda b, c: (row(b, c), 1)),
            pl.BlockSpec((chunk, RET_V), lambda b, c: (row(b, c), 0)),
            pl.BlockSpec((1, RET_V), lambda b, c: (0, 0)),
            pl.BlockSpec((RET_HEADS, chunk, chunk), const3),
            pl.BlockSpec((RET_HEADS, chunk, RET_DK), const3),
            pl.BlockSpec((RET_HEADS, chunk, RET_DK), const3),
            pl.BlockSpec((RET_ROWS, RET_QK), lambda b, c: (srow(b, c), 0)),
            pl.BlockSpec((RET_ROWS, RET_QK), lambda b, c: (srow(b, c), 1)),
            pl.BlockSpec((RET_ROWS, RET_V), lambda b, c: (srow(b, c), 0)),
            pl.BlockSpec((RET_HEADS, RET_ROWS, RET_ROWS), const3),
            pl.BlockSpec((RET_HEADS, RET_ROWS, RET_DK), const3),
            pl.BlockSpec((RET_HEADS, RET_ROWS, RET_DK), const3),
            st_spec,
        ],
        out_specs=[
            pl.BlockSpec((chunk, RET_V), lambda b, c: (row(b, c), 0)),
            pl.BlockSpec((1, 1, RET_HEADS, RET_DK, RET_DV), lambda b, c: (0, b, 0, 0, 0)),
            pl.BlockSpec((RET_ROWS, RET_V), lambda b, c: (row(b, c), 0)),
            st_spec,
        ],
        out_shape=[
            jax.ShapeDtypeStruct((N_P, RET_V), BF16),
            jax.ShapeDtypeStruct((1, BATCH, RET_HEADS, RET_DK, RET_DV), F32),
            jax.ShapeDtypeStruct((N_S, RET_V), BF16),
            jax.ShapeDtypeStruct((1, DEC_BATCH, RET_HEADS, RET_DK, RET_DV), F32),
        ],
        scratch_shapes=[pltpu.VMEM((RET_HEADS, RET_DK, RET_DV), F32)],
        compiler_params=_params(2),
        name="retention",
    )(gc, gc_s, qk, qk, v, ret_gn, dmask, xi_b, zeta_b, qk, qk, v, dm_big, xi_sb, zeta_sb, state)


def _sgate_kernel(u_ref, v_ref, lg_ref, lb_ref, wp_ref, bp_ref, ws_ref, bs_ref, b_ref, sgv_ref, vln_ref):
    i = pl.program_id(0)
    v = v_ref[...].astype(F32)
    mu = jnp.mean(v, axis=-1, keepdims=True)
    d = v - mu
    var = jnp.mean(d * d, axis=-1, keepdims=True)
    vln_ref[...] = (d * lax.rsqrt(var + EPS)) * lg_ref[...] + lb_ref[...]
    gw = SG_WIDTH // SG_GROUPS
    lane_reps = gw // 128

    @pl.when(i < NP_TILES)
    def _():
        r = lax.broadcasted_iota(jnp.int32, (SG_CHUNK, SG_CHUNK), 0)
        c = lax.broadcasted_iota(jnp.int32, (SG_CHUNK, SG_CHUNK), 1)
        for g in range(SG_GROUPS):
            cols = slice(g * gw, (g + 1) * gw)
            w = jnp.where(c <= r, wp_ref[g], 0.0).astype(BF16)
            bias = jnp.concatenate([bp_ref[g]] * lane_reps, axis=1)
            chunks = [slice(ch * SG_CHUNK, (ch + 1) * SG_CHUNK) for ch in range(ROW_TILE // SG_CHUNK)]
            mixed = [_dot(w, vln_ref[rows, cols].astype(BF16)) for rows in chunks]
            for rows, m in zip(chunks, mixed):
                b_ref[rows, cols] = (u_ref[rows, cols].astype(F32) * (m + bias)).astype(BF16)

    @pl.when(i == NP_TILES)
    def _():
        sgv_ref[...] = vln_ref[...].reshape(DEC_BATCH, DEC_SEQ, SG_WIDTH)
        r = lax.broadcasted_iota(jnp.int32, (ROW_TILE, ROW_TILE), 0)
        c = lax.broadcasted_iota(jnp.int32, (ROW_TILE, ROW_TILE), 1)
        keep = ((r >> LOG_DEC_SEQ) == (c >> LOG_DEC_SEQ)) & (c <= r)
        for g in range(SG_GROUPS):
            cols = slice(g * gw, (g + 1) * gw)
            w_rows = jnp.concatenate([ws_ref[g]] * (ROW_TILE // 8), axis=0)
            w_full = jnp.concatenate([w_rows] * (ROW_TILE // 128), axis=1)
            w = jnp.where(keep, w_full, 0.0).astype(BF16)
            b_rows = jnp.concatenate([bs_ref[g]] * (ROW_TILE // 8), axis=0)
            bias = jnp.concatenate([b_rows] * lane_reps, axis=1)
            mixed = _dot(w, vln_ref[:, cols].astype(BF16)) + bias
            b_ref[:, cols] = (u_ref[:, cols].astype(F32) * mixed).astype(BF16)


def _sgate(uv, ln_g, ln_b, sg_ws, sg_bs):
    b_p = jnp.broadcast_to(sg_bs[:, :, None], (SG_GROUPS, SG_CHUNK, 128))
    w_s = jnp.tile(sg_ws[:, :DEC_SEQ, :DEC_SEQ], (1, 8 // DEC_SEQ, 128 // DEC_SEQ))
    b_s = jnp.broadcast_to(jnp.tile(sg_bs[:, :DEC_SEQ], (1, 8 // DEC_SEQ))[:, :, None], (SG_GROUPS, 8, 128))
    const3 = lambda i: (0, 0, 0)
    return pl.pallas_call(
        _sgate_kernel,
        grid=(N_TILES,),
        in_specs=[
            pl.BlockSpec((ROW_TILE, SG_WIDTH), lambda i: (i, 0)),
            pl.BlockSpec((ROW_TILE, SG_WIDTH), lambda i: (i, 1)),
            pl.BlockSpec((1, SG_WIDTH), lambda i: (0, 0)),
            pl.BlockSpec((1, SG_WIDTH), lambda i: (0, 0)),
            pl.BlockSpec((SG_GROUPS, SG_CHUNK, SG_CHUNK), const3),
            pl.BlockSpec((SG_GROUPS, SG_CHUNK, 128), const3),
            pl.BlockSpec((SG_GROUPS, 8, 128), const3),
            pl.BlockSpec((SG_GROUPS, 8, 128), const3),
        ],
        out_specs=[
            pl.BlockSpec((ROW_TILE, SG_WIDTH), lambda i: (i, 0)),
            pl.BlockSpec((DEC_BATCH, DEC_SEQ, SG_WIDTH), lambda i: (0, 0, 0)),
        ],
        out_shape=[
            jax.ShapeDtypeStruct((N_ALL, SG_WIDTH), BF16),
            jax.ShapeDtypeStruct((DEC_BATCH, DEC_SEQ, SG_WIDTH), F32),
        ],
        scratch_shapes=[pltpu.VMEM((ROW_TILE, SG_WIDTH), F32)],
        compiler_params=_params(1),
        name="spatial_gate",
    )(uv, uv, ln_g, ln_b, sg_ws, b_p, w_s, b_s)


def _merge_kernel(ap_ref, as_ref, gs_ref, b_ref, ga_ref, gb_ref, xp_ref, xs_ref, wa_ref, wb_ref, wo_ref,
                  g_ref, wq_ref, x1_ref, q_ref):
    i = pl.program_id(0)

    def run(a_norm, x):
        b = _dot(b_ref[...], wb_ref[...])
        a_in = (gs_ref[...].astype(F32) * a_norm.astype(F32)).astype(BF16)
        a = _dot(a_in, wa_ref[...])
        merged = ga_ref[...].astype(F32) * a + gb_ref[...].astype(F32) * b
        x1 = x + _dot(merged.astype(BF16), wo_ref[...])
        x1_ref[...] = x1
        q_ref[...] = _dot(_rms(x1, g_ref[...]).astype(BF16), wq_ref[...]).astype(BF16)

    @pl.when(i < MERGE_P_TILES)
    def _():
        run(ap_ref[...], xp_ref[...])

    @pl.when(i >= MERGE_P_TILES)
    def _():
        run(as_ref[...], xs_ref[...].reshape(MERGE_TILE, D_MODEL))


def _resident(shape):
    return pl.BlockSpec(shape, lambda i: (0,) * len(shape), pipeline_mode=pl.Buffered(1))


def _merge(a_p, a_s, gs, b_all, gab, xp, xs, wa, wb, wo, g_xa, wq):
    tm = MERGE_TILE
    prompt_tile = lambda i: (jnp.minimum(i, MERGE_P_TILES - 1), 0)
    sample_tile = lambda i: (jnp.maximum(i - MERGE_P_TILES, 0), 0)
    return pl.pallas_call(
        _merge_kernel,
        grid=(N_ALL // tm,),
        in_specs=[
            pl.BlockSpec((tm, RET_V), prompt_tile),
            pl.BlockSpec((tm, RET_V), sample_tile),
            pl.BlockSpec((tm, RET_V), lambda i: (i, 0)),
            pl.BlockSpec((tm, SG_WIDTH), lambda i: (i, 0)),
            pl.BlockSpec((tm, D_MODEL), lambda i: (i, 0)),
            pl.BlockSpec((tm, D_MODEL), lambda i: (i, 1)),
            pl.BlockSpec((tm, D_MODEL), prompt_tile),
            pl.BlockSpec((tm // DEC_SEQ, DEC_SEQ, D_MODEL), lambda i: (jnp.maximum(i - MERGE_P_TILES, 0), 0, 0)),
            _resident((RET_V, D_MODEL)),
            _resident((SG_WIDTH, D_MODEL)),
            _resident((D_MODEL, D_MODEL)),
            pl.BlockSpec((1, D_MODEL), lambda i: (0, 0)),
            _resident((D_MODEL, XA_W)),
        ],
        out_specs=[
            pl.BlockSpec((tm, D_MODEL), lambda i: (i, 0)),
            pl.BlockSpec((tm, XA_W), lambda i: (i, 0)),
        ],
        out_shape=[
            jax.ShapeDtypeStruct((N_ALL, D_MODEL), F32),
            jax.ShapeDtypeStruct((N_ALL, XA_W), BF16),
        ],
        compiler_params=_params(1),
        name="merge_proj",
    )(a_p, a_s, gs, b_all, gab, gab, xp, xs, wa, wb, wo, g_xa, wq)


def _memkv_kernel(m_ref, wk_ref, wv_ref, k_ref, v_ref, kb_ref, vb_ref):
    m = m_ref[...].astype(BF16)
    k = _dot(m, wk_ref[...].astype(BF16))
    v = _dot(m, wv_ref[...].astype(BF16))
    k_ref[0, 0] = k.reshape(MEM_LEN, XA_HEADS, XA_DH)
    v_ref[0, 0] = v.reshape(MEM_LEN, XA_HEADS, XA_DH)
    kb_ref[...] = k.astype(BF16)
    vb_ref[...] = v.astype(BF16)


def _memkv(mem, w_ck, w_cv):
    rows = BATCH * MEM_LEN
    spec = pl.BlockSpec((MEM_LEN, XA_W), lambda b: (b, 0))
    spec5 = pl.BlockSpec((1, 1, MEM_LEN, XA_HEADS, XA_DH), lambda b: (0, b, 0, 0, 0))
    shape5 = jax.ShapeDtypeStruct((1, BATCH, MEM_LEN, XA_HEADS, XA_DH), F32)
    wspec = pl.BlockSpec((D_MODEL, XA_W), lambda b: (0, 0))
    return pl.pallas_call(
        _memkv_kernel,
        grid=(BATCH,),
        in_specs=[pl.BlockSpec((MEM_LEN, D_MODEL), lambda b: (b, 0)), wspec, wspec],
        out_specs=[spec5, spec5, spec, spec],
        out_shape=[shape5, shape5] + [jax.ShapeDtypeStruct((rows, XA_W), BF16)] * 2,
        compiler_params=_params(1),
        name="mem_kv",
    )(mem, w_ck, w_cv)


def _softmax_rows(s):
    m = jnp.max(s, axis=-1, keepdims=True)
    e = jnp.exp(s - m)
    return e / jnp.sum(e, axis=-1, keepdims=True)


def _xattn_prompt_rows(q_ref, k_ref, v_ref, after_scores=lambda: None):
    cols = [slice(h * XA_DH, (h + 1) * XA_DH) for h in range(XA_HEADS)]
    scores = [_dot_nt(q_ref[:, c], k_ref[:, c]) * (XA_DH ** -0.5) for c in cols]
    after_scores()
    probs = [_softmax_rows(s).astype(BF16) for s in scores]
    heads = [_dot(p, v_ref[:, c]).astype(BF16) for p, c in zip(probs, cols)]
    return jnp.concatenate(heads, axis=1)


def _xattn_sample_scores(q_ref, k_ref):
    qf = q_ref[...].astype(F32)
    scores = []
    for b in range(SAMPLE_BB):
        rows = slice(b * DEC_SEQ, (b + 1) * DEC_SEQ)
        qb = jnp.concatenate([qf[rows, h * XA_DH:(h + 1) * XA_DH] for h in range(XA_HEADS)], axis=0)
        scores.append(_dot_nt(qb.astype(BF16), k_ref[b].astype(BF16)) * (XA_DH ** -0.5))
    return scores


def _xattn_sample_finish(scores, v_ref, o_ref):
    n_q = XA_HEADS * DEC_SEQ
    n_kv = MEM_LEN * XA_HEADS
    r = lax.broadcasted_iota(jnp.int32, (n_q, n_kv), 0)
    c = lax.broadcasted_iota(jnp.int32, (n_q, n_kv), 1)
    head_ok = (r >> LOG_DEC_SEQ) == (c & (XA_HEADS - 1))
    probs = [_softmax_rows(jnp.where(head_ok, s, -1e30)).astype(BF16) for s in scores]
    for b in range(SAMPLE_BB):
        rows = slice(b * DEC_SEQ, (b + 1) * DEC_SEQ)
        o = _dot(probs[b], v_ref[b].astype(BF16))
        for h in range(XA_HEADS):
            o_ref[h, rows, :] = o[h * DEC_SEQ:(h + 1) * DEC_SEQ, :]


def _route_kernel(qp_ref, mk_ref, mv_ref, qs_ref, ck_ref, cv_ref, x1_ref, wo_ref, g_ref, wr_ref, br_ref,
                  x2_ref, hm_ref, route_ref, cnt_ref, cnti_ref, run_ref, os_ref):
    i = pl.program_id(0)

    @pl.when(i == 0)
    def _():
        run_ref[...] = jnp.zeros_like(run_ref)

    def run(o, between=lambda: None):
        x2 = x1_ref[...] + _dot(o, wo_ref[...])
        x2_ref[...] = x2
        hm = _rms(x2, g_ref[...])
        hm_ref[...] = _to_row_tiles(hm)
        logits = _dot(hm.astype(BF16), wr_ref[...]) + br_ref[...]
        lane = lax.broadcasted_iota(jnp.int32, logits.shape, 1)
        lane_f = lane.astype(F32)
        neg = jnp.float32(-jnp.inf)
        big = jnp.float32(1 << 20)
        is_g = lane < MOE_GROUPS
        gl = jnp.where(is_g, logits, neg)
        gmax = jnp.max(gl, axis=-1, keepdims=True)
        g_sel = jnp.min(jnp.where(gl == gmax, lane_f, big), axis=-1, keepdims=True)
        g_w = 1.0 / jnp.sum(jnp.where(is_g, jnp.exp(logits - gmax), 0.0), axis=-1, keepdims=True)
        between()
        e_lane = lane - MOE_GROUPS
        e_group = (e_lane >> 3).astype(F32)
        in_grp = (e_lane >= 0) & (e_lane < MOE_EXPERTS) & (e_group == g_sel)
        el = jnp.where(in_grp, logits, neg)
        v0 = jnp.max(el, axis=-1, keepdims=True)
        i0 = jnp.min(jnp.where(el == v0, lane_f, big), axis=-1, keepdims=True)
        el1 = jnp.where(lane_f == i0, neg, el)
        v1 = jnp.max(el1, axis=-1, keepdims=True)
        i1 = jnp.min(jnp.where(el1 == v1, lane_f, big), axis=-1, keepdims=True)
        ex = jnp.exp(v1 - v0)
        den = 1.0 + ex
        w0 = (1.0 / den) * g_w
        w1 = (ex / den) * g_w
        a0 = (lane_f == i0).astype(F32)
        a1 = (lane_f == i1).astype(F32)
        a = a0 + a1
        rr = lax.broadcasted_iota(jnp.int32, (ROW_TILE, ROW_TILE), 0)
        cc = lax.broadcasted_iota(jnp.int32, (ROW_TILE, ROW_TILE), 1)
        lower = jnp.where(cc < rr, 1.0, 0.0).astype(BF16)
        before = _dot(lower, a.astype(BF16)) + run_ref[...]
        rank0 = jnp.sum(before * a0, axis=-1, keepdims=True)
        rank1 = jnp.sum(before * a1, axis=-1, keepdims=True)
        run_ref[...] += jnp.sum(a, axis=0, keepdims=True)
        e0 = i0 - MOE_GROUPS
        e1 = i1 - MOE_GROUPS
        route = jnp.where(lane == 0, e0, 0.0)
        route = jnp.where(lane == 1, e1, route)
        route = jnp.where(lane == 2, rank0, route)
        route = jnp.where(lane == 3, rank1, route)
        route = jnp.where(lane == 4, w0, route)
        route = jnp.where(lane == 5, w1, route)
        route_ref[...] = route
        cnt_ref[...] = run_ref[...]
        cnti_ref[...] = run_ref[...].astype(jnp.int32)

    @pl.when(i < NP_TILES)
    def _():
        scores = []
        o_prompt = _xattn_prompt_rows(qp_ref, mk_ref, mv_ref,
                                      after_scores=lambda: scores.extend(_xattn_sample_scores(qs_ref, ck_ref)))
        run(o_prompt, between=lambda: _xattn_sample_finish(scores, cv_ref, os_ref.at[i]))

    @pl.when(i == NP_TILES)
    def _():
        heads = [os_ref[:, h].reshape(N_S, XA_DH) for h in range(XA_HEADS)]
        run(jnp.concatenate(heads, axis=1).astype(BF16))


def _route(qx, mkb, mvb, ck, cv, x1, w_co, g_moe, w_r, b_r):
    prompt_step = lambda i: jnp.minimum(i, NP_TILES - 1)
    kv = pl.BlockSpec((SAMPLE_BB, MEM_LEN * XA_HEADS, XA_DH), lambda i: (prompt_step(i), 0, 0))
    mem_kv = pl.BlockSpec((MEM_LEN, XA_W), lambda i: (prompt_step(i) // (SEQ // ROW_TILE), 0))
    return pl.pallas_call(
        _route_kernel,
        grid=(N_TILES,),
        in_specs=[
            pl.BlockSpec((ROW_TILE, XA_W), lambda i: (prompt_step(i), 0)),
            mem_kv, mem_kv,
            pl.BlockSpec((SAMPLE_ROWS, XA_W), lambda i: (N_P // SAMPLE_ROWS + prompt_step(i), 0)),
            kv, kv,
            pl.BlockSpec((ROW_TILE, D_MODEL), lambda i: (i, 0)),
            _resident((XA_W, D_MODEL)),
            pl.BlockSpec((1, D_MODEL), lambda i: (0, 0)),
            _resident((D_MODEL, ROUTE_LANES)),
            pl.BlockSpec((1, ROUTE_LANES), lambda i: (0, 0)),
        ],
        out_specs=[
            pl.BlockSpec((ROW_TILE, D_MODEL), lambda i: (i, 0)),
            pl.BlockSpec((ROW_TILE, ROW_SUB, ROW_LANE), lambda i: (i, 0, 0)),
            pl.BlockSpec((ROW_TILE, ROUTE_LANES), lambda i: (i, 0)),
            pl.BlockSpec((1, ROUTE_LANES), lambda i: (0, 0)),
            pl.BlockSpec((1, ROUTE_LANES), lambda i: (0, 0)),
        ],
        out_shape=[
            jax.ShapeDtypeStruct((N_ALL, D_MODEL), F32),
            jax.ShapeDtypeStruct((N_ALL, ROW_SUB, ROW_LANE), BF16),
            jax.ShapeDtypeStruct((N_ALL, ROUTE_LANES), F32),
            jax.ShapeDtypeStruct((1, ROUTE_LANES), F32),
            jax.ShapeDtypeStruct((1, ROUTE_LANES), jnp.int32),
        ],
        scratch_shapes=[pltpu.VMEM((1, ROUTE_LANES), F32),
                        pltpu.VMEM((NP_TILES, XA_HEADS, SAMPLE_ROWS, XA_DH), F32)],
        compiler_params=_params(1),
        name="xa_out_route",
    )(qx, mkb, mvb, qx, ck, cv, x1, w_co, g_moe, w_r, b_r)


def _positions_kernel(route_ref, cnt_ref, pos_ref):
    route = route_ref[...]
    lane = lax.broadcasted_iota(jnp.int32, route.shape, 1)
    lane_f = lane.astype(F32)
    tiles = jnp.floor((cnt_ref[...] + (MOE_TM - 1)) * (1.0 / MOE_TM))
    lr = lax.broadcasted_iota(jnp.int32, (ROUTE_LANES, ROUTE_LANES), 0)
    lc = lax.broadcasted_iota(jnp.int32, (ROUTE_LANES, ROUTE_LANES), 1)
    before = jnp.where(lr < lc, 1.0, 0.0).astype(BF16)
    tiles8 = jnp.broadcast_to(tiles, (8, ROUTE_LANES)).astype(BF16)
    start = _dot(tiles8, before)[0:1, :] * MOE_TM

    def col(k):
        return jnp.sum(jnp.where(lane == k, route, 0.0), axis=-1, keepdims=True)

    def first_row(e):
        return jnp.sum(jnp.where(lane_f == e + MOE_GROUPS, start, 0.0), axis=-1, keepdims=True)

    p0 = first_row(col(0)) + col(2)
    p1 = first_row(col(1)) + col(3)
    p = jnp.where(lane == 0, p0, jnp.where(lane == 1, p1, 0.0))
    pos_ref[...] = p.T[0:8, :].astype(jnp.int32)


def _positions(route, counts):
    rows = N_ALL // 4
    return pl.pallas_call(
        _positions_kernel,
        grid=(4,),
        in_specs=[
            pl.BlockSpec((rows, ROUTE_LANES), lambda i: (i, 0)),
            pl.BlockSpec((1, ROUTE_LANES), lambda i: (0, 0)),
        ],
        out_specs=pl.BlockSpec((8, rows), lambda i: (0, i)),
        out_shape=jax.ShapeDtypeStruct((8, N_ALL), jnp.int32),
        compiler_params=_params(1),
        name="positions",
    )(route, counts)


def _dispatch_kernel(pos_ref, zrow_ref, zon_ref, nu_ref, hm_ref, xs_ref, zbuf, sem, zsem, tbuf, tsem):
    i = pl.program_id(0)

    def zero_tile(row):
        return pltpu.make_async_copy(zbuf, xs_ref.at[pl.ds(pl.multiple_of(row, MOE_TM), MOE_TM)], zsem)

    @pl.when(i == 0)
    def _():
        zbuf[...] = jnp.zeros_like(zbuf)
        for e in range(MOE_EXPERTS):
            @pl.when(zon_ref[e] > 0)
            def _():
                zero_tile(zrow_ref[e]).start()

        def start_tail(t, carry):
            zero_tile(t * MOE_TM).start()
            return carry

        def wait_tail(t, carry):
            zero_tile(t * MOE_TM).wait()
            return carry

        lax.fori_loop(nu_ref[0], MOE_NT, start_tail, 0)
        for e in range(MOE_EXPERTS):
            @pl.when(zon_ref[e] > 0)
            def _():
                zero_tile(zrow_ref[e]).wait()
        lax.fori_loop(nu_ref[0], MOE_NT, wait_tail, 0)

    n_steps = pl.num_programs(0)

    def tile_copy(t):
        return pltpu.make_async_copy(hm_ref.at[pl.ds(t * ROW_TILE, ROW_TILE)], tbuf.at[t % 3], tsem.at[t % 3])

    def wait_rows(t):
        for k in range(MOE_TOPK):
            pltpu.make_async_copy(tbuf.at[t % 3], xs_ref.at[pl.ds(0, ROW_TILE)], sem.at[t % 2, k]).wait()

    @pl.when(i == 0)
    def _():
        tile_copy(0).start()

    tile_copy(i).wait()

    @pl.when(i + 1 < n_steps)
    def _():
        tile_copy(i + 1).start()

    slot = i % 3
    par = i % 2
    base = i * ROW_TILE

    def start(r, carry):
        for k in range(MOE_TOPK):
            dst_row = pos_ref[k, base + r]
            pltpu.make_async_copy(tbuf.at[slot, r], xs_ref.at[dst_row], sem.at[par, k]).start(priority=k)
        return carry

    lax.fori_loop(0, ROW_TILE, start, 0, unroll=32)

    @pl.when(i > 0)
    def _():
        wait_rows(i - 1)

    @pl.when(i == n_steps - 1)
    def _():
        wait_rows(i)


def _dispatch(pos_t, zero_row, zero_on, n_used, hmw):
    grid_spec = pltpu.PrefetchScalarGridSpec(
        num_scalar_prefetch=4,
        grid=(N_TILES,),
        in_specs=[pl.BlockSpec(memory_space=pl.ANY)],
        out_specs=pl.BlockSpec(memory_space=pl.ANY),
        scratch_shapes=[pltpu.VMEM((MOE_TM, ROW_SUB, ROW_LANE), BF16),
                        pltpu.SemaphoreType.DMA((2, MOE_TOPK)), pltpu.SemaphoreType.DMA(()),
                        pltpu.VMEM((3, ROW_TILE, ROW_SUB, ROW_LANE), BF16), pltpu.SemaphoreType.DMA((3,))],
    )
    return pl.pallas_call(
        _dispatch_kernel,
        grid_spec=grid_spec,
        out_shape=jax.ShapeDtypeStruct((MOE_ROWS, ROW_SUB, ROW_LANE), BF16),
        compiler_params=_params(1),
        name="dispatch",
    )(pos_t, zero_row, zero_on, n_used, hmw)


def _expert_kernel(nu_ref, first_ref, ord_ref, oe_ref, no_ref, half_ref, x_ref, w1_hbm, w3_hbm, w2_hbm, y_ref,
                   w1b, w3b, w2b, sem):
    i = pl.program_id(0)
    n_used = nu_ref[0]
    n_ord = no_ref[0]

    def weight_copies(k):
        e = oe_ref[k]
        slot = k % W_SLOTS
        return (pltpu.make_async_copy(w1_hbm.at[e], w1b.at[slot], sem.at[0, slot]),
                pltpu.make_async_copy(w3_hbm.at[e], w3b.at[slot], sem.at[1, slot]),
                pltpu.make_async_copy(w2_hbm.at[e], w2b.at[slot], sem.at[2, slot]))

    def start_weights(k):
        for cp in weight_copies(k):
            cp.start(priority=1)

    @pl.when(i == 0)
    def _():
        for k in range(W_SLOTS - 1):
            @pl.when(k < n_ord)
            def _():
                start_weights(k)

    @pl.when(i < n_used)
    def _():
        k = ord_ref[i]

        @pl.when(first_ref[i] > 0)
        def _():
            for cp in weight_copies(k):
                cp.wait()

            @pl.when(k + (W_SLOTS - 1) < n_ord)
            def _():
                start_weights(k + (W_SLOTS - 1))

        slot = k % W_SLOTS

        def swiglu(rows):
            x = _from_row_tiles(x_ref[0:rows])
            h1 = _dot(x, w1b[slot].astype(BF16))
            h3 = _dot(x, w3b[slot].astype(BF16))
            he = (h1 * _sigmoid(h1) * h3).astype(BF16)
            y_ref[0:rows] = _to_row_tiles(_dot(he, w2b[slot].astype(BF16)))

        @pl.when(half_ref[i] == 0)
        def _():
            swiglu(MOE_TM)

        @pl.when(half_ref[i] > 0)
        def _():
            swiglu(MOE_TM // 2)
            y_ref[MOE_TM // 2:] = jnp.zeros((MOE_TM // 2, ROW_SUB, ROW_LANE), BF16)

    @pl.when(i >= n_used)
    def _():
        y_ref[...] = jnp.zeros_like(y_ref)


def _experts(sched, xs, w_e1, w_e3, w_e2):
    grid_spec = pltpu.PrefetchScalarGridSpec(
        num_scalar_prefetch=6,
        grid=(MOE_NT,),
        in_specs=[
            pl.BlockSpec((MOE_TM, ROW_SUB, ROW_LANE), lambda i, nu, *_: (jnp.minimum(i, nu[0] - 1), 0, 0)),
            pl.BlockSpec(memory_space=pl.ANY),
            pl.BlockSpec(memory_space=pl.ANY),
            pl.BlockSpec(memory_space=pl.ANY),
        ],
        out_specs=pl.BlockSpec((MOE_TM, ROW_SUB, ROW_LANE), lambda i, *_: (i, 0, 0)),
        scratch_shapes=[
            pltpu.VMEM((W_SLOTS, D_MODEL, MOE_FF), F32),
            pltpu.VMEM((W_SLOTS, D_MODEL, MOE_FF), F32),
            pltpu.VMEM((W_SLOTS, MOE_FF, D_MODEL), F32),
            pltpu.SemaphoreType.DMA((3, W_SLOTS)),
        ],
    )
    return pl.pallas_call(
        _expert_kernel,
        grid_spec=grid_spec,
        out_shape=jax.ShapeDtypeStruct((MOE_ROWS, ROW_SUB, ROW_LANE), BF16),
        compiler_params=_params(1),
        name="experts",
    )(sched["n_used"], sched["tile_first"], sched["tile_ord"], sched["ord_expert"], sched["n_ord"],
      sched["tile_half"], xs, w_e1, w_e3, w_e2)


def _combine_kernel(pos_ref, ys_ref, x2_ref, route_ref, g_ref, yp_ref, ysm_ref, gbuf, sem):
    i = pl.program_id(0)

    def start_gather(tile, slot):
        def body(r, carry):
            for k in range(MOE_TOPK):
                src_row = pos_ref[k, tile * ROW_TILE + r]
                pltpu.make_async_copy(ys_ref.at[src_row], gbuf.at[slot, k, r], sem.at[slot, k]).start(priority=k)
            return carry
        lax.fori_loop(0, ROW_TILE, body, 0, unroll=32)

    @pl.when(i == 0)
    def _():
        start_gather(0, 0)

    slot = i % 2

    @pl.when(i + 1 < pl.num_programs(0))
    def _():
        start_gather(i + 1, 1 - slot)

    for k in range(MOE_TOPK):
        pltpu.make_async_copy(ys_ref.at[pl.ds(0, ROW_TILE)], gbuf.at[slot, k], sem.at[slot, k]).wait()

    route = route_ref[...]
    lane = lax.broadcasted_iota(jnp.int32, route.shape, 1)
    w0 = jnp.sum(jnp.where(lane == 4, route, 0.0), axis=-1, keepdims=True)
    w1 = jnp.sum(jnp.where(lane == 5, route, 0.0), axis=-1, keepdims=True)
    g0 = _from_row_tiles(gbuf[slot, 0]).astype(F32)
    g1 = _from_row_tiles(gbuf[slot, 1]).astype(F32)
    x3 = x2_ref[...] + (g0 * w0 + g1 * w1)
    y = _rms(x3, g_ref[...])

    @pl.when(i < NP_TILES)
    def _():
        yp_ref[...] = y

    @pl.when(i == NP_TILES)
    def _():
        ysm_ref[...] = y.reshape(DEC_BATCH, DEC_SEQ, D_MODEL)


def _combine(pos, ys, x2, route, g_f):
    grid_spec = pltpu.PrefetchScalarGridSpec(
        num_scalar_prefetch=1,
        grid=(N_TILES,),
        in_specs=[
            pl.BlockSpec(memory_space=pl.ANY),
            pl.BlockSpec((ROW_TILE, D_MODEL), lambda i, pos: (i, 0)),
            pl.BlockSpec((ROW_TILE, ROUTE_LANES), lambda i, pos: (i, 0)),
            pl.BlockSpec((1, D_MODEL), lambda i, pos: (0, 0)),
        ],
        out_specs=[
            pl.BlockSpec((ROW_TILE, D_MODEL), lambda i, pos: (jnp.minimum(i, NP_TILES - 1), 0)),
            pl.BlockSpec((DEC_BATCH, DEC_SEQ, D_MODEL), lambda i, pos: (0, 0, 0)),
        ],
        scratch_shapes=[pltpu.VMEM((2, MOE_TOPK, ROW_TILE, ROW_SUB, ROW_LANE), BF16),
                        pltpu.SemaphoreType.DMA((2, MOE_TOPK))],
    )
    return pl.pallas_call(
        _combine_kernel,
        grid_spec=grid_spec,
        out_shape=[
            jax.ShapeDtypeStruct((N_P, D_MODEL), F32),
            jax.ShapeDtypeStruct((DEC_BATCH, DEC_SEQ, D_MODEL), F32),
        ],
        compiler_params=_params(1),
        name="combine_norm",
    )(pos, ys, x2, route, g_f)


_SCHEDULE_FIELDS = ("n_used", "tile_first", "tile_ord", "tile_half", "ord_expert", "n_ord", "zero_row", "zero_on")


def _schedule_kernel(cnt_ref, nu_ref, first_ref, ord_ref, half_ref, oe_ref, no_ref, zrow_ref, zon_ref):
    i32 = jnp.int32

    def clear_tile(t, carry):
        first_ref[t] = i32(0)
        ord_ref[t] = i32(0)
        half_ref[t] = i32(0)
        return carry

    lax.fori_loop(0, MOE_NT, clear_tile, 0)

    def clear_expert(e, carry):
        oe_ref[e] = i32(0)
        return carry

    lax.fori_loop(0, MOE_EXPERTS, clear_expert, 0)

    def expert(e, carry):
        t0, k = carry
        c = cnt_ref[0, MOE_GROUPS + e]
        nt = lax.shift_right_logical(c + (MOE_TM - 1), i32(LOG_MOE_TM))
        used = jnp.where(nt > 0, i32(1), i32(0))
        zon_ref[e] = used
        zrow_ref[e] = jnp.maximum(t0 + nt - 1, 0) * MOE_TM

        @pl.when(nt > 0)
        def _():
            oe_ref[k] = e

        def tile(j, cc):
            t = t0 + j
            first_ref[t] = jnp.where(j == 0, i32(1), i32(0))
            ord_ref[t] = k
            half_ref[t] = jnp.where(c - j * MOE_TM <= MOE_TM // 2, i32(1), i32(0))
            return cc

        lax.fori_loop(0, nt, tile, 0)
        return t0 + nt, k + used

    n_tiles, n_experts = lax.fori_loop(0, MOE_EXPERTS, expert, (i32(0), i32(0)))
    nu_ref[0] = n_tiles
    no_ref[0] = n_experts


def _expert_schedule(counts_i32):
    smem = pl.BlockSpec(memory_space=pltpu.SMEM)
    sizes = (1, MOE_NT, MOE_NT, MOE_NT, MOE_EXPERTS, 1, MOE_EXPERTS, MOE_EXPERTS)
    outs = pl.pallas_call(
        _schedule_kernel,
        in_specs=[smem],
        out_specs=[smem] * len(sizes),
        out_shape=[jax.ShapeDtypeStruct((n,), jnp.int32) for n in sizes],
        name="expert_schedule",
    )(counts_i32)
    return dict(zip(_SCHEDULE_FIELDS, outs))


def kernel(x_prompt, x_sample, mem_prompt, state_ret, cache_mem_k, cache_mem_v, norm_mix, w_in, ret_gn,
           sg_ln_g, sg_ln_b, sg_ws, sg_bs, w_a_out, w_b_out, w_o, norm_xa, w_cq, w_ck, w_cv, w_co, norm_moe,
           w_rg, b_rg, w_re, b_re, w_e1, w_e3, w_e2, norm_f):
    xp = x_prompt.reshape(N_P, D_MODEL)
    xs = x_sample

    h = _norm_rows(xp, xs, norm_mix)
    w = w_in[0]
    blocks = lambda m: m.reshape(MERGE_W_BLOCKS, D_MODEL // MERGE_W_BLOCKS, D_MODEL)
    qk, wa = _inproj("rope", IN_TM_ROPE, 0, 2 * RET_QK, h, w, _rope_tables(IN_TM_ROPE), side=blocks(w_a_out[0]))
    v, wb = _inproj("copy", IN_TM, 2 * RET_QK, RET_V, h, w, side=blocks(w_b_out[0]))
    gs, wo = _inproj("silu", IN_TM, 2 * RET_QK + RET_V, RET_V, h, w, side=blocks(w_o[0]))
    uv = _inproj("gelu", IN_TM, 2 * RET_QK + 2 * RET_V, 2 * SG_WIDTH, h, w)
    gab = _inproj("sigmoid", IN_TM, 2 * RET_QK + 2 * RET_V + 2 * SG_WIDTH, 2 * D_MODEL, h, w)

    a_p, ret_p, a_s, ret_s = _retention(qk, v, ret_gn, state_ret)

    b_all, sgv = _sgate(uv, sg_ln_g, sg_ln_b, sg_ws[0], sg_bs[0])

    square = lambda m: m.reshape(D_MODEL, D_MODEL)
    x1, qx = _merge(a_p, a_s, gs, b_all, gab, xp, xs, square(wa), square(wb), square(wo), norm_xa,
                    w_cq[0].astype(BF16))

    mk, mv, mkb, mvb = _memkv(mem_prompt.reshape(BATCH * MEM_LEN, D_MODEL), w_ck[0], w_cv[0])
    ck = cache_mem_k.reshape(DEC_BATCH, MEM_LEN * XA_HEADS, XA_DH)
    cv = cache_mem_v.reshape(DEC_BATCH, MEM_LEN * XA_HEADS, XA_DH)

    pad = ROUTE_LANES - MOE_GROUPS - MOE_EXPERTS
    w_r = jnp.concatenate([w_rg[0], w_re[0], jnp.zeros((D_MODEL, pad), F32)], axis=1)
    b_r = jnp.concatenate([b_rg[0], b_re[0], jnp.zeros((pad,), F32)]).reshape(1, ROUTE_LANES)
    x2, hmw, route, counts, counts_i32 = _route(qx, mkb, mvb, ck, cv, x1, w_co[0].astype(BF16), norm_moe,
                                                w_r.astype(BF16), b_r)

    pos_t = _positions(route, counts)
    sched = _expert_schedule(counts_i32)
    xs_sorted = _dispatch(pos_t, sched["zero_row"], sched["zero_on"], sched["n_used"], hmw)
    ys = _experts(sched, xs_sorted, w_e1[0], w_e3[0], w_e2[0])
    y_p, y_s = _combine(pos_t, ys, x2, route, norm_f.reshape(1, D_MODEL))

    return (y_p.reshape(BATCH, SEQ, D_MODEL), y_s, ret_p, mk, mv, ret_s,
            sgv.reshape(1, DEC_BATCH, DEC_SEQ, SG_WIDTH))
```

```python
import functools

import jax
import jax.numpy as jnp
import numpy as np
from jax import lax
from jax.experimental import pallas as pl
from jax.experimental.pallas import tpu as pltpu

F32 = jnp.float32
BF16 = jnp.bfloat16

D_MODEL = 2048
BATCH = 4
SEQ = 2048
DEC_BATCH = 128
DEC_SEQ = 4
PAST_LEN = 16384
RET_HEADS = 8
RET_DK = 128
RET_DV = 256
ROPE_BASE = 10000.0
RET_QK = RET_HEADS * RET_DK
RET_V = RET_HEADS * RET_DV
SG_GROUPS = 4
SG_WIDTH = 2048
SG_CHUNK = 128
MEM_LEN = 256
XA_HEADS = 4
XA_DH = 128
XA_W = XA_HEADS * XA_DH
MOE_GROUPS = 4
MOE_PER_GROUP = 8
MOE_EXPERTS = MOE_GROUPS * MOE_PER_GROUP
MOE_TOPK = 2
MOE_FF = 512
EPS = 1e-6

N_P = BATCH * SEQ
N_S = DEC_BATCH * DEC_SEQ
N_ALL = N_P + N_S
ROW_TILE = 512
N_TILES = N_ALL // ROW_TILE
NP_TILES = N_P // ROW_TILE
MERGE_TILE = 256
MERGE_P_TILES = N_P // MERGE_TILE

IN_TM = 1024
IN_TN = 2048
MERGE_W_BLOCKS = 4

PROMPT_CHUNK = 256
RET_BB = DEC_BATCH // (BATCH * (SEQ // PROMPT_CHUNK))
RET_ROWS = RET_BB * DEC_SEQ
LOG_DEC_SEQ = 2
SAMPLE_BB = DEC_BATCH // NP_TILES
SAMPLE_ROWS = SAMPLE_BB * DEC_SEQ

MOE_TM = 256
LOG_MOE_TM = 8
MOE_NT = (N_ALL * MOE_TOPK + MOE_EXPERTS * (MOE_TM - 1) + MOE_TM - 1) // MOE_TM
MOE_ROWS = MOE_NT * MOE_TM
ROUTE_LANES = 128
ROW_SUB, ROW_LANE = 16, 128
W_SLOTS = 3

VMEM_LIMIT = 56 * 1024 * 1024


def _params(n_axes, vmem=VMEM_LIMIT):
    return pltpu.CompilerParams(dimension_semantics=("arbitrary",) * n_axes,
                                vmem_limit_bytes=vmem)


def _rms(x, g):
    ms = jnp.mean(x * x, axis=-1, keepdims=True)
    return (x * lax.rsqrt(ms + EPS)) * g


def _dot(a, b):
    return jnp.dot(a, b, preferred_element_type=F32)


def _sigmoid(x):
    return 0.5 * jnp.tanh(0.5 * x) + 0.5


def _silu(x):
    hx = 0.5 * x
    return hx + hx * jnp.tanh(hx)


_GELU_C0 = float(np.float32(np.sqrt(2.0 / np.pi)))


def _gelu_tanh(x):
    hx = 0.5 * x
    return hx + hx * jnp.tanh(x * (_GELU_C0 + (_GELU_C0 * 0.044715) * (x * x)))


def _to_row_tiles(x):
    return x.astype(BF16).reshape(x.shape[0], ROW_SUB, ROW_LANE)


def _from_row_tiles(t):
    return t.reshape(t.shape[0], D_MODEL)


def _dot_nt(a, b):
    return lax.dot_general(a, b, (((1,), (1,)), ((), ())), preferred_element_type=F32)


def _dot_tn(a, b):
    return lax.dot_general(a, b, (((0,), (0,)), ((), ())), preferred_element_type=F32)


def _norm_kernel(xp_ref, xs_ref, g_ref, h_ref):
    i = pl.program_id(0)

    @pl.when(i < NP_TILES)
    def _():
        h_ref[...] = _rms(xp_ref[...], g_ref[...]).astype(BF16)

    @pl.when(i == NP_TILES)
    def _():
        h_ref[...] = _rms(xs_ref[...].reshape(N_S, D_MODEL), g_ref[...]).astype(BF16)


def _norm_rows(xp, xs, g):
    return pl.pallas_call(
        _norm_kernel,
        grid=(N_TILES,),
        in_specs=[
            pl.BlockSpec((ROW_TILE, D_MODEL), lambda i: (jnp.minimum(i, NP_TILES - 1), 0)),
            pl.BlockSpec((DEC_BATCH, DEC_SEQ, D_MODEL), lambda i: (0, 0, 0)),
            pl.BlockSpec((1, D_MODEL), lambda i: (0, 0)),
        ],
        out_specs=pl.BlockSpec((ROW_TILE, D_MODEL), lambda i: (i, 0)),
        out_shape=jax.ShapeDtypeStruct((N_ALL, D_MODEL), BF16),
        compiler_params=_params(1),
        name="norm_rows",
    )(xp, xs, g)


def _inproj_kernel(kind, tm, n_side, h_ref, w_ref, *rest):
    rest = list(rest)
    cos_ref, sin_ref = (rest.pop(0), rest.pop(0)) if kind == "rope" else (None, None)
    side_ref = rest.pop(0) if n_side else None
    z_ref = rest.pop(0)
    side_out_ref = rest.pop(0) if n_side else None
    j = pl.program_id(0)
    i = pl.program_id(1)
    last = N_P // tm

    if n_side:
        @pl.when(j * (last + 1) + i < n_side)
        def _():
            side_out_ref[...] = side_ref[...].astype(BF16)

    def tile(rows):
        acc = _dot(h_ref[0:rows, :], w_ref[...].astype(BF16))
        if kind == "rope":
            c = cos_ref[0:rows, :]
            s = sin_ref[0:rows, :]
            heads_per_block = IN_TN // RET_DK
            for hb in range(heads_per_block):
                scale = jnp.where(j * heads_per_block + hb >= RET_HEADS, RET_DK ** -0.5, 1.0).astype(F32)
                cols = slice(hb * RET_DK, (hb + 1) * RET_DK)
                a = acc[:, cols]
                r = pltpu.roll(a, RET_DK // 2, axis=1)
                z_ref[0:rows, cols] = ((a * c + r * s) * scale).astype(BF16)
        elif kind == "copy":
            z_ref[0:rows, :] = acc.astype(BF16)
        elif kind == "silu":
            z_ref[0:rows, :] = _silu(acc).astype(BF16)
        elif kind == "gelu":
            z_ref[0:rows, :] = _gelu_tanh(acc).astype(BF16)
        else:
            z_ref[0:rows, :] = _sigmoid(acc).astype(BF16)

    @pl.when(i < last)
    def _():
        tile(tm)

    @pl.when(i == last)
    def _():
        tile(N_S)


def _inproj(kind, tm, col0, width, h, w_in, tables=(), side=None):
    last = N_P // tm
    n_j = width // IN_TN
    tab_idx = lambda j, i: (jnp.where(i < last, i % (SEQ // tm), SEQ // tm), 0)
    j0 = col0 // IN_TN
    in_specs = [
        pl.BlockSpec((tm, D_MODEL), lambda j, i: (i, 0)),
        pl.BlockSpec((D_MODEL, IN_TN), lambda j, i: (0, j0 + j)),
    ] + [pl.BlockSpec((tm, RET_DK), tab_idx) for _ in tables]
    out_specs = [pl.BlockSpec((tm, IN_TN), lambda j, i: (i, j))]
    out_shape = [jax.ShapeDtypeStruct((N_ALL, width), BF16)]
    n_side = 0 if side is None else side.shape[0]
    if n_side:
        assert n_side <= n_j * (last + 1)
        blk = (1,) + side.shape[1:]
        side_blk = lambda j, i: (jnp.minimum(j * (last + 1) + i, n_side - 1), 0, 0)
        in_specs.append(pl.BlockSpec(blk, side_blk))
        out_specs.append(pl.BlockSpec(blk, side_blk))
        out_shape.append(jax.ShapeDtypeStruct(side.shape, BF16))
    outs = pl.pallas_call(
        functools.partial(_inproj_kernel, kind, tm, n_side),
        grid=(n_j, last + 1),
        in_specs=in_specs,
        out_specs=out_specs,
        out_shape=out_shape,
        compiler_params=_params(2),
        name="in_proj_" + kind,
    )(h, w_in, *tables, *(() if side is None else (side,)))
    return outs if n_side else outs[0]


def _rope_tables(tm):
    half = RET_DK // 2
    inv = ROPE_BASE ** (-jnp.arange(half, dtype=F32) / half)

    def tab(pos):
        ang = pos.astype(F32)[:, None] * inv[None, :]
        c, s = jnp.cos(ang), jnp.sin(ang)
        return jnp.concatenate([c, c], -1), jnp.concatenate([-s, s], -1)

    cp, sp = tab(jnp.arange(SEQ, dtype=jnp.int32))
    cs, ss = tab(PAST_LEN + jnp.arange(DEC_SEQ, dtype=jnp.int32))
    cs = jnp.tile(cs, (DEC_BATCH, 1))
    ss = jnp.tile(ss, (DEC_BATCH, 1))
    pad = jnp.zeros((tm - N_S, RET_DK), F32)
    return jnp.concatenate([cp, cs, pad], 0), jnp.concatenate([sp, ss, pad], 0)


def _decay_tables(chunk):
    lg = jnp.log1p(-jnp.power(2.0, -5.0 - jnp.arange(RET_HEADS, dtype=F32)))
    idx = jnp.arange(chunk, dtype=F32)
    rel = idx[:, None] - idx[None, :]
    dmask = jnp.where(rel >= 0, jnp.exp(lg[:, None, None] * jnp.maximum(rel, 0.0)), 0.0).astype(F32)
    xi = jnp.exp(lg[:, None] * (idx[None, :] + 1.0)).astype(F32)
    zeta = jnp.exp(lg[:, None] * (chunk - 1.0 - idx[None, :])).astype(F32)
    gc = jnp.exp(lg * chunk).astype(F32)
    return dmask, xi, zeta, gc


def _head_norm(o, gn):
    mu = jnp.mean(o, axis=-1, keepdims=True)
    d = o - mu
    var = jnp.mean(d * d, axis=-1, keepdims=True)
    return ((d * lax.rsqrt(var + EPS)) * gn).astype(BF16)


def _ret_kernel(gcp_ref, gcs_ref, q_ref, k_ref, v_ref, gn_ref, dm_ref, xi_ref, zt_ref,
                qs_ref, ks_ref, vs_ref, dms_ref, xis_ref, zts_ref, s0_ref,
                a_ref, sfin_ref, as_ref, s1_ref, s_ref):
    c = pl.program_id(1)

    @pl.when(c == 0)
    def _():
        s_ref[...] = jnp.zeros_like(s_ref)

    rows_k = lax.broadcasted_iota(jnp.int32, (RET_ROWS, RET_DK), 0) >> LOG_DEC_SEQ
    rows_v = lax.broadcasted_iota(jnp.int32, (RET_ROWS, RET_DV), 0) >> LOG_DEC_SEQ
    for h in range(RET_HEADS):
        kc = slice(h * RET_DK, (h + 1) * RET_DK)
        vc = slice(h * RET_DV, (h + 1) * RET_DV)
        gn = gn_ref[:, vc]

        qh = q_ref[:, kc]
        kh = k_ref[:, kc]
        vh = v_ref[:, vc]
        inner = _dot_nt(qh, kh) * dm_ref[h]
        o = _dot(inner.astype(BF16), vh)
        s_old = s_ref[h]
        xi = xi_ref[h]
        o = o + _dot(qh, s_old.astype(BF16)) * jnp.concatenate([xi, xi], axis=1)
        kz = (kh.astype(F32) * zt_ref[h]).astype(BF16)
        s_ref[h] = gcp_ref[h] * s_old + _dot_tn(kz, vh)
        a_ref[:, vc] = _head_norm(o, gn)

        qh = qs_ref[:, kc]
        kh = ks_ref[:, kc]
        vh = vs_ref[:, vc]
        inner = _dot_nt(qh, kh) * dms_ref[h]
        o = _dot(inner.astype(BF16), vh)
        xi = xis_ref[h]
        xi2 = jnp.concatenate([xi, xi], axis=1)
        kz = kh.astype(F32) * zts_ref[h]
        gch = gcs_ref[h]
        for b in range(RET_BB):
            s_old = s0_ref[0, b, h]
            cross = _dot(qh, s_old.astype(BF16)) * xi2
            o = o + jnp.where(rows_v == b, cross, 0.0)
            kz_b = jnp.where(rows_k == b, kz, 0.0).astype(BF16)
            s1_ref[0, b, h] = gch * s_old + _dot_tn(kz_b, vh)
        as_ref[:, vc] = _head_norm(o, gn)

    @pl.when(c == pl.num_programs(1) - 1)
    def _():
        sfin_ref[0, 0] = s_ref[...]


def _retention(qk, v, ret_gn, state):
    chunk = PROMPT_CHUNK
    n_chunks = SEQ // chunk
    dmask, xi, zeta, gc = _decay_tables(chunk)
    xi_b = jnp.broadcast_to(xi[:, :, None], (RET_HEADS, chunk, RET_DK))
    zeta_b = jnp.broadcast_to(zeta[:, :, None], (RET_HEADS, chunk, RET_DK))
    dmask_s, xi_s, zeta_s, gc_s = _decay_tables(DEC_SEQ)
    eye = jnp.eye(RET_BB, dtype=F32)
    dm_big = jax.vmap(lambda m: jnp.kron(eye, m))(dmask_s)
    xi_sb = jnp.broadcast_to(jnp.tile(xi_s, (1, RET_BB))[:, :, None], (RET_HEADS, RET_ROWS, RET_DK))
    zeta_sb = jnp.broadcast_to(jnp.tile(zeta_s, (1, RET_BB))[:, :, None], (RET_HEADS, RET_ROWS, RET_DK))

    row = lambda b, c: b * n_chunks + c
    srow = lambda b, c: N_P // RET_ROWS + row(b, c)
    const3 = lambda b, c: (0, 0, 0)
    smem = pl.BlockSpec(memory_space=pltpu.SMEM)
    st_spec = pl.BlockSpec((1, RET_BB, RET_HEADS, RET_DK, RET_DV), lambda b, c: (0, row(b, c), 0, 0, 0))
    return pl.pallas_call(
        _ret_kernel,
        grid=(BATCH, n_chunks),
        in_specs=[
            smem, smem,
            pl.BlockSpec((chunk, RET_QK), lambda b, c: (row(b, c), 0)),
            pl.BlockSpec((chunk, RET_QK), lambda b, c: (row(b, c), 1)),
            pl.BlockSpec((chunk, RET_V), lambda b, c: (row(b, c), 0)),
            pl.BlockSpec((1, RET_V), lambda b, c: (0, 0)),
            pl.BlockSpec((RET_HEADS, chunk, chunk), const3),
            pl.BlockSpec((RET_HEADS, chunk, RET_DK), const3),
            pl.BlockSpec((RET_HEADS, chunk, RET_DK), const3),
            pl.BlockSpec((RET_ROWS, RET_QK), lambda b, c: (srow(b, c), 0)),
            pl.BlockSpec((RET_ROWS, RET_QK), lambda b, c: (srow(b, c), 1)),
            pl.BlockSpec((RET_ROWS, RET_V), lambda b, c: (srow(b, c), 0)),
            pl.BlockSpec((RET_HEADS, RET_ROWS, RET_ROWS), const3),
            pl.BlockSpec((RET_HEADS, RET_ROWS, RET_DK), const3),
            pl.BlockSpec((RET_HEADS, RET_ROWS, RET_DK), const3),
            st_spec,
        ],
        out_specs=[
            pl.BlockSpec((chunk, RET_V), lambda b, c: (row(b, c), 0)),
            pl.BlockSpec((1, 1, RET_HEADS, RET_DK, RET_DV), lambda b, c: (0, b, 0, 0, 0)),
            pl.BlockSpec((RET_ROWS, RET_V), lambda b, c: (row(b, c), 0)),
            st_spec,
        ],
        out_shape=[
            jax.ShapeDtypeStruct((N_P, RET_V), BF16),
            jax.ShapeDtypeStruct((1, BATCH, RET_HEADS, RET_DK, RET_DV), F32),
            jax.ShapeDtypeStruct((N_S, RET_V), BF16),
            jax.ShapeDtypeStruct((1, DEC_BATCH, RET_HEADS, RET_DK, RET_DV), F32),
        ],
        scratch_shapes=[pltpu.VMEM((RET_HEADS, RET_DK, RET_DV), F32)],
        compiler_params=_params(2),
        name="retention",
    )(gc, gc_s, qk, qk, v, ret_gn, dmask, xi_b, zeta_b, qk, qk, v, dm_big, xi_sb, zeta_sb, state)


def _sgate_kernel(u_ref, v_ref, lg_ref, lb_ref, wp_ref, bp_ref, ws_ref, bs_ref, b_ref, sgv_ref, vln_ref):
    i = pl.program_id(0)
    v = v_ref[...].astype(F32)
    mu = jnp.mean(v, axis=-1, keepdims=True)
    d = v - mu
    var = jnp.mean(d * d, axis=-1, keepdims=True)
    vln_ref[...] = (d * lax.rsqrt(var + EPS)) * lg_ref[...] + lb_ref[...]
    gw = SG_WIDTH // SG_GROUPS
    lane_reps = gw // 128

    @pl.when(i < NP_TILES)
    def _():
        r = lax.broadcasted_iota(jnp.int32, (SG_CHUNK, SG_CHUNK), 0)
        c = lax.broadcasted_iota(jnp.int32, (SG_CHUNK, SG_CHUNK), 1)
        for g in range(SG_GROUPS):
            cols = slice(g * gw, (g + 1) * gw)
            w = jnp.where(c <= r, wp_ref[g], 0.0).astype(BF16)
            bias = jnp.concatenate([bp_ref[g]] * lane_reps, axis=1)
            chunks = [slice(ch * SG_CHUNK, (ch + 1) * SG_CHUNK) for ch in range(ROW_TILE // SG_CHUNK)]
            mixed = [_dot(w, vln_ref[rows, cols].astype(BF16)) for rows in chunks]
            for rows, m in zip(chunks, mixed):
                b_ref[rows, cols] = (u_ref[rows, cols].astype(F32) * (m + bias)).astype(BF16)

    @pl.when(i == NP_TILES)
    def _():
        sgv_ref[...] = vln_ref[...].reshape(DEC_BATCH, DEC_SEQ, SG_WIDTH)
        r = lax.broadcasted_iota(jnp.int32, (ROW_TILE, ROW_TILE), 0)
        c = lax.broadcasted_iota(jnp.int32, (ROW_TILE, ROW_TILE), 1)
        keep = ((r >> LOG_DEC_SEQ) == (c >> LOG_DEC_SEQ)) & (c <= r)
        for g in range(SG_GROUPS):
            cols = slice(g * gw, (g + 1) * gw)
            w_rows = jnp.concatenate([ws_ref[g]] * (ROW_TILE // 8), axis=0)
            w_full = jnp.concatenate([w_rows] * (ROW_TILE // 128), axis=1)
            w = jnp.where(keep, w_full, 0.0).astype(BF16)
            b_rows = jnp.concatenate([bs_ref[g]] * (ROW_TILE // 8), axis=0)
            bias = jnp.concatenate([b_rows] * lane_reps, axis=1)
            mixed = _dot(w, vln_ref[:, cols].astype(BF16)) + bias
            b_ref[:, cols] = (u_ref[:, cols].astype(F32) * mixed).astype(BF16)


def _sgate(uv, ln_g, ln_b, sg_ws, sg_bs):
    b_p = jnp.broadcast_to(sg_bs[:, :, None], (SG_GROUPS, SG_CHUNK, 128))
    w_s = jnp.tile(sg_ws[:, :DEC_SEQ, :DEC_SEQ], (1, 8 // DEC_SEQ, 128 // DEC_SEQ))
    b_s = jnp.broadcast_to(jnp.tile(sg_bs[:, :DEC_SEQ], (1, 8 // DEC_SEQ))[:, :, None], (SG_GROUPS, 8, 128))
    const3 = lambda i: (0, 0, 0)
    return pl.pallas_call(
        _sgate_kernel,
        grid=(N_TILES,),
        in_specs=[
            pl.BlockSpec((ROW_TILE, SG_WIDTH), lambda i: (i, 0)),
            pl.BlockSpec((ROW_TILE, SG_WIDTH), lambda i: (i, 1)),
            pl.BlockSpec((1, SG_WIDTH), lambda i: (0, 0)),
            pl.BlockSpec((1, SG_WIDTH), lambda i: (0, 0)),
            pl.BlockSpec((SG_GROUPS, SG_CHUNK, SG_CHUNK), const3),
            pl.BlockSpec((SG_GROUPS, SG_CHUNK, 128), const3),
            pl.BlockSpec((SG_GROUPS, 8, 128), const3),
            pl.BlockSpec((SG_GROUPS, 8, 128), const3),
        ],
        out_specs=[
            pl.BlockSpec((ROW_TILE, SG_WIDTH), lambda i: (i, 0)),
            pl.BlockSpec((DEC_BATCH, DEC_SEQ, SG_WIDTH), lambda i: (0, 0, 0)),
        ],
        out_shape=[
            jax.ShapeDtypeStruct((N_ALL, SG_WIDTH), BF16),
            jax.ShapeDtypeStruct((DEC_BATCH, DEC_SEQ, SG_WIDTH), F32),
        ],
        scratch_shapes=[pltpu.VMEM((ROW_TILE, SG_WIDTH), F32)],
        compiler_params=_params(1),
        name="spatial_gate",
    )(uv, uv, ln_g, ln_b, sg_ws, b_p, w_s, b_s)


def _merge_kernel(ap_ref, as_ref, gs_ref, b_ref, ga_ref, gb_ref, xp_ref, xs_ref, wa_ref, wb_ref, wo_ref,
                  g_ref, wq_ref, x1_ref, q_ref):
    i = pl.program_id(0)

    def run(a_norm, x):
        b = _dot(b_ref[...], wb_ref[...])
        a_in = (gs_ref[...].astype(F32) * a_norm.astype(F32)).astype(BF16)
        a = _dot(a_in, wa_ref[...])
        merged = ga_ref[...].astype(F32) * a + gb_ref[...].astype(F32) * b
        x1 = x + _dot(merged.astype(BF16), wo_ref[...])
        x1_ref[...] = x1
        q_ref[...] = _dot(_rms(x1, g_ref[...]).astype(BF16), wq_ref[...]).astype(BF16)

    @pl.when(i < MERGE_P_TILES)
    def _():
        run(ap_ref[...], xp_ref[...])

    @pl.when(i >= MERGE_P_TILES)
    def _():
        run(as_ref[...], xs_ref[...].reshape(MERGE_TILE, D_MODEL))


def _resident(shape):
    return pl.BlockSpec(shape, lambda i: (0,) * len(shape), pipeline_mode=pl.Buffered(1))


def _merge(a_p, a_s, gs, b_all, gab, xp, xs, wa, wb, wo, g_xa, wq):
    tm = MERGE_TILE
    prompt_tile = lambda i: (jnp.minimum(i, MERGE_P_TILES - 1), 0)
    sample_tile = lambda i: (jnp.maximum(i - MERGE_P_TILES, 0), 0)
    return pl.pallas_call(
        _merge_kernel,
        grid=(N_ALL // tm,),
        in_specs=[
            pl.BlockSpec((tm, RET_V), prompt_tile),
            pl.BlockSpec((tm, RET_V), sample_tile),
            pl.BlockSpec((tm, RET_V), lambda i: (i, 0)),
            pl.BlockSpec((tm, SG_WIDTH), lambda i: (i, 0)),
            pl.BlockSpec((tm, D_MODEL), lambda i: (i, 0)),
            pl.BlockSpec((tm, D_MODEL), lambda i: (i, 1)),
            pl.BlockSpec((tm, D_MODEL), prompt_tile),
            pl.BlockSpec((tm // DEC_SEQ, DEC_SEQ, D_MODEL), lambda i: (jnp.maximum(i - MERGE_P_TILES, 0), 0, 0)),
            _resident((RET_V, D_MODEL)),
            _resident((SG_WIDTH, D_MODEL)),
            _resident((D_MODEL, D_MODEL)),
            pl.BlockSpec((1, D_MODEL), lambda i: (0, 0)),
            _resident((D_MODEL, XA_W)),
        ],
        out_specs=[
            pl.BlockSpec((tm, D_MODEL), lambda i: (i, 0)),
            pl.BlockSpec((tm, XA_W), lambda i: (i, 0)),
        ],
        out_shape=[
            jax.ShapeDtypeStruct((N_ALL, D_MODEL), F32),
            jax.ShapeDtypeStruct((N_ALL, XA_W), BF16),
        ],
        compiler_params=_params(1),
        name="merge_proj",
    )(a_p, a_s, gs, b_all, gab, gab, xp, xs, wa, wb, wo, g_xa, wq)


def _memkv_kernel(m_ref, wk_ref, wv_ref, k_ref, v_ref, kb_ref, vb_ref):
    m = m_ref[...].astype(BF16)
    k = _dot(m, wk_ref[...].astype(BF16))
    v = _dot(m, wv_ref[...].astype(BF16))
    k_ref[0, 0] = k.reshape(MEM_LEN, XA_HEADS, XA_DH)
    v_ref[0, 0] = v.reshape(MEM_LEN, XA_HEADS, XA_DH)
    kb_ref[...] = k.astype(BF16)
    vb_ref[...] = v.astype(BF16)


def _memkv(mem, w_ck, w_cv):
    rows = BATCH * MEM_LEN
    spec = pl.BlockSpec((MEM_LEN, XA_W), lambda b: (b, 0))
    spec5 = pl.BlockSpec((1, 1, MEM_LEN, XA_HEADS, XA_DH), lambda b: (0, b, 0, 0, 0))
    shape5 = jax.ShapeDtypeStruct((1, BATCH, MEM_LEN, XA_HEADS, XA_DH), F32)
    wspec = pl.BlockSpec((D_MODEL, XA_W), lambda b: (0, 0))
    return pl.pallas_call(
        _memkv_kernel,
        grid=(BATCH,),
        in_specs=[pl.BlockSpec((MEM_LEN, D_MODEL), lambda b: (b, 0)), wspec, wspec],
        out_specs=[spec5, spec5, spec, spec],
        out_shape=[shape5, shape5] + [jax.ShapeDtypeStruct((rows, XA_W), BF16)] * 2,
        compiler_params=_params(1),
        name="mem_kv",
    )(mem, w_ck, w_cv)


def _softmax_rows(s):
    m = jnp.max(s, axis=-1, keepdims=True)
    e = jnp.exp(s - m)
    return e / jnp.sum(e, axis=-1, keepdims=True)


def _xattn_prompt_rows(q_ref, k_ref, v_ref, after_scores=lambda: None):
    cols = [slice(h * XA_DH, (h + 1) * XA_DH) for h in range(XA_HEADS)]
    scores = [_dot_nt(q_ref[:, c], k_ref[:, c]) * (XA_DH ** -0.5) for c in cols]
    after_scores()
    probs = [_softmax_rows(s).astype(BF16) for s in scores]
    heads = [_dot(p, v_ref[:, c]).astype(BF16) for p, c in zip(probs, cols)]
    return jnp.concatenate(heads, axis=1)


def _xattn_sample_scores(q_ref, k_ref):
    qf = q_ref[...].astype(F32)
    scores = []
    for b in range(SAMPLE_BB):
        rows = slice(b * DEC_SEQ, (b + 1) * DEC_SEQ)
        qb = jnp.concatenate([qf[rows, h * XA_DH:(h + 1) * XA_DH] for h in range(XA_HEADS)], axis=0)
        scores.append(_dot_nt(qb.astype(BF16), k_ref[b].astype(BF16)) * (XA_DH ** -0.5))
    return scores


def _xattn_sample_finish(scores, v_ref, o_ref):
    n_q = XA_HEADS * DEC_SEQ
    n_kv = MEM_LEN * XA_HEADS
    r = lax.broadcasted_iota(jnp.int32, (n_q, n_kv), 0)
    c = lax.broadcasted_iota(jnp.int32, (n_q, n_kv), 1)
    head_ok = (r >> LOG_DEC_SEQ) == (c & (XA_HEADS - 1))
    probs = [_softmax_rows(jnp.where(head_ok, s, -1e30)).astype(BF16) for s in scores]
    for b in range(SAMPLE_BB):
        rows = slice(b * DEC_SEQ, (b + 1) * DEC_SEQ)
        o = _dot(probs[b], v_ref[b].astype(BF16))
        for h in range(XA_HEADS):
            o_ref[h, rows, :] = o[h * DEC_SEQ:(h + 1) * DEC_SEQ, :]


def _route_kernel(qp_ref, mk_ref, mv_ref, qs_ref, ck_ref, cv_ref, x1_ref, wo_ref, g_ref, wr_ref, br_ref,
                  x2_ref, hm_ref, route_ref, cnt_ref, cnti_ref, run_ref, os_ref):
    i = pl.program_id(0)

    @pl.when(i == 0)
    def _():
        run_ref[...] = jnp.zeros_like(run_ref)

    def run(o, between=lambda: None):
        x2 = x1_ref[...] + _dot(o, wo_ref[...])
        x2_ref[...] = x2
        hm = _rms(x2, g_ref[...])
        hm_ref[...] = _to_row_tiles(hm)
        logits = _dot(hm.astype(BF16), wr_ref[...]) + br_ref[...]
        lane = lax.broadcasted_iota(jnp.int32, logits.shape, 1)
        lane_f = lane.astype(F32)
        neg = jnp.float32(-jnp.inf)
        big = jnp.float32(1 << 20)
        is_g = lane < MOE_GROUPS
        gl = jnp.where(is_g, logits, neg)
        gmax = jnp.max(gl, axis=-1, keepdims=True)
        g_sel = jnp.min(jnp.where(gl == gmax, lane_f, big), axis=-1, keepdims=True)
        g_w = 1.0 / jnp.sum(jnp.where(is_g, jnp.exp(logits - gmax), 0.0), axis=-1, keepdims=True)
        between()
        e_lane = lane - MOE_GROUPS
        e_group = (e_lane >> 3).astype(F32)
        in_grp = (e_lane >= 0) & (e_lane < MOE_EXPERTS) & (e_group == g_sel)
        el = jnp.where(in_grp, logits, neg)
        v0 = jnp.max(el, axis=-1, keepdims=True)
        i0 = jnp.min(jnp.where(el == v0, lane_f, big), axis=-1, keepdims=True)
        el1 = jnp.where(lane_f == i0, neg, el)
        v1 = jnp.max(el1, axis=-1, keepdims=True)
        i1 = jnp.min(jnp.where(el1 == v1, lane_f, big), axis=-1, keepdims=True)
        ex = jnp.exp(v1 - v0)
        den = 1.0 + ex
        w0 = (1.0 / den) * g_w
        w1 = (ex / den) * g_w
        a0 = (lane_f == i0).astype(F32)
        a1 = (lane_f == i1).astype(F32)
        a = a0 + a1
        rr = lax.broadcasted_iota(jnp.int32, (ROW_TILE, ROW_TILE), 0)
        cc = lax.broadcasted_iota(jnp.int32, (ROW_TILE, ROW_TILE), 1)
        lower = jnp.where(cc < rr, 1.0, 0.0).astype(BF16)
        before = _dot(lower, a.astype(BF16)) + run_ref[...]
        rank0 = jnp.sum(before * a0, axis=-1, keepdims=True)
        rank1 = jnp.sum(before * a1, axis=-1, keepdims=True)
        run_ref[...] += jnp.sum(a, axis=0, keepdims=True)
        e0 = i0 - MOE_GROUPS
        e1 = i1 - MOE_GROUPS
        route = jnp.where(lane == 0, e0, 0.0)
        route = jnp.where(lane == 1, e1, route)
        route = jnp.where(lane == 2, rank0, route)
        route = jnp.where(lane == 3, rank1, route)
        route = jnp.where(lane == 4, w0, route)
        route = jnp.where(lane == 5, w1, route)
        route_ref[...] = route
        cnt_ref[...] = run_ref[...]
        cnti_ref[...] = run_ref[...].astype(jnp.int32)

    @pl.when(i < NP_TILES)
    def _():
        scores = []
        o_prompt = _xattn_prompt_rows(qp_ref, mk_ref, mv_ref,
                                      after_scores=lambda: scores.extend(_xattn_sample_scores(qs_ref, ck_ref)))
        run(o_prompt, between=lambda: _xattn_sample_finish(scores, cv_ref, os_ref.at[i]))

    @pl.when(i == NP_TILES)
    def _():
        heads = [os_ref[:, h].reshape(N_S, XA_DH) for h in range(XA_HEADS)]
        run(jnp.concatenate(heads, axis=1).astype(BF16))


def _route(qx, mkb, mvb, ck, cv, x1, w_co, g_moe, w_r, b_r):
    prompt_step = lambda i: jnp.minimum(i, NP_TILES - 1)
    kv = pl.BlockSpec((SAMPLE_BB, MEM_LEN * XA_HEADS, XA_DH), lambda i: (prompt_step(i), 0, 0))
    mem_kv = pl.BlockSpec((MEM_LEN, XA_W), lambda i: (prompt_step(i) // (SEQ // ROW_TILE), 0))
    return pl.pallas_call(
        _route_kernel,
        grid=(N_TILES,),
        in_specs=[
            pl.BlockSpec((ROW_TILE, XA_W), lambda i: (prompt_step(i), 0)),
            mem_kv, mem_kv,
            pl.BlockSpec((SAMPLE_ROWS, XA_W), lambda i: (N_P // SAMPLE_ROWS + prompt_step(i), 0)),
            kv, kv,
            pl.BlockSpec((ROW_TILE, D_MODEL), lambda i: (i, 0)),
            _resident((XA_W, D_MODEL)),
            pl.BlockSpec((1, D_MODEL), lambda i: (0, 0)),
            _resident((D_MODEL, ROUTE_LANES)),
            pl.BlockSpec((1, ROUTE_LANES), lambda i: (0, 0)),
        ],
        out_specs=[
            pl.BlockSpec((ROW_TILE, D_MODEL), lambda i: (i, 0)),
            pl.BlockSpec((ROW_TILE, ROW_SUB, ROW_LANE), lambda i: (i, 0, 0)),
            pl.BlockSpec((ROW_TILE, ROUTE_LANES), lambda i: (i, 0)),
            pl.BlockSpec((1, ROUTE_LANES), lambda i: (0, 0)),
            pl.BlockSpec((1, ROUTE_LANES), lambda i: (0, 0)),
        ],
        out_shape=[
            jax.ShapeDtypeStruct((N_ALL, D_MODEL), F32),
            jax.ShapeDtypeStruct((N_ALL, ROW_SUB, ROW_LANE), BF16),
            jax.ShapeDtypeStruct((N_ALL, ROUTE_LANES), F32),
            jax.ShapeDtypeStruct((1, ROUTE_LANES), F32),
            jax.ShapeDtypeStruct((1, ROUTE_LANES), jnp.int32),
        ],
        scratch_shapes=[pltpu.VMEM((1, ROUTE_LANES), F32),
                        pltpu.VMEM((NP_TILES, XA_HEADS, SAMPLE_ROWS, XA_DH), F32)],
        compiler_params=_params(1),
        name="xa_out_route",
    )(qx, mkb, mvb, qx, ck, cv, x1, w_co, g_moe, w_r, b_r)


def _positions_kernel(route_ref, cnt_ref, pos_ref):
    route = route_ref[...]
    lane = lax.broadcasted_iota(jnp.int32, route.shape, 1)
    lane_f = lane.astype(F32)
    tiles = jnp.floor((cnt_ref[...] + (MOE_TM - 1)) * (1.0 / MOE_TM))
    lr = lax.broadcasted_iota(jnp.int32, (ROUTE_LANES, ROUTE_LANES), 0)
    lc = lax.broadcasted_iota(jnp.int32, (ROUTE_LANES, ROUTE_LANES), 1)
    before = jnp.where(lr < lc, 1.0, 0.0).astype(BF16)
    tiles8 = jnp.broadcast_to(tiles, (8, ROUTE_LANES)).astype(BF16)
    start = _dot(tiles8, before)[0:1, :] * MOE_TM

    def col(k):
        return jnp.sum(jnp.where(lane == k, route, 0.0), axis=-1, keepdims=True)

    def first_row(e):
        return jnp.sum(jnp.where(lane_f == e + MOE_GROUPS, start, 0.0), axis=-1, keepdims=True)

    p0 = first_row(col(0)) + col(2)
    p1 = first_row(col(1)) + col(3)
    p = jnp.where(lane == 0, p0, jnp.where(lane == 1, p1, 0.0))
    pos_ref[...] = p.T[0:8, :].astype(jnp.int32)


def _positions(route, counts):
    rows = N_ALL // 4
    return pl.pallas_call(
        _positions_kernel,
        grid=(4,),
        in_specs=[
            pl.BlockSpec((rows, ROUTE_LANES), lambda i: (i, 0)),
            pl.BlockSpec((1, ROUTE_LANES), lambda i: (0, 0)),
        ],
        out_specs=pl.BlockSpec((8, rows), lambda i: (0, i)),
        out_shape=jax.ShapeDtypeStruct((8, N_ALL), jnp.int32),
        compiler_params=_params(1),
        name="positions",
    )(route, counts)


def _dispatch_kernel(pos_ref, zrow_ref, zon_ref, nu_ref, hm_ref, xs_ref, zbuf, sem, zsem, tbuf, tsem):
    i = pl.program_id(0)

    def zero_tile(row):
        return pltpu.make_async_copy(zbuf, xs_ref.at[pl.ds(pl.multiple_of(row, MOE_TM), MOE_TM)], zsem)

    @pl.when(i == 0)
    def _():
        zbuf[...] = jnp.zeros_like(zbuf)
        for e in range(MOE_EXPERTS):
            @pl.when(zon_ref[e] > 0)
            def _():
                zero_tile(zrow_ref[e]).start()

        def start_tail(t, carry):
            zero_tile(t * MOE_TM).start()
            return carry

        def wait_tail(t, carry):
            zero_tile(t * MOE_TM).wait()
            return carry

        lax.fori_loop(nu_ref[0], MOE_NT, start_tail, 0)
        for e in range(MOE_EXPERTS):
            @pl.when(zon_ref[e] > 0)
            def _():
                zero_tile(zrow_ref[e]).wait()
        lax.fori_loop(nu_ref[0], MOE_NT, wait_tail, 0)

    n_steps = pl.num_programs(0)

    def tile_copy(t):
        return pltpu.make_async_copy(hm_ref.at[pl.ds(t * ROW_TILE, ROW_TILE)], tbuf.at[t % 3], tsem.at[t % 3])

    def wait_rows(t):
        for k in range(MOE_TOPK):
            pltpu.make_async_copy(tbuf.at[t % 3], xs_ref.at[pl.ds(0, ROW_TILE)], sem.at[t % 2, k]).wait()

    @pl.when(i == 0)
    def _():
        tile_copy(0).start()

    tile_copy(i).wait()

    @pl.when(i + 1 < n_steps)
    def _():
        tile_copy(i + 1).start()

    slot = i % 3
    par = i % 2
    base = i * ROW_TILE

    def start(r, carry):
        for k in range(MOE_TOPK):
            dst_row = pos_ref[k, base + r]
            pltpu.make_async_copy(tbuf.at[slot, r], xs_ref.at[dst_row], sem.at[par, k]).start(priority=k)
        return carry

    lax.fori_loop(0, ROW_TILE, start, 0, unroll=32)

    @pl.when(i > 0)
    def _():
        wait_rows(i - 1)

    @pl.when(i == n_steps - 1)
    def _():
        wait_rows(i)


def _dispatch(pos_t, zero_row, zero_on, n_used, hmw):
    grid_spec = pltpu.PrefetchScalarGridSpec(
        num_scalar_prefetch=4,
        grid=(N_TILES,),
        in_specs=[pl.BlockSpec(memory_space=pl.ANY)],
        out_specs=pl.BlockSpec(memory_space=pl.ANY),
        scratch_shapes=[pltpu.VMEM((MOE_TM, ROW_SUB, ROW_LANE), BF16),
                        pltpu.SemaphoreType.DMA((2, MOE_TOPK)), pltpu.SemaphoreType.DMA(()),
                        pltpu.VMEM((3, ROW_TILE, ROW_SUB, ROW_LANE), BF16), pltpu.SemaphoreType.DMA((3,))],
    )
    return pl.pallas_call(
        _dispatch_kernel,
        grid_spec=grid_spec,
        out_shape=jax.ShapeDtypeStruct((MOE_ROWS, ROW_SUB, ROW_LANE), BF16),
        compiler_params=_params(1),
        name="dispatch",
    )(pos_t, zero_row, zero_on, n_used, hmw)


def _expert_kernel(nu_ref, first_ref, ord_ref, oe_ref, no_ref, half_ref, x_ref, w1_hbm, w3_hbm, w2_hbm, y_ref,
                   w1b, w3b, w2b, sem):
    i = pl.program_id(0)
    n_used = nu_ref[0]
    n_ord = no_ref[0]

    def weight_copies(k):
        e = oe_ref[k]
        slot = k % W_SLOTS
        return (pltpu.make_async_copy(w1_hbm.at[e], w1b.at[slot], sem.at[0, slot]),
                pltpu.make_async_copy(w3_hbm.at[e], w3b.at[slot], sem.at[1, slot]),
                pltpu.make_async_copy(w2_hbm.at[e], w2b.at[slot], sem.at[2, slot]))

    def start_weights(k):
        for cp in weight_copies(k):
            cp.start(priority=1)

    @pl.when(i == 0)
    def _():
        for k in range(W_SLOTS - 1):
            @pl.when(k < n_ord)
            def _():
                start_weights(k)

    @pl.when(i < n_used)
    def _():
        k = ord_ref[i]

        @pl.when(first_ref[i] > 0)
        def _():
            for cp in weight_copies(k):
                cp.wait()

            @pl.when(k + (W_SLOTS - 1) < n_ord)
            def _():
                start_weights(k + (W_SLOTS - 1))

        slot = k % W_SLOTS

        def swiglu(rows):
            x = _from_row_tiles(x_ref[0:rows])
            h1 = _dot(x, w1b[slot].astype(BF16))
            h3 = _dot(x, w3b[slot].astype(BF16))
            he = (_silu(h1) * h3).astype(BF16)
            y_ref[0:rows] = _to_row_tiles(_dot(he, w2b[slot].astype(BF16)))

        @pl.when(half_ref[i] == 0)
        def _():
            swiglu(MOE_TM)

        @pl.when(half_ref[i] > 0)
        def _():
            swiglu(MOE_TM // 2)
            y_ref[MOE_TM // 2:] = jnp.zeros((MOE_TM // 2, ROW_SUB, ROW_LANE), BF16)

    @pl.when(i >= n_used)
    def _():
        y_ref[...] = jnp.zeros_like(y_ref)


def _experts(sched, xs, w_e1, w_e3, w_e2):
    grid_spec = pltpu.PrefetchScalarGridSpec(
        num_scalar_prefetch=6,
        grid=(MOE_NT,),
        in_specs=[
            pl.BlockSpec((MOE_TM, ROW_SUB, ROW_LANE), lambda i, nu, *_: (jnp.minimum(i, nu[0] - 1), 0, 0)),
            pl.BlockSpec(memory_space=pl.ANY),
            pl.BlockSpec(memory_space=pl.ANY),
            pl.BlockSpec(memory_space=pl.ANY),
        ],
        out_specs=pl.BlockSpec((MOE_TM, ROW_SUB, ROW_LANE), lambda i, *_: (i, 0, 0)),
        scratch_shapes=[
            pltpu.VMEM((W_SLOTS, D_MODEL, MOE_FF), F32),
            pltpu.VMEM((W_SLOTS, D_MODEL, MOE_FF), F32),
            pltpu.VMEM((W_SLOTS, MOE_FF, D_MODEL), F32),
            pltpu.SemaphoreType.DMA((3, W_SLOTS)),
        ],
    )
    return pl.pallas_call(
        _expert_kernel,
        grid_spec=grid_spec,
        out_shape=jax.ShapeDtypeStruct((MOE_ROWS, ROW_SUB, ROW_LANE), BF16),
        compiler_params=_params(1),
        name="experts",
    )(sched["n_used"], sched["tile_first"], sched["tile_ord"], sched["ord_expert"], sched["n_ord"],
      sched["tile_half"], xs, w_e1, w_e3, w_e2)


def _combine_kernel(pos_ref, ys_ref, x2_ref, route_ref, g_ref, yp_ref, ysm_ref, gbuf, sem):
    i = pl.program_id(0)

    def start_gather(tile, slot):
        def body(r, carry):
            for k in range(MOE_TOPK):
                src_row = pos_ref[k, tile * ROW_TILE + r]
                pltpu.make_async_copy(ys_ref.at[src_row], gbuf.at[slot, k, r], sem.at[slot, k]).start(priority=k)
            return carry
        lax.fori_loop(0, ROW_TILE, body, 0, unroll=32)

    @pl.when(i == 0)
    def _():
        start_gather(0, 0)

    slot = i % 2

    @pl.when(i + 1 < pl.num_programs(0))
    def _():
        start_gather(i + 1, 1 - slot)

    for k in range(MOE_TOPK):
        pltpu.make_async_copy(ys_ref.at[pl.ds(0, ROW_TILE)], gbuf.at[slot, k], sem.at[slot, k]).wait()

    route = route_ref[...]
    lane = lax.broadcasted_iota(jnp.int32, route.shape, 1)
    w0 = jnp.sum(jnp.where(lane == 4, route, 0.0), axis=-1, keepdims=True)
    w1 = jnp.sum(jnp.where(lane == 5, route, 0.0), axis=-1, keepdims=True)
    g0 = _from_row_tiles(gbuf[slot, 0]).astype(F32)
    g1 = _from_row_tiles(gbuf[slot, 1]).astype(F32)
    x3 = x2_ref[...] + (g0 * w0 + g1 * w1)
    y = _rms(x3, g_ref[...])

    @pl.when(i < NP_TILES)
    def _():
        yp_ref[...] = y

    @pl.when(i == NP_TILES)
    def _():
        ysm_ref[...] = y.reshape(DEC_BATCH, DEC_SEQ, D_MODEL)


def _combine(pos, ys, x2, route, g_f):
    grid_spec = pltpu.PrefetchScalarGridSpec(
        num_scalar_prefetch=1,
        grid=(N_TILES,),
        in_specs=[
            pl.BlockSpec(memory_space=pl.ANY),
            pl.BlockSpec((ROW_TILE, D_MODEL), lambda i, pos: (i, 0)),
            pl.BlockSpec((ROW_TILE, ROUTE_LANES), lambda i, pos: (i, 0)),
            pl.BlockSpec((1, D_MODEL), lambda i, pos: (0, 0)),
        ],
        out_specs=[
            pl.BlockSpec((ROW_TILE, D_MODEL), lambda i, pos: (jnp.minimum(i, NP_TILES - 1), 0)),
            pl.BlockSpec((DEC_BATCH, DEC_SEQ, D_MODEL), lambda i, pos: (0, 0, 0)),
        ],
        scratch_shapes=[pltpu.VMEM((2, MOE_TOPK, ROW_TILE, ROW_SUB, ROW_LANE), BF16),
                        pltpu.SemaphoreType.DMA((2, MOE_TOPK))],
    )
    return pl.pallas_call(
        _combine_kernel,
        grid_spec=grid_spec,
        out_shape=[
            jax.ShapeDtypeStruct((N_P, D_MODEL), F32),
            jax.ShapeDtypeStruct((DEC_BATCH, DEC_SEQ, D_MODEL), F32),
        ],
        compiler_params=_params(1),
        name="combine_norm",
    )(pos, ys, x2, route, g_f)


_SCHEDULE_FIELDS = ("n_used", "tile_first", "tile_ord", "tile_half", "ord_expert", "n_ord", "zero_row", "zero_on")


def _schedule_kernel(cnt_ref, nu_ref, first_ref, ord_ref, half_ref, oe_ref, no_ref, zrow_ref, zon_ref):
    i32 = jnp.int32

    def clear_tile(t, carry):
        first_ref[t] = i32(0)
        ord_ref[t] = i32(0)
        half_ref[t] = i32(0)
        return carry

    lax.fori_loop(0, MOE_NT, clear_tile, 0)

    def clear_expert(e, carry):
        oe_ref[e] = i32(0)
        return carry

    lax.fori_loop(0, MOE_EXPERTS, clear_expert, 0)

    def expert(e, carry):
        t0, k = carry
        c = cnt_ref[0, MOE_GROUPS + e]
        nt = lax.shift_right_logical(c + (MOE_TM - 1), i32(LOG_MOE_TM))
        used = jnp.where(nt > 0, i32(1), i32(0))
        zon_ref[e] = used
        zrow_ref[e] = jnp.maximum(t0 + nt - 1, 0) * MOE_TM

        @pl.when(nt > 0)
        def _():
            oe_ref[k] = e

        def tile(j, cc):
            t = t0 + j
            first_ref[t] = jnp.where(j == 0, i32(1), i32(0))
            ord_ref[t] = k
            half_ref[t] = jnp.where(c - j * MOE_TM <= MOE_TM // 2, i32(1), i32(0))
            return cc

        lax.fori_loop(0, nt, tile, 0)
        return t0 + nt, k + used

    n_tiles, n_experts = lax.fori_loop(0, MOE_EXPERTS, expert, (i32(0), i32(0)))
    nu_ref[0] = n_tiles
    no_ref[0] = n_experts


def _expert_schedule(counts_i32):
    smem = pl.BlockSpec(memory_space=pltpu.SMEM)
    sizes = (1, MOE_NT, MOE_NT, MOE_NT, MOE_EXPERTS, 1, MOE_EXPERTS, MOE_EXPERTS)
    outs = pl.pallas_call(
        _schedule_kernel,
        in_specs=[smem],
        out_specs=[smem] * len(sizes),
        out_shape=[jax.ShapeDtypeStruct((n,), jnp.int32) for n in sizes],
        name="expert_schedule",
    )(counts_i32)
    return dict(zip(_SCHEDULE_FIELDS, outs))


def kernel(x_prompt, x_sample, mem_prompt, state_ret, cache_mem_k, cache_mem_v, norm_mix, w_in, ret_gn,
           sg_ln_g, sg_ln_b, sg_ws, sg_bs, w_a_out, w_b_out, w_o, norm_xa, w_cq, w_ck, w_cv, w_co, norm_moe,
           w_rg, b_rg, w_re, b_re, w_e1, w_e3, w_e2, norm_f):
    xp = x_prompt.reshape(N_P, D_MODEL)
    xs = x_sample

    h = _norm_rows(xp, xs, norm_mix)
    w = w_in[0]
    blocks = lambda m: m.reshape(MERGE_W_BLOCKS, D_MODEL // MERGE_W_BLOCKS, D_MODEL)
    qk, wa = _inproj("rope", IN_TM, 0, 2 * RET_QK, h, w, _rope_tables(IN_TM), side=blocks(w_a_out[0]))
    v, wb = _inproj("copy", IN_TM, 2 * RET_QK, RET_V, h, w, side=blocks(w_b_out[0]))
    gs, wo = _inproj("silu", IN_TM, 2 * RET_QK + RET_V, RET_V, h, w, side=blocks(w_o[0]))
    uv = _inproj("gelu", IN_TM, 2 * RET_QK + 2 * RET_V, 2 * SG_WIDTH, h, w)
    gab = _inproj("sigmoid", IN_TM, 2 * RET_QK + 2 * RET_V + 2 * SG_WIDTH, 2 * D_MODEL, h, w)

    a_p, ret_p, a_s, ret_s = _retention(qk, v, ret_gn, state_ret)

    b_all, sgv = _sgate(uv, sg_ln_g, sg_ln_b, sg_ws[0], sg_bs[0])

    square = lambda m: m.reshape(D_MODEL, D_MODEL)
    x1, qx = _merge(a_p, a_s, gs, b_all, gab, xp, xs, square(wa), square(wb), square(wo), norm_xa,
                    w_cq[0].astype(BF16))

    mk, mv, mkb, mvb = _memkv(mem_prompt.reshape(BATCH * MEM_LEN, D_MODEL), w_ck[0], w_cv[0])
    ck = cache_mem_k.reshape(DEC_BATCH, MEM_LEN * XA_HEADS, XA_DH)
    cv = cache_mem_v.reshape(DEC_BATCH, MEM_LEN * XA_HEADS, XA_DH)

    pad = ROUTE_LANES - MOE_GROUPS - MOE_EXPERTS
    w_r = jnp.concatenate([w_rg[0], w_re[0], jnp.zeros((D_MODEL, pad), F32)], axis=1)
    b_r = jnp.concatenate([b_rg[0], b_re[0], jnp.zeros((pad,), F32)]).reshape(1, ROUTE_LANES)
    x2, hmw, route, counts, counts_i32 = _route(qx, mkb, mvb, ck, cv, x1, w_co[0].astype(BF16), norm_moe,
                                                w_r.astype(BF16), b_r)

    pos_t = _positions(route, counts)
    sched = _expert_schedule(counts_i32)
    xs_sorted = _dispatch(pos_t, sched["zero_row"], sched["zero_on"], sched["n_used"], hmw)
    ys = _experts(sched, xs_sorted, w_e1[0], w_e3[0], w_e2[0])
    y_p, y_s = _combine(pos_t, ys, x2, route, norm_f.reshape(1, D_MODEL))

    return (y_p.reshape(BATCH, SEQ, D_MODEL), y_s, ret_p, mk, mv, ret_s,
            sgv.reshape(1, DEC_BATCH, DEC_SEQ, SG_WIDTH))
```

```python
import functools

import jax
import jax.numpy as jnp
import numpy as np
from jax import lax
from jax.experimental import pallas as pl
from jax.experimental.pallas import tpu as pltpu

F32 = jnp.float32
BF16 = jnp.bfloat16

D_MODEL = 2048
BATCH = 4
SEQ = 2048
DEC_BATCH = 128
DEC_SEQ = 4
PAST_LEN = 16384
RET_HEADS = 8
RET_DK = 128
RET_DV = 256
ROPE_BASE = 10000.0
RET_QK = RET_HEADS * RET_DK
RET_V = RET_HEADS * RET_DV
SG_GROUPS = 4
SG_WIDTH = 2048
SG_CHUNK = 128
MEM_LEN = 256
XA_HEADS = 4
XA_DH = 128
XA_W = XA_HEADS * XA_DH
MOE_GROUPS = 4
MOE_PER_GROUP = 8
MOE_EXPERTS = MOE_GROUPS * MOE_PER_GROUP
MOE_TOPK = 2
MOE_FF = 512
EPS = 1e-6

N_P = BATCH * SEQ
N_S = DEC_BATCH * DEC_SEQ
N_ALL = N_P + N_S
ROW_TILE = 512
N_TILES = N_ALL // ROW_TILE
NP_TILES = N_P // ROW_TILE
MERGE_TILE = 256
MERGE_P_TILES = N_P // MERGE_TILE

IN_TM = 1024
IN_TN = 2048
MERGE_W_BLOCKS = 4

PROMPT_CHUNK = 256
RET_BB = DEC_BATCH // (BATCH * (SEQ // PROMPT_CHUNK))
RET_ROWS = RET_BB * DEC_SEQ
LOG_DEC_SEQ = 2
SAMPLE_BB = DEC_BATCH // NP_TILES
SAMPLE_ROWS = SAMPLE_BB * DEC_SEQ

MOE_TM = 256
LOG_MOE_TM = 8
MOE_NT = (N_ALL * MOE_TOPK + MOE_EXPERTS * (MOE_TM - 1) + MOE_TM - 1) // MOE_TM
MOE_ROWS = MOE_NT * MOE_TM
ROUTE_LANES = 128
ROW_SUB, ROW_LANE = 16, 128
W_SLOTS = 3

VMEM_LIMIT = 56 * 1024 * 1024


def _params(n_axes, vmem=VMEM_LIMIT):
    return pltpu.CompilerParams(dimension_semantics=("arbitrary",) * n_axes,
                                vmem_limit_bytes=vmem)


def _rms(x, g):
    ms = jnp.mean(x * x, axis=-1, keepdims=True)
    return (x * lax.rsqrt(ms + EPS)) * g


def _dot(a, b):
    return jnp.dot(a, b, preferred_element_type=F32)


def _sigmoid(x):
    return 0.5 * jnp.tanh(0.5 * x) + 0.5


def _silu(x):
    hx = 0.5 * x
    return hx + hx * jnp.tanh(hx)


_GELU_C0 = float(np.float32(np.sqrt(2.0 / np.pi)))


def _gelu_tanh(x):
    hx = 0.5 * x
    return hx + hx * jnp.tanh(x * (_GELU_C0 + (_GELU_C0 * 0.044715) * (x * x)))


def _to_row_tiles(x):
    return x.astype(BF16).reshape(x.shape[0], ROW_SUB, ROW_LANE)


def _from_row_tiles(t):
    return t.reshape(t.shape[0], D_MODEL)


def _dot_nt(a, b):
    return lax.dot_general(a, b, (((1,), (1,)), ((), ())), preferred_element_type=F32)


def _dot_tn(a, b):
    return lax.dot_general(a, b, (((0,), (0,)), ((), ())), preferred_element_type=F32)


def _norm_kernel(xp_ref, xs_ref, g_ref, h_ref):
    i = pl.program_id(0)

    @pl.when(i < NP_TILES)
    def _():
        h_ref[...] = _rms(xp_ref[...], g_ref[...]).astype(BF16)

    @pl.when(i == NP_TILES)
    def _():
        h_ref[...] = _rms(xs_ref[...].reshape(N_S, D_MODEL), g_ref[...]).astype(BF16)


def _norm_rows(xp, xs, g):
    return pl.pallas_call(
        _norm_kernel,
        grid=(N_TILES,),
        in_specs=[
            pl.BlockSpec((ROW_TILE, D_MODEL), lambda i: (jnp.minimum(i, NP_TILES - 1), 0)),
            pl.BlockSpec((DEC_BATCH, DEC_SEQ, D_MODEL), lambda i: (0, 0, 0)),
            pl.BlockSpec((1, D_MODEL), lambda i: (0, 0)),
        ],
        out_specs=pl.BlockSpec((ROW_TILE, D_MODEL), lambda i: (i, 0)),
        out_shape=jax.ShapeDtypeStruct((N_ALL, D_MODEL), BF16),
        compiler_params=_params(1),
        name="norm_rows",
    )(xp, xs, g)


def _inproj_kernel(kind, tm, n_side, h_ref, w_ref, *rest):
    rest = list(rest)
    cos_ref, sin_ref = (rest.pop(0), rest.pop(0)) if kind == "rope" else (None, None)
    side_ref = rest.pop(0) if n_side else None
    z_ref = rest.pop(0)
    side_out_ref = rest.pop(0) if n_side else None
    j = pl.program_id(0)
    i = pl.program_id(1)
    last = N_P // tm

    if n_side:
        @pl.when(j * (last + 1) + i < n_side)
        def _():
            side_out_ref[...] = side_ref[...].astype(BF16)

    def tile(rows):
        acc = _dot(h_ref[0:rows, :], w_ref[...].astype(BF16))
        if kind == "rope":
            c = cos_ref[0:rows, :]
            s = sin_ref[0:rows, :]
            heads_per_block = IN_TN // RET_DK
            for hb in range(heads_per_block):
                scale = jnp.where(j * heads_per_block + hb >= RET_HEADS, RET_DK ** -0.5, 1.0).astype(F32)
                cols = slice(hb * RET_DK, (hb + 1) * RET_DK)
                a = acc[:, cols]
                r = pltpu.roll(a, RET_DK // 2, axis=1)
                z_ref[0:rows, cols] = ((a * c + r * s) * scale).astype(BF16)
        elif kind == "copy":
            z_ref[0:rows, :] = acc.astype(BF16)
        elif kind == "silu":
            z_ref[0:rows, :] = _silu(acc).astype(BF16)
        elif kind == "gelu":
            z_ref[0:rows, :] = _gelu_tanh(acc).astype(BF16)
        else:
            z_ref[0:rows, :] = _sigmoid(acc).astype(BF16)

    @pl.when(i < last)
    def _():
        tile(tm)

    @pl.when(i == last)
    def _():
        tile(N_S)


def _inproj(kind, tm, col0, width, h, w_in, tables=(), side=None):
    last = N_P // tm
    n_j = width // IN_TN
    tab_idx = lambda j, i: (jnp.where(i < last, i % (SEQ // tm), SEQ // tm), 0)
    j0 = col0 // IN_TN
    in_specs = [
        pl.BlockSpec((tm, D_MODEL), lambda j, i: (i, 0)),
        pl.BlockSpec((D_MODEL, IN_TN), lambda j, i: (0, j0 + j)),
    ] + [pl.BlockSpec((tm, RET_DK), tab_idx) for _ in tables]
    out_specs = [pl.BlockSpec((tm, IN_TN), lambda j, i: (i, j))]
    out_shape = [jax.ShapeDtypeStruct((N_ALL, width), BF16)]
    n_side = 0 if side is None else side.shape[0]
    if n_side:
        assert n_side <= n_j * (last + 1)
        blk = (1,) + side.shape[1:]
        side_blk = lambda j, i: (jnp.minimum(j * (last + 1) + i, n_side - 1), 0, 0)
        in_specs.append(pl.BlockSpec(blk, side_blk))
        out_specs.append(pl.BlockSpec(blk, side_blk))
        out_shape.append(jax.ShapeDtypeStruct(side.shape, BF16))
    outs = pl.pallas_call(
        functools.partial(_inproj_kernel, kind, tm, n_side),
        grid=(n_j, last + 1),
        in_specs=in_specs,
        out_specs=out_specs,
        out_shape=out_shape,
        compiler_params=_params(2),
        name="in_proj_" + kind,
    )(h, w_in, *tables, *(() if side is None else (side,)))
    return outs if n_side else outs[0]


def _rope_tables(tm):
    half = RET_DK // 2
    inv = ROPE_BASE ** (-jnp.arange(half, dtype=F32) / half)

    def tab(pos):
        ang = pos.astype(F32)[:, None] * inv[None, :]
        c, s = jnp.cos(ang), jnp.sin(ang)
        return jnp.concatenate([c, c], -1), jnp.concatenate([-s, s], -1)

    cp, sp = tab(jnp.arange(SEQ, dtype=jnp.int32))
    cs, ss = tab(PAST_LEN + jnp.arange(DEC_SEQ, dtype=jnp.int32))
    cs = jnp.tile(cs, (DEC_BATCH, 1))
    ss = jnp.tile(ss, (DEC_BATCH, 1))
    pad = jnp.zeros((tm - N_S, RET_DK), F32)
    return jnp.concatenate([cp, cs, pad], 0), jnp.concatenate([sp, ss, pad], 0)


def _decay_tables(chunk):
    lg = jnp.log1p(-jnp.power(2.0, -5.0 - jnp.arange(RET_HEADS, dtype=F32)))
    idx = jnp.arange(chunk, dtype=F32)
    rel = idx[:, None] - idx[None, :]
    dmask = jnp.where(rel >= 0, jnp.exp(lg[:, None, None] * jnp.maximum(rel, 0.0)), 0.0).astype(F32)
    xi = jnp.exp(lg[:, None] * (idx[None, :] + 1.0)).astype(F32)
    zeta = jnp.exp(lg[:, None] * (chunk - 1.0 - idx[None, :])).astype(F32)
    gc = jnp.exp(lg * chunk).astype(F32)
    return dmask, xi, zeta, gc


def _head_norm(o, gn):
    mu = jnp.mean(o, axis=-1, keepdims=True)
    d = o - mu
    var = jnp.mean(d * d, axis=-1, keepdims=True)
    return ((d * lax.rsqrt(var + EPS)) * gn).astype(BF16)


def _ret_kernel(gcp_ref, gcs_ref, q_ref, k_ref, v_ref, gn_ref, dm_ref, xi_ref, zt_ref,
                qs_ref, ks_ref, vs_ref, dms_ref, xis_ref, zts_ref, s0_ref,
                a_ref, sfin_ref, as_ref, s1_ref, s_ref):
    c = pl.program_id(1)

    @pl.when(c == 0)
    def _():
        s_ref[...] = jnp.zeros_like(s_ref)

    rows_k = lax.broadcasted_iota(jnp.int32, (RET_ROWS, RET_DK), 0) >> LOG_DEC_SEQ
    rows_v = lax.broadcasted_iota(jnp.int32, (RET_ROWS, RET_DV), 0) >> LOG_DEC_SEQ
    for h in range(RET_HEADS):
        kc = slice(h * RET_DK, (h + 1) * RET_DK)
        vc = slice(h * RET_DV, (h + 1) * RET_DV)
        gn = gn_ref[:, vc]

        qh = q_ref[:, kc]
        kh = k_ref[:, kc]
        vh = v_ref[:, vc]
        inner = _dot_nt(qh, kh) * dm_ref[h]
        o = _dot(inner.astype(BF16), vh)
        s_old = s_ref[h]
        xi = xi_ref[h]
        o = o + _dot(qh, s_old.astype(BF16)) * jnp.concatenate([xi, xi], axis=1)
        kz = (kh.astype(F32) * zt_ref[h]).astype(BF16)
        s_ref[h] = gcp_ref[h] * s_old + _dot_tn(kz, vh)
        a_ref[:, vc] = _head_norm(o, gn)

        qh = qs_ref[:, kc]
        kh = ks_ref[:, kc]
        vh = vs_ref[:, vc]
        inner = _dot_nt(qh, kh) * dms_ref[h]
        o = _dot(inner.astype(BF16), vh)
        xi = xis_ref[h]
        xi2 = jnp.concatenate([xi, xi], axis=1)
        kz = kh.astype(F32) * zts_ref[h]
        gch = gcs_ref[h]
        for b in range(RET_BB):
            s_old = s0_ref[0, b, h]
            cross = _dot(qh, s_old.astype(BF16)) * xi2
            o = o + jnp.where(rows_v == b, cross, 0.0)
            kz_b = jnp.where(rows_k == b, kz, 0.0).astype(BF16)
            s1_ref[0, b, h] = gch * s_old + _dot_tn(kz_b, vh)
        as_ref[:, vc] = _head_norm(o, gn)

    @pl.when(c == pl.num_programs(1) - 1)
    def _():
        sfin_ref[0, 0] = s_ref[...]


def _retention(qk, v, ret_gn, state):
    chunk = PROMPT_CHUNK
    n_chunks = SEQ // chunk
    dmask, xi, zeta, gc = _decay_tables(chunk)
    xi_b = jnp.broadcast_to(xi[:, :, None], (RET_HEADS, chunk, RET_DK))
    zeta_b = jnp.broadcast_to(zeta[:, :, None], (RET_HEADS, chunk, RET_DK))
    dmask_s, xi_s, zeta_s, gc_s = _decay_tables(DEC_SEQ)
    eye = jnp.eye(RET_BB, dtype=F32)
    dm_big = jax.vmap(lambda m: jnp.kron(eye, m))(dmask_s)
    xi_sb = jnp.broadcast_to(jnp.tile(xi_s, (1, RET_BB))[:, :, None], (RET_HEADS, RET_ROWS, RET_DK))
    zeta_sb = jnp.broadcast_to(jnp.tile(zeta_s, (1, RET_BB))[:, :, None], (RET_HEADS, RET_ROWS, RET_DK))

    row = lambda b, c: b * n_chunks + c
    srow = lambda b, c: N_P // RET_ROWS + row(b, c)
    const3 = lambda b, c: (0, 0, 0)
    smem = pl.BlockSpec(memory_space=pltpu.SMEM)
    st_spec = pl.BlockSpec((1, RET_BB, RET_HEADS, RET_DK, RET_DV), lambda b, c: (0, row(b, c), 0, 0, 0))
    return pl.pallas_call(
        _ret_kernel,
        grid=(BATCH, n_chunks),
        in_specs=[
            smem, smem,
            pl.BlockSpec((chunk, RET_QK), lambda b, c: (row(b, c), 0)),
            pl.BlockSpec((chunk, RET_QK), lambda b, c: (row(b, c), 1)),
            pl.BlockSpec((chunk, RET_V), lambda b, c: (row(b, c), 0)),
            pl.BlockSpec((1, RET_V), lambda b, c: (0, 0)),
            pl.BlockSpec((RET_HEADS, chunk, chunk), const3),
            pl.BlockSpec((RET_HEADS, chunk, RET_DK), const3),
            pl.BlockSpec((RET_HEADS, chunk, RET_DK), const3),
            pl.BlockSpec((RET_ROWS, RET_QK), lambda b, c: (srow(b, c), 0)),
            pl.BlockSpec((RET_ROWS, RET_QK), lambda b, c: (srow(b, c), 1)),
            pl.BlockSpec((RET_ROWS, RET_V), lambda b, c: (srow(b, c), 0)),
            pl.BlockSpec((RET_HEADS, RET_ROWS, RET_ROWS), const3),
            pl.BlockSpec((RET_HEADS, RET_ROWS, RET_DK), const3),
            pl.BlockSpec((RET_HEADS, RET_ROWS, RET_DK), const3),
            st_spec,
        ],
        out_specs=[
            pl.BlockSpec((chunk, RET_V), lambda b, c: (row(b, c), 0)),
            pl.BlockSpec((1, 1, RET_HEADS, RET_DK, RET_DV), lambda b, c: (0, b, 0, 0, 0)),
            pl.BlockSpec((RET_ROWS, RET_V), lambda b, c: (row(b, c), 0)),
            st_spec,
        ],
        out_shape=[
            jax.ShapeDtypeStruct((N_P, RET_V), BF16),
            jax.ShapeDtypeStruct((1, BATCH, RET_HEADS, RET_DK, RET_DV), F32),
            jax.ShapeDtypeStruct((N_S, RET_V), BF16),
            jax.ShapeDtypeStruct((1, DEC_BATCH, RET_HEADS, RET_DK, RET_DV), F32),
        ],
        scratch_shapes=[pltpu.VMEM((RET_HEADS, RET_DK, RET_DV), F32)],
        compiler_params=_params(2),
        name="retention",
    )(gc, gc_s, qk, qk, v, ret_gn, dmask, xi_b, zeta_b, qk, qk, v, dm_big, xi_sb, zeta_sb, state)


def _sgate_kernel(u_ref, v_ref, lg_ref, lb_ref, wp_ref, bp_ref, ws_ref, bs_ref, b_ref, sgv_ref, vln_ref):
    i = pl.program_id(0)
    v = v_ref[...].astype(F32)
    mu = jnp.mean(v, axis=-1, keepdims=True)
    d = v - mu
    var = jnp.mean(d * d, axis=-1, keepdims=True)
    vln_ref[...] = (d * lax.rsqrt(var + EPS)) * lg_ref[...] + lb_ref[...]
    gw = SG_WIDTH // SG_GROUPS
    lane_reps = gw // 128

    @pl.when(i < NP_TILES)
    def _():
        r = lax.broadcasted_iota(jnp.int32, (SG_CHUNK, SG_CHUNK), 0)
        c = lax.broadcasted_iota(jnp.int32, (SG_CHUNK, SG_CHUNK), 1)
        for g in range(SG_GROUPS):
            cols = slice(g * gw, (g + 1) * gw)
            w = jnp.where(c <= r, wp_ref[g], 0.0).astype(BF16)
            bias = jnp.concatenate([bp_ref[g]] * lane_reps, axis=1)
            chunks = [slice(ch * SG_CHUNK, (ch + 1) * SG_CHUNK) for ch in range(ROW_TILE // SG_CHUNK)]
            mixed = [_dot(w, vln_ref[rows, cols].astype(BF16)) for rows in chunks]
            for rows, m in zip(chunks, mixed):
                b_ref[rows, cols] = (u_ref[rows, cols].astype(F32) * (m + bias)).astype(BF16)

    @pl.when(i == NP_TILES)
    def _():
        sgv_ref[...] = vln_ref[...].reshape(DEC_BATCH, DEC_SEQ, SG_WIDTH)
        r = lax.broadcasted_iota(jnp.int32, (ROW_TILE, ROW_TILE), 0)
        c = lax.broadcasted_iota(jnp.int32, (ROW_TILE, ROW_TILE), 1)
        keep = ((r >> LOG_DEC_SEQ) == (c >> LOG_DEC_SEQ)) & (c <= r)
        for g in range(SG_GROUPS):
            cols = slice(g * gw, (g + 1) * gw)
            w_rows = jnp.concatenate([ws_ref[g]] * (ROW_TILE // 8), axis=0)
            w_full = jnp.concatenate([w_rows] * (ROW_TILE // 128), axis=1)
            w = jnp.where(keep, w_full, 0.0).astype(BF16)
            b_rows = jnp.concatenate([bs_ref[g]] * (ROW_TILE // 8), axis=0)
            bias = jnp.concatenate([b_rows] * lane_reps, axis=1)
            mixed = _dot(w, vln_ref[:, cols].astype(BF16)) + bias
            b_ref[:, cols] = (u_ref[:, cols].astype(F32) * mixed).astype(BF16)


def _sgate(uv, ln_g, ln_b, sg_ws, sg_bs):
    b_p = jnp.broadcast_to(sg_bs[:, :, None], (SG_GROUPS, SG_CHUNK, 128))
    w_s = jnp.tile(sg_ws[:, :DEC_SEQ, :DEC_SEQ], (1, 8 // DEC_SEQ, 128 // DEC_SEQ))
    b_s = jnp.broadcast_to(jnp.tile(sg_bs[:, :DEC_SEQ], (1, 8 // DEC_SEQ))[:, :, None], (SG_GROUPS, 8, 128))
    const3 = lambda i: (0, 0, 0)
    return pl.pallas_call(
        _sgate_kernel,
        grid=(N_TILES,),
        in_specs=[
            pl.BlockSpec((ROW_TILE, SG_WIDTH), lambda i: (i, 0)),
            pl.BlockSpec((ROW_TILE, SG_WIDTH), lambda i: (i, 1)),
            pl.BlockSpec((1, SG_WIDTH), lambda i: (0, 0)),
            pl.BlockSpec((1, SG_WIDTH), lambda i: (0, 0)),
            pl.BlockSpec((SG_GROUPS, SG_CHUNK, SG_CHUNK), const3),
            pl.BlockSpec((SG_GROUPS, SG_CHUNK, 128), const3),
            pl.BlockSpec((SG_GROUPS, 8, 128), const3),
            pl.BlockSpec((SG_GROUPS, 8, 128), const3),
        ],
        out_specs=[
            pl.BlockSpec((ROW_TILE, SG_WIDTH), lambda i: (i, 0)),
            pl.BlockSpec((DEC_BATCH, DEC_SEQ, SG_WIDTH), lambda i: (0, 0, 0)),
        ],
        out_shape=[
            jax.ShapeDtypeStruct((N_ALL, SG_WIDTH), BF16),
            jax.ShapeDtypeStruct((DEC_BATCH, DEC_SEQ, SG_WIDTH), F32),
        ],
        scratch_shapes=[pltpu.VMEM((ROW_TILE, SG_WIDTH), F32)],
        compiler_params=_params(1),
        name="spatial_gate",
    )(uv, uv, ln_g, ln_b, sg_ws, b_p, w_s, b_s)


def _merge_kernel(ap_ref, as_ref, gs_ref, b_ref, ga_ref, gb_ref, xp_ref, xs_ref, wa_ref, wb_ref, wo_ref,
                  g_ref, wq_ref, x1_ref, q_ref):
    i = pl.program_id(0)

    def run(a_norm, x):
        b = _dot(b_ref[...], wb_ref[...])
        a_in = (gs_ref[...].astype(F32) * a_norm.astype(F32)).astype(BF16)
        a = _dot(a_in, wa_ref[...])
        merged = ga_ref[...].astype(F32) * a + gb_ref[...].astype(F32) * b
        x1 = x + _dot(merged.astype(BF16), wo_ref[...])
        x1_ref[...] = x1
        q_ref[...] = _dot(_rms(x1, g_ref[...]).astype(BF16), wq_ref[...]).astype(BF16)

    @pl.when(i < MERGE_P_TILES)
    def _():
        run(ap_ref[...], xp_ref[...])

    @pl.when(i >= MERGE_P_TILES)
    def _():
        run(as_ref[...], xs_ref[...].reshape(MERGE_TILE, D_MODEL))


def _resident(shape):
    return pl.BlockSpec(shape, lambda i: (0,) * len(shape), pipeline_mode=pl.Buffered(1))


def _merge(a_p, a_s, gs, b_all, gab, xp, xs, wa, wb, wo, g_xa, wq):
    tm = MERGE_TILE
    prompt_tile = lambda i: (jnp.minimum(i, MERGE_P_TILES - 1), 0)
    sample_tile = lambda i: (jnp.maximum(i - MERGE_P_TILES, 0), 0)
    return pl.pallas_call(
        _merge_kernel,
        grid=(N_ALL // tm,),
        in_specs=[
            pl.BlockSpec((tm, RET_V), prompt_tile),
            pl.BlockSpec((tm, RET_V), sample_tile),
            pl.BlockSpec((tm, RET_V), lambda i: (i, 0)),
            pl.BlockSpec((tm, SG_WIDTH), lambda i: (i, 0)),
            pl.BlockSpec((tm, D_MODEL), lambda i: (i, 0)),
            pl.BlockSpec((tm, D_MODEL), lambda i: (i, 1)),
            pl.BlockSpec((tm, D_MODEL), prompt_tile),
            pl.BlockSpec((tm // DEC_SEQ, DEC_SEQ, D_MODEL), lambda i: (jnp.maximum(i - MERGE_P_TILES, 0), 0, 0)),
            _resident((RET_V, D_MODEL)),
            _resident((SG_WIDTH, D_MODEL)),
            _resident((D_MODEL, D_MODEL)),
            pl.BlockSpec((1, D_MODEL), lambda i: (0, 0)),
            _resident((D_MODEL, XA_W)),
        ],
        out_specs=[
            pl.BlockSpec((tm, D_MODEL), lambda i: (i, 0)),
            pl.BlockSpec((tm, XA_W), lambda i: (i, 0)),
        ],
        out_shape=[
            jax.ShapeDtypeStruct((N_ALL, D_MODEL), F32),
            jax.ShapeDtypeStruct((N_ALL, XA_W), BF16),
        ],
        compiler_params=_params(1),
        name="merge_proj",
    )(a_p, a_s, gs, b_all, gab, gab, xp, xs, wa, wb, wo, g_xa, wq)


def _memkv_kernel(m_ref, wk_ref, wv_ref, k_ref, v_ref, kb_ref, vb_ref):
    m = m_ref[...].astype(BF16)
    k = _dot(m, wk_ref[...].astype(BF16))
    v = _dot(m, wv_ref[...].astype(BF16))
    rows = m.shape[0]
    k_ref[0] = k.reshape(rows // MEM_LEN, MEM_LEN, XA_HEADS, XA_DH)
    v_ref[0] = v.reshape(rows // MEM_LEN, MEM_LEN, XA_HEADS, XA_DH)
    kb_ref[...] = k.astype(BF16)
    vb_ref[...] = v.astype(BF16)


def _memkv(mem, w_ck, w_cv):
    rows = BATCH * MEM_LEN
    half = BATCH // 2
    spec = pl.BlockSpec((half * MEM_LEN, XA_W), lambda b: (b, 0))
    spec5 = pl.BlockSpec((1, half, MEM_LEN, XA_HEADS, XA_DH), lambda b: (0, b, 0, 0, 0))
    shape5 = jax.ShapeDtypeStruct((1, BATCH, MEM_LEN, XA_HEADS, XA_DH), F32)
    wspec = pl.BlockSpec((D_MODEL, XA_W), lambda b: (0, 0))
    return pl.pallas_call(
        _memkv_kernel,
        grid=(2,),
        in_specs=[pl.BlockSpec((half * MEM_LEN, D_MODEL), lambda b: (b, 0)), wspec, wspec],
        out_specs=[spec5, spec5, spec, spec],
        out_shape=[shape5, shape5] + [jax.ShapeDtypeStruct((rows, XA_W), BF16)] * 2,
        compiler_params=_params(1),
        name="mem_kv",
    )(mem, w_ck, w_cv)


def _softmax_rows(s):
    m = jnp.max(s, axis=-1, keepdims=True)
    e = jnp.exp(s - m)
    return e / jnp.sum(e, axis=-1, keepdims=True)


def _xattn_prompt_rows(q_ref, k_ref, v_ref, after_scores=lambda: None):
    cols = [slice(h * XA_DH, (h + 1) * XA_DH) for h in range(XA_HEADS)]
    scores = [_dot_nt(q_ref[:, c], k_ref[:, c]) * (XA_DH ** -0.5) for c in cols]
    after_scores()
    probs = [_softmax_rows(s).astype(BF16) for s in scores]
    heads = [_dot(p, v_ref[:, c]).astype(BF16) for p, c in zip(probs, cols)]
    return jnp.concatenate(heads, axis=1)


def _xattn_sample_scores(q_ref, k_ref):
    qf = q_ref[...].astype(F32)
    scores = []
    for b in range(SAMPLE_BB):
        rows = slice(b * DEC_SEQ, (b + 1) * DEC_SEQ)
        qb = jnp.concatenate([qf[rows, h * XA_DH:(h + 1) * XA_DH] for h in range(XA_HEADS)], axis=0)
        scores.append(_dot_nt(qb.astype(BF16), k_ref[b].astype(BF16)) * (XA_DH ** -0.5))
    return scores


def _xattn_sample_finish(scores, v_ref, o_ref):
    n_q = XA_HEADS * DEC_SEQ
    n_kv = MEM_LEN * XA_HEADS
    r = lax.broadcasted_iota(jnp.int32, (n_q, n_kv), 0)
    c = lax.broadcasted_iota(jnp.int32, (n_q, n_kv), 1)
    head_ok = (r >> LOG_DEC_SEQ) == (c & (XA_HEADS - 1))
    probs = [_softmax_rows(jnp.where(head_ok, s, -1e30)).astype(BF16) for s in scores]
    for b in range(SAMPLE_BB):
        rows = slice(b * DEC_SEQ, (b + 1) * DEC_SEQ)
        o = _dot(probs[b], v_ref[b].astype(BF16))
        for h in range(XA_HEADS):
            o_ref[h, rows, :] = o[h * DEC_SEQ:(h + 1) * DEC_SEQ, :]


def _route_kernel(qp_ref, mk_ref, mv_ref, qs_ref, ck_ref, cv_ref, x1_ref, wo_ref, g_ref, wr_ref, br_ref,
                  x2_ref, hm_ref, route_ref, cnt_ref, cnti_ref, run_ref, os_ref):
    i = pl.program_id(0)

    @pl.when(i == 0)
    def _():
        run_ref[...] = jnp.zeros_like(run_ref)

    def run(o, between=lambda: None):
        x2 = x1_ref[...] + _dot(o, wo_ref[...])
        x2_ref[...] = x2
        hm = _rms(x2, g_ref[...])
        hm_ref[...] = _to_row_tiles(hm)
        logits = _dot(hm.astype(BF16), wr_ref[...]) + br_ref[...]
        lane = lax.broadcasted_iota(jnp.int32, logits.shape, 1)
        lane_f = lane.astype(F32)
        neg = jnp.float32(-jnp.inf)
        big = jnp.float32(1 << 20)
        is_g = lane < MOE_GROUPS
        gl = jnp.where(is_g, logits, neg)
        gmax = jnp.max(gl, axis=-1, keepdims=True)
        g_sel = jnp.min(jnp.where(gl == gmax, lane_f, big), axis=-1, keepdims=True)
        g_w = 1.0 / jnp.sum(jnp.where(is_g, jnp.exp(logits - gmax), 0.0), axis=-1, keepdims=True)
        between()
        e_lane = lane - MOE_GROUPS
        e_group = (e_lane >> 3).astype(F32)
        in_grp = (e_lane >= 0) & (e_lane < MOE_EXPERTS) & (e_group == g_sel)
        el = jnp.where(in_grp, logits, neg)
        v0 = jnp.max(el, axis=-1, keepdims=True)
        i0 = jnp.min(jnp.where(el == v0, lane_f, big), axis=-1, keepdims=True)
        el1 = jnp.where(lane_f == i0, neg, el)
        v1 = jnp.max(el1, axis=-1, keepdims=True)
        i1 = jnp.min(jnp.where(el1 == v1, lane_f, big), axis=-1, keepdims=True)
        ex = jnp.exp(v1 - v0)
        den = 1.0 + ex
        w0 = (1.0 / den) * g_w
        w1 = (ex / den) * g_w
        a0 = (lane_f == i0).astype(F32)
        a1 = (lane_f == i1).astype(F32)
        a = a0 + a1
        rr = lax.broadcasted_iota(jnp.int32, (ROW_TILE, ROW_TILE), 0)
        cc = lax.broadcasted_iota(jnp.int32, (ROW_TILE, ROW_TILE), 1)
        lower = jnp.where(cc < rr, 1.0, 0.0).astype(BF16)
        before = _dot(lower, a.astype(BF16)) + run_ref[...]
        rank0 = jnp.sum(before * a0, axis=-1, keepdims=True)
        rank1 = jnp.sum(before * a1, axis=-1, keepdims=True)
        run_ref[...] += jnp.sum(a, axis=0, keepdims=True)
        e0 = i0 - MOE_GROUPS
        e1 = i1 - MOE_GROUPS
        route = jnp.where(lane == 0, e0, 0.0)
        route = jnp.where(lane == 1, e1, route)
        route = jnp.where(lane == 2, rank0, route)
        route = jnp.where(lane == 3, rank1, route)
        route = jnp.where(lane == 4, w0, route)
        route = jnp.where(lane == 5, w1, route)
        route_ref[...] = route
        cnt_ref[...] = run_ref[...]
        cnti_ref[...] = run_ref[...].astype(jnp.int32)

    @pl.when(i < NP_TILES)
    def _():
        scores = []
        o_prompt = _xattn_prompt_rows(qp_ref, mk_ref, mv_ref,
                                      after_scores=lambda: scores.extend(_xattn_sample_scores(qs_ref, ck_ref)))
        run(o_prompt, between=lambda: _xattn_sample_finish(scores, cv_ref, os_ref.at[i]))

    @pl.when(i == NP_TILES)
    def _():
        heads = [os_ref[:, h].reshape(N_S, XA_DH) for h in range(XA_HEADS)]
        run(jnp.concatenate(heads, axis=1).astype(BF16))


def _route(qx, mkb, mvb, ck, cv, x1, w_co, g_moe, w_r, b_r):
    prompt_step = lambda i: jnp.minimum(i, NP_TILES - 1)
    kv = pl.BlockSpec((SAMPLE_BB, MEM_LEN * XA_HEADS, XA_DH), lambda i: (prompt_step(i), 0, 0))
    mem_kv = pl.BlockSpec((MEM_LEN, XA_W), lambda i: (prompt_step(i) // (SEQ // ROW_TILE), 0))
    return pl.pallas_call(
        _route_kernel,
        grid=(N_TILES,),
        in_specs=[
            pl.BlockSpec((ROW_TILE, XA_W), lambda i: (prompt_step(i), 0)),
            mem_kv, mem_kv,
            pl.BlockSpec((SAMPLE_ROWS, XA_W), lambda i: (N_P // SAMPLE_ROWS + prompt_step(i), 0)),
            kv, kv,
            pl.BlockSpec((ROW_TILE, D_MODEL), lambda i: (i, 0)),
            _resident((XA_W, D_MODEL)),
            pl.BlockSpec((1, D_MODEL), lambda i: (0, 0)),
            _resident((D_MODEL, ROUTE_LANES)),
            pl.BlockSpec((1, ROUTE_LANES), lambda i: (0, 0)),
        ],
        out_specs=[
            pl.BlockSpec((ROW_TILE, D_MODEL), lambda i: (i, 0)),
            pl.BlockSpec((ROW_TILE, ROW_SUB, ROW_LANE), lambda i: (i, 0, 0)),
            pl.BlockSpec((ROW_TILE, ROUTE_LANES), lambda i: (i, 0)),
            pl.BlockSpec((1, ROUTE_LANES), lambda i: (0, 0)),
            pl.BlockSpec((1, ROUTE_LANES), lambda i: (0, 0)),
        ],
        out_shape=[
            jax.ShapeDtypeStruct((N_ALL, D_MODEL), F32),
            jax.ShapeDtypeStruct((N_ALL, ROW_SUB, ROW_LANE), BF16),
            jax.ShapeDtypeStruct((N_ALL, ROUTE_LANES), F32),
            jax.ShapeDtypeStruct((1, ROUTE_LANES), F32),
            jax.ShapeDtypeStruct((1, ROUTE_LANES), jnp.int32),
        ],
        scratch_shapes=[pltpu.VMEM((1, ROUTE_LANES), F32),
                        pltpu.VMEM((NP_TILES, XA_HEADS, SAMPLE_ROWS, XA_DH), F32)],
        compiler_params=_params(1),
        name="xa_out_route",
    )(qx, mkb, mvb, qx, ck, cv, x1, w_co, g_moe, w_r, b_r)


def _positions_kernel(route_ref, cnt_ref, pos_ref):
    route = route_ref[...]
    lane = lax.broadcasted_iota(jnp.int32, route.shape, 1)
    lane_f = lane.astype(F32)
    tiles = jnp.floor((cnt_ref[...] + (MOE_TM - 1)) * (1.0 / MOE_TM))
    lr = lax.broadcasted_iota(jnp.int32, (ROUTE_LANES, ROUTE_LANES), 0)
    lc = lax.broadcasted_iota(jnp.int32, (ROUTE_LANES, ROUTE_LANES), 1)
    before = jnp.where(lr < lc, 1.0, 0.0).astype(BF16)
    tiles8 = jnp.broadcast_to(tiles, (8, ROUTE_LANES)).astype(BF16)
    start = _dot(tiles8, before)[0:1, :] * MOE_TM

    def col(k):
        return jnp.sum(jnp.where(lane == k, route, 0.0), axis=-1, keepdims=True)

    def first_row(e):
        return jnp.sum(jnp.where(lane_f == e + MOE_GROUPS, start, 0.0), axis=-1, keepdims=True)

    p0 = first_row(col(0)) + col(2)
    p1 = first_row(col(1)) + col(3)
    p = jnp.where(lane == 0, p0, jnp.where(lane == 1, p1, 0.0))
    pos_ref[...] = p.T[0:8, :].astype(jnp.int32)


def _positions(route, counts):
    rows = N_ALL // 4
    return pl.pallas_call(
        _positions_kernel,
        grid=(4,),
        in_specs=[
            pl.BlockSpec((rows, ROUTE_LANES), lambda i: (i, 0)),
            pl.BlockSpec((1, ROUTE_LANES), lambda i: (0, 0)),
        ],
        out_specs=pl.BlockSpec((8, rows), lambda i: (0, i)),
        out_shape=jax.ShapeDtypeStruct((8, N_ALL), jnp.int32),
        compiler_params=_params(1),
        name="positions",
    )(route, counts)


def _dispatch_kernel(pos_ref, zrow_ref, zon_ref, nu_ref, hm_ref, xs_ref, zbuf, sem, zsem, tbuf, tsem):
    i = pl.program_id(0)

    def zero_tile(row):
        return pltpu.make_async_copy(zbuf, xs_ref.at[pl.ds(pl.multiple_of(row, MOE_TM), MOE_TM)], zsem)

    @pl.when(i == 0)
    def _():
        zbuf[...] = jnp.zeros_like(zbuf)
        for e in range(MOE_EXPERTS):
            @pl.when(zon_ref[e] > 0)
            def _():
                zero_tile(zrow_ref[e]).start()

        def start_tail(t, carry):
            zero_tile(t * MOE_TM).start()
            return carry

        def wait_tail(t, carry):
            zero_tile(t * MOE_TM).wait()
            return carry

        lax.fori_loop(nu_ref[0], MOE_NT, start_tail, 0)
        for e in range(MOE_EXPERTS):
            @pl.when(zon_ref[e] > 0)
            def _():
                zero_tile(zrow_ref[e]).wait()
        lax.fori_loop(nu_ref[0], MOE_NT, wait_tail, 0)

    n_steps = pl.num_programs(0)

    def tile_copy(t):
        return pltpu.make_async_copy(hm_ref.at[pl.ds(t * ROW_TILE, ROW_TILE)], tbuf.at[t % 3], tsem.at[t % 3])

    def wait_rows(t):
        for k in range(MOE_TOPK):
            pltpu.make_async_copy(tbuf.at[t % 3], xs_ref.at[pl.ds(0, ROW_TILE)], sem.at[t % 2, k]).wait()

    @pl.when(i == 0)
    def _():
        tile_copy(0).start()

    tile_copy(i).wait()

    @pl.when(i + 1 < n_steps)
    def _():
        tile_copy(i + 1).start()

    slot = i % 3
    par = i % 2
    base = i * ROW_TILE

    def start(r, carry):
        for k in range(MOE_TOPK):
            dst_row = pos_ref[k, base + r]
            pltpu.make_async_copy(tbuf.at[slot, r], xs_ref.at[dst_row], sem.at[par, k]).start(priority=k)
        return carry

    lax.fori_loop(0, ROW_TILE, start, 0, unroll=32)

    @pl.when(i > 0)
    def _():
        wait_rows(i - 1)

    @pl.when(i == n_steps - 1)
    def _():
        wait_rows(i)


def _dispatch(pos_t, zero_row, zero_on, n_used, hmw):
    grid_spec = pltpu.PrefetchScalarGridSpec(
        num_scalar_prefetch=4,
        grid=(N_TILES,),
        in_specs=[pl.BlockSpec(memory_space=pl.ANY)],
        out_specs=pl.BlockSpec(memory_space=pl.ANY),
        scratch_shapes=[pltpu.VMEM((MOE_TM, ROW_SUB, ROW_LANE), BF16),
                        pltpu.SemaphoreType.DMA((2, MOE_TOPK)), pltpu.SemaphoreType.DMA(()),
                        pltpu.VMEM((3, ROW_TILE, ROW_SUB, ROW_LANE), BF16), pltpu.SemaphoreType.DMA((3,))],
    )
    return pl.pallas_call(
        _dispatch_kernel,
        grid_spec=grid_spec,
        out_shape=jax.ShapeDtypeStruct((MOE_ROWS, ROW_SUB, ROW_LANE), BF16),
        compiler_params=_params(1),
        name="dispatch",
    )(pos_t, zero_row, zero_on, n_used, hmw)


def _expert_kernel(nu_ref, first_ref, ord_ref, oe_ref, no_ref, half_ref, x_ref, w1_hbm, w3_hbm, w2_hbm, y_ref,
                   w1b, w3b, w2b, sem):
    i = pl.program_id(0)
    n_used = nu_ref[0]
    n_ord = no_ref[0]

    def weight_copies(k):
        e = oe_ref[k]
        slot = k % W_SLOTS
        return (pltpu.make_async_copy(w1_hbm.at[e], w1b.at[slot], sem.at[0, slot]),
                pltpu.make_async_copy(w3_hbm.at[e], w3b.at[slot], sem.at[1, slot]),
                pltpu.make_async_copy(w2_hbm.at[e], w2b.at[slot], sem.at[2, slot]))

    def start_weights(k):
        for cp in weight_copies(k):
            cp.start(priority=1)

    @pl.when(i == 0)
    def _():
        for k in range(W_SLOTS - 1):
            @pl.when(k < n_ord)
            def _():
                start_weights(k)

    @pl.when(i < n_used)
    def _():
        k = ord_ref[i]

        @pl.when(first_ref[i] > 0)
        def _():
            for cp in weight_copies(k):
                cp.wait()

            @pl.when(k + (W_SLOTS - 1) < n_ord)
            def _():
                start_weights(k + (W_SLOTS - 1))

        slot = k % W_SLOTS

        def swiglu(rows):
            x = _from_row_tiles(x_ref[0:rows])
            h1 = _dot(x, w1b[slot].astype(BF16))
            h3 = _dot(x, w3b[slot].astype(BF16))
            he = (_silu(h1) * h3).astype(BF16)
            y_ref[0:rows] = _to_row_tiles(_dot(he, w2b[slot].astype(BF16)))

        @pl.when(half_ref[i] == 0)
        def _():
            swiglu(MOE_TM)

        @pl.when(half_ref[i] > 0)
        def _():
            swiglu(MOE_TM // 2)
            y_ref[MOE_TM // 2:] = jnp.zeros((MOE_TM // 2, ROW_SUB, ROW_LANE), BF16)

    @pl.when(i >= n_used)
    def _():
        y_ref[...] = jnp.zeros_like(y_ref)


def _experts(sched, xs, w_e1, w_e3, w_e2):
    grid_spec = pltpu.PrefetchScalarGridSpec(
        num_scalar_prefetch=6,
        grid=(MOE_NT,),
        in_specs=[
            pl.BlockSpec((MOE_TM, ROW_SUB, ROW_LANE), lambda i, nu, *_: (jnp.minimum(i, nu[0] - 1), 0, 0)),
            pl.BlockSpec(memory_space=pl.ANY),
            pl.BlockSpec(memory_space=pl.ANY),
            pl.BlockSpec(memory_space=pl.ANY),
        ],
        out_specs=pl.BlockSpec((MOE_TM, ROW_SUB, ROW_LANE), lambda i, *_: (i, 0, 0)),
        scratch_shapes=[
            pltpu.VMEM((W_SLOTS, D_MODEL, MOE_FF), F32),
            pltpu.VMEM((W_SLOTS, D_MODEL, MOE_FF), F32),
            pltpu.VMEM((W_SLOTS, MOE_FF, D_MODEL), F32),
            pltpu.SemaphoreType.DMA((3, W_SLOTS)),
        ],
    )
    return pl.pallas_call(
        _expert_kernel,
        grid_spec=grid_spec,
        out_shape=jax.ShapeDtypeStruct((MOE_ROWS, ROW_SUB, ROW_LANE), BF16),
        compiler_params=_params(1),
        name="experts",
    )(sched["n_used"], sched["tile_first"], sched["tile_ord"], sched["ord_expert"], sched["n_ord"],
      sched["tile_half"], xs, w_e1, w_e3, w_e2)


def _combine_kernel(pos_ref, ys_ref, x2_ref, route_ref, g_ref, yp_ref, ysm_ref, gbuf, sem):
    i = pl.program_id(0)

    def start_gather(tile, slot):
        def body(r, carry):
            for k in range(MOE_TOPK):
                src_row = pos_ref[k, tile * ROW_TILE + r]
                pltpu.make_async_copy(ys_ref.at[src_row], gbuf.at[slot, k, r], sem.at[slot, k]).start(priority=k)
            return carry
        lax.fori_loop(0, ROW_TILE, body, 0, unroll=32)

    @pl.when(i == 0)
    def _():
        start_gather(0, 0)

    slot = i % 2

    @pl.when(i + 1 < pl.num_programs(0))
    def _():
        start_gather(i + 1, 1 - slot)

    for k in range(MOE_TOPK):
        pltpu.make_async_copy(ys_ref.at[pl.ds(0, ROW_TILE)], gbuf.at[slot, k], sem.at[slot, k]).wait()

    route = route_ref[...]
    lane = lax.broadcasted_iota(jnp.int32, route.shape, 1)
    w0 = jnp.sum(jnp.where(lane == 4, route, 0.0), axis=-1, keepdims=True)
    w1 = jnp.sum(jnp.where(lane == 5, route, 0.0), axis=-1, keepdims=True)
    g0 = _from_row_tiles(gbuf[slot, 0]).astype(F32)
    g1 = _from_row_tiles(gbuf[slot, 1]).astype(F32)
    x3 = x2_ref[...] + (g0 * w0 + g1 * w1)
    y = _rms(x3, g_ref[...])

    @pl.when(i < NP_TILES)
    def _():
        yp_ref[...] = y

    @pl.when(i == NP_TILES)
    def _():
        ysm_ref[...] = y.reshape(DEC_BATCH, DEC_SEQ, D_MODEL)


def _combine(pos, ys, x2, route, g_f):
    grid_spec = pltpu.PrefetchScalarGridSpec(
        num_scalar_prefetch=1,
        grid=(N_TILES,),
        in_specs=[
            pl.BlockSpec(memory_space=pl.ANY),
            pl.BlockSpec((ROW_TILE, D_MODEL), lambda i, pos: (i, 0)),
            pl.BlockSpec((ROW_TILE, ROUTE_LANES), lambda i, pos: (i, 0)),
            pl.BlockSpec((1, D_MODEL), lambda i, pos: (0, 0)),
        ],
        out_specs=[
            pl.BlockSpec((ROW_TILE, D_MODEL), lambda i, pos: (jnp.minimum(i, NP_TILES - 1), 0)),
            pl.BlockSpec((DEC_BATCH, DEC_SEQ, D_MODEL), lambda i, pos: (0, 0, 0)),
        ],
        scratch_shapes=[pltpu.VMEM((2, MOE_TOPK, ROW_TILE, ROW_SUB, ROW_LANE), BF16),
                        pltpu.SemaphoreType.DMA((2, MOE_TOPK))],
    )
    return pl.pallas_call(
        _combine_kernel,
        grid_spec=grid_spec,
        out_shape=[
            jax.ShapeDtypeStruct((N_P, D_MODEL), F32),
            jax.ShapeDtypeStruct((DEC_BATCH, DEC_SEQ, D_MODEL), F32),
        ],
        compiler_params=_params(1),
        name="combine_norm",
    )(pos, ys, x2, route, g_f)


_SCHEDULE_FIELDS = ("n_used", "tile_first", "tile_ord", "tile_half", "ord_expert", "n_ord", "zero_row", "zero_on")


def _schedule_kernel(cnt_ref, nu_ref, first_ref, ord_ref, half_ref, oe_ref, no_ref, zrow_ref, zon_ref):
    i32 = jnp.int32

    def clear_tile(t, carry):
        first_ref[t] = i32(0)
        ord_ref[t] = i32(0)
        half_ref[t] = i32(0)
        return carry

    lax.fori_loop(0, MOE_NT, clear_tile, 0)

    def clear_expert(e, carry):
        oe_ref[e] = i32(0)
        return carry

    lax.fori_loop(0, MOE_EXPERTS, clear_expert, 0)

    def expert(e, carry):
        t0, k = carry
        c = cnt_ref[0, MOE_GROUPS + e]
        nt = lax.shift_right_logical(c + (MOE_TM - 1), i32(LOG_MOE_TM))
        used = jnp.where(nt > 0, i32(1), i32(0))
        zon_ref[e] = used
        zrow_ref[e] = jnp.maximum(t0 + nt - 1, 0) * MOE_TM

        @pl.when(nt > 0)
        def _():
            oe_ref[k] = e

        def tile(j, cc):
            t = t0 + j
            first_ref[t] = jnp.where(j == 0, i32(1), i32(0))
            ord_ref[t] = k
            half_ref[t] = jnp.where(c - j * MOE_TM <= MOE_TM // 2, i32(1), i32(0))
            return cc

        lax.fori_loop(0, nt, tile, 0)
        return t0 + nt, k + used

    n_tiles, n_experts = lax.fori_loop(0, MOE_EXPERTS, expert, (i32(0), i32(0)))
    nu_ref[0] = n_tiles
    no_ref[0] = n_experts


def _expert_schedule(counts_i32):
    smem = pl.BlockSpec(memory_space=pltpu.SMEM)
    sizes = (1, MOE_NT, MOE_NT, MOE_NT, MOE_EXPERTS, 1, MOE_EXPERTS, MOE_EXPERTS)
    outs = pl.pallas_call(
        _schedule_kernel,
        in_specs=[smem],
        out_specs=[smem] * len(sizes),
        out_shape=[jax.ShapeDtypeStruct((n,), jnp.int32) for n in sizes],
        name="expert_schedule",
    )(counts_i32)
    return dict(zip(_SCHEDULE_FIELDS, outs))


def kernel(x_prompt, x_sample, mem_prompt, state_ret, cache_mem_k, cache_mem_v, norm_mix, w_in, ret_gn,
           sg_ln_g, sg_ln_b, sg_ws, sg_bs, w_a_out, w_b_out, w_o, norm_xa, w_cq, w_ck, w_cv, w_co, norm_moe,
           w_rg, b_rg, w_re, b_re, w_e1, w_e3, w_e2, norm_f):
    xp = x_prompt.reshape(N_P, D_MODEL)
    xs = x_sample

    h = _norm_rows(xp, xs, norm_mix)
    w = w_in[0]
    blocks = lambda m: m.reshape(MERGE_W_BLOCKS, D_MODEL // MERGE_W_BLOCKS, D_MODEL)
    qk, wa = _inproj("rope", IN_TM, 0, 2 * RET_QK, h, w, _rope_tables(IN_TM), side=blocks(w_a_out[0]))
    v, wb = _inproj("copy", IN_TM, 2 * RET_QK, RET_V, h, w, side=blocks(w_b_out[0]))
    gs, wo = _inproj("silu", IN_TM, 2 * RET_QK + RET_V, RET_V, h, w, side=blocks(w_o[0]))
    uv = _inproj("gelu", IN_TM, 2 * RET_QK + 2 * RET_V, 2 * SG_WIDTH, h, w)
    gab = _inproj("sigmoid", IN_TM, 2 * RET_QK + 2 * RET_V + 2 * SG_WIDTH, 2 * D_MODEL, h, w)

    a_p, ret_p, a_s, ret_s = _retention(qk, v, ret_gn, state_ret)

    b_all, sgv = _sgate(uv, sg_ln_g, sg_ln_b, sg_ws[0], sg_bs[0])

    square = lambda m: m.reshape(D_MODEL, D_MODEL)
    x1, qx = _merge(a_p, a_s, gs, b_all, gab, xp, xs, square(wa), square(wb), square(wo), norm_xa,
                    w_cq[0].astype(BF16))

    mk, mv, mkb, mvb = _memkv(mem_prompt.reshape(BATCH * MEM_LEN, D_MODEL), w_ck[0], w_cv[0])
    ck = cache_mem_k.reshape(DEC_BATCH, MEM_LEN * XA_HEADS, XA_DH)
    cv = cache_mem_v.reshape(DEC_BATCH, MEM_LEN * XA_HEADS, XA_DH)

    pad = ROUTE_LANES - MOE_GROUPS - MOE_EXPERTS
    w_r = jnp.concatenate([w_rg[0], w_re[0], jnp.zeros((D_MODEL, pad), F32)], axis=1)
    b_r = jnp.concatenate([b_rg[0], b_re[0], jnp.zeros((pad,), F32)]).reshape(1, ROUTE_LANES)
    x2, hmw, route, counts, counts_i32 = _route(qx, mkb, mvb, ck, cv, x1, w_co[0].astype(BF16), norm_moe,
                                                w_r.astype(BF16), b_r)

    pos_t = _positions(route, counts)
    sched = _expert_schedule(counts_i32)
    xs_sorted = _dispatch(pos_t, sched["zero_row"], sched["zero_on"], sched["n_used"], hmw)
    ys = _experts(sched, xs_sorted, w_e1[0], w_e3[0], w_e2[0])
    y_p, y_s = _combine(pos_t, ys, x2, route, norm_f.reshape(1, D_MODEL))

    return (y_p.reshape(BATCH, SEQ, D_MODEL), y_s, ret_p, mk, mv, ret_s,
            sgv.reshape(1, DEC_BATCH, DEC_SEQ, SG_WIDTH))
```

```python
import functools

import jax
import jax.numpy as jnp
import numpy as np
from jax import lax
from jax.experimental import pallas as pl
from jax.experimental.pallas import tpu as pltpu

F32 = jnp.float32
BF16 = jnp.bfloat16

D_MODEL = 2048
BATCH = 4
SEQ = 2048
DEC_BATCH = 128
DEC_SEQ = 4
PAST_LEN = 16384
RET_HEADS = 8
RET_DK = 128
RET_DV = 256
ROPE_BASE = 10000.0
RET_QK = RET_HEADS * RET_DK
RET_V = RET_HEADS * RET_DV
SG_GROUPS = 4
SG_WIDTH = 2048
SG_CHUNK = 128
MEM_LEN = 256
XA_HEADS = 4
XA_DH = 128
XA_W = XA_HEADS * XA_DH
MOE_GROUPS = 4
MOE_PER_GROUP = 8
MOE_EXPERTS = MOE_GROUPS * MOE_PER_GROUP
MOE_TOPK = 2
MOE_FF = 512
EPS = 1e-6

N_P = BATCH * SEQ
N_S = DEC_BATCH * DEC_SEQ
N_ALL = N_P + N_S
ROW_TILE = 512
N_TILES = N_ALL // ROW_TILE
NP_TILES = N_P // ROW_TILE
MERGE_TILE = 256
MERGE_P_TILES = N_P // MERGE_TILE

IN_TM = 1024
IN_TN = 2048
MERGE_W_BLOCKS = 4

PROMPT_CHUNK = 256
RET_BB = DEC_BATCH // (BATCH * (SEQ // PROMPT_CHUNK))
RET_ROWS = RET_BB * DEC_SEQ
LOG_DEC_SEQ = 2
SAMPLE_BB = DEC_BATCH // NP_TILES
SAMPLE_ROWS = SAMPLE_BB * DEC_SEQ

MOE_TM = 256
LOG_MOE_TM = 8
MOE_NT = (N_ALL * MOE_TOPK + MOE_EXPERTS * (MOE_TM - 1) + MOE_TM - 1) // MOE_TM
MOE_ROWS = MOE_NT * MOE_TM
ROUTE_LANES = 128
ROW_SUB, ROW_LANE = 16, 128
W_SLOTS = 3

VMEM_LIMIT = 56 * 1024 * 1024


def _params(n_axes, vmem=VMEM_LIMIT):
    return pltpu.CompilerParams(dimension_semantics=("arbitrary",) * n_axes,
                                vmem_limit_bytes=vmem)


def _rms(x, g):
    ms = jnp.mean(x * x, axis=-1, keepdims=True)
    return (x * lax.rsqrt(ms + EPS)) * g


def _dot(a, b):
    return jnp.dot(a, b, preferred_element_type=F32)


def _sigmoid(x):
    return 0.5 * jnp.tanh(0.5 * x) + 0.5


def _silu(x):
    hx = 0.5 * x
    return hx + hx * jnp.tanh(hx)


_GELU_C0 = float(np.float32(np.sqrt(2.0 / np.pi)))


def _gelu_tanh(x):
    hx = 0.5 * x
    return hx + hx * jnp.tanh(x * (_GELU_C0 + (_GELU_C0 * 0.044715) * (x * x)))


def _to_row_tiles(x):
    return x.astype(BF16).reshape(x.shape[0], ROW_SUB, ROW_LANE)


def _from_row_tiles(t):
    return t.reshape(t.shape[0], D_MODEL)


def _dot_nt(a, b):
    return lax.dot_general(a, b, (((1,), (1,)), ((), ())), preferred_element_type=F32)


def _dot_tn(a, b):
    return lax.dot_general(a, b, (((0,), (0,)), ((), ())), preferred_element_type=F32)


def _norm_kernel(xp_ref, xs_ref, g_ref, h_ref):
    i = pl.program_id(0)

    @pl.when(i < NP_TILES)
    def _():
        h_ref[...] = _rms(xp_ref[...], g_ref[...]).astype(BF16)

    @pl.when(i == NP_TILES)
    def _():
        h_ref[...] = _rms(xs_ref[...].reshape(N_S, D_MODEL), g_ref[...]).astype(BF16)


def _norm_rows(xp, xs, g):
    return pl.pallas_call(
        _norm_kernel,
        grid=(N_TILES,),
        in_specs=[
            pl.BlockSpec((ROW_TILE, D_MODEL), lambda i: (jnp.minimum(i, NP_TILES - 1), 0)),
            pl.BlockSpec((DEC_BATCH, DEC_SEQ, D_MODEL), lambda i: (0, 0, 0)),
            pl.BlockSpec((1, D_MODEL), lambda i: (0, 0)),
        ],
        out_specs=pl.BlockSpec((ROW_TILE, D_MODEL), lambda i: (i, 0)),
        out_shape=jax.ShapeDtypeStruct((N_ALL, D_MODEL), BF16),
        compiler_params=_params(1),
        name="norm_rows",
    )(xp, xs, g)


def _inproj_kernel(kind, tm, n_side, h_ref, w_ref, *rest):
    rest = list(rest)
    cos_ref, sin_ref = (rest.pop(0), rest.pop(0)) if kind == "rope" else (None, None)
    side_ref = rest.pop(0) if n_side else None
    z_ref = rest.pop(0)
    side_out_ref = rest.pop(0) if n_side else None
    j = pl.program_id(0)
    i = pl.program_id(1)
    last = N_P // tm

    if n_side:
        @pl.when(j * (last + 1) + i < n_side)
        def _():
            side_out_ref[...] = side_ref[...].astype(BF16)

    def tile(rows):
        acc = _dot(h_ref[0:rows, :], w_ref[...].astype(BF16))
        if kind == "rope":
            c = cos_ref[0:rows, :]
            s = sin_ref[0:rows, :]
            heads_per_block = IN_TN // RET_DK
            for hb in range(heads_per_block):
                scale = jnp.where(j * heads_per_block + hb >= RET_HEADS, RET_DK ** -0.5, 1.0).astype(F32)
                cols = slice(hb * RET_DK, (hb + 1) * RET_DK)
                a = acc[:, cols]
                r = pltpu.roll(a, RET_DK // 2, axis=1)
                z_ref[0:rows, cols] = ((a * c + r * s) * scale).astype(BF16)
        elif kind == "copy":
            z_ref[0:rows, :] = acc.astype(BF16)
        elif kind == "silu":
            z_ref[0:rows, :] = _silu(acc).astype(BF16)
        elif kind == "gelu":
            z_ref[0:rows, :] = _gelu_tanh(acc).astype(BF16)
        else:
            z_ref[0:rows, :] = _sigmoid(acc).astype(BF16)

    @pl.when(i < last)
    def _():
        tile(tm)

    @pl.when(i == last)
    def _():
        tile(N_S)


def _inproj(kind, tm, col0, width, h, w_in, tables=(), side=None):
    last = N_P // tm
    n_j = width // IN_TN
    tab_idx = lambda j, i: (jnp.where(i < last, i % (SEQ // tm), SEQ // tm), 0)
    j0 = col0 // IN_TN
    in_specs = [
        pl.BlockSpec((tm, D_MODEL), lambda j, i: (i, 0)),
        pl.BlockSpec((D_MODEL, IN_TN), lambda j, i: (0, j0 + j)),
    ] + [pl.BlockSpec((tm, RET_DK), tab_idx) for _ in tables]
    out_specs = [pl.BlockSpec((tm, IN_TN), lambda j, i: (i, j))]
    out_shape = [jax.ShapeDtypeStruct((N_ALL, width), BF16)]
    n_side = 0 if side is None else side.shape[0]
    if n_side:
        assert n_side <= n_j * (last + 1)
        blk = (1,) + side.shape[1:]
        side_blk = lambda j, i: (jnp.minimum(j * (last + 1) + i, n_side - 1), 0, 0)
        in_specs.append(pl.BlockSpec(blk, side_blk))
        out_specs.append(pl.BlockSpec(blk, side_blk))
        out_shape.append(jax.ShapeDtypeStruct(side.shape, BF16))
    outs = pl.pallas_call(
        functools.partial(_inproj_kernel, kind, tm, n_side),
        grid=(n_j, last + 1),
        in_specs=in_specs,
        out_specs=out_specs,
        out_shape=out_shape,
        compiler_params=_params(2),
        name="in_proj_" + kind,
    )(h, w_in, *tables, *(() if side is None else (side,)))
    return outs if n_side else outs[0]


def _rope_tables(tm):
    half = RET_DK // 2
    inv = ROPE_BASE ** (-jnp.arange(half, dtype=F32) / half)

    def tab(pos):
        ang = pos.astype(F32)[:, None] * inv[None, :]
        c, s = jnp.cos(ang), jnp.sin(ang)
        return jnp.concatenate([c, c], -1), jnp.concatenate([-s, s], -1)

    cp, sp = tab(jnp.arange(SEQ, dtype=jnp.int32))
    cs, ss = tab(PAST_LEN + jnp.arange(DEC_SEQ, dtype=jnp.int32))
    cs = jnp.tile(cs, (DEC_BATCH, 1))
    ss = jnp.tile(ss, (DEC_BATCH, 1))
    pad = jnp.zeros((tm - N_S, RET_DK), F32)
    return jnp.concatenate([cp, cs, pad], 0), jnp.concatenate([sp, ss, pad], 0)


def _decay_tables(chunk):
    lg = jnp.log1p(-jnp.power(2.0, -5.0 - jnp.arange(RET_HEADS, dtype=F32)))
    idx = jnp.arange(chunk, dtype=F32)
    rel = idx[:, None] - idx[None, :]
    dmask = jnp.where(rel >= 0, jnp.exp(lg[:, None, None] * jnp.maximum(rel, 0.0)), 0.0).astype(F32)
    xi = jnp.exp(lg[:, None] * (idx[None, :] + 1.0)).astype(F32)
    zeta = jnp.exp(lg[:, None] * (chunk - 1.0 - idx[None, :])).astype(F32)
    gc = jnp.exp(lg * chunk).astype(F32)
    return dmask, xi, zeta, gc


def _head_norm(o, gn):
    mu = jnp.mean(o, axis=-1, keepdims=True)
    d = o - mu
    var = jnp.mean(d * d, axis=-1, keepdims=True)
    return ((d * lax.rsqrt(var + EPS)) * gn).astype(BF16)


def _ret_kernel(gcp_ref, gcs_ref, q_ref, k_ref, v_ref, gn_ref, dm_ref, xi_ref, zt_ref,
                qs_ref, ks_ref, vs_ref, dms_ref, xis_ref, zts_ref, s0_ref,
                a_ref, sfin_ref, as_ref, s1_ref, s_ref):
    c = pl.program_id(1)

    @pl.when(c == 0)
    def _():
        s_ref[...] = jnp.zeros_like(s_ref)

    rows_k = lax.broadcasted_iota(jnp.int32, (RET_ROWS, RET_DK), 0) >> LOG_DEC_SEQ
    rows_v = lax.broadcasted_iota(jnp.int32, (RET_ROWS, RET_DV), 0) >> LOG_DEC_SEQ
    for h in range(RET_HEADS):
        kc = slice(h * RET_DK, (h + 1) * RET_DK)
        vc = slice(h * RET_DV, (h + 1) * RET_DV)
        gn = gn_ref[:, vc]

        qh = q_ref[:, kc]
        kh = k_ref[:, kc]
        vh = v_ref[:, vc]
        inner = _dot_nt(qh, kh) * dm_ref[h]
        o = _dot(inner.astype(BF16), vh)
        s_old = s_ref[h]
        xi = xi_ref[h]
        o = o + _dot(qh, s_old.astype(BF16)) * jnp.concatenate([xi, xi], axis=1)
        kz = (kh.astype(F32) * zt_ref[h]).astype(BF16)
        s_ref[h] = gcp_ref[h] * s_old + _dot_tn(kz, vh)
        a_ref[:, vc] = _head_norm(o, gn)

        qh = qs_ref[:, kc]
        kh = ks_ref[:, kc]
        vh = vs_ref[:, vc]
        inner = _dot_nt(qh, kh) * dms_ref[h]
        o = _dot(inner.astype(BF16), vh)
        xi = xis_ref[h]
        xi2 = jnp.concatenate([xi, xi], axis=1)
        kz = kh.astype(F32) * zts_ref[h]
        gch = gcs_ref[h]
        for b in range(RET_BB):
            s_old = s0_ref[0, b, h]
            cross = _dot(qh, s_old.astype(BF16)) * xi2
            o = o + jnp.where(rows_v == b, cross, 0.0)
            kz_b = jnp.where(rows_k == b, kz, 0.0).astype(BF16)
            s1_ref[0, b, h] = gch * s_old + _dot_tn(kz_b, vh)
        as_ref[:, vc] = _head_norm(o, gn)

    @pl.when(c == pl.num_programs(1) - 1)
    def _():
        sfin_ref[0, 0] = s_ref[...]


def _retention(qk, v, ret_gn, state):
    chunk = PROMPT_CHUNK
    n_chunks = SEQ // chunk
    dmask, xi, zeta, gc = _decay_tables(chunk)
    xi_b = jnp.broadcast_to(xi[:, :, None], (RET_HEADS, chunk, RET_DK))
    zeta_b = jnp.broadcast_to(zeta[:, :, None], (RET_HEADS, chunk, RET_DK))
    dmask_s, xi_s, zeta_s, gc_s = _decay_tables(DEC_SEQ)
    eye = jnp.eye(RET_BB, dtype=F32)
    dm_big = jax.vmap(lambda m: jnp.kron(eye, m))(dmask_s)
    xi_sb = jnp.broadcast_to(jnp.tile(xi_s, (1, RET_BB))[:, :, None], (RET_HEADS, RET_ROWS, RET_DK))
    zeta_sb = jnp.broadcast_to(jnp.tile(zeta_s, (1, RET_BB))[:, :, None], (RET_HEADS, RET_ROWS, RET_DK))

    row = lambda b, c: b * n_chunks + c
    srow = lambda b, c: N_P // RET_ROWS + row(b, c)
    const3 = lambda b, c: (0, 0, 0)
    smem = pl.BlockSpec(memory_space=pltpu.SMEM)
    st_spec = pl.BlockSpec((1, RET_BB, RET_HEADS, RET_DK, RET_DV), lambda b, c: (0, row(b, c), 0, 0, 0))
    return pl.pallas_call(
        _ret_kernel,
        grid=(BATCH, n_chunks),
        in_specs=[
            smem, smem,
            pl.BlockSpec((chunk, RET_QK), lambda b, c: (row(b, c), 0)),
            pl.BlockSpec((chunk, RET_QK), lambda b, c: (row(b, c), 1)),
            pl.BlockSpec((chunk, RET_V), lambda b, c: (row(b, c), 0)),
            pl.BlockSpec((1, RET_V), lambda b, c: (0, 0)),
            pl.BlockSpec((RET_HEADS, chunk, chunk), const3),
            pl.BlockSpec((RET_HEADS, chunk, RET_DK), const3),
            pl.BlockSpec((RET_HEADS, chunk, RET_DK), const3),
            pl.BlockSpec((RET_ROWS, RET_QK), lambda b, c: (srow(b, c), 0)),
            pl.BlockSpec((RET_ROWS, RET_QK), lambda b, c: (srow(b, c), 1)),
            pl.BlockSpec((RET_ROWS, RET_V), lambda b, c: (srow(b, c), 0)),
            pl.BlockSpec((RET_HEADS, RET_ROWS, RET_ROWS), const3),
            pl.BlockSpec((RET_HEADS, RET_ROWS, RET_DK), const3),
            pl.BlockSpec((RET_HEADS, RET_ROWS, RET_DK), const3),
            st_spec,
        ],
        out_specs=[
            pl.BlockSpec((chunk, RET_V), lambda b, c: (row(b, c), 0)),
            pl.BlockSpec((1, 1, RET_HEADS, RET_DK, RET_DV), lambda b, c: (0, b, 0, 0, 0)),
            pl.BlockSpec((RET_ROWS, RET_V), lambda b, c: (row(b, c), 0)),
            st_spec,
        ],
        out_shape=[
            jax.ShapeDtypeStruct((N_P, RET_V), BF16),
            jax.ShapeDtypeStruct((1, BATCH, RET_HEADS, RET_DK, RET_DV), F32),
            jax.ShapeDtypeStruct((N_S, RET_V), BF16),
            jax.ShapeDtypeStruct((1, DEC_BATCH, RET_HEADS, RET_DK, RET_DV), F32),
        ],
        scratch_shapes=[pltpu.VMEM((RET_HEADS, RET_DK, RET_DV), F32)],
        compiler_params=_params(2),
        name="retention",
    )(gc, gc_s, qk, qk, v, ret_gn, dmask, xi_b, zeta_b, qk, qk, v, dm_big, xi_sb, zeta_sb, state)


def _sgate_kernel(u_ref, v_ref, lg_ref, lb_ref, wp_ref, bp_ref, ws_ref, bs_ref, b_ref, sgv_ref, vln_ref):
    i = pl.program_id(0)
    v = v_ref[...].astype(F32)
    mu = jnp.mean(v, axis=-1, keepdims=True)
    d = v - mu
    var = jnp.mean(d * d, axis=-1, keepdims=True)
    vln_ref[...] = (d * lax.rsqrt(var + EPS)) * lg_ref[...] + lb_ref[...]
    gw = SG_WIDTH // SG_GROUPS
    lane_reps = gw // 128

    @pl.when(i < NP_TILES)
    def _():
        r = lax.broadcasted_iota(jnp.int32, (SG_CHUNK, SG_CHUNK), 0)
        c = lax.broadcasted_iota(jnp.int32, (SG_CHUNK, SG_CHUNK), 1)
        for g in range(SG_GROUPS):
            cols = slice(g * gw, (g + 1) * gw)
            w = jnp.where(c <= r, wp_ref[g], 0.0).astype(BF16)
            bias = jnp.concatenate([bp_ref[g]] * lane_reps, axis=1)
            chunks = [slice(ch * SG_CHUNK, (ch + 1) * SG_CHUNK) for ch in range(ROW_TILE // SG_CHUNK)]
            mixed = [_dot(w, vln_ref[rows, cols].astype(BF16)) for rows in chunks]
            for rows, m in zip(chunks, mixed):
                b_ref[rows, cols] = (u_ref[rows, cols].astype(F32) * (m + bias)).astype(BF16)

    @pl.when(i == NP_TILES)
    def _():
        sgv_ref[...] = vln_ref[...].reshape(DEC_BATCH, DEC_SEQ, SG_WIDTH)
        r = lax.broadcasted_iota(jnp.int32, (ROW_TILE, ROW_TILE), 0)
        c = lax.broadcasted_iota(jnp.int32, (ROW_TILE, ROW_TILE), 1)
        keep = ((r >> LOG_DEC_SEQ) == (c >> LOG_DEC_SEQ)) & (c <= r)
        for g in range(SG_GROUPS):
            cols = slice(g * gw, (g + 1) * gw)
            w_rows = jnp.concatenate([ws_ref[g]] * (ROW_TILE // 8), axis=0)
            w_full = jnp.concatenate([w_rows] * (ROW_TILE // 128), axis=1)
            w = jnp.where(keep, w_full, 0.0).astype(BF16)
            b_rows = jnp.concatenate([bs_ref[g]] * (ROW_TILE // 8), axis=0)
            bias = jnp.concatenate([b_rows] * lane_reps, axis=1)
            mixed = _dot(w, vln_ref[:, cols].astype(BF16)) + bias
            b_ref[:, cols] = (u_ref[:, cols].astype(F32) * mixed).astype(BF16)


def _sgate(uv, ln_g, ln_b, sg_ws, sg_bs):
    b_p = jnp.broadcast_to(sg_bs[:, :, None], (SG_GROUPS, SG_CHUNK, 128))
    w_s = jnp.tile(sg_ws[:, :DEC_SEQ, :DEC_SEQ], (1, 8 // DEC_SEQ, 128 // DEC_SEQ))
    b_s = jnp.broadcast_to(jnp.tile(sg_bs[:, :DEC_SEQ], (1, 8 // DEC_SEQ))[:, :, None], (SG_GROUPS, 8, 128))
    const3 = lambda i: (0, 0, 0)
    return pl.pallas_call(
        _sgate_kernel,
        grid=(N_TILES,),
        in_specs=[
            pl.BlockSpec((ROW_TILE, SG_WIDTH), lambda i: (i, 0)),
            pl.BlockSpec((ROW_TILE, SG_WIDTH), lambda i: (i, 1)),
            pl.BlockSpec((1, SG_WIDTH), lambda i: (0, 0)),
            pl.BlockSpec((1, SG_WIDTH), lambda i: (0, 0)),
            pl.BlockSpec((SG_GROUPS, SG_CHUNK, SG_CHUNK), const3),
            pl.BlockSpec((SG_GROUPS, SG_CHUNK, 128), const3),
            pl.BlockSpec((SG_GROUPS, 8, 128), const3),
            pl.BlockSpec((SG_GROUPS, 8, 128), const3),
        ],
        out_specs=[
            pl.BlockSpec((ROW_TILE, SG_WIDTH), lambda i: (i, 0)),
            pl.BlockSpec((DEC_BATCH, DEC_SEQ, SG_WIDTH), lambda i: (0, 0, 0)),
        ],
        out_shape=[
            jax.ShapeDtypeStruct((N_ALL, SG_WIDTH), BF16),
            jax.ShapeDtypeStruct((DEC_BATCH, DEC_SEQ, SG_WIDTH), F32),
        ],
        scratch_shapes=[pltpu.VMEM((ROW_TILE, SG_WIDTH), F32)],
        compiler_params=_params(1),
        name="spatial_gate",
    )(uv, uv, ln_g, ln_b, sg_ws, b_p, w_s, b_s)


def _merge_kernel(ap_ref, as_ref, gs_ref, b_ref, ga_ref, gb_ref, xp_ref, xs_ref, wa_ref, wb_ref, wo_ref,
                  g_ref, wq_ref, x1_ref, q_ref):
    i = pl.program_id(0)

    def run(a_norm, x):
        b = _dot(b_ref[...], wb_ref[...])
        a_in = (gs_ref[...].astype(F32) * a_norm.astype(F32)).astype(BF16)
        a = _dot(a_in, wa_ref[...])
        merged = ga_ref[...].astype(F32) * a + gb_ref[...].astype(F32) * b
        x1 = x + _dot(merged.astype(BF16), wo_ref[...])
        x1_ref[...] = x1
        q_ref[...] = _dot(_rms(x1, g_ref[...]).astype(BF16), wq_ref[...]).astype(BF16)

    @pl.when(i < MERGE_P_TILES)
    def _():
        run(ap_ref[...], xp_ref[...])

    @pl.when(i >= MERGE_P_TILES)
    def _():
        run(as_ref[...], xs_ref[...].reshape(MERGE_TILE, D_MODEL))


def _resident(shape):
    return pl.BlockSpec(shape, lambda i: (0,) * len(shape), pipeline_mode=pl.Buffered(1))


def _merge(a_p, a_s, gs, b_all, gab, xp, xs, wa, wb, wo, g_xa, wq):
    tm = MERGE_TILE
    prompt_tile = lambda i: (jnp.minimum(i, MERGE_P_TILES - 1), 0)
    sample_tile = lambda i: (jnp.maximum(i - MERGE_P_TILES, 0), 0)
    return pl.pallas_call(
        _merge_kernel,
        grid=(N_ALL // tm,),
        in_specs=[
            pl.BlockSpec((tm, RET_V), prompt_tile),
            pl.BlockSpec((tm, RET_V), sample_tile),
            pl.BlockSpec((tm, RET_V), lambda i: (i, 0)),
            pl.BlockSpec((tm, SG_WIDTH), lambda i: (i, 0)),
            pl.BlockSpec((tm, D_MODEL), lambda i: (i, 0)),
            pl.BlockSpec((tm, D_MODEL), lambda i: (i, 1)),
            pl.BlockSpec((tm, D_MODEL), prompt_tile),
            pl.BlockSpec((tm // DEC_SEQ, DEC_SEQ, D_MODEL), lambda i: (jnp.maximum(i - MERGE_P_TILES, 0), 0, 0)),
            _resident((RET_V, D_MODEL)),
            _resident((SG_WIDTH, D_MODEL)),
            _resident((D_MODEL, D_MODEL)),
            pl.BlockSpec((1, D_MODEL), lambda i: (0, 0)),
            _resident((D_MODEL, XA_W)),
        ],
        out_specs=[
            pl.BlockSpec((tm, D_MODEL), lambda i: (i, 0)),
            pl.BlockSpec((tm, XA_W), lambda i: (i, 0)),
        ],
        out_shape=[
            jax.ShapeDtypeStruct((N_ALL, D_MODEL), F32),
            jax.ShapeDtypeStruct((N_ALL, XA_W), BF16),
        ],
        compiler_params=_params(1),
        name="merge_proj",
    )(a_p, a_s, gs, b_all, gab, gab, xp, xs, wa, wb, wo, g_xa, wq)


def _memkv_kernel(m_ref, wk_ref, wv_ref, k_ref, v_ref, kb_ref, vb_ref):
    m = m_ref[...].astype(BF16)
    k = _dot(m, wk_ref[...].astype(BF16))
    v = _dot(m, wv_ref[...].astype(BF16))
    k_ref[0, 0] = k.reshape(MEM_LEN, XA_HEADS, XA_DH)
    v_ref[0, 0] = v.reshape(MEM_LEN, XA_HEADS, XA_DH)
    kb_ref[...] = k.astype(BF16)
    vb_ref[...] = v.astype(BF16)


def _memkv(mem, w_ck, w_cv):
    rows = BATCH * MEM_LEN
    spec = pl.BlockSpec((MEM_LEN, XA_W), lambda b: (b, 0))
    spec5 = pl.BlockSpec((1, 1, MEM_LEN, XA_HEADS, XA_DH), lambda b: (0, b, 0, 0, 0))
    shape5 = jax.ShapeDtypeStruct((1, BATCH, MEM_LEN, XA_HEADS, XA_DH), F32)
    wspec = pl.BlockSpec((D_MODEL, XA_W), lambda b: (0, 0))
    return pl.pallas_call(
        _memkv_kernel,
        grid=(BATCH,),
        in_specs=[pl.BlockSpec((MEM_LEN, D_MODEL), lambda b: (b, 0)), wspec, wspec],
        out_specs=[spec5, spec5, spec, spec],
        out_shape=[shape5, shape5] + [jax.ShapeDtypeStruct((rows, XA_W), BF16)] * 2,
        compiler_params=_params(1),
        name="mem_kv",
    )(mem, w_ck, w_cv)


def _softmax_rows(s):
    m = jnp.max(s, axis=-1, keepdims=True)
    e = jnp.exp(s - m)
    return e / jnp.sum(e, axis=-1, keepdims=True)


def _xattn_prompt_rows(q_ref, k_ref, v_ref, after_scores=lambda: None):
    cols = [slice(h * XA_DH, (h + 1) * XA_DH) for h in range(XA_HEADS)]
    scores = [_dot_nt(q_ref[:, c], k_ref[:, c]) * (XA_DH ** -0.5) for c in cols]
    after_scores()
    probs = [_softmax_rows(s).astype(BF16) for s in scores]
    heads = [_dot(p, v_ref[:, c]).astype(BF16) for p, c in zip(probs, cols)]
    return jnp.concatenate(heads, axis=1)


def _xattn_sample_scores(q_ref, k_ref):
    qf = q_ref[...].astype(F32)
    scores = []
    for b in range(SAMPLE_BB):
        rows = slice(b * DEC_SEQ, (b + 1) * DEC_SEQ)
        qb = jnp.concatenate([qf[rows, h * XA_DH:(h + 1) * XA_DH] for h in range(XA_HEADS)], axis=0)
        scores.append(_dot_nt(k_ref[b].astype(BF16), qb.astype(BF16)) * (XA_DH ** -0.5))
    return scores


def _xattn_sample_finish(scores, v_ref, o_ref):
    n_q = XA_HEADS * DEC_SEQ
    n_kv = MEM_LEN * XA_HEADS
    r = lax.broadcasted_iota(jnp.int32, (n_kv, n_q), 0)
    c = lax.broadcasted_iota(jnp.int32, (n_kv, n_q), 1)
    head_ok = (r & (XA_HEADS - 1)) == (c >> LOG_DEC_SEQ)

    def softmax_cols(s):
        m = jnp.max(s, axis=0, keepdims=True)
        e = jnp.exp(s - m)
        return e / jnp.sum(e, axis=0, keepdims=True)

    probs = [softmax_cols(jnp.where(head_ok, s, -1e30)).astype(BF16) for s in scores]
    for b in range(SAMPLE_BB):
        rows = slice(b * DEC_SEQ, (b + 1) * DEC_SEQ)
        o = _dot_tn(probs[b], v_ref[b].astype(BF16))
        for h in range(XA_HEADS):
            o_ref[h, rows, :] = o[h * DEC_SEQ:(h + 1) * DEC_SEQ, :]


def _route_kernel(qp_ref, mk_ref, mv_ref, qs_ref, ck_ref, cv_ref, x1_ref, wo_ref, g_ref, wr_ref, br_ref,
                  x2_ref, hm_ref, route_ref, cnt_ref, cnti_ref, run_ref, os_ref):
    i = pl.program_id(0)

    @pl.when(i == 0)
    def _():
        run_ref[...] = jnp.zeros_like(run_ref)

    def run(o, between=lambda: None):
        x2 = x1_ref[...] + _dot(o, wo_ref[...])
        x2_ref[...] = x2
        hm = _rms(x2, g_ref[...])
        hm_ref[...] = _to_row_tiles(hm)
        logits = _dot(hm.astype(BF16), wr_ref[...]) + br_ref[...]
        lane = lax.broadcasted_iota(jnp.int32, logits.shape, 1)
        lane_f = lane.astype(F32)
        neg = jnp.float32(-jnp.inf)
        big = jnp.float32(1 << 20)
        is_g = lane < MOE_GROUPS
        gl = jnp.where(is_g, logits, neg)
        gmax = jnp.max(gl, axis=-1, keepdims=True)
        g_sel = jnp.min(jnp.where(gl == gmax, lane_f, big), axis=-1, keepdims=True)
        g_w = 1.0 / jnp.sum(jnp.where(is_g, jnp.exp(logits - gmax), 0.0), axis=-1, keepdims=True)
        between()
        e_lane = lane - MOE_GROUPS
        e_group = (e_lane >> 3).astype(F32)
        in_grp = (e_lane >= 0) & (e_lane < MOE_EXPERTS) & (e_group == g_sel)
        el = jnp.where(in_grp, logits, neg)
        v0 = jnp.max(el, axis=-1, keepdims=True)
        i0 = jnp.min(jnp.where(el == v0, lane_f, big), axis=-1, keepdims=True)
        el1 = jnp.where(lane_f == i0, neg, el)
        v1 = jnp.max(el1, axis=-1, keepdims=True)
        i1 = jnp.min(jnp.where(el1 == v1, lane_f, big), axis=-1, keepdims=True)
        ex = jnp.exp(v1 - v0)
        den = 1.0 + ex
        w0 = (1.0 / den) * g_w
        w1 = (ex / den) * g_w
        a0 = (lane_f == i0).astype(F32)
        a1 = (lane_f == i1).astype(F32)
        a = a0 + a1
        rr = lax.broadcasted_iota(jnp.int32, (ROW_TILE, ROW_TILE), 0)
        cc = lax.broadcasted_iota(jnp.int32, (ROW_TILE, ROW_TILE), 1)
        lower = jnp.where(cc < rr, 1.0, 0.0).astype(BF16)
        before = _dot(lower, a.astype(BF16)) + run_ref[...]
        rank0 = jnp.sum(before * a0, axis=-1, keepdims=True)
        rank1 = jnp.sum(before * a1, axis=-1, keepdims=True)
        run_ref[...] += jnp.sum(a, axis=0, keepdims=True)
        e0 = i0 - MOE_GROUPS
        e1 = i1 - MOE_GROUPS
        route = jnp.where(lane == 0, e0, 0.0)
        route = jnp.where(lane == 1, e1, route)
        route = jnp.where(lane == 2, rank0, route)
        route = jnp.where(lane == 3, rank1, route)
        route = jnp.where(lane == 4, w0, route)
        route = jnp.where(lane == 5, w1, route)
        route_ref[...] = route
        cnt_ref[...] = run_ref[...]
        cnti_ref[...] = run_ref[...].astype(jnp.int32)

    @pl.when(i < NP_TILES)
    def _():
        scores = []
        o_prompt = _xattn_prompt_rows(qp_ref, mk_ref, mv_ref,
                                      after_scores=lambda: scores.extend(_xattn_sample_scores(qs_ref, ck_ref)))
        run(o_prompt, between=lambda: _xattn_sample_finish(scores, cv_ref, os_ref.at[i]))

    @pl.when(i == NP_TILES)
    def _():
        heads = [os_ref[:, h].reshape(N_S, XA_DH) for h in range(XA_HEADS)]
        run(jnp.concatenate(heads, axis=1).astype(BF16))


def _route(qx, mkb, mvb, ck, cv, x1, w_co, g_moe, w_r, b_r):
    prompt_step = lambda i: jnp.minimum(i, NP_TILES - 1)
    kv = pl.BlockSpec((SAMPLE_BB, MEM_LEN * XA_HEADS, XA_DH), lambda i: (prompt_step(i), 0, 0))
    mem_kv = pl.BlockSpec((MEM_LEN, XA_W), lambda i: (prompt_step(i) // (SEQ // ROW_TILE), 0))
    return pl.pallas_call(
        _route_kernel,
        grid=(N_TILES,),
        in_specs=[
            pl.BlockSpec((ROW_TILE, XA_W), lambda i: (prompt_step(i), 0)),
            mem_kv, mem_kv,
            pl.BlockSpec((SAMPLE_ROWS, XA_W), lambda i: (N_P // SAMPLE_ROWS + prompt_step(i), 0)),
            kv, kv,
            pl.BlockSpec((ROW_TILE, D_MODEL), lambda i: (i, 0)),
            _resident((XA_W, D_MODEL)),
            pl.BlockSpec((1, D_MODEL), lambda i: (0, 0)),
            _resident((D_MODEL, ROUTE_LANES)),
            pl.BlockSpec((1, ROUTE_LANES), lambda i: (0, 0)),
        ],
        out_specs=[
            pl.BlockSpec((ROW_TILE, D_MODEL), lambda i: (i, 0)),
            pl.BlockSpec((ROW_TILE, ROW_SUB, ROW_LANE), lambda i: (i, 0, 0)),
            pl.BlockSpec((ROW_TILE, ROUTE_LANES), lambda i: (i, 0)),
            pl.BlockSpec((1, ROUTE_LANES), lambda i: (0, 0)),
            pl.BlockSpec((1, ROUTE_LANES), lambda i: (0, 0)),
        ],
        out_shape=[
            jax.ShapeDtypeStruct((N_ALL, D_MODEL), F32),
            jax.ShapeDtypeStruct((N_ALL, ROW_SUB, ROW_LANE), BF16),
            jax.ShapeDtypeStruct((N_ALL, ROUTE_LANES), F32),
            jax.ShapeDtypeStruct((1, ROUTE_LANES), F32),
            jax.ShapeDtypeStruct((1, ROUTE_LANES), jnp.int32),
        ],
        scratch_shapes=[pltpu.VMEM((1, ROUTE_LANES), F32),
                        pltpu.VMEM((NP_TILES, XA_HEADS, SAMPLE_ROWS, XA_DH), F32)],
        compiler_params=_params(1),
        name="xa_out_route",
    )(qx, mkb, mvb, qx, ck, cv, x1, w_co, g_moe, w_r, b_r)


def _positions_kernel(route_ref, cnt_ref, pos_ref):
    route = route_ref[...]
    lane = lax.broadcasted_iota(jnp.int32, route.shape, 1)
    lane_f = lane.astype(F32)
    tiles = jnp.floor((cnt_ref[...] + (MOE_TM - 1)) * (1.0 / MOE_TM))
    lr = lax.broadcasted_iota(jnp.int32, (ROUTE_LANES, ROUTE_LANES), 0)
    lc = lax.broadcasted_iota(jnp.int32, (ROUTE_LANES, ROUTE_LANES), 1)
    before = jnp.where(lr < lc, 1.0, 0.0).astype(BF16)
    tiles8 = jnp.broadcast_to(tiles, (8, ROUTE_LANES)).astype(BF16)
    start = _dot(tiles8, before)[0:1, :] * MOE_TM

    def col(k):
        return jnp.sum(jnp.where(lane == k, route, 0.0), axis=-1, keepdims=True)

    def first_row(e):
        return jnp.sum(jnp.where(lane_f == e + MOE_GROUPS, start, 0.0), axis=-1, keepdims=True)

    p0 = first_row(col(0)) + col(2)
    p1 = first_row(col(1)) + col(3)
    p = jnp.where(lane == 0, p0, jnp.where(lane == 1, p1, 0.0))
    pos_ref[...] = p.T[0:8, :].astype(jnp.int32)


def _positions(route, counts):
    rows = N_ALL // 4
    return pl.pallas_call(
        _positions_kernel,
        grid=(4,),
        in_specs=[
            pl.BlockSpec((rows, ROUTE_LANES), lambda i: (i, 0)),
            pl.BlockSpec((1, ROUTE_LANES), lambda i: (0, 0)),
        ],
        out_specs=pl.BlockSpec((8, rows), lambda i: (0, i)),
        out_shape=jax.ShapeDtypeStruct((8, N_ALL), jnp.int32),
        compiler_params=_params(1),
        name="positions",
    )(route, counts)


def _dispatch_kernel(pos_ref, zrow_ref, zon_ref, nu_ref, hm_ref, xs_ref, zbuf, sem, zsem, tbuf, tsem):
    i = pl.program_id(0)

    def zero_tile(row):
        return pltpu.make_async_copy(zbuf, xs_ref.at[pl.ds(pl.multiple_of(row, MOE_TM), MOE_TM)], zsem)

    @pl.when(i == 0)
    def _():
        zbuf[...] = jnp.zeros_like(zbuf)
        for e in range(MOE_EXPERTS):
            @pl.when(zon_ref[e] > 0)
            def _():
                zero_tile(zrow_ref[e]).start()

        def start_tail(t, carry):
            zero_tile(t * MOE_TM).start()
            return carry

        def wait_tail(t, carry):
            zero_tile(t * MOE_TM).wait()
            return carry

        lax.fori_loop(nu_ref[0], MOE_NT, start_tail, 0)
        for e in range(MOE_EXPERTS):
            @pl.when(zon_ref[e] > 0)
            def _():
                zero_tile(zrow_ref[e]).wait()
        lax.fori_loop(nu_ref[0], MOE_NT, wait_tail, 0)

    n_steps = pl.num_programs(0)

    def tile_copy(t):
        return pltpu.make_async_copy(hm_ref.at[pl.ds(t * ROW_TILE, ROW_TILE)], tbuf.at[t % 3], tsem.at[t % 3])

    def wait_rows(t):
        for k in range(MOE_TOPK):
            pltpu.make_async_copy(tbuf.at[t % 3], xs_ref.at[pl.ds(0, ROW_TILE)], sem.at[t % 2, k]).wait()

    @pl.when(i == 0)
    def _():
        tile_copy(0).start()

    tile_copy(i).wait()

    @pl.when(i + 1 < n_steps)
    def _():
        tile_copy(i + 1).start()

    slot = i % 3
    par = i % 2
    base = i * ROW_TILE

    def start(r, carry):
        for k in range(MOE_TOPK):
            dst_row = pos_ref[k, base + r]
            pltpu.make_async_copy(tbuf.at[slot, r], xs_ref.at[dst_row], sem.at[par, k]).start(priority=k)
        return carry

    lax.fori_loop(0, ROW_TILE, start, 0, unroll=32)

    @pl.when(i > 0)
    def _():
        wait_rows(i - 1)

    @pl.when(i == n_steps - 1)
    def _():
        wait_rows(i)


def _dispatch(pos_t, zero_row, zero_on, n_used, hmw):
    grid_spec = pltpu.PrefetchScalarGridSpec(
        num_scalar_prefetch=4,
        grid=(N_TILES,),
        in_specs=[pl.BlockSpec(memory_space=pl.ANY)],
        out_specs=pl.BlockSpec(memory_space=pl.ANY),
        scratch_shapes=[pltpu.VMEM((MOE_TM, ROW_SUB, ROW_LANE), BF16),
                        pltpu.SemaphoreType.DMA((2, MOE_TOPK)), pltpu.SemaphoreType.DMA(()),
                        pltpu.VMEM((3, ROW_TILE, ROW_SUB, ROW_LANE), BF16), pltpu.SemaphoreType.DMA((3,))],
    )
    return pl.pallas_call(
        _dispatch_kernel,
        grid_spec=grid_spec,
        out_shape=jax.ShapeDtypeStruct((MOE_ROWS, ROW_SUB, ROW_LANE), BF16),
        compiler_params=_params(1),
        name="dispatch",
    )(pos_t, zero_row, zero_on, n_used, hmw)


def _expert_kernel(nu_ref, first_ref, ord_ref, oe_ref, no_ref, half_ref, x_ref, w1_hbm, w3_hbm, w2_hbm, y_ref,
                   w1b, w3b, w2b, sem):
    i = pl.program_id(0)
    n_used = nu_ref[0]
    n_ord = no_ref[0]

    def weight_copies(k):
        e = oe_ref[k]
        slot = k % W_SLOTS
        return (pltpu.make_async_copy(w1_hbm.at[e], w1b.at[slot], sem.at[0, slot]),
                pltpu.make_async_copy(w3_hbm.at[e], w3b.at[slot], sem.at[1, slot]),
                pltpu.make_async_copy(w2_hbm.at[e], w2b.at[slot], sem.at[2, slot]))

    def start_weights(k):
        for cp in weight_copies(k):
            cp.start(priority=1)

    @pl.when(i == 0)
    def _():
        for k in range(W_SLOTS - 1):
            @pl.when(k < n_ord)
            def _():
                start_weights(k)

    @pl.when(i < n_used)
    def _():
        k = ord_ref[i]

        @pl.when(first_ref[i] > 0)
        def _():
            for cp in weight_copies(k):
                cp.wait()

            @pl.when(k + (W_SLOTS - 1) < n_ord)
            def _():
                start_weights(k + (W_SLOTS - 1))

        slot = k % W_SLOTS

        def swiglu(rows):
            x = _from_row_tiles(x_ref[0:rows])
            h1 = _dot(x, w1b[slot].astype(BF16))
            h3 = _dot(x, w3b[slot].astype(BF16))
            he = (_silu(h1) * h3).astype(BF16)
            y_ref[0:rows] = _to_row_tiles(_dot(he, w2b[slot].astype(BF16)))

        @pl.when(half_ref[i] == 0)
        def _():
            swiglu(MOE_TM)

        @pl.when(half_ref[i] > 0)
        def _():
            swiglu(MOE_TM // 2)
            y_ref[MOE_TM // 2:] = jnp.zeros((MOE_TM // 2, ROW_SUB, ROW_LANE), BF16)

    @pl.when(i >= n_used)
    def _():
        y_ref[...] = jnp.zeros_like(y_ref)


def _experts(sched, xs, w_e1, w_e3, w_e2):
    grid_spec = pltpu.PrefetchScalarGridSpec(
        num_scalar_prefetch=6,
        grid=(MOE_NT,),
        in_specs=[
            pl.BlockSpec((MOE_TM, ROW_SUB, ROW_LANE), lambda i, nu, *_: (jnp.minimum(i, nu[0] - 1), 0, 0)),
            pl.BlockSpec(memory_space=pl.ANY),
            pl.BlockSpec(memory_space=pl.ANY),
            pl.BlockSpec(memory_space=pl.ANY),
        ],
        out_specs=pl.BlockSpec((MOE_TM, ROW_SUB, ROW_LANE), lambda i, *_: (i, 0, 0)),
        scratch_shapes=[
            pltpu.VMEM((W_SLOTS, D_MODEL, MOE_FF), F32),
            pltpu.VMEM((W_SLOTS, D_MODEL, MOE_FF), F32),
            pltpu.VMEM((W_SLOTS, MOE_FF, D_MODEL), F32),
            pltpu.SemaphoreType.DMA((3, W_SLOTS)),
        ],
    )
    return pl.pallas_call(
        _expert_kernel,
        grid_spec=grid_spec,
        out_shape=jax.ShapeDtypeStruct((MOE_ROWS, ROW_SUB, ROW_LANE), BF16),
        compiler_params=_params(1),
        name="experts",
    )(sched["n_used"], sched["tile_first"], sched["tile_ord"], sched["ord_expert"], sched["n_ord"],
      sched["tile_half"], xs, w_e1, w_e3, w_e2)


def _combine_kernel(pos_ref, ys_ref, x2_ref, route_ref, g_ref, yp_ref, ysm_ref, gbuf, sem):
    i = pl.program_id(0)

    def start_gather(tile, slot):
        def body(r, carry):
            for k in range(MOE_TOPK):
                src_row = pos_ref[k, tile * ROW_TILE + r]
                pltpu.make_async_copy(ys_ref.at[src_row], gbuf.at[slot, k, r], sem.at[slot, k]).start(priority=k)
            return carry
        lax.fori_loop(0, ROW_TILE, body, 0, unroll=32)

    @pl.when(i == 0)
    def _():
        start_gather(0, 0)

    slot = i % 2

    @pl.when(i + 1 < pl.num_programs(0))
    def _():
        start_gather(i + 1, 1 - slot)

    for k in range(MOE_TOPK):
        pltpu.make_async_copy(ys_ref.at[pl.ds(0, ROW_TILE)], gbuf.at[slot, k], sem.at[slot, k]).wait()

    route = route_ref[...]
    lane = lax.broadcasted_iota(jnp.int32, route.shape, 1)
    w0 = jnp.sum(jnp.where(lane == 4, route, 0.0), axis=-1, keepdims=True)
    w1 = jnp.sum(jnp.where(lane == 5, route, 0.0), axis=-1, keepdims=True)
    g0 = _from_row_tiles(gbuf[slot, 0]).astype(F32)
    g1 = _from_row_tiles(gbuf[slot, 1]).astype(F32)
    x3 = x2_ref[...] + (g0 * w0 + g1 * w1)
    y = _rms(x3, g_ref[...])

    @pl.when(i < NP_TILES)
    def _():
        yp_ref[...] = y

    @pl.when(i == NP_TILES)
    def _():
        ysm_ref[...] = y.reshape(DEC_BATCH, DEC_SEQ, D_MODEL)


def _combine(pos, ys, x2, route, g_f):
    grid_spec = pltpu.PrefetchScalarGridSpec(
        num_scalar_prefetch=1,
        grid=(N_TILES,),
        in_specs=[
            pl.BlockSpec(memory_space=pl.ANY),
            pl.BlockSpec((ROW_TILE, D_MODEL), lambda i, pos: (i, 0)),
            pl.BlockSpec((ROW_TILE, ROUTE_LANES), lambda i, pos: (i, 0)),
            pl.BlockSpec((1, D_MODEL), lambda i, pos: (0, 0)),
        ],
        out_specs=[
            pl.BlockSpec((ROW_TILE, D_MODEL), lambda i, pos: (jnp.minimum(i, NP_TILES - 1), 0)),
            pl.BlockSpec((DEC_BATCH, DEC_SEQ, D_MODEL), lambda i, pos: (0, 0, 0)),
        ],
        scratch_shapes=[pltpu.VMEM((2, MOE_TOPK, ROW_TILE, ROW_SUB, ROW_LANE), BF16),
                        pltpu.SemaphoreType.DMA((2, MOE_TOPK))],
    )
    return pl.pallas_call(
        _combine_kernel,
        grid_spec=grid_spec,
        out_shape=[
            jax.ShapeDtypeStruct((N_P, D_MODEL), F32),
            jax.ShapeDtypeStruct((DEC_BATCH, DEC_SEQ, D_MODEL), F32),
        ],
        compiler_params=_params(1),
        name="combine_norm",
    )(pos, ys, x2, route, g_f)


_SCHEDULE_FIELDS = ("n_used", "tile_first", "tile_ord", "tile_half", "ord_expert", "n_ord", "zero_row", "zero_on")


def _schedule_kernel(cnt_ref, nu_ref, first_ref, ord_ref, half_ref, oe_ref, no_ref, zrow_ref, zon_ref):
    i32 = jnp.int32

    def clear_tile(t, carry):
        first_ref[t] = i32(0)
        ord_ref[t] = i32(0)
        half_ref[t] = i32(0)
        return carry

    lax.fori_loop(0, MOE_NT, clear_tile, 0)

    def clear_expert(e, carry):
        oe_ref[e] = i32(0)
        return carry

    lax.fori_loop(0, MOE_EXPERTS, clear_expert, 0)

    def expert(e, carry):
        t0, k = carry
        c = cnt_ref[0, MOE_GROUPS + e]
        nt = lax.shift_right_logical(c + (MOE_TM - 1), i32(LOG_MOE_TM))
        used = jnp.where(nt > 0, i32(1), i32(0))
        zon_ref[e] = used
        zrow_ref[e] = jnp.maximum(t0 + nt - 1, 0) * MOE_TM

        @pl.when(nt > 0)
        def _():
            oe_ref[k] = e

        def tile(j, cc):
            t = t0 + j
            first_ref[t] = jnp.where(j == 0, i32(1), i32(0))
            ord_ref[t] = k
            half_ref[t] = jnp.where(c - j * MOE_TM <= MOE_TM // 2, i32(1), i32(0))
            return cc

        lax.fori_loop(0, nt, tile, 0)
        return t0 + nt, k + used

    n_tiles, n_experts = lax.fori_loop(0, MOE_EXPERTS, expert, (i32(0), i32(0)))
    nu_ref[0] = n_tiles
    no_ref[0] = n_experts


def _expert_schedule(counts_i32):
    smem = pl.BlockSpec(memory_space=pltpu.SMEM)
    sizes = (1, MOE_NT, MOE_NT, MOE_NT, MOE_EXPERTS, 1, MOE_EXPERTS, MOE_EXPERTS)
    outs = pl.pallas_call(
        _schedule_kernel,
        in_specs=[smem],
        out_specs=[smem] * len(sizes),
        out_shape=[jax.ShapeDtypeStruct((n,), jnp.int32) for n in sizes],
        name="expert_schedule",
    )(counts_i32)
    return dict(zip(_SCHEDULE_FIELDS, outs))


def kernel(x_prompt, x_sample, mem_prompt, state_ret, cache_mem_k, cache_mem_v, norm_mix, w_in, ret_gn,
           sg_ln_g, sg_ln_b, sg_ws, sg_bs, w_a_out, w_b_out, w_o, norm_xa, w_cq, w_ck, w_cv, w_co, norm_moe,
           w_rg, b_rg, w_re, b_re, w_e1, w_e3, w_e2, norm_f):
    xp = x_prompt.reshape(N_P, D_MODEL)
    xs = x_sample

    h = _norm_rows(xp, xs, norm_mix)
    w = w_in[0]
    blocks = lambda m: m.reshape(MERGE_W_BLOCKS, D_MODEL // MERGE_W_BLOCKS, D_MODEL)
    qk, wa = _inproj("rope", IN_TM, 0, 2 * RET_QK, h, w, _rope_tables(IN_TM), side=blocks(w_a_out[0]))
    v, wb = _inproj("copy", IN_TM, 2 * RET_QK, RET_V, h, w, side=blocks(w_b_out[0]))
    gs, wo = _inproj("silu", IN_TM, 2 * RET_QK + RET_V, RET_V, h, w, side=blocks(w_o[0]))
    uv = _inproj("gelu", IN_TM, 2 * RET_QK + 2 * RET_V, 2 * SG_WIDTH, h, w)
    gab = _inproj("sigmoid", IN_TM, 2 * RET_QK + 2 * RET_V + 2 * SG_WIDTH, 2 * D_MODEL, h, w)

    a_p, ret_p, a_s, ret_s = _retention(qk, v, ret_gn, state_ret)

    b_all, sgv = _sgate(uv, sg_ln_g, sg_ln_b, sg_ws[0], sg_bs[0])

    square = lambda m: m.reshape(D_MODEL, D_MODEL)
    x1, qx = _merge(a_p, a_s, gs, b_all, gab, xp, xs, square(wa), square(wb), square(wo), norm_xa,
                    w_cq[0].astype(BF16))

    mk, mv, mkb, mvb = _memkv(mem_prompt.reshape(BATCH * MEM_LEN, D_MODEL), w_ck[0], w_cv[0])
    ck = cache_mem_k.reshape(DEC_BATCH, MEM_LEN * XA_HEADS, XA_DH)
    cv = cache_mem_v.reshape(DEC_BATCH, MEM_LEN * XA_HEADS, XA_DH)

    pad = ROUTE_LANES - MOE_GROUPS - MOE_EXPERTS
    w_r = jnp.concatenate([w_rg[0], w_re[0], jnp.zeros((D_MODEL, pad), F32)], axis=1)
    b_r = jnp.concatenate([b_rg[0], b_re[0], jnp.zeros((pad,), F32)]).reshape(1, ROUTE_LANES)
    x2, hmw, route, counts, counts_i32 = _route(qx, mkb, mvb, ck, cv, x1, w_co[0].astype(BF16), norm_moe,
                                                w_r.astype(BF16), b_r)

    pos_t = _positions(route, counts)
    sched = _expert_schedule(counts_i32)
    xs_sorted = _dispatch(pos_t, sched["zero_row"], sched["zero_on"], sched["n_used"], hmw)
    ys = _experts(sched, xs_sorted, w_e1[0], w_e3[0], w_e2[0])
    y_p, y_s = _combine(pos_t, ys, x2, route, norm_f.reshape(1, D_MODEL))

    return (y_p.reshape(BATCH, SEQ, D_MODEL), y_s, ret_p, mk, mv, ret_s,
            sgv.reshape(1, DEC_BATCH, DEC_SEQ, SG_WIDTH))
```
